```python
import jax
import jax.numpy as jnp
from jax import lax
import numpy as np

D_MODEL = 1024
BATCH = 4
SEQ = 4096
DEPTH = 2
DEC_BATCH = 32
DEC_SEQ = 64
PAST_LEN = 4096

CHUNK = 64
HEAD_DIM = 64
N_MIXERS = 4
D_MIX = D_MODEL
W_GROUP = D_MIX // N_MIXERS
H_A = W_GROUP // HEAD_DIM
H_B = W_GROUP // HEAD_DIM
H_C = W_GROUP // HEAD_DIM
H_D = W_GROUP // HEAD_DIM
LRU_BLOCK = W_GROUP // H_D
A_BAND_CHUNKS = 8
A_REACH = A_BAND_CHUNKS * CHUNK
REL_CLIP = 128
CONV_W = 4
LRU_C = 8.0
N_MEM = 256
H_X = 4
D_X = H_X * HEAD_DIM
D_FF = ((8 * D_MODEL // 3 + 127) // 128) * 128
ALPHA = (2.0 * DEPTH) ** 0.25
BETA = (8.0 * DEPTH) ** -0.25
LN_EPS = 1e-5
IN_WIDTHS = (W_GROUP,) * 13 + (H_C, H_C)
IN_COLS = sum(IN_WIDTHS)
IN_SPLITS = tuple(int(s) for s in np.cumsum(IN_WIDTHS)[:-1])

kernel_name = 'hybrid_streaming_encoder_step'

F32 = jnp.float32


def layer_norm(x, g, b):
    xf = x.astype(F32)
    mu = jnp.mean(xf, -1, keepdims=True)
    var = jnp.mean(jnp.square(xf - mu), -1, keepdims=True)
    return ((xf - mu) * lax.rsqrt(var + LN_EPS) * g.astype(F32) + b.astype(F32)).astype(x.dtype)


def head_norm(h):
    mu = jnp.mean(h, -1, keepdims=True)
    var = jnp.mean(jnp.square(h - mu), -1, keepdims=True)
    return (h - mu) * lax.rsqrt(var + LN_EPS)


def swiglu(x, w_gate, w_up, w_down):
    return (jax.nn.silu(x @ w_gate) * (x @ w_up)) @ w_down


def rotary(x, pos):
    half = HEAD_DIM // 2
    inv = jnp.exp(-jnp.log(10000.0) * jnp.arange(half, dtype=F32) / half)
    ang = pos.astype(F32)[:, None] * inv[None, :]
    cos = jnp.cos(ang)[None, :, None, :]
    sin = jnp.sin(ang)[None, :, None, :]
    x1, x2 = x[..., :half], x[..., half:]
    return jnp.concatenate([x1 * cos - x2 * sin, x1 * sin + x2 * cos], -1)


def chunk_scan(fn, carry, xs):
    T = xs[0].shape[1]
    L = min(CHUNK, T)
    nc = T // L
    blocks = tuple(jnp.moveaxis(a.reshape((a.shape[0], nc, L) + a.shape[2:]), 1, 0) for a in xs)
    carry, ys = lax.scan(fn, carry, blocks)
    ys = jnp.moveaxis(ys, 0, 1)
    return carry, ys.reshape((ys.shape[0], T) + ys.shape[3:])


def attend(q, k, v, rel, valid, rel_bias):
    s = jnp.einsum('...qhd,...khd->...hqk', q.astype(F32), k.astype(F32)) * HEAD_DIM ** -0.5
    s = s + rel_bias.astype(F32)[:, jnp.clip(rel, -REL_CLIP, REL_CLIP) + REL_CLIP]
    if valid is not None:
        s = jnp.where(valid, s, -jnp.inf)
    p = jax.nn.softmax(s, axis=-1)
    return jnp.einsum('...hqk,...khd->...qhd', p, v.astype(F32))


def band_attention_prompt(q, k, v, rel_bias):
    B, T, H, dh = q.shape
    nc = T // CHUNK
    band = (A_BAND_CHUNKS + 1) * CHUNK
    qc = q.reshape(B, nc, CHUNK, H, dh)
    pad = jnp.zeros((B, A_BAND_CHUNKS * CHUNK, H, dh), k.dtype)
    kp = jnp.concatenate([pad, k], 1).reshape(B, nc + A_BAND_CHUNKS, CHUNK, H, dh)
    vp = jnp.concatenate([pad, v], 1).reshape(B, nc + A_BAND_CHUNKS, CHUNK, H, dh)
    kb = jnp.concatenate([kp[:, o:o + nc] for o in range(A_BAND_CHUNKS + 1)], axis=2)
    vb = jnp.concatenate([vp[:, o:o + nc] for o in range(A_BAND_CHUNKS + 1)], axis=2)
    key_off = jnp.arange(band) - A_BAND_CHUNKS * CHUNK
    rel = jnp.arange(CHUNK)[:, None] - key_off[None, :]
    chunk_id = jnp.arange(nc)[:, None] + jnp.arange(band)[None, :] // CHUNK - A_BAND_CHUNKS
    valid = (chunk_id >= 0)[None, :, None, None, :]
    o = attend(qc, kb, vb, rel, valid, rel_bias)
    return o.reshape(B, T, H * dh)


def band_attention_step(q, k, v, k_cache, v_cache, rel_bias):
    B, S, H, dh = q.shape
    R = k_cache.shape[1]
    kk = jnp.concatenate([k_cache.astype(k.dtype), k], 1)
    vv = jnp.concatenate([v_cache.astype(v.dtype), v], 1)
    rel = jnp.arange(S)[:, None] + R - jnp.arange(R + S)[None, :]
    o = attend(q, kk, vv, rel, None, rel_bias)
    return o.reshape(B, S, H * dh)


def retention_chunk(S0, xs):
    q, k, v = xs
    L = q.shape[1]
    log_g = jnp.log1p(-jnp.exp2(-5.0 - jnp.arange(H_B, dtype=F32)))
    idx = jnp.arange(L, dtype=F32)
    diff = idx[:, None] - idx[None, :]
    decay = jnp.exp(jnp.where((diff >= 0)[None], diff[None] * log_g[:, None, None], -jnp.inf))
    scores = jnp.einsum('blhd,bmhd->bhlm', q, k) * decay
    o = jnp.einsum('bhlm,bmhe->blhe', scores, v)
    q_dec = jnp.exp((idx[:, None] + 1.0) * log_g[None, :])
    o = o + jnp.einsum('blhd,bhde->blhe', q, S0) * q_dec[None, :, :, None]
    k_dec = jnp.exp((L - 1.0 - idx)[:, None] * log_g[None, :])
    S = jnp.exp(L * log_g)[None, :, None, None] * S0 + jnp.einsum('blhd,blhe->bhde', k * k_dec[None, :, :, None], v)
    return S, o


def mlstm_chunk(carry, xs):
    C0, n0, m0 = carry
    q, k, v, ig, lf = xs
    L = q.shape[1]
    b = jnp.cumsum(lf, axis=1).transpose(0, 2, 1)
    it = ig.transpose(0, 2, 1)
    causal = jnp.tril(jnp.ones((L, L), bool))
    log_d = jnp.where(causal, b[..., :, None] - b[..., None, :] + it[..., None, :], -jnp.inf)
    log_w0 = b + m0[..., None]
    m = jnp.maximum(log_w0, jnp.max(log_d, -1))
    d = jnp.exp(log_d - m[..., None])
    w0 = jnp.exp(log_w0 - m)
    qk = jnp.einsum('blhd,bmhd->bhlm', q, k) * d
    num = jnp.einsum('bhlm,bmhe->blhe', qk, v) + jnp.einsum('blhd,bhde->blhe', q, C0) * w0.transpose(0, 2, 1)[..., None]
    den = jnp.sum(qk, -1) + w0 * jnp.einsum('blhd,bhd->bhl', q, n0)
    den = jnp.maximum(jnp.abs(den), jnp.exp(-m))
    h = num / den.transpose(0, 2, 1)[..., None]
    m_last = m[..., -1]
    w_state = jnp.exp(log_w0[..., -1] - m_last)
    w_rows = jnp.exp(log_d[..., -1, :] - m_last[..., None])
    C = w_state[..., None, None] * C0 + jnp.einsum('bhl,blhd,blhe->bhde', w_rows, k, v)
    n = w_state[..., None] * n0 + jnp.einsum('bhl,blhd->bhd', w_rows, k)
    return (C, n, m_last), h


def _lin_combine(left, right):
    return (left[0] * right[0], right[0] * left[1] + right[1])


def rg_lru_block(xd, yd, conv_buf, h0, conv_w, conv_b, wa, ba, wi, bi, lam):
    B, T, W = xd.shape
    xp = jnp.concatenate([conv_buf.astype(xd.dtype), xd], 1)
    xc = conv_b.astype(F32) + sum(xp[:, j:j + T].astype(F32) * conv_w[j].astype(F32) for j in range(CONV_W))
    new_buf = xp[:, T:]
    xb = xc.reshape(B, T, H_D, LRU_BLOCK)
    r = jax.nn.sigmoid(jnp.einsum('bthc,hcd->bthd', xb, wa.astype(F32)).reshape(B, T, W) + ba.astype(F32))
    i = jax.nn.sigmoid(jnp.einsum('bthc,hcd->bthd', xb, wi.astype(F32)).reshape(B, T, W) + bi.astype(F32))
    log_a = -LRU_C * r * jax.nn.softplus(-lam.astype(F32))
    a = jnp.exp(log_a)
    u = jnp.sqrt(-jnp.expm1(2.0 * log_a)) * (i * xc)
    u = u.at[:, 0].add(a[:, 0] * h0)
    _, h = lax.associative_scan(_lin_combine, (a, u), axis=1)
    y = h * jax.nn.gelu(yd.astype(F32))
    return y, h[:, -1], new_buf


def parallel_mixer(x, pos, a_past, ret_s, ml_c, ml_n, ml_m, lru_h, conv_buf, lw):
    B, T, _ = x.shape
    proj = x @ lw['w_in'] + lw['b_in']
    (qa, ka, va, qb, kb, vb, gb, qc, kc, vc, oc, xd, yd, ic, fc) = jnp.split(proj, IN_SPLITS, axis=-1)
    heads = lambda t: t.reshape(B, T, -1, HEAD_DIM)
    qa, ka, va = heads(qa), heads(ka), heads(va)
    if a_past is None:
        ya = band_attention_prompt(qa, ka, va, lw['a_rel_bias'])
        keep = min(A_REACH, T)
        a_k, a_v = ka[:, T - keep:], va[:, T - keep:]
    else:
        ya = band_attention_step(qa, ka, va, a_past[0], a_past[1], lw['a_rel_bias'])
        a_k, a_v = ka, va
    qb_ = rotary(heads(qb).astype(F32), pos)
    kb_ = rotary(heads(kb).astype(F32), pos) * HEAD_DIM ** -0.5
    ret_new, ob = chunk_scan(retention_chunk, ret_s.astype(F32), (qb_, kb_, heads(vb).astype(F32)))
    yb = head_norm(ob).reshape(B, T, W_GROUP) * jax.nn.silu(gb.astype(F32))
    carry = (ml_c.astype(F32), ml_n.astype(F32), ml_m.astype(F32))
    (c_new, n_new, m_new), hc = chunk_scan(
        mlstm_chunk, carry,
        (heads(qc).astype(F32), heads(kc).astype(F32) * HEAD_DIM ** -0.5, heads(vc).astype(F32),
         ic.astype(F32), jax.nn.log_sigmoid(fc.astype(F32))))
    yc = head_norm(hc).reshape(B, T, W_GROUP) * jax.nn.sigmoid(oc.astype(F32))
    yd_, h_new, conv_new = rg_lru_block(xd, yd, conv_buf, lru_h.astype(F32), lw['conv_w'], lw['conv_b'],
                                        lw['lru_wa'], lw['lru_ba'], lw['lru_wi'], lw['lru_bi'], lw['lru_lambda'])
    y = jnp.concatenate([ya, yb, yc, yd_], -1).astype(x.dtype) @ lw['w_out']
    return y, (a_k, a_v, ret_new, c_new, n_new, m_new, h_new, conv_new)


def cross_attention(x, mem_k, mem_v, w_q, w_o):
    B, T, _ = x.shape
    q = (x @ w_q).reshape(B, T, H_X, HEAD_DIM).astype(F32)
    s = jnp.einsum('bthd,bmhd->bhtm', q, mem_k.astype(F32)) * HEAD_DIM ** -0.5
    p = jax.nn.softmax(s, axis=-1)
    o = jnp.einsum('bhtm,bmhd->bthd', p, mem_v.astype(F32)).reshape(B, T, D_X)
    return o.astype(x.dtype) @ w_o


def encoder_layer(x, pos, mem_k, mem_v, a_past, rec, lw):
    g, b = lw['ln_g'], lw['ln_b']
    x = layer_norm(ALPHA * x + 0.5 * swiglu(x, lw['ffn1_gate'], lw['ffn1_up'], lw['ffn1_down']), g[0], b[0])
    mix, state = parallel_mixer(x, pos, a_past, rec[0], rec[1], rec[2], rec[3], rec[4], rec[5], lw)
    x = layer_norm(ALPHA * x + mix, g[1], b[1])
    x = layer_norm(ALPHA * x + cross_attention(x, mem_k, mem_v, lw['x_wq'], lw['x_wo']), g[2], b[2])
    x = layer_norm(ALPHA * x + 0.5 * swiglu(x, lw['ffn2_gate'], lw['ffn2_up'], lw['ffn2_down']), g[3], b[3])
    return x, state


def setup_inputs(seed: int = 0) -> dict:
    key = jax.random.key(seed)
    ks = iter(jax.random.split(key, 48))
    nrm = lambda shape, scale=1.0: scale * jax.random.normal(next(ks), shape, F32)
    a_rows = min(A_REACH, PAST_LEN)
    x_prompt = nrm((BATCH, SEQ, D_MODEL))
    x_sample = nrm((DEC_BATCH, DEC_SEQ, D_MODEL))
    mem_prompt = nrm((BATCH, N_MEM, D_MODEL))
    cache_a_k = nrm((DEPTH, DEC_BATCH, a_rows, H_A, HEAD_DIM))
    cache_a_v = nrm((DEPTH, DEC_BATCH, a_rows, H_A, HEAD_DIM))
    state_ret = nrm((DEPTH, DEC_BATCH, H_B, HEAD_DIM, HEAD_DIM), 0.5)
    state_mlstm_c = nrm((DEPTH, DEC_BATCH, H_C, HEAD_DIM, HEAD_DIM), 0.5)
    state_mlstm_n = nrm((DEPTH, DEC_BATCH, H_C, HEAD_DIM), 0.5)
    state_mlstm_m = nrm((DEPTH, DEC_BATCH, H_C))
    state_lru_h = nrm((DEPTH, DEC_BATCH, W_GROUP), 0.5)
    state_conv = nrm((DEPTH, DEC_BATCH, CONV_W - 1, W_GROUP))
    cache_mem_k = nrm((DEPTH, DEC_BATCH, N_MEM, H_X, HEAD_DIM))
    cache_mem_v = nrm((DEPTH, DEC_BATCH, N_MEM, H_X, HEAD_DIM))
    ln_g = 1.0 + nrm((DEPTH, 4, D_MODEL), 0.02)
    ln_b = nrm((DEPTH, 4, D_MODEL), 0.02)
    ffn1_gate = nrm((DEPTH, D_MODEL, D_FF), D_MODEL ** -0.5)
    ffn1_up = nrm((DEPTH, D_MODEL, D_FF), D_MODEL ** -0.5)
    ffn1_down = nrm((DEPTH, D_FF, D_MODEL), BETA * D_FF ** -0.5)
    ffn2_gate = nrm((DEPTH, D_MODEL, D_FF), D_MODEL ** -0.5)
    ffn2_up = nrm((DEPTH, D_MODEL, D_FF), D_MODEL ** -0.5)
    ffn2_down = nrm((DEPTH, D_FF, D_MODEL), BETA * D_FF ** -0.5)
    w_in = nrm((DEPTH, D_MODEL, IN_COLS), D_MODEL ** -0.5)
    forget_bias = jnp.linspace(3.0, 6.0, H_C, dtype=F32)
    b_in = nrm((DEPTH, IN_COLS), 0.02).at[:, IN_COLS - H_C:].add(forget_bias)
    a_rel_bias = nrm((DEPTH, H_A, 2 * REL_CLIP + 1), 0.1)
    conv_w = nrm((DEPTH, CONV_W, W_GROUP), CONV_W ** -0.5)
    conv_b = nrm((DEPTH, W_GROUP), 0.02)
    lru_wa = nrm((DEPTH, H_D, LRU_BLOCK, LRU_BLOCK), LRU_BLOCK ** -0.5)
    lru_ba = nrm((DEPTH, W_GROUP), 0.02)
    lru_wi = nrm((DEPTH, H_D, LRU_BLOCK, LRU_BLOCK), LRU_BLOCK ** -0.5)
    lru_bi = nrm((DEPTH, W_GROUP), 0.02)
    a0 = jax.random.uniform(next(ks), (DEPTH, W_GROUP), F32, 0.9, 0.999)
    s0 = a0 ** (1.0 / LRU_C)
    lru_lambda = jnp.log(s0) - jnp.log1p(-s0)
    w_out = nrm((DEPTH, D_MIX, D_MODEL), BETA * D_MIX ** -0.5)
    x_wq = nrm((DEPTH, D_MODEL, D_X), D_MODEL ** -0.5)
    x_wk = nrm((DEPTH, D_MODEL, D_X), D_MODEL ** -0.5)
    x_wv = nrm((DEPTH, D_MODEL, D_X), D_MODEL ** -0.5)
    x_wo = nrm((DEPTH, D_X, D_MODEL), BETA * D_X ** -0.5)
    return {'x_prompt': x_prompt, 'x_sample': x_sample, 'mem_prompt': mem_prompt,
            'cache_a_k': cache_a_k, 'cache_a_v': cache_a_v, 'state_ret': state_ret,
            'state_mlstm_c': state_mlstm_c, 'state_mlstm_n': state_mlstm_n, 'state_mlstm_m': state_mlstm_m,
            'state_lru_h': state_lru_h, 'state_conv': state_conv,
            'cache_mem_k': cache_mem_k, 'cache_mem_v': cache_mem_v,
            'ln_g': ln_g, 'ln_b': ln_b,
            'ffn1_gate': ffn1_gate, 'ffn1_up': ffn1_up, 'ffn1_down': ffn1_down,
            'ffn2_gate': ffn2_gate, 'ffn2_up': ffn2_up, 'ffn2_down': ffn2_down,
            'w_in': w_in, 'b_in': b_in, 'a_rel_bias': a_rel_bias,
            'conv_w': conv_w, 'conv_b': conv_b,
            'lru_wa': lru_wa, 'lru_ba': lru_ba, 'lru_wi': lru_wi, 'lru_bi': lru_bi, 'lru_lambda': lru_lambda,
            'w_out': w_out, 'x_wq': x_wq, 'x_wk': x_wk, 'x_wv': x_wv, 'x_wo': x_wo}


def reference(x_prompt, x_sample, mem_prompt, cache_a_k, cache_a_v, state_ret, state_mlstm_c, state_mlstm_n,
              state_mlstm_m, state_lru_h, state_conv, cache_mem_k, cache_mem_v, ln_g, ln_b,
              ffn1_gate, ffn1_up, ffn1_down, ffn2_gate, ffn2_up, ffn2_down, w_in, b_in, a_rel_bias,
              conv_w, conv_b, lru_wa, lru_ba, lru_wi, lru_bi, lru_lambda, w_out, x_wq, x_wk, x_wv, x_wo):
    def layer_weights(l):
        return {'ln_g': ln_g[l], 'ln_b': ln_b[l],
                'ffn1_gate': ffn1_gate[l], 'ffn1_up': ffn1_up[l], 'ffn1_down': ffn1_down[l],
                'ffn2_gate': ffn2_gate[l], 'ffn2_up': ffn2_up[l], 'ffn2_down': ffn2_down[l],
                'w_in': w_in[l], 'b_in': b_in[l], 'a_rel_bias': a_rel_bias[l],
                'conv_w': conv_w[l], 'conv_b': conv_b[l],
                'lru_wa': lru_wa[l], 'lru_ba': lru_ba[l], 'lru_wi': lru_wi[l], 'lru_bi': lru_bi[l],
                'lru_lambda': lru_lambda[l], 'w_out': w_out[l], 'x_wq': x_wq[l], 'x_wo': x_wo[l]}

    bp, tp, _ = x_prompt.shape
    pos_p = jnp.arange(tp)
    h = x_prompt
    prompt_states = []
    for l in range(DEPTH):
        lw = layer_weights(l)
        mk = (mem_prompt @ x_wk[l]).reshape(bp, N_MEM, H_X, HEAD_DIM)
        mv = (mem_prompt @ x_wv[l]).reshape(bp, N_MEM, H_X, HEAD_DIM)
        rec0 = (jnp.zeros((bp, H_B, HEAD_DIM, HEAD_DIM), F32),
                jnp.zeros((bp, H_C, HEAD_DIM, HEAD_DIM), F32),
                jnp.zeros((bp, H_C, HEAD_DIM), F32),
                jnp.zeros((bp, H_C), F32),
                jnp.zeros((bp, W_GROUP), F32),
                jnp.zeros((bp, CONV_W - 1, W_GROUP), x_prompt.dtype))
        h, st = encoder_layer(h, pos_p, mk, mv, None, rec0, lw)
        prompt_states.append(st + (mk, mv))
    y_prompt = h
    (p_a_k, p_a_v, p_ret, p_mc, p_mn, p_mm, p_lru, p_conv, p_mem_k, p_mem_v) = [
        jnp.stack(f) for f in zip(*prompt_states)]

    ts = x_sample.shape[1]
    pos_s = PAST_LEN + jnp.arange(ts)
    h = x_sample
    sample_states = []
    for l in range(DEPTH):
        lw = layer_weights(l)
        rec = (state_ret[l], state_mlstm_c[l], state_mlstm_n[l], state_mlstm_m[l], state_lru_h[l], state_conv[l])
        h, st = encoder_layer(h, pos_s, cache_mem_k[l], cache_mem_v[l], (cache_a_k[l], cache_a_v[l]), rec, lw)
        sample_states.append(st)
    y_sample = h
    (s_a_k, s_a_v, s_ret, s_mc, s_mn, s_mm, s_lru, s_conv) = [jnp.stack(f) for f in zip(*sample_states)]

    return (y_prompt, y_sample,
            p_a_k, p_a_v, p_ret, p_mc, p_mn, p_mm, p_lru, p_conv, p_mem_k, p_mem_v,
            s_a_k, s_a_v, s_ret, s_mc, s_mn, s_mm, s_lru, s_conv)
```

```python
import functools

import jax
import jax.numpy as jnp
from jax import lax
from jax.experimental import pallas as pl
from jax.experimental.pallas import tpu as pltpu

F32 = jnp.float32
BF16 = jnp.bfloat16

CHUNK = 64
HEAD_DIM = 64
N_HEADS = 4
W_GROUP = N_HEADS * HEAD_DIM
A_BAND_CHUNKS = 8
A_REACH = A_BAND_CHUNKS * CHUNK
BAND = A_REACH + CHUNK
REL_CLIP = 128
CONV_W = 4
LRU_C = 8.0
LN_EPS = 1e-5
N_GATES = 2 * N_HEADS
PAST_LEN = 4096
ROLL_W = 1024

V7X_VMEM_LIMIT_BYTES = 56 * 1024 * 1024
TOKEN_TILE = 512


def _cparams(*sem):
    return pltpu.CompilerParams(dimension_semantics=sem, vmem_limit_bytes=V7X_VMEM_LIMIT_BYTES)


def _dot(a, b):
    return jnp.dot(a.astype(BF16), b.astype(BF16), preferred_element_type=F32)


def _dot_nt(a, b):
    return lax.dot_general(a.astype(BF16), b.astype(BF16), (((1,), (1,)), ((), ())), preferred_element_type=F32)


def _dot_tn(a, b):
    return lax.dot_general(a.astype(BF16), b.astype(BF16), (((0,), (0,)), ((), ())), preferred_element_type=F32)


def _layer_norm(x, g, b):
    mu = jnp.mean(x, -1, keepdims=True)
    xc = x - mu
    var = jnp.mean(xc * xc, -1, keepdims=True)
    return xc * lax.rsqrt(var + LN_EPS) * g + b


def _head_norm(h):
    mu = jnp.mean(h, -1, keepdims=True)
    hc = h - mu
    var = jnp.mean(hc * hc, -1, keepdims=True)
    return hc * lax.rsqrt(var + LN_EPS)


def _silu(x):
    return x * jax.nn.sigmoid(x)


def _resident(shape):
    nd = len(shape)
    return pl.BlockSpec(shape, lambda *_: (0,) * nd, pipeline_mode=pl.Buffered(1))


def _ffn_ln_kernel(x_ref, wg_ref, wu_ref, wd_ref, g_ref, b_ref, o_ref, *, alpha):
    x = x_ref[...]
    xb = x.astype(BF16)
    gate = jnp.dot(xb, wg_ref[...], preferred_element_type=F32)
    up = jnp.dot(xb, wu_ref[...], preferred_element_type=F32)
    h = (_silu(gate) * up).astype(BF16)
    y = jnp.dot(h, wd_ref[...], preferred_element_type=F32)
    o_ref[...] = _layer_norm(alpha * x + 0.5 * y, g_ref[...], b_ref[...])


def _ffn_ln(x, wg, wu, wd, g, b, alpha):
    n, d = x.shape
    dff = wg.shape[1]
    tm = TOKEN_TILE
    return pl.pallas_call(
        functools.partial(_ffn_ln_kernel, alpha=alpha),
        out_shape=jax.ShapeDtypeStruct((n, d), F32),
        grid=(n // tm,),
        in_specs=[pl.BlockSpec((tm, d), lambda i: (i, 0)),
                  _resident((d, dff)), _resident((d, dff)), _resident((dff, d)),
                  _resident((1, d)), _resident((1, d))],
        out_specs=pl.BlockSpec((tm, d), lambda i: (i, 0)),
        compiler_params=_cparams("parallel"),
        name="ffn_ln",
    )(x, wg, wu, wd, g, b)


def _in_proj_kernel(x_ref, wa_ref, wr_ref, wgt_ref, ba_ref, br_ref, bg_ref, pa_ref, pr_ref, gt_ref):
    xb = x_ref[...].astype(BF16)
    pa_ref[...] = jnp.dot(xb, wa_ref[...], preferred_element_type=F32) + ba_ref[...]
    pr_ref[...] = jnp.dot(xb, wr_ref[...], preferred_element_type=F32) + br_ref[...]
    gt_ref[...] = _dot_nt(wgt_ref[...], xb) + bg_ref[...]


def _in_proj(x, wa, wr, wgt, ba, br, bg):
    n, d = x.shape
    ca, cr = wa.shape[1], wr.shape[1]
    tm = TOKEN_TILE
    return pl.pallas_call(
        _in_proj_kernel,
        out_shape=(jax.ShapeDtypeStruct((n, ca), F32), jax.ShapeDtypeStruct((n, cr), F32),
                   jax.ShapeDtypeStruct((N_GATES, n), F32)),
        grid=(n // tm,),
        in_specs=[pl.BlockSpec((tm, d), lambda i: (i, 0)),
                  _resident((d, ca)), _resident((d, cr)), _resident((N_GATES, d)),
                  _resident((1, ca)), _resident((1, cr)), _resident((N_GATES, 1))],
        out_specs=(pl.BlockSpec((tm, ca), lambda i: (i, 0)), pl.BlockSpec((tm, cr), lambda i: (i, 0)),
                   pl.BlockSpec((N_GATES, tm), lambda i: (0, i))),
        compiler_params=_cparams("parallel"),
        name="in_proj",
    )(x, wa, wr, wgt, ba, br, bg)


def _band_attn_kernel(q_ref, k_ref, v_ref, pk_ref, pv_ref, g_ref, o_ref, kpad, vpad, bias_sc, *, mask_past):
    b = pl.program_id(0)
    c = pl.program_id(1)

    @pl.when((b == 0) & (c == 0))
    def _():
        for h in range(N_HEADS):
            row = jnp.broadcast_to(g_ref[h], (CHUNK, ROLL_W))
            tile = pltpu.roll(row, ROLL_W - CHUNK, 1, stride=1, stride_axis=0)
            bias_sc[h] = tile[:, :BAND]

    @pl.when(c == 0)
    def _():
        kpad[0:A_REACH, :] = pk_ref[...].astype(BF16)
        vpad[0:A_REACH, :] = pv_ref[...].astype(BF16)
        kpad[A_REACH:, :] = k_ref[...].astype(BF16)
        vpad[A_REACH:, :] = v_ref[...].astype(BF16)

    start = pl.multiple_of(c * CHUNK, CHUNK)
    kb = kpad[pl.ds(start, BAND), :]
    vb = vpad[pl.ds(start, BAND), :]
    q = q_ref[...].astype(BF16)
    if mask_past:
        kk = lax.broadcasted_iota(jnp.int32, (CHUNK, BAND), 1)
        valid = kk + c * CHUNK >= A_REACH
    for h in range(N_HEADS):
        sl = slice(h * HEAD_DIM, (h + 1) * HEAD_DIM)
        s = _dot_nt(q[:, sl], kb[:, sl]) * HEAD_DIM ** -0.5 + bias_sc[h]
        if mask_past:
            s = jnp.where(valid, s, -jnp.inf)
        m = jnp.max(s, axis=1, keepdims=True)
        p = jnp.exp(s - m)
        den = jnp.sum(p, axis=1, keepdims=True)
        o_ref[:, sl] = _dot(p, vb[:, sl]) / den


def _band_attention(pa, past_k, past_v, gtab, mask_past):
    bsz, t, _ = pa.shape
    nc = t // CHUNK
    return pl.pallas_call(
        functools.partial(_band_attn_kernel, mask_past=mask_past),
        out_shape=jax.ShapeDtypeStruct((bsz, t, W_GROUP), F32),
        grid=(bsz, nc),
        in_specs=[pl.BlockSpec((None, CHUNK, W_GROUP), lambda b, c: (b, c, 0)),
                  pl.BlockSpec((None, t, W_GROUP), lambda b, c: (b, 0, 1)),
                  pl.BlockSpec((None, t, W_GROUP), lambda b, c: (b, 0, 2)),
                  pl.BlockSpec((None, A_REACH, W_GROUP), lambda b, c: (b, 0, 0)),
                  pl.BlockSpec((None, A_REACH, W_GROUP), lambda b, c: (b, 0, 0)),
                  pl.BlockSpec((N_HEADS, 1, ROLL_W), lambda b, c: (0, 0, 0))],
        out_specs=pl.BlockSpec((None, CHUNK, W_GROUP), lambda b, c: (b, c, 0)),
        scratch_shapes=[pltpu.VMEM((A_REACH + t, W_GROUP), BF16), pltpu.VMEM((A_REACH + t, W_GROUP), BF16),
                        pltpu.VMEM((N_HEADS, CHUNK, BAND), F32)],
        compiler_params=_cparams("arbitrary", "arbitrary"),
        name="band_attention",
    )(pa, pa, pa, past_k, past_v, gtab)


def _swap_halves(x):
    lane = lax.broadcasted_iota(jnp.int32, x.shape, 1)
    w = x.shape[1]
    half = HEAD_DIM // 2
    return jnp.where(lane % HEAD_DIM < half, pltpu.roll(x, w - half, 1), pltpu.roll(x, half, 1))


def _retention_kernel(q_ref, k_ref, v_ref, g_ref, cos_ref, sin_ref, dec_ref, qdec_ref, kdec_ref, sdec_ref, s0_ref,
                      y_ref, s_out_ref, s_sc):
    c = pl.program_id(1)

    @pl.when(c == 0)
    def _():
        s_sc[...] = s0_ref[...]

    cos = cos_ref[...]
    sin = sin_ref[...]
    q = q_ref[...]
    k = k_ref[...]
    q = q * cos + _swap_halves(q) * sin
    k = (k * cos + _swap_halves(k) * sin) * HEAD_DIM ** -0.5
    v = v_ref[...]
    g = g_ref[...]
    qdec = qdec_ref[...]
    kdec = kdec_ref[...]
    for h in range(N_HEADS):
        sl = slice(h * HEAD_DIM, (h + 1) * HEAD_DIM)
        qh, kh, vh = q[:, sl], k[:, sl], v[:, sl]
        s0 = s_sc[h]
        scores = _dot_nt(qh, kh) * dec_ref[h]
        o = _dot(scores, vh) + _dot(qh, s0) * qdec[:, sl]
        s_sc[h] = sdec_ref[h] * s0 + _dot_tn(kh * kdec[:, sl], vh)
        y_ref[:, sl] = _head_norm(o) * _silu(g[:, sl])

    @pl.when(c == pl.num_programs(1) - 1)
    def _():
        s_out_ref[...] = s_sc[...]


def _retention(pr, cos, sin, dec, qdec, kdec, sdec, s0):
    bsz, t, _ = pr.shape
    nc = t // CHUNK
    col = lambda j: pl.BlockSpec((None, CHUNK, W_GROUP), lambda b, c: (b, c, j))
    tab = pl.BlockSpec((CHUNK, W_GROUP), lambda b, c: (c, 0))
    state = pl.BlockSpec((None, N_HEADS, HEAD_DIM, HEAD_DIM), lambda b, c: (b, 0, 0, 0))
    return pl.pallas_call(
        _retention_kernel,
        out_shape=(jax.ShapeDtypeStruct((bsz, t, W_GROUP), F32),
                   jax.ShapeDtypeStruct((bsz, N_HEADS, HEAD_DIM, HEAD_DIM), F32)),
        grid=(bsz, nc),
        in_specs=[col(0), col(1), col(2), col(3), tab, tab,
                  pl.BlockSpec((N_HEADS, CHUNK, CHUNK), lambda b, c: (0, 0, 0)),
                  pl.BlockSpec((CHUNK, W_GROUP), lambda b, c: (0, 0)),
                  pl.BlockSpec((CHUNK, W_GROUP), lambda b, c: (0, 0)),
                  pl.BlockSpec((N_HEADS, HEAD_DIM, HEAD_DIM), lambda b, c: (0, 0, 0)),
                  state],
        out_specs=(pl.BlockSpec((None, CHUNK, W_GROUP), lambda b, c: (b, c, 0)), state),
        scratch_shapes=[pltpu.VMEM((N_HEADS, HEAD_DIM, HEAD_DIM), F32)],
        compiler_params=_cparams("arbitrary", "arbitrary"),
        name="retention",
    )(pr, pr, pr, pr, cos, sin, dec, qdec, kdec, sdec, s0)


def _row_to_col(row, eye):
    return jnp.sum(jnp.where(eye, jnp.broadcast_to(row, eye.shape), 0.0), axis=1, keepdims=True)


def _mlstm_kernel(q_ref, k_ref, v_ref, og_ref, gate_ref, c0_ref, n0_ref, m0_ref,
                  y_ref, c_out_ref, n_out_ref, m_out_ref, c_sc, n_sc, m_sc):
    c = pl.program_id(1)

    @pl.when(c == 0)
    def _():
        c_sc[...] = c0_ref[...]
        n_sc[...] = n0_ref[...]
        m_sc[...] = m0_ref[...]

    gates = gate_ref[...]
    ig_rows = gates[0:N_HEADS]
    lf = jax.nn.log_sigmoid(gates[N_HEADS:N_GATES])
    lane = lax.broadcasted_iota(jnp.int32, lf.shape, 1)
    b_rows = lf
    shift = 1
    while shift < CHUNK:
        b_rows = b_rows + jnp.where(lane >= shift, pltpu.roll(b_rows, shift, 1), 0.0)
        shift *= 2
    a_rows = ig_rows - b_rows

    li = lax.broadcasted_iota(jnp.int32, (CHUNK, CHUNK), 0)
    lj = lax.broadcasted_iota(jnp.int32, (CHUNK, CHUNK), 1)
    eye = li == lj
    causal = lj <= li
    q = q_ref[...]
    k = k_ref[...] * HEAD_DIM ** -0.5
    v = v_ref[...]
    og = og_ref[...]
    for h in range(N_HEADS):
        sl = slice(h * HEAD_DIM, (h + 1) * HEAD_DIM)
        qh, kh, vh = q[:, sl], k[:, sl], v[:, sl]
        c0 = c_sc[h]
        n0 = n_sc[h:h + 1, 0:HEAD_DIM]
        m0 = m_sc[h:h + 1, 0:1]
        b_row = b_rows[h:h + 1, 0:CHUNK]
        a_row = a_rows[h:h + 1, 0:CHUNK]
        b_col = _row_to_col(b_row, eye)
        log_d = jnp.where(causal, b_col + a_row, -jnp.inf)
        log_w0 = b_col + m0
        m = jnp.maximum(log_w0, jnp.max(log_d, axis=1, keepdims=True))
        d = jnp.exp(log_d - m)
        w0 = jnp.exp(log_w0 - m)
        qk = _dot_nt(qh, kh) * d
        num = _dot(qk, vh) + _dot(qh, c0) * w0
        den = jnp.sum(qk, axis=1, keepdims=True) + w0 * jnp.sum(qh * n0, axis=1, keepdims=True)
        den = jnp.maximum(jnp.abs(den), jnp.exp(-m))
        hh = num / den
        m_last = m[CHUNK - 1:CHUNK, :]
        w_state = jnp.exp(log_w0[CHUNK - 1:CHUNK, :] - m_last)
        w_row = jnp.exp(b_row[:, CHUNK - 1:CHUNK] + a_row - m_last)
        w_col = _row_to_col(w_row, eye)
        kw = kh * w_col
        c_sc[h] = w_state * c0 + _dot_tn(kw, vh)
        n_sc[h:h + 1, 0:HEAD_DIM] = w_state * n0 + jnp.sum(kw, axis=0, keepdims=True)
        m_sc[h:h + 1, :] = jnp.broadcast_to(m_last, (1, m_sc.shape[1]))
        y_ref[:, sl] = _head_norm(hh) * jax.nn.sigmoid(og[:, sl])

    @pl.when(c == pl.num_programs(1) - 1)
    def _():
        c_out_ref[...] = c_sc[...]
        n_out_ref[...] = n_sc[...]
        m_out_ref[...] = m_sc[...]


def _mlstm(pr, gates, c0, n0, m0):
    bsz, t, _ = pr.shape
    nc = t // CHUNK
    col = lambda j: pl.BlockSpec((None, CHUNK, W_GROUP), lambda b, c: (b, c, j))
    c_spec = pl.BlockSpec((None, N_HEADS, HEAD_DIM, HEAD_DIM), lambda b, c: (b, 0, 0, 0))
    v_spec = pl.BlockSpec((None, 8, 128), lambda b, c: (b, 0, 0))
    return pl.pallas_call(
        _mlstm_kernel,
        out_shape=(jax.ShapeDtypeStruct((bsz, t, W_GROUP), F32),
                   jax.ShapeDtypeStruct((bsz, N_HEADS, HEAD_DIM, HEAD_DIM), F32),
                   jax.ShapeDtypeStruct((bsz, 8, 128), F32), jax.ShapeDtypeStruct((bsz, 8, 128), F32)),
        grid=(bsz, nc),
        in_specs=[col(4), col(5), col(6), col(7),
                  pl.BlockSpec((None, None, N_GATES, 128), lambda b, c: (b, c, 0, 0)),
                  c_spec, v_spec, v_spec],
        out_specs=(pl.BlockSpec((None, CHUNK, W_GROUP), lambda b, c: (b, c, 0)), c_spec, v_spec, v_spec),
        scratch_shapes=[pltpu.VMEM((N_HEADS, HEAD_DIM, HEAD_DIM), F32), pltpu.VMEM((8, 128), F32),
                        pltpu.VMEM((8, 128), F32)],
        compiler_params=_cparams("arbitrary", "arbitrary"),
        name="mlstm",
    )(pr, pr, pr, pr, gates, c0, n0, m0)


CONV_PAD = 8


def _shift_rows(x, s, fill):
    row = lax.broadcasted_iota(jnp.int32, x.shape, 0)
    return jnp.where(row >= s, pltpu.roll(x, s, 0), fill)


def _rglru_kernel(x_ref, yg_ref, cw_ref, cb_ref, wa_ref, ba_ref, wi_ref, bi_ref, lam_ref, buf0_ref, h0_ref,
                  y_ref, h_out_ref, buf_out_ref, win, h_sc):
    c = pl.program_id(1)
    tail = CONV_W - 1

    @pl.when(c == 0)
    def _():
        win[CONV_PAD - tail:CONV_PAD, :] = buf0_ref[...]
        h_sc[...] = h0_ref[...]

    x = x_ref[...]
    win[CONV_PAD:CONV_PAD + CHUNK, :] = x
    xc = cb_ref[...]
    for j in range(CONV_W):
        off = CONV_PAD - tail + j
        xc = xc + win[off:off + CHUNK, :] * cw_ref[j:j + 1, :]
    win[CONV_PAD - tail:CONV_PAD, :] = x[CHUNK - tail:, :]

    r = jax.nn.sigmoid(_dot(xc, wa_ref[...]) + ba_ref[...])
    i = jax.nn.sigmoid(_dot(xc, wi_ref[...]) + bi_ref[...])
    neg_lam = -lam_ref[...]
    softplus = jnp.maximum(neg_lam, 0.0) + jnp.log1p(jnp.exp(-jnp.abs(neg_lam)))
    log_a = -LRU_C * r * softplus
    a = jnp.exp(log_a)
    u = jnp.sqrt(-jnp.tanh(log_a) * (jnp.exp(2.0 * log_a) + 1.0)) * (i * xc)

    s = 1
    while s < CHUNK:
        u = a * _shift_rows(u, s, 0.0) + u
        a = a * _shift_rows(a, s, 1.0)
        s *= 2
    hseq = u + a * h_sc[...]
    h_sc[...] = hseq[CHUNK - 1:CHUNK, :]
    y_ref[...] = hseq * jax.nn.gelu(yg_ref[...])

    @pl.when(c == pl.num_programs(1) - 1)
    def _():
        h_out_ref[...] = hseq[CHUNK - 1:CHUNK, :]
        buf_out_ref[...] = x[CHUNK - tail:, :]


def _rglru(pr, cw, cb, wa, ba, wi, bi, lam, buf0, h0):
    bsz, t, _ = pr.shape
    nc = t // CHUNK
    tail = CONV_W - 1
    col = lambda j: pl.BlockSpec((None, CHUNK, W_GROUP), lambda b, c: (b, c, j))
    vec = pl.BlockSpec((1, W_GROUP), lambda b, c: (0, 0))
    mat = pl.BlockSpec((W_GROUP, W_GROUP), lambda b, c: (0, 0))
    buf_spec = pl.BlockSpec((None, tail, W_GROUP), lambda b, c: (b, 0, 0))
    h_spec = pl.BlockSpec((None, 1, W_GROUP), lambda b, c: (b, 0, 0))
    return pl.pallas_call(
        _rglru_kernel,
        out_shape=(jax.ShapeDtypeStruct((bsz, t, W_GROUP), F32),
                   jax.ShapeDtypeStruct((bsz, 1, W_GROUP), F32),
                   jax.ShapeDtypeStruct((bsz, tail, W_GROUP), F32)),
        grid=(bsz, nc),
        in_specs=[col(8), col(9), pl.BlockSpec((CONV_W, W_GROUP), lambda b, c: (0, 0)), vec,
                  mat, vec, mat, vec, vec, buf_spec, h_spec],
        out_specs=(pl.BlockSpec((None, CHUNK, W_GROUP), lambda b, c: (b, c, 0)), h_spec, buf_spec),
        scratch_shapes=[pltpu.VMEM((CONV_PAD + CHUNK, W_GROUP), F32), pltpu.VMEM((1, W_GROUP), F32)],
        compiler_params=_cparams("arbitrary", "arbitrary"),
        name="rglru",
    )(pr, pr, cw, cb, wa, ba, wi, bi, lam, buf0, h0)


def _post_kernel(x_ref, ya_ref, yb_ref, yc_ref, yd_ref, wo_ref, g1_ref, b1_ref, wq_ref, mk_ref, mv_ref, xo_ref,
                 g2_ref, b2_ref, o_ref, att_sc, *, alpha, seqs):
    x = x_ref[...]
    y = jnp.concatenate([ya_ref[...], yb_ref[...], yc_ref[...], yd_ref[...]], axis=1)
    x2 = _layer_norm(alpha * x + _dot(y, wo_ref[...]), g1_ref[...], b1_ref[...])
    q = _dot(x2, wq_ref[...]).astype(BF16)
    rows = x.shape[0] // seqs
    for s in range(seqs):
        rs = slice(s * rows, (s + 1) * rows)
        mk = mk_ref[s].astype(BF16)
        mv = mv_ref[s].astype(BF16)
        for h in range(N_HEADS):
            sl = slice(h * HEAD_DIM, (h + 1) * HEAD_DIM)
            sc = _dot_nt(q[rs, sl], mk[:, sl]) * HEAD_DIM ** -0.5
            m = jnp.max(sc, axis=1, keepdims=True)
            p = jnp.exp(sc - m)
            den = jnp.sum(p, axis=1, keepdims=True)
            att_sc[rs, sl] = _dot(p, mv[:, sl]) / den
    o_ref[...] = _layer_norm(alpha * x2 + _dot(att_sc[...], xo_ref[...]), g2_ref[...], b2_ref[...])


def _post(x, ya, yb, yc, yd, wo, g1, b1, wq, mk, mv, xo, g2, b2, alpha, seq_len):
    n, d = x.shape
    n_mem, dx = mk.shape[1], mk.shape[2]
    tm = TOKEN_TILE
    seqs = max(1, tm // seq_len)
    tiles_per_seq = max(1, seq_len // tm)
    tok = lambda w: pl.BlockSpec((tm, w), lambda i: (i, 0))
    mem = pl.BlockSpec((seqs, n_mem, dx), lambda i: (i // tiles_per_seq, 0, 0))
    return pl.pallas_call(
        functools.partial(_post_kernel, alpha=alpha, seqs=seqs),
        out_shape=jax.ShapeDtypeStruct((n, d), F32),
        grid=(n // tm,),
        in_specs=[tok(d), tok(W_GROUP), tok(W_GROUP), tok(W_GROUP), tok(W_GROUP),
                  _resident(wo.shape), _resident((1, d)), _resident((1, d)), _resident(wq.shape),
                  mem, mem, _resident(xo.shape), _resident((1, d)), _resident((1, d))],
        out_specs=tok(d),
        scratch_shapes=[pltpu.VMEM((tm, dx), F32)],
        compiler_params=_cparams("parallel"),
        name="post",
    )(x, ya, yb, yc, yd, wo, g1, b1, wq, mk, mv, xo, g2, b2)


def _mem_kv_kernel(x_ref, w_ref, o_ref):
    o_ref[...] = _dot(x_ref[...], w_ref[...])


def _mem_kv(mem, w):
    n, d = mem.shape
    tm = min(TOKEN_TILE, n)
    return pl.pallas_call(
        _mem_kv_kernel,
        out_shape=jax.ShapeDtypeStruct((n, w.shape[1]), F32),
        grid=(n // tm,),
        in_specs=[pl.BlockSpec((tm, d), lambda i: (i, 0)), _resident(w.shape)],
        out_specs=pl.BlockSpec((tm, w.shape[1]), lambda i: (i, 0)),
        compiler_params=_cparams("parallel"),
        name="mem_kv",
    )(mem, w)


def _rotary_tables(pos):
    half = HEAD_DIM // 2
    inv = jnp.exp(-jnp.log(10000.0) * jnp.arange(half, dtype=F32) / half)
    ang = pos.astype(F32)[:, None] * inv[None, :]
    cos, sin = jnp.cos(ang), jnp.sin(ang)
    cos_full = jnp.tile(jnp.concatenate([cos, cos], -1), (1, N_HEADS))
    sin_full = jnp.tile(jnp.concatenate([-sin, sin], -1), (1, N_HEADS))
    return cos_full, sin_full


def _retention_tables():
    log_g = jnp.log1p(-jnp.exp2(-5.0 - jnp.arange(N_HEADS, dtype=F32)))
    idx = jnp.arange(CHUNK, dtype=F32)
    diff = idx[:, None] - idx[None, :]
    dec = jnp.exp(jnp.where((diff >= 0)[None], diff[None] * log_g[:, None, None], -jnp.inf))
    q_dec = jnp.exp((idx[:, None] + 1.0) * log_g[None, :])
    k_dec = jnp.exp((CHUNK - 1.0 - idx)[:, None] * log_g[None, :])
    s_dec = jnp.exp(CHUNK * log_g)
    lanes = lambda t: jnp.repeat(t, HEAD_DIM, axis=1)
    return dec, lanes(q_dec), lanes(k_dec), jnp.broadcast_to(s_dec[:, None, None], (N_HEADS, HEAD_DIM, HEAD_DIM))


def _rel_bias_rows(rel_bias):
    n_far = BAND + CHUNK - (CHUNK + REL_CLIP)
    far = jnp.broadcast_to(rel_bias[:, 2 * REL_CLIP:], (N_HEADS, n_far))
    near = rel_bias[:, REL_CLIP - CHUNK + 1:][:, ::-1]
    rows = jnp.concatenate([far, near], axis=1)
    rows = jnp.pad(rows, ((0, 0), (0, ROLL_W - rows.shape[1])))
    return rows[:, None, :]


def _block_diag(w):
    h, c, _ = w.shape
    eye = jnp.eye(h, dtype=w.dtype)
    return (eye[:, None, :, None] * w[:, :, None, :]).reshape(h * c, h * c)


def _layer(x, pos, mem_k, mem_v, past_k, past_v, rec, lw, mask_past, alpha):
    bsz, t, d = x.shape
    n = bsz * t
    nc = t // CHUNK
    ret_s, ml_c, ml_n, ml_m, lru_h, conv_buf = rec
    g, b = lw['ln_g'], lw['ln_b']

    x1 = _ffn_ln(x.reshape(n, d), lw['ffn1_gate'], lw['ffn1_up'], lw['ffn1_down'], g[0:1], b[0:1], alpha)
    pa, pr, gt = _in_proj(x1, lw['w_in_a'], lw['w_in_r'], lw['w_in_gt'], lw['b_in_a'], lw['b_in_r'], lw['b_in_g'])
    pa = pa.reshape(bsz, t, 3 * W_GROUP)
    pr = pr.reshape(bsz, t, 10 * W_GROUP)
    gates = gt.reshape(N_GATES, bsz, nc, CHUNK).transpose(1, 2, 0, 3)
    gates = jnp.pad(gates, ((0, 0), (0, 0), (0, 0), (0, 128 - CHUNK)))

    ya = _band_attention(pa, past_k, past_v, lw['rel_rows'], mask_past)
    cos, sin = _rotary_tables(pos)
    yb, ret_new = _retention(pr, cos, sin, *_retention_tables(), ret_s)
    n0 = jnp.pad(ml_n, ((0, 0), (0, 8 - N_HEADS), (0, 128 - HEAD_DIM)))
    m0 = jnp.pad(jnp.broadcast_to(ml_m[:, :, None], (bsz, N_HEADS, 128)), ((0, 0), (0, 8 - N_HEADS), (0, 0)))
    yc, c_new, n_new, m_new = _mlstm(pr, gates, ml_c, n0, m0)
    yd, h_new, conv_new = _rglru(pr, lw['conv_w'], lw['conv_b'], lw['lru_wa'], lw['lru_ba'], lw['lru_wi'],
                                 lw['lru_bi'], lw['lru_lambda'], conv_buf, lru_h[:, None, :])

    flat = lambda y: y.reshape(n, W_GROUP)
    x3 = _post(x1, flat(ya), flat(yb), flat(yc), flat(yd), lw['w_out'], g[1:2], b[1:2], lw['x_wq'],
               mem_k, mem_v, lw['x_wo'], g[2:3], b[2:3], alpha, t)
    x4 = _ffn_ln(x3, lw['ffn2_gate'], lw['ffn2_up'], lw['ffn2_down'], g[3:4], b[3:4], alpha)

    heads = lambda a: a.reshape(bsz, -1, N_HEADS, HEAD_DIM)
    keep = min(A_REACH, t)
    a_k = heads(pa[:, t - keep:, W_GROUP:2 * W_GROUP])
    a_v = heads(pa[:, t - keep:, 2 * W_GROUP:])
    state = (a_k, a_v, ret_new, c_new, n_new[:, :N_HEADS, :HEAD_DIM], m_new[:, :N_HEADS, 0],
             h_new[:, 0, :], conv_new)
    return x4.reshape(bsz, t, d), state


def kernel(x_prompt, x_sample, mem_prompt, cache_a_k, cache_a_v, state_ret, state_mlstm_c, state_mlstm_n,
           state_mlstm_m, state_lru_h, state_conv, cache_mem_k, cache_mem_v, ln_g, ln_b,
           ffn1_gate, ffn1_up, ffn1_down, ffn2_gate, ffn2_up, ffn2_down, w_in, b_in, a_rel_bias,
           conv_w, conv_b, lru_wa, lru_ba, lru_wi, lru_bi, lru_lambda, w_out, x_wq, x_wk, x_wv, x_wo):
    depth = ln_g.shape[0]
    alpha = (2.0 * depth) ** 0.25
    n_a, n_main = 3 * W_GROUP, 13 * W_GROUP
    row = lambda v: v[None, :]

    def layer_weights(l):
        bf = lambda w: w[l].astype(BF16)
        return {'ln_g': ln_g[l], 'ln_b': ln_b[l],
                'ffn1_gate': bf(ffn1_gate), 'ffn1_up': bf(ffn1_up), 'ffn1_down': bf(ffn1_down),
                'ffn2_gate': bf(ffn2_gate), 'ffn2_up': bf(ffn2_up), 'ffn2_down': bf(ffn2_down),
                'w_in_a': w_in[l][:, :n_a].astype(BF16), 'w_in_r': w_in[l][:, n_a:n_main].astype(BF16),
                'w_in_gt': w_in[l][:, n_main:].T.astype(BF16),
                'b_in_a': row(b_in[l][:n_a]), 'b_in_r': row(b_in[l][n_a:n_main]), 'b_in_g': b_in[l][n_main:, None],
                'rel_rows': _rel_bias_rows(a_rel_bias[l]),
                'conv_w': conv_w[l], 'conv_b': row(conv_b[l]),
                'lru_wa': _block_diag(lru_wa[l]).astype(BF16), 'lru_ba': row(lru_ba[l]),
                'lru_wi': _block_diag(lru_wi[l]).astype(BF16), 'lru_bi': row(lru_bi[l]),
                'lru_lambda': row(lru_lambda[l]),
                'w_out': bf(w_out), 'x_wq': bf(x_wq), 'x_wo': bf(x_wo),
                'x_wkv': jnp.concatenate([x_wk[l], x_wv[l]], axis=1).astype(BF16)}

    weights = [layer_weights(l) for l in range(depth)]
    dx = x_wk.shape[2]

    bp, tp, d = x_prompt.shape
    n_mem = mem_prompt.shape[1]
    h = x_prompt
    prompt_states = []
    zeros_past = jnp.zeros((bp, A_REACH, W_GROUP), F32)
    for l in range(depth):
        lw = weights[l]
        mkv = _mem_kv(mem_prompt.reshape(bp * n_mem, d), lw['x_wkv'])
        mk = mkv[:, :dx].reshape(bp, n_mem, dx)
        mv = mkv[:, dx:].reshape(bp, n_mem, dx)
        rec0 = (jnp.zeros((bp, N_HEADS, HEAD_DIM, HEAD_DIM), F32),
                jnp.zeros((bp, N_HEADS, HEAD_DIM, HEAD_DIM), F32),
                jnp.zeros((bp, N_HEADS, HEAD_DIM), F32),
                jnp.zeros((bp, N_HEADS), F32),
                jnp.zeros((bp, W_GROUP), F32),
                jnp.zeros((bp, CONV_W - 1, W_GROUP), F32))
        h, st = _layer(h, jnp.arange(tp), mk, mv, zeros_past, zeros_past, rec0, lw, True, alpha)
        prompt_states.append(st + (mk.reshape(bp, n_mem, -1, HEAD_DIM), mv.reshape(bp, n_mem, -1, HEAD_DIM)))
    y_prompt = h
    prompt_out = [jnp.stack(f) for f in zip(*prompt_states)]

    bs, ts, _ = x_sample.shape
    h = x_sample
    sample_states = []
    for l in range(depth):
        rec = (state_ret[l], state_mlstm_c[l], state_mlstm_n[l], state_mlstm_m[l], state_lru_h[l], state_conv[l])
        h, st = _layer(h, PAST_LEN + jnp.arange(ts), cache_mem_k[l].reshape(bs, n_mem, dx),
                       cache_mem_v[l].reshape(bs, n_mem, dx),
                       cache_a_k[l].reshape(bs, A_REACH, W_GROUP), cache_a_v[l].reshape(bs, A_REACH, W_GROUP),
                       rec, weights[l], False, alpha)
        sample_states.append(st)
    y_sample = h
    sample_out = [jnp.stack(f) for f in zip(*sample_states)]

    return (y_prompt, y_sample, *prompt_out, *sample_out)
```

```python
import functools

import numpy as np
import jax
import jax.numpy as jnp
from jax import lax
from jax.experimental import pallas as pl
from jax.experimental.pallas import tpu as pltpu

F32 = jnp.float32
BF16 = jnp.bfloat16

CHUNK = 64
HEAD_DIM = 64
N_HEADS = 4
W_GROUP = N_HEADS * HEAD_DIM
SLAB = 2 * HEAD_DIM
N_PAIRS = N_HEADS // 2
A_BAND_CHUNKS = 8
A_REACH = A_BAND_CHUNKS * CHUNK
REL_CLIP = 128
CONV_W = 4
LRU_C = 8.0
LN_EPS = 1e-5
N_GATES = 2 * N_HEADS
PAST_LEN = 4096
ROLL_W = 1024
CONV_PAD = 8

V7X_VMEM_LIMIT_BYTES = 56 * 1024 * 1024
TOKEN_TILE = 512
MIX_BLOCK = 256
MIX_SEQS = 4

PR_QB, PR_QBS, PR_VB, PR_GB, PR_QC, PR_VC, PR_OC, PR_XD, PR_YD = range(9)
N_PR = 9


def _cparams(*sem):
    return pltpu.CompilerParams(dimension_semantics=sem, vmem_limit_bytes=V7X_VMEM_LIMIT_BYTES)


def _dot(a, b):
    return jnp.dot(a.astype(BF16), b.astype(BF16), preferred_element_type=F32)


def _dot_nt(a, b):
    return lax.dot_general(a.astype(BF16), b.astype(BF16), (((1,), (1,)), ((), ())), preferred_element_type=F32)


def _layer_norm(x, g, b):
    mu = jnp.mean(x, -1, keepdims=True)
    xc = x - mu
    var = jnp.mean(xc * xc, -1, keepdims=True)
    return xc * lax.rsqrt(var + LN_EPS) * g + b


def _silu(x):
    return x * jax.nn.sigmoid(x)


def _resident(shape):
    nd = len(shape)
    return pl.BlockSpec(shape, lambda *_: (0,) * nd, pipeline_mode=pl.Buffered(1))


def _ffn_ln_kernel(x_ref, wg_ref, wu_ref, wd_ref, g_ref, b_ref, o_ref, *, alpha):
    x = x_ref[...]
    xb = x.astype(BF16)
    gate = jnp.dot(xb, wg_ref[...], preferred_element_type=F32)
    up = jnp.dot(xb, wu_ref[...], preferred_element_type=F32)
    h = (_silu(gate) * up).astype(BF16)
    y = jnp.dot(h, wd_ref[...], preferred_element_type=F32)
    o_ref[...] = _layer_norm(alpha * x + 0.5 * y, g_ref[...], b_ref[...])


def _ffn_ln(x, wg, wu, wd, g, b, alpha):
    n, d = x.shape
    dff = wg.shape[1]
    tm = TOKEN_TILE
    return pl.pallas_call(
        functools.partial(_ffn_ln_kernel, alpha=alpha),
        out_shape=jax.ShapeDtypeStruct((n, d), F32),
        grid=(n // tm,),
        in_specs=[pl.BlockSpec((tm, d), lambda i: (i, 0)),
                  _resident((d, dff)), _resident((d, dff)), _resident((dff, d)),
                  _resident((1, d)), _resident((1, d))],
        out_specs=pl.BlockSpec((tm, d), lambda i: (i, 0)),
        compiler_params=_cparams("parallel"),
        name="ffn_ln",
    )(x, wg, wu, wd, g, b)


def _in_proj_kernel(x_ref, wa_ref, wr_ref, wt_ref, ba_ref, br_ref, bt_ref,
                    pa_ref, pr_ref, kat_ref, kbt_ref, kct_ref, gt_ref):
    xb = x_ref[...].astype(BF16)
    pa_ref[...] = jnp.dot(xb, wa_ref[...], preferred_element_type=F32) + ba_ref[...]
    pr_ref[...] = jnp.dot(xb, wr_ref[...], preferred_element_type=F32) + br_ref[...]
    tr = _dot_nt(wt_ref[...], xb) + bt_ref[...]
    kat_ref[...] = tr[0:W_GROUP].astype(BF16)
    kbt_ref[...] = tr[W_GROUP:2 * W_GROUP]
    kct_ref[...] = tr[2 * W_GROUP:3 * W_GROUP]
    gt_ref[...] = tr[3 * W_GROUP:3 * W_GROUP + N_GATES]


def _in_proj(x, wa, wr, wt, ba, br, bt):
    n, d = x.shape
    ca, cr, ct = wa.shape[1], wr.shape[1], wt.shape[0]
    tm = TOKEN_TILE
    tok = lambda w: pl.BlockSpec((tm, w), lambda i: (i, 0))
    trn = lambda r: pl.BlockSpec((r, tm), lambda i: (0, i))
    return pl.pallas_call(
        _in_proj_kernel,
        out_shape=(jax.ShapeDtypeStruct((n, ca), F32), jax.ShapeDtypeStruct((n, cr), F32),
                   jax.ShapeDtypeStruct((W_GROUP, n), BF16), jax.ShapeDtypeStruct((W_GROUP, n), F32),
                   jax.ShapeDtypeStruct((W_GROUP, n), F32), jax.ShapeDtypeStruct((N_GATES, n), F32)),
        grid=(n // tm,),
        in_specs=[tok(d), _resident((d, ca)), _resident((d, cr)), _resident((ct, d)),
                  _resident((1, ca)), _resident((1, cr)), _resident((ct, 1))],
        out_specs=(tok(ca), tok(cr), trn(W_GROUP), trn(W_GROUP), trn(W_GROUP), trn(N_GATES)),
        compiler_params=_cparams("parallel"),
        name="in_proj",
    )(x, wa, wr, wt, ba, br, bt)


def _head_rows(xt, h, pos):
    xh = xt[h * HEAD_DIM:(h + 1) * HEAD_DIM]
    z = jnp.zeros_like(xh)
    return jnp.concatenate([xh, z] if pos == 0 else [z, xh], axis=0)


def _pick_half(lane_lo, a, b):
    return b if a is None else jnp.where(lane_lo, a, b)


def _col_bcast(eye_b, rows):
    hi = rows.astype(BF16)
    lo = (rows - hi.astype(F32)).astype(BF16)
    dn = (((1,), (1,)), ((), ()))
    return (lax.dot_general(eye_b, hi, dn, preferred_element_type=F32)
            + lax.dot_general(eye_b, lo, dn, preferred_element_type=F32))


def _rep_heads(rows4, n):
    return jnp.concatenate([jnp.broadcast_to(rows4[h:h + 1], (n, rows4.shape[1])) for h in range(N_HEADS)], axis=0)


def _group_norm(x, gmat):
    mu = _dot(x, gmat)
    xc = x - mu
    var = _dot(xc * xc, gmat)
    return xc * lax.rsqrt(var + LN_EPS)


def _seg_scan(x, seg_pos, seg_len, op, fill):
    s = 1
    while s < seg_len:
        x = op(x, jnp.where(seg_pos >= s, pltpu.roll(x, s, 1), fill))
        s *= 2
    return x


def _shift_rows(x, s, fill):
    row = lax.broadcasted_iota(jnp.int32, x.shape, 0)
    return jnp.where(row >= s, pltpu.roll(x, s, 0), fill)


def _mixer_kernel(pa_ref, pr_ref, kat_ref, kbt_ref, kct_ref, gt_ref, pastk_ref, pastv_ref, rel_ref,
                  cos_ref, sin_ref, cost_ref, sint_ref, dec_ref, qdec_ref, kdect_ref, sdec_ref,
                  ret0_ref, c0_ref, n0_ref, m0_ref,
                  cw_ref, cb_ref, wa_ref, ba_ref, wi_ref, bi_ref, lam_ref, buf0_ref, h0_ref,
                  y_ref, ret_out, c_out, n_out, m_out, h_out, buf_out,
                  kpadt, vpad, bias_sc, spair, ppair, m_sc, win, h_sc,
                  *, blk, seqs, blocked, mask_past):
    L = blk
    band = A_REACH + L
    bb = pl.program_id(0)
    tb = pl.program_id(1)
    last = tb == pl.num_programs(1) - 1
    tail = CONV_W - 1

    li = lax.broadcasted_iota(jnp.int32, (L, L), 0)
    lj = lax.broadcasted_iota(jnp.int32, (L, L), 1)
    causal = lj <= li
    eye_b = jnp.where(li == lj, 1.0, 0.0).astype(BF16)
    lane_lo = lax.broadcasted_iota(jnp.int32, (L, SLAB), 1) < HEAD_DIM
    ones_slab = jnp.ones((L, SLAB), BF16)
    ri = lax.broadcasted_iota(jnp.int32, (SLAB, SLAB), 0) // HEAD_DIM
    rj = lax.broadcasted_iota(jnp.int32, (SLAB, SLAB), 1) // HEAD_DIM
    smask = jnp.where(ri == rj, 1.0, 0.0)
    pmask = jnp.concatenate([smask, smask], axis=1)
    gi = lax.broadcasted_iota(jnp.int32, (W_GROUP, W_GROUP), 0) // HEAD_DIM
    gj = lax.broadcasted_iota(jnp.int32, (W_GROUP, W_GROUP), 1) // HEAD_DIM
    gmat = jnp.where(gi == gj, 1.0 / HEAD_DIM, 0.0).astype(BF16)
    e64i = lax.broadcasted_iota(jnp.int32, (HEAD_DIM, HEAD_DIM), 0)
    e64j = lax.broadcasted_iota(jnp.int32, (HEAD_DIM, HEAD_DIM), 1)
    eye64 = e64i == e64j
    row_lo = lax.broadcasted_iota(jnp.int32, (SLAB, 1), 0) < HEAD_DIM

    @pl.when((bb == 0) & (tb == 0))
    def _():
        qq = lax.broadcasted_iota(jnp.int32, (L, band), 0)
        kk = lax.broadcasted_iota(jnp.int32, (L, band), 1)
        off = kk - (qq // CHUNK) * CHUNK
        in_band = (off >= 0) & (off < A_REACH + CHUNK)
        for h in range(N_HEADS):
            row = jnp.broadcast_to(rel_ref[h], (L, ROLL_W))
            tile = pltpu.roll(row, ROLL_W - L, 1, stride=1, stride_axis=0)
            bias_sc[h] = jnp.where(in_band, tile[:, :band], -jnp.inf)

    g8 = gt_ref[...]
    seg_pos = lax.broadcasted_iota(jnp.int32, g8.shape, 1) % L
    b8 = _seg_scan(jax.nn.log_sigmoid(g8), seg_pos, L, jnp.add, 0.0)
    a8 = g8 - pltpu.roll(b8, N_HEADS, 0)
    cm8 = _seg_scan(a8, seg_pos, L, jnp.maximum, -jnp.inf)
    b8 = pltpu.roll(b8, N_HEADS, 0)

    for s in range(seqs):
        seg = slice(s * L, (s + 1) * L)

        @pl.when(tb == 0)
        def _():
            for p in range(N_PAIRS):
                sblk, cblk = [], []
                for hh in range(2):
                    h = 2 * p + hh
                    z = jnp.zeros((HEAD_DIM, HEAD_DIM), F32)
                    s_h = ret0_ref[s, h]
                    c_h = c0_ref[s, h]
                    n_row = n0_ref[s, h:h + 1, :]
                    n_col = jnp.sum(jnp.where(eye64, jnp.broadcast_to(n_row, eye64.shape), 0.0),
                                    axis=1, keepdims=True)
                    n_rep = jnp.broadcast_to(n_col, (HEAD_DIM, HEAD_DIM))
                    sblk.append(jnp.concatenate([s_h, z] if hh == 0 else [z, s_h], axis=1))
                    cblk.append(jnp.concatenate([c_h, z, n_rep, z] if hh == 0 else [z, c_h, z, n_rep], axis=1))
                spair[s, p] = jnp.concatenate(sblk, axis=0)
                ppair[s, p] = jnp.concatenate(cblk, axis=0)
            m_sc[s, 0:N_HEADS, :] = jnp.broadcast_to(m0_ref[s], (N_HEADS, m_sc.shape[2]))
            win[s, CONV_PAD - tail:CONV_PAD, :] = buf0_ref[s]
            h_sc[s] = h0_ref[s]
            vpad[s, 0:A_REACH, :] = pastv_ref[s]
            if blocked:
                for i in range(A_REACH // L):
                    kpadt[s, i * W_GROUP:(i + 1) * W_GROUP, :] = pastk_ref[s, :, i * L:(i + 1) * L]
            else:
                kpadt[s, :, 0:A_REACH] = pastk_ref[s]

        pa = pa_ref[s]
        col = lambda j: pr_ref[s, :, j * W_GROUP:(j + 1) * W_GROUP]

        if blocked:
            n_past = A_REACH // L
            kpadt[s, pl.ds(pl.multiple_of((tb + n_past) * W_GROUP, W_GROUP), W_GROUP), :] = kat_ref[:, seg]
            kwin = kpadt[s, pl.ds(pl.multiple_of(tb * W_GROUP, W_GROUP), (n_past + 1) * W_GROUP), :]
            kt_tiles = [kwin[i * W_GROUP:(i + 1) * W_GROUP] for i in range(n_past + 1)]
        else:
            kpadt[s, :, A_REACH:band] = kat_ref[:, seg]
            kt_tiles = [kpadt[s]]
        start = pl.multiple_of(tb * L, L)
        vpad[s, pl.ds(A_REACH + start, L), :] = pa[:, 2 * W_GROUP:].astype(BF16)
        vband = vpad[s, pl.ds(start, band), :]
        qa = (pa[:, 0:W_GROUP] * HEAD_DIM ** -0.5).astype(BF16)
        if mask_past:
            valid = lax.broadcasted_iota(jnp.int32, (L, band), 1) + tb * L >= A_REACH
        ya = []
        for p in range(N_PAIRS):
            q_slab = qa[:, p * SLAB:(p + 1) * SLAB]
            o_pair = None
            for hh in range(2):
                h = 2 * p + hh
                parts = [jnp.dot(q_slab, _head_rows(kt, h, hh), preferred_element_type=F32) for kt in kt_tiles]
                sc = (parts[0] if len(parts) == 1 else jnp.concatenate(parts, axis=1)) + bias_sc[h]
                if mask_past:
                    sc = jnp.where(valid, sc, -jnp.inf)
                mx = jnp.max(sc, axis=1, keepdims=True)
                pe = jnp.exp(sc - mx)
                den = jnp.sum(pe, axis=1, keepdims=True)
                res = _dot(pe, vband)[:, p * SLAB:(p + 1) * SLAB] / den
                o_pair = _pick_half(lane_lo, o_pair, res)
            ya.append(o_pair)
        y_ref[s, :, 0:W_GROUP] = jnp.concatenate(ya, axis=1)

        qr = (col(PR_QB) * cos_ref[...] + col(PR_QBS) * sin_ref[...]).astype(BF16)
        kbt = kbt_ref[:, seg]
        half = HEAD_DIM // 2
        kbt_sw = jnp.concatenate([kbt[h * HEAD_DIM + o:h * HEAD_DIM + o + half]
                                  for h in range(N_HEADS) for o in (half, 0)], axis=0)
        krt = (kbt * cost_ref[:, seg] + kbt_sw * sint_ref[:, seg]) * HEAD_DIM ** -0.5
        krt_b = krt.astype(BF16)
        kdt_b = (krt * kdect_ref[...]).astype(BF16)
        vb = col(PR_VB).astype(BF16)
        qdec = qdec_ref[...]
        ob = []
        for p in range(N_PAIRS):
            sl = slice(p * SLAB, (p + 1) * SLAB)
            q_slab = qr[:, sl]
            v_slab = vb[:, sl]
            s0 = spair[s, p]
            o_pair = None
            for hh in range(2):
                h = 2 * p + hh
                scores = jnp.dot(q_slab, _head_rows(krt_b, h, hh), preferred_element_type=F32) * dec_ref[h]
                o_pair = _pick_half(lane_lo, o_pair, _dot(scores, v_slab))
            ob.append(o_pair + _dot(q_slab, s0) * qdec[:, sl])
            spair[s, p] = s0 * sdec_ref[p] + _dot(kdt_b[sl], v_slab) * smask
        ob = jnp.concatenate(ob, axis=1)
        y_ref[s, :, W_GROUP:2 * W_GROUP] = _group_norm(ob, gmat) * _silu(col(PR_GB))

        a4 = a8[0:N_HEADS, seg]
        b4 = b8[0:N_HEADS, seg]
        m0c = m_sc[s, 0:N_HEADS, 0:1]
        big_m = jnp.maximum(m0c, cm8[0:N_HEADS, seg])
        m4 = b4 + big_m
        w0_4 = jnp.exp(m0c - big_m)
        m_last = big_m[:, L - 1:L]
        wrow4 = jnp.exp(a4 - m_last)
        w0_bc = _col_bcast(eye_b, _rep_heads(w0_4, HEAD_DIM))
        floor_bc = _col_bcast(eye_b, _rep_heads(jnp.exp(-m4), HEAD_DIM))
        kct = kct_ref[:, seg] * HEAD_DIM ** -0.5
        kct_b = kct.astype(BF16)
        kw_b = (kct * _rep_heads(wrow4, HEAD_DIM)).astype(BF16)
        qc = col(PR_QC).astype(BF16)
        vc = col(PR_VC).astype(BF16)
        hc = []
        for p in range(N_PAIRS):
            sl = slice(p * SLAB, (p + 1) * SLAB)
            q_slab = qc[:, sl]
            v_aug = jnp.concatenate([vc[:, sl], ones_slab], axis=1)
            p0 = ppair[s, p]
            num_pair = None
            den_pair = None
            for hh in range(2):
                h = 2 * p + hh
                m_bc = _col_bcast(eye_b, jnp.broadcast_to(big_m[h:h + 1], (L, L)))
                d = jnp.exp(jnp.where(causal, a4[h:h + 1] - m_bc, -jnp.inf))
                qk = jnp.dot(q_slab, _head_rows(kct_b, h, hh), preferred_element_type=F32) * d
                res = _dot(qk, v_aug)
                num_pair = _pick_half(lane_lo, num_pair, res[:, 0:SLAB])
                den_pair = _pick_half(lane_lo, den_pair, res[:, SLAB:])
            inter = _dot(q_slab, p0)
            w0 = w0_bc[:, sl]
            num = num_pair + inter[:, 0:SLAB] * w0
            den = den_pair + inter[:, SLAB:] * w0
            den = jnp.maximum(jnp.abs(den), floor_bc[:, sl])
            hc.append(num / den)
            w_state = jnp.where(row_lo, w0_4[2 * p:2 * p + 1, L - 1:L], w0_4[2 * p + 1:2 * p + 2, L - 1:L])
            ppair[s, p] = p0 * w_state + _dot(kw_b[sl], v_aug) * pmask
        hc = jnp.concatenate(hc, axis=1)
        m_sc[s, 0:N_HEADS, :] = jnp.broadcast_to(m4[:, L - 1:L], (N_HEADS, m_sc.shape[2]))
        y_ref[s, :, 2 * W_GROUP:3 * W_GROUP] = _group_norm(hc, gmat) * jax.nn.sigmoid(col(PR_OC))

        xd = col(PR_XD)
        win[s, CONV_PAD:CONV_PAD + L, :] = xd
        xc = cb_ref[...]
        for j in range(CONV_W):
            off = CONV_PAD - tail + j
            xc = xc + win[s, off:off + L, :] * cw_ref[j:j + 1, :]
        win[s, CONV_PAD - tail:CONV_PAD, :] = xd[L - tail:, :]
        r = jax.nn.sigmoid(_dot(xc, wa_ref[...]) + ba_ref[...])
        i = jax.nn.sigmoid(_dot(xc, wi_ref[...]) + bi_ref[...])
        neg_lam = -lam_ref[...]
        softplus = jnp.maximum(neg_lam, 0.0) + jnp.log1p(jnp.exp(-jnp.abs(neg_lam)))
        log_a = -LRU_C * r * softplus
        a = jnp.exp(log_a)
        u = jnp.sqrt(-jnp.tanh(log_a) * (jnp.exp(2.0 * log_a) + 1.0)) * (i * xc)
        st = 1
        while st < L:
            u = a * _shift_rows(u, st, 0.0) + u
            a = a * _shift_rows(a, st, 1.0)
            st *= 2
        hseq = u + a * h_sc[s]
        h_sc[s] = hseq[L - 1:L, :]
        y_ref[s, :, 3 * W_GROUP:] = hseq * jax.nn.gelu(col(PR_YD))

        @pl.when(last)
        def _():
            for p in range(N_PAIRS):
                sp = spair[s, p]
                pp = ppair[s, p]
                for hh in range(2):
                    h = 2 * p + hh
                    rs = slice(hh * HEAD_DIM, (hh + 1) * HEAD_DIM)
                    ret_out[s, h] = sp[rs, rs]
                    c_out[s, h] = pp[rs, rs]
                    n_rep = pp[rs, SLAB + hh * HEAD_DIM:SLAB + (hh + 1) * HEAD_DIM]
                    n_out[s, h:h + 1, :] = jnp.sum(jnp.where(eye64, n_rep, 0.0), axis=0, keepdims=True)
            m_out[s] = m_sc[s, 0:N_HEADS, 0:1]
            h_out[s] = h_sc[s]
            buf_out[s] = xd[L - tail:, :]


def _mixer(pa, pr, kat, kbt, kct, gt, past_kt, past_v, rel_rows, pos, ret0, c0, n0, m0, lw, buf0, h0, mask_past):
    bsz, t, _ = pa.shape
    blocked = t > CHUNK
    blk = MIX_BLOCK if blocked else t
    seqs = 1 if blocked else MIX_SEQS
    nb = t // blk
    assert t % blk == 0 and bsz % seqs == 0 and (A_REACH % blk == 0 or not blocked)
    band = A_REACH + blk
    cos, sin = _rotary_tables(pos)
    cost, sint = jnp.tile(cos.T, (1, seqs)), jnp.tile(sin.T, (1, seqs))
    dec, qdec, kdect, sdec = _retention_tables(blk)
    tail = CONV_W - 1

    tok3 = lambda w: pl.BlockSpec((seqs, blk, w), lambda b, c: (b, c, 0))
    trn = lambda r: pl.BlockSpec((r, seqs * blk), lambda b, c: (0, b * nb + c))
    per_seq = lambda *shape: pl.BlockSpec((seqs,) + shape, lambda b, c: (b,) + (0,) * len(shape))
    const = lambda *shape: pl.BlockSpec(shape, lambda b, c: (0,) * len(shape))
    in_specs = [tok3(3 * W_GROUP), tok3(N_PR * W_GROUP), trn(W_GROUP), trn(W_GROUP), trn(W_GROUP), trn(N_GATES),
                per_seq(W_GROUP, A_REACH), per_seq(A_REACH, W_GROUP), const(N_HEADS, 1, ROLL_W),
                pl.BlockSpec((blk, W_GROUP), lambda b, c: (c, 0)), pl.BlockSpec((blk, W_GROUP), lambda b, c: (c, 0)),
                pl.BlockSpec((W_GROUP, seqs * blk), lambda b, c: (0, c)),
                pl.BlockSpec((W_GROUP, seqs * blk), lambda b, c: (0, c)),
                const(N_HEADS, blk, blk), const(blk, W_GROUP), const(W_GROUP, blk), const(N_PAIRS, SLAB, SLAB),
                per_seq(N_HEADS, HEAD_DIM, HEAD_DIM), per_seq(N_HEADS, HEAD_DIM, HEAD_DIM),
                per_seq(N_HEADS, HEAD_DIM), per_seq(N_HEADS, 1),
                const(CONV_W, W_GROUP), const(1, W_GROUP), const(W_GROUP, W_GROUP), const(1, W_GROUP),
                const(W_GROUP, W_GROUP), const(1, W_GROUP), const(1, W_GROUP),
                per_seq(tail, W_GROUP), per_seq(1, W_GROUP)]
    out_shape = (jax.ShapeDtypeStruct((bsz, t, 4 * W_GROUP), F32),
                 jax.ShapeDtypeStruct((bsz, N_HEADS, HEAD_DIM, HEAD_DIM), F32),
                 jax.ShapeDtypeStruct((bsz, N_HEADS, HEAD_DIM, HEAD_DIM), F32),
                 jax.ShapeDtypeStruct((bsz, N_HEADS, HEAD_DIM), F32),
                 jax.ShapeDtypeStruct((bsz, N_HEADS, 1), F32),
                 jax.ShapeDtypeStruct((bsz, 1, W_GROUP), F32),
                 jax.ShapeDtypeStruct((bsz, tail, W_GROUP), F32))
    out_specs = (tok3(4 * W_GROUP), per_seq(N_HEADS, HEAD_DIM, HEAD_DIM), per_seq(N_HEADS, HEAD_DIM, HEAD_DIM),
                 per_seq(N_HEADS, HEAD_DIM), per_seq(N_HEADS, 1), per_seq(1, W_GROUP), per_seq(tail, W_GROUP))
    kpadt_shape = (seqs, (nb + A_REACH // blk) * W_GROUP, blk) if blocked else (seqs, W_GROUP, band)
    scratch = [pltpu.VMEM(kpadt_shape, BF16), pltpu.VMEM((seqs, A_REACH + t, W_GROUP), BF16),
               pltpu.VMEM((N_HEADS, blk, band), F32),
               pltpu.VMEM((seqs, N_PAIRS, SLAB, SLAB), F32), pltpu.VMEM((seqs, N_PAIRS, SLAB, 2 * SLAB), F32),
               pltpu.VMEM((seqs, 8, 128), F32), pltpu.VMEM((seqs, CONV_PAD + blk, W_GROUP), F32),
               pltpu.VMEM((seqs, 1, W_GROUP), F32)]
    return pl.pallas_call(
        functools.partial(_mixer_kernel, blk=blk, seqs=seqs, blocked=blocked, mask_past=mask_past),
        out_shape=out_shape,
        grid=(bsz // seqs, nb),
        in_specs=in_specs,
        out_specs=out_specs,
        scratch_shapes=scratch,
        compiler_params=_cparams("arbitrary", "arbitrary"),
        name="mixer",
    )(pa, pr, kat, kbt, kct, gt, past_kt, past_v, rel_rows, cos, sin, cost, sint, dec, qdec, kdect, sdec,
      ret0, c0, n0, m0, lw['conv_w'], lw['conv_b'], lw['lru_wa'], lw['lru_ba'], lw['lru_wi'], lw['lru_bi'],
      lw['lru_lambda'], buf0, h0)


def _post_kernel(x_ref, y_ref, wo_ref, g1_ref, b1_ref, wq_ref, mk_ref, mv_ref, xo_ref,
                 g2_ref, b2_ref, o_ref, att_sc, *, alpha, seqs):
    x = x_ref[...]
    x2 = _layer_norm(alpha * x + _dot(y_ref[...], wo_ref[...]), g1_ref[...], b1_ref[...])
    q = _dot(x2, wq_ref[...]).astype(BF16)
    rows = x.shape[0] // seqs
    for s in range(seqs):
        rs = slice(s * rows, (s + 1) * rows)
        mk = mk_ref[s].astype(BF16)
        mv = mv_ref[s].astype(BF16)
        for h in range(N_HEADS):
            sl = slice(h * HEAD_DIM, (h + 1) * HEAD_DIM)
            sc = _dot_nt(q[rs, sl], mk[:, sl]) * HEAD_DIM ** -0.5
            m = jnp.max(sc, axis=1, keepdims=True)
            p = jnp.exp(sc - m)
            den = jnp.sum(p, axis=1, keepdims=True)
            att_sc[rs, sl] = _dot(p, mv[:, sl]) / den
    o_ref[...] = _layer_norm(alpha * x2 + _dot(att_sc[...], xo_ref[...]), g2_ref[...], b2_ref[...])


def _post(x, y, wo, g1, b1, wq, mk, mv, xo, g2, b2, alpha, seq_len):
    n, d = x.shape
    n_mem, dx = mk.shape[1], mk.shape[2]
    tm = TOKEN_TILE
    seqs = max(1, tm // seq_len)
    tiles_per_seq = max(1, seq_len // tm)
    tok = lambda w: pl.BlockSpec((tm, w), lambda i: (i, 0))
    mem = pl.BlockSpec((seqs, n_mem, dx), lambda i: (i // tiles_per_seq, 0, 0))
    return pl.pallas_call(
        functools.partial(_post_kernel, alpha=alpha, seqs=seqs),
        out_shape=jax.ShapeDtypeStruct((n, d), F32),
        grid=(n // tm,),
        in_specs=[tok(d), tok(y.shape[1]),
                  _resident(wo.shape), _resident((1, d)), _resident((1, d)), _resident(wq.shape),
                  mem, mem, _resident(xo.shape), _resident((1, d)), _resident((1, d))],
        out_specs=tok(d),
        scratch_shapes=[pltpu.VMEM((tm, dx), F32)],
        compiler_params=_cparams("parallel"),
        name="post",
    )(x, y, wo, g1, b1, wq, mk, mv, xo, g2, b2)


def _mem_kv_kernel(x_ref, w_ref, o_ref):
    o_ref[...] = _dot(x_ref[...], w_ref[...])


def _mem_kv(mem, w):
    n, d = mem.shape
    tm = min(TOKEN_TILE, n)
    return pl.pallas_call(
        _mem_kv_kernel,
        out_shape=jax.ShapeDtypeStruct((n, w.shape[1]), F32),
        grid=(n // tm,),
        in_specs=[pl.BlockSpec((tm, d), lambda i: (i, 0)), _resident(w.shape)],
        out_specs=pl.BlockSpec((tm, w.shape[1]), lambda i: (i, 0)),
        compiler_params=_cparams("parallel"),
        name="mem_kv",
    )(mem, w)


def _rotary_tables(pos):
    half = HEAD_DIM // 2
    inv = jnp.exp(-jnp.log(10000.0) * jnp.arange(half, dtype=F32) / half)
    ang = pos.astype(F32)[:, None] * inv[None, :]
    cos, sin = jnp.cos(ang), jnp.sin(ang)
    cos_full = jnp.tile(jnp.concatenate([cos, cos], -1), (1, N_HEADS))
    sin_full = jnp.tile(jnp.concatenate([-sin, sin], -1), (1, N_HEADS))
    return cos_full, sin_full


def _retention_tables(blk):
    log_g = jnp.log1p(-jnp.exp2(-5.0 - jnp.arange(N_HEADS, dtype=F32)))
    idx = jnp.arange(blk, dtype=F32)
    diff = idx[:, None] - idx[None, :]
    dec = jnp.exp(jnp.where((diff >= 0)[None], diff[None] * log_g[:, None, None], -jnp.inf))
    q_dec = jnp.exp((idx[:, None] + 1.0) * log_g[None, :])
    k_dec = jnp.exp((blk - 1.0 - idx)[:, None] * log_g[None, :])
    s_dec = jnp.exp(blk * log_g)
    lanes = lambda tbl: jnp.repeat(tbl, HEAD_DIM, axis=1)
    pair = lambda p: jnp.kron(jnp.diag(s_dec[2 * p:2 * p + 2]), jnp.ones((HEAD_DIM, HEAD_DIM), F32))
    return dec, lanes(q_dec), lanes(k_dec).T, jnp.stack([pair(p) for p in range(N_PAIRS)])


def _rel_bias_rows(rel_bias, blk):
    idx = np.clip(A_REACH + blk - np.arange(ROLL_W), -REL_CLIP, REL_CLIP) + REL_CLIP
    return rel_bias[:, idx][:, None, :]


def _block_diag(w):
    h, c, _ = w.shape
    eye = jnp.eye(h, dtype=w.dtype)
    return (eye[:, None, :, None] * w[:, :, None, :]).reshape(h * c, h * c)


def _swap_perm():
    j = np.arange(W_GROUP)
    return (j // HEAD_DIM) * HEAD_DIM + (j % HEAD_DIM + HEAD_DIM // 2) % HEAD_DIM


def _layer(x, pos, mem_k, mem_v, past_k, past_v, rec, lw, mask_past, alpha):
    bsz, t, d = x.shape
    n = bsz * t
    ret_s, ml_c, ml_n, ml_m, lru_h, conv_buf = rec
    g, b = lw['ln_g'], lw['ln_b']

    x1 = _ffn_ln(x.reshape(n, d), lw['ffn1_gate'], lw['ffn1_up'], lw['ffn1_down'], g[0:1], b[0:1], alpha)
    pa, pr, kat, kbt, kct, gt = _in_proj(x1, lw['w_in_a'], lw['w_in_r'], lw['w_in_t'],
                                         lw['b_in_a'], lw['b_in_r'], lw['b_in_t'])
    pa = pa.reshape(bsz, t, 3 * W_GROUP)
    pr = pr.reshape(bsz, t, N_PR * W_GROUP)
    blk = MIX_BLOCK if t > CHUNK else t
    past_kt = past_k.transpose(0, 2, 1).astype(BF16)
    y, ret_new, c_new, n_new, m_new, h_new, conv_new = _mixer(
        pa, pr, kat, kbt, kct, gt, past_kt, past_v.astype(BF16), _rel_bias_rows(lw['rel_bias'], blk), pos,
        ret_s, ml_c, ml_n, ml_m[:, :, None], lw, conv_buf, lru_h[:, None, :], mask_past)

    x3 = _post(x1, y.reshape(n, 4 * W_GROUP), lw['w_out'], g[1:2], b[1:2], lw['x_wq'],
               mem_k, mem_v, lw['x_wo'], g[2:3], b[2:3], alpha, t)
    x4 = _ffn_ln(x3, lw['ffn2_gate'], lw['ffn2_up'], lw['ffn2_down'], g[3:4], b[3:4], alpha)

    heads = lambda a: a.reshape(bsz, -1, N_HEADS, HEAD_DIM)
    keep = min(A_REACH, t)
    a_k = heads(pa[:, t - keep:, W_GROUP:2 * W_GROUP])
    a_v = heads(pa[:, t - keep:, 2 * W_GROUP:])
    state = (a_k, a_v, ret_new, c_new, n_new, m_new[:, :, 0], h_new[:, 0, :], conv_new)
    return x4.reshape(bsz, t, d), state


def kernel(x_prompt, x_sample, mem_prompt, cache_a_k, cache_a_v, state_ret, state_mlstm_c, state_mlstm_n,
           state_mlstm_m, state_lru_h, state_conv, cache_mem_k, cache_mem_v, ln_g, ln_b,
           ffn1_gate, ffn1_up, ffn1_down, ffn2_gate, ffn2_up, ffn2_down, w_in, b_in, a_rel_bias,
           conv_w, conv_b, lru_wa, lru_ba, lru_wi, lru_bi, lru_lambda, w_out, x_wq, x_wk, x_wv, x_wo):
    depth = ln_g.shape[0]
    alpha = (2.0 * depth) ** 0.25
    row = lambda v: v[None, :]
    grp = lambda a, j: a[..., j * W_GROUP:(j + 1) * W_GROUP]
    perm = _swap_perm()
    QA, KA, VA, QB, KB, VB, GB, QC, KC, VC, OC, XD, YD = range(13)

    def layer_weights(l):
        bf = lambda w: w[l].astype(BF16)
        w, bias = w_in[l], b_in[l]
        nat = lambda a: jnp.concatenate(
            [grp(a, QB), grp(a, QB)[..., perm], grp(a, VB), grp(a, GB), grp(a, QC), grp(a, VC), grp(a, OC),
             grp(a, XD), grp(a, YD)], axis=-1)
        trn = lambda a: jnp.concatenate([grp(a, KA), grp(a, KB), grp(a, KC), a[..., 13 * W_GROUP:]], axis=-1)
        return {'ln_g': ln_g[l], 'ln_b': ln_b[l],
                'ffn1_gate': bf(ffn1_gate), 'ffn1_up': bf(ffn1_up), 'ffn1_down': bf(ffn1_down),
                'ffn2_gate': bf(ffn2_gate), 'ffn2_up': bf(ffn2_up), 'ffn2_down': bf(ffn2_down),
                'w_in_a': w[:, :3 * W_GROUP].astype(BF16), 'w_in_r': nat(w).astype(BF16),
                'w_in_t': trn(w).T.astype(BF16),
                'b_in_a': row(bias[:3 * W_GROUP]), 'b_in_r': row(nat(bias)), 'b_in_t': trn(bias)[:, None],
                'rel_bias': a_rel_bias[l],
                'conv_w': conv_w[l], 'conv_b': row(conv_b[l]),
                'lru_wa': _block_diag(lru_wa[l]).astype(BF16), 'lru_ba': row(lru_ba[l]),
                'lru_wi': _block_diag(lru_wi[l]).astype(BF16), 'lru_bi': row(lru_bi[l]),
                'lru_lambda': row(lru_lambda[l]),
                'w_out': bf(w_out), 'x_wq': bf(x_wq), 'x_wo': bf(x_wo),
                'x_wkv': jnp.concatenate([x_wk[l], x_wv[l]], axis=1).astype(BF16)}

    weights = [layer_weights(l) for l in range(depth)]
    dx = x_wk.shape[2]

    bp, tp, d = x_prompt.shape
    n_mem = mem_prompt.shape[1]
    h = x_prompt
    prompt_states = []
    zeros_past = jnp.zeros((bp, A_REACH, W_GROUP), F32)
    for l in range(depth):
        lw = weights[l]
        mkv = _mem_kv(mem_prompt.reshape(bp * n_mem, d), lw['x_wkv'])
        mk = mkv[:, :dx].reshape(bp, n_mem, dx)
        mv = mkv[:, dx:].reshape(bp, n_mem, dx)
        rec0 = (jnp.zeros((bp, N_HEADS, HEAD_DIM, HEAD_DIM), F32),
                jnp.zeros((bp, N_HEADS, HEAD_DIM, HEAD_DIM), F32),
                jnp.zeros((bp, N_HEADS, HEAD_DIM), F32),
                jnp.zeros((bp, N_HEADS), F32),
                jnp.zeros((bp, W_GROUP), F32),
                jnp.zeros((bp, CONV_W - 1, W_GROUP), F32))
        h, st = _layer(h, jnp.arange(tp), mk, mv, zeros_past, zeros_past, rec0, lw, True, alpha)
        prompt_states.append(st + (mk.reshape(bp, n_mem, -1, HEAD_DIM), mv.reshape(bp, n_mem, -1, HEAD_DIM)))
    y_prompt = h
    prompt_out = [jnp.stack(f) for f in zip(*prompt_states)]

    bs, ts, _ = x_sample.shape
    h = x_sample
    sample_states = []
    for l in range(depth):
        rec = (state_ret[l], state_mlstm_c[l], state_mlstm_n[l], state_mlstm_m[l], state_lru_h[l], state_conv[l])
        h, st = _layer(h, PAST_LEN + jnp.arange(ts), cache_mem_k[l].reshape(bs, n_mem, dx),
                       cache_mem_v[l].reshape(bs, n_mem, dx),
                       cache_a_k[l].reshape(bs, A_REACH, W_GROUP), cache_a_v[l].reshape(bs, A_REACH, W_GROUP),
                       rec, weights[l], False, alpha)
        sample_states.append(st)
    y_sample = h
    sample_out = [jnp.stack(f) for f in zip(*sample_states)]

    return (y_prompt, y_sample, *prompt_out, *sample_out)
```

```python
import functools

import numpy as np
import jax
import jax.numpy as jnp
from jax import lax
from jax.experimental import pallas as pl
from jax.experimental.pallas import tpu as pltpu

F32 = jnp.float32
BF16 = jnp.bfloat16

CHUNK = 64
HEAD_DIM = 64
N_HEADS = 4
W_GROUP = N_HEADS * HEAD_DIM
SLAB = 2 * HEAD_DIM
N_PAIRS = N_HEADS // 2
A_BAND_CHUNKS = 8
A_REACH = A_BAND_CHUNKS * CHUNK
REL_CLIP = 128
CONV_W = 4
LRU_C = 8.0
LN_EPS = 1e-5
LOG2E = 1.4426950408889634
N_GATES = 2 * N_HEADS
PAST_LEN = 4096
ROLL_W = 1024
CONV_PAD = 8

V7X_VMEM_LIMIT_BYTES = 56 * 1024 * 1024
TOKEN_TILE = 512
MIX_BLOCK = 256
MIX_SEQS = 4

PR_QB, PR_QBS, PR_VB, PR_GB, PR_QC, PR_VC, PR_OC, PR_XD, PR_YD = range(9)
N_PR = 9


def _cparams(*sem):
    return pltpu.CompilerParams(dimension_semantics=sem, vmem_limit_bytes=V7X_VMEM_LIMIT_BYTES)


def _dot(a, b):
    return jnp.dot(a.astype(BF16), b.astype(BF16), preferred_element_type=F32)


def _dot_nt(a, b):
    return lax.dot_general(a.astype(BF16), b.astype(BF16), (((1,), (1,)), ((), ())), preferred_element_type=F32)


def _layer_norm(x, g, b):
    mu = jnp.mean(x, -1, keepdims=True)
    xc = x - mu
    var = jnp.mean(xc * xc, -1, keepdims=True)
    return xc * lax.rsqrt(var + LN_EPS) * g + b


def _silu(x):
    return x * jax.nn.sigmoid(x)


def _resident(shape):
    nd = len(shape)
    return pl.BlockSpec(shape, lambda *_: (0,) * nd, pipeline_mode=pl.Buffered(1))


def _ffn_ln_kernel(x_ref, wg_ref, wu_ref, wd_ref, g_ref, b_ref, o_ref, *, alpha):
    x = x_ref[...]
    xb = x.astype(BF16)
    gate = jnp.dot(xb, wg_ref[...], preferred_element_type=F32)
    up = jnp.dot(xb, wu_ref[...], preferred_element_type=F32)
    h = (_silu(gate) * up).astype(BF16)
    y = jnp.dot(h, wd_ref[...], preferred_element_type=F32)
    o_ref[...] = _layer_norm(alpha * x + 0.5 * y, g_ref[...], b_ref[...])


def _ffn_ln(x, wg, wu, wd, g, b, alpha):
    n, d = x.shape
    dff = wg.shape[1]
    tm = TOKEN_TILE
    return pl.pallas_call(
        functools.partial(_ffn_ln_kernel, alpha=alpha),
        out_shape=jax.ShapeDtypeStruct((n, d), F32),
        grid=(n // tm,),
        in_specs=[pl.BlockSpec((tm, d), lambda i: (i, 0)),
                  _resident((d, dff)), _resident((d, dff)), _resident((dff, d)),
                  _resident((1, d)), _resident((1, d))],
        out_specs=pl.BlockSpec((tm, d), lambda i: (i, 0)),
        compiler_params=_cparams("parallel"),
        name="ffn_ln",
    )(x, wg, wu, wd, g, b)


def _in_proj_kernel(x_ref, wa_ref, wr_ref, wt_ref, ba_ref, br_ref, bt_ref,
                    pa_ref, pr_ref, kat_ref, kbt_ref, kct_ref, gt_ref):
    xb = x_ref[...].astype(BF16)
    pa_ref[...] = jnp.dot(xb, wa_ref[...], preferred_element_type=F32) + ba_ref[...]
    pr_ref[...] = jnp.dot(xb, wr_ref[...], preferred_element_type=F32) + br_ref[...]
    tr = _dot_nt(wt_ref[...], xb) + bt_ref[...]
    kat_ref[...] = tr[0:W_GROUP].astype(BF16)
    kbt_ref[...] = tr[W_GROUP:2 * W_GROUP]
    kct_ref[...] = tr[2 * W_GROUP:3 * W_GROUP]
    gt_ref[...] = tr[3 * W_GROUP:3 * W_GROUP + N_GATES]


def _in_proj(x, wa, wr, wt, ba, br, bt):
    n, d = x.shape
    ca, cr, ct = wa.shape[1], wr.shape[1], wt.shape[0]
    tm = TOKEN_TILE
    tok = lambda w: pl.BlockSpec((tm, w), lambda i: (i, 0))
    trn = lambda r: pl.BlockSpec((r, tm), lambda i: (0, i))
    return pl.pallas_call(
        _in_proj_kernel,
        out_shape=(jax.ShapeDtypeStruct((n, ca), F32), jax.ShapeDtypeStruct((n, cr), F32),
                   jax.ShapeDtypeStruct((W_GROUP, n), BF16), jax.ShapeDtypeStruct((W_GROUP, n), F32),
                   jax.ShapeDtypeStruct((W_GROUP, n), F32), jax.ShapeDtypeStruct((N_GATES, n), F32)),
        grid=(n // tm,),
        in_specs=[tok(d), _resident((d, ca)), _resident((d, cr)), _resident((ct, d)),
                  _resident((1, ca)), _resident((1, cr)), _resident((ct, 1))],
        out_specs=(tok(ca), tok(cr), trn(W_GROUP), trn(W_GROUP), trn(W_GROUP), trn(N_GATES)),
        compiler_params=_cparams("parallel"),
        name="in_proj",
    )(x, wa, wr, wt, ba, br, bt)


def _head_rows(xt, h, pos):
    xh = xt[h * HEAD_DIM:(h + 1) * HEAD_DIM]
    z = jnp.zeros_like(xh)
    return jnp.concatenate([xh, z] if pos == 0 else [z, xh], axis=0)


def _pick_half(lane_lo, a, b):
    return b if a is None else jnp.where(lane_lo, a, b)


def _col_bcast(eye_b, rows):
    hi = rows.astype(BF16)
    lo = (rows - hi.astype(F32)).astype(BF16)
    dn = (((1,), (1,)), ((), ()))
    return (lax.dot_general(eye_b, hi, dn, preferred_element_type=F32)
            + lax.dot_general(eye_b, lo, dn, preferred_element_type=F32))


def _rep_heads(rows4, n):
    return jnp.concatenate([jnp.broadcast_to(rows4[h:h + 1], (n, rows4.shape[1])) for h in range(N_HEADS)], axis=0)


def _group_norm(x, gmat):
    mu = _dot(x, gmat)
    xc = x - mu
    var = _dot(xc * xc, gmat)
    return xc * lax.rsqrt(var + LN_EPS)


def _seg_scan(x, seg_pos, seg_len, op, fill):
    s = 1
    while s < seg_len:
        x = op(x, jnp.where(seg_pos >= s, pltpu.roll(x, s, 1), fill))
        s *= 2
    return x


def _shift_rows(x, s, fill):
    if s % 8 == 0:
        return jnp.concatenate([jnp.full((s, x.shape[1]), fill, x.dtype), x[:x.shape[0] - s]], axis=0)
    row = lax.broadcasted_iota(jnp.int32, x.shape, 0)
    return jnp.where(row >= s, pltpu.roll(x, s, 0), fill)


def _mixer_kernel(pa_ref, pr_ref, kat_ref, kbt_ref, kct_ref, gt_ref, pastk_ref, pastv_ref, rel_ref,
                  cos_ref, sin_ref, cost_ref, sint_ref, dec_ref, qdec_ref, kdect_ref, sdec_ref,
                  ret0_ref, c0_ref, n0_ref, m0_ref,
                  cw_ref, cb_ref, wa_ref, ba_ref, wi_ref, bi_ref, lam_ref, buf0_ref, h0_ref,
                  y_ref, ret_out, c_out, n_out, m_out, h_out, buf_out,
                  bias_sc, spair, ppair, m_sc, win, h_sc, kpadt=None, vpad=None,
                  *, blk, seqs, n_steps, has_past):
    L = blk
    band = A_REACH + L
    n_past = A_REACH // L if not has_past else 0
    n_var = n_past + 1
    bb = pl.program_id(0)
    tb = pl.program_id(1)
    tail = CONV_W - 1
    if n_steps == 1:
        at_first = at_last = lambda f: f()
    else:
        at_first = pl.when(tb == 0)
        at_last = pl.when(tb == n_steps - 1)

    li = lax.broadcasted_iota(jnp.int32, (L, L), 0)
    lj = lax.broadcasted_iota(jnp.int32, (L, L), 1)
    causal_neg = jnp.where(lj <= li, 0.0, -jnp.inf)
    eye_b = jnp.where(li == lj, 1.0, 0.0).astype(BF16)
    lane_lo = lax.broadcasted_iota(jnp.int32, (L, SLAB), 1) < HEAD_DIM
    ones_slab = jnp.ones((L, SLAB), BF16)
    ri = lax.broadcasted_iota(jnp.int32, (SLAB, SLAB), 0) // HEAD_DIM
    rj = lax.broadcasted_iota(jnp.int32, (SLAB, SLAB), 1) // HEAD_DIM
    smask = jnp.where(ri == rj, 1.0, 0.0)
    pmask = jnp.concatenate([smask, smask], axis=1)
    gi = lax.broadcasted_iota(jnp.int32, (W_GROUP, W_GROUP), 0) // HEAD_DIM
    gj = lax.broadcasted_iota(jnp.int32, (W_GROUP, W_GROUP), 1) // HEAD_DIM
    gmat = jnp.where(gi == gj, 1.0 / HEAD_DIM, 0.0).astype(BF16)
    e64i = lax.broadcasted_iota(jnp.int32, (HEAD_DIM, HEAD_DIM), 0)
    e64j = lax.broadcasted_iota(jnp.int32, (HEAD_DIM, HEAD_DIM), 1)
    eye64 = e64i == e64j
    row_lo = lax.broadcasted_iota(jnp.int32, (SLAB, 1), 0) < HEAD_DIM

    @pl.when((bb == 0) & (tb == 0))
    def _():
        qq = lax.broadcasted_iota(jnp.int32, (L, band), 0)
        kk = lax.broadcasted_iota(jnp.int32, (L, band), 1)
        off = kk - (qq // CHUNK) * CHUNK
        for h in range(N_HEADS):
            row = jnp.broadcast_to(rel_ref[h], (L, ROLL_W))
            tile = pltpu.roll(row, ROLL_W - L, 1, stride=1, stride_axis=0)[:, :band] * LOG2E
            tile = jnp.where(off >= 0, jnp.where(off < A_REACH + CHUNK, tile, -jnp.inf), -jnp.inf)
            for v in range(n_var):
                first_col = (n_past - v) * L
                bias_sc[v * N_HEADS + h] = jnp.where(kk >= first_col, tile, -jnp.inf) if first_col > 0 else tile

    g8 = gt_ref[...]
    seg_pos = lax.broadcasted_iota(jnp.int32, g8.shape, 1) % L
    b8 = _seg_scan(jax.nn.log_sigmoid(g8), seg_pos, L, jnp.add, 0.0)
    a8 = g8 - pltpu.roll(b8, N_HEADS, 0)
    cm8 = _seg_scan(a8, seg_pos, L, jnp.maximum, -jnp.inf)
    b8 = pltpu.roll(b8, N_HEADS, 0)

    for s in range(seqs):
        seg = slice(s * L, (s + 1) * L)

        @at_first
        def _():
            for p in range(N_PAIRS):
                sblk, cblk = [], []
                for hh in range(2):
                    h = 2 * p + hh
                    z = jnp.zeros((HEAD_DIM, HEAD_DIM), F32)
                    s_h = ret0_ref[s, h]
                    c_h = c0_ref[s, h]
                    n_row = n0_ref[s, h:h + 1, :]
                    n_col = jnp.sum(jnp.where(eye64, jnp.broadcast_to(n_row, eye64.shape), 0.0),
                                    axis=1, keepdims=True)
                    n_rep = jnp.broadcast_to(n_col, (HEAD_DIM, HEAD_DIM))
                    sblk.append(jnp.concatenate([s_h, z] if hh == 0 else [z, s_h], axis=1))
                    cblk.append(jnp.concatenate([c_h, z, n_rep, z] if hh == 0 else [z, c_h, z, n_rep], axis=1))
                spair[s, p] = jnp.concatenate(sblk, axis=0)
                ppair[s, p] = jnp.concatenate(cblk, axis=0)
            m_sc[s, 0:N_HEADS, :] = jnp.broadcast_to(m0_ref[s], (N_HEADS, m_sc.shape[2]))
            win[s, CONV_PAD - tail:CONV_PAD, :] = buf0_ref[s]
            h_sc[s] = h0_ref[s]
            if not has_past:
                vpad[s, 0:A_REACH, :] = jnp.zeros((A_REACH, W_GROUP), BF16)
                kpadt[s, 0:n_past * W_GROUP, :] = jnp.zeros((n_past * W_GROUP, L), BF16)

        pa = pa_ref[s]
        col = lambda j: pr_ref[s, :, j * W_GROUP:(j + 1) * W_GROUP]

        kat = kat_ref[:, seg]
        v_new = pa[:, 2 * W_GROUP:].astype(BF16)
        if has_past:
            kpast = pastk_ref[s].astype(BF16)
            vband = jnp.concatenate([pastv_ref[s].astype(BF16), v_new], axis=0)
            bias_at = lambda h: bias_sc[h]
        else:
            kpadt[s, pl.ds(pl.multiple_of((tb + n_past) * W_GROUP, W_GROUP), W_GROUP), :] = kat
            kwin = kpadt[s, pl.ds(pl.multiple_of(tb * W_GROUP, W_GROUP), (n_past + 1) * W_GROUP), :]
            start = pl.multiple_of(tb * L, L)
            vpad[s, pl.ds(A_REACH + start, L), :] = v_new
            vband = vpad[s, pl.ds(start, band), :]
            var = jnp.minimum(tb, n_var - 1) * N_HEADS
            bias_at = lambda h: bias_sc[var + h]
        qa = (pa[:, 0:W_GROUP] * (HEAD_DIM ** -0.5 * LOG2E)).astype(BF16)
        ya = []
        for p in range(N_PAIRS):
            sl = slice(p * SLAB, (p + 1) * SLAB)
            q_slab = qa[:, sl]
            o_pair = None
            for hh in range(2):
                h = 2 * p + hh
                if has_past:
                    k_slab = kpast[:, sl]
                    lane = lax.broadcasted_iota(jnp.int32, k_slab.shape, 1)
                    lane_h = lane < HEAD_DIM if hh == 0 else lane >= HEAD_DIM
                    parts = [_dot_nt(q_slab, jnp.where(lane_h, k_slab, jnp.zeros_like(k_slab))),
                             jnp.dot(q_slab, _head_rows(kat, h, hh), preferred_element_type=F32)]
                else:
                    parts = [jnp.dot(q_slab, _head_rows(kwin[i * W_GROUP:(i + 1) * W_GROUP], h, hh),
                                     preferred_element_type=F32) for i in range(n_past + 1)]
                sc = jnp.concatenate(parts, axis=1) + bias_at(h)
                mx = jnp.max(sc, axis=1, keepdims=True)
                pe = jnp.exp2(sc - mx)
                den = jnp.sum(pe, axis=1, keepdims=True)
                res = _dot(pe, vband)[:, sl] / den
                o_pair = _pick_half(lane_lo, o_pair, res)
            ya.append(o_pair)
        y_ref[s, :, 0:W_GROUP] = jnp.concatenate(ya, axis=1)

        qr = (col(PR_QB) * cos_ref[...] + col(PR_QBS) * sin_ref[...]).astype(BF16)
        kbt = kbt_ref[:, seg]
        half = HEAD_DIM // 2
        kbt_sw = jnp.concatenate([kbt[h * HEAD_DIM + o:h * HEAD_DIM + o + half]
                                  for h in range(N_HEADS) for o in (half, 0)], axis=0)
        krt = (kbt * cost_ref[:, seg] + kbt_sw * sint_ref[:, seg]) * HEAD_DIM ** -0.5
        krt_b = krt.astype(BF16)
        kdt_b = (krt * kdect_ref[...]).astype(BF16)
        vb = col(PR_VB).astype(BF16)
        qdec = qdec_ref[...]
        ob = []
        for p in range(N_PAIRS):
            sl = slice(p * SLAB, (p + 1) * SLAB)
            q_slab = qr[:, sl]
            v_slab = vb[:, sl]
            s0 = spair[s, p]
            o_pair = None
            for hh in range(2):
                h = 2 * p + hh
                scores = jnp.dot(q_slab, _head_rows(krt_b, h, hh), preferred_element_type=F32) * dec_ref[h]
                o_pair = _pick_half(lane_lo, o_pair, _dot(scores, v_slab))
            ob.append(o_pair + _dot(q_slab, s0) * qdec[:, sl])
            spair[s, p] = s0 * sdec_ref[p] + _dot(kdt_b[sl], v_slab) * smask
        ob = jnp.concatenate(ob, axis=1)
        y_ref[s, :, W_GROUP:2 * W_GROUP] = _group_norm(ob, gmat) * _silu(col(PR_GB))

        a4 = a8[0:N_HEADS, seg]
        b4 = b8[0:N_HEADS, seg]
        m0c = m_sc[s, 0:N_HEADS, 0:1]
        big_m = jnp.maximum(m0c, cm8[0:N_HEADS, seg])
        m4 = b4 + big_m
        w0_4 = jnp.exp(m0c - big_m)
        m_last = big_m[:, L - 1:L]
        wrow4 = jnp.exp(a4 - m_last)
        w0_bc = _col_bcast(eye_b, _rep_heads(w0_4, HEAD_DIM))
        floor_bc = _col_bcast(eye_b, _rep_heads(jnp.exp(-m4), HEAD_DIM))
        kct = kct_ref[:, seg] * HEAD_DIM ** -0.5
        kct_b = kct.astype(BF16)
        kw_b = (kct * _rep_heads(wrow4, HEAD_DIM)).astype(BF16)
        qc = col(PR_QC).astype(BF16)
        vc = col(PR_VC).astype(BF16)
        hc = []
        for p in range(N_PAIRS):
            sl = slice(p * SLAB, (p + 1) * SLAB)
            q_slab = qc[:, sl]
            v_aug = jnp.concatenate([vc[:, sl], ones_slab], axis=1)
            p0 = ppair[s, p]
            num_pair = None
            den_pair = None
            for hh in range(2):
                h = 2 * p + hh
                m_bc = _col_bcast(eye_b, jnp.broadcast_to(big_m[h:h + 1], (L, L)))
                d = jnp.exp(a4[h:h + 1] - m_bc + causal_neg)
                qk = jnp.dot(q_slab, _head_rows(kct_b, h, hh), preferred_element_type=F32) * d
                res = _dot(qk, v_aug)
                num_pair = _pick_half(lane_lo, num_pair, res[:, 0:SLAB])
                den_pair = _pick_half(lane_lo, den_pair, res[:, SLAB:])
            inter = _dot(q_slab, p0)
            w0 = w0_bc[:, sl]
            num = num_pair + inter[:, 0:SLAB] * w0
            den = den_pair + inter[:, SLAB:] * w0
            den = jnp.maximum(jnp.abs(den), floor_bc[:, sl])
            hc.append(num / den)
            w_state = jnp.where(row_lo, w0_4[2 * p:2 * p + 1, L - 1:L], w0_4[2 * p + 1:2 * p + 2, L - 1:L])
            ppair[s, p] = p0 * w_state + _dot(kw_b[sl], v_aug) * pmask
        hc = jnp.concatenate(hc, axis=1)
        m_sc[s, 0:N_HEADS, :] = jnp.broadcast_to(m4[:, L - 1:L], (N_HEADS, m_sc.shape[2]))
        y_ref[s, :, 2 * W_GROUP:3 * W_GROUP] = _group_norm(hc, gmat) * jax.nn.sigmoid(col(PR_OC))

        xd = col(PR_XD)
        win[s, CONV_PAD:CONV_PAD + L, :] = xd
        xc = cb_ref[...]
        for j in range(CONV_W):
            off = CONV_PAD - tail + j
            xc = xc + win[s, off:off + L, :] * cw_ref[j:j + 1, :]
        win[s, CONV_PAD - tail:CONV_PAD, :] = xd[L - tail:, :]
        r = jax.nn.sigmoid(_dot(xc, wa_ref[...]) + ba_ref[...])
        i = jax.nn.sigmoid(_dot(xc, wi_ref[...]) + bi_ref[...])
        neg_lam = -lam_ref[...]
        softplus = jnp.maximum(neg_lam, 0.0) + jnp.log1p(jnp.exp(-jnp.abs(neg_lam)))
        log_a = -LRU_C * r * softplus
        a = jnp.exp(log_a)
        u = jnp.sqrt(-jnp.tanh(log_a) * (jnp.exp(2.0 * log_a) + 1.0)) * (i * xc)
        st = 1
        while st < L:
            u = a * _shift_rows(u, st, 0.0) + u
            a = a * _shift_rows(a, st, 1.0)
            st *= 2
        hseq = u + a * h_sc[s]
        h_sc[s] = hseq[L - 1:L, :]
        y_ref[s, :, 3 * W_GROUP:] = hseq * jax.nn.gelu(col(PR_YD))

        @at_last
        def _():
            for p in range(N_PAIRS):
                sp = spair[s, p]
                pp = ppair[s, p]
                for hh in range(2):
                    h = 2 * p + hh
                    rs = slice(hh * HEAD_DIM, (hh + 1) * HEAD_DIM)
                    ret_out[s, h] = sp[rs, rs]
                    c_out[s, h] = pp[rs, rs]
                    n_rep = pp[rs, SLAB + hh * HEAD_DIM:SLAB + (hh + 1) * HEAD_DIM]
                    n_out[s, h:h + 1, :] = jnp.sum(jnp.where(eye64, n_rep, 0.0), axis=0, keepdims=True)
            m_out[s] = m_sc[s, 0:N_HEADS, 0:1]
            h_out[s] = h_sc[s]
            buf_out[s] = xd[L - tail:, :]


def _mixer_kernel_fresh(pa_ref, pr_ref, kat_ref, kbt_ref, kct_ref, gt_ref, *rest, **static):
    _mixer_kernel(pa_ref, pr_ref, kat_ref, kbt_ref, kct_ref, gt_ref, None, None, *rest, **static)


def _mixer_geometry(t, has_past):
    if has_past:
        assert t == CHUNK, "sequences with a carried cache are expected to be a single chunk"
        return t, MIX_SEQS
    assert t % MIX_BLOCK == 0 and A_REACH % MIX_BLOCK == 0
    return MIX_BLOCK, 1


def _mixer_tables(pos, blk, seqs):
    cos, sin = _rotary_tables(pos)
    return (cos, sin, jnp.tile(cos.T, (1, seqs)), jnp.tile(sin.T, (1, seqs))) + _retention_tables(blk)


def _mixer(pa, pr, kat, kbt, kct, gt, past, rel_rows, tables, ret0, c0, n0, m0, lw, buf0, h0):
    bsz, t, _ = pa.shape
    has_past = past is not None
    blk, seqs = _mixer_geometry(t, has_past)
    nb = t // blk
    assert bsz % seqs == 0
    band = A_REACH + blk
    n_var = 1 if has_past else A_REACH // blk + 1
    tail = CONV_W - 1

    tok3 = lambda w: pl.BlockSpec((seqs, blk, w), lambda b, c: (b, c, 0))
    trn = lambda r: pl.BlockSpec((r, seqs * blk), lambda b, c: (0, b * nb + c))
    per_seq = lambda *shape: pl.BlockSpec((seqs,) + shape, lambda b, c: (b,) + (0,) * len(shape))
    const = lambda *shape: pl.BlockSpec(shape, lambda b, c: (0,) * len(shape))
    past_specs = [per_seq(A_REACH, W_GROUP), per_seq(A_REACH, W_GROUP)] if has_past else []
    in_specs = [tok3(3 * W_GROUP), tok3(N_PR * W_GROUP), trn(W_GROUP), trn(W_GROUP), trn(W_GROUP), trn(N_GATES),
                *past_specs, const(N_HEADS, 1, ROLL_W),
                pl.BlockSpec((blk, W_GROUP), lambda b, c: (c, 0)), pl.BlockSpec((blk, W_GROUP), lambda b, c: (c, 0)),
                pl.BlockSpec((W_GROUP, seqs * blk), lambda b, c: (0, c)),
                pl.BlockSpec((W_GROUP, seqs * blk), lambda b, c: (0, c)),
                const(N_HEADS, blk, blk), const(blk, W_GROUP), const(W_GROUP, blk), const(N_PAIRS, SLAB, SLAB),
                per_seq(N_HEADS, HEAD_DIM, HEAD_DIM), per_seq(N_HEADS, HEAD_DIM, HEAD_DIM),
                per_seq(N_HEADS, HEAD_DIM), per_seq(N_HEADS, 1),
                const(CONV_W, W_GROUP), const(1, W_GROUP), const(W_GROUP, W_GROUP), const(1, W_GROUP),
                const(W_GROUP, W_GROUP), const(1, W_GROUP), const(1, W_GROUP),
                per_seq(tail, W_GROUP), per_seq(1, W_GROUP)]
    out_shape = (jax.ShapeDtypeStruct((bsz, t, 4 * W_GROUP), F32),
                 jax.ShapeDtypeStruct((bsz, N_HEADS, HEAD_DIM, HEAD_DIM), F32),
                 jax.ShapeDtypeStruct((bsz, N_HEADS, HEAD_DIM, HEAD_DIM), F32),
                 jax.ShapeDtypeStruct((bsz, N_HEADS, HEAD_DIM), F32),
                 jax.ShapeDtypeStruct((bsz, N_HEADS, 1), F32),
                 jax.ShapeDtypeStruct((bsz, 1, W_GROUP), F32),
                 jax.ShapeDtypeStruct((bsz, tail, W_GROUP), F32))
    out_specs = (tok3(4 * W_GROUP), per_seq(N_HEADS, HEAD_DIM, HEAD_DIM), per_seq(N_HEADS, HEAD_DIM, HEAD_DIM),
                 per_seq(N_HEADS, HEAD_DIM), per_seq(N_HEADS, 1), per_seq(1, W_GROUP), per_seq(tail, W_GROUP))
    scratch = [pltpu.VMEM((n_var * N_HEADS, blk, band), F32),
               pltpu.VMEM((seqs, N_PAIRS, SLAB, SLAB), F32), pltpu.VMEM((seqs, N_PAIRS, SLAB, 2 * SLAB), F32),
               pltpu.VMEM((seqs, 8, 128), F32), pltpu.VMEM((seqs, CONV_PAD + blk, W_GROUP), F32),
               pltpu.VMEM((seqs, 1, W_GROUP), F32)]
    if not has_past:
        scratch += [pltpu.VMEM((seqs, (nb + A_REACH // blk) * W_GROUP, blk), BF16),
                    pltpu.VMEM((seqs, A_REACH + t, W_GROUP), BF16)]
    body = _mixer_kernel if has_past else _mixer_kernel_fresh
    return pl.pallas_call(
        functools.partial(body, blk=blk, seqs=seqs, n_steps=nb, has_past=has_past),
        out_shape=out_shape,
        grid=(bsz // seqs, nb),
        in_specs=in_specs,
        out_specs=out_specs,
        scratch_shapes=scratch,
        compiler_params=_cparams("arbitrary", "arbitrary"),
        name="mixer",
    )(pa, pr, kat, kbt, kct, gt, *(past or ()), rel_rows, *tables,
      ret0, c0, n0, m0, lw['conv_w'], lw['conv_b'], lw['lru_wa'], lw['lru_ba'], lw['lru_wi'], lw['lru_bi'],
      lw['lru_lambda'], buf0, h0)


def _post_kernel(x_ref, y_ref, wo_ref, g1_ref, b1_ref, wq_ref, mk_ref, mv_ref, xo_ref,
                 g2_ref, b2_ref, o_ref, att_sc, *, alpha, seqs):
    x = x_ref[...]
    x2 = _layer_norm(alpha * x + _dot(y_ref[...], wo_ref[...]), g1_ref[...], b1_ref[...])
    q = _dot(x2, wq_ref[...]).astype(BF16)
    rows = x.shape[0] // seqs
    for s in range(seqs):
        rs = slice(s * rows, (s + 1) * rows)
        mk = mk_ref[s].astype(BF16)
        mv = mv_ref[s].astype(BF16)
        for h in range(N_HEADS):
            sl = slice(h * HEAD_DIM, (h + 1) * HEAD_DIM)
            sc = _dot_nt(q[rs, sl], mk[:, sl]) * HEAD_DIM ** -0.5
            m = jnp.max(sc, axis=1, keepdims=True)
            p = jnp.exp(sc - m)
            den = jnp.sum(p, axis=1, keepdims=True)
            att_sc[rs, sl] = _dot(p, mv[:, sl]) / den
    o_ref[...] = _layer_norm(alpha * x2 + _dot(att_sc[...], xo_ref[...]), g2_ref[...], b2_ref[...])


def _post(x, y, wo, g1, b1, wq, mk, mv, xo, g2, b2, alpha, seq_len):
    n, d = x.shape
    n_mem, dx = mk.shape[1], mk.shape[2]
    tm = TOKEN_TILE
    seqs = max(1, tm // seq_len)
    tiles_per_seq = max(1, seq_len // tm)
    tok = lambda w: pl.BlockSpec((tm, w), lambda i: (i, 0))
    mem = pl.BlockSpec((seqs, n_mem, dx), lambda i: (i // tiles_per_seq, 0, 0))
    return pl.pallas_call(
        functools.partial(_post_kernel, alpha=alpha, seqs=seqs),
        out_shape=jax.ShapeDtypeStruct((n, d), F32),
        grid=(n // tm,),
        in_specs=[tok(d), tok(y.shape[1]),
                  _resident(wo.shape), _resident((1, d)), _resident((1, d)), _resident(wq.shape),
                  mem, mem, _resident(xo.shape), _resident((1, d)), _resident((1, d))],
        out_specs=tok(d),
        scratch_shapes=[pltpu.VMEM((tm, dx), F32)],
        compiler_params=_cparams("parallel"),
        name="post",
    )(x, y, wo, g1, b1, wq, mk, mv, xo, g2, b2)


def _mem_kv_kernel(x_ref, w_ref, o_ref):
    o_ref[...] = _dot(x_ref[...], w_ref[...])


def _mem_kv(mem, w):
    n, d = mem.shape
    tm = min(TOKEN_TILE, n)
    return pl.pallas_call(
        _mem_kv_kernel,
        out_shape=jax.ShapeDtypeStruct((n, w.shape[1]), F32),
        grid=(n // tm,),
        in_specs=[pl.BlockSpec((tm, d), lambda i: (i, 0)), _resident(w.shape)],
        out_specs=pl.BlockSpec((tm, w.shape[1]), lambda i: (i, 0)),
        compiler_params=_cparams("parallel"),
        name="mem_kv",
    )(mem, w)


def _rotary_tables(pos):
    half = HEAD_DIM // 2
    inv = jnp.exp(-jnp.log(10000.0) * jnp.arange(half, dtype=F32) / half)
    ang = pos.astype(F32)[:, None] * inv[None, :]
    cos, sin = jnp.cos(ang), jnp.sin(ang)
    cos_full = jnp.tile(jnp.concatenate([cos, cos], -1), (1, N_HEADS))
    sin_full = jnp.tile(jnp.concatenate([-sin, sin], -1), (1, N_HEADS))
    return cos_full, sin_full


def _retention_tables(blk):
    log_g = jnp.log1p(-jnp.exp2(-5.0 - jnp.arange(N_HEADS, dtype=F32)))
    idx = jnp.arange(blk, dtype=F32)
    diff = idx[:, None] - idx[None, :]
    dec = jnp.exp(jnp.where((diff >= 0)[None], diff[None] * log_g[:, None, None], -jnp.inf))
    q_dec = jnp.exp((idx[:, None] + 1.0) * log_g[None, :])
    k_dec = jnp.exp((blk - 1.0 - idx)[:, None] * log_g[None, :])
    s_dec = jnp.exp(blk * log_g)
    lanes = lambda tbl: jnp.repeat(tbl, HEAD_DIM, axis=1)
    pair = lambda p: jnp.kron(jnp.diag(s_dec[2 * p:2 * p + 2]), jnp.ones((HEAD_DIM, HEAD_DIM), F32))
    return dec, lanes(q_dec), lanes(k_dec).T, jnp.stack([pair(p) for p in range(N_PAIRS)])


def _rel_bias_rows(rel_bias, blk):
    idx = np.clip(A_REACH + blk - np.arange(ROLL_W), -REL_CLIP, REL_CLIP) + REL_CLIP
    return rel_bias[:, idx][:, None, :]


def _block_diag(w):
    h, c, _ = w.shape
    eye = jnp.eye(h, dtype=w.dtype)
    return (eye[:, None, :, None] * w[:, :, None, :]).reshape(h * c, h * c)


def _swap_perm():
    j = np.arange(W_GROUP)
    return (j // HEAD_DIM) * HEAD_DIM + (j % HEAD_DIM + HEAD_DIM // 2) % HEAD_DIM


def _layer(x, tables, mem_k, mem_v, past, rec, lw, alpha):
    bsz, t, d = x.shape
    n = bsz * t
    ret_s, ml_c, ml_n, ml_m, lru_h, conv_buf = rec
    g, b = lw['ln_g'], lw['ln_b']

    x1 = _ffn_ln(x.reshape(n, d), lw['ffn1_gate'], lw['ffn1_up'], lw['ffn1_down'], g[0:1], b[0:1], alpha)
    pa, pr, kat, kbt, kct, gt = _in_proj(x1, lw['w_in_a'], lw['w_in_r'], lw['w_in_t'],
                                         lw['b_in_a'], lw['b_in_r'], lw['b_in_t'])
    pa = pa.reshape(bsz, t, 3 * W_GROUP)
    pr = pr.reshape(bsz, t, N_PR * W_GROUP)
    blk, _ = _mixer_geometry(t, past is not None)
    y, ret_new, c_new, n_new, m_new, h_new, conv_new = _mixer(
        pa, pr, kat, kbt, kct, gt, past, _rel_bias_rows(lw['rel_bias'], blk), tables,
        ret_s, ml_c, ml_n, ml_m[:, :, None], lw, conv_buf, lru_h[:, None, :])

    x3 = _post(x1, y.reshape(n, 4 * W_GROUP), lw['w_out'], g[1:2], b[1:2], lw['x_wq'],
               mem_k, mem_v, lw['x_wo'], g[2:3], b[2:3], alpha, t)
    x4 = _ffn_ln(x3, lw['ffn2_gate'], lw['ffn2_up'], lw['ffn2_down'], g[3:4], b[3:4], alpha)

    heads = lambda a: a.reshape(bsz, -1, N_HEADS, HEAD_DIM)
    keep = min(A_REACH, t)
    a_k = heads(pa[:, t - keep:, W_GROUP:2 * W_GROUP])
    a_v = heads(pa[:, t - keep:, 2 * W_GROUP:])
    state = (a_k, a_v, ret_new, c_new, n_new, m_new[:, :, 0], h_new[:, 0, :], conv_new)
    return x4.reshape(bsz, t, d), state


def kernel(x_prompt, x_sample, mem_prompt, cache_a_k, cache_a_v, state_ret, state_mlstm_c, state_mlstm_n,
           state_mlstm_m, state_lru_h, state_conv, cache_mem_k, cache_mem_v, ln_g, ln_b,
           ffn1_gate, ffn1_up, ffn1_down, ffn2_gate, ffn2_up, ffn2_down, w_in, b_in, a_rel_bias,
           conv_w, conv_b, lru_wa, lru_ba, lru_wi, lru_bi, lru_lambda, w_out, x_wq, x_wk, x_wv, x_wo):
    depth = ln_g.shape[0]
    alpha = (2.0 * depth) ** 0.25
    row = lambda v: v[None, :]
    grp = lambda a, j: a[..., j * W_GROUP:(j + 1) * W_GROUP]
    perm = _swap_perm()
    QA, KA, VA, QB, KB, VB, GB, QC, KC, VC, OC, XD, YD = range(13)

    def layer_weights(l):
        bf = lambda w: w[l].astype(BF16)
        w, bias = w_in[l], b_in[l]
        nat = lambda a: jnp.concatenate(
            [grp(a, QB), grp(a, QB)[..., perm], grp(a, VB), grp(a, GB), grp(a, QC), grp(a, VC), grp(a, OC),
             grp(a, XD), grp(a, YD)], axis=-1)
        trn = lambda a: jnp.concatenate([grp(a, KA), grp(a, KB), grp(a, KC), a[..., 13 * W_GROUP:]], axis=-1)
        return {'ln_g': ln_g[l], 'ln_b': ln_b[l],
                'ffn1_gate': bf(ffn1_gate), 'ffn1_up': bf(ffn1_up), 'ffn1_down': bf(ffn1_down),
                'ffn2_gate': bf(ffn2_gate), 'ffn2_up': bf(ffn2_up), 'ffn2_down': bf(ffn2_down),
                'w_in_a': w[:, :3 * W_GROUP].astype(BF16), 'w_in_r': nat(w).astype(BF16),
                'w_in_t': trn(w).T.astype(BF16),
                'b_in_a': row(bias[:3 * W_GROUP]), 'b_in_r': row(nat(bias)), 'b_in_t': trn(bias)[:, None],
                'rel_bias': a_rel_bias[l],
                'conv_w': conv_w[l], 'conv_b': row(conv_b[l]),
                'lru_wa': _block_diag(lru_wa[l]).astype(BF16), 'lru_ba': row(lru_ba[l]),
                'lru_wi': _block_diag(lru_wi[l]).astype(BF16), 'lru_bi': row(lru_bi[l]),
                'lru_lambda': row(lru_lambda[l]),
                'w_out': bf(w_out), 'x_wq': bf(x_wq), 'x_wo': bf(x_wo),
                'x_wkv': jnp.concatenate([x_wk[l], x_wv[l]], axis=1).astype(BF16)}

    weights = [layer_weights(l) for l in range(depth)]
    dx = x_wk.shape[2]

    bp, tp, d = x_prompt.shape
    n_mem = mem_prompt.shape[1]
    h = x_prompt
    prompt_states = []
    tables = _mixer_tables(jnp.arange(tp), *_mixer_geometry(tp, False))
    for l in range(depth):
        lw = weights[l]
        mkv = _mem_kv(mem_prompt.reshape(bp * n_mem, d), lw['x_wkv'])
        mk = mkv[:, :dx].reshape(bp, n_mem, dx)
        mv = mkv[:, dx:].reshape(bp, n_mem, dx)
        rec0 = (jnp.zeros((bp, N_HEADS, HEAD_DIM, HEAD_DIM), F32),
                jnp.zeros((bp, N_HEADS, HEAD_DIM, HEAD_DIM), F32),
                jnp.zeros((bp, N_HEADS, HEAD_DIM), F32),
                jnp.zeros((bp, N_HEADS), F32),
                jnp.zeros((bp, W_GROUP), F32),
                jnp.zeros((bp, CONV_W - 1, W_GROUP), F32))
        h, st = _layer(h, tables, mk, mv, None, rec0, lw, alpha)
        prompt_states.append(st + (mk.reshape(bp, n_mem, -1, HEAD_DIM), mv.reshape(bp, n_mem, -1, HEAD_DIM)))
    y_prompt = h
    prompt_out = [jnp.stack(f) for f in zip(*prompt_states)]

    bs, ts, _ = x_sample.shape
    h = x_sample
    sample_states = []
    tables = _mixer_tables(PAST_LEN + jnp.arange(ts), *_mixer_geometry(ts, True))
    for l in range(depth):
        rec = (state_ret[l], state_mlstm_c[l], state_mlstm_n[l], state_mlstm_m[l], state_lru_h[l], state_conv[l])
        past = (cache_a_k[l].reshape(bs, A_REACH, W_GROUP), cache_a_v[l].reshape(bs, A_REACH, W_GROUP))
        h, st = _layer(h, tables, cache_mem_k[l].reshape(bs, n_mem, dx), cache_mem_v[l].reshape(bs, n_mem, dx),
                       past, rec, weights[l], alpha)
        sample_states.append(st)
    y_sample = h
    sample_out = [jnp.stack(f) for f in zip(*sample_states)]

    return (y_prompt, y_sample, *prompt_out, *sample_out)
```

```python
import functools

import numpy as np
import jax
import jax.numpy as jnp
from jax import lax
from jax.experimental import pallas as pl
from jax.experimental.pallas import tpu as pltpu

F32 = jnp.float32
BF16 = jnp.bfloat16

CHUNK = 64
HEAD_DIM = 64
N_HEADS = 4
W_GROUP = N_HEADS * HEAD_DIM
SLAB = 2 * HEAD_DIM
N_PAIRS = N_HEADS // 2
A_BAND_CHUNKS = 8
A_REACH = A_BAND_CHUNKS * CHUNK
REL_CLIP = 128
CONV_W = 4
LRU_C = 8.0
LN_EPS = 1e-5
LOG2E = 1.4426950408889634
N_GATES = 2 * N_HEADS
PAST_LEN = 4096
ROLL_W = 1024
CONV_PAD = 8

V7X_VMEM_LIMIT_BYTES = 56 * 1024 * 1024
TOKEN_TILE = 512
MIX_BLOCK = 256
MIX_SEQS = 4

PR_QB, PR_QBS, PR_VB, PR_GB, PR_QC, PR_VC, PR_OC, PR_XD, PR_YD = range(9)
N_PR = 9


def _cparams(*sem):
    return pltpu.CompilerParams(dimension_semantics=sem, vmem_limit_bytes=V7X_VMEM_LIMIT_BYTES)


def _dot(a, b):
    return jnp.dot(a.astype(BF16), b.astype(BF16), preferred_element_type=F32)


def _dot_nt(a, b):
    return lax.dot_general(a.astype(BF16), b.astype(BF16), (((1,), (1,)), ((), ())), preferred_element_type=F32)


def _layer_norm(x, g, b):
    mu = jnp.mean(x, -1, keepdims=True)
    xc = x - mu
    var = jnp.mean(xc * xc, -1, keepdims=True)
    return xc * lax.rsqrt(var + LN_EPS) * g + b


def _silu(x):
    return x * jax.nn.sigmoid(x)


def _resident(shape):
    nd = len(shape)
    return pl.BlockSpec(shape, lambda *_: (0,) * nd, pipeline_mode=pl.Buffered(1))


def _ffn_ln_kernel(x_ref, wg_ref, wu_ref, wd_ref, g_ref, b_ref, o_ref, *, alpha):
    x = x_ref[...]
    xb = x.astype(BF16)
    gate = jnp.dot(xb, wg_ref[...], preferred_element_type=F32)
    up = jnp.dot(xb, wu_ref[...], preferred_element_type=F32)
    h = (_silu(gate) * up).astype(BF16)
    y = jnp.dot(h, wd_ref[...], preferred_element_type=F32)
    o_ref[...] = _layer_norm(alpha * x + 0.5 * y, g_ref[...], b_ref[...])


def _ffn_ln(x, wg, wu, wd, g, b, alpha):
    n, d = x.shape
    dff = wg.shape[1]
    tm = TOKEN_TILE
    return pl.pallas_call(
        functools.partial(_ffn_ln_kernel, alpha=alpha),
        out_shape=jax.ShapeDtypeStruct((n, d), F32),
        grid=(n // tm,),
        in_specs=[pl.BlockSpec((tm, d), lambda i: (i, 0)),
                  _resident((d, dff)), _resident((d, dff)), _resident((dff, d)),
                  _resident((1, d)), _resident((1, d))],
        out_specs=pl.BlockSpec((tm, d), lambda i: (i, 0)),
        compiler_params=_cparams("parallel"),
        name="ffn_ln",
    )(x, wg, wu, wd, g, b)


def _in_proj_kernel(x_ref, wa_ref, wr_ref, wt_ref, ba_ref, br_ref, bt_ref,
                    pa_ref, pr_ref, kat_ref, kbt_ref, kct_ref, gt_ref):
    xb = x_ref[...].astype(BF16)
    pa_ref[...] = jnp.dot(xb, wa_ref[...], preferred_element_type=F32) + ba_ref[...]
    pr_ref[...] = jnp.dot(xb, wr_ref[...], preferred_element_type=F32) + br_ref[...]
    tr = _dot_nt(wt_ref[...], xb) + bt_ref[...]
    kat_ref[...] = tr[0:W_GROUP].astype(BF16)
    kbt_ref[...] = tr[W_GROUP:2 * W_GROUP]
    kct_ref[...] = tr[2 * W_GROUP:3 * W_GROUP]
    gt_ref[...] = tr[3 * W_GROUP:3 * W_GROUP + N_GATES]


def _in_proj(x, wa, wr, wt, ba, br, bt):
    n, d = x.shape
    ca, cr, ct = wa.shape[1], wr.shape[1], wt.shape[0]
    tm = TOKEN_TILE
    tok = lambda w: pl.BlockSpec((tm, w), lambda i: (i, 0))
    trn = lambda r: pl.BlockSpec((r, tm), lambda i: (0, i))
    return pl.pallas_call(
        _in_proj_kernel,
        out_shape=(jax.ShapeDtypeStruct((n, ca), F32), jax.ShapeDtypeStruct((n, cr), F32),
                   jax.ShapeDtypeStruct((W_GROUP, n), BF16), jax.ShapeDtypeStruct((W_GROUP, n), F32),
                   jax.ShapeDtypeStruct((W_GROUP, n), F32), jax.ShapeDtypeStruct((N_GATES, n), F32)),
        grid=(n // tm,),
        in_specs=[tok(d), _resident((d, ca)), _resident((d, cr)), _resident((ct, d)),
                  _resident((1, ca)), _resident((1, cr)), _resident((ct, 1))],
        out_specs=(tok(ca), tok(cr), trn(W_GROUP), trn(W_GROUP), trn(W_GROUP), trn(N_GATES)),
        compiler_params=_cparams("parallel"),
        name="in_proj",
    )(x, wa, wr, wt, ba, br, bt)


def _head_rows(xt, h, pos):
    xh = xt[h * HEAD_DIM:(h + 1) * HEAD_DIM]
    z = jnp.zeros_like(xh)
    return jnp.concatenate([xh, z] if pos == 0 else [z, xh], axis=0)


def _pick_half(lane_lo, a, b):
    return b if a is None else jnp.where(lane_lo, a, b)


def _col_bcast(eye_b, rows):
    hi = rows.astype(BF16)
    lo = (rows - hi.astype(F32)).astype(BF16)
    dn = (((1,), (1,)), ((), ()))
    return (lax.dot_general(eye_b, hi, dn, preferred_element_type=F32)
            + lax.dot_general(eye_b, lo, dn, preferred_element_type=F32))


def _rep_heads(rows4, n):
    return jnp.concatenate([jnp.broadcast_to(rows4[h:h + 1], (n, rows4.shape[1])) for h in range(N_HEADS)], axis=0)


def _group_norm(x, gmat):
    mu = _dot(x, gmat)
    xc = x - mu
    var = _dot(xc * xc, gmat)
    return xc * lax.rsqrt(var + LN_EPS)


def _seg_scan(x, seg_pos, seg_len, op, fill):
    s = 1
    while s < seg_len:
        x = op(x, jnp.where(seg_pos >= s, pltpu.roll(x, s, 1), fill))
        s *= 2
    return x


def _shift_rows(x, s, fill):
    if s % 8 == 0:
        return jnp.concatenate([jnp.full((s, x.shape[1]), fill, x.dtype), x[:x.shape[0] - s]], axis=0)
    row = lax.broadcasted_iota(jnp.int32, x.shape, 0)
    return jnp.where(row >= s, pltpu.roll(x, s, 0), fill)


def _mixer_kernel(pa_ref, pr_ref, kat_ref, kbt_ref, kct_ref, gt_ref, pastk_ref, pastv_ref, rel_ref,
                  cos_ref, sin_ref, cost_ref, sint_ref, dec_ref, qdec_ref, kdect_ref, sdec_ref,
                  ret0_ref, c0_ref, n0_ref, m0_ref,
                  cw_ref, cb_ref, wa_ref, ba_ref, wi_ref, bi_ref, lam_ref, buf0_ref, h0_ref,
                  y_ref, ret_out, c_out, n_out, m_out, h_out, buf_out,
                  bias_sc, spair, ppair, m_sc, win, h_sc, kpadt=None, vpad=None,
                  *, blk, seqs, n_steps, has_past):
    L = blk
    band = A_REACH + L
    n_past = A_REACH // L if not has_past else 0
    n_var = n_past + 1
    bb = pl.program_id(0)
    tb = pl.program_id(1)
    tail = CONV_W - 1
    if n_steps == 1:
        at_first = at_last = lambda f: f()
    else:
        at_first = pl.when(tb == 0)
        at_last = pl.when(tb == n_steps - 1)

    li = lax.broadcasted_iota(jnp.int32, (L, L), 0)
    lj = lax.broadcasted_iota(jnp.int32, (L, L), 1)
    causal_neg = jnp.where(lj <= li, 0.0, -jnp.inf)
    eye_b = jnp.where(li == lj, 1.0, 0.0).astype(BF16)
    lane_lo = lax.broadcasted_iota(jnp.int32, (L, SLAB), 1) < HEAD_DIM
    ones_slab = jnp.ones((L, SLAB), BF16)
    ri = lax.broadcasted_iota(jnp.int32, (SLAB, SLAB), 0) // HEAD_DIM
    rj = lax.broadcasted_iota(jnp.int32, (SLAB, SLAB), 1) // HEAD_DIM
    smask = jnp.where(ri == rj, 1.0, 0.0)
    pmask = jnp.concatenate([smask, smask], axis=1)
    gi = lax.broadcasted_iota(jnp.int32, (W_GROUP, W_GROUP), 0) // HEAD_DIM
    gj = lax.broadcasted_iota(jnp.int32, (W_GROUP, W_GROUP), 1) // HEAD_DIM
    gmat = jnp.where(gi == gj, 1.0 / HEAD_DIM, 0.0).astype(BF16)
    e64i = lax.broadcasted_iota(jnp.int32, (HEAD_DIM, HEAD_DIM), 0)
    e64j = lax.broadcasted_iota(jnp.int32, (HEAD_DIM, HEAD_DIM), 1)
    eye64 = e64i == e64j
    row_lo = lax.broadcasted_iota(jnp.int32, (SLAB, 1), 0) < HEAD_DIM

    @pl.when((bb == 0) & (tb == 0))
    def _():
        qq = lax.broadcasted_iota(jnp.int32, (L, band), 0)
        kk = lax.broadcasted_iota(jnp.int32, (L, band), 1)
        off = kk - (qq // CHUNK) * CHUNK
        for h in range(N_HEADS):
            row = jnp.broadcast_to(rel_ref[h], (L, ROLL_W))
            tile = pltpu.roll(row, ROLL_W - L, 1, stride=1, stride_axis=0)[:, :band] * LOG2E
            tile = jnp.where(off >= 0, jnp.where(off < A_REACH + CHUNK, tile, -jnp.inf), -jnp.inf)
            for v in range(n_var):
                first_col = (n_past - v) * L
                bias_sc[v * N_HEADS + h] = jnp.where(kk >= first_col, tile, -jnp.inf) if first_col > 0 else tile

    g8 = gt_ref[...]
    seg_pos = lax.broadcasted_iota(jnp.int32, g8.shape, 1) % L
    b8 = _seg_scan(jax.nn.log_sigmoid(g8), seg_pos, L, jnp.add, 0.0)
    a8 = g8 - pltpu.roll(b8, N_HEADS, 0)
    cm8 = _seg_scan(a8, seg_pos, L, jnp.maximum, -jnp.inf)
    b8 = pltpu.roll(b8, N_HEADS, 0)

    for s in range(seqs):
        seg = slice(s * L, (s + 1) * L)

        @at_first
        def _():
            for p in range(N_PAIRS):
                sblk, cblk = [], []
                for hh in range(2):
                    h = 2 * p + hh
                    z = jnp.zeros((HEAD_DIM, HEAD_DIM), F32)
                    s_h = ret0_ref[s, h]
                    c_h = c0_ref[s, h]
                    n_row = n0_ref[s, h:h + 1, :]
                    n_col = jnp.sum(jnp.where(eye64, jnp.broadcast_to(n_row, eye64.shape), 0.0),
                                    axis=1, keepdims=True)
                    n_rep = jnp.broadcast_to(n_col, (HEAD_DIM, HEAD_DIM))
                    sblk.append(jnp.concatenate([s_h, z] if hh == 0 else [z, s_h], axis=1))
                    cblk.append(jnp.concatenate([c_h, z, n_rep, z] if hh == 0 else [z, c_h, z, n_rep], axis=1))
                spair[s, p] = jnp.concatenate(sblk, axis=0)
                ppair[s, p] = jnp.concatenate(cblk, axis=0)
            m_sc[s, 0:N_HEADS, :] = jnp.broadcast_to(m0_ref[s], (N_HEADS, m_sc.shape[2]))
            win[s, CONV_PAD - tail:CONV_PAD, :] = buf0_ref[s]
            h_sc[s] = h0_ref[s]
            if not has_past:
                vpad[s, 0:A_REACH, :] = jnp.zeros((A_REACH, W_GROUP), BF16)
                kpadt[s, 0:n_past * W_GROUP, :] = jnp.zeros((n_past * W_GROUP, L), BF16)

        pa = pa_ref[s]
        col = lambda j: pr_ref[s, :, j * W_GROUP:(j + 1) * W_GROUP]

        kat = kat_ref[:, seg]
        v_new = pa[:, 2 * W_GROUP:].astype(BF16)
        if has_past:
            kpast = pastk_ref[s].astype(BF16)
            vband = jnp.concatenate([pastv_ref[s].astype(BF16), v_new], axis=0)
            bias_at = lambda h: bias_sc[h]
        else:
            kpadt[s, pl.ds(pl.multiple_of((tb + n_past) * W_GROUP, W_GROUP), W_GROUP), :] = kat
            kwin = kpadt[s, pl.ds(pl.multiple_of(tb * W_GROUP, W_GROUP), (n_past + 1) * W_GROUP), :]
            start = pl.multiple_of(tb * L, L)
            vpad[s, pl.ds(A_REACH + start, L), :] = v_new
            vband = vpad[s, pl.ds(start, band), :]
            var = jnp.minimum(tb, n_var - 1) * N_HEADS
            bias_at = lambda h: bias_sc[var + h]
        qa = (pa[:, 0:W_GROUP] * (HEAD_DIM ** -0.5 * LOG2E)).astype(BF16)
        ya = []
        for p in range(N_PAIRS):
            sl = slice(p * SLAB, (p + 1) * SLAB)
            q_slab = qa[:, sl]
            o_pair = None
            for hh in range(2):
                h = 2 * p + hh
                if has_past:
                    k_slab = kpast[:, sl]
                    lane = lax.broadcasted_iota(jnp.int32, k_slab.shape, 1)
                    lane_h = lane < HEAD_DIM if hh == 0 else lane >= HEAD_DIM
                    parts = [_dot_nt(q_slab, jnp.where(lane_h, k_slab, jnp.zeros_like(k_slab))),
                             jnp.dot(q_slab, _head_rows(kat, h, hh), preferred_element_type=F32)]
                else:
                    parts = [jnp.dot(q_slab, _head_rows(kwin[i * W_GROUP:(i + 1) * W_GROUP], h, hh),
                                     preferred_element_type=F32) for i in range(n_past + 1)]
                sc = jnp.concatenate(parts, axis=1) + bias_at(h)
                mx = jnp.max(sc, axis=1, keepdims=True)
                pe = jnp.exp2(sc - mx)
                den = jnp.sum(pe, axis=1, keepdims=True)
                res = _dot(pe, vband)[:, sl] / den
                o_pair = _pick_half(lane_lo, o_pair, res)
            ya.append(o_pair)
        y_ref[s, :, 0:W_GROUP] = jnp.concatenate(ya, axis=1)

        qr = (col(PR_QB) * cos_ref[...] + col(PR_QBS) * sin_ref[...]).astype(BF16)
        kbt = kbt_ref[:, seg]
        half = HEAD_DIM // 2
        kbt_sw = jnp.concatenate([kbt[h * HEAD_DIM + o:h * HEAD_DIM + o + half]
                                  for h in range(N_HEADS) for o in (half, 0)], axis=0)
        krt = (kbt * cost_ref[:, seg] + kbt_sw * sint_ref[:, seg]) * HEAD_DIM ** -0.5
        krt_b = krt.astype(BF16)
        kdt_b = (krt * kdect_ref[...]).astype(BF16)
        vb = col(PR_VB).astype(BF16)
        qdec = qdec_ref[...]
        ob = []
        for p in range(N_PAIRS):
            sl = slice(p * SLAB, (p + 1) * SLAB)
            q_slab = qr[:, sl]
            v_slab = vb[:, sl]
            s0 = spair[s, p]
            o_pair = None
            for hh in range(2):
                h = 2 * p + hh
                scores = jnp.dot(q_slab, _head_rows(krt_b, h, hh), preferred_element_type=F32) * dec_ref[h]
                o_pair = _pick_half(lane_lo, o_pair, _dot(scores, v_slab))
            ob.append(o_pair + _dot(q_slab, s0) * qdec[:, sl])
            spair[s, p] = s0 * sdec_ref[p] + _dot(kdt_b[sl], v_slab) * smask
        ob = jnp.concatenate(ob, axis=1)
        y_ref[s, :, W_GROUP:2 * W_GROUP] = _group_norm(ob, gmat) * _silu(col(PR_GB))

        a4 = a8[0:N_HEADS, seg]
        b4 = b8[0:N_HEADS, seg]
        m0c = m_sc[s, 0:N_HEADS, 0:1]
        big_m = jnp.maximum(m0c, cm8[0:N_HEADS, seg])
        m4 = b4 + big_m
        w0_4 = jnp.exp(m0c - big_m)
        m_last = big_m[:, L - 1:L]
        wrow4 = jnp.exp(a4 - m_last)
        w0_bc = _col_bcast(eye_b, _rep_heads(w0_4, HEAD_DIM))
        floor_bc = _col_bcast(eye_b, _rep_heads(jnp.exp(-m4), HEAD_DIM))
        kct = kct_ref[:, seg] * HEAD_DIM ** -0.5
        kct_b = kct.astype(BF16)
        kw_b = (kct * _rep_heads(wrow4, HEAD_DIM)).astype(BF16)
        qc = col(PR_QC).astype(BF16)
        vc = col(PR_VC).astype(BF16)
        hc = []
        for p in range(N_PAIRS):
            sl = slice(p * SLAB, (p + 1) * SLAB)
            q_slab = qc[:, sl]
            v_aug = jnp.concatenate([vc[:, sl], ones_slab], axis=1)
            p0 = ppair[s, p]
            num_pair = None
            den_pair = None
            for hh in range(2):
                h = 2 * p + hh
                m_bc = _col_bcast(eye_b, jnp.broadcast_to(big_m[h:h + 1], (L, L)))
                d = jnp.exp(a4[h:h + 1] - m_bc + causal_neg)
                qk = jnp.dot(q_slab, _head_rows(kct_b, h, hh), preferred_element_type=F32) * d
                res = _dot(qk, v_aug)
                num_pair = _pick_half(lane_lo, num_pair, res[:, 0:SLAB])
                den_pair = _pick_half(lane_lo, den_pair, res[:, SLAB:])
            inter = _dot(q_slab, p0)
            w0 = w0_bc[:, sl]
            num = num_pair + inter[:, 0:SLAB] * w0
            den = den_pair + inter[:, SLAB:] * w0
            den = jnp.maximum(jnp.abs(den), floor_bc[:, sl])
            hc.append(num / den)
            w_state = jnp.where(row_lo, w0_4[2 * p:2 * p + 1, L - 1:L], w0_4[2 * p + 1:2 * p + 2, L - 1:L])
            ppair[s, p] = p0 * w_state + _dot(kw_b[sl], v_aug) * pmask
        hc = jnp.concatenate(hc, axis=1)
        m_sc[s, 0:N_HEADS, :] = jnp.broadcast_to(m4[:, L - 1:L], (N_HEADS, m_sc.shape[2]))
        y_ref[s, :, 2 * W_GROUP:3 * W_GROUP] = _group_norm(hc, gmat) * jax.nn.sigmoid(col(PR_OC))

        xd = col(PR_XD)
        win[s, CONV_PAD:CONV_PAD + L, :] = xd
        xc = cb_ref[...]
        for j in range(CONV_W):
            off = CONV_PAD - tail + j
            xc = xc + win[s, off:off + L, :] * cw_ref[j:j + 1, :]
        win[s, CONV_PAD - tail:CONV_PAD, :] = xd[L - tail:, :]
        r = jax.nn.sigmoid(_dot(xc, wa_ref[...]) + ba_ref[...])
        i = jax.nn.sigmoid(_dot(xc, wi_ref[...]) + bi_ref[...])
        neg_lam = -lam_ref[...]
        softplus = jnp.maximum(neg_lam, 0.0) + jnp.log1p(jnp.exp(-jnp.abs(neg_lam)))
        log_a = -LRU_C * r * softplus
        a = jnp.exp(log_a)
        u = jnp.sqrt(-jnp.tanh(log_a) * (jnp.exp(2.0 * log_a) + 1.0)) * (i * xc)
        st = 1
        while st < L:
            u = a * _shift_rows(u, st, 0.0) + u
            a = a * _shift_rows(a, st, 1.0)
            st *= 2
        hseq = u + a * h_sc[s]
        h_sc[s] = hseq[L - 1:L, :]
        y_ref[s, :, 3 * W_GROUP:] = hseq * jax.nn.gelu(col(PR_YD))

        @at_last
        def _():
            for p in range(N_PAIRS):
                sp = spair[s, p]
                pp = ppair[s, p]
                for hh in range(2):
                    h = 2 * p + hh
                    rs = slice(hh * HEAD_DIM, (hh + 1) * HEAD_DIM)
                    ret_out[s, h] = sp[rs, rs]
                    c_out[s, h] = pp[rs, rs]
                    n_rep = pp[rs, SLAB + hh * HEAD_DIM:SLAB + (hh + 1) * HEAD_DIM]
                    n_out[s, h:h + 1, :] = jnp.sum(jnp.where(eye64, n_rep, 0.0), axis=0, keepdims=True)
            m_out[s] = m_sc[s, 0:N_HEADS, 0:1]
            h_out[s] = h_sc[s]
            buf_out[s] = xd[L - tail:, :]


def _mixer_kernel_fresh(pa_ref, pr_ref, kat_ref, kbt_ref, kct_ref, gt_ref, *rest, **static):
    _mixer_kernel(pa_ref, pr_ref, kat_ref, kbt_ref, kct_ref, gt_ref, None, None, *rest, **static)


def _mixer_geometry(t, has_past):
    if has_past:
        assert t == CHUNK, "sequences with a carried cache are expected to be a single chunk"
        return t, MIX_SEQS
    assert t % MIX_BLOCK == 0 and A_REACH % MIX_BLOCK == 0
    return MIX_BLOCK, 1


def _mixer_tables(pos, blk, seqs):
    cos, sin = _rotary_tables(pos)
    return (cos, sin, jnp.tile(cos.T, (1, seqs)), jnp.tile(sin.T, (1, seqs))) + _retention_tables(blk)


def _mixer(pa, pr, kat, kbt, kct, gt, past, rel_rows, tables, ret0, c0, n0, m0, lw, buf0, h0):
    bsz, t, _ = pa.shape
    has_past = past is not None
    blk, seqs = _mixer_geometry(t, has_past)
    nb = t // blk
    assert bsz % seqs == 0
    band = A_REACH + blk
    n_var = 1 if has_past else A_REACH // blk + 1
    tail = CONV_W - 1

    tok3 = lambda w: pl.BlockSpec((seqs, blk, w), lambda b, c: (b, c, 0))
    trn = lambda r: pl.BlockSpec((r, seqs * blk), lambda b, c: (0, b * nb + c))
    per_seq = lambda *shape: pl.BlockSpec((seqs,) + shape, lambda b, c: (b,) + (0,) * len(shape))
    const = lambda *shape: pl.BlockSpec(shape, lambda b, c: (0,) * len(shape))
    past_specs, past_args = [], ()
    if has_past:
        layer = past[2]
        cache = pl.BlockSpec((None, seqs, A_REACH, W_GROUP), lambda b, c: (layer, b, 0, 0))
        past_specs, past_args = [cache, cache], past[:2]
    in_specs = [tok3(3 * W_GROUP), tok3(N_PR * W_GROUP), trn(W_GROUP), trn(W_GROUP), trn(W_GROUP), trn(N_GATES),
                *past_specs, const(N_HEADS, 1, ROLL_W),
                pl.BlockSpec((blk, W_GROUP), lambda b, c: (c, 0)), pl.BlockSpec((blk, W_GROUP), lambda b, c: (c, 0)),
                pl.BlockSpec((W_GROUP, seqs * blk), lambda b, c: (0, c)),
                pl.BlockSpec((W_GROUP, seqs * blk), lambda b, c: (0, c)),
                const(N_HEADS, blk, blk), const(blk, W_GROUP), const(W_GROUP, blk), const(N_PAIRS, SLAB, SLAB),
                per_seq(N_HEADS, HEAD_DIM, HEAD_DIM), per_seq(N_HEADS, HEAD_DIM, HEAD_DIM),
                per_seq(N_HEADS, HEAD_DIM), per_seq(N_HEADS, 1),
                const(CONV_W, W_GROUP), const(1, W_GROUP), const(W_GROUP, W_GROUP), const(1, W_GROUP),
                const(W_GROUP, W_GROUP), const(1, W_GROUP), const(1, W_GROUP),
                per_seq(tail, W_GROUP), per_seq(1, W_GROUP)]
    out_shape = (jax.ShapeDtypeStruct((bsz, t, 4 * W_GROUP), F32),
                 jax.ShapeDtypeStruct((bsz, N_HEADS, HEAD_DIM, HEAD_DIM), F32),
                 jax.ShapeDtypeStruct((bsz, N_HEADS, HEAD_DIM, HEAD_DIM), F32),
                 jax.ShapeDtypeStruct((bsz, N_HEADS, HEAD_DIM), F32),
                 jax.ShapeDtypeStruct((bsz, N_HEADS, 1), F32),
                 jax.ShapeDtypeStruct((bsz, 1, W_GROUP), F32),
                 jax.ShapeDtypeStruct((bsz, tail, W_GROUP), F32))
    out_specs = (tok3(4 * W_GROUP), per_seq(N_HEADS, HEAD_DIM, HEAD_DIM), per_seq(N_HEADS, HEAD_DIM, HEAD_DIM),
                 per_seq(N_HEADS, HEAD_DIM), per_seq(N_HEADS, 1), per_seq(1, W_GROUP), per_seq(tail, W_GROUP))
    scratch = [pltpu.VMEM((n_var * N_HEADS, blk, band), F32),
               pltpu.VMEM((seqs, N_PAIRS, SLAB, SLAB), F32), pltpu.VMEM((seqs, N_PAIRS, SLAB, 2 * SLAB), F32),
               pltpu.VMEM((seqs, 8, 128), F32), pltpu.VMEM((seqs, CONV_PAD + blk, W_GROUP), F32),
               pltpu.VMEM((seqs, 1, W_GROUP), F32)]
    if not has_past:
        scratch += [pltpu.VMEM((seqs, (nb + A_REACH // blk) * W_GROUP, blk), BF16),
                    pltpu.VMEM((seqs, A_REACH + t, W_GROUP), BF16)]
    body = _mixer_kernel if has_past else _mixer_kernel_fresh
    return pl.pallas_call(
        functools.partial(body, blk=blk, seqs=seqs, n_steps=nb, has_past=has_past),
        out_shape=out_shape,
        grid=(bsz // seqs, nb),
        in_specs=in_specs,
        out_specs=out_specs,
        scratch_shapes=scratch,
        compiler_params=_cparams("arbitrary", "arbitrary"),
        name="mixer",
    )(pa, pr, kat, kbt, kct, gt, *past_args, rel_rows, *tables,
      ret0, c0, n0, m0, lw['conv_w'], lw['conv_b'], lw['lru_wa'], lw['lru_ba'], lw['lru_wi'], lw['lru_bi'],
      lw['lru_lambda'], buf0, h0)


def _post_kernel(x_ref, y_ref, wo_ref, g1_ref, b1_ref, wq_ref, mk_ref, mv_ref, xo_ref,
                 g2_ref, b2_ref, o_ref, att_sc, *, alpha, seqs):
    x = x_ref[...]
    x2 = _layer_norm(alpha * x + _dot(y_ref[...], wo_ref[...]), g1_ref[...], b1_ref[...])
    q = _dot(x2, wq_ref[...]).astype(BF16)
    rows = x.shape[0] // seqs
    for s in range(seqs):
        rs = slice(s * rows, (s + 1) * rows)
        mk = mk_ref[s].astype(BF16)
        mv = mv_ref[s].astype(BF16)
        for h in range(N_HEADS):
            sl = slice(h * HEAD_DIM, (h + 1) * HEAD_DIM)
            sc = _dot_nt(q[rs, sl], mk[:, sl]) * HEAD_DIM ** -0.5
            m = jnp.max(sc, axis=1, keepdims=True)
            p = jnp.exp(sc - m)
            den = jnp.sum(p, axis=1, keepdims=True)
            att_sc[rs, sl] = _dot(p, mv[:, sl]) / den
    o_ref[...] = _layer_norm(alpha * x2 + _dot(att_sc[...], xo_ref[...]), g2_ref[...], b2_ref[...])


def _post(x, y, wo, g1, b1, wq, mk, mv, layer, xo, g2, b2, alpha, seq_len):
    n, d = x.shape
    n_mem, dx = mk.shape[2], mk.shape[3]
    tm = TOKEN_TILE
    seqs = max(1, tm // seq_len)
    tiles_per_seq = max(1, seq_len // tm)
    tok = lambda w: pl.BlockSpec((tm, w), lambda i: (i, 0))
    mem = pl.BlockSpec((None, seqs, n_mem, dx), lambda i: (layer, i // tiles_per_seq, 0, 0))
    return pl.pallas_call(
        functools.partial(_post_kernel, alpha=alpha, seqs=seqs),
        out_shape=jax.ShapeDtypeStruct((n, d), F32),
        grid=(n // tm,),
        in_specs=[tok(d), tok(y.shape[1]),
                  _resident(wo.shape), _resident((1, d)), _resident((1, d)), _resident(wq.shape),
                  mem, mem, _resident(xo.shape), _resident((1, d)), _resident((1, d))],
        out_specs=tok(d),
        scratch_shapes=[pltpu.VMEM((tm, dx), F32)],
        compiler_params=_cparams("parallel"),
        name="post",
    )(x, y, wo, g1, b1, wq, mk, mv, xo, g2, b2)


def _mem_kv_kernel(x_ref, w_ref, o_ref):
    o_ref[...] = _dot(x_ref[...], w_ref[...])


def _mem_kv(mem, w):
    n, d = mem.shape
    tm = min(TOKEN_TILE, n)
    return pl.pallas_call(
        _mem_kv_kernel,
        out_shape=jax.ShapeDtypeStruct((n, w.shape[1]), F32),
        grid=(n // tm,),
        in_specs=[pl.BlockSpec((tm, d), lambda i: (i, 0)), _resident(w.shape)],
        out_specs=pl.BlockSpec((tm, w.shape[1]), lambda i: (i, 0)),
        compiler_params=_cparams("parallel"),
        name="mem_kv",
    )(mem, w)


def _rotary_tables(pos):
    half = HEAD_DIM // 2
    inv = jnp.exp(-jnp.log(10000.0) * jnp.arange(half, dtype=F32) / half)
    ang = pos.astype(F32)[:, None] * inv[None, :]
    cos, sin = jnp.cos(ang), jnp.sin(ang)
    cos_full = jnp.tile(jnp.concatenate([cos, cos], -1), (1, N_HEADS))
    sin_full = jnp.tile(jnp.concatenate([-sin, sin], -1), (1, N_HEADS))
    return cos_full, sin_full


def _retention_tables(blk):
    log_g = jnp.log1p(-jnp.exp2(-5.0 - jnp.arange(N_HEADS, dtype=F32)))
    idx = jnp.arange(blk, dtype=F32)
    diff = idx[:, None] - idx[None, :]
    dec = jnp.exp(jnp.where((diff >= 0)[None], diff[None] * log_g[:, None, None], -jnp.inf))
    q_dec = jnp.exp((idx[:, None] + 1.0) * log_g[None, :])
    k_dec = jnp.exp((blk - 1.0 - idx)[:, None] * log_g[None, :])
    s_dec = jnp.exp(blk * log_g)
    lanes = lambda tbl: jnp.repeat(tbl, HEAD_DIM, axis=1)
    pair = lambda p: jnp.kron(jnp.diag(s_dec[2 * p:2 * p + 2]), jnp.ones((HEAD_DIM, HEAD_DIM), F32))
    return dec, lanes(q_dec), lanes(k_dec).T, jnp.stack([pair(p) for p in range(N_PAIRS)])


def _rel_bias_rows(rel_bias, blk):
    idx = np.clip(A_REACH + blk - np.arange(ROLL_W), -REL_CLIP, REL_CLIP) + REL_CLIP
    return rel_bias[:, idx][:, None, :]


def _block_diag(w):
    h, c, _ = w.shape
    eye = jnp.eye(h, dtype=w.dtype)
    return (eye[:, None, :, None] * w[:, :, None, :]).reshape(h * c, h * c)


def _swap_perm():
    j = np.arange(W_GROUP)
    return (j // HEAD_DIM) * HEAD_DIM + (j % HEAD_DIM + HEAD_DIM // 2) % HEAD_DIM


def _layer(x, tables, mem, past, rec, lw, alpha):
    bsz, t, d = x.shape
    n = bsz * t
    ret_s, ml_c, ml_n, ml_m, lru_h, conv_buf = rec
    g, b = lw['ln_g'], lw['ln_b']

    x1 = _ffn_ln(x.reshape(n, d), lw['ffn1_gate'], lw['ffn1_up'], lw['ffn1_down'], g[0:1], b[0:1], alpha)
    pa, pr, kat, kbt, kct, gt = _in_proj(x1, lw['w_in_a'], lw['w_in_r'], lw['w_in_t'],
                                         lw['b_in_a'], lw['b_in_r'], lw['b_in_t'])
    pa = pa.reshape(bsz, t, 3 * W_GROUP)
    pr = pr.reshape(bsz, t, N_PR * W_GROUP)
    blk, _ = _mixer_geometry(t, past is not None)
    y, ret_new, c_new, n_new, m_new, h_new, conv_new = _mixer(
        pa, pr, kat, kbt, kct, gt, past, _rel_bias_rows(lw['rel_bias'], blk), tables,
        ret_s, ml_c, ml_n, ml_m[:, :, None], lw, conv_buf, lru_h[:, None, :])

    x3 = _post(x1, y.reshape(n, 4 * W_GROUP), lw['w_out'], g[1:2], b[1:2], lw['x_wq'],
               *mem, lw['x_wo'], g[2:3], b[2:3], alpha, t)
    x4 = _ffn_ln(x3, lw['ffn2_gate'], lw['ffn2_up'], lw['ffn2_down'], g[3:4], b[3:4], alpha)

    heads = lambda a: a.reshape(bsz, -1, N_HEADS, HEAD_DIM)
    keep = min(A_REACH, t)
    a_k = heads(pa[:, t - keep:, W_GROUP:2 * W_GROUP])
    a_v = heads(pa[:, t - keep:, 2 * W_GROUP:])
    state = (a_k, a_v, ret_new, c_new, n_new, m_new[:, :, 0], h_new[:, 0, :], conv_new)
    return x4.reshape(bsz, t, d), state


def kernel(x_prompt, x_sample, mem_prompt, cache_a_k, cache_a_v, state_ret, state_mlstm_c, state_mlstm_n,
           state_mlstm_m, state_lru_h, state_conv, cache_mem_k, cache_mem_v, ln_g, ln_b,
           ffn1_gate, ffn1_up, ffn1_down, ffn2_gate, ffn2_up, ffn2_down, w_in, b_in, a_rel_bias,
           conv_w, conv_b, lru_wa, lru_ba, lru_wi, lru_bi, lru_lambda, w_out, x_wq, x_wk, x_wv, x_wo):
    depth = ln_g.shape[0]
    alpha = (2.0 * depth) ** 0.25
    row = lambda v: v[None, :]
    grp = lambda a, j: a[..., j * W_GROUP:(j + 1) * W_GROUP]
    perm = _swap_perm()
    QA, KA, VA, QB, KB, VB, GB, QC, KC, VC, OC, XD, YD = range(13)

    def layer_weights(l):
        bf = lambda w: w[l].astype(BF16)
        w, bias = w_in[l], b_in[l]
        nat = lambda a: jnp.concatenate(
            [grp(a, QB), grp(a, QB)[..., perm], grp(a, VB), grp(a, GB), grp(a, QC), grp(a, VC), grp(a, OC),
             grp(a, XD), grp(a, YD)], axis=-1)
        trn = lambda a: jnp.concatenate([grp(a, KA), grp(a, KB), grp(a, KC), a[..., 13 * W_GROUP:]], axis=-1)
        return {'ln_g': ln_g[l], 'ln_b': ln_b[l],
                'ffn1_gate': bf(ffn1_gate), 'ffn1_up': bf(ffn1_up), 'ffn1_down': bf(ffn1_down),
                'ffn2_gate': bf(ffn2_gate), 'ffn2_up': bf(ffn2_up), 'ffn2_down': bf(ffn2_down),
                'w_in_a': w[:, :3 * W_GROUP].astype(BF16), 'w_in_r': nat(w).astype(BF16),
                'w_in_t': trn(w).T.astype(BF16),
                'b_in_a': row(bias[:3 * W_GROUP]), 'b_in_r': row(nat(bias)), 'b_in_t': trn(bias)[:, None],
                'rel_bias': a_rel_bias[l],
                'conv_w': conv_w[l], 'conv_b': row(conv_b[l]),
                'lru_wa': _block_diag(lru_wa[l]).astype(BF16), 'lru_ba': row(lru_ba[l]),
                'lru_wi': _block_diag(lru_wi[l]).astype(BF16), 'lru_bi': row(lru_bi[l]),
                'lru_lambda': row(lru_lambda[l]),
                'w_out': bf(w_out), 'x_wq': bf(x_wq), 'x_wo': bf(x_wo),
                'x_wkv': jnp.concatenate([x_wk[l], x_wv[l]], axis=1).astype(BF16)}

    weights = [layer_weights(l) for l in range(depth)]
    dx = x_wk.shape[2]

    bp, tp, d = x_prompt.shape
    n_mem = mem_prompt.shape[1]
    h = x_prompt
    prompt_states = []
    tables = _mixer_tables(jnp.arange(tp), *_mixer_geometry(tp, False))
    for l in range(depth):
        lw = weights[l]
        mkv = _mem_kv(mem_prompt.reshape(bp * n_mem, d), lw['x_wkv'])
        mk = mkv[:, :dx].reshape(bp, n_mem, dx)
        mv = mkv[:, dx:].reshape(bp, n_mem, dx)
        rec0 = (jnp.zeros((bp, N_HEADS, HEAD_DIM, HEAD_DIM), F32),
                jnp.zeros((bp, N_HEADS, HEAD_DIM, HEAD_DIM), F32),
                jnp.zeros((bp, N_HEADS, HEAD_DIM), F32),
                jnp.zeros((bp, N_HEADS), F32),
                jnp.zeros((bp, W_GROUP), F32),
                jnp.zeros((bp, CONV_W - 1, W_GROUP), F32))
        h, st = _layer(h, tables, (mk[None], mv[None], 0), None, rec0, lw, alpha)
        prompt_states.append(st + (mk.reshape(bp, n_mem, -1, HEAD_DIM), mv.reshape(bp, n_mem, -1, HEAD_DIM)))
    y_prompt = h
    prompt_out = [jnp.stack(f) for f in zip(*prompt_states)]

    bs, ts, _ = x_sample.shape
    h = x_sample
    sample_states = []
    tables = _mixer_tables(PAST_LEN + jnp.arange(ts), *_mixer_geometry(ts, True))
    past_k_all = cache_a_k.reshape(depth, bs, A_REACH, W_GROUP)
    past_v_all = cache_a_v.reshape(depth, bs, A_REACH, W_GROUP)
    mem_k_all = cache_mem_k.reshape(depth, bs, n_mem, dx)
    mem_v_all = cache_mem_v.reshape(depth, bs, n_mem, dx)
    for l in range(depth):
        rec = (state_ret[l], state_mlstm_c[l], state_mlstm_n[l], state_mlstm_m[l], state_lru_h[l], state_conv[l])
        h, st = _layer(h, tables, (mem_k_all, mem_v_all, l), (past_k_all, past_v_all, l), rec, weights[l], alpha)
        sample_states.append(st)
    y_sample = h
    sample_out = [jnp.stack(f) for f in zip(*sample_states)]

    return (y_prompt, y_sample, *prompt_out, *sample_out)
```

```python
import functools

import numpy as np
import jax
import jax.numpy as jnp
from jax import lax
from jax.experimental import pallas as pl
from jax.experimental.pallas import tpu as pltpu

F32 = jnp.float32
BF16 = jnp.bfloat16

CHUNK = 64
HEAD_DIM = 64
N_HEADS = 4
W_GROUP = N_HEADS * HEAD_DIM
SLAB = 2 * HEAD_DIM
N_PAIRS = N_HEADS // 2
A_BAND_CHUNKS = 8
A_REACH = A_BAND_CHUNKS * CHUNK
REL_CLIP = 128
CONV_W = 4
LRU_C = 8.0
LN_EPS = 1e-5
LOG2E = 1.4426950408889634
N_GATES = 2 * N_HEADS
PAST_LEN = 4096
ROLL_W = 1024
CONV_PAD = 8

V7X_VMEM_LIMIT_BYTES = 56 * 1024 * 1024
TOKEN_TILE = 512
MIX_BLOCK = 256
MIX_SEQS = 4

PR_QB, PR_QBS, PR_VB, PR_GB, PR_QC, PR_VC, PR_OC, PR_XD, PR_YD = range(9)
N_PR = 9


def _cparams(*sem):
    return pltpu.CompilerParams(dimension_semantics=sem, vmem_limit_bytes=V7X_VMEM_LIMIT_BYTES)


def _dot(a, b):
    return jnp.dot(a.astype(BF16), b.astype(BF16), preferred_element_type=F32)


def _dot_nt(a, b):
    return lax.dot_general(a.astype(BF16), b.astype(BF16), (((1,), (1,)), ((), ())), preferred_element_type=F32)


def _layer_norm(x, g, b):
    mu = jnp.mean(x, -1, keepdims=True)
    xc = x - mu
    var = jnp.mean(xc * xc, -1, keepdims=True)
    return xc * lax.rsqrt(var + LN_EPS) * g + b


def _silu(x):
    return x * jax.nn.sigmoid(x)


def _run_interleaved(chains):
    while chains:
        chains = [c for c in chains if next(c, "done") != "done"]


def _resident(shape):
    nd = len(shape)
    return pl.BlockSpec(shape, lambda *_: (0,) * nd, pipeline_mode=pl.Buffered(1))


def _ffn_ln_kernel(x_ref, wg_ref, wu_ref, wd_ref, g_ref, b_ref, o_ref, *, alpha):
    x = x_ref[...]
    xb = x.astype(BF16)
    gate = jnp.dot(xb, wg_ref[...], preferred_element_type=F32)
    up = jnp.dot(xb, wu_ref[...], preferred_element_type=F32)
    h = (_silu(gate) * up).astype(BF16)
    y = jnp.dot(h, wd_ref[...], preferred_element_type=F32)
    o_ref[...] = _layer_norm(alpha * x + 0.5 * y, g_ref[...], b_ref[...])


def _ffn_ln(x, wg, wu, wd, g, b, alpha):
    n, d = x.shape
    dff = wg.shape[1]
    tm = TOKEN_TILE
    return pl.pallas_call(
        functools.partial(_ffn_ln_kernel, alpha=alpha),
        out_shape=jax.ShapeDtypeStruct((n, d), F32),
        grid=(n // tm,),
        in_specs=[pl.BlockSpec((tm, d), lambda i: (i, 0)),
                  _resident((d, dff)), _resident((d, dff)), _resident((dff, d)),
                  _resident((1, d)), _resident((1, d))],
        out_specs=pl.BlockSpec((tm, d), lambda i: (i, 0)),
        compiler_params=_cparams("parallel"),
        name="ffn_ln",
    )(x, wg, wu, wd, g, b)


def _in_proj_kernel(x_ref, wa_ref, wr_ref, wt_ref, ba_ref, br_ref, bt_ref,
                    pa_ref, pr_ref, kat_ref, kbt_ref, kct_ref, gt_ref):
    xb = x_ref[...].astype(BF16)
    pa_ref[...] = jnp.dot(xb, wa_ref[...], preferred_element_type=F32) + ba_ref[...]
    pr_ref[...] = jnp.dot(xb, wr_ref[...], preferred_element_type=F32) + br_ref[...]
    tr = _dot_nt(wt_ref[...], xb) + bt_ref[...]
    kat_ref[...] = tr[0:W_GROUP].astype(BF16)
    kbt_ref[...] = tr[W_GROUP:2 * W_GROUP]
    kct_ref[...] = tr[2 * W_GROUP:3 * W_GROUP]
    gt_ref[...] = tr[3 * W_GROUP:3 * W_GROUP + N_GATES]


def _in_proj(x, wa, wr, wt, ba, br, bt):
    n, d = x.shape
    ca, cr, ct = wa.shape[1], wr.shape[1], wt.shape[0]
    tm = TOKEN_TILE
    tok = lambda w: pl.BlockSpec((tm, w), lambda i: (i, 0))
    trn = lambda r: pl.BlockSpec((r, tm), lambda i: (0, i))
    return pl.pallas_call(
        _in_proj_kernel,
        out_shape=(jax.ShapeDtypeStruct((n, ca), F32), jax.ShapeDtypeStruct((n, cr), F32),
                   jax.ShapeDtypeStruct((W_GROUP, n), BF16), jax.ShapeDtypeStruct((W_GROUP, n), F32),
                   jax.ShapeDtypeStruct((W_GROUP, n), F32), jax.ShapeDtypeStruct((N_GATES, n), F32)),
        grid=(n // tm,),
        in_specs=[tok(d), _resident((d, ca)), _resident((d, cr)), _resident((ct, d)),
                  _resident((1, ca)), _resident((1, cr)), _resident((ct, 1))],
        out_specs=(tok(ca), tok(cr), trn(W_GROUP), trn(W_GROUP), trn(W_GROUP), trn(N_GATES)),
        compiler_params=_cparams("parallel"),
        name="in_proj",
    )(x, wa, wr, wt, ba, br, bt)


def _head_rows(xt, h, pos):
    xh = xt[h * HEAD_DIM:(h + 1) * HEAD_DIM]
    z = jnp.zeros_like(xh)
    return jnp.concatenate([xh, z] if pos == 0 else [z, xh], axis=0)


def _col_bcast(eye_b, rows):
    hi = rows.astype(BF16)
    lo = (rows - hi.astype(F32)).astype(BF16)
    dn = (((1,), (1,)), ((), ()))
    return (lax.dot_general(eye_b, hi, dn, preferred_element_type=F32)
            + lax.dot_general(eye_b, lo, dn, preferred_element_type=F32))


def _rep_heads(rows4, n):
    return jnp.concatenate([jnp.broadcast_to(rows4[h:h + 1], (n, rows4.shape[1])) for h in range(N_HEADS)], axis=0)


def _seg_scan(x, seg_pos, seg_len, op, fill):
    s = 1
    while s < seg_len:
        x = op(x, jnp.where(seg_pos >= s, pltpu.roll(x, s, 1), fill))
        s *= 2
    return x


def _shift_rows(x, s, fill):
    if s % 8 == 0:
        return jnp.concatenate([jnp.full((s, x.shape[1]), fill, x.dtype), x[:x.shape[0] - s]], axis=0)
    row = lax.broadcasted_iota(jnp.int32, x.shape, 0)
    return jnp.where(row >= s, pltpu.roll(x, s, 0), fill)


def _mixer_kernel(pa_ref, pr_ref, kat_ref, kbt_ref, kct_ref, gt_ref, pastk_ref, pastv_ref, rel_ref,
                  cos_ref, sin_ref, cost_ref, sint_ref, dec_ref, qdec_ref, kdect_ref, sdec_ref,
                  ret0_ref, c0_ref, n0_ref, m0_ref,
                  cw_ref, cb_ref, wa_ref, ba_ref, wi_ref, bi_ref, lam_ref, buf0_ref, h0_ref,
                  y_ref, ret_out, c_out, n_out, m_out, h_out, buf_out,
                  bias_sc, spair, ppair, m_sc, win, h_sc, kpadt=None, vpad=None,
                  *, blk, seqs, n_steps, has_past):
    L = blk
    band = A_REACH + L
    n_past = A_REACH // L if not has_past else 0
    n_var = n_past + 1
    bb = pl.program_id(0)
    tb = pl.program_id(1)
    tail = CONV_W - 1
    if n_steps == 1:
        at_first = at_last = lambda f: f()
    else:
        at_first = pl.when(tb == 0)
        at_last = pl.when(tb == n_steps - 1)

    li = lax.broadcasted_iota(jnp.int32, (L, L), 0)
    lj = lax.broadcasted_iota(jnp.int32, (L, L), 1)
    causal_neg = jnp.where(lj <= li, 0.0, -jnp.inf)
    eye_b = jnp.where(li == lj, 1.0, 0.0).astype(BF16)
    lane_lo = lax.broadcasted_iota(jnp.int32, (L, SLAB), 1) < HEAD_DIM
    ones_slab = jnp.ones((L, SLAB), BF16)
    ri = lax.broadcasted_iota(jnp.int32, (SLAB, SLAB), 0) // HEAD_DIM
    rj = lax.broadcasted_iota(jnp.int32, (SLAB, SLAB), 1) // HEAD_DIM
    smask = jnp.where(ri == rj, 1.0, 0.0)
    pmask = jnp.concatenate([smask, smask], axis=1)
    gi = lax.broadcasted_iota(jnp.int32, (W_GROUP, W_GROUP), 0) // HEAD_DIM
    gj = lax.broadcasted_iota(jnp.int32, (W_GROUP, W_GROUP), 1) // HEAD_DIM
    gmat = jnp.where(gi == gj, 1.0 / HEAD_DIM, 0.0).astype(BF16)
    e64i = lax.broadcasted_iota(jnp.int32, (HEAD_DIM, HEAD_DIM), 0)
    e64j = lax.broadcasted_iota(jnp.int32, (HEAD_DIM, HEAD_DIM), 1)
    eye64 = e64i == e64j
    row_lo = lax.broadcasted_iota(jnp.int32, (SLAB, 1), 0) < HEAD_DIM

    @pl.when((bb == 0) & (tb == 0))
    def _():
        qq = lax.broadcasted_iota(jnp.int32, (L, band), 0)
        kk = lax.broadcasted_iota(jnp.int32, (L, band), 1)
        off = kk - (qq // CHUNK) * CHUNK
        for h in range(N_HEADS):
            row = jnp.broadcast_to(rel_ref[h], (L, ROLL_W))
            tile = pltpu.roll(row, ROLL_W - L, 1, stride=1, stride_axis=0)[:, :band] * LOG2E
            tile = jnp.where(off >= 0, jnp.where(off < A_REACH + CHUNK, tile, -jnp.inf), -jnp.inf)
            for v in range(n_var):
                first_col = (n_past - v) * L
                bias_sc[v * N_HEADS + h] = jnp.where(kk >= first_col, tile, -jnp.inf) if first_col > 0 else tile

    for s in range(seqs):
        @at_first
        def _():
            for p in range(N_PAIRS):
                sblk, cblk = [], []
                for hh in range(2):
                    h = 2 * p + hh
                    z = jnp.zeros((HEAD_DIM, HEAD_DIM), F32)
                    s_h = ret0_ref[s, h]
                    c_h = c0_ref[s, h]
                    n_row = n0_ref[s, h:h + 1, :]
                    n_col = jnp.sum(jnp.where(eye64, jnp.broadcast_to(n_row, eye64.shape), 0.0),
                                    axis=1, keepdims=True)
                    n_rep = jnp.broadcast_to(n_col, (HEAD_DIM, HEAD_DIM))
                    sblk.append(jnp.concatenate([s_h, z] if hh == 0 else [z, s_h], axis=1))
                    cblk.append(jnp.concatenate([c_h, z, n_rep, z] if hh == 0 else [z, c_h, z, n_rep], axis=1))
                spair[s, p] = jnp.concatenate(sblk, axis=0)
                ppair[s, p] = jnp.concatenate(cblk, axis=0)
            m_sc[s, 0:N_HEADS, :] = jnp.broadcast_to(m0_ref[s], (N_HEADS, m_sc.shape[2]))
            win[s, CONV_PAD - tail:CONV_PAD, :] = buf0_ref[s]
            h_sc[s] = h0_ref[s]
            if not has_past:
                vpad[s, 0:A_REACH, :] = jnp.zeros((A_REACH, W_GROUP), BF16)
                kpadt[s, 0:n_past * W_GROUP, :] = jnp.zeros((n_past * W_GROUP, L), BF16)

    g8 = gt_ref[...]
    seg_pos = lax.broadcasted_iota(jnp.int32, g8.shape, 1) % L
    b8 = _seg_scan(jax.nn.log_sigmoid(g8), seg_pos, L, jnp.add, 0.0)
    a8 = g8 - pltpu.roll(b8, N_HEADS, 0)
    cm8 = _seg_scan(a8, seg_pos, L, jnp.maximum, -jnp.inf)
    b8 = pltpu.roll(b8, N_HEADS, 0)

    heads = [(h // 2, h % 2, h) for h in range(N_HEADS)]
    slab = lambda p: slice(p * SLAB, (p + 1) * SLAB)
    seg_of = lambda s: slice(s * L, (s + 1) * L)
    col = lambda s, j: pr_ref[s, :, j * W_GROUP:(j + 1) * W_GROUP]

    def halves(per_head):
        return jnp.concatenate([jnp.where(lane_lo, per_head[2 * p], per_head[2 * p + 1]) for p in range(N_PAIRS)],
                               axis=1)


    def attention_chain(s):
        pa = pa_ref[s]
        kat = kat_ref[:, seg_of(s)]
        v_new = pa[:, 2 * W_GROUP:].astype(BF16)
        if has_past:
            kpast = pastk_ref[s].astype(BF16)
            vband = jnp.concatenate([pastv_ref[s].astype(BF16), v_new], axis=0)
            bias_at = lambda h: bias_sc[h]
        else:
            kpadt[s, pl.ds(pl.multiple_of((tb + n_past) * W_GROUP, W_GROUP), W_GROUP), :] = kat
            kwin = kpadt[s, pl.ds(pl.multiple_of(tb * W_GROUP, W_GROUP), (n_past + 1) * W_GROUP), :]
            start = pl.multiple_of(tb * L, L)
            vpad[s, pl.ds(A_REACH + start, L), :] = v_new
            vband = vpad[s, pl.ds(start, band), :]
            var = jnp.minimum(tb, n_var - 1) * N_HEADS
            bias_at = lambda h: bias_sc[var + h]
        qa = (pa[:, 0:W_GROUP] * (HEAD_DIM ** -0.5 * LOG2E)).astype(BF16)
        sc = {}
        for p, hh, h in heads:
            q_slab = qa[:, slab(p)]
            if has_past:
                k_slab = kpast[:, slab(p)]
                lane = lax.broadcasted_iota(jnp.int32, k_slab.shape, 1)
                lane_h = lane < HEAD_DIM if hh == 0 else lane >= HEAD_DIM
                parts = [_dot_nt(q_slab, jnp.where(lane_h, k_slab, jnp.zeros_like(k_slab))),
                         jnp.dot(q_slab, _head_rows(kat, h, hh), preferred_element_type=F32)]
            else:
                parts = [jnp.dot(q_slab, _head_rows(kwin[i * W_GROUP:(i + 1) * W_GROUP], h, hh),
                                 preferred_element_type=F32) for i in range(n_past + 1)]
            sc[h] = jnp.concatenate(parts, axis=1)
        yield
        pe, den = {}, {}
        for _, _, h in heads:
            sh = sc[h] + bias_at(h)
            pe[h] = jnp.exp2(sh - jnp.max(sh, axis=1, keepdims=True))
            den[h] = jnp.sum(pe[h], axis=1, keepdims=True)
        yield
        res = {h: _dot(pe[h], vband)[:, slab(p)] for p, _, h in heads}
        yield
        y_ref[s, :, 0:W_GROUP] = halves({h: res[h] / den[h] for _, _, h in heads})

    def retention_chain(s):
        qr = (col(s, PR_QB) * cos_ref[...] + col(s, PR_QBS) * sin_ref[...]).astype(BF16)
        kbt = kbt_ref[:, seg_of(s)]
        half = HEAD_DIM // 2
        kbt_sw = jnp.concatenate([kbt[h * HEAD_DIM + o:h * HEAD_DIM + o + half]
                                  for h in range(N_HEADS) for o in (half, 0)], axis=0)
        krt = (kbt * cost_ref[:, seg_of(s)] + kbt_sw * sint_ref[:, seg_of(s)]) * HEAD_DIM ** -0.5
        krt_b = krt.astype(BF16)
        kdt_b = (krt * kdect_ref[...]).astype(BF16)
        vb = col(s, PR_VB).astype(BF16)
        raw = {h: jnp.dot(qr[:, slab(p)], _head_rows(krt_b, h, hh), preferred_element_type=F32)
               for p, hh, h in heads}
        s0 = {p: spair[s, p] for p in range(N_PAIRS)}
        inter = {p: _dot(qr[:, slab(p)], s0[p]) for p in range(N_PAIRS)}
        add = {p: _dot(kdt_b[slab(p)], vb[:, slab(p)]) for p in range(N_PAIRS)}
        yield
        scores = {h: raw[h] * dec_ref[h] for _, _, h in heads}
        for p in range(N_PAIRS):
            spair[s, p] = s0[p] * sdec_ref[p] + add[p] * smask
        yield
        o = {h: _dot(scores[h], vb[:, slab(p)]) for p, _, h in heads}
        yield
        ob = halves(o) + jnp.concatenate([inter[p] for p in range(N_PAIRS)], axis=1) * qdec_ref[...]
        mu = _dot(ob, gmat)
        yield
        oc = ob - mu
        var = _dot(oc * oc, gmat)
        yield
        y_ref[s, :, W_GROUP:2 * W_GROUP] = oc * lax.rsqrt(var + LN_EPS) * _silu(col(s, PR_GB))

    def mlstm_chain(s):
        a4 = a8[0:N_HEADS, seg_of(s)]
        b4 = b8[0:N_HEADS, seg_of(s)]
        m0c = m_sc[s, 0:N_HEADS, 0:1]
        big_m = jnp.maximum(m0c, cm8[0:N_HEADS, seg_of(s)])
        m4 = b4 + big_m
        w0_4 = jnp.exp(m0c - big_m)
        m_last = big_m[:, L - 1:L]
        wrow4 = jnp.exp(a4 - m_last)
        kct = kct_ref[:, seg_of(s)] * HEAD_DIM ** -0.5
        kct_b = kct.astype(BF16)
        kw_b = (kct * _rep_heads(wrow4, HEAD_DIM)).astype(BF16)
        qc = col(s, PR_QC).astype(BF16)
        vc = col(s, PR_VC).astype(BF16)
        v_aug = {p: jnp.concatenate([vc[:, slab(p)], ones_slab], axis=1) for p in range(N_PAIRS)}
        p0 = {p: ppair[s, p] for p in range(N_PAIRS)}
        w0_bc = _col_bcast(eye_b, _rep_heads(w0_4, HEAD_DIM))
        floor_bc = _col_bcast(eye_b, _rep_heads(jnp.exp(-m4), HEAD_DIM))
        m_bc = {h: _col_bcast(eye_b, jnp.broadcast_to(big_m[h:h + 1], (L, L))) for _, _, h in heads}
        raw = {h: jnp.dot(qc[:, slab(p)], _head_rows(kct_b, h, hh), preferred_element_type=F32)
               for p, hh, h in heads}
        inter = {p: _dot(qc[:, slab(p)], p0[p]) for p in range(N_PAIRS)}
        add = {p: _dot(kw_b[slab(p)], v_aug[p]) for p in range(N_PAIRS)}
        yield
        qk = {h: raw[h] * jnp.exp(a4[h:h + 1] - m_bc[h] + causal_neg) for _, _, h in heads}
        for p in range(N_PAIRS):
            w_state = jnp.where(row_lo, w0_4[2 * p:2 * p + 1, L - 1:L], w0_4[2 * p + 1:2 * p + 2, L - 1:L])
            ppair[s, p] = p0[p] * w_state + add[p] * pmask
        m_sc[s, 0:N_HEADS, :] = jnp.broadcast_to(m4[:, L - 1:L], (N_HEADS, m_sc.shape[2]))
        yield
        res = {h: _dot(qk[h], v_aug[p]) for p, _, h in heads}
        yield
        hc = []
        for p in range(N_PAIRS):
            w0 = w0_bc[:, slab(p)]
            num = jnp.where(lane_lo, res[2 * p][:, 0:SLAB], res[2 * p + 1][:, 0:SLAB]) + inter[p][:, 0:SLAB] * w0
            den = jnp.where(lane_lo, res[2 * p][:, SLAB:], res[2 * p + 1][:, SLAB:]) + inter[p][:, SLAB:] * w0
            hc.append(num / jnp.maximum(jnp.abs(den), floor_bc[:, slab(p)]))
        hc = jnp.concatenate(hc, axis=1)
        mu = _dot(hc, gmat)
        yield
        hcc = hc - mu
        var = _dot(hcc * hcc, gmat)
        yield
        y_ref[s, :, 2 * W_GROUP:3 * W_GROUP] = hcc * lax.rsqrt(var + LN_EPS) * jax.nn.sigmoid(col(s, PR_OC))

    def rglru_chain(s):
        xd = col(s, PR_XD)
        win[s, CONV_PAD:CONV_PAD + L, :] = xd
        xc = cb_ref[...]
        for j in range(CONV_W):
            off = CONV_PAD - tail + j
            xc = xc + win[s, off:off + L, :] * cw_ref[j:j + 1, :]
        win[s, CONV_PAD - tail:CONV_PAD, :] = xd[L - tail:, :]
        r_pre = _dot(xc, wa_ref[...])
        i_pre = _dot(xc, wi_ref[...])
        yield
        r = jax.nn.sigmoid(r_pre + ba_ref[...])
        i = jax.nn.sigmoid(i_pre + bi_ref[...])
        neg_lam = -lam_ref[...]
        softplus = jnp.maximum(neg_lam, 0.0) + jnp.log1p(jnp.exp(-jnp.abs(neg_lam)))
        log_a = -LRU_C * r * softplus
        a = jnp.exp(log_a)
        u = jnp.sqrt(-jnp.tanh(log_a) * (jnp.exp(2.0 * log_a) + 1.0)) * (i * xc)
        st = 1
        while st < L:
            u = a * _shift_rows(u, st, 0.0) + u
            a = a * _shift_rows(a, st, 1.0)
            st *= 2
            if st in (8, 64):
                yield
        hseq = u + a * h_sc[s]
        h_sc[s] = hseq[L - 1:L, :]
        y_ref[s, :, 3 * W_GROUP:] = hseq * jax.nn.gelu(col(s, PR_YD))

    _run_interleaved([chain(s) for chain in (attention_chain, retention_chain, mlstm_chain, rglru_chain)
                      for s in range(seqs)])

    for s in range(seqs):
        @at_last
        def _():
            for p in range(N_PAIRS):
                sp = spair[s, p]
                pp = ppair[s, p]
                for hh in range(2):
                    h = 2 * p + hh
                    rs = slice(hh * HEAD_DIM, (hh + 1) * HEAD_DIM)
                    ret_out[s, h] = sp[rs, rs]
                    c_out[s, h] = pp[rs, rs]
                    n_rep = pp[rs, SLAB + hh * HEAD_DIM:SLAB + (hh + 1) * HEAD_DIM]
                    n_out[s, h:h + 1, :] = jnp.sum(jnp.where(eye64, n_rep, 0.0), axis=0, keepdims=True)
            m_out[s] = m_sc[s, 0:N_HEADS, 0:1]
            h_out[s] = h_sc[s]
            buf_out[s] = pr_ref[s, L - tail:, PR_XD * W_GROUP:(PR_XD + 1) * W_GROUP]


def _mixer_kernel_fresh(pa_ref, pr_ref, kat_ref, kbt_ref, kct_ref, gt_ref, *rest, **static):
    _mixer_kernel(pa_ref, pr_ref, kat_ref, kbt_ref, kct_ref, gt_ref, None, None, *rest, **static)


def _mixer_geometry(t, has_past):
    if has_past:
        assert t == CHUNK, "sequences with a carried cache are expected to be a single chunk"
        return t, MIX_SEQS
    assert t % MIX_BLOCK == 0 and A_REACH % MIX_BLOCK == 0
    return MIX_BLOCK, 1


def _mixer_tables(pos, blk, seqs):
    cos, sin = _rotary_tables(pos)
    return (cos, sin, jnp.tile(cos.T, (1, seqs)), jnp.tile(sin.T, (1, seqs))) + _retention_tables(blk)


def _mixer(pa, pr, kat, kbt, kct, gt, past, rel_rows, tables, ret0, c0, n0, m0, lw, buf0, h0):
    bsz, t, _ = pa.shape
    has_past = past is not None
    blk, seqs = _mixer_geometry(t, has_past)
    nb = t // blk
    assert bsz % seqs == 0
    band = A_REACH + blk
    n_var = 1 if has_past else A_REACH // blk + 1
    tail = CONV_W - 1

    tok3 = lambda w: pl.BlockSpec((seqs, blk, w), lambda b, c: (b, c, 0))
    trn = lambda r: pl.BlockSpec((r, seqs * blk), lambda b, c: (0, b * nb + c))
    per_seq = lambda *shape: pl.BlockSpec((seqs,) + shape, lambda b, c: (b,) + (0,) * len(shape))
    const = lambda *shape: pl.BlockSpec(shape, lambda b, c: (0,) * len(shape))
    past_specs, past_args = [], ()
    if has_past:
        layer = past[2]
        cache = pl.BlockSpec((None, seqs, A_REACH, W_GROUP), lambda b, c: (layer, b, 0, 0))
        past_specs, past_args = [cache, cache], past[:2]
    in_specs = [tok3(3 * W_GROUP), tok3(N_PR * W_GROUP), trn(W_GROUP), trn(W_GROUP), trn(W_GROUP), trn(N_GATES),
                *past_specs, const(N_HEADS, 1, ROLL_W),
                pl.BlockSpec((blk, W_GROUP), lambda b, c: (c, 0)), pl.BlockSpec((blk, W_GROUP), lambda b, c: (c, 0)),
                pl.BlockSpec((W_GROUP, seqs * blk), lambda b, c: (0, c)),
                pl.BlockSpec((W_GROUP, seqs * blk), lambda b, c: (0, c)),
                const(N_HEADS, blk, blk), const(blk, W_GROUP), const(W_GROUP, blk), const(N_PAIRS, SLAB, SLAB),
                per_seq(N_HEADS, HEAD_DIM, HEAD_DIM), per_seq(N_HEADS, HEAD_DIM, HEAD_DIM),
                per_seq(N_HEADS, HEAD_DIM), per_seq(N_HEADS, 1),
                const(CONV_W, W_GROUP), const(1, W_GROUP), const(W_GROUP, W_GROUP), const(1, W_GROUP),
                const(W_GROUP, W_GROUP), const(1, W_GROUP), const(1, W_GROUP),
                per_seq(tail, W_GROUP), per_seq(1, W_GROUP)]
    out_shape = (jax.ShapeDtypeStruct((bsz, t, 4 * W_GROUP), F32),
                 jax.ShapeDtypeStruct((bsz, N_HEADS, HEAD_DIM, HEAD_DIM), F32),
                 jax.ShapeDtypeStruct((bsz, N_HEADS, HEAD_DIM, HEAD_DIM), F32),
                 jax.ShapeDtypeStruct((bsz, N_HEADS, HEAD_DIM), F32),
                 jax.ShapeDtypeStruct((bsz, N_HEADS, 1), F32),
                 jax.ShapeDtypeStruct((bsz, 1, W_GROUP), F32),
                 jax.ShapeDtypeStruct((bsz, tail, W_GROUP), F32))
    out_specs = (tok3(4 * W_GROUP), per_seq(N_HEADS, HEAD_DIM, HEAD_DIM), per_seq(N_HEADS, HEAD_DIM, HEAD_DIM),
                 per_seq(N_HEADS, HEAD_DIM), per_seq(N_HEADS, 1), per_seq(1, W_GROUP), per_seq(tail, W_GROUP))
    scratch = [pltpu.VMEM((n_var * N_HEADS, blk, band), F32),
               pltpu.VMEM((seqs, N_PAIRS, SLAB, SLAB), F32), pltpu.VMEM((seqs, N_PAIRS, SLAB, 2 * SLAB), F32),
               pltpu.VMEM((seqs, 8, 128), F32), pltpu.VMEM((seqs, CONV_PAD + blk, W_GROUP), F32),
               pltpu.VMEM((seqs, 1, W_GROUP), F32)]
    if not has_past:
        scratch += [pltpu.VMEM((seqs, (nb + A_REACH // blk) * W_GROUP, blk), BF16),
                    pltpu.VMEM((seqs, A_REACH + t, W_GROUP), BF16)]
    body = _mixer_kernel if has_past else _mixer_kernel_fresh
    return pl.pallas_call(
        functools.partial(body, blk=blk, seqs=seqs, n_steps=nb, has_past=has_past),
        out_shape=out_shape,
        grid=(bsz // seqs, nb),
        in_specs=in_specs,
        out_specs=out_specs,
        scratch_shapes=scratch,
        compiler_params=_cparams("arbitrary", "arbitrary"),
        name="mixer",
    )(pa, pr, kat, kbt, kct, gt, *past_args, rel_rows, *tables,
      ret0, c0, n0, m0, lw['conv_w'], lw['conv_b'], lw['lru_wa'], lw['lru_ba'], lw['lru_wi'], lw['lru_bi'],
      lw['lru_lambda'], buf0, h0)


def _post_kernel(x_ref, y_ref, wo_ref, g1_ref, b1_ref, wq_ref, mk_ref, mv_ref, xo_ref,
                 g2_ref, b2_ref, o_ref, att_sc, *, alpha, seqs):
    x = x_ref[...]
    x2 = _layer_norm(alpha * x + _dot(y_ref[...], wo_ref[...]), g1_ref[...], b1_ref[...])
    q = _dot(x2, wq_ref[...]).astype(BF16)
    rows = x.shape[0] // seqs
    head = lambda h: slice(h * HEAD_DIM, (h + 1) * HEAD_DIM)

    def attend(s, h):
        rs = slice(s * rows, (s + 1) * rows)
        sc = _dot_nt(q[rs, head(h)], mk_ref[s, :, head(h)]) * HEAD_DIM ** -0.5
        yield
        p = jnp.exp(sc - jnp.max(sc, axis=1, keepdims=True))
        den = jnp.sum(p, axis=1, keepdims=True)
        yield
        o = _dot(p, mv_ref[s, :, head(h)])
        yield
        att_sc[rs, head(h)] = o / den

    chains = [attend(s, h) for s in range(seqs) for h in range(N_HEADS)]
    if seqs > 1:
        _run_interleaved(chains)
    else:
        for c in chains:
            _run_interleaved([c])
    o_ref[...] = _layer_norm(alpha * x2 + _dot(att_sc[...], xo_ref[...]), g2_ref[...], b2_ref[...])


def _post(x, y, wo, g1, b1, wq, mk, mv, layer, xo, g2, b2, alpha, seq_len):
    n, d = x.shape
    n_mem, dx = mk.shape[2], mk.shape[3]
    tm = TOKEN_TILE
    seqs = max(1, tm // seq_len)
    tiles_per_seq = max(1, seq_len // tm)
    tok = lambda w: pl.BlockSpec((tm, w), lambda i: (i, 0))
    mem = pl.BlockSpec((None, seqs, n_mem, dx), lambda i: (layer, i // tiles_per_seq, 0, 0))
    return pl.pallas_call(
        functools.partial(_post_kernel, alpha=alpha, seqs=seqs),
        out_shape=jax.ShapeDtypeStruct((n, d), F32),
        grid=(n // tm,),
        in_specs=[tok(d), tok(y.shape[1]),
                  _resident(wo.shape), _resident((1, d)), _resident((1, d)), _resident(wq.shape),
                  mem, mem, _resident(xo.shape), _resident((1, d)), _resident((1, d))],
        out_specs=tok(d),
        scratch_shapes=[pltpu.VMEM((tm, dx), F32)],
        compiler_params=_cparams("parallel"),
        name="post",
    )(x, y, wo, g1, b1, wq, mk, mv, xo, g2, b2)


def _mem_kv_kernel(x_ref, w_ref, o_ref):
    o_ref[...] = _dot(x_ref[...], w_ref[...])


def _mem_kv(mem, w):
    n, d = mem.shape
    tm = min(TOKEN_TILE, n)
    return pl.pallas_call(
        _mem_kv_kernel,
        out_shape=jax.ShapeDtypeStruct((n, w.shape[1]), F32),
        grid=(n // tm,),
        in_specs=[pl.BlockSpec((tm, d), lambda i: (i, 0)), _resident(w.shape)],
        out_specs=pl.BlockSpec((tm, w.shape[1]), lambda i: (i, 0)),
        compiler_params=_cparams("parallel"),
        name="mem_kv",
    )(mem, w)


def _rotary_tables(pos):
    half = HEAD_DIM // 2
    inv = jnp.exp(-jnp.log(10000.0) * jnp.arange(half, dtype=F32) / half)
    ang = pos.astype(F32)[:, None] * inv[None, :]
    cos, sin = jnp.cos(ang), jnp.sin(ang)
    cos_full = jnp.tile(jnp.concatenate([cos, cos], -1), (1, N_HEADS))
    sin_full = jnp.tile(jnp.concatenate([-sin, sin], -1), (1, N_HEADS))
    return cos_full, sin_full


def _retention_tables(blk):
    log_g = jnp.log1p(-jnp.exp2(-5.0 - jnp.arange(N_HEADS, dtype=F32)))
    idx = jnp.arange(blk, dtype=F32)
    diff = idx[:, None] - idx[None, :]
    dec = jnp.exp(jnp.where((diff >= 0)[None], diff[None] * log_g[:, None, None], -jnp.inf))
    q_dec = jnp.exp((idx[:, None] + 1.0) * log_g[None, :])
    k_dec = jnp.exp((blk - 1.0 - idx)[:, None] * log_g[None, :])
    s_dec = jnp.exp(blk * log_g)
    lanes = lambda tbl: jnp.repeat(tbl, HEAD_DIM, axis=1)
    pair = lambda p: jnp.kron(jnp.diag(s_dec[2 * p:2 * p + 2]), jnp.ones((HEAD_DIM, HEAD_DIM), F32))
    return dec, lanes(q_dec), lanes(k_dec).T, jnp.stack([pair(p) for p in range(N_PAIRS)])


def _rel_bias_rows(rel_bias, blk):
    idx = np.clip(A_REACH + blk - np.arange(ROLL_W), -REL_CLIP, REL_CLIP) + REL_CLIP
    return rel_bias[:, idx][:, None, :]


def _block_diag(w):
    h, c, _ = w.shape
    eye = jnp.eye(h, dtype=w.dtype)
    return (eye[:, None, :, None] * w[:, :, None, :]).reshape(h * c, h * c)


def _swap_perm():
    j = np.arange(W_GROUP)
    return (j // HEAD_DIM) * HEAD_DIM + (j % HEAD_DIM + HEAD_DIM // 2) % HEAD_DIM


def _layer(x, tables, mem, past, rec, lw, alpha):
    bsz, t, d = x.shape
    n = bsz * t
    ret_s, ml_c, ml_n, ml_m, lru_h, conv_buf = rec
    g, b = lw['ln_g'], lw['ln_b']

    x1 = _ffn_ln(x.reshape(n, d), lw['ffn1_gate'], lw['ffn1_up'], lw['ffn1_down'], g[0:1], b[0:1], alpha)
    pa, pr, kat, kbt, kct, gt = _in_proj(x1, lw['w_in_a'], lw['w_in_r'], lw['w_in_t'],
                                         lw['b_in_a'], lw['b_in_r'], lw['b_in_t'])
    pa = pa.reshape(bsz, t, 3 * W_GROUP)
    pr = pr.reshape(bsz, t, N_PR * W_GROUP)
    blk, _ = _mixer_geometry(t, past is not None)
    y, ret_new, c_new, n_new, m_new, h_new, conv_new = _mixer(
        pa, pr, kat, kbt, kct, gt, past, _rel_bias_rows(lw['rel_bias'], blk), tables,
        ret_s, ml_c, ml_n, ml_m[:, :, None], lw, conv_buf, lru_h[:, None, :])

    x3 = _post(x1, y.reshape(n, 4 * W_GROUP), lw['w_out'], g[1:2], b[1:2], lw['x_wq'],
               *mem, lw['x_wo'], g[2:3], b[2:3], alpha, t)
    x4 = _ffn_ln(x3, lw['ffn2_gate'], lw['ffn2_up'], lw['ffn2_down'], g[3:4], b[3:4], alpha)

    heads = lambda a: a.reshape(bsz, -1, N_HEADS, HEAD_DIM)
    keep = min(A_REACH, t)
    a_k = heads(pa[:, t - keep:, W_GROUP:2 * W_GROUP])
    a_v = heads(pa[:, t - keep:, 2 * W_GROUP:])
    state = (a_k, a_v, ret_new, c_new, n_new, m_new[:, :, 0], h_new[:, 0, :], conv_new)
    return x4.reshape(bsz, t, d), state


def kernel(x_prompt, x_sample, mem_prompt, cache_a_k, cache_a_v, state_ret, state_mlstm_c, state_mlstm_n,
           state_mlstm_m, state_lru_h, state_conv, cache_mem_k, cache_mem_v, ln_g, ln_b,
           ffn1_gate, ffn1_up, ffn1_down, ffn2_gate, ffn2_up, ffn2_down, w_in, b_in, a_rel_bias,
           conv_w, conv_b, lru_wa, lru_ba, lru_wi, lru_bi, lru_lambda, w_out, x_wq, x_wk, x_wv, x_wo):
    depth = ln_g.shape[0]
    alpha = (2.0 * depth) ** 0.25
    row = lambda v: v[None, :]
    grp = lambda a, j: a[..., j * W_GROUP:(j + 1) * W_GROUP]
    perm = _swap_perm()
    QA, KA, VA, QB, KB, VB, GB, QC, KC, VC, OC, XD, YD = range(13)

    def layer_weights(l):
        bf = lambda w: w[l].astype(BF16)
        w, bias = w_in[l], b_in[l]
        nat = lambda a: jnp.concatenate(
            [grp(a, QB), grp(a, QB)[..., perm], grp(a, VB), grp(a, GB), grp(a, QC), grp(a, VC), grp(a, OC),
             grp(a, XD), grp(a, YD)], axis=-1)
        trn = lambda a: jnp.concatenate([grp(a, KA), grp(a, KB), grp(a, KC), a[..., 13 * W_GROUP:]], axis=-1)
        return {'ln_g': ln_g[l], 'ln_b': ln_b[l],
                'ffn1_gate': bf(ffn1_gate), 'ffn1_up': bf(ffn1_up), 'ffn1_down': bf(ffn1_down),
                'ffn2_gate': bf(ffn2_gate), 'ffn2_up': bf(ffn2_up), 'ffn2_down': bf(ffn2_down),
                'w_in_a': w[:, :3 * W_GROUP].astype(BF16), 'w_in_r': nat(w).astype(BF16),
                'w_in_t': trn(w).T.astype(BF16),
                'b_in_a': row(bias[:3 * W_GROUP]), 'b_in_r': row(nat(bias)), 'b_in_t': trn(bias)[:, None],
                'rel_bias': a_rel_bias[l],
                'conv_w': conv_w[l], 'conv_b': row(conv_b[l]),
                'lru_wa': _block_diag(lru_wa[l]).astype(BF16), 'lru_ba': row(lru_ba[l]),
                'lru_wi': _block_diag(lru_wi[l]).astype(BF16), 'lru_bi': row(lru_bi[l]),
                'lru_lambda': row(lru_lambda[l]),
                'w_out': bf(w_out), 'x_wq': bf(x_wq), 'x_wo': bf(x_wo),
                'x_wkv': jnp.concatenate([x_wk[l], x_wv[l]], axis=1).astype(BF16)}

    weights = [layer_weights(l) for l in range(depth)]
    dx = x_wk.shape[2]

    bp, tp, d = x_prompt.shape
    n_mem = mem_prompt.shape[1]
    h = x_prompt
    prompt_states = []
    tables = _mixer_tables(jnp.arange(tp), *_mixer_geometry(tp, False))
    for l in range(depth):
        lw = weights[l]
        mkv = _mem_kv(mem_prompt.reshape(bp * n_mem, d), lw['x_wkv'])
        mk = mkv[:, :dx].reshape(bp, n_mem, dx)
        mv = mkv[:, dx:].reshape(bp, n_mem, dx)
        rec0 = (jnp.zeros((bp, N_HEADS, HEAD_DIM, HEAD_DIM), F32),
                jnp.zeros((bp, N_HEADS, HEAD_DIM, HEAD_DIM), F32),
                jnp.zeros((bp, N_HEADS, HEAD_DIM), F32),
                jnp.zeros((bp, N_HEADS), F32),
                jnp.zeros((bp, W_GROUP), F32),
                jnp.zeros((bp, CONV_W - 1, W_GROUP), F32))
        h, st = _layer(h, tables, (mk[None], mv[None], 0), None, rec0, lw, alpha)
        prompt_states.append(st + (mk.reshape(bp, n_mem, -1, HEAD_DIM), mv.reshape(bp, n_mem, -1, HEAD_DIM)))
    y_prompt = h
    prompt_out = [jnp.stack(f) for f in zip(*prompt_states)]

    bs, ts, _ = x_sample.shape
    h = x_sample
    sample_states = []
    tables = _mixer_tables(PAST_LEN + jnp.arange(ts), *_mixer_geometry(ts, True))
    past_k_all = cache_a_k.reshape(depth, bs, A_REACH, W_GROUP)
    past_v_all = cache_a_v.reshape(depth, bs, A_REACH, W_GROUP)
    mem_k_all = cache_mem_k.reshape(depth, bs, n_mem, dx)
    mem_v_all = cache_mem_v.reshape(depth, bs, n_mem, dx)
    for l in range(depth):
        rec = (state_ret[l], state_mlstm_c[l], state_mlstm_n[l], state_mlstm_m[l], state_lru_h[l], state_conv[l])
        h, st = _layer(h, tables, (mem_k_all, mem_v_all, l), (past_k_all, past_v_all, l), rec, weights[l], alpha)
        sample_states.append(st)
    y_sample = h
    sample_out = [jnp.stack(f) for f in zip(*sample_states)]

    return (y_prompt, y_sample, *prompt_out, *sample_out)
```

```python
import functools

import numpy as np
import jax
import jax.numpy as jnp
from jax import lax
from jax.experimental import pallas as pl
from jax.experimental.pallas import tpu as pltpu

F32 = jnp.float32
BF16 = jnp.bfloat16

CHUNK = 64
HEAD_DIM = 64
N_HEADS = 4
W_GROUP = N_HEADS * HEAD_DIM
SLAB = 2 * HEAD_DIM
N_PAIRS = N_HEADS // 2
A_BAND_CHUNKS = 8
A_REACH = A_BAND_CHUNKS * CHUNK
REL_CLIP = 128
CONV_W = 4
LRU_C = 8.0
LN_EPS = 1e-5
LOG2E = 1.4426950408889634
N_GATES = 2 * N_HEADS
PAST_LEN = 4096
ROLL_W = 1024
CONV_PAD = 8

V7X_VMEM_LIMIT_BYTES = 56 * 1024 * 1024
TOKEN_TILE = 512
MIX_BLOCK = 256
MIX_SEQS = 4
POST_SUBTILES = 2

PR_QB, PR_QBS, PR_VB, PR_GB, PR_QC, PR_VC, PR_OC, PR_XD, PR_YD = range(9)
N_PR = 9


def _cparams(*sem):
    return pltpu.CompilerParams(dimension_semantics=sem, vmem_limit_bytes=V7X_VMEM_LIMIT_BYTES)


def _dot(a, b):
    return jnp.dot(a.astype(BF16), b.astype(BF16), preferred_element_type=F32)


def _dot_nt(a, b):
    return lax.dot_general(a.astype(BF16), b.astype(BF16), (((1,), (1,)), ((), ())), preferred_element_type=F32)


def _layer_norm(x, g, b):
    mu = jnp.mean(x, -1, keepdims=True)
    xc = x - mu
    var = jnp.mean(xc * xc, -1, keepdims=True)
    return xc * lax.rsqrt(var + LN_EPS) * g + b


def _silu(x):
    return x * jax.nn.sigmoid(x)


def _lockstep(chains):
    while chains:
        chains = [c for c in chains if next(c, "done") != "done"]
        yield


def _run_interleaved(chains, lag=0):
    pending, live, rnd = list(chains), [], 0
    while pending or live:
        while pending and rnd >= lag * (len(chains) - len(pending)):
            live.append(pending.pop(0))
        live = [c for c in live if next(c, "done") != "done"]
        rnd += 1


def _resident(shape, layer=None):
    nd = len(shape)
    if layer is None:
        return pl.BlockSpec(shape, lambda *_: (0,) * nd, pipeline_mode=pl.Buffered(1))
    return pl.BlockSpec((None,) + tuple(shape), lambda *_: (layer,) + (0,) * nd, pipeline_mode=pl.Buffered(1))


def _ffn_ln_kernel(x_ref, wg_ref, wu_ref, wd_ref, g_ref, b_ref, o_ref, *, alpha):
    x = x_ref[...]
    xb = x.astype(BF16)
    gate = jnp.dot(xb, wg_ref[...], preferred_element_type=F32)
    up = jnp.dot(xb, wu_ref[...], preferred_element_type=F32)
    h = (_silu(gate) * up).astype(BF16)
    y = jnp.dot(h, wd_ref[...], preferred_element_type=F32)
    o_ref[...] = _layer_norm(alpha * x + 0.5 * y, g_ref[...], b_ref[...])


def _ffn_ln(x, wg, wu, wd, layer, g, b, alpha):
    n, d = x.shape
    dff = wg.shape[2]
    tm = TOKEN_TILE
    return pl.pallas_call(
        functools.partial(_ffn_ln_kernel, alpha=alpha),
        out_shape=jax.ShapeDtypeStruct((n, d), F32),
        grid=(n // tm,),
        in_specs=[pl.BlockSpec((tm, d), lambda i: (i, 0)),
                  _resident((d, dff), layer), _resident((d, dff), layer), _resident((dff, d), layer),
                  _resident((1, d)), _resident((1, d))],
        out_specs=pl.BlockSpec((tm, d), lambda i: (i, 0)),
        compiler_params=_cparams("parallel"),
        name="ffn_ln",
    )(x, wg, wu, wd, g, b)


def _in_proj_kernel(x_ref, wa_ref, wr_ref, wt_ref, ba_ref, br_ref, bt_ref,
                    pa_ref, pr_ref, kat_ref, kbt_ref, kct_ref, gt_ref):
    xb = x_ref[...].astype(BF16)
    pa_ref[...] = jnp.dot(xb, wa_ref[...], preferred_element_type=F32) + ba_ref[...]
    pr_ref[...] = jnp.dot(xb, wr_ref[...], preferred_element_type=F32) + br_ref[...]
    tr = _dot_nt(wt_ref[...], xb) + bt_ref[...]
    kat_ref[...] = tr[0:W_GROUP].astype(BF16)
    kbt_ref[...] = tr[W_GROUP:2 * W_GROUP]
    kct_ref[...] = tr[2 * W_GROUP:3 * W_GROUP]
    gt_ref[...] = tr[3 * W_GROUP:3 * W_GROUP + N_GATES]


def _in_proj(x, wa, wr, wt, ba, br, bt):
    n, d = x.shape
    ca, cr, ct = wa.shape[1], wr.shape[1], wt.shape[0]
    tm = TOKEN_TILE
    tok = lambda w: pl.BlockSpec((tm, w), lambda i: (i, 0))
    trn = lambda r: pl.BlockSpec((r, tm), lambda i: (0, i))
    return pl.pallas_call(
        _in_proj_kernel,
        out_shape=(jax.ShapeDtypeStruct((n, ca), F32), jax.ShapeDtypeStruct((n, cr), F32),
                   jax.ShapeDtypeStruct((W_GROUP, n), BF16), jax.ShapeDtypeStruct((W_GROUP, n), F32),
                   jax.ShapeDtypeStruct((W_GROUP, n), F32), jax.ShapeDtypeStruct((N_GATES, n), F32)),
        grid=(n // tm,),
        in_specs=[tok(d), _resident((d, ca)), _resident((d, cr)), _resident((ct, d)),
                  _resident((1, ca)), _resident((1, cr)), _resident((ct, 1))],
        out_specs=(tok(ca), tok(cr), trn(W_GROUP), trn(W_GROUP), trn(W_GROUP), trn(N_GATES)),
        compiler_params=_cparams("parallel"),
        name="in_proj",
    )(x, wa, wr, wt, ba, br, bt)


def _head_rows(xt, h, pos):
    xh = xt[h * HEAD_DIM:(h + 1) * HEAD_DIM]
    z = jnp.zeros_like(xh)
    return jnp.concatenate([xh, z] if pos == 0 else [z, xh], axis=0)


def _col_bcast(eye_b, rows):
    hi = rows.astype(BF16)
    lo = (rows - hi.astype(F32)).astype(BF16)
    dn = (((1,), (1,)), ((), ()))
    return (lax.dot_general(eye_b, hi, dn, preferred_element_type=F32)
            + lax.dot_general(eye_b, lo, dn, preferred_element_type=F32))


def _rep_heads(rows4, n):
    return jnp.concatenate([jnp.broadcast_to(rows4[h:h + 1], (n, rows4.shape[1])) for h in range(N_HEADS)], axis=0)


def _seg_scan(x, seg_pos, seg_len, op, fill):
    s = 1
    while s < seg_len:
        x = op(x, jnp.where(seg_pos >= s, pltpu.roll(x, s, 1), fill))
        s *= 2
    return x


def _shift_rows(x, s, fill):
    if s % 8 == 0:
        return jnp.concatenate([jnp.full((s, x.shape[1]), fill, x.dtype), x[:x.shape[0] - s]], axis=0)
    row = lax.broadcasted_iota(jnp.int32, x.shape, 0)
    return jnp.where(row >= s, pltpu.roll(x, s, 0), fill)


def _mixer_kernel(pa_ref, pr_ref, kat_ref, kbt_ref, kct_ref, gt_ref, pastk_ref, pastv_ref, rel_ref,
                  cos_ref, sin_ref, cost_ref, sint_ref, dec_ref, qdec_ref, kdect_ref, sdec_ref,
                  ret0_ref, c0_ref, n0_ref, m0_ref,
                  cw_ref, cb_ref, wa_ref, ba_ref, wi_ref, bi_ref, lam_ref, buf0_ref, h0_ref,
                  y_ref, ret_out, c_out, n_out, m_out, h_out, buf_out,
                  bias_sc, spair, ppair, m_sc, win, h_sc, kpadt=None, vpad=None,
                  *, blk, seqs, n_steps, has_past):
    L = blk
    band = A_REACH + L
    n_past = A_REACH // L if not has_past else 0
    n_var = n_past + 1
    bb = pl.program_id(0)
    tb = pl.program_id(1)
    tail = CONV_W - 1
    if n_steps == 1:
        at_first = at_last = lambda f: f()
    else:
        at_first = pl.when(tb == 0)
        at_last = pl.when(tb == n_steps - 1)

    li = lax.broadcasted_iota(jnp.int32, (L, L), 0)
    lj = lax.broadcasted_iota(jnp.int32, (L, L), 1)
    causal_neg = jnp.where(lj <= li, 0.0, -jnp.inf)
    eye_b = jnp.where(li == lj, 1.0, 0.0).astype(BF16)
    lane_lo = lax.broadcasted_iota(jnp.int32, (L, SLAB), 1) < HEAD_DIM
    ones_slab = jnp.ones((L, SLAB), BF16)
    ri = lax.broadcasted_iota(jnp.int32, (SLAB, SLAB), 0) // HEAD_DIM
    rj = lax.broadcasted_iota(jnp.int32, (SLAB, SLAB), 1) // HEAD_DIM
    smask = jnp.where(ri == rj, 1.0, 0.0)
    pmask = jnp.concatenate([smask, smask], axis=1)
    gi = lax.broadcasted_iota(jnp.int32, (W_GROUP, W_GROUP), 0) // HEAD_DIM
    gj = lax.broadcasted_iota(jnp.int32, (W_GROUP, W_GROUP), 1) // HEAD_DIM
    gmat = jnp.where(gi == gj, 1.0 / HEAD_DIM, 0.0).astype(BF16)
    e64i = lax.broadcasted_iota(jnp.int32, (HEAD_DIM, HEAD_DIM), 0)
    e64j = lax.broadcasted_iota(jnp.int32, (HEAD_DIM, HEAD_DIM), 1)
    eye64 = e64i == e64j
    row_lo = lax.broadcasted_iota(jnp.int32, (SLAB, 1), 0) < HEAD_DIM

    @pl.when((bb == 0) & (tb == 0))
    def _():
        qq = lax.broadcasted_iota(jnp.int32, (L, band), 0)
        kk = lax.broadcasted_iota(jnp.int32, (L, band), 1)
        off = kk - (qq // CHUNK) * CHUNK
        for h in range(N_HEADS):
            row = jnp.broadcast_to(rel_ref[h], (L, ROLL_W))
            tile = pltpu.roll(row, ROLL_W - L, 1, stride=1, stride_axis=0)[:, :band] * LOG2E
            tile = jnp.where(off >= 0, jnp.where(off < A_REACH + CHUNK, tile, -jnp.inf), -jnp.inf)
            for v in range(n_var):
                first_col = (n_past - v) * L
                bias_sc[v * N_HEADS + h] = jnp.where(kk >= first_col, tile, -jnp.inf) if first_col > 0 else tile

    for s in range(seqs):
        @at_first
        def _():
            for p in range(N_PAIRS):
                sblk, cblk = [], []
                for hh in range(2):
                    h = 2 * p + hh
                    z = jnp.zeros((HEAD_DIM, HEAD_DIM), F32)
                    s_h = ret0_ref[s, h]
                    c_h = c0_ref[s, h]
                    n_row = n0_ref[s, h:h + 1, :]
                    n_col = jnp.sum(jnp.where(eye64, jnp.broadcast_to(n_row, eye64.shape), 0.0),
                                    axis=1, keepdims=True)
                    n_rep = jnp.broadcast_to(n_col, (HEAD_DIM, HEAD_DIM))
                    sblk.append(jnp.concatenate([s_h, z] if hh == 0 else [z, s_h], axis=1))
                    cblk.append(jnp.concatenate([c_h, z, n_rep, z] if hh == 0 else [z, c_h, z, n_rep], axis=1))
                spair[s, p] = jnp.concatenate(sblk, axis=0)
                ppair[s, p] = jnp.concatenate(cblk, axis=0)
            m_sc[s, 0:N_HEADS, :] = jnp.broadcast_to(m0_ref[s], (N_HEADS, m_sc.shape[2]))
            win[s, CONV_PAD - tail:CONV_PAD, :] = buf0_ref[s]
            h_sc[s] = h0_ref[s]
            if not has_past:
                vpad[s, 0:A_REACH, :] = jnp.zeros((A_REACH, W_GROUP), BF16)
                kpadt[s, 0:n_past * W_GROUP, :] = jnp.zeros((n_past * W_GROUP, L), BF16)

    g8 = gt_ref[...]
    seg_pos = lax.broadcasted_iota(jnp.int32, g8.shape, 1) % L
    b8 = _seg_scan(jax.nn.log_sigmoid(g8), seg_pos, L, jnp.add, 0.0)
    a8 = g8 - pltpu.roll(b8, N_HEADS, 0)
    cm8 = _seg_scan(a8, seg_pos, L, jnp.maximum, -jnp.inf)
    b8 = pltpu.roll(b8, N_HEADS, 0)

    heads = [(h // 2, h % 2, h) for h in range(N_HEADS)]
    slab = lambda p: slice(p * SLAB, (p + 1) * SLAB)
    seg_of = lambda s: slice(s * L, (s + 1) * L)
    col = lambda s, j: pr_ref[s, :, j * W_GROUP:(j + 1) * W_GROUP]

    def halves(per_head):
        return jnp.concatenate([jnp.where(lane_lo, per_head[2 * p], per_head[2 * p + 1]) for p in range(N_PAIRS)],
                               axis=1)


    def attention_chain(s):
        pa = pa_ref[s]
        kat = kat_ref[:, seg_of(s)]
        v_new = pa[:, 2 * W_GROUP:].astype(BF16)
        if has_past:
            kpast = pastk_ref[s].astype(BF16)
            vband = jnp.concatenate([pastv_ref[s].astype(BF16), v_new], axis=0)
            bias_at = lambda h: bias_sc[h]
        else:
            kpadt[s, pl.ds(pl.multiple_of((tb + n_past) * W_GROUP, W_GROUP), W_GROUP), :] = kat
            kwin = kpadt[s, pl.ds(pl.multiple_of(tb * W_GROUP, W_GROUP), (n_past + 1) * W_GROUP), :]
            start = pl.multiple_of(tb * L, L)
            vpad[s, pl.ds(A_REACH + start, L), :] = v_new
            vband = vpad[s, pl.ds(start, band), :]
            var = jnp.minimum(tb, n_var - 1) * N_HEADS
            bias_at = lambda h: bias_sc[var + h]
        qa = (pa[:, 0:W_GROUP] * (HEAD_DIM ** -0.5 * LOG2E)).astype(BF16)
        sc = {}
        for p, hh, h in heads:
            q_slab = qa[:, slab(p)]
            if has_past:
                k_slab = kpast[:, slab(p)]
                lane = lax.broadcasted_iota(jnp.int32, k_slab.shape, 1)
                lane_h = lane < HEAD_DIM if hh == 0 else lane >= HEAD_DIM
                parts = [_dot_nt(q_slab, jnp.where(lane_h, k_slab, jnp.zeros_like(k_slab))),
                         jnp.dot(q_slab, _head_rows(kat, h, hh), preferred_element_type=F32)]
            else:
                parts = [jnp.dot(q_slab, _head_rows(kwin[i * W_GROUP:(i + 1) * W_GROUP], h, hh),
                                 preferred_element_type=F32) for i in range(n_past + 1)]
            sc[h] = jnp.concatenate(parts, axis=1)
        yield
        pe, den = {}, {}
        for _, _, h in heads:
            sh = sc[h] + bias_at(h)
            pe[h] = jnp.exp2(sh - jnp.max(sh, axis=1, keepdims=True))
            den[h] = jnp.sum(pe[h], axis=1, keepdims=True)
        yield
        res = {h: _dot(pe[h], vband)[:, slab(p)] for p, _, h in heads}
        yield
        y_ref[s, :, 0:W_GROUP] = halves({h: res[h] / den[h] for _, _, h in heads})

    def retention_chain(s):
        qr = (col(s, PR_QB) * cos_ref[...] + col(s, PR_QBS) * sin_ref[...]).astype(BF16)
        kbt = kbt_ref[:, seg_of(s)]
        half = HEAD_DIM // 2
        kbt_sw = jnp.concatenate([kbt[h * HEAD_DIM + o:h * HEAD_DIM + o + half]
                                  for h in range(N_HEADS) for o in (half, 0)], axis=0)
        krt = (kbt * cost_ref[:, seg_of(s)] + kbt_sw * sint_ref[:, seg_of(s)]) * HEAD_DIM ** -0.5
        krt_b = krt.astype(BF16)
        kdt_b = (krt * kdect_ref[...]).astype(BF16)
        vb = col(s, PR_VB).astype(BF16)
        raw = {h: jnp.dot(qr[:, slab(p)], _head_rows(krt_b, h, hh), preferred_element_type=F32)
               for p, hh, h in heads}
        s0 = {p: spair[s, p] for p in range(N_PAIRS)}
        inter = {p: _dot(qr[:, slab(p)], s0[p]) for p in range(N_PAIRS)}
        add = {p: _dot(kdt_b[slab(p)], vb[:, slab(p)]) for p in range(N_PAIRS)}
        yield
        scores = {h: raw[h] * dec_ref[h] for _, _, h in heads}
        for p in range(N_PAIRS):
            spair[s, p] = s0[p] * sdec_ref[p] + add[p] * smask
        yield
        o = {h: _dot(scores[h], vb[:, slab(p)]) for p, _, h in heads}
        yield
        ob = halves(o) + jnp.concatenate([inter[p] for p in range(N_PAIRS)], axis=1) * qdec_ref[...]
        mu = _dot(ob, gmat)
        yield
        oc = ob - mu
        var = _dot(oc * oc, gmat)
        yield
        y_ref[s, :, W_GROUP:2 * W_GROUP] = oc * lax.rsqrt(var + LN_EPS) * _silu(col(s, PR_GB))

    def mlstm_chain(s):
        a4 = a8[0:N_HEADS, seg_of(s)]
        b4 = b8[0:N_HEADS, seg_of(s)]
        m0c = m_sc[s, 0:N_HEADS, 0:1]
        big_m = jnp.maximum(m0c, cm8[0:N_HEADS, seg_of(s)])
        m4 = b4 + big_m
        w0_4 = jnp.exp(m0c - big_m)
        m_last = big_m[:, L - 1:L]
        wrow4 = jnp.exp(a4 - m_last)
        kct = kct_ref[:, seg_of(s)] * HEAD_DIM ** -0.5
        kct_b = kct.astype(BF16)
        kw_b = (kct * _rep_heads(wrow4, HEAD_DIM)).astype(BF16)
        qc = col(s, PR_QC).astype(BF16)
        vc = col(s, PR_VC).astype(BF16)
        v_aug = {p: jnp.concatenate([vc[:, slab(p)], ones_slab], axis=1) for p in range(N_PAIRS)}
        p0 = {p: ppair[s, p] for p in range(N_PAIRS)}
        w0_bc = _col_bcast(eye_b, _rep_heads(w0_4, HEAD_DIM))
        floor_bc = _col_bcast(eye_b, _rep_heads(jnp.exp(-m4), HEAD_DIM))
        m_bc = {h: _col_bcast(eye_b, jnp.broadcast_to(big_m[h:h + 1], (L, L))) for _, _, h in heads}
        raw = {h: jnp.dot(qc[:, slab(p)], _head_rows(kct_b, h, hh), preferred_element_type=F32)
               for p, hh, h in heads}
        inter = {p: _dot(qc[:, slab(p)], p0[p]) for p in range(N_PAIRS)}
        add = {p: _dot(kw_b[slab(p)], v_aug[p]) for p in range(N_PAIRS)}
        yield
        qk = {h: raw[h] * jnp.exp(a4[h:h + 1] - m_bc[h] + causal_neg) for _, _, h in heads}
        for p in range(N_PAIRS):
            w_state = jnp.where(row_lo, w0_4[2 * p:2 * p + 1, L - 1:L], w0_4[2 * p + 1:2 * p + 2, L - 1:L])
            ppair[s, p] = p0[p] * w_state + add[p] * pmask
        m_sc[s, 0:N_HEADS, :] = jnp.broadcast_to(m4[:, L - 1:L], (N_HEADS, m_sc.shape[2]))
        yield
        res = {h: _dot(qk[h], v_aug[p]) for p, _, h in heads}
        yield
        hc = []
        for p in range(N_PAIRS):
            w0 = w0_bc[:, slab(p)]
            num = jnp.where(lane_lo, res[2 * p][:, 0:SLAB], res[2 * p + 1][:, 0:SLAB]) + inter[p][:, 0:SLAB] * w0
            den = jnp.where(lane_lo, res[2 * p][:, SLAB:], res[2 * p + 1][:, SLAB:]) + inter[p][:, SLAB:] * w0
            hc.append(num / jnp.maximum(jnp.abs(den), floor_bc[:, slab(p)]))
        hc = jnp.concatenate(hc, axis=1)
        mu = _dot(hc, gmat)
        yield
        hcc = hc - mu
        var = _dot(hcc * hcc, gmat)
        yield
        y_ref[s, :, 2 * W_GROUP:3 * W_GROUP] = hcc * lax.rsqrt(var + LN_EPS) * jax.nn.sigmoid(col(s, PR_OC))

    def rglru_chain(s):
        xd = col(s, PR_XD)
        win[s, CONV_PAD:CONV_PAD + L, :] = xd
        xc = cb_ref[...]
        for j in range(CONV_W):
            off = CONV_PAD - tail + j
            xc = xc + win[s, off:off + L, :] * cw_ref[j:j + 1, :]
        win[s, CONV_PAD - tail:CONV_PAD, :] = xd[L - tail:, :]
        r_pre = _dot(xc, wa_ref[...])
        i_pre = _dot(xc, wi_ref[...])
        yield
        r = jax.nn.sigmoid(r_pre + ba_ref[...])
        i = jax.nn.sigmoid(i_pre + bi_ref[...])
        neg_lam = -lam_ref[...]
        softplus = jnp.maximum(neg_lam, 0.0) + jnp.log1p(jnp.exp(-jnp.abs(neg_lam)))
        log_a = -LRU_C * r * softplus
        a = jnp.exp(log_a)
        u = jnp.sqrt(-jnp.tanh(log_a) * (jnp.exp(2.0 * log_a) + 1.0)) * (i * xc)
        st = 1
        while st < L:
            u = a * _shift_rows(u, st, 0.0) + u
            a = a * _shift_rows(a, st, 1.0)
            st *= 2
            if st in (8, 64):
                yield
        hseq = u + a * h_sc[s]
        h_sc[s] = hseq[L - 1:L, :]
        y_ref[s, :, 3 * W_GROUP:] = hseq * jax.nn.gelu(col(s, PR_YD))

    _run_interleaved([chain(s) for chain in (attention_chain, retention_chain, mlstm_chain, rglru_chain)
                      for s in range(seqs)], lag=1 if seqs == 1 else 0)

    for s in range(seqs):
        @at_last
        def _():
            for p in range(N_PAIRS):
                sp = spair[s, p]
                pp = ppair[s, p]
                for hh in range(2):
                    h = 2 * p + hh
                    rs = slice(hh * HEAD_DIM, (hh + 1) * HEAD_DIM)
                    ret_out[s, h] = sp[rs, rs]
                    c_out[s, h] = pp[rs, rs]
                    n_rep = pp[rs, SLAB + hh * HEAD_DIM:SLAB + (hh + 1) * HEAD_DIM]
                    n_out[s, h:h + 1, :] = jnp.sum(jnp.where(eye64, n_rep, 0.0), axis=0, keepdims=True)
            m_out[s] = m_sc[s, 0:N_HEADS, 0:1]
            h_out[s] = h_sc[s]
            buf_out[s] = pr_ref[s, L - tail:, PR_XD * W_GROUP:(PR_XD + 1) * W_GROUP]


def _mixer_kernel_fresh(pa_ref, pr_ref, kat_ref, kbt_ref, kct_ref, gt_ref, *rest, **static):
    _mixer_kernel(pa_ref, pr_ref, kat_ref, kbt_ref, kct_ref, gt_ref, None, None, *rest, **static)


def _mixer_geometry(t, has_past):
    if has_past:
        assert t == CHUNK, "sequences with a carried cache are expected to be a single chunk"
        return t, MIX_SEQS
    assert t % MIX_BLOCK == 0 and A_REACH % MIX_BLOCK == 0
    return MIX_BLOCK, 1


def _mixer_tables(pos, blk, seqs):
    cos, sin = _rotary_tables(pos)
    return (cos, sin, jnp.tile(cos.T, (1, seqs)), jnp.tile(sin.T, (1, seqs))) + _retention_tables(blk)


def _mixer(pa, pr, kat, kbt, kct, gt, past, rel_rows, tables, ret0, c0, n0, m0, lw, buf0, h0):
    bsz, t, _ = pa.shape
    has_past = past is not None
    blk, seqs = _mixer_geometry(t, has_past)
    nb = t // blk
    assert bsz % seqs == 0
    band = A_REACH + blk
    n_var = 1 if has_past else A_REACH // blk + 1
    tail = CONV_W - 1

    tok3 = lambda w: pl.BlockSpec((seqs, blk, w), lambda b, c: (b, c, 0))
    trn = lambda r: pl.BlockSpec((r, seqs * blk), lambda b, c: (0, b * nb + c))
    per_seq = lambda *shape: pl.BlockSpec((seqs,) + shape, lambda b, c: (b,) + (0,) * len(shape))
    const = lambda *shape: pl.BlockSpec(shape, lambda b, c: (0,) * len(shape))
    past_specs, past_args = [], ()
    if has_past:
        layer = past[2]
        cache = pl.BlockSpec((None, seqs, A_REACH, W_GROUP), lambda b, c: (layer, b, 0, 0))
        past_specs, past_args = [cache, cache], past[:2]
    in_specs = [tok3(3 * W_GROUP), tok3(N_PR * W_GROUP), trn(W_GROUP), trn(W_GROUP), trn(W_GROUP), trn(N_GATES),
                *past_specs, const(N_HEADS, 1, ROLL_W),
                pl.BlockSpec((blk, W_GROUP), lambda b, c: (c, 0)), pl.BlockSpec((blk, W_GROUP), lambda b, c: (c, 0)),
                pl.BlockSpec((W_GROUP, seqs * blk), lambda b, c: (0, c)),
                pl.BlockSpec((W_GROUP, seqs * blk), lambda b, c: (0, c)),
                const(N_HEADS, blk, blk), const(blk, W_GROUP), const(W_GROUP, blk), const(N_PAIRS, SLAB, SLAB),
                per_seq(N_HEADS, HEAD_DIM, HEAD_DIM), per_seq(N_HEADS, HEAD_DIM, HEAD_DIM),
                per_seq(N_HEADS, HEAD_DIM), per_seq(N_HEADS, 1),
                const(CONV_W, W_GROUP), const(1, W_GROUP), const(W_GROUP, W_GROUP), const(1, W_GROUP),
                const(W_GROUP, W_GROUP), const(1, W_GROUP), const(1, W_GROUP),
                per_seq(tail, W_GROUP), per_seq(1, W_GROUP)]
    out_shape = (jax.ShapeDtypeStruct((bsz, t, 4 * W_GROUP), F32),
                 jax.ShapeDtypeStruct((bsz, N_HEADS, HEAD_DIM, HEAD_DIM), F32),
                 jax.ShapeDtypeStruct((bsz, N_HEADS, HEAD_DIM, HEAD_DIM), F32),
                 jax.ShapeDtypeStruct((bsz, N_HEADS, HEAD_DIM), F32),
                 jax.ShapeDtypeStruct((bsz, N_HEADS, 1), F32),
                 jax.ShapeDtypeStruct((bsz, 1, W_GROUP), F32),
                 jax.ShapeDtypeStruct((bsz, tail, W_GROUP), F32))
    out_specs = (tok3(4 * W_GROUP), per_seq(N_HEADS, HEAD_DIM, HEAD_DIM), per_seq(N_HEADS, HEAD_DIM, HEAD_DIM),
                 per_seq(N_HEADS, HEAD_DIM), per_seq(N_HEADS, 1), per_seq(1, W_GROUP), per_seq(tail, W_GROUP))
    scratch = [pltpu.VMEM((n_var * N_HEADS, blk, band), F32),
               pltpu.VMEM((seqs, N_PAIRS, SLAB, SLAB), F32), pltpu.VMEM((seqs, N_PAIRS, SLAB, 2 * SLAB), F32),
               pltpu.VMEM((seqs, 8, 128), F32), pltpu.VMEM((seqs, CONV_PAD + blk, W_GROUP), F32),
               pltpu.VMEM((seqs, 1, W_GROUP), F32)]
    if not has_past:
        scratch += [pltpu.VMEM((seqs, (nb + A_REACH // blk) * W_GROUP, blk), BF16),
                    pltpu.VMEM((seqs, A_REACH + t, W_GROUP), BF16)]
    body = _mixer_kernel if has_past else _mixer_kernel_fresh
    return pl.pallas_call(
        functools.partial(body, blk=blk, seqs=seqs, n_steps=nb, has_past=has_past),
        out_shape=out_shape,
        grid=(bsz // seqs, nb),
        in_specs=in_specs,
        out_specs=out_specs,
        scratch_shapes=scratch,
        compiler_params=_cparams("arbitrary", "arbitrary"),
        name="mixer",
    )(pa, pr, kat, kbt, kct, gt, *past_args, rel_rows, *tables,
      ret0, c0, n0, m0, lw['conv_w'], lw['conv_b'], lw['lru_wa'], lw['lru_ba'], lw['lru_wi'], lw['lru_bi'],
      lw['lru_lambda'], buf0, h0)


def _post_kernel(x_ref, y_ref, wo_ref, g1_ref, b1_ref, wq_ref, mk_ref, mv_ref, xo_ref,
                 g2_ref, b2_ref, o_ref, att_sc, *, alpha, seqs, subtiles):
    rows_sub = x_ref.shape[0] // subtiles
    rows = rows_sub // seqs
    head = lambda h: slice(h * HEAD_DIM, (h + 1) * HEAD_DIM)

    def subtile_chain(i):
        rs = slice(i * rows_sub, (i + 1) * rows_sub)
        x = x_ref[rs, :]
        mix = _dot(y_ref[rs, :], wo_ref[...])
        yield
        x2 = _layer_norm(alpha * x + mix, g1_ref[...], b1_ref[...])
        yield
        q = _dot(x2, wq_ref[...]).astype(BF16)
        yield

        def attend(s, h):
            qs = slice(s * rows, (s + 1) * rows)
            sc = _dot_nt(q[qs, head(h)], mk_ref[s, :, head(h)]) * HEAD_DIM ** -0.5
            yield
            p = jnp.exp(sc - jnp.max(sc, axis=1, keepdims=True))
            den = jnp.sum(p, axis=1, keepdims=True)
            yield
            o = _dot(p, mv_ref[s, :, head(h)])
            yield
            att_sc[i * rows_sub + s * rows:i * rows_sub + (s + 1) * rows, head(h)] = o / den

        yield from _lockstep([attend(s, h) for s in range(seqs) for h in range(N_HEADS)])
        out = _dot(att_sc[rs, :], xo_ref[...])
        yield
        o_ref[rs, :] = _layer_norm(alpha * x2 + out, g2_ref[...], b2_ref[...])

    _run_interleaved([subtile_chain(i) for i in range(subtiles)], lag=2)


def _post(x, y, wo, wq, xo, wlayer, g1, b1, g2, b2, mk, mv, layer, alpha, seq_len):
    n, d = x.shape
    n_mem, dx = mk.shape[2], mk.shape[3]
    if seq_len >= POST_SUBTILES * TOKEN_TILE:
        subtiles, seqs = POST_SUBTILES, 1
        tm = subtiles * TOKEN_TILE
        mem = pl.BlockSpec((None, 1, n_mem, dx), lambda i: (layer, i // (seq_len // tm), 0, 0))
    else:
        tm = TOKEN_TILE
        subtiles, seqs = 1, tm // seq_len
        mem = pl.BlockSpec((None, seqs, n_mem, dx), lambda i: (layer, i, 0, 0))
    assert n % tm == 0 and tm % (subtiles * seqs) == 0
    tok = lambda w: pl.BlockSpec((tm, w), lambda i: (i, 0))
    return pl.pallas_call(
        functools.partial(_post_kernel, alpha=alpha, seqs=seqs, subtiles=subtiles),
        out_shape=jax.ShapeDtypeStruct((n, d), F32),
        grid=(n // tm,),
        in_specs=[tok(d), tok(y.shape[1]),
                  _resident(wo.shape[1:], wlayer), _resident((1, d)), _resident((1, d)),
                  _resident(wq.shape[1:], wlayer), mem, mem, _resident(xo.shape[1:], wlayer),
                  _resident((1, d)), _resident((1, d))],
        out_specs=tok(d),
        scratch_shapes=[pltpu.VMEM((tm, dx), F32)],
        compiler_params=_cparams("parallel"),
        name="post",
    )(x, y, wo, g1, b1, wq, mk, mv, xo, g2, b2)


def _mem_kv_kernel(x_ref, w_ref, o_ref):
    o_ref[...] = _dot(x_ref[...], w_ref[...])


def _mem_kv(mem, w):
    n, d = mem.shape
    tm = min(TOKEN_TILE, n)
    return pl.pallas_call(
        _mem_kv_kernel,
        out_shape=jax.ShapeDtypeStruct((n, w.shape[1]), F32),
        grid=(n // tm,),
        in_specs=[pl.BlockSpec((tm, d), lambda i: (i, 0)), _resident(w.shape)],
        out_specs=pl.BlockSpec((tm, w.shape[1]), lambda i: (i, 0)),
        compiler_params=_cparams("parallel"),
        name="mem_kv",
    )(mem, w)


def _rotary_tables(pos):
    half = HEAD_DIM // 2
    inv = jnp.exp(-jnp.log(10000.0) * jnp.arange(half, dtype=F32) / half)
    ang = pos.astype(F32)[:, None] * inv[None, :]
    cos, sin = jnp.cos(ang), jnp.sin(ang)
    cos_full = jnp.tile(jnp.concatenate([cos, cos], -1), (1, N_HEADS))
    sin_full = jnp.tile(jnp.concatenate([-sin, sin], -1), (1, N_HEADS))
    return cos_full, sin_full


def _retention_tables(blk):
    log_g = jnp.log1p(-jnp.exp2(-5.0 - jnp.arange(N_HEADS, dtype=F32)))
    idx = jnp.arange(blk, dtype=F32)
    diff = idx[:, None] - idx[None, :]
    dec = jnp.exp(jnp.where((diff >= 0)[None], diff[None] * log_g[:, None, None], -jnp.inf))
    q_dec = jnp.exp((idx[:, None] + 1.0) * log_g[None, :])
    k_dec = jnp.exp((blk - 1.0 - idx)[:, None] * log_g[None, :])
    s_dec = jnp.exp(blk * log_g)
    lanes = lambda tbl: jnp.repeat(tbl, HEAD_DIM, axis=1)
    pair = lambda p: jnp.kron(jnp.diag(s_dec[2 * p:2 * p + 2]), jnp.ones((HEAD_DIM, HEAD_DIM), F32))
    return dec, lanes(q_dec), lanes(k_dec).T, jnp.stack([pair(p) for p in range(N_PAIRS)])


def _rel_bias_rows(rel_bias, blk):
    idx = np.clip(A_REACH + blk - np.arange(ROLL_W), -REL_CLIP, REL_CLIP) + REL_CLIP
    return rel_bias[:, idx][:, None, :]


def _block_diag(w):
    h, c, _ = w.shape
    eye = jnp.eye(h, dtype=w.dtype)
    return (eye[:, None, :, None] * w[:, :, None, :]).reshape(h * c, h * c)


def _swap_perm():
    j = np.arange(W_GROUP)
    return (j // HEAD_DIM) * HEAD_DIM + (j % HEAD_DIM + HEAD_DIM // 2) % HEAD_DIM


def _layer(x, tables, mem, past, rec, lw, alpha):
    bsz, t, d = x.shape
    n = bsz * t
    ret_s, ml_c, ml_n, ml_m, lru_h, conv_buf = rec
    g, b = lw['ln_g'], lw['ln_b']

    wl = lw['layer']
    x1 = _ffn_ln(x.reshape(n, d), lw['ffn1_gate'], lw['ffn1_up'], lw['ffn1_down'], wl, g[0:1], b[0:1], alpha)
    pa, pr, kat, kbt, kct, gt = _in_proj(x1, lw['w_in_a'], lw['w_in_r'], lw['w_in_t'],
                                         lw['b_in_a'], lw['b_in_r'], lw['b_in_t'])
    pa = pa.reshape(bsz, t, 3 * W_GROUP)
    pr = pr.reshape(bsz, t, N_PR * W_GROUP)
    blk, _ = _mixer_geometry(t, past is not None)
    y, ret_new, c_new, n_new, m_new, h_new, conv_new = _mixer(
        pa, pr, kat, kbt, kct, gt, past, _rel_bias_rows(lw['rel_bias'], blk), tables,
        ret_s, ml_c, ml_n, ml_m[:, :, None], lw, conv_buf, lru_h[:, None, :])

    x3 = _post(x1, y.reshape(n, 4 * W_GROUP), lw['w_out'], lw['x_wq'], lw['x_wo'], wl,
               g[1:2], b[1:2], g[2:3], b[2:3], *mem, alpha, t)
    x4 = _ffn_ln(x3, lw['ffn2_gate'], lw['ffn2_up'], lw['ffn2_down'], wl, g[3:4], b[3:4], alpha)

    heads = lambda a: a.reshape(bsz, -1, N_HEADS, HEAD_DIM)
    keep = min(A_REACH, t)
    a_k = heads(pa[:, t - keep:, W_GROUP:2 * W_GROUP])
    a_v = heads(pa[:, t - keep:, 2 * W_GROUP:])
    state = (a_k, a_v, ret_new, c_new, n_new, m_new[:, :, 0], h_new[:, 0, :], conv_new)
    return x4.reshape(bsz, t, d), state


def kernel(x_prompt, x_sample, mem_prompt, cache_a_k, cache_a_v, state_ret, state_mlstm_c, state_mlstm_n,
           state_mlstm_m, state_lru_h, state_conv, cache_mem_k, cache_mem_v, ln_g, ln_b,
           ffn1_gate, ffn1_up, ffn1_down, ffn2_gate, ffn2_up, ffn2_down, w_in, b_in, a_rel_bias,
           conv_w, conv_b, lru_wa, lru_ba, lru_wi, lru_bi, lru_lambda, w_out, x_wq, x_wk, x_wv, x_wo):
    depth = ln_g.shape[0]
    alpha = (2.0 * depth) ** 0.25
    row = lambda v: v[None, :]
    grp = lambda a, j: a[..., j * W_GROUP:(j + 1) * W_GROUP]
    perm = _swap_perm()
    QA, KA, VA, QB, KB, VB, GB, QC, KC, VC, OC, XD, YD = range(13)

    def layer_weights(l):
        w, bias = w_in[l], b_in[l]
        nat = lambda a: jnp.concatenate(
            [grp(a, QB), grp(a, QB)[..., perm], grp(a, VB), grp(a, GB), grp(a, QC), grp(a, VC), grp(a, OC),
             grp(a, XD), grp(a, YD)], axis=-1)
        trn = lambda a: jnp.concatenate([grp(a, KA), grp(a, KB), grp(a, KC), a[..., 13 * W_GROUP:]], axis=-1)
        return {'ln_g': ln_g[l], 'ln_b': ln_b[l],
                'layer': l, **stacks,
                'w_in_a': w[:, :3 * W_GROUP].astype(BF16), 'w_in_r': nat(w).astype(BF16),
                'w_in_t': trn(w).T.astype(BF16),
                'b_in_a': row(bias[:3 * W_GROUP]), 'b_in_r': row(nat(bias)), 'b_in_t': trn(bias)[:, None],
                'rel_bias': a_rel_bias[l],
                'conv_w': conv_w[l], 'conv_b': row(conv_b[l]),
                'lru_wa': _block_diag(lru_wa[l]).astype(BF16), 'lru_ba': row(lru_ba[l]),
                'lru_wi': _block_diag(lru_wi[l]).astype(BF16), 'lru_bi': row(lru_bi[l]),
                'lru_lambda': row(lru_lambda[l]),
                'x_wkv': jnp.concatenate([x_wk[l], x_wv[l]], axis=1).astype(BF16)}

    stacks = {name: w.astype(BF16) for name, w in (
        ('ffn1_gate', ffn1_gate), ('ffn1_up', ffn1_up), ('ffn1_down', ffn1_down),
        ('ffn2_gate', ffn2_gate), ('ffn2_up', ffn2_up), ('ffn2_down', ffn2_down),
        ('w_out', w_out), ('x_wq', x_wq), ('x_wo', x_wo))}
    weights = [layer_weights(l) for l in range(depth)]
    dx = x_wk.shape[2]

    bp, tp, d = x_prompt.shape
    n_mem = mem_prompt.shape[1]
    h = x_prompt
    prompt_states = []
    tables = _mixer_tables(jnp.arange(tp), *_mixer_geometry(tp, False))
    for l in range(depth):
        lw = weights[l]
        mkv = _mem_kv(mem_prompt.reshape(bp * n_mem, d), lw['x_wkv'])
        mk = mkv[:, :dx].reshape(bp, n_mem, dx)
        mv = mkv[:, dx:].reshape(bp, n_mem, dx)
        rec0 = (jnp.zeros((bp, N_HEADS, HEAD_DIM, HEAD_DIM), F32),
                jnp.zeros((bp, N_HEADS, HEAD_DIM, HEAD_DIM), F32),
                jnp.zeros((bp, N_HEADS, HEAD_DIM), F32),
                jnp.zeros((bp, N_HEADS), F32),
                jnp.zeros((bp, W_GROUP), F32),
                jnp.zeros((bp, CONV_W - 1, W_GROUP), F32))
        h, st = _layer(h, tables, (mk[None], mv[None], 0), None, rec0, lw, alpha)
        prompt_states.append(st + (mk.reshape(bp, n_mem, -1, HEAD_DIM), mv.reshape(bp, n_mem, -1, HEAD_DIM)))
    y_prompt = h
    prompt_out = [jnp.stack(f) for f in zip(*prompt_states)]

    bs, ts, _ = x_sample.shape
    h = x_sample
    sample_states = []
    tables = _mixer_tables(PAST_LEN + jnp.arange(ts), *_mixer_geometry(ts, True))
    past_k_all = cache_a_k.reshape(depth, bs, A_REACH, W_GROUP)
    past_v_all = cache_a_v.reshape(depth, bs, A_REACH, W_GROUP)
    mem_k_all = cache_mem_k.reshape(depth, bs, n_mem, dx)
    mem_v_all = cache_mem_v.reshape(depth, bs, n_mem, dx)
    for l in range(depth):
        rec = (state_ret[l], state_mlstm_c[l], state_mlstm_n[l], state_mlstm_m[l], state_lru_h[l], state_conv[l])
        h, st = _layer(h, tables, (mem_k_all, mem_v_all, l), (past_k_all, past_v_all, l), rec, weights[l], alpha)
        sample_states.append(st)
    y_sample = h
    sample_out = [jnp.stack(f) for f in zip(*sample_states)]

    return (y_prompt, y_sample, *prompt_out, *sample_out)
```

```python
import functools

import numpy as np
import jax
import jax.numpy as jnp
from jax import lax
from jax.experimental import pallas as pl
from jax.experimental.pallas import tpu as pltpu

F32 = jnp.float32
BF16 = jnp.bfloat16

CHUNK = 64
HEAD_DIM = 64
N_HEADS = 4
W_GROUP = N_HEADS * HEAD_DIM
SLAB = 2 * HEAD_DIM
N_PAIRS = N_HEADS // 2
A_BAND_CHUNKS = 8
A_REACH = A_BAND_CHUNKS * CHUNK
REL_CLIP = 128
CONV_W = 4
LRU_C = 8.0
LN_EPS = 1e-5
LOG2E = 1.4426950408889634
N_GATES = 2 * N_HEADS
PAST_LEN = 4096
ROLL_W = 1024
CONV_PAD = 8

V7X_VMEM_LIMIT_BYTES = 56 * 1024 * 1024
TOKEN_TILE = 512
MIX_BLOCK = 256
MIX_SEQS = 4
POST_SUBTILES = 2
FFN_SUBTILES = 2

PR_QB, PR_QBS, PR_VB, PR_GB, PR_QC, PR_VC, PR_OC, PR_XD, PR_YD = range(9)
N_PR = 9


def _cparams(*sem):
    return pltpu.CompilerParams(dimension_semantics=sem, vmem_limit_bytes=V7X_VMEM_LIMIT_BYTES)


def _dot(a, b):
    return jnp.dot(a.astype(BF16), b.astype(BF16), preferred_element_type=F32)


def _dot_nt(a, b):
    return lax.dot_general(a.astype(BF16), b.astype(BF16), (((1,), (1,)), ((), ())), preferred_element_type=F32)


def _layer_norm(x, g, b):
    mu = jnp.mean(x, -1, keepdims=True)
    xc = x - mu
    var = jnp.mean(xc * xc, -1, keepdims=True)
    return xc * lax.rsqrt(var + LN_EPS) * g + b


def _silu(x):
    return x * jax.nn.sigmoid(x)


def _lockstep(chains):
    while chains:
        chains = [c for c in chains if next(c, "done") != "done"]
        yield


def _run_interleaved(chains, lag=0):
    pending, live, rnd = list(chains), [], 0
    while pending or live:
        while pending and rnd >= lag * (len(chains) - len(pending)):
            live.append(pending.pop(0))
        live = [c for c in live if next(c, "done") != "done"]
        rnd += 1


def _resident(shape, layer=None):
    nd = len(shape)
    if layer is None:
        return pl.BlockSpec(shape, lambda *_: (0,) * nd, pipeline_mode=pl.Buffered(1))
    return pl.BlockSpec((None,) + tuple(shape), lambda *_: (layer,) + (0,) * nd, pipeline_mode=pl.Buffered(1))


def _ffn_ln_kernel(x_ref, wg_ref, wu_ref, wd_ref, g_ref, b_ref, o_ref, *, alpha, subtiles):
    rows = x_ref.shape[0] // subtiles

    def subtile_chain(i):
        rs = slice(i * rows, (i + 1) * rows)
        x = x_ref[rs, :]
        xb = x.astype(BF16)
        gate = jnp.dot(xb, wg_ref[...], preferred_element_type=F32)
        up = jnp.dot(xb, wu_ref[...], preferred_element_type=F32)
        yield
        h = (_silu(gate) * up).astype(BF16)
        yield
        y = jnp.dot(h, wd_ref[...], preferred_element_type=F32)
        yield
        o_ref[rs, :] = _layer_norm(alpha * x + 0.5 * y, g_ref[...], b_ref[...])

    _run_interleaved([subtile_chain(i) for i in range(subtiles)], lag=1)


def _ffn_ln(x, wg, wu, wd, layer, g, b, alpha):
    n, d = x.shape
    dff = wg.shape[2]
    tm = FFN_SUBTILES * TOKEN_TILE
    assert n % tm == 0
    return pl.pallas_call(
        functools.partial(_ffn_ln_kernel, alpha=alpha, subtiles=FFN_SUBTILES),
        out_shape=jax.ShapeDtypeStruct((n, d), F32),
        grid=(n // tm,),
        in_specs=[pl.BlockSpec((tm, d), lambda i: (i, 0)),
                  _resident((d, dff), layer), _resident((d, dff), layer), _resident((dff, d), layer),
                  _resident((1, d)), _resident((1, d))],
        out_specs=pl.BlockSpec((tm, d), lambda i: (i, 0)),
        compiler_params=_cparams("parallel"),
        name="ffn_ln",
    )(x, wg, wu, wd, g, b)


def _in_proj_kernel(x_ref, wa_ref, wr_ref, wt_ref, ba_ref, br_ref, bt_ref,
                    pa_ref, pr_ref, kat_ref, kbt_ref, kct_ref, gt_ref):
    xb = x_ref[...].astype(BF16)
    pa_ref[...] = jnp.dot(xb, wa_ref[...], preferred_element_type=F32) + ba_ref[...]
    pr_ref[...] = jnp.dot(xb, wr_ref[...], preferred_element_type=F32) + br_ref[...]
    tr = _dot_nt(wt_ref[...], xb) + bt_ref[...]
    kat_ref[...] = tr[0:W_GROUP].astype(BF16)
    kbt_ref[...] = tr[W_GROUP:2 * W_GROUP]
    kct_ref[...] = tr[2 * W_GROUP:3 * W_GROUP]
    gt_ref[...] = tr[3 * W_GROUP:3 * W_GROUP + N_GATES]


def _in_proj(x, wa, wr, wt, ba, br, bt):
    n, d = x.shape
    ca, cr, ct = wa.shape[1], wr.shape[1], wt.shape[0]
    tm = TOKEN_TILE
    tok = lambda w: pl.BlockSpec((tm, w), lambda i: (i, 0))
    trn = lambda r: pl.BlockSpec((r, tm), lambda i: (0, i))
    return pl.pallas_call(
        _in_proj_kernel,
        out_shape=(jax.ShapeDtypeStruct((n, ca), F32), jax.ShapeDtypeStruct((n, cr), F32),
                   jax.ShapeDtypeStruct((W_GROUP, n), BF16), jax.ShapeDtypeStruct((W_GROUP, n), F32),
                   jax.ShapeDtypeStruct((W_GROUP, n), F32), jax.ShapeDtypeStruct((N_GATES, n), F32)),
        grid=(n // tm,),
        in_specs=[tok(d), _resident((d, ca)), _resident((d, cr)), _resident((ct, d)),
                  _resident((1, ca)), _resident((1, cr)), _resident((ct, 1))],
        out_specs=(tok(ca), tok(cr), trn(W_GROUP), trn(W_GROUP), trn(W_GROUP), trn(N_GATES)),
        compiler_params=_cparams("parallel"),
        name="in_proj",
    )(x, wa, wr, wt, ba, br, bt)


def _head_rows(xt, h, pos):
    xh = xt[h * HEAD_DIM:(h + 1) * HEAD_DIM]
    z = jnp.zeros_like(xh)
    return jnp.concatenate([xh, z] if pos == 0 else [z, xh], axis=0)


def _col_bcast(eye_b, rows):
    hi = rows.astype(BF16)
    lo = (rows - hi.astype(F32)).astype(BF16)
    dn = (((1,), (1,)), ((), ()))
    return (lax.dot_general(eye_b, hi, dn, preferred_element_type=F32)
            + lax.dot_general(eye_b, lo, dn, preferred_element_type=F32))


def _rep_heads(rows4, n):
    return jnp.concatenate([jnp.broadcast_to(rows4[h:h + 1], (n, rows4.shape[1])) for h in range(N_HEADS)], axis=0)


def _seg_scan(x, seg_pos, seg_len, op, fill):
    s = 1
    while s < seg_len:
        x = op(x, jnp.where(seg_pos >= s, pltpu.roll(x, s, 1), fill))
        s *= 2
    return x


def _shift_rows(x, s, fill):
    if s % 8 == 0:
        return jnp.concatenate([jnp.full((s, x.shape[1]), fill, x.dtype), x[:x.shape[0] - s]], axis=0)
    row = lax.broadcasted_iota(jnp.int32, x.shape, 0)
    return jnp.where(row >= s, pltpu.roll(x, s, 0), fill)


def _mixer_kernel(pa_ref, pr_ref, kat_ref, kbt_ref, kct_ref, gt_ref, pastk_ref, pastv_ref, rel_ref,
                  cos_ref, sin_ref, cost_ref, sint_ref, dec_ref, qdec_ref, kdect_ref, sdec_ref,
                  ret0_ref, c0_ref, n0_ref, m0_ref,
                  cw_ref, cb_ref, wa_ref, ba_ref, wi_ref, bi_ref, lam_ref, buf0_ref, h0_ref,
                  y_ref, ret_out, c_out, n_out, m_out, h_out, buf_out,
                  bias_sc, spair, ppair, m_sc, win, h_sc, kpadt=None, vpad=None,
                  *, blk, seqs, n_steps, has_past):
    L = blk
    band = A_REACH + L
    n_past = A_REACH // L if not has_past else 0
    n_var = n_past + 1
    bb = pl.program_id(0)
    tb = pl.program_id(1)
    tail = CONV_W - 1
    if n_steps == 1:
        at_first = at_last = lambda f: f()
    else:
        at_first = pl.when(tb == 0)
        at_last = pl.when(tb == n_steps - 1)

    li = lax.broadcasted_iota(jnp.int32, (L, L), 0)
    lj = lax.broadcasted_iota(jnp.int32, (L, L), 1)
    causal_neg = jnp.where(lj <= li, 0.0, -jnp.inf)
    eye_b = jnp.where(li == lj, 1.0, 0.0).astype(BF16)
    lane_lo = lax.broadcasted_iota(jnp.int32, (L, SLAB), 1) < HEAD_DIM
    ones_slab = jnp.ones((L, SLAB), BF16)
    ri = lax.broadcasted_iota(jnp.int32, (SLAB, SLAB), 0) // HEAD_DIM
    rj = lax.broadcasted_iota(jnp.int32, (SLAB, SLAB), 1) // HEAD_DIM
    smask = jnp.where(ri == rj, 1.0, 0.0)
    pmask = jnp.concatenate([smask, smask], axis=1)
    gi = lax.broadcasted_iota(jnp.int32, (W_GROUP, W_GROUP), 0) // HEAD_DIM
    gj = lax.broadcasted_iota(jnp.int32, (W_GROUP, W_GROUP), 1) // HEAD_DIM
    gmat = jnp.where(gi == gj, 1.0 / HEAD_DIM, 0.0).astype(BF16)
    e64i = lax.broadcasted_iota(jnp.int32, (HEAD_DIM, HEAD_DIM), 0)
    e64j = lax.broadcasted_iota(jnp.int32, (HEAD_DIM, HEAD_DIM), 1)
    eye64 = e64i == e64j
    row_lo = lax.broadcasted_iota(jnp.int32, (SLAB, 1), 0) < HEAD_DIM

    @pl.when((bb == 0) & (tb == 0))
    def _():
        qq = lax.broadcasted_iota(jnp.int32, (L, band), 0)
        kk = lax.broadcasted_iota(jnp.int32, (L, band), 1)
        off = kk - (qq // CHUNK) * CHUNK
        for h in range(N_HEADS):
            row = jnp.broadcast_to(rel_ref[h], (L, ROLL_W))
            tile = pltpu.roll(row, ROLL_W - L, 1, stride=1, stride_axis=0)[:, :band] * LOG2E
            tile = jnp.where(off >= 0, jnp.where(off < A_REACH + CHUNK, tile, -jnp.inf), -jnp.inf)
            for v in range(n_var):
                first_col = (n_past - v) * L
                bias_sc[v * N_HEADS + h] = jnp.where(kk >= first_col, tile, -jnp.inf) if first_col > 0 else tile

    for s in range(seqs):
        @at_first
        def _():
            for p in range(N_PAIRS):
                sblk, cblk = [], []
                for hh in range(2):
                    h = 2 * p + hh
                    z = jnp.zeros((HEAD_DIM, HEAD_DIM), F32)
                    s_h = ret0_ref[s, h]
                    c_h = c0_ref[s, h]
                    n_row = n0_ref[s, h:h + 1, :]
                    n_col = jnp.sum(jnp.where(eye64, jnp.broadcast_to(n_row, eye64.shape), 0.0),
                                    axis=1, keepdims=True)
                    n_rep = jnp.broadcast_to(n_col, (HEAD_DIM, HEAD_DIM))
                    sblk.append(jnp.concatenate([s_h, z] if hh == 0 else [z, s_h], axis=1))
                    cblk.append(jnp.concatenate([c_h, z, n_rep, z] if hh == 0 else [z, c_h, z, n_rep], axis=1))
                spair[s, p] = jnp.concatenate(sblk, axis=0)
                ppair[s, p] = jnp.concatenate(cblk, axis=0)
            m_sc[s, 0:N_HEADS, :] = jnp.broadcast_to(m0_ref[s], (N_HEADS, m_sc.shape[2]))
            win[s, CONV_PAD - tail:CONV_PAD, :] = buf0_ref[s]
            h_sc[s] = h0_ref[s]
            if not has_past:
                vpad[s, 0:A_REACH, :] = jnp.zeros((A_REACH, W_GROUP), BF16)
                kpadt[s, 0:n_past * W_GROUP, :] = jnp.zeros((n_past * W_GROUP, L), BF16)

    g8 = gt_ref[...]
    seg_pos = lax.broadcasted_iota(jnp.int32, g8.shape, 1) % L
    b8 = _seg_scan(jax.nn.log_sigmoid(g8), seg_pos, L, jnp.add, 0.0)
    a8 = g8 - pltpu.roll(b8, N_HEADS, 0)
    cm8 = _seg_scan(a8, seg_pos, L, jnp.maximum, -jnp.inf)
    b8 = pltpu.roll(b8, N_HEADS, 0)

    heads = [(h // 2, h % 2, h) for h in range(N_HEADS)]
    slab = lambda p: slice(p * SLAB, (p + 1) * SLAB)
    seg_of = lambda s: slice(s * L, (s + 1) * L)
    col = lambda s, j: pr_ref[s, :, j * W_GROUP:(j + 1) * W_GROUP]

    def halves(per_head):
        return jnp.concatenate([jnp.where(lane_lo, per_head[2 * p], per_head[2 * p + 1]) for p in range(N_PAIRS)],
                               axis=1)


    def attention_chain(s):
        pa = pa_ref[s]
        kat = kat_ref[:, seg_of(s)]
        v_new = pa[:, 2 * W_GROUP:].astype(BF16)
        if has_past:
            kpast = pastk_ref[s].astype(BF16)
            vband = jnp.concatenate([pastv_ref[s].astype(BF16), v_new], axis=0)
            bias_at = lambda h: bias_sc[h]
        else:
            kpadt[s, pl.ds(pl.multiple_of((tb + n_past) * W_GROUP, W_GROUP), W_GROUP), :] = kat
            kwin = kpadt[s, pl.ds(pl.multiple_of(tb * W_GROUP, W_GROUP), (n_past + 1) * W_GROUP), :]
            start = pl.multiple_of(tb * L, L)
            vpad[s, pl.ds(A_REACH + start, L), :] = v_new
            vband = vpad[s, pl.ds(start, band), :]
            var = jnp.minimum(tb, n_var - 1) * N_HEADS
            bias_at = lambda h: bias_sc[var + h]
        qa = (pa[:, 0:W_GROUP] * (HEAD_DIM ** -0.5 * LOG2E)).astype(BF16)
        sc = {}
        for p, hh, h in heads:
            q_slab = qa[:, slab(p)]
            if has_past:
                k_slab = kpast[:, slab(p)]
                lane = lax.broadcasted_iota(jnp.int32, k_slab.shape, 1)
                lane_h = lane < HEAD_DIM if hh == 0 else lane >= HEAD_DIM
                parts = [_dot_nt(q_slab, jnp.where(lane_h, k_slab, jnp.zeros_like(k_slab))),
                         jnp.dot(q_slab, _head_rows(kat, h, hh), preferred_element_type=F32)]
            else:
                parts = [jnp.dot(q_slab, _head_rows(kwin[i * W_GROUP:(i + 1) * W_GROUP], h, hh),
                                 preferred_element_type=F32) for i in range(n_past + 1)]
            sc[h] = jnp.concatenate(parts, axis=1)
        yield
        pe, den = {}, {}
        for _, _, h in heads:
            sh = sc[h] + bias_at(h)
            pe[h] = jnp.exp2(sh - jnp.max(sh, axis=1, keepdims=True))
            den[h] = jnp.sum(pe[h], axis=1, keepdims=True)
        yield
        res = {h: _dot(pe[h], vband)[:, slab(p)] for p, _, h in heads}
        yield
        y_ref[s, :, 0:W_GROUP] = halves({h: res[h] / den[h] for _, _, h in heads})

    def retention_chain(s):
        qr = (col(s, PR_QB) * cos_ref[...] + col(s, PR_QBS) * sin_ref[...]).astype(BF16)
        kbt = kbt_ref[:, seg_of(s)]
        half = HEAD_DIM // 2
        kbt_sw = jnp.concatenate([kbt[h * HEAD_DIM + o:h * HEAD_DIM + o + half]
                                  for h in range(N_HEADS) for o in (half, 0)], axis=0)
        krt = (kbt * cost_ref[:, seg_of(s)] + kbt_sw * sint_ref[:, seg_of(s)]) * HEAD_DIM ** -0.5
        krt_b = krt.astype(BF16)
        kdt_b = (krt * kdect_ref[...]).astype(BF16)
        vb = col(s, PR_VB).astype(BF16)
        raw = {h: jnp.dot(qr[:, slab(p)], _head_rows(krt_b, h, hh), preferred_element_type=F32)
               for p, hh, h in heads}
        s0 = {p: spair[s, p] for p in range(N_PAIRS)}
        inter = {p: _dot(qr[:, slab(p)], s0[p]) for p in range(N_PAIRS)}
        add = {p: _dot(kdt_b[slab(p)], vb[:, slab(p)]) for p in range(N_PAIRS)}
        yield
        scores = {h: raw[h] * dec_ref[h] for _, _, h in heads}
        for p in range(N_PAIRS):
            spair[s, p] = s0[p] * sdec_ref[p] + add[p] * smask
        yield
        o = {h: _dot(scores[h], vb[:, slab(p)]) for p, _, h in heads}
        yield
        ob = halves(o) + jnp.concatenate([inter[p] for p in range(N_PAIRS)], axis=1) * qdec_ref[...]
        mu = _dot(ob, gmat)
        yield
        oc = ob - mu
        var = _dot(oc * oc, gmat)
        yield
        y_ref[s, :, W_GROUP:2 * W_GROUP] = oc * lax.rsqrt(var + LN_EPS) * _silu(col(s, PR_GB))

    def mlstm_chain(s):
        a4 = a8[0:N_HEADS, seg_of(s)]
        b4 = b8[0:N_HEADS, seg_of(s)]
        m0c = m_sc[s, 0:N_HEADS, 0:1]
        big_m = jnp.maximum(m0c, cm8[0:N_HEADS, seg_of(s)])
        m4 = b4 + big_m
        w0_4 = jnp.exp(m0c - big_m)
        m_last = big_m[:, L - 1:L]
        wrow4 = jnp.exp(a4 - m_last)
        kct = kct_ref[:, seg_of(s)] * HEAD_DIM ** -0.5
        kct_b = kct.astype(BF16)
        kw_b = (kct * _rep_heads(wrow4, HEAD_DIM)).astype(BF16)
        qc = col(s, PR_QC).astype(BF16)
        vc = col(s, PR_VC).astype(BF16)
        v_aug = {p: jnp.concatenate([vc[:, slab(p)], ones_slab], axis=1) for p in range(N_PAIRS)}
        p0 = {p: ppair[s, p] for p in range(N_PAIRS)}
        w0_bc = _col_bcast(eye_b, _rep_heads(w0_4, HEAD_DIM))
        floor_bc = _col_bcast(eye_b, _rep_heads(jnp.exp(-m4), HEAD_DIM))
        m_bc = {h: _col_bcast(eye_b, jnp.broadcast_to(big_m[h:h + 1], (L, L))) for _, _, h in heads}
        raw = {h: jnp.dot(qc[:, slab(p)], _head_rows(kct_b, h, hh), preferred_element_type=F32)
               for p, hh, h in heads}
        inter = {p: _dot(qc[:, slab(p)], p0[p]) for p in range(N_PAIRS)}
        add = {p: _dot(kw_b[slab(p)], v_aug[p]) for p in range(N_PAIRS)}
        yield
        qk = {h: raw[h] * jnp.exp(a4[h:h + 1] - m_bc[h] + causal_neg) for _, _, h in heads}
        for p in range(N_PAIRS):
            w_state = jnp.where(row_lo, w0_4[2 * p:2 * p + 1, L - 1:L], w0_4[2 * p + 1:2 * p + 2, L - 1:L])
            ppair[s, p] = p0[p] * w_state + add[p] * pmask
        m_sc[s, 0:N_HEADS, :] = jnp.broadcast_to(m4[:, L - 1:L], (N_HEADS, m_sc.shape[2]))
        yield
        res = {h: _dot(qk[h], v_aug[p]) for p, _, h in heads}
        yield
        hc = []
        for p in range(N_PAIRS):
            w0 = w0_bc[:, slab(p)]
            num = jnp.where(lane_lo, res[2 * p][:, 0:SLAB], res[2 * p + 1][:, 0:SLAB]) + inter[p][:, 0:SLAB] * w0
            den = jnp.where(lane_lo, res[2 * p][:, SLAB:], res[2 * p + 1][:, SLAB:]) + inter[p][:, SLAB:] * w0
            hc.append(num / jnp.maximum(jnp.abs(den), floor_bc[:, slab(p)]))
        hc = jnp.concatenate(hc, axis=1)
        mu = _dot(hc, gmat)
        yield
        hcc = hc - mu
        var = _dot(hcc * hcc, gmat)
        yield
        y_ref[s, :, 2 * W_GROUP:3 * W_GROUP] = hcc * lax.rsqrt(var + LN_EPS) * jax.nn.sigmoid(col(s, PR_OC))

    def rglru_chain(s):
        xd = col(s, PR_XD)
        win[s, CONV_PAD:CONV_PAD + L, :] = xd
        xc = cb_ref[...]
        for j in range(CONV_W):
            off = CONV_PAD - tail + j
            xc = xc + win[s, off:off + L, :] * cw_ref[j:j + 1, :]
        win[s, CONV_PAD - tail:CONV_PAD, :] = xd[L - tail:, :]
        r_pre = _dot(xc, wa_ref[...])
        i_pre = _dot(xc, wi_ref[...])
        yield
        r = jax.nn.sigmoid(r_pre + ba_ref[...])
        i = jax.nn.sigmoid(i_pre + bi_ref[...])
        neg_lam = -lam_ref[...]
        softplus = jnp.maximum(neg_lam, 0.0) + jnp.log1p(jnp.exp(-jnp.abs(neg_lam)))
        log_a = -LRU_C * r * softplus
        a = jnp.exp(log_a)
        u = jnp.sqrt(-jnp.tanh(log_a) * (jnp.exp(2.0 * log_a) + 1.0)) * (i * xc)
        st = 1
        while st < L:
            u = a * _shift_rows(u, st, 0.0) + u
            a = a * _shift_rows(a, st, 1.0)
            st *= 2
            if st in (8, 64):
                yield
        hseq = u + a * h_sc[s]
        h_sc[s] = hseq[L - 1:L, :]
        y_ref[s, :, 3 * W_GROUP:] = hseq * jax.nn.gelu(col(s, PR_YD))

    _run_interleaved([chain(s) for chain in (attention_chain, retention_chain, mlstm_chain, rglru_chain)
                      for s in range(seqs)], lag=1 if seqs == 1 else 0)

    for s in range(seqs):
        @at_last
        def _():
            for p in range(N_PAIRS):
                sp = spair[s, p]
                pp = ppair[s, p]
                for hh in range(2):
                    h = 2 * p + hh
                    rs = slice(hh * HEAD_DIM, (hh + 1) * HEAD_DIM)
                    ret_out[s, h] = sp[rs, rs]
                    c_out[s, h] = pp[rs, rs]
                    n_rep = pp[rs, SLAB + hh * HEAD_DIM:SLAB + (hh + 1) * HEAD_DIM]
                    n_out[s, h:h + 1, :] = jnp.sum(jnp.where(eye64, n_rep, 0.0), axis=0, keepdims=True)
            m_out[s] = m_sc[s, 0:N_HEADS, 0:1]
            h_out[s] = h_sc[s]
            buf_out[s] = pr_ref[s, L - tail:, PR_XD * W_GROUP:(PR_XD + 1) * W_GROUP]


def _mixer_kernel_fresh(pa_ref, pr_ref, kat_ref, kbt_ref, kct_ref, gt_ref, *rest, **static):
    _mixer_kernel(pa_ref, pr_ref, kat_ref, kbt_ref, kct_ref, gt_ref, None, None, *rest, **static)


def _mixer_geometry(t, has_past):
    if has_past:
        assert t == CHUNK, "sequences with a carried cache are expected to be a single chunk"
        return t, MIX_SEQS
    assert t % MIX_BLOCK == 0 and A_REACH % MIX_BLOCK == 0
    return MIX_BLOCK, 1


def _mixer_tables(pos, blk, seqs):
    cos, sin = _rotary_tables(pos)
    return (cos, sin, jnp.tile(cos.T, (1, seqs)), jnp.tile(sin.T, (1, seqs))) + _retention_tables(blk)


def _mixer(pa, pr, kat, kbt, kct, gt, past, rel_rows, tables, ret0, c0, n0, m0, lw, buf0, h0):
    bsz, t, _ = pa.shape
    has_past = past is not None
    blk, seqs = _mixer_geometry(t, has_past)
    nb = t // blk
    assert bsz % seqs == 0
    band = A_REACH + blk
    n_var = 1 if has_past else A_REACH // blk + 1
    tail = CONV_W - 1

    tok3 = lambda w: pl.BlockSpec((seqs, blk, w), lambda b, c: (b, c, 0))
    trn = lambda r: pl.BlockSpec((r, seqs * blk), lambda b, c: (0, b * nb + c))
    per_seq = lambda *shape: pl.BlockSpec((seqs,) + shape, lambda b, c: (b,) + (0,) * len(shape))
    const = lambda *shape: pl.BlockSpec(shape, lambda b, c: (0,) * len(shape))
    past_specs, past_args = [], ()
    if has_past:
        layer = past[2]
        cache = pl.BlockSpec((None, seqs, A_REACH, W_GROUP), lambda b, c: (layer, b, 0, 0))
        past_specs, past_args = [cache, cache], past[:2]
    in_specs = [tok3(3 * W_GROUP), tok3(N_PR * W_GROUP), trn(W_GROUP), trn(W_GROUP), trn(W_GROUP), trn(N_GATES),
                *past_specs, const(N_HEADS, 1, ROLL_W),
                pl.BlockSpec((blk, W_GROUP), lambda b, c: (c, 0)), pl.BlockSpec((blk, W_GROUP), lambda b, c: (c, 0)),
                pl.BlockSpec((W_GROUP, seqs * blk), lambda b, c: (0, c)),
                pl.BlockSpec((W_GROUP, seqs * blk), lambda b, c: (0, c)),
                const(N_HEADS, blk, blk), const(blk, W_GROUP), const(W_GROUP, blk), const(N_PAIRS, SLAB, SLAB),
                per_seq(N_HEADS, HEAD_DIM, HEAD_DIM), per_seq(N_HEADS, HEAD_DIM, HEAD_DIM),
                per_seq(N_HEADS, HEAD_DIM), per_seq(N_HEADS, 1),
                const(CONV_W, W_GROUP), const(1, W_GROUP), const(W_GROUP, W_GROUP), const(1, W_GROUP),
                const(W_GROUP, W_GROUP), const(1, W_GROUP), const(1, W_GROUP),
                per_seq(tail, W_GROUP), per_seq(1, W_GROUP)]
    out_shape = (jax.ShapeDtypeStruct((bsz, t, 4 * W_GROUP), F32),
                 jax.ShapeDtypeStruct((bsz, N_HEADS, HEAD_DIM, HEAD_DIM), F32),
                 jax.ShapeDtypeStruct((bsz, N_HEADS, HEAD_DIM, HEAD_DIM), F32),
                 jax.ShapeDtypeStruct((bsz, N_HEADS, HEAD_DIM), F32),
                 jax.ShapeDtypeStruct((bsz, N_HEADS, 1), F32),
                 jax.ShapeDtypeStruct((bsz, 1, W_GROUP), F32),
                 jax.ShapeDtypeStruct((bsz, tail, W_GROUP), F32))
    out_specs = (tok3(4 * W_GROUP), per_seq(N_HEADS, HEAD_DIM, HEAD_DIM), per_seq(N_HEADS, HEAD_DIM, HEAD_DIM),
                 per_seq(N_HEADS, HEAD_DIM), per_seq(N_HEADS, 1), per_seq(1, W_GROUP), per_seq(tail, W_GROUP))
    scratch = [pltpu.VMEM((n_var * N_HEADS, blk, band), F32),
               pltpu.VMEM((seqs, N_PAIRS, SLAB, SLAB), F32), pltpu.VMEM((seqs, N_PAIRS, SLAB, 2 * SLAB), F32),
               pltpu.VMEM((seqs, 8, 128), F32), pltpu.VMEM((seqs, CONV_PAD + blk, W_GROUP), F32),
               pltpu.VMEM((seqs, 1, W_GROUP), F32)]
    if not has_past:
        scratch += [pltpu.VMEM((seqs, (nb + A_REACH // blk) * W_GROUP, blk), BF16),
                    pltpu.VMEM((seqs, A_REACH + t, W_GROUP), BF16)]
    body = _mixer_kernel if has_past else _mixer_kernel_fresh
    return pl.pallas_call(
        functools.partial(body, blk=blk, seqs=seqs, n_steps=nb, has_past=has_past),
        out_shape=out_shape,
        grid=(bsz // seqs, nb),
        in_specs=in_specs,
        out_specs=out_specs,
        scratch_shapes=scratch,
        compiler_params=_cparams("arbitrary", "arbitrary"),
        name="mixer",
    )(pa, pr, kat, kbt, kct, gt, *past_args, rel_rows, *tables,
      ret0, c0, n0, m0, lw['conv_w'], lw['conv_b'], lw['lru_wa'], lw['lru_ba'], lw['lru_wi'], lw['lru_bi'],
      lw['lru_lambda'], buf0, h0)


def _post_kernel(x_ref, y_ref, wo_ref, g1_ref, b1_ref, wq_ref, mk_ref, mv_ref, xo_ref,
                 g2_ref, b2_ref, o_ref, att_sc, *, alpha, seqs, subtiles):
    rows_sub = x_ref.shape[0] // subtiles
    rows = rows_sub // seqs
    head = lambda h: slice(h * HEAD_DIM, (h + 1) * HEAD_DIM)

    def subtile_chain(i):
        rs = slice(i * rows_sub, (i + 1) * rows_sub)
        x = x_ref[rs, :]
        mix = _dot(y_ref[rs, :], wo_ref[...])
        yield
        x2 = _layer_norm(alpha * x + mix, g1_ref[...], b1_ref[...])
        yield
        q = _dot(x2, wq_ref[...]).astype(BF16)
        yield

        def attend(s, h):
            qs = slice(s * rows, (s + 1) * rows)
            sc = _dot_nt(q[qs, head(h)], mk_ref[s, :, head(h)]) * HEAD_DIM ** -0.5
            yield
            p = jnp.exp(sc - jnp.max(sc, axis=1, keepdims=True))
            den = jnp.sum(p, axis=1, keepdims=True)
            yield
            o = _dot(p, mv_ref[s, :, head(h)])
            yield
            att_sc[i * rows_sub + s * rows:i * rows_sub + (s + 1) * rows, head(h)] = o / den

        yield from _lockstep([attend(s, h) for s in range(seqs) for h in range(N_HEADS)])
        out = _dot(att_sc[rs, :], xo_ref[...])
        yield
        o_ref[rs, :] = _layer_norm(alpha * x2 + out, g2_ref[...], b2_ref[...])

    _run_interleaved([subtile_chain(i) for i in range(subtiles)], lag=2)


def _post(x, y, wo, wq, xo, wlayer, g1, b1, g2, b2, mk, mv, layer, alpha, seq_len):
    n, d = x.shape
    n_mem, dx = mk.shape[2], mk.shape[3]
    if seq_len >= POST_SUBTILES * TOKEN_TILE:
        subtiles, seqs = POST_SUBTILES, 1
        tm = subtiles * TOKEN_TILE
        mem = pl.BlockSpec((None, 1, n_mem, dx), lambda i: (layer, i // (seq_len // tm), 0, 0))
    else:
        tm = TOKEN_TILE
        subtiles, seqs = 1, tm // seq_len
        mem = pl.BlockSpec((None, seqs, n_mem, dx), lambda i: (layer, i, 0, 0))
    assert n % tm == 0 and tm % (subtiles * seqs) == 0
    tok = lambda w: pl.BlockSpec((tm, w), lambda i: (i, 0))
    return pl.pallas_call(
        functools.partial(_post_kernel, alpha=alpha, seqs=seqs, subtiles=subtiles),
        out_shape=jax.ShapeDtypeStruct((n, d), F32),
        grid=(n // tm,),
        in_specs=[tok(d), tok(y.shape[1]),
                  _resident(wo.shape[1:], wlayer), _resident((1, d)), _resident((1, d)),
                  _resident(wq.shape[1:], wlayer), mem, mem, _resident(xo.shape[1:], wlayer),
                  _resident((1, d)), _resident((1, d))],
        out_specs=tok(d),
        scratch_shapes=[pltpu.VMEM((tm, dx), F32)],
        compiler_params=_cparams("parallel"),
        name="post",
    )(x, y, wo, g1, b1, wq, mk, mv, xo, g2, b2)


def _mem_kv_kernel(x_ref, w_ref, o_ref):
    o_ref[...] = _dot(x_ref[...], w_ref[...])


def _mem_kv(mem, w):
    n, d = mem.shape
    tm = min(TOKEN_TILE, n)
    return pl.pallas_call(
        _mem_kv_kernel,
        out_shape=jax.ShapeDtypeStruct((n, w.shape[1]), F32),
        grid=(n // tm,),
        in_specs=[pl.BlockSpec((tm, d), lambda i: (i, 0)), _resident(w.shape)],
        out_specs=pl.BlockSpec((tm, w.shape[1]), lambda i: (i, 0)),
        compiler_params=_cparams("parallel"),
        name="mem_kv",
    )(mem, w)


def _rotary_tables(pos):
    half = HEAD_DIM // 2
    inv = jnp.exp(-jnp.log(10000.0) * jnp.arange(half, dtype=F32) / half)
    ang = pos.astype(F32)[:, None] * inv[None, :]
    cos, sin = jnp.cos(ang), jnp.sin(ang)
    cos_full = jnp.tile(jnp.concatenate([cos, cos], -1), (1, N_HEADS))
    sin_full = jnp.tile(jnp.concatenate([-sin, sin], -1), (1, N_HEADS))
    return cos_full, sin_full


def _retention_tables(blk):
    log_g = jnp.log1p(-jnp.exp2(-5.0 - jnp.arange(N_HEADS, dtype=F32)))
    idx = jnp.arange(blk, dtype=F32)
    diff = idx[:, None] - idx[None, :]
    dec = jnp.exp(jnp.where((diff >= 0)[None], diff[None] * log_g[:, None, None], -jnp.inf))
    q_dec = jnp.exp((idx[:, None] + 1.0) * log_g[None, :])
    k_dec = jnp.exp((blk - 1.0 - idx)[:, None] * log_g[None, :])
    s_dec = jnp.exp(blk * log_g)
    lanes = lambda tbl: jnp.repeat(tbl, HEAD_DIM, axis=1)
    pair = lambda p: jnp.kron(jnp.diag(s_dec[2 * p:2 * p + 2]), jnp.ones((HEAD_DIM, HEAD_DIM), F32))
    return dec, lanes(q_dec), lanes(k_dec).T, jnp.stack([pair(p) for p in range(N_PAIRS)])


def _rel_bias_rows(rel_bias, blk):
    idx = np.clip(A_REACH + blk - np.arange(ROLL_W), -REL_CLIP, REL_CLIP) + REL_CLIP
    return rel_bias[:, idx][:, None, :]


def _block_diag(w):
    h, c, _ = w.shape
    eye = jnp.eye(h, dtype=w.dtype)
    return (eye[:, None, :, None] * w[:, :, None, :]).reshape(h * c, h * c)


def _swap_perm():
    j = np.arange(W_GROUP)
    return (j // HEAD_DIM) * HEAD_DIM + (j % HEAD_DIM + HEAD_DIM // 2) % HEAD_DIM


def _layer(x, tables, mem, past, rec, lw, alpha):
    bsz, t, d = x.shape
    n = bsz * t
    ret_s, ml_c, ml_n, ml_m, lru_h, conv_buf = rec
    g, b = lw['ln_g'], lw['ln_b']

    wl = lw['layer']
    x1 = _ffn_ln(x.reshape(n, d), lw['ffn1_gate'], lw['ffn1_up'], lw['ffn1_down'], wl, g[0:1], b[0:1], alpha)
    pa, pr, kat, kbt, kct, gt = _in_proj(x1, lw['w_in_a'], lw['w_in_r'], lw['w_in_t'],
                                         lw['b_in_a'], lw['b_in_r'], lw['b_in_t'])
    pa = pa.reshape(bsz, t, 3 * W_GROUP)
    pr = pr.reshape(bsz, t, N_PR * W_GROUP)
    blk, _ = _mixer_geometry(t, past is not None)
    y, ret_new, c_new, n_new, m_new, h_new, conv_new = _mixer(
        pa, pr, kat, kbt, kct, gt, past, _rel_bias_rows(lw['rel_bias'], blk), tables,
        ret_s, ml_c, ml_n, ml_m[:, :, None], lw, conv_buf, lru_h[:, None, :])

    x3 = _post(x1, y.reshape(n, 4 * W_GROUP), lw['w_out'], lw['x_wq'], lw['x_wo'], wl,
               g[1:2], b[1:2], g[2:3], b[2:3], *mem, alpha, t)
    x4 = _ffn_ln(x3, lw['ffn2_gate'], lw['ffn2_up'], lw['ffn2_down'], wl, g[3:4], b[3:4], alpha)

    heads = lambda a: a.reshape(bsz, -1, N_HEADS, HEAD_DIM)
    keep = min(A_REACH, t)
    a_k = heads(pa[:, t - keep:, W_GROUP:2 * W_GROUP])
    a_v = heads(pa[:, t - keep:, 2 * W_GROUP:])
    state = (a_k, a_v, ret_new, c_new, n_new, m_new[:, :, 0], h_new[:, 0, :], conv_new)
    return x4.reshape(bsz, t, d), state


def kernel(x_prompt, x_sample, mem_prompt, cache_a_k, cache_a_v, state_ret, state_mlstm_c, state_mlstm_n,
           state_mlstm_m, state_lru_h, state_conv, cache_mem_k, cache_mem_v, ln_g, ln_b,
           ffn1_gate, ffn1_up, ffn1_down, ffn2_gate, ffn2_up, ffn2_down, w_in, b_in, a_rel_bias,
           conv_w, conv_b, lru_wa, lru_ba, lru_wi, lru_bi, lru_lambda, w_out, x_wq, x_wk, x_wv, x_wo):
    depth = ln_g.shape[0]
    alpha = (2.0 * depth) ** 0.25
    row = lambda v: v[None, :]
    grp = lambda a, j: a[..., j * W_GROUP:(j + 1) * W_GROUP]
    perm = _swap_perm()
    QA, KA, VA, QB, KB, VB, GB, QC, KC, VC, OC, XD, YD = range(13)

    def layer_weights(l):
        w, bias = w_in[l], b_in[l]
        nat = lambda a: jnp.concatenate(
            [grp(a, QB), grp(a, QB)[..., perm], grp(a, VB), grp(a, GB), grp(a, QC), grp(a, VC), grp(a, OC),
             grp(a, XD), grp(a, YD)], axis=-1)
        trn = lambda a: jnp.concatenate([grp(a, KA), grp(a, KB), grp(a, KC), a[..., 13 * W_GROUP:]], axis=-1)
        return {'ln_g': ln_g[l], 'ln_b': ln_b[l],
                'layer': l, **stacks,
                'w_in_a': w[:, :3 * W_GROUP].astype(BF16), 'w_in_r': nat(w).astype(BF16),
                'w_in_t': trn(w).T.astype(BF16),
                'b_in_a': row(bias[:3 * W_GROUP]), 'b_in_r': row(nat(bias)), 'b_in_t': trn(bias)[:, None],
                'rel_bias': a_rel_bias[l],
                'conv_w': conv_w[l], 'conv_b': row(conv_b[l]),
                'lru_wa': _block_diag(lru_wa[l]).astype(BF16), 'lru_ba': row(lru_ba[l]),
                'lru_wi': _block_diag(lru_wi[l]).astype(BF16), 'lru_bi': row(lru_bi[l]),
                'lru_lambda': row(lru_lambda[l]),
                'x_wkv': jnp.concatenate([x_wk[l], x_wv[l]], axis=1).astype(BF16)}

    stacks = {name: w.astype(BF16) for name, w in (
        ('ffn1_gate', ffn1_gate), ('ffn1_up', ffn1_up), ('ffn1_down', ffn1_down),
        ('ffn2_gate', ffn2_gate), ('ffn2_up', ffn2_up), ('ffn2_down', ffn2_down),
        ('w_out', w_out), ('x_wq', x_wq), ('x_wo', x_wo))}
    weights = [layer_weights(l) for l in range(depth)]
    dx = x_wk.shape[2]

    bp, tp, d = x_prompt.shape
    n_mem = mem_prompt.shape[1]
    h = x_prompt
    prompt_states = []
    tables = _mixer_tables(jnp.arange(tp), *_mixer_geometry(tp, False))
    for l in range(depth):
        lw = weights[l]
        mkv = _mem_kv(mem_prompt.reshape(bp * n_mem, d), lw['x_wkv'])
        mk = mkv[:, :dx].reshape(bp, n_mem, dx)
        mv = mkv[:, dx:].reshape(bp, n_mem, dx)
        rec0 = (jnp.zeros((bp, N_HEADS, HEAD_DIM, HEAD_DIM), F32),
                jnp.zeros((bp, N_HEADS, HEAD_DIM, HEAD_DIM), F32),
                jnp.zeros((bp, N_HEADS, HEAD_DIM), F32),
                jnp.zeros((bp, N_HEADS), F32),
                jnp.zeros((bp, W_GROUP), F32),
                jnp.zeros((bp, CONV_W - 1, W_GROUP), F32))
        h, st = _layer(h, tables, (mk[None], mv[None], 0), None, rec0, lw, alpha)
        prompt_states.append(st + (mk.reshape(bp, n_mem, -1, HEAD_DIM), mv.reshape(bp, n_mem, -1, HEAD_DIM)))
    y_prompt = h
    prompt_out = [jnp.stack(f) for f in zip(*prompt_states)]

    bs, ts, _ = x_sample.shape
    h = x_sample
    sample_states = []
    tables = _mixer_tables(PAST_LEN + jnp.arange(ts), *_mixer_geometry(ts, True))
    past_k_all = cache_a_k.reshape(depth, bs, A_REACH, W_GROUP)
    past_v_all = cache_a_v.reshape(depth, bs, A_REACH, W_GROUP)
    mem_k_all = cache_mem_k.reshape(depth, bs, n_mem, dx)
    mem_v_all = cache_mem_v.reshape(depth, bs, n_mem, dx)
    for l in range(depth):
        rec = (state_ret[l], state_mlstm_c[l], state_mlstm_n[l], state_mlstm_m[l], state_lru_h[l], state_conv[l])
        h, st = _layer(h, tables, (mem_k_all, mem_v_all, l), (past_k_all, past_v_all, l), rec, weights[l], alpha)
        sample_states.append(st)
    y_sample = h
    sample_out = [jnp.stack(f) for f in zip(*sample_states)]

    return (y_prompt, y_sample, *prompt_out, *sample_out)
```

```python
import functools

import numpy as np
import jax
import jax.numpy as jnp
from jax import lax
from jax.experimental import pallas as pl
from jax.experimental.pallas import tpu as pltpu

F32 = jnp.float32
BF16 = jnp.bfloat16

CHUNK = 64
HEAD_DIM = 64
N_HEADS = 4
W_GROUP = N_HEADS * HEAD_DIM
SLAB = 2 * HEAD_DIM
N_PAIRS = N_HEADS // 2
A_BAND_CHUNKS = 8
A_REACH = A_BAND_CHUNKS * CHUNK
REL_CLIP = 128
CONV_W = 4
LRU_C = 8.0
LN_EPS = 1e-5
LOG2E = 1.4426950408889634
N_GATES = 2 * N_HEADS
PAST_LEN = 4096
ROLL_W = 1024
CONV_PAD = 8

V7X_VMEM_LIMIT_BYTES = 56 * 1024 * 1024
TOKEN_TILE = 512
MIX_BLOCK = 256
MIX_SEQS = 4
POST_SUBTILES = 2

PR_QB, PR_QBS, PR_VB, PR_GB, PR_QC, PR_VC, PR_OC, PR_XD, PR_YD = range(9)
N_PR = 9


def _cparams(*sem):
    return pltpu.CompilerParams(dimension_semantics=sem, vmem_limit_bytes=V7X_VMEM_LIMIT_BYTES)


def _dot(a, b):
    return jnp.dot(a.astype(BF16), b.astype(BF16), preferred_element_type=F32)


def _dot_nt(a, b):
    return lax.dot_general(a.astype(BF16), b.astype(BF16), (((1,), (1,)), ((), ())), preferred_element_type=F32)


def _layer_norm(x, g, b):
    mu = jnp.mean(x, -1, keepdims=True)
    xc = x - mu
    var = jnp.mean(xc * xc, -1, keepdims=True)
    return xc * lax.rsqrt(var + LN_EPS) * g + b


def _silu(x):
    return x * jax.nn.sigmoid(x)


def _lockstep(chains):
    while chains:
        chains = [c for c in chains if next(c, "done") != "done"]
        yield


def _run_interleaved(chains, lag=0):
    pending, live, rnd = list(chains), [], 0
    while pending or live:
        while pending and rnd >= lag * (len(chains) - len(pending)):
            live.append(pending.pop(0))
        live = [c for c in live if next(c, "done") != "done"]
        rnd += 1


def _resident(shape, layer=None):
    nd = len(shape)
    if layer is None:
        return pl.BlockSpec(shape, lambda *_: (0,) * nd, pipeline_mode=pl.Buffered(1))
    return pl.BlockSpec((None,) + tuple(shape), lambda *_: (layer,) + (0,) * nd, pipeline_mode=pl.Buffered(1))


def _cast_plan(stack, layer, n_steps):
    _, r, c = stack.shape
    nb = next(k for k in range(n_steps, 0, -1) if n_steps % k == 0 and r % k == 0 and (r // k) % 16 == 0)
    every = n_steps // nb
    return (pl.BlockSpec((None, r // nb, c), lambda i: (layer, i // every, 0)),
            pl.BlockSpec((r // nb, c), lambda i: (i // every, 0)),
            jax.ShapeDtypeStruct((r, c), BF16))


def _with_casts(body, n_in, n_out, n_cast):
    def wrapped(*refs):
        ins, rest = refs[:n_in + n_cast], refs[n_in + n_cast:]
        outs, scratch = rest[:n_out + n_cast], rest[n_out + n_cast:]
        body(*ins[:n_in], *outs[:n_out], *scratch)
        for src, dst in zip(ins[n_in:], outs[n_out:]):
            dst[...] = src[...].astype(BF16)
    return wrapped


def _ffn_ln_kernel(x_ref, wg_ref, wu_ref, wd_ref, g_ref, b_ref, o_ref, *, alpha):
    x = x_ref[...]
    xb = x.astype(BF16)
    gate = jnp.dot(xb, wg_ref[...], preferred_element_type=F32)
    up = jnp.dot(xb, wu_ref[...], preferred_element_type=F32)
    h = (_silu(gate) * up).astype(BF16)
    y = jnp.dot(h, wd_ref[...], preferred_element_type=F32)
    o_ref[...] = _layer_norm(alpha * x + 0.5 * y, g_ref[...], b_ref[...])


def _ffn_ln(x, wg, wu, wd, g, b, alpha):
    n, d = x.shape
    dff = wg.shape[1]
    tm = TOKEN_TILE
    return pl.pallas_call(
        functools.partial(_ffn_ln_kernel, alpha=alpha),
        out_shape=jax.ShapeDtypeStruct((n, d), F32),
        grid=(n // tm,),
        in_specs=[pl.BlockSpec((tm, d), lambda i: (i, 0)),
                  _resident((d, dff)), _resident((d, dff)), _resident((dff, d)),
                  _resident((1, d)), _resident((1, d))],
        out_specs=pl.BlockSpec((tm, d), lambda i: (i, 0)),
        compiler_params=_cparams("parallel"),
        name="ffn_ln",
    )(x, wg, wu, wd, g, b)


def _in_proj_kernel(x_ref, wa_ref, wr_ref, wt_ref, ba_ref, br_ref, bt_ref,
                    pa_ref, pr_ref, kat_ref, kbt_ref, kct_ref, gt_ref):
    xb = x_ref[...].astype(BF16)
    pa_ref[...] = jnp.dot(xb, wa_ref[...], preferred_element_type=F32) + ba_ref[...]
    pr_ref[...] = jnp.dot(xb, wr_ref[...], preferred_element_type=F32) + br_ref[...]
    tr = _dot_nt(wt_ref[...], xb) + bt_ref[...]
    kat_ref[...] = tr[0:W_GROUP].astype(BF16)
    kbt_ref[...] = tr[W_GROUP:2 * W_GROUP]
    kct_ref[...] = tr[2 * W_GROUP:3 * W_GROUP]
    gt_ref[...] = tr[3 * W_GROUP:3 * W_GROUP + N_GATES]


def _in_proj(x, wa, wr, wt, ba, br, bt, casts=()):
    n, d = x.shape
    ca, cr, ct = wa.shape[1], wr.shape[1], wt.shape[0]
    tm = TOKEN_TILE
    tok = lambda w: pl.BlockSpec((tm, w), lambda i: (i, 0))
    trn = lambda r: pl.BlockSpec((r, tm), lambda i: (0, i))
    plans = [_cast_plan(stack, layer, n // tm) for stack, layer in casts]
    out = pl.pallas_call(
        _with_casts(_in_proj_kernel, 7, 6, len(casts)),
        out_shape=(jax.ShapeDtypeStruct((n, ca), F32), jax.ShapeDtypeStruct((n, cr), F32),
                   jax.ShapeDtypeStruct((W_GROUP, n), BF16), jax.ShapeDtypeStruct((W_GROUP, n), F32),
                   jax.ShapeDtypeStruct((W_GROUP, n), F32), jax.ShapeDtypeStruct((N_GATES, n), F32),
                   *[p[2] for p in plans]),
        grid=(n // tm,),
        in_specs=[tok(d), _resident((d, ca)), _resident((d, cr)), _resident((ct, d)),
                  _resident((1, ca)), _resident((1, cr)), _resident((ct, 1)), *[p[0] for p in plans]],
        out_specs=(tok(ca), tok(cr), trn(W_GROUP), trn(W_GROUP), trn(W_GROUP), trn(N_GATES), *[p[1] for p in plans]),
        compiler_params=_cparams("parallel"),
        name="in_proj",
    )(x, wa, wr, wt, ba, br, bt, *[stack for stack, _ in casts])
    return out[:6], out[6:]


def _head_rows(xt, h, pos):
    xh = xt[h * HEAD_DIM:(h + 1) * HEAD_DIM]
    z = jnp.zeros_like(xh)
    return jnp.concatenate([xh, z] if pos == 0 else [z, xh], axis=0)


def _col_bcast(eye_b, rows):
    hi = rows.astype(BF16)
    lo = (rows - hi.astype(F32)).astype(BF16)
    dn = (((1,), (1,)), ((), ()))
    return (lax.dot_general(eye_b, hi, dn, preferred_element_type=F32)
            + lax.dot_general(eye_b, lo, dn, preferred_element_type=F32))


def _rep_heads(rows4, n):
    return jnp.concatenate([jnp.broadcast_to(rows4[h:h + 1], (n, rows4.shape[1])) for h in range(N_HEADS)], axis=0)


def _seg_scan(x, seg_pos, seg_len, op, fill):
    s = 1
    while s < seg_len:
        x = op(x, jnp.where(seg_pos >= s, pltpu.roll(x, s, 1), fill))
        s *= 2
    return x


def _shift_rows(x, s, fill):
    if s % 8 == 0:
        return jnp.concatenate([jnp.full((s, x.shape[1]), fill, x.dtype), x[:x.shape[0] - s]], axis=0)
    row = lax.broadcasted_iota(jnp.int32, x.shape, 0)
    return jnp.where(row >= s, pltpu.roll(x, s, 0), fill)


def _mixer_kernel(pa_ref, pr_ref, kat_ref, kbt_ref, kct_ref, gt_ref, pastk_ref, pastv_ref, rel_ref,
                  cos_ref, sin_ref, cost_ref, sint_ref, dec_ref, qdec_ref, kdect_ref, sdec_ref,
                  ret0_ref, c0_ref, n0_ref, m0_ref,
                  cw_ref, cb_ref, wa_ref, ba_ref, wi_ref, bi_ref, lam_ref, buf0_ref, h0_ref,
                  y_ref, ret_out, c_out, n_out, m_out, h_out, buf_out,
                  bias_sc, spair, ppair, m_sc, win, h_sc, kpadt=None, vpad=None,
                  *, blk, seqs, n_steps, has_past):
    L = blk
    band = A_REACH + L
    n_past = A_REACH // L if not has_past else 0
    n_var = n_past + 1
    bb = pl.program_id(0)
    tb = pl.program_id(1)
    tail = CONV_W - 1
    if n_steps == 1:
        at_first = at_last = lambda f: f()
    else:
        at_first = pl.when(tb == 0)
        at_last = pl.when(tb == n_steps - 1)

    li = lax.broadcasted_iota(jnp.int32, (L, L), 0)
    lj = lax.broadcasted_iota(jnp.int32, (L, L), 1)
    causal_neg = jnp.where(lj <= li, 0.0, -jnp.inf)
    eye_b = jnp.where(li == lj, 1.0, 0.0).astype(BF16)
    lane_lo = lax.broadcasted_iota(jnp.int32, (L, SLAB), 1) < HEAD_DIM
    ones_slab = jnp.ones((L, SLAB), BF16)
    ri = lax.broadcasted_iota(jnp.int32, (SLAB, SLAB), 0) // HEAD_DIM
    rj = lax.broadcasted_iota(jnp.int32, (SLAB, SLAB), 1) // HEAD_DIM
    smask = jnp.where(ri == rj, 1.0, 0.0)
    pmask = jnp.concatenate([smask, smask], axis=1)
    gi = lax.broadcasted_iota(jnp.int32, (W_GROUP, W_GROUP), 0) // HEAD_DIM
    gj = lax.broadcasted_iota(jnp.int32, (W_GROUP, W_GROUP), 1) // HEAD_DIM
    gmat = jnp.where(gi == gj, 1.0 / HEAD_DIM, 0.0).astype(BF16)
    e64i = lax.broadcasted_iota(jnp.int32, (HEAD_DIM, HEAD_DIM), 0)
    e64j = lax.broadcasted_iota(jnp.int32, (HEAD_DIM, HEAD_DIM), 1)
    eye64 = e64i == e64j
    row_lo = lax.broadcasted_iota(jnp.int32, (SLAB, 1), 0) < HEAD_DIM

    @pl.when((bb == 0) & (tb == 0))
    def _():
        qq = lax.broadcasted_iota(jnp.int32, (L, band), 0)
        kk = lax.broadcasted_iota(jnp.int32, (L, band), 1)
        off = kk - (qq // CHUNK) * CHUNK
        for h in range(N_HEADS):
            row = jnp.broadcast_to(rel_ref[h], (L, ROLL_W))
            tile = pltpu.roll(row, ROLL_W - L, 1, stride=1, stride_axis=0)[:, :band] * LOG2E
            tile = jnp.where(off >= 0, jnp.where(off < A_REACH + CHUNK, tile, -jnp.inf), -jnp.inf)
            for v in range(n_var):
                first_col = (n_past - v) * L
                bias_sc[v * N_HEADS + h] = jnp.where(kk >= first_col, tile, -jnp.inf) if first_col > 0 else tile

    for s in range(seqs):
        @at_first
        def _():
            for p in range(N_PAIRS):
                sblk, cblk = [], []
                for hh in range(2):
                    h = 2 * p + hh
                    z = jnp.zeros((HEAD_DIM, HEAD_DIM), F32)
                    s_h = ret0_ref[s, h]
                    c_h = c0_ref[s, h]
                    n_row = n0_ref[s, h:h + 1, :]
                    n_col = jnp.sum(jnp.where(eye64, jnp.broadcast_to(n_row, eye64.shape), 0.0),
                                    axis=1, keepdims=True)
                    n_rep = jnp.broadcast_to(n_col, (HEAD_DIM, HEAD_DIM))
                    sblk.append(jnp.concatenate([s_h, z] if hh == 0 else [z, s_h], axis=1))
                    cblk.append(jnp.concatenate([c_h, z, n_rep, z] if hh == 0 else [z, c_h, z, n_rep], axis=1))
                spair[s, p] = jnp.concatenate(sblk, axis=0)
                ppair[s, p] = jnp.concatenate(cblk, axis=0)
            m_sc[s, 0:N_HEADS, :] = jnp.broadcast_to(m0_ref[s], (N_HEADS, m_sc.shape[2]))
            win[s, CONV_PAD - tail:CONV_PAD, :] = buf0_ref[s]
            h_sc[s] = h0_ref[s]
            if not has_past:
                vpad[s, 0:A_REACH, :] = jnp.zeros((A_REACH, W_GROUP), BF16)
                kpadt[s, 0:n_past * W_GROUP, :] = jnp.zeros((n_past * W_GROUP, L), BF16)

    g8 = gt_ref[...]
    seg_pos = lax.broadcasted_iota(jnp.int32, g8.shape, 1) % L
    b8 = _seg_scan(jax.nn.log_sigmoid(g8), seg_pos, L, jnp.add, 0.0)
    a8 = g8 - pltpu.roll(b8, N_HEADS, 0)
    cm8 = _seg_scan(a8, seg_pos, L, jnp.maximum, -jnp.inf)
    b8 = pltpu.roll(b8, N_HEADS, 0)

    heads = [(h // 2, h % 2, h) for h in range(N_HEADS)]
    slab = lambda p: slice(p * SLAB, (p + 1) * SLAB)
    seg_of = lambda s: slice(s * L, (s + 1) * L)
    col = lambda s, j: pr_ref[s, :, j * W_GROUP:(j + 1) * W_GROUP]

    def halves(per_head):
        return jnp.concatenate([jnp.where(lane_lo, per_head[2 * p], per_head[2 * p + 1]) for p in range(N_PAIRS)],
                               axis=1)


    def attention_chain(s):
        pa = pa_ref[s]
        kat = kat_ref[:, seg_of(s)]
        v_new = pa[:, 2 * W_GROUP:].astype(BF16)
        if has_past:
            kpast = pastk_ref[s].astype(BF16)
            vband = jnp.concatenate([pastv_ref[s].astype(BF16), v_new], axis=0)
            bias_at = lambda h: bias_sc[h]
        else:
            kpadt[s, pl.ds(pl.multiple_of((tb + n_past) * W_GROUP, W_GROUP), W_GROUP), :] = kat
            kwin = kpadt[s, pl.ds(pl.multiple_of(tb * W_GROUP, W_GROUP), (n_past + 1) * W_GROUP), :]
            start = pl.multiple_of(tb * L, L)
            vpad[s, pl.ds(A_REACH + start, L), :] = v_new
            vband = vpad[s, pl.ds(start, band), :]
            var = jnp.minimum(tb, n_var - 1) * N_HEADS
            bias_at = lambda h: bias_sc[var + h]
        qa = (pa[:, 0:W_GROUP] * (HEAD_DIM ** -0.5 * LOG2E)).astype(BF16)
        sc = {}
        for p, hh, h in heads:
            q_slab = qa[:, slab(p)]
            if has_past:
                k_slab = kpast[:, slab(p)]
                lane = lax.broadcasted_iota(jnp.int32, k_slab.shape, 1)
                lane_h = lane < HEAD_DIM if hh == 0 else lane >= HEAD_DIM
                parts = [_dot_nt(q_slab, jnp.where(lane_h, k_slab, jnp.zeros_like(k_slab))),
                         jnp.dot(q_slab, _head_rows(kat, h, hh), preferred_element_type=F32)]
            else:
                parts = [jnp.dot(q_slab, _head_rows(kwin[i * W_GROUP:(i + 1) * W_GROUP], h, hh),
                                 preferred_element_type=F32) for i in range(n_past + 1)]
            sc[h] = jnp.concatenate(parts, axis=1)
        yield
        pe, den = {}, {}
        for _, _, h in heads:
            sh = sc[h] + bias_at(h)
            pe[h] = jnp.exp2(sh - jnp.max(sh, axis=1, keepdims=True))
            den[h] = jnp.sum(pe[h], axis=1, keepdims=True)
        yield
        res = {h: _dot(pe[h], vband)[:, slab(p)] for p, _, h in heads}
        yield
        y_ref[s, :, 0:W_GROUP] = halves({h: res[h] / den[h] for _, _, h in heads})

    def retention_chain(s):
        qr = (col(s, PR_QB) * cos_ref[...] + col(s, PR_QBS) * sin_ref[...]).astype(BF16)
        kbt = kbt_ref[:, seg_of(s)]
        half = HEAD_DIM // 2
        kbt_sw = jnp.concatenate([kbt[h * HEAD_DIM + o:h * HEAD_DIM + o + half]
                                  for h in range(N_HEADS) for o in (half, 0)], axis=0)
        krt = (kbt * cost_ref[:, seg_of(s)] + kbt_sw * sint_ref[:, seg_of(s)]) * HEAD_DIM ** -0.5
        krt_b = krt.astype(BF16)
        kdt_b = (krt * kdect_ref[...]).astype(BF16)
        vb = col(s, PR_VB).astype(BF16)
        raw = {h: jnp.dot(qr[:, slab(p)], _head_rows(krt_b, h, hh), preferred_element_type=F32)
               for p, hh, h in heads}
        s0 = {p: spair[s, p] for p in range(N_PAIRS)}
        inter = {p: _dot(qr[:, slab(p)], s0[p]) for p in range(N_PAIRS)}
        add = {p: _dot(kdt_b[slab(p)], vb[:, slab(p)]) for p in range(N_PAIRS)}
        yield
        scores = {h: raw[h] * dec_ref[h] for _, _, h in heads}
        for p in range(N_PAIRS):
            spair[s, p] = s0[p] * sdec_ref[p] + add[p] * smask
        yield
        o = {h: _dot(scores[h], vb[:, slab(p)]) for p, _, h in heads}
        yield
        ob = halves(o) + jnp.concatenate([inter[p] for p in range(N_PAIRS)], axis=1) * qdec_ref[...]
        mu = _dot(ob, gmat)
        yield
        oc = ob - mu
        var = _dot(oc * oc, gmat)
        yield
        y_ref[s, :, W_GROUP:2 * W_GROUP] = oc * lax.rsqrt(var + LN_EPS) * _silu(col(s, PR_GB))

    def mlstm_chain(s):
        a4 = a8[0:N_HEADS, seg_of(s)]
        b4 = b8[0:N_HEADS, seg_of(s)]
        m0c = m_sc[s, 0:N_HEADS, 0:1]
        big_m = jnp.maximum(m0c, cm8[0:N_HEADS, seg_of(s)])
        m4 = b4 + big_m
        w0_4 = jnp.exp(m0c - big_m)
        m_last = big_m[:, L - 1:L]
        wrow4 = jnp.exp(a4 - m_last)
        kct = kct_ref[:, seg_of(s)] * HEAD_DIM ** -0.5
        kct_b = kct.astype(BF16)
        kw_b = (kct * _rep_heads(wrow4, HEAD_DIM)).astype(BF16)
        qc = col(s, PR_QC).astype(BF16)
        vc = col(s, PR_VC).astype(BF16)
        v_aug = {p: jnp.concatenate([vc[:, slab(p)], ones_slab], axis=1) for p in range(N_PAIRS)}
        p0 = {p: ppair[s, p] for p in range(N_PAIRS)}
        w0_bc = _col_bcast(eye_b, _rep_heads(w0_4, HEAD_DIM))
        floor_bc = _col_bcast(eye_b, _rep_heads(jnp.exp(-m4), HEAD_DIM))
        m_bc = {h: _col_bcast(eye_b, jnp.broadcast_to(big_m[h:h + 1], (L, L))) for _, _, h in heads}
        raw = {h: jnp.dot(qc[:, slab(p)], _head_rows(kct_b, h, hh), preferred_element_type=F32)
               for p, hh, h in heads}
        inter = {p: _dot(qc[:, slab(p)], p0[p]) for p in range(N_PAIRS)}
        add = {p: _dot(kw_b[slab(p)], v_aug[p]) for p in range(N_PAIRS)}
        yield
        qk = {h: raw[h] * jnp.exp(a4[h:h + 1] - m_bc[h] + causal_neg) for _, _, h in heads}
        for p in range(N_PAIRS):
            w_state = jnp.where(row_lo, w0_4[2 * p:2 * p + 1, L - 1:L], w0_4[2 * p + 1:2 * p + 2, L - 1:L])
            ppair[s, p] = p0[p] * w_state + add[p] * pmask
        m_sc[s, 0:N_HEADS, :] = jnp.broadcast_to(m4[:, L - 1:L], (N_HEADS, m_sc.shape[2]))
        yield
        res = {h: _dot(qk[h], v_aug[p]) for p, _, h in heads}
        yield
        hc = []
        for p in range(N_PAIRS):
            w0 = w0_bc[:, slab(p)]
            num = jnp.where(lane_lo, res[2 * p][:, 0:SLAB], res[2 * p + 1][:, 0:SLAB]) + inter[p][:, 0:SLAB] * w0
            den = jnp.where(lane_lo, res[2 * p][:, SLAB:], res[2 * p + 1][:, SLAB:]) + inter[p][:, SLAB:] * w0
            hc.append(num / jnp.maximum(jnp.abs(den), floor_bc[:, slab(p)]))
        hc = jnp.concatenate(hc, axis=1)
        mu = _dot(hc, gmat)
        yield
        hcc = hc - mu
        var = _dot(hcc * hcc, gmat)
        yield
        y_ref[s, :, 2 * W_GROUP:3 * W_GROUP] = hcc * lax.rsqrt(var + LN_EPS) * jax.nn.sigmoid(col(s, PR_OC))

    def rglru_chain(s):
        xd = col(s, PR_XD)
        win[s, CONV_PAD:CONV_PAD + L, :] = xd
        xc = cb_ref[...]
        for j in range(CONV_W):
            off = CONV_PAD - tail + j
            xc = xc + win[s, off:off + L, :] * cw_ref[j:j + 1, :]
        win[s, CONV_PAD - tail:CONV_PAD, :] = xd[L - tail:, :]
        r_pre = _dot(xc, wa_ref[...])
        i_pre = _dot(xc, wi_ref[...])
        yield
        r = jax.nn.sigmoid(r_pre + ba_ref[...])
        i = jax.nn.sigmoid(i_pre + bi_ref[...])
        neg_lam = -lam_ref[...]
        softplus = jnp.maximum(neg_lam, 0.0) + jnp.log1p(jnp.exp(-jnp.abs(neg_lam)))
        log_a = -LRU_C * r * softplus
        a = jnp.exp(log_a)
        u = jnp.sqrt(-jnp.tanh(log_a) * (jnp.exp(2.0 * log_a) + 1.0)) * (i * xc)
        st = 1
        while st < L:
            u = a * _shift_rows(u, st, 0.0) + u
            a = a * _shift_rows(a, st, 1.0)
            st *= 2
            if st in (8, 64):
                yield
        hseq = u + a * h_sc[s]
        h_sc[s] = hseq[L - 1:L, :]
        y_ref[s, :, 3 * W_GROUP:] = hseq * jax.nn.gelu(col(s, PR_YD))

    _run_interleaved([chain(s) for chain in (attention_chain, retention_chain, mlstm_chain, rglru_chain)
                      for s in range(seqs)], lag=1 if seqs == 1 else 0)

    for s in range(seqs):
        @at_last
        def _():
            for p in range(N_PAIRS):
                sp = spair[s, p]
                pp = ppair[s, p]
                for hh in range(2):
                    h = 2 * p + hh
                    rs = slice(hh * HEAD_DIM, (hh + 1) * HEAD_DIM)
                    ret_out[s, h] = sp[rs, rs]
                    c_out[s, h] = pp[rs, rs]
                    n_rep = pp[rs, SLAB + hh * HEAD_DIM:SLAB + (hh + 1) * HEAD_DIM]
                    n_out[s, h:h + 1, :] = jnp.sum(jnp.where(eye64, n_rep, 0.0), axis=0, keepdims=True)
            m_out[s] = m_sc[s, 0:N_HEADS, 0:1]
            h_out[s] = h_sc[s]
            buf_out[s] = pr_ref[s, L - tail:, PR_XD * W_GROUP:(PR_XD + 1) * W_GROUP]


def _mixer_kernel_fresh(pa_ref, pr_ref, kat_ref, kbt_ref, kct_ref, gt_ref, *rest, **static):
    _mixer_kernel(pa_ref, pr_ref, kat_ref, kbt_ref, kct_ref, gt_ref, None, None, *rest, **static)


def _mixer_geometry(t, has_past):
    if has_past:
        assert t == CHUNK, "sequences with a carried cache are expected to be a single chunk"
        return t, MIX_SEQS
    assert t % MIX_BLOCK == 0 and A_REACH % MIX_BLOCK == 0
    return MIX_BLOCK, 1


def _mixer_tables(pos, blk, seqs):
    cos, sin = _rotary_tables(pos)
    return (cos, sin, jnp.tile(cos.T, (1, seqs)), jnp.tile(sin.T, (1, seqs))) + _retention_tables(blk)


def _mixer(pa, pr, kat, kbt, kct, gt, past, rel_rows, tables, rec, lw):
    ret0, c0, n0, m0, h0, buf0, rec_layer = rec
    m0, h0 = m0[..., None], h0[:, :, None, :]
    bsz, t, _ = pa.shape
    has_past = past is not None
    blk, seqs = _mixer_geometry(t, has_past)
    nb = t // blk
    assert bsz % seqs == 0
    band = A_REACH + blk
    n_var = 1 if has_past else A_REACH // blk + 1
    tail = CONV_W - 1

    tok3 = lambda w: pl.BlockSpec((seqs, blk, w), lambda b, c: (b, c, 0))
    trn = lambda r: pl.BlockSpec((r, seqs * blk), lambda b, c: (0, b * nb + c))
    per_seq = lambda *shape: pl.BlockSpec((seqs,) + shape, lambda b, c: (b,) + (0,) * len(shape))
    const = lambda *shape: pl.BlockSpec(shape, lambda b, c: (0,) * len(shape))
    carried = lambda *shape: pl.BlockSpec((None, seqs) + shape, lambda b, c: (rec_layer, b) + (0,) * len(shape))
    past_specs, past_args = [], ()
    if has_past:
        layer = past[2]
        cache = pl.BlockSpec((None, seqs, A_REACH, W_GROUP), lambda b, c: (layer, b, 0, 0))
        past_specs, past_args = [cache, cache], past[:2]
    in_specs = [tok3(3 * W_GROUP), tok3(N_PR * W_GROUP), trn(W_GROUP), trn(W_GROUP), trn(W_GROUP), trn(N_GATES),
                *past_specs, const(N_HEADS, 1, ROLL_W),
                pl.BlockSpec((blk, W_GROUP), lambda b, c: (c, 0)), pl.BlockSpec((blk, W_GROUP), lambda b, c: (c, 0)),
                pl.BlockSpec((W_GROUP, seqs * blk), lambda b, c: (0, c)),
                pl.BlockSpec((W_GROUP, seqs * blk), lambda b, c: (0, c)),
                const(N_HEADS, blk, blk), const(blk, W_GROUP), const(W_GROUP, blk), const(N_PAIRS, SLAB, SLAB),
                carried(N_HEADS, HEAD_DIM, HEAD_DIM), carried(N_HEADS, HEAD_DIM, HEAD_DIM),
                carried(N_HEADS, HEAD_DIM), carried(N_HEADS, 1),
                const(CONV_W, W_GROUP), const(1, W_GROUP), const(W_GROUP, W_GROUP), const(1, W_GROUP),
                const(W_GROUP, W_GROUP), const(1, W_GROUP), const(1, W_GROUP),
                carried(tail, W_GROUP), carried(1, W_GROUP)]
    out_shape = (jax.ShapeDtypeStruct((bsz, t, 4 * W_GROUP), F32),
                 jax.ShapeDtypeStruct((bsz, N_HEADS, HEAD_DIM, HEAD_DIM), F32),
                 jax.ShapeDtypeStruct((bsz, N_HEADS, HEAD_DIM, HEAD_DIM), F32),
                 jax.ShapeDtypeStruct((bsz, N_HEADS, HEAD_DIM), F32),
                 jax.ShapeDtypeStruct((bsz, N_HEADS, 1), F32),
                 jax.ShapeDtypeStruct((bsz, 1, W_GROUP), F32),
                 jax.ShapeDtypeStruct((bsz, tail, W_GROUP), F32))
    out_specs = (tok3(4 * W_GROUP), per_seq(N_HEADS, HEAD_DIM, HEAD_DIM), per_seq(N_HEADS, HEAD_DIM, HEAD_DIM),
                 per_seq(N_HEADS, HEAD_DIM), per_seq(N_HEADS, 1), per_seq(1, W_GROUP), per_seq(tail, W_GROUP))
    scratch = [pltpu.VMEM((n_var * N_HEADS, blk, band), F32),
               pltpu.VMEM((seqs, N_PAIRS, SLAB, SLAB), F32), pltpu.VMEM((seqs, N_PAIRS, SLAB, 2 * SLAB), F32),
               pltpu.VMEM((seqs, 8, 128), F32), pltpu.VMEM((seqs, CONV_PAD + blk, W_GROUP), F32),
               pltpu.VMEM((seqs, 1, W_GROUP), F32)]
    if not has_past:
        scratch += [pltpu.VMEM((seqs, (nb + A_REACH // blk) * W_GROUP, blk), BF16),
                    pltpu.VMEM((seqs, A_REACH + t, W_GROUP), BF16)]
    body = _mixer_kernel if has_past else _mixer_kernel_fresh
    return pl.pallas_call(
        functools.partial(body, blk=blk, seqs=seqs, n_steps=nb, has_past=has_past),
        out_shape=out_shape,
        grid=(bsz // seqs, nb),
        in_specs=in_specs,
        out_specs=out_specs,
        scratch_shapes=scratch,
        compiler_params=_cparams("arbitrary", "arbitrary"),
        name="mixer",
    )(pa, pr, kat, kbt, kct, gt, *past_args, rel_rows, *tables,
      ret0, c0, n0, m0, lw['conv_w'], lw['conv_b'], lw['lru_wa'], lw['lru_ba'], lw['lru_wi'], lw['lru_bi'],
      lw['lru_lambda'], buf0, h0)


def _post_kernel(x_ref, y_ref, wo_ref, g1_ref, b1_ref, wq_ref, mk_ref, mv_ref, xo_ref,
                 g2_ref, b2_ref, o_ref, att_sc, *, alpha, seqs, subtiles):
    rows_sub = x_ref.shape[0] // subtiles
    rows = rows_sub // seqs
    head = lambda h: slice(h * HEAD_DIM, (h + 1) * HEAD_DIM)

    def subtile_chain(i):
        rs = slice(i * rows_sub, (i + 1) * rows_sub)
        x = x_ref[rs, :]
        mix = _dot(y_ref[rs, :], wo_ref[...])
        yield
        x2 = _layer_norm(alpha * x + mix, g1_ref[...], b1_ref[...])
        yield
        q = _dot(x2, wq_ref[...]).astype(BF16)
        yield

        def attend(s, h):
            qs = slice(s * rows, (s + 1) * rows)
            sc = _dot_nt(q[qs, head(h)], mk_ref[s, :, head(h)]) * HEAD_DIM ** -0.5
            yield
            p = jnp.exp(sc - jnp.max(sc, axis=1, keepdims=True))
            den = jnp.sum(p, axis=1, keepdims=True)
            yield
            o = _dot(p, mv_ref[s, :, head(h)])
            yield
            att_sc[i * rows_sub + s * rows:i * rows_sub + (s + 1) * rows, head(h)] = o / den

        yield from _lockstep([attend(s, h) for s in range(seqs) for h in range(N_HEADS)])
        out = _dot(att_sc[rs, :], xo_ref[...])
        yield
        o_ref[rs, :] = _layer_norm(alpha * x2 + out, g2_ref[...], b2_ref[...])

    _run_interleaved([subtile_chain(i) for i in range(subtiles)], lag=2)


def _post(x, y, wo, wq, xo, wlayer, g1, b1, g2, b2, mk, mv, layer, alpha, seq_len, casts=()):
    n, d = x.shape
    n_mem, dx = mk.shape[2], mk.shape[3]
    if seq_len >= POST_SUBTILES * TOKEN_TILE:
        subtiles, seqs = POST_SUBTILES, 1
        tm = subtiles * TOKEN_TILE
        mem = pl.BlockSpec((None, 1, n_mem, dx), lambda i: (layer, i // (seq_len // tm), 0, 0))
    else:
        tm = TOKEN_TILE
        subtiles, seqs = 1, tm // seq_len
        mem = pl.BlockSpec((None, seqs, n_mem, dx), lambda i: (layer, i, 0, 0))
    assert n % tm == 0 and tm % (subtiles * seqs) == 0
    tok = lambda w: pl.BlockSpec((tm, w), lambda i: (i, 0))
    plans = [_cast_plan(stack, lyr, n // tm) for stack, lyr in casts]
    out = pl.pallas_call(
        _with_casts(functools.partial(_post_kernel, alpha=alpha, seqs=seqs, subtiles=subtiles), 11, 1, len(casts)),
        out_shape=(jax.ShapeDtypeStruct((n, d), F32), *[p[2] for p in plans]),
        grid=(n // tm,),
        in_specs=[tok(d), tok(y.shape[1]),
                  _resident(wo.shape[1:], wlayer), _resident((1, d)), _resident((1, d)),
                  _resident(wq.shape[1:], wlayer), mem, mem, _resident(xo.shape[1:], wlayer),
                  _resident((1, d)), _resident((1, d)), *[p[0] for p in plans]],
        out_specs=(tok(d), *[p[1] for p in plans]),
        scratch_shapes=[pltpu.VMEM((tm, dx), F32)],
        compiler_params=_cparams("parallel"),
        name="post",
    )(x, y, wo, g1, b1, wq, mk, mv, xo, g2, b2, *[stack for stack, _ in casts])
    return out[0], out[1:]


def _mem_kv_kernel(x_ref, w_ref, o_ref):
    o_ref[...] = _dot(x_ref[...], w_ref[...])


def _mem_kv(mem, w):
    n, d = mem.shape
    tm = min(TOKEN_TILE, n)
    return pl.pallas_call(
        _mem_kv_kernel,
        out_shape=jax.ShapeDtypeStruct((n, w.shape[1]), F32),
        grid=(n // tm,),
        in_specs=[pl.BlockSpec((tm, d), lambda i: (i, 0)), _resident(w.shape)],
        out_specs=pl.BlockSpec((tm, w.shape[1]), lambda i: (i, 0)),
        compiler_params=_cparams("parallel"),
        name="mem_kv",
    )(mem, w)


def _rotary_tables(pos):
    half = HEAD_DIM // 2
    inv = jnp.exp(-jnp.log(10000.0) * jnp.arange(half, dtype=F32) / half)
    ang = pos.astype(F32)[:, None] * inv[None, :]
    cos, sin = jnp.cos(ang), jnp.sin(ang)
    cos_full = jnp.tile(jnp.concatenate([cos, cos], -1), (1, N_HEADS))
    sin_full = jnp.tile(jnp.concatenate([-sin, sin], -1), (1, N_HEADS))
    return cos_full, sin_full


def _retention_tables(blk):
    log_g = jnp.log1p(-jnp.exp2(-5.0 - jnp.arange(N_HEADS, dtype=F32)))
    idx = jnp.arange(blk, dtype=F32)
    diff = idx[:, None] - idx[None, :]
    dec = jnp.exp(jnp.where((diff >= 0)[None], diff[None] * log_g[:, None, None], -jnp.inf))
    q_dec = jnp.exp((idx[:, None] + 1.0) * log_g[None, :])
    k_dec = jnp.exp((blk - 1.0 - idx)[:, None] * log_g[None, :])
    s_dec = jnp.exp(blk * log_g)
    lanes = lambda tbl: jnp.repeat(tbl, HEAD_DIM, axis=1)
    pair = lambda p: jnp.kron(jnp.diag(s_dec[2 * p:2 * p + 2]), jnp.ones((HEAD_DIM, HEAD_DIM), F32))
    return dec, lanes(q_dec), lanes(k_dec).T, jnp.stack([pair(p) for p in range(N_PAIRS)])


def _rel_bias_rows(rel_bias, blk):
    idx = np.clip(A_REACH + blk - np.arange(ROLL_W), -REL_CLIP, REL_CLIP) + REL_CLIP
    return rel_bias[:, idx][:, None, :]


def _block_diag(w):
    h, c, _ = w.shape
    eye = jnp.eye(h, dtype=w.dtype)
    return (eye[:, None, :, None] * w[:, :, None, :]).reshape(h * c, h * c)


def _swap_perm():
    j = np.arange(W_GROUP)
    return (j // HEAD_DIM) * HEAD_DIM + (j % HEAD_DIM + HEAD_DIM // 2) % HEAD_DIM


def _layer(x, tables, mem, past, rec, lw, ffn1, ffn2, alpha, proj_casts=(), post_casts=()):
    bsz, t, d = x.shape
    n = bsz * t
    g, b = lw['ln_g'], lw['ln_b']

    x1 = _ffn_ln(x.reshape(n, d), *ffn1, g[0:1], b[0:1], alpha)
    (pa, pr, kat, kbt, kct, gt), proj_cast_out = _in_proj(x1, lw['w_in_a'], lw['w_in_r'], lw['w_in_t'],
                                                          lw['b_in_a'], lw['b_in_r'], lw['b_in_t'], proj_casts)
    if ffn2 is None:
        ffn2 = proj_cast_out[:3]
    pa = pa.reshape(bsz, t, 3 * W_GROUP)
    pr = pr.reshape(bsz, t, N_PR * W_GROUP)
    blk, _ = _mixer_geometry(t, past is not None)
    y, ret_new, c_new, n_new, m_new, h_new, conv_new = _mixer(
        pa, pr, kat, kbt, kct, gt, past, _rel_bias_rows(lw['rel_bias'], blk), tables, rec, lw)

    x3, post_cast_out = _post(x1, y.reshape(n, 4 * W_GROUP), lw['w_out'], lw['x_wq'], lw['x_wo'], lw['layer'],
                              g[1:2], b[1:2], g[2:3], b[2:3], *mem, alpha, t, post_casts)
    x4 = _ffn_ln(x3, *ffn2, g[3:4], b[3:4], alpha)

    heads = lambda a: a.reshape(bsz, -1, N_HEADS, HEAD_DIM)
    keep = min(A_REACH, t)
    a_k = heads(pa[:, t - keep:, W_GROUP:2 * W_GROUP])
    a_v = heads(pa[:, t - keep:, 2 * W_GROUP:])
    state = (a_k, a_v, ret_new, c_new, n_new, m_new[:, :, 0], h_new[:, 0, :], conv_new)
    return x4.reshape(bsz, t, d), state, proj_cast_out, post_cast_out


def kernel(x_prompt, x_sample, mem_prompt, cache_a_k, cache_a_v, state_ret, state_mlstm_c, state_mlstm_n,
           state_mlstm_m, state_lru_h, state_conv, cache_mem_k, cache_mem_v, ln_g, ln_b,
           ffn1_gate, ffn1_up, ffn1_down, ffn2_gate, ffn2_up, ffn2_down, w_in, b_in, a_rel_bias,
           conv_w, conv_b, lru_wa, lru_ba, lru_wi, lru_bi, lru_lambda, w_out, x_wq, x_wk, x_wv, x_wo):
    depth = ln_g.shape[0]
    alpha = (2.0 * depth) ** 0.25
    row = lambda v: v[None, :]
    grp = lambda a, j: a[..., j * W_GROUP:(j + 1) * W_GROUP]
    perm = _swap_perm()
    QA, KA, VA, QB, KB, VB, GB, QC, KC, VC, OC, XD, YD = range(13)

    def layer_weights(l):
        w, bias = w_in[l], b_in[l]
        nat = lambda a: jnp.concatenate(
            [grp(a, QB), grp(a, QB)[..., perm], grp(a, VB), grp(a, GB), grp(a, QC), grp(a, VC), grp(a, OC),
             grp(a, XD), grp(a, YD)], axis=-1)
        trn = lambda a: jnp.concatenate([grp(a, KA), grp(a, KB), grp(a, KC), a[..., 13 * W_GROUP:]], axis=-1)
        return {'ln_g': ln_g[l], 'ln_b': ln_b[l],
                'layer': l, **stacks,
                'w_in_a': w[:, :3 * W_GROUP].astype(BF16), 'w_in_r': nat(w).astype(BF16),
                'w_in_t': trn(w).T.astype(BF16),
                'b_in_a': row(bias[:3 * W_GROUP]), 'b_in_r': row(nat(bias)), 'b_in_t': trn(bias)[:, None],
                'rel_bias': a_rel_bias[l],
                'conv_w': conv_w[l], 'conv_b': row(conv_b[l]),
                'lru_wa': _block_diag(lru_wa[l]).astype(BF16), 'lru_ba': row(lru_ba[l]),
                'lru_wi': _block_diag(lru_wi[l]).astype(BF16), 'lru_bi': row(lru_bi[l]),
                'lru_lambda': row(lru_lambda[l]),
                'x_wkv': jnp.concatenate([x_wk[l], x_wv[l]], axis=1).astype(BF16)}

    stacks = {name: w.astype(BF16) for name, w in (('w_out', w_out), ('x_wq', x_wq), ('x_wo', x_wo))}
    ffn1_f32, ffn2_f32 = (ffn1_gate, ffn1_up, ffn1_down), (ffn2_gate, ffn2_up, ffn2_down)
    ffn1_w = {0: tuple(w[0].astype(BF16) for w in ffn1_f32)}
    ffn2_w = {}
    weights = [layer_weights(l) for l in range(depth)]
    dx = x_wk.shape[2]

    bp, tp, d = x_prompt.shape
    n_mem = mem_prompt.shape[1]
    h = x_prompt
    prompt_states = []
    tables = _mixer_tables(jnp.arange(tp), *_mixer_geometry(tp, False))
    rec0 = (jnp.zeros((1, bp, N_HEADS, HEAD_DIM, HEAD_DIM), F32), jnp.zeros((1, bp, N_HEADS, HEAD_DIM, HEAD_DIM), F32),
            jnp.zeros((1, bp, N_HEADS, HEAD_DIM), F32), jnp.zeros((1, bp, N_HEADS), F32),
            jnp.zeros((1, bp, W_GROUP), F32), jnp.zeros((1, bp, CONV_W - 1, W_GROUP), F32), 0)
    for l in range(depth):
        lw = weights[l]
        mkv = _mem_kv(mem_prompt.reshape(bp * n_mem, d), lw['x_wkv'])
        mk = mkv[:, :dx].reshape(bp, n_mem, dx)
        mv = mkv[:, dx:].reshape(bp, n_mem, dx)
        post_casts = [(w, l + 1) for w in ffn1_f32] if l + 1 < depth else []
        h, st, ffn2_w[l], cast_next = _layer(h, tables, (mk[None], mv[None], 0), None, rec0, lw, ffn1_w[l], None,
                                             alpha, [(w, l) for w in ffn2_f32], post_casts)
        if post_casts:
            ffn1_w[l + 1] = cast_next
        prompt_states.append(st + (mk.reshape(bp, n_mem, -1, HEAD_DIM), mv.reshape(bp, n_mem, -1, HEAD_DIM)))
    y_prompt = h
    prompt_out = [jnp.stack(f) for f in zip(*prompt_states)]

    bs, ts, _ = x_sample.shape
    h = x_sample
    sample_states = []
    tables = _mixer_tables(PAST_LEN + jnp.arange(ts), *_mixer_geometry(ts, True))
    past_k_all = cache_a_k.reshape(depth, bs, A_REACH, W_GROUP)
    past_v_all = cache_a_v.reshape(depth, bs, A_REACH, W_GROUP)
    mem_k_all = cache_mem_k.reshape(depth, bs, n_mem, dx)
    mem_v_all = cache_mem_v.reshape(depth, bs, n_mem, dx)
    for l in range(depth):
        rec = (state_ret, state_mlstm_c, state_mlstm_n, state_mlstm_m, state_lru_h, state_conv, l)
        h, st, _, _ = _layer(h, tables, (mem_k_all, mem_v_all, l), (past_k_all, past_v_all, l), rec, weights[l],
                             ffn1_w[l], ffn2_w[l], alpha)
        sample_states.append(st)
    y_sample = h
    sample_out = [jnp.stack(f) for f in zip(*sample_states)]

    return (y_prompt, y_sample, *prompt_out, *sample_out)
```

```python
import functools

import numpy as np
import jax
import jax.numpy as jnp
from jax import lax
from jax.experimental import pallas as pl
from jax.experimental.pallas import tpu as pltpu

F32 = jnp.float32
BF16 = jnp.bfloat16

CHUNK = 64
HEAD_DIM = 64
N_HEADS = 4
W_GROUP = N_HEADS * HEAD_DIM
SLAB = 2 * HEAD_DIM
N_PAIRS = N_HEADS // 2
A_BAND_CHUNKS = 8
A_REACH = A_BAND_CHUNKS * CHUNK
REL_CLIP = 128
CONV_W = 4
LRU_C = 8.0
LN_EPS = 1e-5
LOG2E = 1.4426950408889634
N_GATES = 2 * N_HEADS
PAST_LEN = 4096
ROLL_W = 1024
CONV_PAD = 8

V7X_VMEM_LIMIT_BYTES = 56 * 1024 * 1024
TOKEN_TILE = 512
MIX_BLOCK = 256
MIX_SEQS = 4
POST_SUBTILES = 2

PR_QB, PR_QBS, PR_VB, PR_GB, PR_QC, PR_VC, PR_OC, PR_XD, PR_YD = range(9)
N_PR = 9


def _cparams(*sem):
    return pltpu.CompilerParams(dimension_semantics=sem, vmem_limit_bytes=V7X_VMEM_LIMIT_BYTES)


def _dot(a, b):
    return jnp.dot(a.astype(BF16), b.astype(BF16), preferred_element_type=F32)


def _dot_nt(a, b):
    return lax.dot_general(a.astype(BF16), b.astype(BF16), (((1,), (1,)), ((), ())), preferred_element_type=F32)


def _layer_norm(x, g, b):
    mu = jnp.mean(x, -1, keepdims=True)
    xc = x - mu
    var = jnp.mean(xc * xc, -1, keepdims=True)
    return xc * lax.rsqrt(var + LN_EPS) * g + b


def _silu(x):
    return x * jax.nn.sigmoid(x)


def _lockstep(chains):
    while chains:
        chains = [c for c in chains if next(c, "done") != "done"]
        yield


def _run_interleaved(chains, lag=0):
    pending, live, rnd = list(chains), [], 0
    while pending or live:
        while pending and rnd >= lag * (len(chains) - len(pending)):
            live.append(pending.pop(0))
        live = [c for c in live if next(c, "done") != "done"]
        rnd += 1


def _resident(shape, layer=None):
    nd = len(shape)
    if layer is None:
        return pl.BlockSpec(shape, lambda *_: (0,) * nd, pipeline_mode=pl.Buffered(1))
    return pl.BlockSpec((None,) + tuple(shape), lambda *_: (layer,) + (0,) * nd, pipeline_mode=pl.Buffered(1))


def _cast_plan(stack, layer, n_steps):
    _, r, c = stack.shape
    nb = next(k for k in range(n_steps, 0, -1) if n_steps % k == 0 and r % k == 0 and (r // k) % 16 == 0)
    every = n_steps // nb
    return (pl.BlockSpec((None, r // nb, c), lambda i: (layer, i // every, 0)),
            pl.BlockSpec((r // nb, c), lambda i: (i // every, 0)),
            jax.ShapeDtypeStruct((r, c), BF16))


def _with_casts(body, n_in, n_out, n_cast):
    def wrapped(*refs):
        ins, rest = refs[:n_in + n_cast], refs[n_in + n_cast:]
        outs, scratch = rest[:n_out + n_cast], rest[n_out + n_cast:]
        body(*ins[:n_in], *outs[:n_out], *scratch)
        for src, dst in zip(ins[n_in:], outs[n_out:]):
            dst[...] = src[...].astype(BF16)
    return wrapped


def _ffn_ln_kernel(x_ref, wg_ref, wu_ref, wd_ref, g_ref, b_ref, o_ref, *, alpha):
    x = x_ref[...]
    xb = x.astype(BF16)
    gate = jnp.dot(xb, wg_ref[...], preferred_element_type=F32)
    up = jnp.dot(xb, wu_ref[...], preferred_element_type=F32)
    h = (_silu(gate) * up).astype(BF16)
    y = jnp.dot(h, wd_ref[...], preferred_element_type=F32)
    o_ref[...] = _layer_norm(alpha * x + 0.5 * y, g_ref[...], b_ref[...])


def _ffn_ln(x, wg, wu, wd, g, b, alpha):
    n, d = x.shape
    dff = wg.shape[1]
    tm = TOKEN_TILE
    return pl.pallas_call(
        functools.partial(_ffn_ln_kernel, alpha=alpha),
        out_shape=jax.ShapeDtypeStruct((n, d), F32),
        grid=(n // tm,),
        in_specs=[pl.BlockSpec((tm, d), lambda i: (i, 0)),
                  _resident((d, dff)), _resident((d, dff)), _resident((dff, d)),
                  _resident((1, d)), _resident((1, d))],
        out_specs=pl.BlockSpec((tm, d), lambda i: (i, 0)),
        compiler_params=_cparams("parallel"),
        name="ffn_ln",
    )(x, wg, wu, wd, g, b)


def _in_proj_kernel(x_ref, wn_ref, wt_ref, bn_ref, bt_ref, pq_ref, tr_ref):
    xb = x_ref[...].astype(BF16)
    pq_ref[...] = jnp.dot(xb, wn_ref[...], preferred_element_type=F32) + bn_ref[...]
    tr_ref[...] = _dot_nt(wt_ref[...], xb) + bt_ref[...]


def _in_proj(x, wn, wt, bn, bt, casts=()):
    n, d = x.shape
    cn, ct = wn.shape[1], wt.shape[0]
    tm = TOKEN_TILE
    tok = lambda w: pl.BlockSpec((tm, w), lambda i: (i, 0))
    trn = lambda r: pl.BlockSpec((r, tm), lambda i: (0, i))
    plans = [_cast_plan(stack, layer, n // tm) for stack, layer in casts]
    out = pl.pallas_call(
        _with_casts(_in_proj_kernel, 5, 2, len(casts)),
        out_shape=(jax.ShapeDtypeStruct((n, cn), F32), jax.ShapeDtypeStruct((ct, n), F32), *[p[2] for p in plans]),
        grid=(n // tm,),
        in_specs=[tok(d), _resident((d, cn)), _resident((ct, d)), _resident((1, cn)), _resident((ct, 1)),
                  *[p[0] for p in plans]],
        out_specs=(tok(cn), trn(ct), *[p[1] for p in plans]),
        compiler_params=_cparams("parallel"),
        name="in_proj",
    )(x, wn, wt, bn, bt, *[stack for stack, _ in casts])
    return out[:2], out[2:]


def _head_rows(xt, h, pos):
    xh = xt[h * HEAD_DIM:(h + 1) * HEAD_DIM]
    z = jnp.zeros_like(xh)
    return jnp.concatenate([xh, z] if pos == 0 else [z, xh], axis=0)


def _col_bcast(eye_b, rows):
    hi = rows.astype(BF16)
    lo = (rows - hi.astype(F32)).astype(BF16)
    dn = (((1,), (1,)), ((), ()))
    return (lax.dot_general(eye_b, hi, dn, preferred_element_type=F32)
            + lax.dot_general(eye_b, lo, dn, preferred_element_type=F32))


def _rep_heads(rows4, n):
    return jnp.concatenate([jnp.broadcast_to(rows4[h:h + 1], (n, rows4.shape[1])) for h in range(N_HEADS)], axis=0)


def _seg_scan(x, seg_pos, seg_len, op, fill):
    s = 1
    while s < seg_len:
        x = op(x, jnp.where(seg_pos >= s, pltpu.roll(x, s, 1), fill))
        s *= 2
    return x


def _shift_rows(x, s, fill):
    if s % 8 == 0:
        return jnp.concatenate([jnp.full((s, x.shape[1]), fill, x.dtype), x[:x.shape[0] - s]], axis=0)
    row = lax.broadcasted_iota(jnp.int32, x.shape, 0)
    return jnp.where(row >= s, pltpu.roll(x, s, 0), fill)


def _mixer_kernel(pq_ref, tr_ref, pastk_ref, pastv_ref, rel_ref,
                  cs_ref, cst_ref, dec_ref, qdec_ref, kdect_ref, sdec_ref,
                  ret0_ref, c0_ref, n0_ref, m0_ref,
                  cw_ref, cb_ref, wa_ref, ba_ref, wi_ref, bi_ref, lam_ref, buf0_ref, h0_ref,
                  y_ref, ret_out, c_out, n_out, m_out, h_out, buf_out,
                  bias_sc, spair, ppair, m_sc, win, h_sc, kpadt=None, vpad=None,
                  *, blk, seqs, n_steps, has_past):
    L = blk
    band = A_REACH + L
    n_past = A_REACH // L if not has_past else 0
    n_var = n_past + 1
    bb = pl.program_id(0)
    tb = pl.program_id(1)
    tail = CONV_W - 1
    if n_steps == 1:
        at_first = at_last = lambda f: f()
    else:
        at_first = pl.when(tb == 0)
        at_last = pl.when(tb == n_steps - 1)

    li = lax.broadcasted_iota(jnp.int32, (L, L), 0)
    lj = lax.broadcasted_iota(jnp.int32, (L, L), 1)
    causal_neg = jnp.where(lj <= li, 0.0, -jnp.inf)
    eye_b = jnp.where(li == lj, 1.0, 0.0).astype(BF16)
    lane_lo = lax.broadcasted_iota(jnp.int32, (L, SLAB), 1) < HEAD_DIM
    ones_slab = jnp.ones((L, SLAB), BF16)
    ri = lax.broadcasted_iota(jnp.int32, (SLAB, SLAB), 0) // HEAD_DIM
    rj = lax.broadcasted_iota(jnp.int32, (SLAB, SLAB), 1) // HEAD_DIM
    smask = jnp.where(ri == rj, 1.0, 0.0)
    pmask = jnp.concatenate([smask, smask], axis=1)
    gi = lax.broadcasted_iota(jnp.int32, (W_GROUP, W_GROUP), 0) // HEAD_DIM
    gj = lax.broadcasted_iota(jnp.int32, (W_GROUP, W_GROUP), 1) // HEAD_DIM
    gmat = jnp.where(gi == gj, 1.0 / HEAD_DIM, 0.0).astype(BF16)
    e64i = lax.broadcasted_iota(jnp.int32, (HEAD_DIM, HEAD_DIM), 0)
    e64j = lax.broadcasted_iota(jnp.int32, (HEAD_DIM, HEAD_DIM), 1)
    eye64 = e64i == e64j
    row_lo = lax.broadcasted_iota(jnp.int32, (SLAB, 1), 0) < HEAD_DIM

    @pl.when((bb == 0) & (tb == 0))
    def _():
        qq = lax.broadcasted_iota(jnp.int32, (L, band), 0)
        kk = lax.broadcasted_iota(jnp.int32, (L, band), 1)
        off = kk - (qq // CHUNK) * CHUNK
        for h in range(N_HEADS):
            row = jnp.broadcast_to(rel_ref[h], (L, ROLL_W))
            tile = pltpu.roll(row, ROLL_W - L, 1, stride=1, stride_axis=0)[:, :band] * LOG2E
            tile = jnp.where(off >= 0, jnp.where(off < A_REACH + CHUNK, tile, -jnp.inf), -jnp.inf)
            for v in range(n_var):
                first_col = (n_past - v) * L
                bias_sc[v * N_HEADS + h] = jnp.where(kk >= first_col, tile, -jnp.inf) if first_col > 0 else tile

    for s in range(seqs):
        @at_first
        def _():
            for p in range(N_PAIRS):
                sblk, cblk = [], []
                for hh in range(2):
                    h = 2 * p + hh
                    z = jnp.zeros((HEAD_DIM, HEAD_DIM), F32)
                    s_h = ret0_ref[s, h]
                    c_h = c0_ref[s, h]
                    n_row = n0_ref[s, h:h + 1, :]
                    n_col = jnp.sum(jnp.where(eye64, jnp.broadcast_to(n_row, eye64.shape), 0.0),
                                    axis=1, keepdims=True)
                    n_rep = jnp.broadcast_to(n_col, (HEAD_DIM, HEAD_DIM))
                    sblk.append(jnp.concatenate([s_h, z] if hh == 0 else [z, s_h], axis=1))
                    cblk.append(jnp.concatenate([c_h, z, n_rep, z] if hh == 0 else [z, c_h, z, n_rep], axis=1))
                spair[s, p] = jnp.concatenate(sblk, axis=0)
                ppair[s, p] = jnp.concatenate(cblk, axis=0)
            m_sc[s, 0:N_HEADS, :] = jnp.broadcast_to(m0_ref[s], (N_HEADS, m_sc.shape[2]))
            win[s, CONV_PAD - tail:CONV_PAD, :] = buf0_ref[s]
            h_sc[s] = h0_ref[s]
            if not has_past:
                vpad[s, 0:A_REACH, :] = jnp.zeros((A_REACH, W_GROUP), BF16)
                kpadt[s, 0:n_past * W_GROUP, :] = jnp.zeros((n_past * W_GROUP, L), BF16)

    g8 = tr_ref[3 * W_GROUP:3 * W_GROUP + N_GATES, :]
    seg_pos = lax.broadcasted_iota(jnp.int32, g8.shape, 1) % L
    b8 = _seg_scan(jax.nn.log_sigmoid(g8), seg_pos, L, jnp.add, 0.0)
    a8 = g8 - pltpu.roll(b8, N_HEADS, 0)
    cm8 = _seg_scan(a8, seg_pos, L, jnp.maximum, -jnp.inf)
    b8 = pltpu.roll(b8, N_HEADS, 0)

    heads = [(h // 2, h % 2, h) for h in range(N_HEADS)]
    slab = lambda p: slice(p * SLAB, (p + 1) * SLAB)
    seg_of = lambda s: slice(s * L, (s + 1) * L)
    col = lambda s, j: pq_ref[s, :, (3 + j) * W_GROUP:(4 + j) * W_GROUP]

    def halves(per_head):
        return jnp.concatenate([jnp.where(lane_lo, per_head[2 * p], per_head[2 * p + 1]) for p in range(N_PAIRS)],
                               axis=1)


    def attention_chain(s):
        pa = pq_ref[s, :, 0:3 * W_GROUP]
        kat = tr_ref[0:W_GROUP, seg_of(s)].astype(BF16)
        v_new = pa[:, 2 * W_GROUP:].astype(BF16)
        if has_past:
            kpast = pastk_ref[s].astype(BF16)
            vband = jnp.concatenate([pastv_ref[s].astype(BF16), v_new], axis=0)
            bias_at = lambda h: bias_sc[h]
        else:
            kpadt[s, pl.ds(pl.multiple_of((tb + n_past) * W_GROUP, W_GROUP), W_GROUP), :] = kat
            kwin = kpadt[s, pl.ds(pl.multiple_of(tb * W_GROUP, W_GROUP), (n_past + 1) * W_GROUP), :]
            start = pl.multiple_of(tb * L, L)
            vpad[s, pl.ds(A_REACH + start, L), :] = v_new
            vband = vpad[s, pl.ds(start, band), :]
            var = jnp.minimum(tb, n_var - 1) * N_HEADS
            bias_at = lambda h: bias_sc[var + h]
        qa = (pa[:, 0:W_GROUP] * (HEAD_DIM ** -0.5 * LOG2E)).astype(BF16)
        sc = {}
        for p, hh, h in heads:
            q_slab = qa[:, slab(p)]
            if has_past:
                k_slab = kpast[:, slab(p)]
                lane = lax.broadcasted_iota(jnp.int32, k_slab.shape, 1)
                lane_h = lane < HEAD_DIM if hh == 0 else lane >= HEAD_DIM
                parts = [_dot_nt(q_slab, jnp.where(lane_h, k_slab, jnp.zeros_like(k_slab))),
                         jnp.dot(q_slab, _head_rows(kat, h, hh), preferred_element_type=F32)]
            else:
                parts = [jnp.dot(q_slab, _head_rows(kwin[i * W_GROUP:(i + 1) * W_GROUP], h, hh),
                                 preferred_element_type=F32) for i in range(n_past + 1)]
            sc[h] = jnp.concatenate(parts, axis=1)
        yield
        pe, den = {}, {}
        for _, _, h in heads:
            sh = sc[h] + bias_at(h)
            pe[h] = jnp.exp2(sh - jnp.max(sh, axis=1, keepdims=True))
            den[h] = jnp.sum(pe[h], axis=1, keepdims=True)
        yield
        res = {h: _dot(pe[h], vband)[:, slab(p)] for p, _, h in heads}
        yield
        y_ref[s, :, 0:W_GROUP] = halves({h: res[h] / den[h] for _, _, h in heads})

    def retention_chain(s):
        qr = (col(s, PR_QB) * cs_ref[:, 0:W_GROUP] + col(s, PR_QBS) * cs_ref[:, W_GROUP:]).astype(BF16)
        kbt = tr_ref[W_GROUP:2 * W_GROUP, seg_of(s)]
        half = HEAD_DIM // 2
        kbt_sw = jnp.concatenate([kbt[h * HEAD_DIM + o:h * HEAD_DIM + o + half]
                                  for h in range(N_HEADS) for o in (half, 0)], axis=0)
        krt = (kbt * cst_ref[0:W_GROUP, seg_of(s)] + kbt_sw * cst_ref[W_GROUP:, seg_of(s)]) * HEAD_DIM ** -0.5
        krt_b = krt.astype(BF16)
        kdt_b = (krt * kdect_ref[...]).astype(BF16)
        vb = col(s, PR_VB).astype(BF16)
        raw = {h: jnp.dot(qr[:, slab(p)], _head_rows(krt_b, h, hh), preferred_element_type=F32)
               for p, hh, h in heads}
        s0 = {p: spair[s, p] for p in range(N_PAIRS)}
        inter = {p: _dot(qr[:, slab(p)], s0[p]) for p in range(N_PAIRS)}
        add = {p: _dot(kdt_b[slab(p)], vb[:, slab(p)]) for p in range(N_PAIRS)}
        yield
        scores = {h: raw[h] * dec_ref[h] for _, _, h in heads}
        for p in range(N_PAIRS):
            spair[s, p] = s0[p] * sdec_ref[p] + add[p] * smask
        yield
        o = {h: _dot(scores[h], vb[:, slab(p)]) for p, _, h in heads}
        yield
        ob = halves(o) + jnp.concatenate([inter[p] for p in range(N_PAIRS)], axis=1) * qdec_ref[...]
        mu = _dot(ob, gmat)
        yield
        oc = ob - mu
        var = _dot(oc * oc, gmat)
        yield
        y_ref[s, :, W_GROUP:2 * W_GROUP] = oc * lax.rsqrt(var + LN_EPS) * _silu(col(s, PR_GB))

    def mlstm_chain(s):
        a4 = a8[0:N_HEADS, seg_of(s)]
        b4 = b8[0:N_HEADS, seg_of(s)]
        m0c = m_sc[s, 0:N_HEADS, 0:1]
        big_m = jnp.maximum(m0c, cm8[0:N_HEADS, seg_of(s)])
        m4 = b4 + big_m
        w0_4 = jnp.exp(m0c - big_m)
        m_last = big_m[:, L - 1:L]
        wrow4 = jnp.exp(a4 - m_last)
        kct = tr_ref[2 * W_GROUP:3 * W_GROUP, seg_of(s)] * HEAD_DIM ** -0.5
        kct_b = kct.astype(BF16)
        kw_b = (kct * _rep_heads(wrow4, HEAD_DIM)).astype(BF16)
        qc = col(s, PR_QC).astype(BF16)
        vc = col(s, PR_VC).astype(BF16)
        v_aug = {p: jnp.concatenate([vc[:, slab(p)], ones_slab], axis=1) for p in range(N_PAIRS)}
        p0 = {p: ppair[s, p] for p in range(N_PAIRS)}
        w0_bc = _col_bcast(eye_b, _rep_heads(w0_4, HEAD_DIM))
        floor_bc = _col_bcast(eye_b, _rep_heads(jnp.exp(-m4), HEAD_DIM))
        m_bc = {h: _col_bcast(eye_b, jnp.broadcast_to(big_m[h:h + 1], (L, L))) for _, _, h in heads}
        raw = {h: jnp.dot(qc[:, slab(p)], _head_rows(kct_b, h, hh), preferred_element_type=F32)
               for p, hh, h in heads}
        inter = {p: _dot(qc[:, slab(p)], p0[p]) for p in range(N_PAIRS)}
        add = {p: _dot(kw_b[slab(p)], v_aug[p]) for p in range(N_PAIRS)}
        yield
        qk = {h: raw[h] * jnp.exp(a4[h:h + 1] - m_bc[h] + causal_neg) for _, _, h in heads}
        for p in range(N_PAIRS):
            w_state = jnp.where(row_lo, w0_4[2 * p:2 * p + 1, L - 1:L], w0_4[2 * p + 1:2 * p + 2, L - 1:L])
            ppair[s, p] = p0[p] * w_state + add[p] * pmask
        m_sc[s, 0:N_HEADS, :] = jnp.broadcast_to(m4[:, L - 1:L], (N_HEADS, m_sc.shape[2]))
        yield
        res = {h: _dot(qk[h], v_aug[p]) for p, _, h in heads}
        yield
        hc = []
        for p in range(N_PAIRS):
            w0 = w0_bc[:, slab(p)]
            num = jnp.where(lane_lo, res[2 * p][:, 0:SLAB], res[2 * p + 1][:, 0:SLAB]) + inter[p][:, 0:SLAB] * w0
            den = jnp.where(lane_lo, res[2 * p][:, SLAB:], res[2 * p + 1][:, SLAB:]) + inter[p][:, SLAB:] * w0
            hc.append(num / jnp.maximum(jnp.abs(den), floor_bc[:, slab(p)]))
        hc = jnp.concatenate(hc, axis=1)
        mu = _dot(hc, gmat)
        yield
        hcc = hc - mu
        var = _dot(hcc * hcc, gmat)
        yield
        y_ref[s, :, 2 * W_GROUP:3 * W_GROUP] = hcc * lax.rsqrt(var + LN_EPS) * jax.nn.sigmoid(col(s, PR_OC))

    def rglru_chain(s):
        xd = col(s, PR_XD)
        win[s, CONV_PAD:CONV_PAD + L, :] = xd
        xc = cb_ref[...]
        for j in range(CONV_W):
            off = CONV_PAD - tail + j
            xc = xc + win[s, off:off + L, :] * cw_ref[j:j + 1, :]
        win[s, CONV_PAD - tail:CONV_PAD, :] = xd[L - tail:, :]
        r_pre = _dot(xc, wa_ref[...])
        i_pre = _dot(xc, wi_ref[...])
        yield
        r = jax.nn.sigmoid(r_pre + ba_ref[...])
        i = jax.nn.sigmoid(i_pre + bi_ref[...])
        neg_lam = -lam_ref[...]
        softplus = jnp.maximum(neg_lam, 0.0) + jnp.log1p(jnp.exp(-jnp.abs(neg_lam)))
        log_a = -LRU_C * r * softplus
        a = jnp.exp(log_a)
        u = jnp.sqrt(-jnp.tanh(log_a) * (jnp.exp(2.0 * log_a) + 1.0)) * (i * xc)
        st = 1
        while st < L:
            u = a * _shift_rows(u, st, 0.0) + u
            a = a * _shift_rows(a, st, 1.0)
            st *= 2
            if st in (8, 64):
                yield
        hseq = u + a * h_sc[s]
        h_sc[s] = hseq[L - 1:L, :]
        y_ref[s, :, 3 * W_GROUP:] = hseq * jax.nn.gelu(col(s, PR_YD))

    _run_interleaved([chain(s) for chain in (attention_chain, retention_chain, mlstm_chain, rglru_chain)
                      for s in range(seqs)], lag=1 if seqs == 1 else 0)

    for s in range(seqs):
        @at_last
        def _():
            for p in range(N_PAIRS):
                sp = spair[s, p]
                pp = ppair[s, p]
                for hh in range(2):
                    h = 2 * p + hh
                    rs = slice(hh * HEAD_DIM, (hh + 1) * HEAD_DIM)
                    ret_out[s, h] = sp[rs, rs]
                    c_out[s, h] = pp[rs, rs]
                    n_rep = pp[rs, SLAB + hh * HEAD_DIM:SLAB + (hh + 1) * HEAD_DIM]
                    n_out[s, h:h + 1, :] = jnp.sum(jnp.where(eye64, n_rep, 0.0), axis=0, keepdims=True)
            m_out[s] = m_sc[s, 0:N_HEADS, 0:1]
            h_out[s] = h_sc[s]
            buf_out[s] = pq_ref[s, L - tail:, (3 + PR_XD) * W_GROUP:(4 + PR_XD) * W_GROUP]


def _mixer_kernel_fresh(pq_ref, tr_ref, *rest, **static):
    _mixer_kernel(pq_ref, tr_ref, None, None, *rest, **static)


def _mixer_geometry(t, has_past):
    if has_past:
        assert t == CHUNK, "sequences with a carried cache are expected to be a single chunk"
        return t, MIX_SEQS
    assert t % MIX_BLOCK == 0 and A_REACH % MIX_BLOCK == 0
    return MIX_BLOCK, 1


def _mixer_tables(pos, blk, seqs):
    cos, sin = _rotary_tables(pos)
    natural = jnp.concatenate([cos, sin], axis=1)
    transposed = jnp.tile(jnp.concatenate([cos.T, sin.T], axis=0), (1, seqs))
    return (natural, transposed) + _retention_tables(blk)


def _mixer(pq, tr, past, rel_rows, tables, rec, lw):
    ret0, c0, n0, m0, h0, buf0, rec_layer = rec
    m0, h0 = m0[..., None], h0[:, :, None, :]
    bsz, t, _ = pq.shape
    has_past = past is not None
    blk, seqs = _mixer_geometry(t, has_past)
    nb = t // blk
    assert bsz % seqs == 0
    band = A_REACH + blk
    n_var = 1 if has_past else A_REACH // blk + 1
    tail = CONV_W - 1

    tok3 = lambda w: pl.BlockSpec((seqs, blk, w), lambda b, c: (b, c, 0))
    trn = lambda r: pl.BlockSpec((r, seqs * blk), lambda b, c: (0, b * nb + c))
    per_seq = lambda *shape: pl.BlockSpec((seqs,) + shape, lambda b, c: (b,) + (0,) * len(shape))
    const = lambda *shape: pl.BlockSpec(shape, lambda b, c: (0,) * len(shape))
    carried = lambda *shape: pl.BlockSpec((None, seqs) + shape, lambda b, c: (rec_layer, b) + (0,) * len(shape))
    past_specs, past_args = [], ()
    if has_past:
        layer = past[2]
        cache = pl.BlockSpec((None, seqs, A_REACH, W_GROUP), lambda b, c: (layer, b, 0, 0))
        past_specs, past_args = [cache, cache], past[:2]
    in_specs = [tok3((3 + N_PR) * W_GROUP), trn(3 * W_GROUP + N_GATES),
                *past_specs, const(N_HEADS, 1, ROLL_W),
                pl.BlockSpec((blk, 2 * W_GROUP), lambda b, c: (c, 0)),
                pl.BlockSpec((2 * W_GROUP, seqs * blk), lambda b, c: (0, c)),
                const(N_HEADS, blk, blk), const(blk, W_GROUP), const(W_GROUP, blk), const(N_PAIRS, SLAB, SLAB),
                carried(N_HEADS, HEAD_DIM, HEAD_DIM), carried(N_HEADS, HEAD_DIM, HEAD_DIM),
                carried(N_HEADS, HEAD_DIM), carried(N_HEADS, 1),
                const(CONV_W, W_GROUP), const(1, W_GROUP), const(W_GROUP, W_GROUP), const(1, W_GROUP),
                const(W_GROUP, W_GROUP), const(1, W_GROUP), const(1, W_GROUP),
                carried(tail, W_GROUP), carried(1, W_GROUP)]
    out_shape = (jax.ShapeDtypeStruct((bsz, t, 4 * W_GROUP), F32),
                 jax.ShapeDtypeStruct((bsz, N_HEADS, HEAD_DIM, HEAD_DIM), F32),
                 jax.ShapeDtypeStruct((bsz, N_HEADS, HEAD_DIM, HEAD_DIM), F32),
                 jax.ShapeDtypeStruct((bsz, N_HEADS, HEAD_DIM), F32),
                 jax.ShapeDtypeStruct((bsz, N_HEADS, 1), F32),
                 jax.ShapeDtypeStruct((bsz, 1, W_GROUP), F32),
                 jax.ShapeDtypeStruct((bsz, tail, W_GROUP), F32))
    out_specs = (tok3(4 * W_GROUP), per_seq(N_HEADS, HEAD_DIM, HEAD_DIM), per_seq(N_HEADS, HEAD_DIM, HEAD_DIM),
                 per_seq(N_HEADS, HEAD_DIM), per_seq(N_HEADS, 1), per_seq(1, W_GROUP), per_seq(tail, W_GROUP))
    scratch = [pltpu.VMEM((n_var * N_HEADS, blk, band), F32),
               pltpu.VMEM((seqs, N_PAIRS, SLAB, SLAB), F32), pltpu.VMEM((seqs, N_PAIRS, SLAB, 2 * SLAB), F32),
               pltpu.VMEM((seqs, 8, 128), F32), pltpu.VMEM((seqs, CONV_PAD + blk, W_GROUP), F32),
               pltpu.VMEM((seqs, 1, W_GROUP), F32)]
    if not has_past:
        scratch += [pltpu.VMEM((seqs, (nb + A_REACH // blk) * W_GROUP, blk), BF16),
                    pltpu.VMEM((seqs, A_REACH + t, W_GROUP), BF16)]
    body = _mixer_kernel if has_past else _mixer_kernel_fresh
    return pl.pallas_call(
        functools.partial(body, blk=blk, seqs=seqs, n_steps=nb, has_past=has_past),
        out_shape=out_shape,
        grid=(bsz // seqs, nb),
        in_specs=in_specs,
        out_specs=out_specs,
        scratch_shapes=scratch,
        compiler_params=_cparams("arbitrary", "arbitrary"),
        name="mixer",
    )(pq, tr, *past_args, rel_rows, *tables,
      ret0, c0, n0, m0, lw['conv_w'], lw['conv_b'], lw['lru_wa'], lw['lru_ba'], lw['lru_wi'], lw['lru_bi'],
      lw['lru_lambda'], buf0, h0)


def _post_kernel(x_ref, y_ref, wo_ref, g1_ref, b1_ref, wq_ref, mk_ref, mv_ref, xo_ref,
                 g2_ref, b2_ref, o_ref, att_sc, *, alpha, seqs, subtiles):
    rows_sub = x_ref.shape[0] // subtiles
    rows = rows_sub // seqs
    head = lambda h: slice(h * HEAD_DIM, (h + 1) * HEAD_DIM)

    def subtile_chain(i):
        rs = slice(i * rows_sub, (i + 1) * rows_sub)
        x = x_ref[rs, :]
        mix = _dot(y_ref[rs, :], wo_ref[...])
        yield
        x2 = _layer_norm(alpha * x + mix, g1_ref[...], b1_ref[...])
        yield
        q = _dot(x2, wq_ref[...]).astype(BF16)
        yield

        def attend(s, h):
            qs = slice(s * rows, (s + 1) * rows)
            sc = _dot_nt(q[qs, head(h)], mk_ref[s, :, head(h)]) * HEAD_DIM ** -0.5
            yield
            p = jnp.exp(sc - jnp.max(sc, axis=1, keepdims=True))
            den = jnp.sum(p, axis=1, keepdims=True)
            yield
            o = _dot(p, mv_ref[s, :, head(h)])
            yield
            att_sc[i * rows_sub + s * rows:i * rows_sub + (s + 1) * rows, head(h)] = o / den

        yield from _lockstep([attend(s, h) for s in range(seqs) for h in range(N_HEADS)])
        out = _dot(att_sc[rs, :], xo_ref[...])
        yield
        o_ref[rs, :] = _layer_norm(alpha * x2 + out, g2_ref[...], b2_ref[...])

    _run_interleaved([subtile_chain(i) for i in range(subtiles)], lag=2)


def _post(x, y, wo, wq, xo, wlayer, g1, b1, g2, b2, mk, mv, layer, alpha, seq_len, casts=()):
    n, d = x.shape
    n_mem, dx = mk.shape[2], mk.shape[3]
    if seq_len >= POST_SUBTILES * TOKEN_TILE:
        subtiles, seqs = POST_SUBTILES, 1
        tm = subtiles * TOKEN_TILE
        mem = pl.BlockSpec((None, 1, n_mem, dx), lambda i: (layer, i // (seq_len // tm), 0, 0))
    else:
        tm = TOKEN_TILE
        subtiles, seqs = 1, tm // seq_len
        mem = pl.BlockSpec((None, seqs, n_mem, dx), lambda i: (layer, i, 0, 0))
    assert n % tm == 0 and tm % (subtiles * seqs) == 0
    tok = lambda w: pl.BlockSpec((tm, w), lambda i: (i, 0))
    plans = [_cast_plan(stack, lyr, n // tm) for stack, lyr in casts]
    out = pl.pallas_call(
        _with_casts(functools.partial(_post_kernel, alpha=alpha, seqs=seqs, subtiles=subtiles), 11, 1, len(casts)),
        out_shape=(jax.ShapeDtypeStruct((n, d), F32), *[p[2] for p in plans]),
        grid=(n // tm,),
        in_specs=[tok(d), tok(y.shape[1]),
                  _resident(wo.shape[1:], wlayer), _resident((1, d)), _resident((1, d)),
                  _resident(wq.shape[1:], wlayer), mem, mem, _resident(xo.shape[1:], wlayer),
                  _resident((1, d)), _resident((1, d)), *[p[0] for p in plans]],
        out_specs=(tok(d), *[p[1] for p in plans]),
        scratch_shapes=[pltpu.VMEM((tm, dx), F32)],
        compiler_params=_cparams("parallel"),
        name="post",
    )(x, y, wo, g1, b1, wq, mk, mv, xo, g2, b2, *[stack for stack, _ in casts])
    return out[0], out[1:]


def _mem_kv_kernel(x_ref, w_ref, o_ref):
    o_ref[...] = _dot(x_ref[...], w_ref[...])


def _mem_kv(mem, w):
    n, d = mem.shape
    tm = min(TOKEN_TILE, n)
    return pl.pallas_call(
        _mem_kv_kernel,
        out_shape=jax.ShapeDtypeStruct((n, w.shape[1]), F32),
        grid=(n // tm,),
        in_specs=[pl.BlockSpec((tm, d), lambda i: (i, 0)), _resident(w.shape)],
        out_specs=pl.BlockSpec((tm, w.shape[1]), lambda i: (i, 0)),
        compiler_params=_cparams("parallel"),
        name="mem_kv",
    )(mem, w)


def _rotary_tables(pos):
    half = HEAD_DIM // 2
    inv = jnp.exp(-jnp.log(10000.0) * jnp.arange(half, dtype=F32) / half)
    ang = pos.astype(F32)[:, None] * inv[None, :]
    cos, sin = jnp.cos(ang), jnp.sin(ang)
    cos_full = jnp.tile(jnp.concatenate([cos, cos], -1), (1, N_HEADS))
    sin_full = jnp.tile(jnp.concatenate([-sin, sin], -1), (1, N_HEADS))
    return cos_full, sin_full


def _retention_tables(blk):
    log_g = jnp.log1p(-jnp.exp2(-5.0 - jnp.arange(N_HEADS, dtype=F32)))
    idx = jnp.arange(blk, dtype=F32)
    diff = idx[:, None] - idx[None, :]
    dec = jnp.exp(jnp.where((diff >= 0)[None], diff[None] * log_g[:, None, None], -jnp.inf))
    q_dec = jnp.exp((idx[:, None] + 1.0) * log_g[None, :])
    k_dec = jnp.exp((blk - 1.0 - idx)[:, None] * log_g[None, :])
    s_dec = jnp.exp(blk * log_g)
    lanes = lambda tbl: jnp.repeat(tbl, HEAD_DIM, axis=1)
    pair = lambda p: jnp.kron(jnp.diag(s_dec[2 * p:2 * p + 2]), jnp.ones((HEAD_DIM, HEAD_DIM), F32))
    return dec, lanes(q_dec), lanes(k_dec).T, jnp.stack([pair(p) for p in range(N_PAIRS)])


def _rel_bias_rows(rel_bias, blk):
    idx = np.clip(A_REACH + blk - np.arange(ROLL_W), -REL_CLIP, REL_CLIP) + REL_CLIP
    return rel_bias[:, idx][:, None, :]


def _block_diag(w):
    h, c, _ = w.shape
    eye = jnp.eye(h, dtype=w.dtype)
    return (eye[:, None, :, None] * w[:, :, None, :]).reshape(h * c, h * c)


def _swap_perm():
    j = np.arange(W_GROUP)
    return (j // HEAD_DIM) * HEAD_DIM + (j % HEAD_DIM + HEAD_DIM // 2) % HEAD_DIM


def _layer(x, tables, mem, past, rec, lw, ffn1, ffn2, alpha, proj_casts=(), post_casts=()):
    bsz, t, d = x.shape
    n = bsz * t
    g, b = lw['ln_g'], lw['ln_b']

    x1 = _ffn_ln(x.reshape(n, d), *ffn1, g[0:1], b[0:1], alpha)
    (pq, tr), proj_cast_out = _in_proj(x1, lw['w_in_n'], lw['w_in_t'], lw['b_in_n'], lw['b_in_t'], proj_casts)
    if ffn2 is None:
        ffn2 = proj_cast_out[:3]
    pq = pq.reshape(bsz, t, (3 + N_PR) * W_GROUP)
    blk, _ = _mixer_geometry(t, past is not None)
    y, ret_new, c_new, n_new, m_new, h_new, conv_new = _mixer(
        pq, tr, past, _rel_bias_rows(lw['rel_bias'], blk), tables, rec, lw)

    x3, post_cast_out = _post(x1, y.reshape(n, 4 * W_GROUP), lw['w_out'], lw['x_wq'], lw['x_wo'], lw['layer'],
                              g[1:2], b[1:2], g[2:3], b[2:3], *mem, alpha, t, post_casts)
    x4 = _ffn_ln(x3, *ffn2, g[3:4], b[3:4], alpha)

    heads = lambda a: a.reshape(bsz, -1, N_HEADS, HEAD_DIM)
    keep = min(A_REACH, t)
    a_k = heads(pq[:, t - keep:, W_GROUP:2 * W_GROUP])
    a_v = heads(pq[:, t - keep:, 2 * W_GROUP:3 * W_GROUP])
    state = (a_k, a_v, ret_new, c_new, n_new, m_new[:, :, 0], h_new[:, 0, :], conv_new)
    return x4.reshape(bsz, t, d), state, proj_cast_out, post_cast_out


def kernel(x_prompt, x_sample, mem_prompt, cache_a_k, cache_a_v, state_ret, state_mlstm_c, state_mlstm_n,
           state_mlstm_m, state_lru_h, state_conv, cache_mem_k, cache_mem_v, ln_g, ln_b,
           ffn1_gate, ffn1_up, ffn1_down, ffn2_gate, ffn2_up, ffn2_down, w_in, b_in, a_rel_bias,
           conv_w, conv_b, lru_wa, lru_ba, lru_wi, lru_bi, lru_lambda, w_out, x_wq, x_wk, x_wv, x_wo):
    depth = ln_g.shape[0]
    alpha = (2.0 * depth) ** 0.25
    row = lambda v: v[None, :]
    grp = lambda a, j: a[..., j * W_GROUP:(j + 1) * W_GROUP]
    perm = _swap_perm()
    QA, KA, VA, QB, KB, VB, GB, QC, KC, VC, OC, XD, YD = range(13)

    def layer_weights(l):
        w, bias = w_in[l], b_in[l]
        nat = lambda a: jnp.concatenate(
            [a[..., :3 * W_GROUP], grp(a, QB), grp(a, QB)[..., perm], grp(a, VB), grp(a, GB), grp(a, QC), grp(a, VC),
             grp(a, OC), grp(a, XD), grp(a, YD)], axis=-1)
        trn = lambda a: jnp.concatenate([grp(a, KA), grp(a, KB), grp(a, KC), a[..., 13 * W_GROUP:]], axis=-1)
        return {'ln_g': ln_g[l], 'ln_b': ln_b[l],
                'layer': l, **stacks,
                'w_in_n': nat(w).astype(BF16), 'w_in_t': trn(w).T.astype(BF16),
                'b_in_n': row(nat(bias)), 'b_in_t': trn(bias)[:, None],
                'rel_bias': a_rel_bias[l],
                'conv_w': conv_w[l], 'conv_b': row(conv_b[l]),
                'lru_wa': _block_diag(lru_wa[l]).astype(BF16), 'lru_ba': row(lru_ba[l]),
                'lru_wi': _block_diag(lru_wi[l]).astype(BF16), 'lru_bi': row(lru_bi[l]),
                'lru_lambda': row(lru_lambda[l]),
                'x_wkv': jnp.concatenate([x_wk[l], x_wv[l]], axis=1).astype(BF16)}

    stacks = {name: w.astype(BF16) for name, w in (('w_out', w_out), ('x_wq', x_wq), ('x_wo', x_wo))}
    ffn1_f32, ffn2_f32 = (ffn1_gate, ffn1_up, ffn1_down), (ffn2_gate, ffn2_up, ffn2_down)
    ffn1_w = {0: tuple(w[0].astype(BF16) for w in ffn1_f32)}
    ffn2_w = {}
    weights = [layer_weights(l) for l in range(depth)]
    dx = x_wk.shape[2]

    bp, tp, d = x_prompt.shape
    n_mem = mem_prompt.shape[1]
    h = x_prompt
    prompt_states = []
    tables = _mixer_tables(jnp.arange(tp), *_mixer_geometry(tp, False))
    rec0 = (jnp.zeros((1, bp, N_HEADS, HEAD_DIM, HEAD_DIM), F32), jnp.zeros((1, bp, N_HEADS, HEAD_DIM, HEAD_DIM), F32),
            jnp.zeros((1, bp, N_HEADS, HEAD_DIM), F32), jnp.zeros((1, bp, N_HEADS), F32),
            jnp.zeros((1, bp, W_GROUP), F32), jnp.zeros((1, bp, CONV_W - 1, W_GROUP), F32), 0)
    for l in range(depth):
        lw = weights[l]
        mkv = _mem_kv(mem_prompt.reshape(bp * n_mem, d), lw['x_wkv'])
        mk = mkv[:, :dx].reshape(bp, n_mem, dx)
        mv = mkv[:, dx:].reshape(bp, n_mem, dx)
        post_casts = [(w, l + 1) for w in ffn1_f32] if l + 1 < depth else []
        h, st, ffn2_w[l], cast_next = _layer(h, tables, (mk[None], mv[None], 0), None, rec0, lw, ffn1_w[l], None,
                                             alpha, [(w, l) for w in ffn2_f32], post_casts)
        if post_casts:
            ffn1_w[l + 1] = cast_next
        prompt_states.append(st + (mk.reshape(bp, n_mem, -1, HEAD_DIM), mv.reshape(bp, n_mem, -1, HEAD_DIM)))
    y_prompt = h
    prompt_out = [jnp.stack(f) for f in zip(*prompt_states)]

    bs, ts, _ = x_sample.shape
    h = x_sample
    sample_states = []
    tables = _mixer_tables(PAST_LEN + jnp.arange(ts), *_mixer_geometry(ts, True))
    past_k_all = cache_a_k.reshape(depth, bs, A_REACH, W_GROUP)
    past_v_all = cache_a_v.reshape(depth, bs, A_REACH, W_GROUP)
    mem_k_all = cache_mem_k.reshape(depth, bs, n_mem, dx)
    mem_v_all = cache_mem_v.reshape(depth, bs, n_mem, dx)
    for l in range(depth):
        rec = (state_ret, state_mlstm_c, state_mlstm_n, state_mlstm_m, state_lru_h, state_conv, l)
        h, st, _, _ = _layer(h, tables, (mem_k_all, mem_v_all, l), (past_k_all, past_v_all, l), rec, weights[l],
                             ffn1_w[l], ffn2_w[l], alpha)
        sample_states.append(st)
    y_sample = h
    sample_out = [jnp.stack(f) for f in zip(*sample_states)]

    return (y_prompt, y_sample, *prompt_out, *sample_out)
```

```python
import functools

import numpy as np
import jax
import jax.numpy as jnp
from jax import lax
from jax.experimental import pallas as pl
from jax.experimental.pallas import tpu as pltpu

F32 = jnp.float32
BF16 = jnp.bfloat16

CHUNK = 64
HEAD_DIM = 64
N_HEADS = 4
W_GROUP = N_HEADS * HEAD_DIM
SLAB = 2 * HEAD_DIM
N_PAIRS = N_HEADS // 2
A_BAND_CHUNKS = 8
A_REACH = A_BAND_CHUNKS * CHUNK
REL_CLIP = 128
CONV_W = 4
LRU_C = 8.0
LN_EPS = 1e-5
LOG2E = 1.4426950408889634
N_GATES = 2 * N_HEADS
PAST_LEN = 4096
ROLL_W = 1024
CONV_PAD = 8

V7X_VMEM_LIMIT_BYTES = 56 * 1024 * 1024
TOKEN_TILE = 512
MIX_BLOCK = 256
MIX_SEQS = 4
POST_SUBTILES = 2

PR_QB, PR_QBS, PR_VB, PR_GB, PR_QC, PR_VC, PR_OC, PR_XD, PR_YD = range(9)
N_PR = 9


def _cparams(*sem):
    return pltpu.CompilerParams(dimension_semantics=sem, vmem_limit_bytes=V7X_VMEM_LIMIT_BYTES)


def _dot(a, b):
    return jnp.dot(a.astype(BF16), b.astype(BF16), preferred_element_type=F32)


def _dot_nt(a, b):
    return lax.dot_general(a.astype(BF16), b.astype(BF16), (((1,), (1,)), ((), ())), preferred_element_type=F32)


def _layer_norm(x, g, b):
    mu = jnp.mean(x, -1, keepdims=True)
    xc = x - mu
    var = jnp.mean(xc * xc, -1, keepdims=True)
    return xc * lax.rsqrt(var + LN_EPS) * g + b


def _silu(x):
    return x * jax.nn.sigmoid(x)


def _lockstep(chains):
    while chains:
        chains = [c for c in chains if next(c, "done") != "done"]
        yield


def _run_interleaved(chains, lag=0):
    pending, live, rnd = list(chains), [], 0
    while pending or live:
        while pending and rnd >= lag * (len(chains) - len(pending)):
            live.append(pending.pop(0))
        live = [c for c in live if next(c, "done") != "done"]
        rnd += 1


def _resident(shape, layer=None):
    nd = len(shape)
    if layer is None:
        return pl.BlockSpec(shape, lambda *_: (0,) * nd, pipeline_mode=pl.Buffered(1))
    return pl.BlockSpec((None,) + tuple(shape), lambda *_: (layer,) + (0,) * nd, pipeline_mode=pl.Buffered(1))


def _cast_plan(stack, layer, n_steps):
    _, r, c = stack.shape
    nb = next(k for k in range(n_steps, 0, -1) if n_steps % k == 0 and r % k == 0 and (r // k) % 16 == 0)
    every = n_steps // nb
    return (pl.BlockSpec((None, r // nb, c), lambda i: (layer, i // every, 0)),
            pl.BlockSpec((r // nb, c), lambda i: (i // every, 0)),
            jax.ShapeDtypeStruct((r, c), BF16))


def _with_casts(body, n_in, n_out, n_cast):
    def wrapped(*refs):
        ins, rest = refs[:n_in + n_cast], refs[n_in + n_cast:]
        outs, scratch = rest[:n_out + n_cast], rest[n_out + n_cast:]
        body(*ins[:n_in], *outs[:n_out], *scratch)
        for src, dst in zip(ins[n_in:], outs[n_out:]):
            dst[...] = src[...].astype(BF16)
    return wrapped


def _ffn_ln_kernel(x_ref, wg_ref, wu_ref, wd_ref, g_ref, b_ref, o_ref, *, alpha):
    x = x_ref[...]
    xb = x.astype(BF16)
    gate = jnp.dot(xb, wg_ref[...], preferred_element_type=F32)
    up = jnp.dot(xb, wu_ref[...], preferred_element_type=F32)
    h = (_silu(gate) * up).astype(BF16)
    y = jnp.dot(h, wd_ref[...], preferred_element_type=F32)
    o_ref[...] = _layer_norm(alpha * x + 0.5 * y, g_ref[...], b_ref[...])


def _ffn_ln(x, wg, wu, wd, g, b, alpha):
    n, d = x.shape
    dff = wg.shape[1]
    tm = TOKEN_TILE
    return pl.pallas_call(
        functools.partial(_ffn_ln_kernel, alpha=alpha),
        out_shape=jax.ShapeDtypeStruct((n, d), F32),
        grid=(n // tm,),
        in_specs=[pl.BlockSpec((tm, d), lambda i: (i, 0)),
                  _resident((d, dff)), _resident((d, dff)), _resident((dff, d)),
                  _resident((1, d)), _resident((1, d))],
        out_specs=pl.BlockSpec((tm, d), lambda i: (i, 0)),
        compiler_params=_cparams("parallel"),
        name="ffn_ln",
    )(x, wg, wu, wd, g, b)


def _in_proj_kernel(x_ref, wa_ref, wr_ref, wt_ref, ba_ref, br_ref, bt_ref,
                    pa_ref, pr_ref, kat_ref, kbt_ref, kct_ref, gt_ref):
    xb = x_ref[...].astype(BF16)
    pa_ref[...] = jnp.dot(xb, wa_ref[...], preferred_element_type=F32) + ba_ref[...]
    pr_ref[...] = jnp.dot(xb, wr_ref[...], preferred_element_type=F32) + br_ref[...]
    tr = _dot_nt(wt_ref[...], xb) + bt_ref[...]
    kat_ref[...] = tr[0:W_GROUP].astype(BF16)
    kbt_ref[...] = tr[W_GROUP:2 * W_GROUP]
    kct_ref[...] = tr[2 * W_GROUP:3 * W_GROUP]
    gt_ref[...] = tr[3 * W_GROUP:3 * W_GROUP + N_GATES]


def _in_proj(x, wa, wr, wt, ba, br, bt, casts=()):
    n, d = x.shape
    ca, cr, ct = wa.shape[1], wr.shape[1], wt.shape[0]
    tm = TOKEN_TILE
    tok = lambda w: pl.BlockSpec((tm, w), lambda i: (i, 0))
    trn = lambda r: pl.BlockSpec((r, tm), lambda i: (0, i))
    plans = [_cast_plan(stack, layer, n // tm) for stack, layer in casts]
    out = pl.pallas_call(
        _with_casts(_in_proj_kernel, 7, 6, len(casts)),
        out_shape=(jax.ShapeDtypeStruct((n, ca), F32), jax.ShapeDtypeStruct((n, cr), F32),
                   jax.ShapeDtypeStruct((W_GROUP, n), BF16), jax.ShapeDtypeStruct((W_GROUP, n), F32),
                   jax.ShapeDtypeStruct((W_GROUP, n), F32), jax.ShapeDtypeStruct((N_GATES, n), F32),
                   *[p[2] for p in plans]),
        grid=(n // tm,),
        in_specs=[tok(d), _resident((d, ca)), _resident((d, cr)), _resident((ct, d)),
                  _resident((1, ca)), _resident((1, cr)), _resident((ct, 1)), *[p[0] for p in plans]],
        out_specs=(tok(ca), tok(cr), trn(W_GROUP), trn(W_GROUP), trn(W_GROUP), trn(N_GATES), *[p[1] for p in plans]),
        compiler_params=_cparams("parallel"),
        name="in_proj",
    )(x, wa, wr, wt, ba, br, bt, *[stack for stack, _ in casts])
    return out[:6], out[6:]


def _head_rows(xt, h, pos):
    xh = xt[h * HEAD_DIM:(h + 1) * HEAD_DIM]
    z = jnp.zeros_like(xh)
    return jnp.concatenate([xh, z] if pos == 0 else [z, xh], axis=0)


def _col_bcast(eye_b, rows):
    hi = rows.astype(BF16)
    lo = (rows - hi.astype(F32)).astype(BF16)
    dn = (((1,), (1,)), ((), ()))
    return (lax.dot_general(eye_b, hi, dn, preferred_element_type=F32)
            + lax.dot_general(eye_b, lo, dn, preferred_element_type=F32))


def _rep_heads(rows4, n):
    return jnp.concatenate([jnp.broadcast_to(rows4[h:h + 1], (n, rows4.shape[1])) for h in range(N_HEADS)], axis=0)


def _seg_scan(x, seg_pos, seg_len, op, fill):
    s = 1
    while s < seg_len:
        x = op(x, jnp.where(seg_pos >= s, pltpu.roll(x, s, 1), fill))
        s *= 2
    return x


def _shift_rows(x, s, fill):
    if s % 8 == 0:
        return jnp.concatenate([jnp.full((s, x.shape[1]), fill, x.dtype), x[:x.shape[0] - s]], axis=0)
    row = lax.broadcasted_iota(jnp.int32, x.shape, 0)
    return jnp.where(row >= s, pltpu.roll(x, s, 0), fill)


def _mixer_kernel(pa_ref, pr_ref, kat_ref, kbt_ref, kct_ref, gt_ref, pastk_ref, pastv_ref, rel_ref,
                  cos_ref, sin_ref, cost_ref, sint_ref, dec_ref, qdec_ref, kdect_ref, sdec_ref,
                  ret0_ref, c0_ref, n0_ref, m0_ref,
                  cw_ref, cb_ref, wa_ref, ba_ref, wi_ref, bi_ref, lam_ref, buf0_ref, h0_ref,
                  y_ref, ret_out, c_out, n_out, m_out, h_out, buf_out,
                  bias_sc, spair, ppair, m_sc, win, h_sc, kpadt=None, vpad=None,
                  *, blk, seqs, n_steps, has_past):
    L = blk
    band = A_REACH + L
    n_past = A_REACH // L if not has_past else 0
    n_var = n_past + 1
    bb = pl.program_id(0)
    tb = pl.program_id(1)
    tail = CONV_W - 1
    if n_steps == 1:
        at_first = at_last = lambda f: f()
    else:
        at_first = pl.when(tb == 0)
        at_last = pl.when(tb == n_steps - 1)

    li = lax.broadcasted_iota(jnp.int32, (L, L), 0)
    lj = lax.broadcasted_iota(jnp.int32, (L, L), 1)
    causal_neg = jnp.where(lj <= li, 0.0, -jnp.inf)
    eye_b = jnp.where(li == lj, 1.0, 0.0).astype(BF16)
    lane_lo = lax.broadcasted_iota(jnp.int32, (L, SLAB), 1) < HEAD_DIM
    ones_slab = jnp.ones((L, SLAB), BF16)
    ri = lax.broadcasted_iota(jnp.int32, (SLAB, SLAB), 0) // HEAD_DIM
    rj = lax.broadcasted_iota(jnp.int32, (SLAB, SLAB), 1) // HEAD_DIM
    smask = jnp.where(ri == rj, 1.0, 0.0)
    pmask = jnp.concatenate([smask, smask], axis=1)
    gi = lax.broadcasted_iota(jnp.int32, (W_GROUP, W_GROUP), 0) // HEAD_DIM
    gj = lax.broadcasted_iota(jnp.int32, (W_GROUP, W_GROUP), 1) // HEAD_DIM
    gmat = jnp.where(gi == gj, 1.0 / HEAD_DIM, 0.0).astype(BF16)
    e64i = lax.broadcasted_iota(jnp.int32, (HEAD_DIM, HEAD_DIM), 0)
    e64j = lax.broadcasted_iota(jnp.int32, (HEAD_DIM, HEAD_DIM), 1)
    eye64 = e64i == e64j
    row_lo = lax.broadcasted_iota(jnp.int32, (SLAB, 1), 0) < HEAD_DIM

    @pl.when((bb == 0) & (tb == 0))
    def _():
        qq = lax.broadcasted_iota(jnp.int32, (L, band), 0)
        kk = lax.broadcasted_iota(jnp.int32, (L, band), 1)
        off = kk - (qq // CHUNK) * CHUNK
        for h in range(N_HEADS):
            row = jnp.broadcast_to(rel_ref[h], (L, ROLL_W))
            tile = pltpu.roll(row, ROLL_W - L, 1, stride=1, stride_axis=0)[:, :band] * LOG2E
            tile = jnp.where(off >= 0, jnp.where(off < A_REACH + CHUNK, tile, -jnp.inf), -jnp.inf)
            for v in range(n_var):
                first_col = (n_past - v) * L
                bias_sc[v * N_HEADS + h] = jnp.where(kk >= first_col, tile, -jnp.inf) if first_col > 0 else tile

    for s in range(seqs):
        @at_first
        def _():
            for p in range(N_PAIRS):
                sblk, cblk = [], []
                for hh in range(2):
                    h = 2 * p + hh
                    z = jnp.zeros((HEAD_DIM, HEAD_DIM), F32)
                    s_h = ret0_ref[s, h]
                    c_h = c0_ref[s, h]
                    n_row = n0_ref[s, h:h + 1, :]
                    n_col = jnp.sum(jnp.where(eye64, jnp.broadcast_to(n_row, eye64.shape), 0.0),
                                    axis=1, keepdims=True)
                    n_rep = jnp.broadcast_to(n_col, (HEAD_DIM, HEAD_DIM))
                    sblk.append(jnp.concatenate([s_h, z] if hh == 0 else [z, s_h], axis=1))
                    cblk.append(jnp.concatenate([c_h, z, n_rep, z] if hh == 0 else [z, c_h, z, n_rep], axis=1))
                spair[s, p] = jnp.concatenate(sblk, axis=0)
                ppair[s, p] = jnp.concatenate(cblk, axis=0)
            m_sc[s, 0:N_HEADS, :] = jnp.broadcast_to(m0_ref[s], (N_HEADS, m_sc.shape[2]))
            win[s, CONV_PAD - tail:CONV_PAD, :] = buf0_ref[s]
            h_sc[s] = h0_ref[s]
            if not has_past:
                vpad[s, 0:A_REACH, :] = jnp.zeros((A_REACH, W_GROUP), BF16)
                kpadt[s, 0:n_past * W_GROUP, :] = jnp.zeros((n_past * W_GROUP, L), BF16)

    g8 = gt_ref[...]
    seg_pos = lax.broadcasted_iota(jnp.int32, g8.shape, 1) % L
    b8 = _seg_scan(jax.nn.log_sigmoid(g8), seg_pos, L, jnp.add, 0.0)
    a8 = g8 - pltpu.roll(b8, N_HEADS, 0)
    cm8 = _seg_scan(a8, seg_pos, L, jnp.maximum, -jnp.inf)
    b8 = pltpu.roll(b8, N_HEADS, 0)

    heads = [(h // 2, h % 2, h) for h in range(N_HEADS)]
    slab = lambda p: slice(p * SLAB, (p + 1) * SLAB)
    seg_of = lambda s: slice(s * L, (s + 1) * L)
    col = lambda s, j: pr_ref[s, :, j * W_GROUP:(j + 1) * W_GROUP]

    def halves(per_head):
        return jnp.concatenate([jnp.where(lane_lo, per_head[2 * p], per_head[2 * p + 1]) for p in range(N_PAIRS)],
                               axis=1)


    def attention_chain(s):
        pa = pa_ref[s]
        kat = kat_ref[:, seg_of(s)]
        v_new = pa[:, 2 * W_GROUP:].astype(BF16)
        if has_past:
            kpast = pastk_ref[s].astype(BF16)
            vband = jnp.concatenate([pastv_ref[s].astype(BF16), v_new], axis=0)
            bias_at = lambda h: bias_sc[h]
        else:
            kpadt[s, pl.ds(pl.multiple_of((tb + n_past) * W_GROUP, W_GROUP), W_GROUP), :] = kat
            kwin = kpadt[s, pl.ds(pl.multiple_of(tb * W_GROUP, W_GROUP), (n_past + 1) * W_GROUP), :]
            start = pl.multiple_of(tb * L, L)
            vpad[s, pl.ds(A_REACH + start, L), :] = v_new
            vband = vpad[s, pl.ds(start, band), :]
            var = jnp.minimum(tb, n_var - 1) * N_HEADS
            bias_at = lambda h: bias_sc[var + h]
        qa = (pa[:, 0:W_GROUP] * (HEAD_DIM ** -0.5 * LOG2E)).astype(BF16)
        sc = {}
        for p, hh, h in heads:
            q_slab = qa[:, slab(p)]
            if has_past:
                k_slab = kpast[:, slab(p)]
                lane = lax.broadcasted_iota(jnp.int32, k_slab.shape, 1)
                lane_h = lane < HEAD_DIM if hh == 0 else lane >= HEAD_DIM
                parts = [_dot_nt(q_slab, jnp.where(lane_h, k_slab, jnp.zeros_like(k_slab))),
                         jnp.dot(q_slab, _head_rows(kat, h, hh), preferred_element_type=F32)]
            else:
                parts = [jnp.dot(q_slab, _head_rows(kwin[i * W_GROUP:(i + 1) * W_GROUP], h, hh),
                                 preferred_element_type=F32) for i in range(n_past + 1)]
            sc[h] = jnp.concatenate(parts, axis=1)
        yield
        pe, den = {}, {}
        for _, _, h in heads:
            sh = sc[h] + bias_at(h)
            pe[h] = jnp.exp2(sh - jnp.max(sh, axis=1, keepdims=True))
            den[h] = jnp.sum(pe[h], axis=1, keepdims=True)
        yield
        res = {h: _dot(pe[h], vband)[:, slab(p)] for p, _, h in heads}
        yield
        y_ref[s, :, 0:W_GROUP] = halves({h: res[h] / den[h] for _, _, h in heads})

    def retention_chain(s):
        qr = (col(s, PR_QB) * cos_ref[...] + col(s, PR_QBS) * sin_ref[...]).astype(BF16)
        kbt = kbt_ref[:, seg_of(s)]
        half = HEAD_DIM // 2
        kbt_sw = jnp.concatenate([kbt[h * HEAD_DIM + o:h * HEAD_DIM + o + half]
                                  for h in range(N_HEADS) for o in (half, 0)], axis=0)
        krt = (kbt * cost_ref[:, seg_of(s)] + kbt_sw * sint_ref[:, seg_of(s)]) * HEAD_DIM ** -0.5
        krt_b = krt.astype(BF16)
        kdt_b = (krt * kdect_ref[...]).astype(BF16)
        vb = col(s, PR_VB).astype(BF16)
        raw = {h: jnp.dot(qr[:, slab(p)], _head_rows(krt_b, h, hh), preferred_element_type=F32)
               for p, hh, h in heads}
        s0 = {p: spair[s, p] for p in range(N_PAIRS)}
        inter = {p: _dot(qr[:, slab(p)], s0[p]) for p in range(N_PAIRS)}
        add = {p: _dot(kdt_b[slab(p)], vb[:, slab(p)]) for p in range(N_PAIRS)}
        yield
        scores = {h: raw[h] * dec_ref[h] for _, _, h in heads}
        for p in range(N_PAIRS):
            spair[s, p] = s0[p] * sdec_ref[p] + add[p] * smask
        yield
        o = {h: _dot(scores[h], vb[:, slab(p)]) for p, _, h in heads}
        yield
        ob = halves(o) + jnp.concatenate([inter[p] for p in range(N_PAIRS)], axis=1) * qdec_ref[...]
        mu = _dot(ob, gmat)
        yield
        oc = ob - mu
        var = _dot(oc * oc, gmat)
        yield
        y_ref[s, :, W_GROUP:2 * W_GROUP] = oc * lax.rsqrt(var + LN_EPS) * _silu(col(s, PR_GB))

    def mlstm_chain(s):
        a4 = a8[0:N_HEADS, seg_of(s)]
        b4 = b8[0:N_HEADS, seg_of(s)]
        m0c = m_sc[s, 0:N_HEADS, 0:1]
        big_m = jnp.maximum(m0c, cm8[0:N_HEADS, seg_of(s)])
        m4 = b4 + big_m
        w0_4 = jnp.exp(m0c - big_m)
        m_last = big_m[:, L - 1:L]
        wrow4 = jnp.exp(a4 - m_last)
        kct = kct_ref[:, seg_of(s)] * HEAD_DIM ** -0.5
        kct_b = kct.astype(BF16)
        kw_b = (kct * _rep_heads(wrow4, HEAD_DIM)).astype(BF16)
        qc = col(s, PR_QC).astype(BF16)
        vc = col(s, PR_VC).astype(BF16)
        v_aug = {p: jnp.concatenate([vc[:, slab(p)], ones_slab], axis=1) for p in range(N_PAIRS)}
        p0 = {p: ppair[s, p] for p in range(N_PAIRS)}
        w0_bc = _col_bcast(eye_b, _rep_heads(w0_4, HEAD_DIM))
        floor_bc = _col_bcast(eye_b, _rep_heads(jnp.exp(-m4), HEAD_DIM))
        m_bc = {h: _col_bcast(eye_b, jnp.broadcast_to(big_m[h:h + 1], (L, L))) for _, _, h in heads}
        raw = {h: jnp.dot(qc[:, slab(p)], _head_rows(kct_b, h, hh), preferred_element_type=F32)
               for p, hh, h in heads}
        inter = {p: _dot(qc[:, slab(p)], p0[p]) for p in range(N_PAIRS)}
        add = {p: _dot(kw_b[slab(p)], v_aug[p]) for p in range(N_PAIRS)}
        yield
        qk = {h: raw[h] * jnp.exp(a4[h:h + 1] - m_bc[h] + causal_neg) for _, _, h in heads}
        for p in range(N_PAIRS):
            w_state = jnp.where(row_lo, w0_4[2 * p:2 * p + 1, L - 1:L], w0_4[2 * p + 1:2 * p + 2, L - 1:L])
            ppair[s, p] = p0[p] * w_state + add[p] * pmask
        m_sc[s, 0:N_HEADS, :] = jnp.broadcast_to(m4[:, L - 1:L], (N_HEADS, m_sc.shape[2]))
        yield
        res = {h: _dot(qk[h], v_aug[p]) for p, _, h in heads}
        yield
        hc = []
        for p in range(N_PAIRS):
            w0 = w0_bc[:, slab(p)]
            num = jnp.where(lane_lo, res[2 * p][:, 0:SLAB], res[2 * p + 1][:, 0:SLAB]) + inter[p][:, 0:SLAB] * w0
            den = jnp.where(lane_lo, res[2 * p][:, SLAB:], res[2 * p + 1][:, SLAB:]) + inter[p][:, SLAB:] * w0
            hc.append(num / jnp.maximum(jnp.abs(den), floor_bc[:, slab(p)]))
        hc = jnp.concatenate(hc, axis=1)
        mu = _dot(hc, gmat)
        yield
        hcc = hc - mu
        var = _dot(hcc * hcc, gmat)
        yield
        y_ref[s, :, 2 * W_GROUP:3 * W_GROUP] = hcc * lax.rsqrt(var + LN_EPS) * jax.nn.sigmoid(col(s, PR_OC))

    def rglru_chain(s):
        xd = col(s, PR_XD)
        win[s, CONV_PAD:CONV_PAD + L, :] = xd
        xc = cb_ref[...]
        for j in range(CONV_W):
            off = CONV_PAD - tail + j
            xc = xc + win[s, off:off + L, :] * cw_ref[j:j + 1, :]
        win[s, CONV_PAD - tail:CONV_PAD, :] = xd[L - tail:, :]
        r_pre = _dot(xc, wa_ref[...])
        i_pre = _dot(xc, wi_ref[...])
        yield
        r = jax.nn.sigmoid(r_pre + ba_ref[...])
        i = jax.nn.sigmoid(i_pre + bi_ref[...])
        neg_lam = -lam_ref[...]
        softplus = jnp.maximum(neg_lam, 0.0) + jnp.log1p(jnp.exp(-jnp.abs(neg_lam)))
        log_a = -LRU_C * r * softplus
        a = jnp.exp(log_a)
        u = jnp.sqrt(-jnp.tanh(log_a) * (jnp.exp(2.0 * log_a) + 1.0)) * (i * xc)
        st = 1
        while st < L:
            u = a * _shift_rows(u, st, 0.0) + u
            a = a * _shift_rows(a, st, 1.0)
            st *= 2
            if st in (8, 64):
                yield
        hseq = u + a * h_sc[s]
        h_sc[s] = hseq[L - 1:L, :]
        y_ref[s, :, 3 * W_GROUP:] = hseq * jax.nn.gelu(col(s, PR_YD))

    _run_interleaved([chain(s) for chain in (attention_chain, retention_chain, mlstm_chain, rglru_chain)
                      for s in range(seqs)], lag=1 if seqs == 1 else 0)

    for s in range(seqs):
        @at_last
        def _():
            for p in range(N_PAIRS):
                sp = spair[s, p]
                pp = ppair[s, p]
                for hh in range(2):
                    h = 2 * p + hh
                    rs = slice(hh * HEAD_DIM, (hh + 1) * HEAD_DIM)
                    ret_out[s, h] = sp[rs, rs]
                    c_out[s, h] = pp[rs, rs]
                    n_rep = pp[rs, SLAB + hh * HEAD_DIM:SLAB + (hh + 1) * HEAD_DIM]
                    n_out[s, h:h + 1, :] = jnp.sum(jnp.where(eye64, n_rep, 0.0), axis=0, keepdims=True)
            m_out[s] = m_sc[s, 0:N_HEADS, 0:1]
            h_out[s] = h_sc[s]
            buf_out[s] = pr_ref[s, L - tail:, PR_XD * W_GROUP:(PR_XD + 1) * W_GROUP]


def _mixer_kernel_fresh(pa_ref, pr_ref, kat_ref, kbt_ref, kct_ref, gt_ref, *rest, **static):
    _mixer_kernel(pa_ref, pr_ref, kat_ref, kbt_ref, kct_ref, gt_ref, None, None, *rest, **static)


def _mixer_geometry(t, has_past):
    if has_past:
        assert t == CHUNK, "sequences with a carried cache are expected to be a single chunk"
        return t, MIX_SEQS
    assert t % MIX_BLOCK == 0 and A_REACH % MIX_BLOCK == 0
    return MIX_BLOCK, 1


def _mixer_tables(pos, blk, seqs):
    cos, sin = _rotary_tables(pos)
    return (cos, sin, jnp.tile(cos.T, (1, seqs)), jnp.tile(sin.T, (1, seqs))) + _retention_tables(blk)


def _mixer(pa, pr, kat, kbt, kct, gt, past, rel_rows, tables, rec, lw):
    ret0, c0, n0, m0, h0, buf0, rec_layer = rec
    m0, h0 = m0[..., None], h0[:, :, None, :]
    bsz, t, _ = pa.shape
    has_past = past is not None
    blk, seqs = _mixer_geometry(t, has_past)
    nb = t // blk
    assert bsz % seqs == 0
    band = A_REACH + blk
    n_var = 1 if has_past else A_REACH // blk + 1
    tail = CONV_W - 1

    tok3 = lambda w: pl.BlockSpec((seqs, blk, w), lambda b, c: (b, c, 0))
    trn = lambda r: pl.BlockSpec((r, seqs * blk), lambda b, c: (0, b * nb + c))
    per_seq = lambda *shape: pl.BlockSpec((seqs,) + shape, lambda b, c: (b,) + (0,) * len(shape))
    const = lambda *shape: pl.BlockSpec(shape, lambda b, c: (0,) * len(shape))
    carried = lambda *shape: pl.BlockSpec((None, seqs) + shape, lambda b, c: (rec_layer, b) + (0,) * len(shape))
    past_specs, past_args = [], ()
    if has_past:
        layer = past[2]
        cache = pl.BlockSpec((None, seqs, A_REACH, W_GROUP), lambda b, c: (layer, b, 0, 0))
        past_specs, past_args = [cache, cache], past[:2]
    in_specs = [tok3(3 * W_GROUP), tok3(N_PR * W_GROUP), trn(W_GROUP), trn(W_GROUP), trn(W_GROUP), trn(N_GATES),
                *past_specs, const(N_HEADS, 1, ROLL_W),
                pl.BlockSpec((blk, W_GROUP), lambda b, c: (c, 0)), pl.BlockSpec((blk, W_GROUP), lambda b, c: (c, 0)),
                pl.BlockSpec((W_GROUP, seqs * blk), lambda b, c: (0, c)),
                pl.BlockSpec((W_GROUP, seqs * blk), lambda b, c: (0, c)),
                const(N_HEADS, blk, blk), const(blk, W_GROUP), const(W_GROUP, blk), const(N_PAIRS, SLAB, SLAB),
                carried(N_HEADS, HEAD_DIM, HEAD_DIM), carried(N_HEADS, HEAD_DIM, HEAD_DIM),
                carried(N_HEADS, HEAD_DIM), carried(N_HEADS, 1),
                const(CONV_W, W_GROUP), const(1, W_GROUP), const(W_GROUP, W_GROUP), const(1, W_GROUP),
                const(W_GROUP, W_GROUP), const(1, W_GROUP), const(1, W_GROUP),
                carried(tail, W_GROUP), carried(1, W_GROUP)]
    out_shape = (jax.ShapeDtypeStruct((bsz, t, 4 * W_GROUP), F32),
                 jax.ShapeDtypeStruct((bsz, N_HEADS, HEAD_DIM, HEAD_DIM), F32),
                 jax.ShapeDtypeStruct((bsz, N_HEADS, HEAD_DIM, HEAD_DIM), F32),
                 jax.ShapeDtypeStruct((bsz, N_HEADS, HEAD_DIM), F32),
                 jax.ShapeDtypeStruct((bsz, N_HEADS, 1), F32),
                 jax.ShapeDtypeStruct((bsz, 1, W_GROUP), F32),
                 jax.ShapeDtypeStruct((bsz, tail, W_GROUP), F32))
    out_specs = (tok3(4 * W_GROUP), per_seq(N_HEADS, HEAD_DIM, HEAD_DIM), per_seq(N_HEADS, HEAD_DIM, HEAD_DIM),
                 per_seq(N_HEADS, HEAD_DIM), per_seq(N_HEADS, 1), per_seq(1, W_GROUP), per_seq(tail, W_GROUP))
    scratch = [pltpu.VMEM((n_var * N_HEADS, blk, band), F32),
               pltpu.VMEM((seqs, N_PAIRS, SLAB, SLAB), F32), pltpu.VMEM((seqs, N_PAIRS, SLAB, 2 * SLAB), F32),
               pltpu.VMEM((seqs, 8, 128), F32), pltpu.VMEM((seqs, CONV_PAD + blk, W_GROUP), F32),
               pltpu.VMEM((seqs, 1, W_GROUP), F32)]
    if not has_past:
        scratch += [pltpu.VMEM((seqs, (nb + A_REACH // blk) * W_GROUP, blk), BF16),
                    pltpu.VMEM((seqs, A_REACH + t, W_GROUP), BF16)]
    body = _mixer_kernel if has_past else _mixer_kernel_fresh
    return pl.pallas_call(
        functools.partial(body, blk=blk, seqs=seqs, n_steps=nb, has_past=has_past),
        out_shape=out_shape,
        grid=(bsz // seqs, nb),
        in_specs=in_specs,
        out_specs=out_specs,
        scratch_shapes=scratch,
        compiler_params=_cparams("arbitrary", "arbitrary"),
        name="mixer",
    )(pa, pr, kat, kbt, kct, gt, *past_args, rel_rows, *tables,
      ret0, c0, n0, m0, lw['conv_w'], lw['conv_b'], lw['lru_wa'], lw['lru_ba'], lw['lru_wi'], lw['lru_bi'],
      lw['lru_lambda'], buf0, h0)


def _post_kernel(x_ref, y_ref, wo_ref, g1_ref, b1_ref, wq_ref, mk_ref, mv_ref, xo_ref,
                 g2_ref, b2_ref, o_ref, att_sc, *, alpha, seqs, subtiles):
    rows_sub = x_ref.shape[0] // subtiles
    rows = rows_sub // seqs
    head = lambda h: slice(h * HEAD_DIM, (h + 1) * HEAD_DIM)

    def subtile_chain(i):
        rs = slice(i * rows_sub, (i + 1) * rows_sub)
        x = x_ref[rs, :]
        mix = _dot(y_ref[rs, :], wo_ref[...])
        yield
        x2 = _layer_norm(alpha * x + mix, g1_ref[...], b1_ref[...])
        yield
        q = _dot(x2, wq_ref[...]).astype(BF16)
        yield

        def attend(s, h):
            qs = slice(s * rows, (s + 1) * rows)
            sc = _dot_nt(q[qs, head(h)], mk_ref[s, :, head(h)]) * HEAD_DIM ** -0.5
            yield
            p = jnp.exp(sc - jnp.max(sc, axis=1, keepdims=True))
            den = jnp.sum(p, axis=1, keepdims=True)
            yield
            o = _dot(p, mv_ref[s, :, head(h)])
            yield
            att_sc[i * rows_sub + s * rows:i * rows_sub + (s + 1) * rows, head(h)] = o / den

        yield from _lockstep([attend(s, h) for s in range(seqs) for h in range(N_HEADS)])
        out = _dot(att_sc[rs, :], xo_ref[...])
        yield
        o_ref[rs, :] = _layer_norm(alpha * x2 + out, g2_ref[...], b2_ref[...])

    _run_interleaved([subtile_chain(i) for i in range(subtiles)], lag=2)


def _post(x, y, wo, wq, xo, wlayer, g1, b1, g2, b2, mk, mv, layer, alpha, seq_len, casts=()):
    n, d = x.shape
    n_mem, dx = mk.shape[2], mk.shape[3]
    if seq_len >= POST_SUBTILES * TOKEN_TILE:
        subtiles, seqs = POST_SUBTILES, 1
        tm = subtiles * TOKEN_TILE
        mem = pl.BlockSpec((None, 1, n_mem, dx), lambda i: (layer, i // (seq_len // tm), 0, 0))
    else:
        tm = TOKEN_TILE
        subtiles, seqs = 1, tm // seq_len
        mem = pl.BlockSpec((None, seqs, n_mem, dx), lambda i: (layer, i, 0, 0))
    assert n % tm == 0 and tm % (subtiles * seqs) == 0
    tok = lambda w: pl.BlockSpec((tm, w), lambda i: (i, 0))
    plans = [_cast_plan(stack, lyr, n // tm) for stack, lyr in casts]
    out = pl.pallas_call(
        _with_casts(functools.partial(_post_kernel, alpha=alpha, seqs=seqs, subtiles=subtiles), 11, 1, len(casts)),
        out_shape=(jax.ShapeDtypeStruct((n, d), F32), *[p[2] for p in plans]),
        grid=(n // tm,),
        in_specs=[tok(d), tok(y.shape[1]),
                  _resident(wo.shape[1:], wlayer), _resident((1, d)), _resident((1, d)),
                  _resident(wq.shape[1:], wlayer), mem, mem, _resident(xo.shape[1:], wlayer),
                  _resident((1, d)), _resident((1, d)), *[p[0] for p in plans]],
        out_specs=(tok(d), *[p[1] for p in plans]),
        scratch_shapes=[pltpu.VMEM((tm, dx), F32)],
        compiler_params=_cparams("parallel"),
        name="post",
    )(x, y, wo, g1, b1, wq, mk, mv, xo, g2, b2, *[stack for stack, _ in casts])
    return out[0], out[1:]


def _mem_kv_kernel(x_ref, w_ref, o_ref):
    o_ref[...] = _dot(x_ref[...], w_ref[...])


def _mem_kv(mem, w):
    n, d = mem.shape
    tm = min(TOKEN_TILE, n)
    return pl.pallas_call(
        _mem_kv_kernel,
        out_shape=jax.ShapeDtypeStruct((n, w.shape[1]), F32),
        grid=(n // tm,),
        in_specs=[pl.BlockSpec((tm, d), lambda i: (i, 0)), _resident(w.shape)],
        out_specs=pl.BlockSpec((tm, w.shape[1]), lambda i: (i, 0)),
        compiler_params=_cparams("parallel"),
        name="mem_kv",
    )(mem, w)


def _rotary_tables(pos):
    half = HEAD_DIM // 2
    inv = jnp.exp(-jnp.log(10000.0) * jnp.arange(half, dtype=F32) / half)
    ang = pos.astype(F32)[:, None] * inv[None, :]
    cos, sin = jnp.cos(ang), jnp.sin(ang)
    cos_full = jnp.tile(jnp.concatenate([cos, cos], -1), (1, N_HEADS))
    sin_full = jnp.tile(jnp.concatenate([-sin, sin], -1), (1, N_HEADS))
    return cos_full, sin_full


def _retention_tables(blk):
    log_g = jnp.log1p(-jnp.exp2(-5.0 - jnp.arange(N_HEADS, dtype=F32)))
    idx = jnp.arange(blk, dtype=F32)
    diff = idx[:, None] - idx[None, :]
    dec = jnp.exp(jnp.where((diff >= 0)[None], diff[None] * log_g[:, None, None], -jnp.inf))
    q_dec = jnp.exp((idx[:, None] + 1.0) * log_g[None, :])
    k_dec = jnp.exp((blk - 1.0 - idx)[:, None] * log_g[None, :])
    s_dec = jnp.exp(blk * log_g)
    lanes = lambda tbl: jnp.repeat(tbl, HEAD_DIM, axis=1)
    pair = lambda p: jnp.kron(jnp.diag(s_dec[2 * p:2 * p + 2]), jnp.ones((HEAD_DIM, HEAD_DIM), F32))
    return dec, lanes(q_dec), lanes(k_dec).T, jnp.stack([pair(p) for p in range(N_PAIRS)])


def _rel_bias_rows(rel_bias, blk):
    idx = np.clip(A_REACH + blk - np.arange(ROLL_W), -REL_CLIP, REL_CLIP) + REL_CLIP
    return rel_bias[:, idx][:, None, :]


def _block_diag(w):
    h, c, _ = w.shape
    eye = jnp.eye(h, dtype=w.dtype)
    return (eye[:, None, :, None] * w[:, :, None, :]).reshape(h * c, h * c)


def _swap_perm():
    j = np.arange(W_GROUP)
    return (j // HEAD_DIM) * HEAD_DIM + (j % HEAD_DIM + HEAD_DIM // 2) % HEAD_DIM


def _layer(x, tables, mem, past, rec, lw, ffn1, ffn2, alpha, proj_casts=(), post_casts=()):
    bsz, t, d = x.shape
    n = bsz * t
    g, b = lw['ln_g'], lw['ln_b']

    x1 = _ffn_ln(x.reshape(n, d), *ffn1, g[0:1], b[0:1], alpha)
    (pa, pr, kat, kbt, kct, gt), proj_cast_out = _in_proj(x1, lw['w_in_a'], lw['w_in_r'], lw['w_in_t'],
                                                          lw['b_in_a'], lw['b_in_r'], lw['b_in_t'], proj_casts)
    if ffn2 is None:
        ffn2 = proj_cast_out[:3]
    pa = pa.reshape(bsz, t, 3 * W_GROUP)
    pr = pr.reshape(bsz, t, N_PR * W_GROUP)
    blk, _ = _mixer_geometry(t, past is not None)
    y, ret_new, c_new, n_new, m_new, h_new, conv_new = _mixer(
        pa, pr, kat, kbt, kct, gt, past, _rel_bias_rows(lw['rel_bias'], blk), tables, rec, lw)

    x3, post_cast_out = _post(x1, y.reshape(n, 4 * W_GROUP), lw['w_out'], lw['x_wq'], lw['x_wo'], lw['layer'],
                              g[1:2], b[1:2], g[2:3], b[2:3], *mem, alpha, t, post_casts)
    x4 = _ffn_ln(x3, *ffn2, g[3:4], b[3:4], alpha)

    keep = min(A_REACH, t)
    a_k = pa[:, t - keep:, W_GROUP:2 * W_GROUP]
    a_v = pa[:, t - keep:, 2 * W_GROUP:]
    state = (a_k, a_v, ret_new, c_new, n_new, m_new, h_new, conv_new)
    return x4.reshape(bsz, t, d), state, proj_cast_out, post_cast_out


def _stack_states(per_layer):
    stacked = [jnp.stack(f) for f in zip(*per_layer)]
    split = lambda a: a.reshape(a.shape[:-1] + (N_HEADS, HEAD_DIM))
    a_k, a_v, ret, c, n, m, h, conv = stacked[:8]
    return [split(a_k), split(a_v), ret, c, n, m[..., 0], h[:, :, 0, :], conv] + [split(a) for a in stacked[8:]]


def kernel(x_prompt, x_sample, mem_prompt, cache_a_k, cache_a_v, state_ret, state_mlstm_c, state_mlstm_n,
           state_mlstm_m, state_lru_h, state_conv, cache_mem_k, cache_mem_v, ln_g, ln_b,
           ffn1_gate, ffn1_up, ffn1_down, ffn2_gate, ffn2_up, ffn2_down, w_in, b_in, a_rel_bias,
           conv_w, conv_b, lru_wa, lru_ba, lru_wi, lru_bi, lru_lambda, w_out, x_wq, x_wk, x_wv, x_wo):
    depth = ln_g.shape[0]
    alpha = (2.0 * depth) ** 0.25
    row = lambda v: v[None, :]
    grp = lambda a, j: a[..., j * W_GROUP:(j + 1) * W_GROUP]
    perm = _swap_perm()
    QA, KA, VA, QB, KB, VB, GB, QC, KC, VC, OC, XD, YD = range(13)

    def layer_weights(l):
        w, bias = w_in[l], b_in[l]
        nat = lambda a: jnp.concatenate(
            [grp(a, QB), grp(a, QB)[..., perm], grp(a, VB), grp(a, GB), grp(a, QC), grp(a, VC), grp(a, OC),
             grp(a, XD), grp(a, YD)], axis=-1)
        trn = lambda a: jnp.concatenate([grp(a, KA), grp(a, KB), grp(a, KC), a[..., 13 * W_GROUP:]], axis=-1)
        return {'ln_g': ln_g[l], 'ln_b': ln_b[l],
                'layer': l, **stacks,
                'w_in_a': w[:, :3 * W_GROUP].astype(BF16), 'w_in_r': nat(w).astype(BF16),
                'w_in_t': trn(w).T.astype(BF16),
                'b_in_a': row(bias[:3 * W_GROUP]), 'b_in_r': row(nat(bias)), 'b_in_t': trn(bias)[:, None],
                'rel_bias': a_rel_bias[l],
                'conv_w': conv_w[l], 'conv_b': row(conv_b[l]),
                'lru_wa': _block_diag(lru_wa[l]).astype(BF16), 'lru_ba': row(lru_ba[l]),
                'lru_wi': _block_diag(lru_wi[l]).astype(BF16), 'lru_bi': row(lru_bi[l]),
                'lru_lambda': row(lru_lambda[l]),
                'x_wkv': jnp.concatenate([x_wk[l], x_wv[l]], axis=1).astype(BF16)}

    stacks = {name: w.astype(BF16) for name, w in (('w_out', w_out), ('x_wq', x_wq), ('x_wo', x_wo))}
    ffn1_f32, ffn2_f32 = (ffn1_gate, ffn1_up, ffn1_down), (ffn2_gate, ffn2_up, ffn2_down)
    ffn1_w = {0: tuple(w[0].astype(BF16) for w in ffn1_f32)}
    ffn2_w = {}
    weights = [layer_weights(l) for l in range(depth)]
    dx = x_wk.shape[2]

    bp, tp, d = x_prompt.shape
    n_mem = mem_prompt.shape[1]
    h = x_prompt
    prompt_states = []
    tables = _mixer_tables(jnp.arange(tp), *_mixer_geometry(tp, False))
    rec0 = (jnp.zeros((1, bp, N_HEADS, HEAD_DIM, HEAD_DIM), F32), jnp.zeros((1, bp, N_HEADS, HEAD_DIM, HEAD_DIM), F32),
            jnp.zeros((1, bp, N_HEADS, HEAD_DIM), F32), jnp.zeros((1, bp, N_HEADS), F32),
            jnp.zeros((1, bp, W_GROUP), F32), jnp.zeros((1, bp, CONV_W - 1, W_GROUP), F32), 0)
    for l in range(depth):
        lw = weights[l]
        mkv = _mem_kv(mem_prompt.reshape(bp * n_mem, d), lw['x_wkv'])
        mk = mkv[:, :dx].reshape(bp, n_mem, dx)
        mv = mkv[:, dx:].reshape(bp, n_mem, dx)
        post_casts = [(w, l + 1) for w in ffn1_f32] if l + 1 < depth else []
        h, st, ffn2_w[l], cast_next = _layer(h, tables, (mk[None], mv[None], 0), None, rec0, lw, ffn1_w[l], None,
                                             alpha, [(w, l) for w in ffn2_f32], post_casts)
        if post_casts:
            ffn1_w[l + 1] = cast_next
        prompt_states.append(st + (mk, mv))
    y_prompt = h
    prompt_out = _stack_states(prompt_states)

    bs, ts, _ = x_sample.shape
    h = x_sample
    sample_states = []
    tables = _mixer_tables(PAST_LEN + jnp.arange(ts), *_mixer_geometry(ts, True))
    past_k_all = cache_a_k.reshape(depth, bs, A_REACH, W_GROUP)
    past_v_all = cache_a_v.reshape(depth, bs, A_REACH, W_GROUP)
    mem_k_all = cache_mem_k.reshape(depth, bs, n_mem, dx)
    mem_v_all = cache_mem_v.reshape(depth, bs, n_mem, dx)
    for l in range(depth):
        rec = (state_ret, state_mlstm_c, state_mlstm_n, state_mlstm_m, state_lru_h, state_conv, l)
        h, st, _, _ = _layer(h, tables, (mem_k_all, mem_v_all, l), (past_k_all, past_v_all, l), rec, weights[l],
                             ffn1_w[l], ffn2_w[l], alpha)
        sample_states.append(st)
    y_sample = h
    sample_out = _stack_states(sample_states)

    return (y_prompt, y_sample, *prompt_out, *sample_out)
```

```python
import functools

import numpy as np
import jax
import jax.numpy as jnp
from jax import lax
from jax.experimental import pallas as pl
from jax.experimental.pallas import tpu as pltpu

F32 = jnp.float32
BF16 = jnp.bfloat16

CHUNK = 64
HEAD_DIM = 64
N_HEADS = 4
W_GROUP = N_HEADS * HEAD_DIM
SLAB = 2 * HEAD_DIM
N_PAIRS = N_HEADS // 2
A_BAND_CHUNKS = 8
A_REACH = A_BAND_CHUNKS * CHUNK
REL_CLIP = 128
CONV_W = 4
LRU_C = 8.0
LN_EPS = 1e-5
LOG2E = 1.4426950408889634
N_GATES = 2 * N_HEADS
PAST_LEN = 4096
ROLL_W = 1024
CONV_PAD = 8

V7X_VMEM_LIMIT_BYTES = 56 * 1024 * 1024
TOKEN_TILE = 512
MIX_BLOCK = 256
MIX_SEQS = 4
POST_SUBTILES = 2

PR_QB, PR_QBS, PR_VB, PR_GB, PR_QC, PR_VC, PR_OC, PR_XD, PR_YD = range(9)
N_PR = 9


def _cparams(*sem):
    return pltpu.CompilerParams(dimension_semantics=sem, vmem_limit_bytes=V7X_VMEM_LIMIT_BYTES)


def _dot(a, b):
    return jnp.dot(a.astype(BF16), b.astype(BF16), preferred_element_type=F32)


def _dot_nt(a, b):
    return lax.dot_general(a.astype(BF16), b.astype(BF16), (((1,), (1,)), ((), ())), preferred_element_type=F32)


def _layer_norm(x, g, b):
    mu = jnp.mean(x, -1, keepdims=True)
    xc = x - mu
    var = jnp.mean(xc * xc, -1, keepdims=True)
    return xc * lax.rsqrt(var + LN_EPS) * g + b


def _silu(x):
    return x * jax.nn.sigmoid(x)


def _lockstep(chains):
    while chains:
        chains = [c for c in chains if next(c, "done") != "done"]
        yield


def _run_interleaved(chains, lag=0):
    pending, live, rnd = list(chains), [], 0
    while pending or live:
        while pending and rnd >= lag * (len(chains) - len(pending)):
            live.append(pending.pop(0))
        live = [c for c in live if next(c, "done") != "done"]
        rnd += 1


def _resident(shape, layer=None):
    nd = len(shape)
    if layer is None:
        return pl.BlockSpec(shape, lambda *_: (0,) * nd, pipeline_mode=pl.Buffered(1))
    return pl.BlockSpec((None,) + tuple(shape), lambda *_: (layer,) + (0,) * nd, pipeline_mode=pl.Buffered(1))


def _cast_plan(stack, layer, n_steps):
    _, r, c = stack.shape
    nb = next(k for k in range(n_steps, 0, -1) if n_steps % k == 0 and r % k == 0 and (r // k) % 16 == 0)
    every = n_steps // nb
    return (pl.BlockSpec((None, r // nb, c), lambda i: (layer, i // every, 0)),
            pl.BlockSpec((r // nb, c), lambda i: (i // every, 0)),
            jax.ShapeDtypeStruct((r, c), BF16))


def _with_casts(body, n_in, n_out, n_cast):
    def wrapped(*refs):
        ins, rest = refs[:n_in + n_cast], refs[n_in + n_cast:]
        outs, scratch = rest[:n_out + n_cast], rest[n_out + n_cast:]
        body(*ins[:n_in], *outs[:n_out], *scratch)
        for src, dst in zip(ins[n_in:], outs[n_out:]):
            dst[...] = src[...].astype(BF16)
    return wrapped


def _by_group(body, n_first, n_shared, n_out):
    def wrapped(xa_ref, xb_ref, *refs):
        shared, outs = refs[:n_shared], refs[n_shared:]
        i = pl.program_id(0)

        @pl.when(i < n_first)
        def _():
            body(xa_ref, *shared, *outs[:n_out])

        @pl.when(i >= n_first)
        def _():
            body(xb_ref, *shared, *outs[n_out:2 * n_out])
    return wrapped


def _group_blocks(tm, n_first):
    tok_a = lambda w: pl.BlockSpec((tm, w), lambda i: (jnp.minimum(i, n_first - 1), 0))
    tok_b = lambda w: pl.BlockSpec((tm, w), lambda i: (jnp.maximum(i - n_first, 0), 0))
    return tok_a, tok_b


def _ffn_ln_kernel(x_ref, wg_ref, wu_ref, wd_ref, g_ref, b_ref, o_ref, *, alpha):
    x = x_ref[...]
    xb = x.astype(BF16)
    gate = jnp.dot(xb, wg_ref[...], preferred_element_type=F32)
    up = jnp.dot(xb, wu_ref[...], preferred_element_type=F32)
    h = (_silu(gate) * up).astype(BF16)
    y = jnp.dot(h, wd_ref[...], preferred_element_type=F32)
    o_ref[...] = _layer_norm(alpha * x + 0.5 * y, g_ref[...], b_ref[...])


def _ffn_ln(xa, xb, wg, wu, wd, g, b, alpha):
    d = xa.shape[1]
    dff = wg.shape[1]
    tm = TOKEN_TILE
    na, nb = xa.shape[0] // tm, xb.shape[0] // tm
    tok_a, tok_b = _group_blocks(tm, na)
    return pl.pallas_call(
        _by_group(functools.partial(_ffn_ln_kernel, alpha=alpha), na, 5, 1),
        out_shape=(jax.ShapeDtypeStruct(xa.shape, F32), jax.ShapeDtypeStruct(xb.shape, F32)),
        grid=(na + nb,),
        in_specs=[tok_a(d), tok_b(d),
                  _resident((d, dff)), _resident((d, dff)), _resident((dff, d)),
                  _resident((1, d)), _resident((1, d))],
        out_specs=(tok_a(d), tok_b(d)),
        compiler_params=_cparams("arbitrary"),
        name="ffn_ln",
    )(xa, xb, wg, wu, wd, g, b)


def _in_proj_kernel(x_ref, wa_ref, wr_ref, wt_ref, ba_ref, br_ref, bt_ref,
                    pa_ref, pr_ref, kat_ref, kbt_ref, kct_ref, gt_ref):
    xb = x_ref[...].astype(BF16)
    pa_ref[...] = jnp.dot(xb, wa_ref[...], preferred_element_type=F32) + ba_ref[...]
    pr_ref[...] = jnp.dot(xb, wr_ref[...], preferred_element_type=F32) + br_ref[...]
    tr = _dot_nt(wt_ref[...], xb) + bt_ref[...]
    kat_ref[...] = tr[0:W_GROUP].astype(BF16)
    kbt_ref[...] = tr[W_GROUP:2 * W_GROUP]
    kct_ref[...] = tr[2 * W_GROUP:3 * W_GROUP]
    gt_ref[...] = tr[3 * W_GROUP:3 * W_GROUP + N_GATES]


def _in_proj(x, wa, wr, wt, ba, br, bt, casts=()):
    n, d = x.shape
    ca, cr, ct = wa.shape[1], wr.shape[1], wt.shape[0]
    tm = TOKEN_TILE
    tok = lambda w: pl.BlockSpec((tm, w), lambda i: (i, 0))
    trn = lambda r: pl.BlockSpec((r, tm), lambda i: (0, i))
    plans = [_cast_plan(stack, layer, n // tm) for stack, layer in casts]
    out = pl.pallas_call(
        _with_casts(_in_proj_kernel, 7, 6, len(casts)),
        out_shape=(jax.ShapeDtypeStruct((n, ca), F32), jax.ShapeDtypeStruct((n, cr), F32),
                   jax.ShapeDtypeStruct((W_GROUP, n), BF16), jax.ShapeDtypeStruct((W_GROUP, n), F32),
                   jax.ShapeDtypeStruct((W_GROUP, n), F32), jax.ShapeDtypeStruct((N_GATES, n), F32),
                   *[p[2] for p in plans]),
        grid=(n // tm,),
        in_specs=[tok(d), _resident((d, ca)), _resident((d, cr)), _resident((ct, d)),
                  _resident((1, ca)), _resident((1, cr)), _resident((ct, 1)), *[p[0] for p in plans]],
        out_specs=(tok(ca), tok(cr), trn(W_GROUP), trn(W_GROUP), trn(W_GROUP), trn(N_GATES), *[p[1] for p in plans]),
        compiler_params=_cparams("parallel"),
        name="in_proj",
    )(x, wa, wr, wt, ba, br, bt, *[stack for stack, _ in casts])
    return out[:6], out[6:]


def _head_rows(xt, h, pos):
    xh = xt[h * HEAD_DIM:(h + 1) * HEAD_DIM]
    z = jnp.zeros_like(xh)
    return jnp.concatenate([xh, z] if pos == 0 else [z, xh], axis=0)


def _col_bcast(eye_b, rows):
    hi = rows.astype(BF16)
    lo = (rows - hi.astype(F32)).astype(BF16)
    dn = (((1,), (1,)), ((), ()))
    return (lax.dot_general(eye_b, hi, dn, preferred_element_type=F32)
            + lax.dot_general(eye_b, lo, dn, preferred_element_type=F32))


def _rep_heads(rows4, n):
    return jnp.concatenate([jnp.broadcast_to(rows4[h:h + 1], (n, rows4.shape[1])) for h in range(N_HEADS)], axis=0)


def _seg_scan(x, seg_pos, seg_len, op, fill):
    s = 1
    while s < seg_len:
        x = op(x, jnp.where(seg_pos >= s, pltpu.roll(x, s, 1), fill))
        s *= 2
    return x


def _shift_rows(x, s, fill):
    if s % 8 == 0:
        return jnp.concatenate([jnp.full((s, x.shape[1]), fill, x.dtype), x[:x.shape[0] - s]], axis=0)
    row = lax.broadcasted_iota(jnp.int32, x.shape, 0)
    return jnp.where(row >= s, pltpu.roll(x, s, 0), fill)


def _mixer_kernel(pa_ref, pr_ref, kat_ref, kbt_ref, kct_ref, gt_ref, pastk_ref, pastv_ref, rel_ref,
                  cos_ref, sin_ref, cost_ref, sint_ref, dec_ref, qdec_ref, kdect_ref, sdec_ref,
                  ret0_ref, c0_ref, n0_ref, m0_ref,
                  cw_ref, cb_ref, wa_ref, ba_ref, wi_ref, bi_ref, lam_ref, buf0_ref, h0_ref,
                  y_ref, ret_out, c_out, n_out, m_out, h_out, buf_out,
                  bias_sc, spair, ppair, m_sc, win, h_sc, kpadt=None, vpad=None,
                  *, blk, seqs, n_steps, has_past):
    L = blk
    band = A_REACH + L
    n_past = A_REACH // L if not has_past else 0
    n_var = n_past + 1
    bb = pl.program_id(0)
    tb = pl.program_id(1)
    tail = CONV_W - 1
    if n_steps == 1:
        at_first = at_last = lambda f: f()
    else:
        at_first = pl.when(tb == 0)
        at_last = pl.when(tb == n_steps - 1)

    li = lax.broadcasted_iota(jnp.int32, (L, L), 0)
    lj = lax.broadcasted_iota(jnp.int32, (L, L), 1)
    causal_neg = jnp.where(lj <= li, 0.0, -jnp.inf)
    eye_b = jnp.where(li == lj, 1.0, 0.0).astype(BF16)
    lane_lo = lax.broadcasted_iota(jnp.int32, (L, SLAB), 1) < HEAD_DIM
    ones_slab = jnp.ones((L, SLAB), BF16)
    ri = lax.broadcasted_iota(jnp.int32, (SLAB, SLAB), 0) // HEAD_DIM
    rj = lax.broadcasted_iota(jnp.int32, (SLAB, SLAB), 1) // HEAD_DIM
    smask = jnp.where(ri == rj, 1.0, 0.0)
    pmask = jnp.concatenate([smask, smask], axis=1)
    gi = lax.broadcasted_iota(jnp.int32, (W_GROUP, W_GROUP), 0) // HEAD_DIM
    gj = lax.broadcasted_iota(jnp.int32, (W_GROUP, W_GROUP), 1) // HEAD_DIM
    gmat = jnp.where(gi == gj, 1.0 / HEAD_DIM, 0.0).astype(BF16)
    e64i = lax.broadcasted_iota(jnp.int32, (HEAD_DIM, HEAD_DIM), 0)
    e64j = lax.broadcasted_iota(jnp.int32, (HEAD_DIM, HEAD_DIM), 1)
    eye64 = e64i == e64j
    row_lo = lax.broadcasted_iota(jnp.int32, (SLAB, 1), 0) < HEAD_DIM

    @pl.when((bb == 0) & (tb == 0))
    def _():
        qq = lax.broadcasted_iota(jnp.int32, (L, band), 0)
        kk = lax.broadcasted_iota(jnp.int32, (L, band), 1)
        off = kk - (qq // CHUNK) * CHUNK
        for h in range(N_HEADS):
            row = jnp.broadcast_to(rel_ref[h], (L, ROLL_W))
            tile = pltpu.roll(row, ROLL_W - L, 1, stride=1, stride_axis=0)[:, :band] * LOG2E
            tile = jnp.where(off >= 0, jnp.where(off < A_REACH + CHUNK, tile, -jnp.inf), -jnp.inf)
            for v in range(n_var):
                first_col = (n_past - v) * L
                bias_sc[v * N_HEADS + h] = jnp.where(kk >= first_col, tile, -jnp.inf) if first_col > 0 else tile

    for s in range(seqs):
        @at_first
        def _():
            for p in range(N_PAIRS):
                sblk, cblk = [], []
                for hh in range(2):
                    h = 2 * p + hh
                    z = jnp.zeros((HEAD_DIM, HEAD_DIM), F32)
                    s_h = ret0_ref[s, h]
                    c_h = c0_ref[s, h]
                    n_row = n0_ref[s, h:h + 1, :]
                    n_col = jnp.sum(jnp.where(eye64, jnp.broadcast_to(n_row, eye64.shape), 0.0),
                                    axis=1, keepdims=True)
                    n_rep = jnp.broadcast_to(n_col, (HEAD_DIM, HEAD_DIM))
                    sblk.append(jnp.concatenate([s_h, z] if hh == 0 else [z, s_h], axis=1))
                    cblk.append(jnp.concatenate([c_h, z, n_rep, z] if hh == 0 else [z, c_h, z, n_rep], axis=1))
                spair[s, p] = jnp.concatenate(sblk, axis=0)
                ppair[s, p] = jnp.concatenate(cblk, axis=0)
            m_sc[s, 0:N_HEADS, :] = jnp.broadcast_to(m0_ref[s], (N_HEADS, m_sc.shape[2]))
            win[s, CONV_PAD - tail:CONV_PAD, :] = buf0_ref[s]
            h_sc[s] = h0_ref[s]
            if not has_past:
                vpad[s, 0:A_REACH, :] = jnp.zeros((A_REACH, W_GROUP), BF16)
                kpadt[s, 0:n_past * W_GROUP, :] = jnp.zeros((n_past * W_GROUP, L), BF16)

    g8 = gt_ref[...]
    seg_pos = lax.broadcasted_iota(jnp.int32, g8.shape, 1) % L
    b8 = _seg_scan(jax.nn.log_sigmoid(g8), seg_pos, L, jnp.add, 0.0)
    a8 = g8 - pltpu.roll(b8, N_HEADS, 0)
    cm8 = _seg_scan(a8, seg_pos, L, jnp.maximum, -jnp.inf)
    b8 = pltpu.roll(b8, N_HEADS, 0)

    heads = [(h // 2, h % 2, h) for h in range(N_HEADS)]
    slab = lambda p: slice(p * SLAB, (p + 1) * SLAB)
    seg_of = lambda s: slice(s * L, (s + 1) * L)
    col = lambda s, j: pr_ref[s, :, j * W_GROUP:(j + 1) * W_GROUP]

    def halves(per_head):
        return jnp.concatenate([jnp.where(lane_lo, per_head[2 * p], per_head[2 * p + 1]) for p in range(N_PAIRS)],
                               axis=1)


    def attention_chain(s):
        pa = pa_ref[s]
        kat = kat_ref[:, seg_of(s)]
        v_new = pa[:, 2 * W_GROUP:].astype(BF16)
        if has_past:
            kpast = pastk_ref[s].astype(BF16)
            vband = jnp.concatenate([pastv_ref[s].astype(BF16), v_new], axis=0)
            bias_at = lambda h: bias_sc[h]
        else:
            kpadt[s, pl.ds(pl.multiple_of((tb + n_past) * W_GROUP, W_GROUP), W_GROUP), :] = kat
            kwin = kpadt[s, pl.ds(pl.multiple_of(tb * W_GROUP, W_GROUP), (n_past + 1) * W_GROUP), :]
            start = pl.multiple_of(tb * L, L)
            vpad[s, pl.ds(A_REACH + start, L), :] = v_new
            vband = vpad[s, pl.ds(start, band), :]
            var = jnp.minimum(tb, n_var - 1) * N_HEADS
            bias_at = lambda h: bias_sc[var + h]
        qa = (pa[:, 0:W_GROUP] * (HEAD_DIM ** -0.5 * LOG2E)).astype(BF16)
        sc = {}
        for p, hh, h in heads:
            q_slab = qa[:, slab(p)]
            if has_past:
                k_slab = kpast[:, slab(p)]
                lane = lax.broadcasted_iota(jnp.int32, k_slab.shape, 1)
                lane_h = lane < HEAD_DIM if hh == 0 else lane >= HEAD_DIM
                parts = [_dot_nt(q_slab, jnp.where(lane_h, k_slab, jnp.zeros_like(k_slab))),
                         jnp.dot(q_slab, _head_rows(kat, h, hh), preferred_element_type=F32)]
            else:
                parts = [jnp.dot(q_slab, _head_rows(kwin[i * W_GROUP:(i + 1) * W_GROUP], h, hh),
                                 preferred_element_type=F32) for i in range(n_past + 1)]
            sc[h] = jnp.concatenate(parts, axis=1)
        yield
        pe, den = {}, {}
        for _, _, h in heads:
            sh = sc[h] + bias_at(h)
            pe[h] = jnp.exp2(sh - jnp.max(sh, axis=1, keepdims=True))
            den[h] = jnp.sum(pe[h], axis=1, keepdims=True)
        yield
        res = {h: _dot(pe[h], vband)[:, slab(p)] for p, _, h in heads}
        yield
        y_ref[s, :, 0:W_GROUP] = halves({h: res[h] / den[h] for _, _, h in heads})

    def retention_chain(s):
        qr = (col(s, PR_QB) * cos_ref[...] + col(s, PR_QBS) * sin_ref[...]).astype(BF16)
        kbt = kbt_ref[:, seg_of(s)]
        half = HEAD_DIM // 2
        kbt_sw = jnp.concatenate([kbt[h * HEAD_DIM + o:h * HEAD_DIM + o + half]
                                  for h in range(N_HEADS) for o in (half, 0)], axis=0)
        krt = (kbt * cost_ref[:, seg_of(s)] + kbt_sw * sint_ref[:, seg_of(s)]) * HEAD_DIM ** -0.5
        krt_b = krt.astype(BF16)
        kdt_b = (krt * kdect_ref[...]).astype(BF16)
        vb = col(s, PR_VB).astype(BF16)
        raw = {h: jnp.dot(qr[:, slab(p)], _head_rows(krt_b, h, hh), preferred_element_type=F32)
               for p, hh, h in heads}
        s0 = {p: spair[s, p] for p in range(N_PAIRS)}
        inter = {p: _dot(qr[:, slab(p)], s0[p]) for p in range(N_PAIRS)}
        add = {p: _dot(kdt_b[slab(p)], vb[:, slab(p)]) for p in range(N_PAIRS)}
        yield
        scores = {h: raw[h] * dec_ref[h] for _, _, h in heads}
        for p in range(N_PAIRS):
            spair[s, p] = s0[p] * sdec_ref[p] + add[p] * smask
        yield
        o = {h: _dot(scores[h], vb[:, slab(p)]) for p, _, h in heads}
        yield
        ob = halves(o) + jnp.concatenate([inter[p] for p in range(N_PAIRS)], axis=1) * qdec_ref[...]
        mu = _dot(ob, gmat)
        yield
        oc = ob - mu
        var = _dot(oc * oc, gmat)
        yield
        y_ref[s, :, W_GROUP:2 * W_GROUP] = oc * lax.rsqrt(var + LN_EPS) * _silu(col(s, PR_GB))

    def mlstm_chain(s):
        a4 = a8[0:N_HEADS, seg_of(s)]
        b4 = b8[0:N_HEADS, seg_of(s)]
        m0c = m_sc[s, 0:N_HEADS, 0:1]
        big_m = jnp.maximum(m0c, cm8[0:N_HEADS, seg_of(s)])
        m4 = b4 + big_m
        w0_4 = jnp.exp(m0c - big_m)
        m_last = big_m[:, L - 1:L]
        wrow4 = jnp.exp(a4 - m_last)
        kct = kct_ref[:, seg_of(s)] * HEAD_DIM ** -0.5
        kct_b = kct.astype(BF16)
        kw_b = (kct * _rep_heads(wrow4, HEAD_DIM)).astype(BF16)
        qc = col(s, PR_QC).astype(BF16)
        vc = col(s, PR_VC).astype(BF16)
        v_aug = {p: jnp.concatenate([vc[:, slab(p)], ones_slab], axis=1) for p in range(N_PAIRS)}
        p0 = {p: ppair[s, p] for p in range(N_PAIRS)}
        w0_bc = _col_bcast(eye_b, _rep_heads(w0_4, HEAD_DIM))
        floor_bc = _col_bcast(eye_b, _rep_heads(jnp.exp(-m4), HEAD_DIM))
        m_bc = {h: _col_bcast(eye_b, jnp.broadcast_to(big_m[h:h + 1], (L, L))) for _, _, h in heads}
        raw = {h: jnp.dot(qc[:, slab(p)], _head_rows(kct_b, h, hh), preferred_element_type=F32)
               for p, hh, h in heads}
        inter = {p: _dot(qc[:, slab(p)], p0[p]) for p in range(N_PAIRS)}
        add = {p: _dot(kw_b[slab(p)], v_aug[p]) for p in range(N_PAIRS)}
        yield
        qk = {h: raw[h] * jnp.exp(a4[h:h + 1] - m_bc[h] + causal_neg) for _, _, h in heads}
        for p in range(N_PAIRS):
            w_state = jnp.where(row_lo, w0_4[2 * p:2 * p + 1, L - 1:L], w0_4[2 * p + 1:2 * p + 2, L - 1:L])
            ppair[s, p] = p0[p] * w_state + add[p] * pmask
        m_sc[s, 0:N_HEADS, :] = jnp.broadcast_to(m4[:, L - 1:L], (N_HEADS, m_sc.shape[2]))
        yield
        res = {h: _dot(qk[h], v_aug[p]) for p, _, h in heads}
        yield
        hc = []
        for p in range(N_PAIRS):
            w0 = w0_bc[:, slab(p)]
            num = jnp.where(lane_lo, res[2 * p][:, 0:SLAB], res[2 * p + 1][:, 0:SLAB]) + inter[p][:, 0:SLAB] * w0
            den = jnp.where(lane_lo, res[2 * p][:, SLAB:], res[2 * p + 1][:, SLAB:]) + inter[p][:, SLAB:] * w0
            hc.append(num / jnp.maximum(jnp.abs(den), floor_bc[:, slab(p)]))
        hc = jnp.concatenate(hc, axis=1)
        mu = _dot(hc, gmat)
        yield
        hcc = hc - mu
        var = _dot(hcc * hcc, gmat)
        yield
        y_ref[s, :, 2 * W_GROUP:3 * W_GROUP] = hcc * lax.rsqrt(var + LN_EPS) * jax.nn.sigmoid(col(s, PR_OC))

    def rglru_chain(s):
        xd = col(s, PR_XD)
        win[s, CONV_PAD:CONV_PAD + L, :] = xd
        xc = cb_ref[...]
        for j in range(CONV_W):
            off = CONV_PAD - tail + j
            xc = xc + win[s, off:off + L, :] * cw_ref[j:j + 1, :]
        win[s, CONV_PAD - tail:CONV_PAD, :] = xd[L - tail:, :]
        r_pre = _dot(xc, wa_ref[...])
        i_pre = _dot(xc, wi_ref[...])
        yield
        r = jax.nn.sigmoid(r_pre + ba_ref[...])
        i = jax.nn.sigmoid(i_pre + bi_ref[...])
        neg_lam = -lam_ref[...]
        softplus = jnp.maximum(neg_lam, 0.0) + jnp.log1p(jnp.exp(-jnp.abs(neg_lam)))
        log_a = -LRU_C * r * softplus
        a = jnp.exp(log_a)
        u = jnp.sqrt(-jnp.tanh(log_a) * (jnp.exp(2.0 * log_a) + 1.0)) * (i * xc)
        st = 1
        while st < L:
            u = a * _shift_rows(u, st, 0.0) + u
            a = a * _shift_rows(a, st, 1.0)
            st *= 2
            if st in (8, 64):
                yield
        hseq = u + a * h_sc[s]
        h_sc[s] = hseq[L - 1:L, :]
        y_ref[s, :, 3 * W_GROUP:] = hseq * jax.nn.gelu(col(s, PR_YD))

    _run_interleaved([chain(s) for chain in (attention_chain, retention_chain, mlstm_chain, rglru_chain)
                      for s in range(seqs)], lag=1 if seqs == 1 else 0)

    for s in range(seqs):
        @at_last
        def _():
            for p in range(N_PAIRS):
                sp = spair[s, p]
                pp = ppair[s, p]
                for hh in range(2):
                    h = 2 * p + hh
                    rs = slice(hh * HEAD_DIM, (hh + 1) * HEAD_DIM)
                    ret_out[s, h] = sp[rs, rs]
                    c_out[s, h] = pp[rs, rs]
                    n_rep = pp[rs, SLAB + hh * HEAD_DIM:SLAB + (hh + 1) * HEAD_DIM]
                    n_out[s, h:h + 1, :] = jnp.sum(jnp.where(eye64, n_rep, 0.0), axis=0, keepdims=True)
            m_out[s] = m_sc[s, 0:N_HEADS, 0:1]
            h_out[s] = h_sc[s]
            buf_out[s] = pr_ref[s, L - tail:, PR_XD * W_GROUP:(PR_XD + 1) * W_GROUP]


def _mixer_kernel_fresh(pa_ref, pr_ref, kat_ref, kbt_ref, kct_ref, gt_ref, *rest, **static):
    _mixer_kernel(pa_ref, pr_ref, kat_ref, kbt_ref, kct_ref, gt_ref, None, None, *rest, **static)


def _mixer_geometry(t, has_past):
    if has_past:
        assert t == CHUNK, "sequences with a carried cache are expected to be a single chunk"
        return t, MIX_SEQS
    assert t % MIX_BLOCK == 0 and A_REACH % MIX_BLOCK == 0
    return MIX_BLOCK, 1


def _mixer_tables(pos, blk, seqs):
    cos, sin = _rotary_tables(pos)
    return (cos, sin, jnp.tile(cos.T, (1, seqs)), jnp.tile(sin.T, (1, seqs))) + _retention_tables(blk)


def _mixer(pa, pr, kat, kbt, kct, gt, past, rel_rows, tables, rec, lw):
    ret0, c0, n0, m0, h0, buf0, rec_layer = rec
    m0, h0 = m0[..., None], h0[:, :, None, :]
    bsz, t, _ = pa.shape
    has_past = past is not None
    blk, seqs = _mixer_geometry(t, has_past)
    nb = t // blk
    assert bsz % seqs == 0
    band = A_REACH + blk
    n_var = 1 if has_past else A_REACH // blk + 1
    tail = CONV_W - 1

    tok3 = lambda w: pl.BlockSpec((seqs, blk, w), lambda b, c: (b, c, 0))
    trn = lambda r: pl.BlockSpec((r, seqs * blk), lambda b, c: (0, b * nb + c))
    per_seq = lambda *shape: pl.BlockSpec((seqs,) + shape, lambda b, c: (b,) + (0,) * len(shape))
    const = lambda *shape: pl.BlockSpec(shape, lambda b, c: (0,) * len(shape))
    carried = lambda *shape: pl.BlockSpec((None, seqs) + shape, lambda b, c: (rec_layer, b) + (0,) * len(shape))
    past_specs, past_args = [], ()
    if has_past:
        layer = past[2]
        cache = pl.BlockSpec((None, seqs, A_REACH, W_GROUP), lambda b, c: (layer, b, 0, 0))
        past_specs, past_args = [cache, cache], past[:2]
    in_specs = [tok3(3 * W_GROUP), tok3(N_PR * W_GROUP), trn(W_GROUP), trn(W_GROUP), trn(W_GROUP), trn(N_GATES),
                *past_specs, const(N_HEADS, 1, ROLL_W),
                pl.BlockSpec((blk, W_GROUP), lambda b, c: (c, 0)), pl.BlockSpec((blk, W_GROUP), lambda b, c: (c, 0)),
                pl.BlockSpec((W_GROUP, seqs * blk), lambda b, c: (0, c)),
                pl.BlockSpec((W_GROUP, seqs * blk), lambda b, c: (0, c)),
                const(N_HEADS, blk, blk), const(blk, W_GROUP), const(W_GROUP, blk), const(N_PAIRS, SLAB, SLAB),
                carried(N_HEADS, HEAD_DIM, HEAD_DIM), carried(N_HEADS, HEAD_DIM, HEAD_DIM),
                carried(N_HEADS, HEAD_DIM), carried(N_HEADS, 1),
                const(CONV_W, W_GROUP), const(1, W_GROUP), const(W_GROUP, W_GROUP), const(1, W_GROUP),
                const(W_GROUP, W_GROUP), const(1, W_GROUP), const(1, W_GROUP),
                carried(tail, W_GROUP), carried(1, W_GROUP)]
    out_shape = (jax.ShapeDtypeStruct((bsz, t, 4 * W_GROUP), F32),
                 jax.ShapeDtypeStruct((bsz, N_HEADS, HEAD_DIM, HEAD_DIM), F32),
                 jax.ShapeDtypeStruct((bsz, N_HEADS, HEAD_DIM, HEAD_DIM), F32),
                 jax.ShapeDtypeStruct((bsz, N_HEADS, HEAD_DIM), F32),
                 jax.ShapeDtypeStruct((bsz, N_HEADS, 1), F32),
                 jax.ShapeDtypeStruct((bsz, 1, W_GROUP), F32),
                 jax.ShapeDtypeStruct((bsz, tail, W_GROUP), F32))
    out_specs = (tok3(4 * W_GROUP), per_seq(N_HEADS, HEAD_DIM, HEAD_DIM), per_seq(N_HEADS, HEAD_DIM, HEAD_DIM),
                 per_seq(N_HEADS, HEAD_DIM), per_seq(N_HEADS, 1), per_seq(1, W_GROUP), per_seq(tail, W_GROUP))
    scratch = [pltpu.VMEM((n_var * N_HEADS, blk, band), F32),
               pltpu.VMEM((seqs, N_PAIRS, SLAB, SLAB), F32), pltpu.VMEM((seqs, N_PAIRS, SLAB, 2 * SLAB), F32),
               pltpu.VMEM((seqs, 8, 128), F32), pltpu.VMEM((seqs, CONV_PAD + blk, W_GROUP), F32),
               pltpu.VMEM((seqs, 1, W_GROUP), F32)]
    if not has_past:
        scratch += [pltpu.VMEM((seqs, (nb + A_REACH // blk) * W_GROUP, blk), BF16),
                    pltpu.VMEM((seqs, A_REACH + t, W_GROUP), BF16)]
    body = _mixer_kernel if has_past else _mixer_kernel_fresh
    return pl.pallas_call(
        functools.partial(body, blk=blk, seqs=seqs, n_steps=nb, has_past=has_past),
        out_shape=out_shape,
        grid=(bsz // seqs, nb),
        in_specs=in_specs,
        out_specs=out_specs,
        scratch_shapes=scratch,
        compiler_params=_cparams("arbitrary", "arbitrary"),
        name="mixer",
    )(pa, pr, kat, kbt, kct, gt, *past_args, rel_rows, *tables,
      ret0, c0, n0, m0, lw['conv_w'], lw['conv_b'], lw['lru_wa'], lw['lru_ba'], lw['lru_wi'], lw['lru_bi'],
      lw['lru_lambda'], buf0, h0)


def _post_kernel(x_ref, y_ref, wo_ref, g1_ref, b1_ref, wq_ref, mk_ref, mv_ref, xo_ref,
                 g2_ref, b2_ref, o_ref, att_sc, *, alpha, seqs, subtiles):
    rows_sub = x_ref.shape[0] // subtiles
    rows = rows_sub // seqs
    head = lambda h: slice(h * HEAD_DIM, (h + 1) * HEAD_DIM)

    def subtile_chain(i):
        rs = slice(i * rows_sub, (i + 1) * rows_sub)
        x = x_ref[rs, :]
        mix = _dot(y_ref[rs, :], wo_ref[...])
        yield
        x2 = _layer_norm(alpha * x + mix, g1_ref[...], b1_ref[...])
        yield
        q = _dot(x2, wq_ref[...]).astype(BF16)
        yield

        def attend(s, h):
            qs = slice(s * rows, (s + 1) * rows)
            sc = _dot_nt(q[qs, head(h)], mk_ref[s, :, head(h)]) * HEAD_DIM ** -0.5
            yield
            p = jnp.exp(sc - jnp.max(sc, axis=1, keepdims=True))
            den = jnp.sum(p, axis=1, keepdims=True)
            yield
            o = _dot(p, mv_ref[s, :, head(h)])
            yield
            att_sc[i * rows_sub + s * rows:i * rows_sub + (s + 1) * rows, head(h)] = o / den

        yield from _lockstep([attend(s, h) for s in range(seqs) for h in range(N_HEADS)])
        out = _dot(att_sc[rs, :], xo_ref[...])
        yield
        o_ref[rs, :] = _layer_norm(alpha * x2 + out, g2_ref[...], b2_ref[...])

    _run_interleaved([subtile_chain(i) for i in range(subtiles)], lag=2)


def _post(x, y, wo, wq, xo, wlayer, g1, b1, g2, b2, mk, mv, layer, alpha, seq_len, casts=()):
    n, d = x.shape
    n_mem, dx = mk.shape[2], mk.shape[3]
    if seq_len >= POST_SUBTILES * TOKEN_TILE:
        subtiles, seqs = POST_SUBTILES, 1
        tm = subtiles * TOKEN_TILE
        mem = pl.BlockSpec((None, 1, n_mem, dx), lambda i: (layer, i // (seq_len // tm), 0, 0))
    else:
        tm = TOKEN_TILE
        subtiles, seqs = 1, tm // seq_len
        mem = pl.BlockSpec((None, seqs, n_mem, dx), lambda i: (layer, i, 0, 0))
    assert n % tm == 0 and tm % (subtiles * seqs) == 0
    tok = lambda w: pl.BlockSpec((tm, w), lambda i: (i, 0))
    plans = [_cast_plan(stack, lyr, n // tm) for stack, lyr in casts]
    out = pl.pallas_call(
        _with_casts(functools.partial(_post_kernel, alpha=alpha, seqs=seqs, subtiles=subtiles), 11, 1, len(casts)),
        out_shape=(jax.ShapeDtypeStruct((n, d), F32), *[p[2] for p in plans]),
        grid=(n // tm,),
        in_specs=[tok(d), tok(y.shape[1]),
                  _resident(wo.shape[1:], wlayer), _resident((1, d)), _resident((1, d)),
                  _resident(wq.shape[1:], wlayer), mem, mem, _resident(xo.shape[1:], wlayer),
                  _resident((1, d)), _resident((1, d)), *[p[0] for p in plans]],
        out_specs=(tok(d), *[p[1] for p in plans]),
        scratch_shapes=[pltpu.VMEM((tm, dx), F32)],
        compiler_params=_cparams("parallel"),
        name="post",
    )(x, y, wo, g1, b1, wq, mk, mv, xo, g2, b2, *[stack for stack, _ in casts])
    return out[0], out[1:]


def _mem_kv_kernel(x_ref, w_ref, o_ref):
    o_ref[...] = _dot(x_ref[...], w_ref[...])


def _mem_kv(mem, w):
    n, d = mem.shape
    tm = min(TOKEN_TILE, n)
    return pl.pallas_call(
        _mem_kv_kernel,
        out_shape=jax.ShapeDtypeStruct((n, w.shape[1]), F32),
        grid=(n // tm,),
        in_specs=[pl.BlockSpec((tm, d), lambda i: (i, 0)), _resident(w.shape)],
        out_specs=pl.BlockSpec((tm, w.shape[1]), lambda i: (i, 0)),
        compiler_params=_cparams("parallel"),
        name="mem_kv",
    )(mem, w)


def _rotary_tables(pos):
    half = HEAD_DIM // 2
    inv = jnp.exp(-jnp.log(10000.0) * jnp.arange(half, dtype=F32) / half)
    ang = pos.astype(F32)[:, None] * inv[None, :]
    cos, sin = jnp.cos(ang), jnp.sin(ang)
    cos_full = jnp.tile(jnp.concatenate([cos, cos], -1), (1, N_HEADS))
    sin_full = jnp.tile(jnp.concatenate([-sin, sin], -1), (1, N_HEADS))
    return cos_full, sin_full


def _retention_tables(blk):
    log_g = jnp.log1p(-jnp.exp2(-5.0 - jnp.arange(N_HEADS, dtype=F32)))
    idx = jnp.arange(blk, dtype=F32)
    diff = idx[:, None] - idx[None, :]
    dec = jnp.exp(jnp.where((diff >= 0)[None], diff[None] * log_g[:, None, None], -jnp.inf))
    q_dec = jnp.exp((idx[:, None] + 1.0) * log_g[None, :])
    k_dec = jnp.exp((blk - 1.0 - idx)[:, None] * log_g[None, :])
    s_dec = jnp.exp(blk * log_g)
    lanes = lambda tbl: jnp.repeat(tbl, HEAD_DIM, axis=1)
    pair = lambda p: jnp.kron(jnp.diag(s_dec[2 * p:2 * p + 2]), jnp.ones((HEAD_DIM, HEAD_DIM), F32))
    return dec, lanes(q_dec), lanes(k_dec).T, jnp.stack([pair(p) for p in range(N_PAIRS)])


def _rel_bias_rows(rel_bias, blk):
    idx = np.clip(A_REACH + blk - np.arange(ROLL_W), -REL_CLIP, REL_CLIP) + REL_CLIP
    return rel_bias[:, idx][:, None, :]


def _block_diag(w):
    h, c, _ = w.shape
    eye = jnp.eye(h, dtype=w.dtype)
    return (eye[:, None, :, None] * w[:, :, None, :]).reshape(h * c, h * c)


def _swap_perm():
    j = np.arange(W_GROUP)
    return (j // HEAD_DIM) * HEAD_DIM + (j % HEAD_DIM + HEAD_DIM // 2) % HEAD_DIM


def _mix_and_post(x1, proj, shape, tables, mem, past, rec, lw, alpha, post_casts=()):
    bsz, t, d = shape
    n = bsz * t
    g, b = lw['ln_g'], lw['ln_b']
    pa, pr, kat, kbt, kct, gt = proj
    pa = pa.reshape(bsz, t, 3 * W_GROUP)
    pr = pr.reshape(bsz, t, N_PR * W_GROUP)
    blk, _ = _mixer_geometry(t, past is not None)
    y, ret_new, c_new, n_new, m_new, h_new, conv_new = _mixer(
        pa, pr, kat, kbt, kct, gt, past, _rel_bias_rows(lw['rel_bias'], blk), tables, rec, lw)
    x3, post_cast_out = _post(x1, y.reshape(n, 4 * W_GROUP), lw['w_out'], lw['x_wq'], lw['x_wo'], lw['layer'],
                              g[1:2], b[1:2], g[2:3], b[2:3], *mem, alpha, t, post_casts)
    heads = lambda a: a.reshape(bsz, -1, N_HEADS, HEAD_DIM)
    keep = min(A_REACH, t)
    a_k = heads(pa[:, t - keep:, W_GROUP:2 * W_GROUP])
    a_v = heads(pa[:, t - keep:, 2 * W_GROUP:])
    state = (a_k, a_v, ret_new, c_new, n_new, m_new[:, :, 0], h_new[:, 0, :], conv_new)
    return x3, state, post_cast_out


def kernel(x_prompt, x_sample, mem_prompt, cache_a_k, cache_a_v, state_ret, state_mlstm_c, state_mlstm_n,
           state_mlstm_m, state_lru_h, state_conv, cache_mem_k, cache_mem_v, ln_g, ln_b,
           ffn1_gate, ffn1_up, ffn1_down, ffn2_gate, ffn2_up, ffn2_down, w_in, b_in, a_rel_bias,
           conv_w, conv_b, lru_wa, lru_ba, lru_wi, lru_bi, lru_lambda, w_out, x_wq, x_wk, x_wv, x_wo):
    depth = ln_g.shape[0]
    alpha = (2.0 * depth) ** 0.25
    row = lambda v: v[None, :]
    grp = lambda a, j: a[..., j * W_GROUP:(j + 1) * W_GROUP]
    perm = _swap_perm()
    QA, KA, VA, QB, KB, VB, GB, QC, KC, VC, OC, XD, YD = range(13)

    def layer_weights(l):
        w, bias = w_in[l], b_in[l]
        nat = lambda a: jnp.concatenate(
            [grp(a, QB), grp(a, QB)[..., perm], grp(a, VB), grp(a, GB), grp(a, QC), grp(a, VC), grp(a, OC),
             grp(a, XD), grp(a, YD)], axis=-1)
        trn = lambda a: jnp.concatenate([grp(a, KA), grp(a, KB), grp(a, KC), a[..., 13 * W_GROUP:]], axis=-1)
        return {'ln_g': ln_g[l], 'ln_b': ln_b[l],
                'layer': l, **stacks,
                'w_in_a': w[:, :3 * W_GROUP].astype(BF16), 'w_in_r': nat(w).astype(BF16),
                'w_in_t': trn(w).T.astype(BF16),
                'b_in_a': row(bias[:3 * W_GROUP]), 'b_in_r': row(nat(bias)), 'b_in_t': trn(bias)[:, None],
                'rel_bias': a_rel_bias[l],
                'conv_w': conv_w[l], 'conv_b': row(conv_b[l]),
                'lru_wa': _block_diag(lru_wa[l]).astype(BF16), 'lru_ba': row(lru_ba[l]),
                'lru_wi': _block_diag(lru_wi[l]).astype(BF16), 'lru_bi': row(lru_bi[l]),
                'lru_lambda': row(lru_lambda[l]),
                'x_wkv': jnp.concatenate([x_wk[l], x_wv[l]], axis=1).astype(BF16)}

    stacks = {name: w.astype(BF16) for name, w in (('w_out', w_out), ('x_wq', x_wq), ('x_wo', x_wo))}
    ffn1_f32, ffn2_f32 = (ffn1_gate, ffn1_up, ffn1_down), (ffn2_gate, ffn2_up, ffn2_down)
    ffn1_w = {0: tuple(w[0].astype(BF16) for w in ffn1_f32)}
    weights = [layer_weights(l) for l in range(depth)]
    dx = x_wk.shape[2]

    bp, tp, d = x_prompt.shape
    bs, ts, _ = x_sample.shape
    n_mem = mem_prompt.shape[1]
    tables_p = _mixer_tables(jnp.arange(tp), *_mixer_geometry(tp, False))
    rec0 = (jnp.zeros((1, bp, N_HEADS, HEAD_DIM, HEAD_DIM), F32), jnp.zeros((1, bp, N_HEADS, HEAD_DIM, HEAD_DIM), F32),
            jnp.zeros((1, bp, N_HEADS, HEAD_DIM), F32), jnp.zeros((1, bp, N_HEADS), F32),
            jnp.zeros((1, bp, W_GROUP), F32), jnp.zeros((1, bp, CONV_W - 1, W_GROUP), F32), 0)
    tables_s = _mixer_tables(PAST_LEN + jnp.arange(ts), *_mixer_geometry(ts, True))
    past_k_all = cache_a_k.reshape(depth, bs, A_REACH, W_GROUP)
    past_v_all = cache_a_v.reshape(depth, bs, A_REACH, W_GROUP)
    mem_k_all = cache_mem_k.reshape(depth, bs, n_mem, dx)
    mem_v_all = cache_mem_v.reshape(depth, bs, n_mem, dx)

    hp, hs = x_prompt.reshape(bp * tp, d), x_sample.reshape(bs * ts, d)
    prompt_states, sample_states = [], []
    for l in range(depth):
        lw = weights[l]
        g, b = lw['ln_g'], lw['ln_b']
        mkv = _mem_kv(mem_prompt.reshape(bp * n_mem, d), lw['x_wkv'])
        mk = mkv[:, :dx].reshape(bp, n_mem, dx)
        mv = mkv[:, dx:].reshape(bp, n_mem, dx)
        x1p, x1s = _ffn_ln(hp, hs, *ffn1_w[l], g[0:1], b[0:1], alpha)
        w_in = (lw['w_in_a'], lw['w_in_r'], lw['w_in_t'], lw['b_in_a'], lw['b_in_r'], lw['b_in_t'])
        proj_p, ffn2 = _in_proj(x1p, *w_in, [(w, l) for w in ffn2_f32])
        proj_s, _ = _in_proj(x1s, *w_in)
        post_casts = [(w, l + 1) for w in ffn1_f32] if l + 1 < depth else []
        x3p, st_p, cast_next = _mix_and_post(x1p, proj_p, (bp, tp, d), tables_p, (mk[None], mv[None], 0), None, rec0,
                                             lw, alpha, post_casts)
        rec = (state_ret, state_mlstm_c, state_mlstm_n, state_mlstm_m, state_lru_h, state_conv, l)
        x3s, st_s, _ = _mix_and_post(x1s, proj_s, (bs, ts, d), tables_s, (mem_k_all, mem_v_all, l),
                                     (past_k_all, past_v_all, l), rec, lw, alpha)
        if post_casts:
            ffn1_w[l + 1] = cast_next
        hp, hs = _ffn_ln(x3p, x3s, *ffn2, g[3:4], b[3:4], alpha)
        prompt_states.append(st_p + (mk.reshape(bp, n_mem, -1, HEAD_DIM), mv.reshape(bp, n_mem, -1, HEAD_DIM)))
        sample_states.append(st_s)
    y_prompt, y_sample = hp.reshape(bp, tp, d), hs.reshape(bs, ts, d)
    prompt_out = [jnp.stack(f) for f in zip(*prompt_states)]
    sample_out = [jnp.stack(f) for f in zip(*sample_states)]

    return (y_prompt, y_sample, *prompt_out, *sample_out)
```

```python
import functools

import numpy as np
import jax
import jax.numpy as jnp
from jax import lax
from jax.experimental import pallas as pl
from jax.experimental.pallas import tpu as pltpu

F32 = jnp.float32
BF16 = jnp.bfloat16

CHUNK = 64
HEAD_DIM = 64
N_HEADS = 4
W_GROUP = N_HEADS * HEAD_DIM
SLAB = 2 * HEAD_DIM
N_PAIRS = N_HEADS // 2
A_BAND_CHUNKS = 8
A_REACH = A_BAND_CHUNKS * CHUNK
REL_CLIP = 128
CONV_W = 4
LRU_C = 8.0
LN_EPS = 1e-5
LOG2E = 1.4426950408889634
N_GATES = 2 * N_HEADS
PAST_LEN = 4096
ROLL_W = 1024
CONV_PAD = 8

V7X_VMEM_LIMIT_BYTES = 56 * 1024 * 1024
TOKEN_TILE = 512
MIX_BLOCK = 256
MIX_SEQS = 4
POST_SUBTILES = 2

PR_QB, PR_QBS, PR_VB, PR_GB, PR_QC, PR_VC, PR_OC, PR_XD, PR_YD = range(9)
N_PR = 9


def _cparams(*sem):
    return pltpu.CompilerParams(dimension_semantics=sem, vmem_limit_bytes=V7X_VMEM_LIMIT_BYTES)


def _dot(a, b):
    return jnp.dot(a.astype(BF16), b.astype(BF16), preferred_element_type=F32)


def _dot_nt(a, b):
    return lax.dot_general(a.astype(BF16), b.astype(BF16), (((1,), (1,)), ((), ())), preferred_element_type=F32)


def _layer_norm(x, g, b):
    mu = jnp.mean(x, -1, keepdims=True)
    xc = x - mu
    var = jnp.mean(xc * xc, -1, keepdims=True)
    return xc * lax.rsqrt(var + LN_EPS) * g + b


def _silu(x):
    return x * jax.nn.sigmoid(x)


def _lockstep(chains):
    while chains:
        chains = [c for c in chains if next(c, "done") != "done"]
        yield


def _run_interleaved(chains, lag=0):
    pending, live, rnd = list(chains), [], 0
    while pending or live:
        while pending and rnd >= lag * (len(chains) - len(pending)):
            live.append(pending.pop(0))
        live = [c for c in live if next(c, "done") != "done"]
        rnd += 1


def _resident(shape, layer=None):
    nd = len(shape)
    if layer is None:
        return pl.BlockSpec(shape, lambda *_: (0,) * nd, pipeline_mode=pl.Buffered(1))
    return pl.BlockSpec((None,) + tuple(shape), lambda *_: (layer,) + (0,) * nd, pipeline_mode=pl.Buffered(1))


def _cast_plan(stack, layer, n_steps):
    _, r, c = stack.shape
    nb = next(k for k in range(n_steps, 0, -1) if n_steps % k == 0 and r % k == 0 and (r // k) % 16 == 0)
    every = n_steps // nb
    return (pl.BlockSpec((None, r // nb, c), lambda i: (layer, i // every, 0)),
            pl.BlockSpec((r // nb, c), lambda i: (i // every, 0)),
            jax.ShapeDtypeStruct((r, c), BF16))


def _with_casts(body, n_in, n_out, n_cast):
    def wrapped(*refs):
        ins, rest = refs[:n_in + n_cast], refs[n_in + n_cast:]
        outs, scratch = rest[:n_out + n_cast], rest[n_out + n_cast:]
        body(*ins[:n_in], *outs[:n_out], *scratch)
        for src, dst in zip(ins[n_in:], outs[n_out:]):
            dst[...] = src[...].astype(BF16)
    return wrapped


def _ffn_ln_kernel(x_ref, wg_ref, wu_ref, wd_ref, g_ref, b_ref, o_ref, *, alpha):
    x = x_ref[...]
    xb = x.astype(BF16)
    gate = jnp.dot(xb, wg_ref[...], preferred_element_type=F32)
    up = jnp.dot(xb, wu_ref[...], preferred_element_type=F32)
    h = (_silu(gate) * up).astype(BF16)
    y = jnp.dot(h, wd_ref[...], preferred_element_type=F32)
    o_ref[...] = _layer_norm(alpha * x + 0.5 * y, g_ref[...], b_ref[...])


def _ffn_ln(x, wg, wu, wd, g, b, alpha):
    n, d = x.shape
    dff = wg.shape[1]
    tm = TOKEN_TILE
    return pl.pallas_call(
        functools.partial(_ffn_ln_kernel, alpha=alpha),
        out_shape=jax.ShapeDtypeStruct((n, d), F32),
        grid=(n // tm,),
        in_specs=[pl.BlockSpec((tm, d), lambda i: (i, 0)),
                  _resident((d, dff)), _resident((d, dff)), _resident((dff, d)),
                  _resident((1, d)), _resident((1, d))],
        out_specs=pl.BlockSpec((tm, d), lambda i: (i, 0)),
        compiler_params=_cparams("parallel"),
        name="ffn_ln",
    )(x, wg, wu, wd, g, b)


def _in_proj_kernel(x_ref, wa_ref, wr_ref, wt_ref, ba_ref, br_ref, bt_ref,
                    pa_ref, pr_ref, kat_ref, kbt_ref, kct_ref, gt_ref):
    xb = x_ref[...].astype(BF16)
    pa_ref[...] = jnp.dot(xb, wa_ref[...], preferred_element_type=F32) + ba_ref[...]
    pr_ref[...] = jnp.dot(xb, wr_ref[...], preferred_element_type=F32) + br_ref[...]
    tr = _dot_nt(wt_ref[...], xb) + bt_ref[...]
    kat_ref[...] = tr[0:W_GROUP].astype(BF16)
    kbt_ref[...] = tr[W_GROUP:2 * W_GROUP]
    kct_ref[...] = tr[2 * W_GROUP:3 * W_GROUP]
    gt_ref[...] = tr[3 * W_GROUP:3 * W_GROUP + N_GATES]


def _in_proj(x, wa, wr, wt, ba, br, bt, casts=()):
    n, d = x.shape
    ca, cr, ct = wa.shape[1], wr.shape[1], wt.shape[0]
    tm = TOKEN_TILE
    tok = lambda w: pl.BlockSpec((tm, w), lambda i: (i, 0))
    trn = lambda r: pl.BlockSpec((r, tm), lambda i: (0, i))
    plans = [_cast_plan(stack, layer, n // tm) for stack, layer in casts]
    out = pl.pallas_call(
        _with_casts(_in_proj_kernel, 7, 6, len(casts)),
        out_shape=(jax.ShapeDtypeStruct((n, ca), F32), jax.ShapeDtypeStruct((n, cr), F32),
                   jax.ShapeDtypeStruct((W_GROUP, n), BF16), jax.ShapeDtypeStruct((W_GROUP, n), F32),
                   jax.ShapeDtypeStruct((W_GROUP, n), F32), jax.ShapeDtypeStruct((N_GATES, n), F32),
                   *[p[2] for p in plans]),
        grid=(n // tm,),
        in_specs=[tok(d), _resident((d, ca)), _resident((d, cr)), _resident((ct, d)),
                  _resident((1, ca)), _resident((1, cr)), _resident((ct, 1)), *[p[0] for p in plans]],
        out_specs=(tok(ca), tok(cr), trn(W_GROUP), trn(W_GROUP), trn(W_GROUP), trn(N_GATES), *[p[1] for p in plans]),
        compiler_params=_cparams("parallel"),
        name="in_proj",
    )(x, wa, wr, wt, ba, br, bt, *[stack for stack, _ in casts])
    return out[:6], out[6:]


def _head_rows(xt, h, pos):
    xh = xt[h * HEAD_DIM:(h + 1) * HEAD_DIM]
    z = jnp.zeros_like(xh)
    return jnp.concatenate([xh, z] if pos == 0 else [z, xh], axis=0)


def _col_bcast(eye_b, rows):
    hi = rows.astype(BF16)
    lo = (rows - hi.astype(F32)).astype(BF16)
    dn = (((1,), (1,)), ((), ()))
    return (lax.dot_general(eye_b, hi, dn, preferred_element_type=F32)
            + lax.dot_general(eye_b, lo, dn, preferred_element_type=F32))


def _rep_heads(rows4, n):
    return jnp.concatenate([jnp.broadcast_to(rows4[h:h + 1], (n, rows4.shape[1])) for h in range(N_HEADS)], axis=0)


def _seg_scan(x, seg_pos, seg_len, op, fill):
    s = 1
    while s < seg_len:
        x = op(x, jnp.where(seg_pos >= s, pltpu.roll(x, s, 1), fill))
        s *= 2
    return x


def _shift_rows(x, s, fill):
    if s % 8 == 0:
        return jnp.concatenate([jnp.full((s, x.shape[1]), fill, x.dtype), x[:x.shape[0] - s]], axis=0)
    row = lax.broadcasted_iota(jnp.int32, x.shape, 0)
    return jnp.where(row >= s, pltpu.roll(x, s, 0), fill)


def _mixer_kernel(pa_ref, pr_ref, kat_ref, kbt_ref, kct_ref, gt_ref, pastk_ref, pastv_ref, rel_ref,
                  cos_ref, sin_ref, cost_ref, sint_ref, dec_ref, qdec_ref, kdect_ref, sdec_ref,
                  ret0_ref, c0_ref, n0_ref, m0_ref,
                  cw_ref, cb_ref, wa_ref, ba_ref, wi_ref, bi_ref, lam_ref, buf0_ref, h0_ref,
                  y_ref, ret_out, c_out, n_out, m_out, h_out, buf_out,
                  bias_sc, spair, ppair, m_sc, win, h_sc, kpadt=None, vpad=None,
                  *, blk, seqs, n_steps, has_past):
    L = blk
    band = A_REACH + L
    n_past = A_REACH // L if not has_past else 0
    n_var = n_past + 1
    bb = pl.program_id(0)
    tb = pl.program_id(1)
    tail = CONV_W - 1
    if n_steps == 1:
        at_first = at_last = lambda f: f()
    else:
        at_first = pl.when(tb == 0)
        at_last = pl.when(tb == n_steps - 1)

    li = lax.broadcasted_iota(jnp.int32, (L, L), 0)
    lj = lax.broadcasted_iota(jnp.int32, (L, L), 1)
    causal_neg = jnp.where(lj <= li, 0.0, -jnp.inf)
    eye_b = jnp.where(li == lj, 1.0, 0.0).astype(BF16)
    lane_lo = lax.broadcasted_iota(jnp.int32, (L, SLAB), 1) < HEAD_DIM
    ones_slab = jnp.ones((L, SLAB), BF16)
    ri = lax.broadcasted_iota(jnp.int32, (SLAB, SLAB), 0) // HEAD_DIM
    rj = lax.broadcasted_iota(jnp.int32, (SLAB, SLAB), 1) // HEAD_DIM
    smask = jnp.where(ri == rj, 1.0, 0.0)
    pmask = jnp.concatenate([smask, smask], axis=1)
    gi = lax.broadcasted_iota(jnp.int32, (W_GROUP, W_GROUP), 0) // HEAD_DIM
    gj = lax.broadcasted_iota(jnp.int32, (W_GROUP, W_GROUP), 1) // HEAD_DIM
    gmat = jnp.where(gi == gj, 1.0 / HEAD_DIM, 0.0).astype(BF16)
    e64i = lax.broadcasted_iota(jnp.int32, (HEAD_DIM, HEAD_DIM), 0)
    e64j = lax.broadcasted_iota(jnp.int32, (HEAD_DIM, HEAD_DIM), 1)
    eye64 = e64i == e64j
    row_lo = lax.broadcasted_iota(jnp.int32, (SLAB, 1), 0) < HEAD_DIM

    @pl.when((bb == 0) & (tb == 0))
    def _():
        qq = lax.broadcasted_iota(jnp.int32, (L, band), 0)
        kk = lax.broadcasted_iota(jnp.int32, (L, band), 1)
        off = kk - (qq // CHUNK) * CHUNK
        for h in range(N_HEADS):
            row = jnp.broadcast_to(rel_ref[h], (L, ROLL_W))
            tile = pltpu.roll(row, ROLL_W - L, 1, stride=1, stride_axis=0)[:, :band] * LOG2E
            tile = jnp.where(off >= 0, jnp.where(off < A_REACH + CHUNK, tile, -jnp.inf), -jnp.inf)
            for v in range(n_var):
                first_col = (n_past - v) * L
                bias_sc[v * N_HEADS + h] = jnp.where(kk >= first_col, tile, -jnp.inf) if first_col > 0 else tile

    for s in range(seqs):
        @at_first
        def _():
            for p in range(N_PAIRS):
                sblk, cblk = [], []
                for hh in range(2):
                    h = 2 * p + hh
                    z = jnp.zeros((HEAD_DIM, HEAD_DIM), F32)
                    s_h = ret0_ref[s, h]
                    c_h = c0_ref[s, h]
                    n_row = n0_ref[s, h:h + 1, :]
                    n_col = jnp.sum(jnp.where(eye64, jnp.broadcast_to(n_row, eye64.shape), 0.0),
                                    axis=1, keepdims=True)
                    n_rep = jnp.broadcast_to(n_col, (HEAD_DIM, HEAD_DIM))
                    sblk.append(jnp.concatenate([s_h, z] if hh == 0 else [z, s_h], axis=1))
                    cblk.append(jnp.concatenate([c_h, z, n_rep, z] if hh == 0 else [z, c_h, z, n_rep], axis=1))
                spair[s, p] = jnp.concatenate(sblk, axis=0)
                ppair[s, p] = jnp.concatenate(cblk, axis=0)
            m_sc[s, 0:N_HEADS, :] = jnp.broadcast_to(m0_ref[s], (N_HEADS, m_sc.shape[2]))
            win[s, CONV_PAD - tail:CONV_PAD, :] = buf0_ref[s]
            h_sc[s] = h0_ref[s]
            if not has_past:
                vpad[s, 0:A_REACH, :] = jnp.zeros((A_REACH, W_GROUP), BF16)
                kpadt[s, 0:n_past * W_GROUP, :] = jnp.zeros((n_past * W_GROUP, L), BF16)

    g8 = gt_ref[...]
    seg_pos = lax.broadcasted_iota(jnp.int32, g8.shape, 1) % L
    b8 = _seg_scan(jax.nn.log_sigmoid(g8), seg_pos, L, jnp.add, 0.0)
    a8 = g8 - pltpu.roll(b8, N_HEADS, 0)
    cm8 = _seg_scan(a8, seg_pos, L, jnp.maximum, -jnp.inf)
    b8 = pltpu.roll(b8, N_HEADS, 0)

    heads = [(h // 2, h % 2, h) for h in range(N_HEADS)]
    slab = lambda p: slice(p * SLAB, (p + 1) * SLAB)
    seg_of = lambda s: slice(s * L, (s + 1) * L)
    col = lambda s, j: pr_ref[s, :, j * W_GROUP:(j + 1) * W_GROUP]

    def halves(per_head):
        return jnp.concatenate([jnp.where(lane_lo, per_head[2 * p], per_head[2 * p + 1]) for p in range(N_PAIRS)],
                               axis=1)


    def attention_chain(s):
        pa = pa_ref[s]
        kat = kat_ref[:, seg_of(s)]
        v_new = pa[:, 2 * W_GROUP:].astype(BF16)
        if has_past:
            kpast = pastk_ref[s].astype(BF16)
            vband = jnp.concatenate([pastv_ref[s].astype(BF16), v_new], axis=0)
            bias_at = lambda h: bias_sc[h]
        else:
            kpadt[s, pl.ds(pl.multiple_of((tb + n_past) * W_GROUP, W_GROUP), W_GROUP), :] = kat
            kwin = kpadt[s, pl.ds(pl.multiple_of(tb * W_GROUP, W_GROUP), (n_past + 1) * W_GROUP), :]
            start = pl.multiple_of(tb * L, L)
            vpad[s, pl.ds(A_REACH + start, L), :] = v_new
            vband = vpad[s, pl.ds(start, band), :]
            var = jnp.minimum(tb, n_var - 1) * N_HEADS
            bias_at = lambda h: bias_sc[var + h]
        qa = (pa[:, 0:W_GROUP] * (HEAD_DIM ** -0.5 * LOG2E)).astype(BF16)
        sc = {}
        for p, hh, h in heads:
            q_slab = qa[:, slab(p)]
            if has_past:
                k_slab = kpast[:, slab(p)]
                lane = lax.broadcasted_iota(jnp.int32, k_slab.shape, 1)
                lane_h = lane < HEAD_DIM if hh == 0 else lane >= HEAD_DIM
                parts = [_dot_nt(q_slab, jnp.where(lane_h, k_slab, jnp.zeros_like(k_slab))),
                         jnp.dot(q_slab, _head_rows(kat, h, hh), preferred_element_type=F32)]
            else:
                parts = [jnp.dot(q_slab, _head_rows(kwin[i * W_GROUP:(i + 1) * W_GROUP], h, hh),
                                 preferred_element_type=F32) for i in range(n_past + 1)]
            sc[h] = jnp.concatenate(parts, axis=1)
        yield
        pe, den = {}, {}
        for _, _, h in heads:
            sh = sc[h] + bias_at(h)
            pe[h] = jnp.exp2(sh - jnp.max(sh, axis=1, keepdims=True))
            den[h] = jnp.sum(pe[h], axis=1, keepdims=True)
        yield
        res = {h: _dot(pe[h], vband)[:, slab(p)] for p, _, h in heads}
        yield
        y_ref[s, :, 0:W_GROUP] = halves({h: res[h] / den[h] for _, _, h in heads})

    def retention_chain(s):
        qr = (col(s, PR_QB) * cos_ref[...] + col(s, PR_QBS) * sin_ref[...]).astype(BF16)
        kbt = kbt_ref[:, seg_of(s)]
        half = HEAD_DIM // 2
        kbt_sw = jnp.concatenate([kbt[h * HEAD_DIM + o:h * HEAD_DIM + o + half]
                                  for h in range(N_HEADS) for o in (half, 0)], axis=0)
        krt = (kbt * cost_ref[:, seg_of(s)] + kbt_sw * sint_ref[:, seg_of(s)]) * HEAD_DIM ** -0.5
        krt_b = krt.astype(BF16)
        kdt_b = (krt * kdect_ref[...]).astype(BF16)
        vb = col(s, PR_VB).astype(BF16)
        raw = {h: jnp.dot(qr[:, slab(p)], _head_rows(krt_b, h, hh), preferred_element_type=F32)
               for p, hh, h in heads}
        s0 = {p: spair[s, p] for p in range(N_PAIRS)}
        inter = {p: _dot(qr[:, slab(p)], s0[p]) for p in range(N_PAIRS)}
        add = {p: _dot(kdt_b[slab(p)], vb[:, slab(p)]) for p in range(N_PAIRS)}
        yield
        scores = {h: raw[h] * dec_ref[h] for _, _, h in heads}
        for p in range(N_PAIRS):
            spair[s, p] = s0[p] * sdec_ref[p] + add[p] * smask
        yield
        o = {h: _dot(scores[h], vb[:, slab(p)]) for p, _, h in heads}
        yield
        ob = halves(o) + jnp.concatenate([inter[p] for p in range(N_PAIRS)], axis=1) * qdec_ref[...]
        mu = _dot(ob, gmat)
        yield
        oc = ob - mu
        var = _dot(oc * oc, gmat)
        yield
        y_ref[s, :, W_GROUP:2 * W_GROUP] = oc * lax.rsqrt(var + LN_EPS) * _silu(col(s, PR_GB))

    def mlstm_chain(s):
        a4 = a8[0:N_HEADS, seg_of(s)]
        b4 = b8[0:N_HEADS, seg_of(s)]
        m0c = m_sc[s, 0:N_HEADS, 0:1]
        big_m = jnp.maximum(m0c, cm8[0:N_HEADS, seg_of(s)])
        m4 = b4 + big_m
        w0_4 = jnp.exp(m0c - big_m)
        m_last = big_m[:, L - 1:L]
        wrow4 = jnp.exp(a4 - m_last)
        kct = kct_ref[:, seg_of(s)] * HEAD_DIM ** -0.5
        kct_b = kct.astype(BF16)
        kw_b = (kct * _rep_heads(wrow4, HEAD_DIM)).astype(BF16)
        qc = col(s, PR_QC).astype(BF16)
        vc = col(s, PR_VC).astype(BF16)
        v_aug = {p: jnp.concatenate([vc[:, slab(p)], ones_slab], axis=1) for p in range(N_PAIRS)}
        p0 = {p: ppair[s, p] for p in range(N_PAIRS)}
        w0_bc = _col_bcast(eye_b, _rep_heads(w0_4, HEAD_DIM))
        floor_bc = _col_bcast(eye_b, _rep_heads(jnp.exp(-m4), HEAD_DIM))
        m_bc = {h: _col_bcast(eye_b, jnp.broadcast_to(big_m[h:h + 1], (L, L))) for _, _, h in heads}
        raw = {h: jnp.dot(qc[:, slab(p)], _head_rows(kct_b, h, hh), preferred_element_type=F32)
               for p, hh, h in heads}
        inter = {p: _dot(qc[:, slab(p)], p0[p]) for p in range(N_PAIRS)}
        add = {p: _dot(kw_b[slab(p)], v_aug[p]) for p in range(N_PAIRS)}
        yield
        qk = {h: raw[h] * jnp.exp(a4[h:h + 1] - m_bc[h] + causal_neg) for _, _, h in heads}
        for p in range(N_PAIRS):
            w_state = jnp.where(row_lo, w0_4[2 * p:2 * p + 1, L - 1:L], w0_4[2 * p + 1:2 * p + 2, L - 1:L])
            ppair[s, p] = p0[p] * w_state + add[p] * pmask
        m_sc[s, 0:N_HEADS, :] = jnp.broadcast_to(m4[:, L - 1:L], (N_HEADS, m_sc.shape[2]))
        yield
        res = {h: _dot(qk[h], v_aug[p]) for p, _, h in heads}
        yield
        hc = []
        for p in range(N_PAIRS):
            w0 = w0_bc[:, slab(p)]
            num = jnp.where(lane_lo, res[2 * p][:, 0:SLAB], res[2 * p + 1][:, 0:SLAB]) + inter[p][:, 0:SLAB] * w0
            den = jnp.where(lane_lo, res[2 * p][:, SLAB:], res[2 * p + 1][:, SLAB:]) + inter[p][:, SLAB:] * w0
            hc.append(num / jnp.maximum(jnp.abs(den), floor_bc[:, slab(p)]))
        hc = jnp.concatenate(hc, axis=1)
        mu = _dot(hc, gmat)
        yield
        hcc = hc - mu
        var = _dot(hcc * hcc, gmat)
        yield
        y_ref[s, :, 2 * W_GROUP:3 * W_GROUP] = hcc * lax.rsqrt(var + LN_EPS) * jax.nn.sigmoid(col(s, PR_OC))

    def rglru_chain(s):
        xd = col(s, PR_XD)
        win[s, CONV_PAD:CONV_PAD + L, :] = xd
        xc = cb_ref[...]
        for j in range(CONV_W):
            off = CONV_PAD - tail + j
            xc = xc + win[s, off:off + L, :] * cw_ref[j:j + 1, :]
        win[s, CONV_PAD - tail:CONV_PAD, :] = xd[L - tail:, :]
        r_pre = _dot(xc, wa_ref[...])
        i_pre = _dot(xc, wi_ref[...])
        yield
        r = jax.nn.sigmoid(r_pre + ba_ref[...])
        i = jax.nn.sigmoid(i_pre + bi_ref[...])
        neg_lam = -lam_ref[...]
        softplus = jnp.maximum(neg_lam, 0.0) + jnp.log1p(jnp.exp(-jnp.abs(neg_lam)))
        log_a = -LRU_C * r * softplus
        a = jnp.exp(log_a)
        u = jnp.sqrt(-jnp.tanh(log_a) * (jnp.exp(2.0 * log_a) + 1.0)) * (i * xc)
        st = 1
        while st < L:
            u = a * _shift_rows(u, st, 0.0) + u
            a = a * _shift_rows(a, st, 1.0)
            st *= 2
            if st in (8, 64):
                yield
        hseq = u + a * h_sc[s]
        h_sc[s] = hseq[L - 1:L, :]
        y_ref[s, :, 3 * W_GROUP:] = hseq * jax.nn.gelu(col(s, PR_YD))

    if seqs == 1:
        order, lag = (attention_chain, rglru_chain, mlstm_chain, retention_chain), 2
    else:
        order, lag = (attention_chain, retention_chain, mlstm_chain, rglru_chain), 0
    _run_interleaved([chain(s) for chain in order for s in range(seqs)], lag=lag)

    for s in range(seqs):
        @at_last
        def _():
            for p in range(N_PAIRS):
                sp = spair[s, p]
                pp = ppair[s, p]
                for hh in range(2):
                    h = 2 * p + hh
                    rs = slice(hh * HEAD_DIM, (hh + 1) * HEAD_DIM)
                    ret_out[s, h] = sp[rs, rs]
                    c_out[s, h] = pp[rs, rs]
                    n_rep = pp[rs, SLAB + hh * HEAD_DIM:SLAB + (hh + 1) * HEAD_DIM]
                    n_out[s, h:h + 1, :] = jnp.sum(jnp.where(eye64, n_rep, 0.0), axis=0, keepdims=True)
            m_out[s] = m_sc[s, 0:N_HEADS, 0:1]
            h_out[s] = h_sc[s]
            buf_out[s] = pr_ref[s, L - tail:, PR_XD * W_GROUP:(PR_XD + 1) * W_GROUP]


def _mixer_kernel_fresh(pa_ref, pr_ref, kat_ref, kbt_ref, kct_ref, gt_ref, *rest, **static):
    _mixer_kernel(pa_ref, pr_ref, kat_ref, kbt_ref, kct_ref, gt_ref, None, None, *rest, **static)


def _mixer_geometry(t, has_past):
    if has_past:
        assert t == CHUNK, "sequences with a carried cache are expected to be a single chunk"
        return t, MIX_SEQS
    assert t % MIX_BLOCK == 0 and A_REACH % MIX_BLOCK == 0
    return MIX_BLOCK, 1


def _mixer_tables(pos, blk, seqs):
    cos, sin = _rotary_tables(pos)
    return (cos, sin, jnp.tile(cos.T, (1, seqs)), jnp.tile(sin.T, (1, seqs))) + _retention_tables(blk)


def _mixer(pa, pr, kat, kbt, kct, gt, past, rel_rows, tables, rec, lw):
    ret0, c0, n0, m0, h0, buf0, rec_layer = rec
    m0, h0 = m0[..., None], h0[:, :, None, :]
    bsz, t, _ = pa.shape
    has_past = past is not None
    blk, seqs = _mixer_geometry(t, has_past)
    nb = t // blk
    assert bsz % seqs == 0
    band = A_REACH + blk
    n_var = 1 if has_past else A_REACH // blk + 1
    tail = CONV_W - 1

    tok3 = lambda w: pl.BlockSpec((seqs, blk, w), lambda b, c: (b, c, 0))
    trn = lambda r: pl.BlockSpec((r, seqs * blk), lambda b, c: (0, b * nb + c))
    per_seq = lambda *shape: pl.BlockSpec((seqs,) + shape, lambda b, c: (b,) + (0,) * len(shape))
    const = lambda *shape: pl.BlockSpec(shape, lambda b, c: (0,) * len(shape))
    carried = lambda *shape: pl.BlockSpec((None, seqs) + shape, lambda b, c: (rec_layer, b) + (0,) * len(shape))
    past_specs, past_args = [], ()
    if has_past:
        layer = past[2]
        cache = pl.BlockSpec((None, seqs, A_REACH, W_GROUP), lambda b, c: (layer, b, 0, 0))
        past_specs, past_args = [cache, cache], past[:2]
    in_specs = [tok3(3 * W_GROUP), tok3(N_PR * W_GROUP), trn(W_GROUP), trn(W_GROUP), trn(W_GROUP), trn(N_GATES),
                *past_specs, const(N_HEADS, 1, ROLL_W),
                pl.BlockSpec((blk, W_GROUP), lambda b, c: (c, 0)), pl.BlockSpec((blk, W_GROUP), lambda b, c: (c, 0)),
                pl.BlockSpec((W_GROUP, seqs * blk), lambda b, c: (0, c)),
                pl.BlockSpec((W_GROUP, seqs * blk), lambda b, c: (0, c)),
                const(N_HEADS, blk, blk), const(blk, W_GROUP), const(W_GROUP, blk), const(N_PAIRS, SLAB, SLAB),
                carried(N_HEADS, HEAD_DIM, HEAD_DIM), carried(N_HEADS, HEAD_DIM, HEAD_DIM),
                carried(N_HEADS, HEAD_DIM), carried(N_HEADS, 1),
                const(CONV_W, W_GROUP), const(1, W_GROUP), const(W_GROUP, W_GROUP), const(1, W_GROUP),
                const(W_GROUP, W_GROUP), const(1, W_GROUP), const(1, W_GROUP),
                carried(tail, W_GROUP), carried(1, W_GROUP)]
    out_shape = (jax.ShapeDtypeStruct((bsz, t, 4 * W_GROUP), F32),
                 jax.ShapeDtypeStruct((bsz, N_HEADS, HEAD_DIM, HEAD_DIM), F32),
                 jax.ShapeDtypeStruct((bsz, N_HEADS, HEAD_DIM, HEAD_DIM), F32),
                 jax.ShapeDtypeStruct((bsz, N_HEADS, HEAD_DIM), F32),
                 jax.ShapeDtypeStruct((bsz, N_HEADS, 1), F32),
                 jax.ShapeDtypeStruct((bsz, 1, W_GROUP), F32),
                 jax.ShapeDtypeStruct((bsz, tail, W_GROUP), F32))
    out_specs = (tok3(4 * W_GROUP), per_seq(N_HEADS, HEAD_DIM, HEAD_DIM), per_seq(N_HEADS, HEAD_DIM, HEAD_DIM),
                 per_seq(N_HEADS, HEAD_DIM), per_seq(N_HEADS, 1), per_seq(1, W_GROUP), per_seq(tail, W_GROUP))
    scratch = [pltpu.VMEM((n_var * N_HEADS, blk, band), F32),
               pltpu.VMEM((seqs, N_PAIRS, SLAB, SLAB), F32), pltpu.VMEM((seqs, N_PAIRS, SLAB, 2 * SLAB), F32),
               pltpu.VMEM((seqs, 8, 128), F32), pltpu.VMEM((seqs, CONV_PAD + blk, W_GROUP), F32),
               pltpu.VMEM((seqs, 1, W_GROUP), F32)]
    if not has_past:
        scratch += [pltpu.VMEM((seqs, (nb + A_REACH // blk) * W_GROUP, blk), BF16),
                    pltpu.VMEM((seqs, A_REACH + t, W_GROUP), BF16)]
    body = _mixer_kernel if has_past else _mixer_kernel_fresh
    return pl.pallas_call(
        functools.partial(body, blk=blk, seqs=seqs, n_steps=nb, has_past=has_past),
        out_shape=out_shape,
        grid=(bsz // seqs, nb),
        in_specs=in_specs,
        out_specs=out_specs,
        scratch_shapes=scratch,
        compiler_params=_cparams("arbitrary", "arbitrary"),
        name="mixer",
    )(pa, pr, kat, kbt, kct, gt, *past_args, rel_rows, *tables,
      ret0, c0, n0, m0, lw['conv_w'], lw['conv_b'], lw['lru_wa'], lw['lru_ba'], lw['lru_wi'], lw['lru_bi'],
      lw['lru_lambda'], buf0, h0)


def _post_kernel(x_ref, y_ref, wo_ref, g1_ref, b1_ref, wq_ref, mk_ref, mv_ref, xo_ref,
                 g2_ref, b2_ref, o_ref, att_sc, *, alpha, seqs, subtiles):
    rows_sub = x_ref.shape[0] // subtiles
    rows = rows_sub // seqs
    head = lambda h: slice(h * HEAD_DIM, (h + 1) * HEAD_DIM)

    def subtile_chain(i):
        rs = slice(i * rows_sub, (i + 1) * rows_sub)
        x = x_ref[rs, :]
        mix = _dot(y_ref[rs, :], wo_ref[...])
        yield
        x2 = _layer_norm(alpha * x + mix, g1_ref[...], b1_ref[...])
        yield
        q = _dot(x2, wq_ref[...]).astype(BF16)
        yield

        def attend(s, h):
            qs = slice(s * rows, (s + 1) * rows)
            sc = _dot_nt(q[qs, head(h)], mk_ref[s, :, head(h)]) * HEAD_DIM ** -0.5
            yield
            p = jnp.exp(sc - jnp.max(sc, axis=1, keepdims=True))
            den = jnp.sum(p, axis=1, keepdims=True)
            yield
            o = _dot(p, mv_ref[s, :, head(h)])
            yield
            att_sc[i * rows_sub + s * rows:i * rows_sub + (s + 1) * rows, head(h)] = o / den

        yield from _lockstep([attend(s, h) for s in range(seqs) for h in range(N_HEADS)])
        out = _dot(att_sc[rs, :], xo_ref[...])
        yield
        o_ref[rs, :] = _layer_norm(alpha * x2 + out, g2_ref[...], b2_ref[...])

    _run_interleaved([subtile_chain(i) for i in range(subtiles)], lag=2)


def _post(x, y, wo, wq, xo, wlayer, g1, b1, g2, b2, mk, mv, layer, alpha, seq_len, casts=()):
    n, d = x.shape
    n_mem, dx = mk.shape[2], mk.shape[3]
    if seq_len >= POST_SUBTILES * TOKEN_TILE:
        subtiles, seqs = POST_SUBTILES, 1
        tm = subtiles * TOKEN_TILE
        mem = pl.BlockSpec((None, 1, n_mem, dx), lambda i: (layer, i // (seq_len // tm), 0, 0))
    else:
        tm = TOKEN_TILE
        subtiles, seqs = 1, tm // seq_len
        mem = pl.BlockSpec((None, seqs, n_mem, dx), lambda i: (layer, i, 0, 0))
    assert n % tm == 0 and tm % (subtiles * seqs) == 0
    tok = lambda w: pl.BlockSpec((tm, w), lambda i: (i, 0))
    plans = [_cast_plan(stack, lyr, n // tm) for stack, lyr in casts]
    out = pl.pallas_call(
        _with_casts(functools.partial(_post_kernel, alpha=alpha, seqs=seqs, subtiles=subtiles), 11, 1, len(casts)),
        out_shape=(jax.ShapeDtypeStruct((n, d), F32), *[p[2] for p in plans]),
        grid=(n // tm,),
        in_specs=[tok(d), tok(y.shape[1]),
                  _resident(wo.shape[1:], wlayer), _resident((1, d)), _resident((1, d)),
                  _resident(wq.shape[1:], wlayer), mem, mem, _resident(xo.shape[1:], wlayer),
                  _resident((1, d)), _resident((1, d)), *[p[0] for p in plans]],
        out_specs=(tok(d), *[p[1] for p in plans]),
        scratch_shapes=[pltpu.VMEM((tm, dx), F32)],
        compiler_params=_cparams("parallel"),
        name="post",
    )(x, y, wo, g1, b1, wq, mk, mv, xo, g2, b2, *[stack for stack, _ in casts])
    return out[0], out[1:]


def _mem_kv_kernel(x_ref, w_ref, o_ref):
    o_ref[...] = _dot(x_ref[...], w_ref[...])


def _mem_kv(mem, w):
    n, d = mem.shape
    tm = min(TOKEN_TILE, n)
    return pl.pallas_call(
        _mem_kv_kernel,
        out_shape=jax.ShapeDtypeStruct((n, w.shape[1]), F32),
        grid=(n // tm,),
        in_specs=[pl.BlockSpec((tm, d), lambda i: (i, 0)), _resident(w.shape)],
        out_specs=pl.BlockSpec((tm, w.shape[1]), lambda i: (i, 0)),
        compiler_params=_cparams("parallel"),
        name="mem_kv",
    )(mem, w)


def _rotary_tables(pos):
    half = HEAD_DIM // 2
    inv = jnp.exp(-jnp.log(10000.0) * jnp.arange(half, dtype=F32) / half)
    ang = pos.astype(F32)[:, None] * inv[None, :]
    cos, sin = jnp.cos(ang), jnp.sin(ang)
    cos_full = jnp.tile(jnp.concatenate([cos, cos], -1), (1, N_HEADS))
    sin_full = jnp.tile(jnp.concatenate([-sin, sin], -1), (1, N_HEADS))
    return cos_full, sin_full


def _retention_tables(blk):
    log_g = jnp.log1p(-jnp.exp2(-5.0 - jnp.arange(N_HEADS, dtype=F32)))
    idx = jnp.arange(blk, dtype=F32)
    diff = idx[:, None] - idx[None, :]
    dec = jnp.exp(jnp.where((diff >= 0)[None], diff[None] * log_g[:, None, None], -jnp.inf))
    q_dec = jnp.exp((idx[:, None] + 1.0) * log_g[None, :])
    k_dec = jnp.exp((blk - 1.0 - idx)[:, None] * log_g[None, :])
    s_dec = jnp.exp(blk * log_g)
    lanes = lambda tbl: jnp.repeat(tbl, HEAD_DIM, axis=1)
    pair = lambda p: jnp.kron(jnp.diag(s_dec[2 * p:2 * p + 2]), jnp.ones((HEAD_DIM, HEAD_DIM), F32))
    return dec, lanes(q_dec), lanes(k_dec).T, jnp.stack([pair(p) for p in range(N_PAIRS)])


def _rel_bias_rows(rel_bias, blk):
    idx = np.clip(A_REACH + blk - np.arange(ROLL_W), -REL_CLIP, REL_CLIP) + REL_CLIP
    return rel_bias[:, idx][:, None, :]


def _block_diag(w):
    h, c, _ = w.shape
    eye = jnp.eye(h, dtype=w.dtype)
    return (eye[:, None, :, None] * w[:, :, None, :]).reshape(h * c, h * c)


def _swap_perm():
    j = np.arange(W_GROUP)
    return (j // HEAD_DIM) * HEAD_DIM + (j % HEAD_DIM + HEAD_DIM // 2) % HEAD_DIM


def _layer(x, tables, mem, past, rec, lw, ffn1, ffn2, alpha, proj_casts=(), post_casts=()):
    bsz, t, d = x.shape
    n = bsz * t
    g, b = lw['ln_g'], lw['ln_b']

    x1 = _ffn_ln(x.reshape(n, d), *ffn1, g[0:1], b[0:1], alpha)
    (pa, pr, kat, kbt, kct, gt), proj_cast_out = _in_proj(x1, lw['w_in_a'], lw['w_in_r'], lw['w_in_t'],
                                                          lw['b_in_a'], lw['b_in_r'], lw['b_in_t'], proj_casts)
    if ffn2 is None:
        ffn2 = proj_cast_out[:3]
    pa = pa.reshape(bsz, t, 3 * W_GROUP)
    pr = pr.reshape(bsz, t, N_PR * W_GROUP)
    blk, _ = _mixer_geometry(t, past is not None)
    y, ret_new, c_new, n_new, m_new, h_new, conv_new = _mixer(
        pa, pr, kat, kbt, kct, gt, past, _rel_bias_rows(lw['rel_bias'], blk), tables, rec, lw)

    x3, post_cast_out = _post(x1, y.reshape(n, 4 * W_GROUP), lw['w_out'], lw['x_wq'], lw['x_wo'], lw['layer'],
                              g[1:2], b[1:2], g[2:3], b[2:3], *mem, alpha, t, post_casts)
    x4 = _ffn_ln(x3, *ffn2, g[3:4], b[3:4], alpha)

    heads = lambda a: a.reshape(bsz, -1, N_HEADS, HEAD_DIM)
    keep = min(A_REACH, t)
    a_k = heads(pa[:, t - keep:, W_GROUP:2 * W_GROUP])
    a_v = heads(pa[:, t - keep:, 2 * W_GROUP:])
    state = (a_k, a_v, ret_new, c_new, n_new, m_new[:, :, 0], h_new[:, 0, :], conv_new)
    return x4.reshape(bsz, t, d), state, proj_cast_out, post_cast_out


def kernel(x_prompt, x_sample, mem_prompt, cache_a_k, cache_a_v, state_ret, state_mlstm_c, state_mlstm_n,
           state_mlstm_m, state_lru_h, state_conv, cache_mem_k, cache_mem_v, ln_g, ln_b,
           ffn1_gate, ffn1_up, ffn1_down, ffn2_gate, ffn2_up, ffn2_down, w_in, b_in, a_rel_bias,
           conv_w, conv_b, lru_wa, lru_ba, lru_wi, lru_bi, lru_lambda, w_out, x_wq, x_wk, x_wv, x_wo):
    depth = ln_g.shape[0]
    alpha = (2.0 * depth) ** 0.25
    row = lambda v: v[None, :]
    grp = lambda a, j: a[..., j * W_GROUP:(j + 1) * W_GROUP]
    perm = _swap_perm()
    QA, KA, VA, QB, KB, VB, GB, QC, KC, VC, OC, XD, YD = range(13)

    def layer_weights(l):
        w, bias = w_in[l], b_in[l]
        nat = lambda a: jnp.concatenate(
            [grp(a, QB), grp(a, QB)[..., perm], grp(a, VB), grp(a, GB), grp(a, QC), grp(a, VC), grp(a, OC),
             grp(a, XD), grp(a, YD)], axis=-1)
        trn = lambda a: jnp.concatenate([grp(a, KA), grp(a, KB), grp(a, KC), a[..., 13 * W_GROUP:]], axis=-1)
        return {'ln_g': ln_g[l], 'ln_b': ln_b[l],
                'layer': l, **stacks,
                'w_in_a': w[:, :3 * W_GROUP].astype(BF16), 'w_in_r': nat(w).astype(BF16),
                'w_in_t': trn(w).T.astype(BF16),
                'b_in_a': row(bias[:3 * W_GROUP]), 'b_in_r': row(nat(bias)), 'b_in_t': trn(bias)[:, None],
                'rel_bias': a_rel_bias[l],
                'conv_w': conv_w[l], 'conv_b': row(conv_b[l]),
                'lru_wa': _block_diag(lru_wa[l]).astype(BF16), 'lru_ba': row(lru_ba[l]),
                'lru_wi': _block_diag(lru_wi[l]).astype(BF16), 'lru_bi': row(lru_bi[l]),
                'lru_lambda': row(lru_lambda[l]),
                'x_wkv': jnp.concatenate([x_wk[l], x_wv[l]], axis=1).astype(BF16)}

    stacks = {name: w.astype(BF16) for name, w in (('w_out', w_out), ('x_wq', x_wq), ('x_wo', x_wo))}
    ffn1_f32, ffn2_f32 = (ffn1_gate, ffn1_up, ffn1_down), (ffn2_gate, ffn2_up, ffn2_down)
    ffn1_w = {0: tuple(w[0].astype(BF16) for w in ffn1_f32)}
    ffn2_w = {}
    weights = [layer_weights(l) for l in range(depth)]
    dx = x_wk.shape[2]

    bp, tp, d = x_prompt.shape
    n_mem = mem_prompt.shape[1]
    h = x_prompt
    prompt_states = []
    tables = _mixer_tables(jnp.arange(tp), *_mixer_geometry(tp, False))
    rec0 = (jnp.zeros((1, bp, N_HEADS, HEAD_DIM, HEAD_DIM), F32), jnp.zeros((1, bp, N_HEADS, HEAD_DIM, HEAD_DIM), F32),
            jnp.zeros((1, bp, N_HEADS, HEAD_DIM), F32), jnp.zeros((1, bp, N_HEADS), F32),
            jnp.zeros((1, bp, W_GROUP), F32), jnp.zeros((1, bp, CONV_W - 1, W_GROUP), F32), 0)
    for l in range(depth):
        lw = weights[l]
        mkv = _mem_kv(mem_prompt.reshape(bp * n_mem, d), lw['x_wkv'])
        mk = mkv[:, :dx].reshape(bp, n_mem, dx)
        mv = mkv[:, dx:].reshape(bp, n_mem, dx)
        post_casts = [(w, l + 1) for w in ffn1_f32] if l + 1 < depth else []
        h, st, ffn2_w[l], cast_next = _layer(h, tables, (mk[None], mv[None], 0), None, rec0, lw, ffn1_w[l], None,
                                             alpha, [(w, l) for w in ffn2_f32], post_casts)
        if post_casts:
            ffn1_w[l + 1] = cast_next
        prompt_states.append(st + (mk.reshape(bp, n_mem, -1, HEAD_DIM), mv.reshape(bp, n_mem, -1, HEAD_DIM)))
    y_prompt = h
    prompt_out = [jnp.stack(f) for f in zip(*prompt_states)]

    bs, ts, _ = x_sample.shape
    h = x_sample
    sample_states = []
    tables = _mixer_tables(PAST_LEN + jnp.arange(ts), *_mixer_geometry(ts, True))
    past_k_all = cache_a_k.reshape(depth, bs, A_REACH, W_GROUP).astype(BF16)
    past_v_all = cache_a_v.reshape(depth, bs, A_REACH, W_GROUP).astype(BF16)
    mem_k_all = cache_mem_k.reshape(depth, bs, n_mem, dx).astype(BF16)
    mem_v_all = cache_mem_v.reshape(depth, bs, n_mem, dx).astype(BF16)
    for l in range(depth):
        rec = (state_ret, state_mlstm_c, state_mlstm_n, state_mlstm_m, state_lru_h, state_conv, l)
        h, st, _, _ = _layer(h, tables, (mem_k_all, mem_v_all, l), (past_k_all, past_v_all, l), rec, weights[l],
                             ffn1_w[l], ffn2_w[l], alpha)
        sample_states.append(st)
    y_sample = h
    sample_out = [jnp.stack(f) for f in zip(*sample_states)]

    return (y_prompt, y_sample, *prompt_out, *sample_out)
```

```python
import functools

import numpy as np
import jax
import jax.numpy as jnp
from jax import lax
from jax.experimental import pallas as pl
from jax.experimental.pallas import tpu as pltpu

F32 = jnp.float32
BF16 = jnp.bfloat16

CHUNK = 64
HEAD_DIM = 64
N_HEADS = 4
W_GROUP = N_HEADS * HEAD_DIM
SLAB = 2 * HEAD_DIM
N_PAIRS = N_HEADS // 2
A_BAND_CHUNKS = 8
A_REACH = A_BAND_CHUNKS * CHUNK
REL_CLIP = 128
CONV_W = 4
LRU_C = 8.0
LN_EPS = 1e-5
LOG2E = 1.4426950408889634
N_GATES = 2 * N_HEADS
PAST_LEN = 4096
ROLL_W = 1024
CONV_PAD = 8

V7X_VMEM_LIMIT_BYTES = 56 * 1024 * 1024
TOKEN_TILE = 512
MIX_BLOCK = 256
MIX_SEQS = 4
POST_SUBTILES = 2

PR_QB, PR_QBS, PR_VB, PR_GB, PR_QC, PR_VC, PR_OC, PR_XD, PR_YD = range(9)
N_PR = 9


def _cparams(*sem):
    return pltpu.CompilerParams(dimension_semantics=sem, vmem_limit_bytes=V7X_VMEM_LIMIT_BYTES)


def _dot(a, b):
    return jnp.dot(a.astype(BF16), b.astype(BF16), preferred_element_type=F32)


def _dot_nt(a, b):
    return lax.dot_general(a.astype(BF16), b.astype(BF16), (((1,), (1,)), ((), ())), preferred_element_type=F32)


def _layer_norm(x, g, b):
    mu = jnp.mean(x, -1, keepdims=True)
    xc = x - mu
    var = jnp.mean(xc * xc, -1, keepdims=True)
    return xc * lax.rsqrt(var + LN_EPS) * g + b


def _silu(x):
    return x * jax.nn.sigmoid(x)


def _lockstep(chains):
    while chains:
        chains = [c for c in chains if next(c, "done") != "done"]
        yield


def _run_interleaved(chains, lag=0):
    pending, live, rnd = list(chains), [], 0
    while pending or live:
        while pending and rnd >= lag * (len(chains) - len(pending)):
            live.append(pending.pop(0))
        live = [c for c in live if next(c, "done") != "done"]
        rnd += 1


def _resident(shape, layer=None):
    nd = len(shape)
    if layer is None:
        return pl.BlockSpec(shape, lambda *_: (0,) * nd, pipeline_mode=pl.Buffered(1))
    return pl.BlockSpec((None,) + tuple(shape), lambda *_: (layer,) + (0,) * nd, pipeline_mode=pl.Buffered(1))


def _cast_plan(stack, layer, n_steps):
    _, r, c = stack.shape
    nb = next(k for k in range(n_steps, 0, -1) if n_steps % k == 0 and r % k == 0 and (r // k) % 16 == 0)
    every = n_steps // nb
    return (pl.BlockSpec((None, r // nb, c), lambda i: (layer, i // every, 0)),
            pl.BlockSpec((r // nb, c), lambda i: (i // every, 0)),
            jax.ShapeDtypeStruct((r, c), BF16))


def _with_casts(body, n_in, n_out, n_cast):
    def wrapped(*refs):
        ins, rest = refs[:n_in + n_cast], refs[n_in + n_cast:]
        outs, scratch = rest[:n_out + n_cast], rest[n_out + n_cast:]
        body(*ins[:n_in], *outs[:n_out], *scratch)
        for src, dst in zip(ins[n_in:], outs[n_out:]):
            dst[...] = src[...].astype(BF16)
    return wrapped


def _ffn_ln_kernel(x_ref, wg_ref, wu_ref, wd_ref, g_ref, b_ref, o_ref, *, alpha):
    x = x_ref[...]
    xb = x.astype(BF16)
    gate = jnp.dot(xb, wg_ref[...], preferred_element_type=F32)
    up = jnp.dot(xb, wu_ref[...], preferred_element_type=F32)
    h = (_silu(gate) * up).astype(BF16)
    y = jnp.dot(h, wd_ref[...], preferred_element_type=F32)
    o_ref[...] = _layer_norm(alpha * x + 0.5 * y, g_ref[...], b_ref[...])


def _ffn_ln(x, wg, wu, wd, g, b, alpha):
    n, d = x.shape
    dff = wg.shape[1]
    tm = TOKEN_TILE
    return pl.pallas_call(
        functools.partial(_ffn_ln_kernel, alpha=alpha),
        out_shape=jax.ShapeDtypeStruct((n, d), F32),
        grid=(n // tm,),
        in_specs=[pl.BlockSpec((tm, d), lambda i: (i, 0)),
                  _resident((d, dff)), _resident((d, dff)), _resident((dff, d)),
                  _resident((1, d)), _resident((1, d))],
        out_specs=pl.BlockSpec((tm, d), lambda i: (i, 0)),
        compiler_params=_cparams("parallel"),
        name="ffn_ln",
    )(x, wg, wu, wd, g, b)


def _in_proj_kernel(x_ref, wa_ref, wr_ref, wt_ref, ba_ref, br_ref, bt_ref,
                    pa_ref, pr_ref, kat_ref, kbt_ref, kct_ref, gt_ref):
    xb = x_ref[...].astype(BF16)
    pa_ref[...] = jnp.dot(xb, wa_ref[...], preferred_element_type=F32) + ba_ref[...]
    pr_ref[...] = jnp.dot(xb, wr_ref[...], preferred_element_type=F32) + br_ref[...]
    tr = _dot_nt(wt_ref[...], xb) + bt_ref[...]
    kat_ref[...] = tr[0:W_GROUP].astype(BF16)
    kbt_ref[...] = tr[W_GROUP:2 * W_GROUP]
    kct_ref[...] = tr[2 * W_GROUP:3 * W_GROUP]
    gt_ref[...] = tr[3 * W_GROUP:3 * W_GROUP + N_GATES]


def _in_proj(x, wa, wr, wt, ba, br, bt, casts=()):
    n, d = x.shape
    ca, cr, ct = wa.shape[1], wr.shape[1], wt.shape[0]
    tm = TOKEN_TILE
    tok = lambda w: pl.BlockSpec((tm, w), lambda i: (i, 0))
    trn = lambda r: pl.BlockSpec((r, tm), lambda i: (0, i))
    plans = [_cast_plan(stack, layer, n // tm) for stack, layer in casts]
    out = pl.pallas_call(
        _with_casts(_in_proj_kernel, 7, 6, len(casts)),
        out_shape=(jax.ShapeDtypeStruct((n, ca), F32), jax.ShapeDtypeStruct((n, cr), F32),
                   jax.ShapeDtypeStruct((W_GROUP, n), BF16), jax.ShapeDtypeStruct((W_GROUP, n), F32),
                   jax.ShapeDtypeStruct((W_GROUP, n), F32), jax.ShapeDtypeStruct((N_GATES, n), F32),
                   *[p[2] for p in plans]),
        grid=(n // tm,),
        in_specs=[tok(d), _resident((d, ca)), _resident((d, cr)), _resident((ct, d)),
                  _resident((1, ca)), _resident((1, cr)), _resident((ct, 1)), *[p[0] for p in plans]],
        out_specs=(tok(ca), tok(cr), trn(W_GROUP), trn(W_GROUP), trn(W_GROUP), trn(N_GATES), *[p[1] for p in plans]),
        compiler_params=_cparams("parallel"),
        name="in_proj",
    )(x, wa, wr, wt, ba, br, bt, *[stack for stack, _ in casts])
    return out[:6], out[6:]


def _head_rows(xt, h, pos):
    xh = xt[h * HEAD_DIM:(h + 1) * HEAD_DIM]
    z = jnp.zeros_like(xh)
    return jnp.concatenate([xh, z] if pos == 0 else [z, xh], axis=0)


def _col_bcast(eye_b, rows):
    hi = rows.astype(BF16)
    lo = (rows - hi.astype(F32)).astype(BF16)
    dn = (((1,), (1,)), ((), ()))
    return (lax.dot_general(eye_b, hi, dn, preferred_element_type=F32)
            + lax.dot_general(eye_b, lo, dn, preferred_element_type=F32))


def _rep_heads(rows4, n):
    return jnp.concatenate([jnp.broadcast_to(rows4[h:h + 1], (n, rows4.shape[1])) for h in range(N_HEADS)], axis=0)


def _seg_scan(x, seg_pos, seg_len, op, fill):
    s = 1
    while s < seg_len:
        x = op(x, jnp.where(seg_pos >= s, pltpu.roll(x, s, 1), fill))
        s *= 2
    return x


def _shift_rows(x, s, fill):
    if s % 8 == 0:
        return jnp.concatenate([jnp.full((s, x.shape[1]), fill, x.dtype), x[:x.shape[0] - s]], axis=0)
    row = lax.broadcasted_iota(jnp.int32, x.shape, 0)
    return jnp.where(row >= s, pltpu.roll(x, s, 0), fill)


def _mixer_kernel(pa_ref, pr_ref, kat_ref, kbt_ref, kct_ref, gt_ref, pastk_ref, pastv_ref, rel_ref,
                  cos_ref, sin_ref, cost_ref, sint_ref, dec_ref, qdec_ref, kdect_ref, sdec_ref,
                  ret0_ref, c0_ref, n0_ref, m0_ref,
                  cw_ref, cb_ref, wa_ref, ba_ref, wi_ref, bi_ref, lam_ref, buf0_ref, h0_ref,
                  y_ref, ret_out, c_out, n_out, m_out, h_out, buf_out,
                  bias_sc, spair, ppair, m_sc, win, h_sc, kpadt=None, vpad=None,
                  *, blk, seqs, n_steps, has_past, first_of_stack):
    state_outs = (ret_out, c_out, n_out, m_out, h_out, buf_out)
    if first_of_stack is not None:
        ret_out, c_out, n_out, m_out, h_out, buf_out = [ref.at[first_of_stack[0]] for ref in state_outs]
    L = blk
    band = A_REACH + L
    n_past = A_REACH // L if not has_past else 0
    n_var = n_past + 1
    bb = pl.program_id(0)
    tb = pl.program_id(1)
    tail = CONV_W - 1
    if n_steps == 1:
        at_first = at_last = lambda f: f()
    else:
        at_first = pl.when(tb == 0)
        at_last = pl.when(tb == n_steps - 1)

    li = lax.broadcasted_iota(jnp.int32, (L, L), 0)
    lj = lax.broadcasted_iota(jnp.int32, (L, L), 1)
    causal_neg = jnp.where(lj <= li, 0.0, -jnp.inf)
    eye_b = jnp.where(li == lj, 1.0, 0.0).astype(BF16)
    lane_lo = lax.broadcasted_iota(jnp.int32, (L, SLAB), 1) < HEAD_DIM
    ones_slab = jnp.ones((L, SLAB), BF16)
    ri = lax.broadcasted_iota(jnp.int32, (SLAB, SLAB), 0) // HEAD_DIM
    rj = lax.broadcasted_iota(jnp.int32, (SLAB, SLAB), 1) // HEAD_DIM
    smask = jnp.where(ri == rj, 1.0, 0.0)
    pmask = jnp.concatenate([smask, smask], axis=1)
    gi = lax.broadcasted_iota(jnp.int32, (W_GROUP, W_GROUP), 0) // HEAD_DIM
    gj = lax.broadcasted_iota(jnp.int32, (W_GROUP, W_GROUP), 1) // HEAD_DIM
    gmat = jnp.where(gi == gj, 1.0 / HEAD_DIM, 0.0).astype(BF16)
    e64i = lax.broadcasted_iota(jnp.int32, (HEAD_DIM, HEAD_DIM), 0)
    e64j = lax.broadcasted_iota(jnp.int32, (HEAD_DIM, HEAD_DIM), 1)
    eye64 = e64i == e64j
    row_lo = lax.broadcasted_iota(jnp.int32, (SLAB, 1), 0) < HEAD_DIM

    @pl.when((bb == 0) & (tb == 0))
    def _():
        qq = lax.broadcasted_iota(jnp.int32, (L, band), 0)
        kk = lax.broadcasted_iota(jnp.int32, (L, band), 1)
        off = kk - (qq // CHUNK) * CHUNK
        for h in range(N_HEADS):
            row = jnp.broadcast_to(rel_ref[h], (L, ROLL_W))
            tile = pltpu.roll(row, ROLL_W - L, 1, stride=1, stride_axis=0)[:, :band] * LOG2E
            tile = jnp.where(off >= 0, jnp.where(off < A_REACH + CHUNK, tile, -jnp.inf), -jnp.inf)
            for v in range(n_var):
                first_col = (n_past - v) * L
                bias_sc[v * N_HEADS + h] = jnp.where(kk >= first_col, tile, -jnp.inf) if first_col > 0 else tile

    for s in range(seqs):
        @at_first
        def _():
            for p in range(N_PAIRS):
                sblk, cblk = [], []
                for hh in range(2):
                    h = 2 * p + hh
                    z = jnp.zeros((HEAD_DIM, HEAD_DIM), F32)
                    s_h = ret0_ref[s, h]
                    c_h = c0_ref[s, h]
                    n_row = n0_ref[s, h:h + 1, :]
                    n_col = jnp.sum(jnp.where(eye64, jnp.broadcast_to(n_row, eye64.shape), 0.0),
                                    axis=1, keepdims=True)
                    n_rep = jnp.broadcast_to(n_col, (HEAD_DIM, HEAD_DIM))
                    sblk.append(jnp.concatenate([s_h, z] if hh == 0 else [z, s_h], axis=1))
                    cblk.append(jnp.concatenate([c_h, z, n_rep, z] if hh == 0 else [z, c_h, z, n_rep], axis=1))
                spair[s, p] = jnp.concatenate(sblk, axis=0)
                ppair[s, p] = jnp.concatenate(cblk, axis=0)
            m_sc[s, 0:N_HEADS, :] = jnp.broadcast_to(m0_ref[s], (N_HEADS, m_sc.shape[2]))
            win[s, CONV_PAD - tail:CONV_PAD, :] = buf0_ref[s]
            h_sc[s] = h0_ref[s]
            if not has_past:
                vpad[s, 0:A_REACH, :] = jnp.zeros((A_REACH, W_GROUP), BF16)
                kpadt[s, 0:n_past * W_GROUP, :] = jnp.zeros((n_past * W_GROUP, L), BF16)

    g8 = gt_ref[...]
    seg_pos = lax.broadcasted_iota(jnp.int32, g8.shape, 1) % L
    b8 = _seg_scan(jax.nn.log_sigmoid(g8), seg_pos, L, jnp.add, 0.0)
    a8 = g8 - pltpu.roll(b8, N_HEADS, 0)
    cm8 = _seg_scan(a8, seg_pos, L, jnp.maximum, -jnp.inf)
    b8 = pltpu.roll(b8, N_HEADS, 0)

    heads = [(h // 2, h % 2, h) for h in range(N_HEADS)]
    slab = lambda p: slice(p * SLAB, (p + 1) * SLAB)
    seg_of = lambda s: slice(s * L, (s + 1) * L)
    col = lambda s, j: pr_ref[s, :, j * W_GROUP:(j + 1) * W_GROUP]

    def halves(per_head):
        return jnp.concatenate([jnp.where(lane_lo, per_head[2 * p], per_head[2 * p + 1]) for p in range(N_PAIRS)],
                               axis=1)


    def attention_chain(s):
        pa = pa_ref[s]
        kat = kat_ref[:, seg_of(s)]
        v_new = pa[:, 2 * W_GROUP:].astype(BF16)
        if has_past:
            kpast = pastk_ref[s].astype(BF16)
            vband = jnp.concatenate([pastv_ref[s].astype(BF16), v_new], axis=0)
            bias_at = lambda h: bias_sc[h]
        else:
            kpadt[s, pl.ds(pl.multiple_of((tb + n_past) * W_GROUP, W_GROUP), W_GROUP), :] = kat
            kwin = kpadt[s, pl.ds(pl.multiple_of(tb * W_GROUP, W_GROUP), (n_past + 1) * W_GROUP), :]
            start = pl.multiple_of(tb * L, L)
            vpad[s, pl.ds(A_REACH + start, L), :] = v_new
            vband = vpad[s, pl.ds(start, band), :]
            var = jnp.minimum(tb, n_var - 1) * N_HEADS
            bias_at = lambda h: bias_sc[var + h]
        qa = (pa[:, 0:W_GROUP] * (HEAD_DIM ** -0.5 * LOG2E)).astype(BF16)
        sc = {}
        for p, hh, h in heads:
            q_slab = qa[:, slab(p)]
            if has_past:
                k_slab = kpast[:, slab(p)]
                lane = lax.broadcasted_iota(jnp.int32, k_slab.shape, 1)
                lane_h = lane < HEAD_DIM if hh == 0 else lane >= HEAD_DIM
                parts = [_dot_nt(q_slab, jnp.where(lane_h, k_slab, jnp.zeros_like(k_slab))),
                         jnp.dot(q_slab, _head_rows(kat, h, hh), preferred_element_type=F32)]
            else:
                parts = [jnp.dot(q_slab, _head_rows(kwin[i * W_GROUP:(i + 1) * W_GROUP], h, hh),
                                 preferred_element_type=F32) for i in range(n_past + 1)]
            sc[h] = jnp.concatenate(parts, axis=1)
        yield
        pe, den = {}, {}
        for _, _, h in heads:
            sh = sc[h] + bias_at(h)
            pe[h] = jnp.exp2(sh - jnp.max(sh, axis=1, keepdims=True))
            den[h] = jnp.sum(pe[h], axis=1, keepdims=True)
        yield
        res = {h: _dot(pe[h], vband)[:, slab(p)] for p, _, h in heads}
        yield
        y_ref[s, :, 0:W_GROUP] = halves({h: res[h] / den[h] for _, _, h in heads})

    def retention_chain(s):
        qr = (col(s, PR_QB) * cos_ref[...] + col(s, PR_QBS) * sin_ref[...]).astype(BF16)
        kbt = kbt_ref[:, seg_of(s)]
        half = HEAD_DIM // 2
        kbt_sw = jnp.concatenate([kbt[h * HEAD_DIM + o:h * HEAD_DIM + o + half]
                                  for h in range(N_HEADS) for o in (half, 0)], axis=0)
        krt = (kbt * cost_ref[:, seg_of(s)] + kbt_sw * sint_ref[:, seg_of(s)]) * HEAD_DIM ** -0.5
        krt_b = krt.astype(BF16)
        kdt_b = (krt * kdect_ref[...]).astype(BF16)
        vb = col(s, PR_VB).astype(BF16)
        raw = {h: jnp.dot(qr[:, slab(p)], _head_rows(krt_b, h, hh), preferred_element_type=F32)
               for p, hh, h in heads}
        s0 = {p: spair[s, p] for p in range(N_PAIRS)}
        inter = {p: _dot(qr[:, slab(p)], s0[p]) for p in range(N_PAIRS)}
        add = {p: _dot(kdt_b[slab(p)], vb[:, slab(p)]) for p in range(N_PAIRS)}
        yield
        scores = {h: raw[h] * dec_ref[h] for _, _, h in heads}
        for p in range(N_PAIRS):
            spair[s, p] = s0[p] * sdec_ref[p] + add[p] * smask
        yield
        o = {h: _dot(scores[h], vb[:, slab(p)]) for p, _, h in heads}
        yield
        ob = halves(o) + jnp.concatenate([inter[p] for p in range(N_PAIRS)], axis=1) * qdec_ref[...]
        mu = _dot(ob, gmat)
        yield
        oc = ob - mu
        var = _dot(oc * oc, gmat)
        yield
        y_ref[s, :, W_GROUP:2 * W_GROUP] = oc * lax.rsqrt(var + LN_EPS) * _silu(col(s, PR_GB))

    def mlstm_chain(s):
        a4 = a8[0:N_HEADS, seg_of(s)]
        b4 = b8[0:N_HEADS, seg_of(s)]
        m0c = m_sc[s, 0:N_HEADS, 0:1]
        big_m = jnp.maximum(m0c, cm8[0:N_HEADS, seg_of(s)])
        m4 = b4 + big_m
        w0_4 = jnp.exp(m0c - big_m)
        m_last = big_m[:, L - 1:L]
        wrow4 = jnp.exp(a4 - m_last)
        kct = kct_ref[:, seg_of(s)] * HEAD_DIM ** -0.5
        kct_b = kct.astype(BF16)
        kw_b = (kct * _rep_heads(wrow4, HEAD_DIM)).astype(BF16)
        qc = col(s, PR_QC).astype(BF16)
        vc = col(s, PR_VC).astype(BF16)
        v_aug = {p: jnp.concatenate([vc[:, slab(p)], ones_slab], axis=1) for p in range(N_PAIRS)}
        p0 = {p: ppair[s, p] for p in range(N_PAIRS)}
        w0_bc = _col_bcast(eye_b, _rep_heads(w0_4, HEAD_DIM))
        floor_bc = _col_bcast(eye_b, _rep_heads(jnp.exp(-m4), HEAD_DIM))
        m_bc = {h: _col_bcast(eye_b, jnp.broadcast_to(big_m[h:h + 1], (L, L))) for _, _, h in heads}
        raw = {h: jnp.dot(qc[:, slab(p)], _head_rows(kct_b, h, hh), preferred_element_type=F32)
               for p, hh, h in heads}
        inter = {p: _dot(qc[:, slab(p)], p0[p]) for p in range(N_PAIRS)}
        add = {p: _dot(kw_b[slab(p)], v_aug[p]) for p in range(N_PAIRS)}
        yield
        qk = {h: raw[h] * jnp.exp(a4[h:h + 1] - m_bc[h] + causal_neg) for _, _, h in heads}
        for p in range(N_PAIRS):
            w_state = jnp.where(row_lo, w0_4[2 * p:2 * p + 1, L - 1:L], w0_4[2 * p + 1:2 * p + 2, L - 1:L])
            ppair[s, p] = p0[p] * w_state + add[p] * pmask
        m_sc[s, 0:N_HEADS, :] = jnp.broadcast_to(m4[:, L - 1:L], (N_HEADS, m_sc.shape[2]))
        yield
        res = {h: _dot(qk[h], v_aug[p]) for p, _, h in heads}
        yield
        hc = []
        for p in range(N_PAIRS):
            w0 = w0_bc[:, slab(p)]
            num = jnp.where(lane_lo, res[2 * p][:, 0:SLAB], res[2 * p + 1][:, 0:SLAB]) + inter[p][:, 0:SLAB] * w0
            den = jnp.where(lane_lo, res[2 * p][:, SLAB:], res[2 * p + 1][:, SLAB:]) + inter[p][:, SLAB:] * w0
            hc.append(num / jnp.maximum(jnp.abs(den), floor_bc[:, slab(p)]))
        hc = jnp.concatenate(hc, axis=1)
        mu = _dot(hc, gmat)
        yield
        hcc = hc - mu
        var = _dot(hcc * hcc, gmat)
        yield
        y_ref[s, :, 2 * W_GROUP:3 * W_GROUP] = hcc * lax.rsqrt(var + LN_EPS) * jax.nn.sigmoid(col(s, PR_OC))

    def rglru_chain(s):
        xd = col(s, PR_XD)
        win[s, CONV_PAD:CONV_PAD + L, :] = xd
        xc = cb_ref[...]
        for j in range(CONV_W):
            off = CONV_PAD - tail + j
            xc = xc + win[s, off:off + L, :] * cw_ref[j:j + 1, :]
        win[s, CONV_PAD - tail:CONV_PAD, :] = xd[L - tail:, :]
        r_pre = _dot(xc, wa_ref[...])
        i_pre = _dot(xc, wi_ref[...])
        yield
        r = jax.nn.sigmoid(r_pre + ba_ref[...])
        i = jax.nn.sigmoid(i_pre + bi_ref[...])
        neg_lam = -lam_ref[...]
        softplus = jnp.maximum(neg_lam, 0.0) + jnp.log1p(jnp.exp(-jnp.abs(neg_lam)))
        log_a = -LRU_C * r * softplus
        a = jnp.exp(log_a)
        u = jnp.sqrt(-jnp.tanh(log_a) * (jnp.exp(2.0 * log_a) + 1.0)) * (i * xc)
        st = 1
        while st < L:
            u = a * _shift_rows(u, st, 0.0) + u
            a = a * _shift_rows(a, st, 1.0)
            st *= 2
            if st in (8, 64):
                yield
        hseq = u + a * h_sc[s]
        h_sc[s] = hseq[L - 1:L, :]
        y_ref[s, :, 3 * W_GROUP:] = hseq * jax.nn.gelu(col(s, PR_YD))

    _run_interleaved([chain(s) for chain in (attention_chain, retention_chain, mlstm_chain, rglru_chain)
                      for s in range(seqs)], lag=1 if seqs == 1 else 0)

    for s in range(seqs):
        @at_last
        def _():
            for p in range(N_PAIRS):
                sp = spair[s, p]
                pp = ppair[s, p]
                for hh in range(2):
                    h = 2 * p + hh
                    rs = slice(hh * HEAD_DIM, (hh + 1) * HEAD_DIM)
                    ret_out[s, h] = sp[rs, rs]
                    c_out[s, h] = pp[rs, rs]
                    n_rep = pp[rs, SLAB + hh * HEAD_DIM:SLAB + (hh + 1) * HEAD_DIM]
                    n_out[s, h:h + 1, :] = jnp.sum(jnp.where(eye64, n_rep, 0.0), axis=0, keepdims=True)
            m_out[s] = m_sc[s, 0:N_HEADS, 0:1]
            h_out[s] = h_sc[s]
            buf_out[s] = pr_ref[s, L - tail:, PR_XD * W_GROUP:(PR_XD + 1) * W_GROUP]
            if first_of_stack is not None:
                for other in first_of_stack[1]:
                    for ref in state_outs:
                        ref[other, s] = jnp.zeros(ref.shape[2:], F32)


def _mixer_kernel_fresh(pa_ref, pr_ref, kat_ref, kbt_ref, kct_ref, gt_ref, *rest, **static):
    _mixer_kernel(pa_ref, pr_ref, kat_ref, kbt_ref, kct_ref, gt_ref, None, None, *rest, **static)


def _mixer_geometry(t, has_past):
    if has_past:
        assert t == CHUNK, "sequences with a carried cache are expected to be a single chunk"
        return t, MIX_SEQS
    assert t % MIX_BLOCK == 0 and A_REACH % MIX_BLOCK == 0
    return MIX_BLOCK, 1


def _mixer_tables(pos, blk, seqs):
    cos, sin = _rotary_tables(pos)
    return (cos, sin, jnp.tile(cos.T, (1, seqs)), jnp.tile(sin.T, (1, seqs))) + _retention_tables(blk)


def _mixer(pa, pr, kat, kbt, kct, gt, past, rel_rows, tables, rec, lw, n_layers, prev_states=None):
    ret0, c0, n0, m0, h0, buf0, rec_layer = rec
    m0, h0 = m0[..., None], h0[:, :, None, :]
    bsz, t, _ = pa.shape
    has_past = past is not None
    blk, seqs = _mixer_geometry(t, has_past)
    nb = t // blk
    assert bsz % seqs == 0
    band = A_REACH + blk
    n_var = 1 if has_past else A_REACH // blk + 1
    tail = CONV_W - 1

    tok3 = lambda w: pl.BlockSpec((seqs, blk, w), lambda b, c: (b, c, 0))
    trn = lambda r: pl.BlockSpec((r, seqs * blk), lambda b, c: (0, b * nb + c))
    per_seq = lambda *shape: pl.BlockSpec((seqs,) + shape, lambda b, c: (b,) + (0,) * len(shape))
    const = lambda *shape: pl.BlockSpec(shape, lambda b, c: (0,) * len(shape))
    carried = lambda *shape: pl.BlockSpec((None, seqs) + shape, lambda b, c: (rec_layer, b) + (0,) * len(shape))
    past_specs, past_args = [], ()
    if has_past:
        layer = past[2]
        cache = pl.BlockSpec((None, seqs, A_REACH, W_GROUP), lambda b, c: (layer, b, 0, 0))
        past_specs, past_args = [cache, cache], past[:2]
    in_specs = [tok3(3 * W_GROUP), tok3(N_PR * W_GROUP), trn(W_GROUP), trn(W_GROUP), trn(W_GROUP), trn(N_GATES),
                *past_specs, const(N_HEADS, 1, ROLL_W),
                pl.BlockSpec((blk, W_GROUP), lambda b, c: (c, 0)), pl.BlockSpec((blk, W_GROUP), lambda b, c: (c, 0)),
                pl.BlockSpec((W_GROUP, seqs * blk), lambda b, c: (0, c)),
                pl.BlockSpec((W_GROUP, seqs * blk), lambda b, c: (0, c)),
                const(N_HEADS, blk, blk), const(blk, W_GROUP), const(W_GROUP, blk), const(N_PAIRS, SLAB, SLAB),
                carried(N_HEADS, HEAD_DIM, HEAD_DIM), carried(N_HEADS, HEAD_DIM, HEAD_DIM),
                carried(N_HEADS, HEAD_DIM), carried(N_HEADS, 1),
                const(CONV_W, W_GROUP), const(1, W_GROUP), const(W_GROUP, W_GROUP), const(1, W_GROUP),
                const(W_GROUP, W_GROUP), const(1, W_GROUP), const(1, W_GROUP),
                carried(tail, W_GROUP), carried(1, W_GROUP)]
    out_layer = lw['layer']
    state_shapes = ((N_HEADS, HEAD_DIM, HEAD_DIM), (N_HEADS, HEAD_DIM, HEAD_DIM), (N_HEADS, HEAD_DIM), (N_HEADS, 1),
                    (1, W_GROUP), (tail, W_GROUP))
    if prev_states is None:
        first_of_stack = (out_layer, tuple(l for l in range(n_layers) if l != out_layer))
        produced = lambda *shape: pl.BlockSpec((n_layers, seqs) + shape, lambda b, c: (0, b) + (0,) * len(shape))
    else:
        first_of_stack = None
        produced = lambda *shape: pl.BlockSpec((None, seqs) + shape, lambda b, c: (out_layer, b) + (0,) * len(shape))
    out_shape = (jax.ShapeDtypeStruct((bsz, t, 4 * W_GROUP), F32),
                 *[jax.ShapeDtypeStruct((n_layers, bsz) + sh, F32) for sh in state_shapes])
    out_specs = (tok3(4 * W_GROUP), *[produced(*sh) for sh in state_shapes])
    n_in = len(in_specs)
    aliases = {}
    if prev_states is not None:
        in_specs += [pl.BlockSpec(memory_space=pl.ANY)] * len(state_shapes)
        aliases = {n_in + k: 1 + k for k in range(len(state_shapes))}
    scratch = [pltpu.VMEM((n_var * N_HEADS, blk, band), F32),
               pltpu.VMEM((seqs, N_PAIRS, SLAB, SLAB), F32), pltpu.VMEM((seqs, N_PAIRS, SLAB, 2 * SLAB), F32),
               pltpu.VMEM((seqs, 8, 128), F32), pltpu.VMEM((seqs, CONV_PAD + blk, W_GROUP), F32),
               pltpu.VMEM((seqs, 1, W_GROUP), F32)]
    if not has_past:
        scratch += [pltpu.VMEM((seqs, (nb + A_REACH // blk) * W_GROUP, blk), BF16),
                    pltpu.VMEM((seqs, A_REACH + t, W_GROUP), BF16)]
    body = functools.partial(_mixer_kernel if has_past else _mixer_kernel_fresh,
                             blk=blk, seqs=seqs, n_steps=nb, has_past=has_past, first_of_stack=first_of_stack)
    if prev_states is not None:
        inner, n_alias = body, len(state_shapes)
        body = lambda *refs: inner(*refs[:n_in], *refs[n_in + n_alias:])
    return pl.pallas_call(
        body,
        out_shape=out_shape,
        grid=(bsz // seqs, nb),
        in_specs=in_specs,
        out_specs=out_specs,
        scratch_shapes=scratch,
        input_output_aliases=aliases,
        compiler_params=_cparams("arbitrary", "arbitrary"),
        name="mixer",
    )(pa, pr, kat, kbt, kct, gt, *past_args, rel_rows, *tables,
      ret0, c0, n0, m0, lw['conv_w'], lw['conv_b'], lw['lru_wa'], lw['lru_ba'], lw['lru_wi'], lw['lru_bi'],
      lw['lru_lambda'], buf0, h0, *(prev_states or ()))


def _post_kernel(x_ref, y_ref, wo_ref, g1_ref, b1_ref, wq_ref, mk_ref, mv_ref, xo_ref,
                 g2_ref, b2_ref, o_ref, att_sc, *, alpha, seqs, subtiles):
    rows_sub = x_ref.shape[0] // subtiles
    rows = rows_sub // seqs
    head = lambda h: slice(h * HEAD_DIM, (h + 1) * HEAD_DIM)

    def subtile_chain(i):
        rs = slice(i * rows_sub, (i + 1) * rows_sub)
        x = x_ref[rs, :]
        mix = _dot(y_ref[rs, :], wo_ref[...])
        yield
        x2 = _layer_norm(alpha * x + mix, g1_ref[...], b1_ref[...])
        yield
        q = _dot(x2, wq_ref[...]).astype(BF16)
        yield

        def attend(s, h):
            qs = slice(s * rows, (s + 1) * rows)
            sc = _dot_nt(q[qs, head(h)], mk_ref[s, :, head(h)]) * HEAD_DIM ** -0.5
            yield
            p = jnp.exp(sc - jnp.max(sc, axis=1, keepdims=True))
            den = jnp.sum(p, axis=1, keepdims=True)
            yield
            o = _dot(p, mv_ref[s, :, head(h)])
            yield
            att_sc[i * rows_sub + s * rows:i * rows_sub + (s + 1) * rows, head(h)] = o / den

        yield from _lockstep([attend(s, h) for s in range(seqs) for h in range(N_HEADS)])
        out = _dot(att_sc[rs, :], xo_ref[...])
        yield
        o_ref[rs, :] = _layer_norm(alpha * x2 + out, g2_ref[...], b2_ref[...])

    _run_interleaved([subtile_chain(i) for i in range(subtiles)], lag=2)


def _post(x, y, wo, wq, xo, wlayer, g1, b1, g2, b2, mk, mv, layer, alpha, seq_len, casts=()):
    n, d = x.shape
    n_mem, dx = mk.shape[2], mk.shape[3]
    if seq_len >= POST_SUBTILES * TOKEN_TILE:
        subtiles, seqs = POST_SUBTILES, 1
        tm = subtiles * TOKEN_TILE
        mem = pl.BlockSpec((None, 1, n_mem, dx), lambda i: (layer, i // (seq_len // tm), 0, 0))
    else:
        tm = TOKEN_TILE
        subtiles, seqs = 1, tm // seq_len
        mem = pl.BlockSpec((None, seqs, n_mem, dx), lambda i: (layer, i, 0, 0))
    assert n % tm == 0 and tm % (subtiles * seqs) == 0
    tok = lambda w: pl.BlockSpec((tm, w), lambda i: (i, 0))
    plans = [_cast_plan(stack, lyr, n // tm) for stack, lyr in casts]
    out = pl.pallas_call(
        _with_casts(functools.partial(_post_kernel, alpha=alpha, seqs=seqs, subtiles=subtiles), 11, 1, len(casts)),
        out_shape=(jax.ShapeDtypeStruct((n, d), F32), *[p[2] for p in plans]),
        grid=(n // tm,),
        in_specs=[tok(d), tok(y.shape[1]),
                  _resident(wo.shape[1:], wlayer), _resident((1, d)), _resident((1, d)),
                  _resident(wq.shape[1:], wlayer), mem, mem, _resident(xo.shape[1:], wlayer),
                  _resident((1, d)), _resident((1, d)), *[p[0] for p in plans]],
        out_specs=(tok(d), *[p[1] for p in plans]),
        scratch_shapes=[pltpu.VMEM((tm, dx), F32)],
        compiler_params=_cparams("parallel"),
        name="post",
    )(x, y, wo, g1, b1, wq, mk, mv, xo, g2, b2, *[stack for stack, _ in casts])
    return out[0], out[1:]


def _mem_kv_kernel(x_ref, w_ref, o_ref):
    o_ref[...] = _dot(x_ref[...], w_ref[...])


def _mem_kv(mem, w):
    n, d = mem.shape
    tm = min(TOKEN_TILE, n)
    return pl.pallas_call(
        _mem_kv_kernel,
        out_shape=jax.ShapeDtypeStruct((n, w.shape[1]), F32),
        grid=(n // tm,),
        in_specs=[pl.BlockSpec((tm, d), lambda i: (i, 0)), _resident(w.shape)],
        out_specs=pl.BlockSpec((tm, w.shape[1]), lambda i: (i, 0)),
        compiler_params=_cparams("parallel"),
        name="mem_kv",
    )(mem, w)


def _rotary_tables(pos):
    half = HEAD_DIM // 2
    inv = jnp.exp(-jnp.log(10000.0) * jnp.arange(half, dtype=F32) / half)
    ang = pos.astype(F32)[:, None] * inv[None, :]
    cos, sin = jnp.cos(ang), jnp.sin(ang)
    cos_full = jnp.tile(jnp.concatenate([cos, cos], -1), (1, N_HEADS))
    sin_full = jnp.tile(jnp.concatenate([-sin, sin], -1), (1, N_HEADS))
    return cos_full, sin_full


def _retention_tables(blk):
    log_g = jnp.log1p(-jnp.exp2(-5.0 - jnp.arange(N_HEADS, dtype=F32)))
    idx = jnp.arange(blk, dtype=F32)
    diff = idx[:, None] - idx[None, :]
    dec = jnp.exp(jnp.where((diff >= 0)[None], diff[None] * log_g[:, None, None], -jnp.inf))
    q_dec = jnp.exp((idx[:, None] + 1.0) * log_g[None, :])
    k_dec = jnp.exp((blk - 1.0 - idx)[:, None] * log_g[None, :])
    s_dec = jnp.exp(blk * log_g)
    lanes = lambda tbl: jnp.repeat(tbl, HEAD_DIM, axis=1)
    pair = lambda p: jnp.kron(jnp.diag(s_dec[2 * p:2 * p + 2]), jnp.ones((HEAD_DIM, HEAD_DIM), F32))
    return dec, lanes(q_dec), lanes(k_dec).T, jnp.stack([pair(p) for p in range(N_PAIRS)])


def _rel_bias_rows(rel_bias, blk):
    idx = np.clip(A_REACH + blk - np.arange(ROLL_W), -REL_CLIP, REL_CLIP) + REL_CLIP
    return rel_bias[:, idx][:, None, :]


def _block_diag(w):
    h, c, _ = w.shape
    eye = jnp.eye(h, dtype=w.dtype)
    return (eye[:, None, :, None] * w[:, :, None, :]).reshape(h * c, h * c)


def _swap_perm():
    j = np.arange(W_GROUP)
    return (j // HEAD_DIM) * HEAD_DIM + (j % HEAD_DIM + HEAD_DIM // 2) % HEAD_DIM


def _layer(x, tables, mem, past, rec, lw, ffn1, ffn2, alpha, n_layers, prev_states, proj_casts=(), post_casts=()):
    bsz, t, d = x.shape
    n = bsz * t
    g, b = lw['ln_g'], lw['ln_b']

    x1 = _ffn_ln(x.reshape(n, d), *ffn1, g[0:1], b[0:1], alpha)
    (pa, pr, kat, kbt, kct, gt), proj_cast_out = _in_proj(x1, lw['w_in_a'], lw['w_in_r'], lw['w_in_t'],
                                                          lw['b_in_a'], lw['b_in_r'], lw['b_in_t'], proj_casts)
    if ffn2 is None:
        ffn2 = proj_cast_out[:3]
    pa = pa.reshape(bsz, t, 3 * W_GROUP)
    pr = pr.reshape(bsz, t, N_PR * W_GROUP)
    blk, _ = _mixer_geometry(t, past is not None)
    y, *rec_stacks = _mixer(pa, pr, kat, kbt, kct, gt, past, _rel_bias_rows(lw['rel_bias'], blk), tables, rec, lw,
                            n_layers, prev_states)

    x3, post_cast_out = _post(x1, y.reshape(n, 4 * W_GROUP), lw['w_out'], lw['x_wq'], lw['x_wo'], lw['layer'],
                              g[1:2], b[1:2], g[2:3], b[2:3], *mem, alpha, t, post_casts)
    x4 = _ffn_ln(x3, *ffn2, g[3:4], b[3:4], alpha)

    heads = lambda a: a.reshape(bsz, -1, N_HEADS, HEAD_DIM)
    keep = min(A_REACH, t)
    a_k = heads(pa[:, t - keep:, W_GROUP:2 * W_GROUP])
    a_v = heads(pa[:, t - keep:, 2 * W_GROUP:])
    return x4.reshape(bsz, t, d), (a_k, a_v), rec_stacks, proj_cast_out, post_cast_out


def _result_states(stacks):
    ret, c, n, m, h, conv = stacks
    return [ret, c, n, m[..., 0], h[:, :, 0, :], conv]


def kernel(x_prompt, x_sample, mem_prompt, cache_a_k, cache_a_v, state_ret, state_mlstm_c, state_mlstm_n,
           state_mlstm_m, state_lru_h, state_conv, cache_mem_k, cache_mem_v, ln_g, ln_b,
           ffn1_gate, ffn1_up, ffn1_down, ffn2_gate, ffn2_up, ffn2_down, w_in, b_in, a_rel_bias,
           conv_w, conv_b, lru_wa, lru_ba, lru_wi, lru_bi, lru_lambda, w_out, x_wq, x_wk, x_wv, x_wo):
    depth = ln_g.shape[0]
    alpha = (2.0 * depth) ** 0.25
    row = lambda v: v[None, :]
    grp = lambda a, j: a[..., j * W_GROUP:(j + 1) * W_GROUP]
    perm = _swap_perm()
    QA, KA, VA, QB, KB, VB, GB, QC, KC, VC, OC, XD, YD = range(13)

    def layer_weights(l):
        w, bias = w_in[l], b_in[l]
        nat = lambda a: jnp.concatenate(
            [grp(a, QB), grp(a, QB)[..., perm], grp(a, VB), grp(a, GB), grp(a, QC), grp(a, VC), grp(a, OC),
             grp(a, XD), grp(a, YD)], axis=-1)
        trn = lambda a: jnp.concatenate([grp(a, KA), grp(a, KB), grp(a, KC), a[..., 13 * W_GROUP:]], axis=-1)
        return {'ln_g': ln_g[l], 'ln_b': ln_b[l],
                'layer': l, **stacks,
                'w_in_a': w[:, :3 * W_GROUP].astype(BF16), 'w_in_r': nat(w).astype(BF16),
                'w_in_t': trn(w).T.astype(BF16),
                'b_in_a': row(bias[:3 * W_GROUP]), 'b_in_r': row(nat(bias)), 'b_in_t': trn(bias)[:, None],
                'rel_bias': a_rel_bias[l],
                'conv_w': conv_w[l], 'conv_b': row(conv_b[l]),
                'lru_wa': _block_diag(lru_wa[l]).astype(BF16), 'lru_ba': row(lru_ba[l]),
                'lru_wi': _block_diag(lru_wi[l]).astype(BF16), 'lru_bi': row(lru_bi[l]),
                'lru_lambda': row(lru_lambda[l]),
                'x_wkv': jnp.concatenate([x_wk[l], x_wv[l]], axis=1).astype(BF16)}

    stacks = {name: w.astype(BF16) for name, w in (('w_out', w_out), ('x_wq', x_wq), ('x_wo', x_wo))}
    ffn1_f32, ffn2_f32 = (ffn1_gate, ffn1_up, ffn1_down), (ffn2_gate, ffn2_up, ffn2_down)
    ffn1_w = {0: tuple(w[0].astype(BF16) for w in ffn1_f32)}
    ffn2_w = {}
    weights = [layer_weights(l) for l in range(depth)]
    dx = x_wk.shape[2]

    bp, tp, d = x_prompt.shape
    n_mem = mem_prompt.shape[1]
    h = x_prompt
    prompt_states, rec_p = [], None
    tables = _mixer_tables(jnp.arange(tp), *_mixer_geometry(tp, False))
    rec0 = (jnp.zeros((1, bp, N_HEADS, HEAD_DIM, HEAD_DIM), F32), jnp.zeros((1, bp, N_HEADS, HEAD_DIM, HEAD_DIM), F32),
            jnp.zeros((1, bp, N_HEADS, HEAD_DIM), F32), jnp.zeros((1, bp, N_HEADS), F32),
            jnp.zeros((1, bp, W_GROUP), F32), jnp.zeros((1, bp, CONV_W - 1, W_GROUP), F32), 0)
    for l in range(depth):
        lw = weights[l]
        mkv = _mem_kv(mem_prompt.reshape(bp * n_mem, d), lw['x_wkv'])
        mk = mkv[:, :dx].reshape(bp, n_mem, dx)
        mv = mkv[:, dx:].reshape(bp, n_mem, dx)
        post_casts = [(w, l + 1) for w in ffn1_f32] if l + 1 < depth else []
        h, st, rec_p, ffn2_w[l], cast_next = _layer(h, tables, (mk[None], mv[None], 0), None, rec0, lw, ffn1_w[l], None,
                                                    alpha, depth, rec_p, [(w, l) for w in ffn2_f32], post_casts)
        if post_casts:
            ffn1_w[l + 1] = cast_next
        prompt_states.append(st + (mk.reshape(bp, n_mem, -1, HEAD_DIM), mv.reshape(bp, n_mem, -1, HEAD_DIM)))
    y_prompt = h
    p_ak, p_av, p_mk, p_mv = [jnp.stack(f) for f in zip(*prompt_states)]
    prompt_out = [p_ak, p_av, *_result_states(rec_p), p_mk, p_mv]

    bs, ts, _ = x_sample.shape
    h = x_sample
    sample_states, rec_s = [], None
    tables = _mixer_tables(PAST_LEN + jnp.arange(ts), *_mixer_geometry(ts, True))
    past_k_all = cache_a_k.reshape(depth, bs, A_REACH, W_GROUP)
    past_v_all = cache_a_v.reshape(depth, bs, A_REACH, W_GROUP)
    mem_k_all = cache_mem_k.reshape(depth, bs, n_mem, dx)
    mem_v_all = cache_mem_v.reshape(depth, bs, n_mem, dx)
    for l in range(depth):
        rec = (state_ret, state_mlstm_c, state_mlstm_n, state_mlstm_m, state_lru_h, state_conv, l)
        h, st, rec_s, _, _ = _layer(h, tables, (mem_k_all, mem_v_all, l), (past_k_all, past_v_all, l), rec, weights[l],
                                    ffn1_w[l], ffn2_w[l], alpha, depth, rec_s)
        sample_states.append(st)
    y_sample = h
    sample_out = [jnp.stack(f) for f in zip(*sample_states)] + _result_states(rec_s)

    return (y_prompt, y_sample, *prompt_out, *sample_out)
```

```python
import functools

import numpy as np
import jax
import jax.numpy as jnp
from jax import lax
from jax.experimental import pallas as pl
from jax.experimental.pallas import tpu as pltpu

F32 = jnp.float32
BF16 = jnp.bfloat16

CHUNK = 64
HEAD_DIM = 64
N_HEADS = 4
W_GROUP = N_HEADS * HEAD_DIM
SLAB = 2 * HEAD_DIM
N_PAIRS = N_HEADS // 2
A_BAND_CHUNKS = 8
A_REACH = A_BAND_CHUNKS * CHUNK
REL_CLIP = 128
CONV_W = 4
LRU_C = 8.0
LN_EPS = 1e-5
LOG2E = 1.4426950408889634
N_GATES = 2 * N_HEADS
PAST_LEN = 4096
ROLL_W = 1024
CONV_PAD = 8

V7X_VMEM_LIMIT_BYTES = 56 * 1024 * 1024
TOKEN_TILE = 512
MIX_BLOCK = 256
MIX_SEQS = 4
POST_SUBTILES = 2

PR_QB, PR_QBS, PR_VB, PR_GB, PR_QC, PR_VC, PR_OC, PR_XD, PR_YD = range(9)
N_PR = 9


def _cparams(*sem):
    return pltpu.CompilerParams(dimension_semantics=sem, vmem_limit_bytes=V7X_VMEM_LIMIT_BYTES)


def _dot(a, b):
    return jnp.dot(a.astype(BF16), b.astype(BF16), preferred_element_type=F32)


def _dot_nt(a, b):
    return lax.dot_general(a.astype(BF16), b.astype(BF16), (((1,), (1,)), ((), ())), preferred_element_type=F32)


def _layer_norm(x, g, b):
    mu = jnp.mean(x, -1, keepdims=True)
    xc = x - mu
    var = jnp.mean(xc * xc, -1, keepdims=True)
    return xc * lax.rsqrt(var + LN_EPS) * g + b


def _silu(x):
    return x * jax.nn.sigmoid(x)


def _lockstep(chains):
    while chains:
        chains = [c for c in chains if next(c, "done") != "done"]
        yield


def _run_interleaved(chains, lag=0):
    pending, live, rnd = list(chains), [], 0
    while pending or live:
        while pending and rnd >= lag * (len(chains) - len(pending)):
            live.append(pending.pop(0))
        live = [c for c in live if next(c, "done") != "done"]
        rnd += 1


def _resident(shape, layer=None):
    nd = len(shape)
    if layer is None:
        return pl.BlockSpec(shape, lambda *_: (0,) * nd, pipeline_mode=pl.Buffered(1))
    return pl.BlockSpec((None,) + tuple(shape), lambda *_: (layer,) + (0,) * nd, pipeline_mode=pl.Buffered(1))


def _cast_plan(stack, layer, n_steps):
    _, r, c = stack.shape
    nb = next(k for k in range(n_steps, 0, -1) if n_steps % k == 0 and r % k == 0 and (r // k) % 16 == 0)
    every = n_steps // nb
    return (pl.BlockSpec((None, r // nb, c), lambda i: (layer, i // every, 0)),
            pl.BlockSpec((r // nb, c), lambda i: (i // every, 0)),
            jax.ShapeDtypeStruct((r, c), BF16))


def _with_casts(body, n_in, n_out, n_cast):
    def wrapped(*refs):
        ins, rest = refs[:n_in + n_cast], refs[n_in + n_cast:]
        outs, scratch = rest[:n_out + n_cast], rest[n_out + n_cast:]
        body(*ins[:n_in], *outs[:n_out], *scratch)
        for src, dst in zip(ins[n_in:], outs[n_out:]):
            dst[...] = src[...].astype(BF16)
    return wrapped


def _by_group(body, n_first, n_shared, n_out):
    def wrapped(xa_ref, xb_ref, *refs):
        shared, outs = refs[:n_shared], refs[n_shared:]
        i = pl.program_id(0)

        @pl.when(i < n_first)
        def _():
            body(xa_ref, *shared, *outs[:n_out])

        @pl.when(i >= n_first)
        def _():
            body(xb_ref, *shared, *outs[n_out:2 * n_out])
    return wrapped


def _group_blocks(tm, n_first):
    tok_a = lambda w: pl.BlockSpec((tm, w), lambda i: (jnp.minimum(i, n_first - 1), 0))
    tok_b = lambda w: pl.BlockSpec((tm, w), lambda i: (jnp.maximum(i - n_first, 0), 0))
    return tok_a, tok_b


def _ffn_ln_kernel(x_ref, wg_ref, wu_ref, wd_ref, g_ref, b_ref, o_ref, *, alpha):
    x = x_ref[...]
    xb = x.astype(BF16)
    gate = jnp.dot(xb, wg_ref[...], preferred_element_type=F32)
    up = jnp.dot(xb, wu_ref[...], preferred_element_type=F32)
    h = (_silu(gate) * up).astype(BF16)
    y = jnp.dot(h, wd_ref[...], preferred_element_type=F32)
    o_ref[...] = _layer_norm(alpha * x + 0.5 * y, g_ref[...], b_ref[...])


def _ffn_ln(xa, xb, wg, wu, wd, g, b, alpha):
    d = xa.shape[1]
    dff = wg.shape[1]
    tm = TOKEN_TILE
    na, nb = xa.shape[0] // tm, xb.shape[0] // tm
    tok_a, tok_b = _group_blocks(tm, na)
    return pl.pallas_call(
        _by_group(functools.partial(_ffn_ln_kernel, alpha=alpha), na, 5, 1),
        out_shape=(jax.ShapeDtypeStruct(xa.shape, F32), jax.ShapeDtypeStruct(xb.shape, F32)),
        grid=(na + nb,),
        in_specs=[tok_a(d), tok_b(d),
                  _resident((d, dff)), _resident((d, dff)), _resident((dff, d)),
                  _resident((1, d)), _resident((1, d))],
        out_specs=(tok_a(d), tok_b(d)),
        compiler_params=_cparams("arbitrary"),
        name="ffn_ln",
    )(xa, xb, wg, wu, wd, g, b)


def _in_proj_kernel(x_ref, wa_ref, wr_ref, wt_ref, ba_ref, br_ref, bt_ref,
                    pa_ref, pr_ref, kat_ref, kbt_ref, kct_ref, gt_ref):
    xb = x_ref[...].astype(BF16)
    pa_ref[...] = jnp.dot(xb, wa_ref[...], preferred_element_type=F32) + ba_ref[...]
    pr_ref[...] = jnp.dot(xb, wr_ref[...], preferred_element_type=F32) + br_ref[...]
    tr = _dot_nt(wt_ref[...], xb) + bt_ref[...]
    kat_ref[...] = tr[0:W_GROUP].astype(BF16)
    kbt_ref[...] = tr[W_GROUP:2 * W_GROUP]
    kct_ref[...] = tr[2 * W_GROUP:3 * W_GROUP]
    gt_ref[...] = tr[3 * W_GROUP:3 * W_GROUP + N_GATES]


def _in_proj(x, wa, wr, wt, ba, br, bt, casts=()):
    n, d = x.shape
    ca, cr, ct = wa.shape[1], wr.shape[1], wt.shape[0]
    tm = TOKEN_TILE
    tok = lambda w: pl.BlockSpec((tm, w), lambda i: (i, 0))
    trn = lambda r: pl.BlockSpec((r, tm), lambda i: (0, i))
    plans = [_cast_plan(stack, layer, n // tm) for stack, layer in casts]
    out = pl.pallas_call(
        _with_casts(_in_proj_kernel, 7, 6, len(casts)),
        out_shape=(jax.ShapeDtypeStruct((n, ca), F32), jax.ShapeDtypeStruct((n, cr), F32),
                   jax.ShapeDtypeStruct((W_GROUP, n), BF16), jax.ShapeDtypeStruct((W_GROUP, n), F32),
                   jax.ShapeDtypeStruct((W_GROUP, n), F32), jax.ShapeDtypeStruct((N_GATES, n), F32),
                   *[p[2] for p in plans]),
        grid=(n // tm,),
        in_specs=[tok(d), _resident((d, ca)), _resident((d, cr)), _resident((ct, d)),
                  _resident((1, ca)), _resident((1, cr)), _resident((ct, 1)), *[p[0] for p in plans]],
        out_specs=(tok(ca), tok(cr), trn(W_GROUP), trn(W_GROUP), trn(W_GROUP), trn(N_GATES), *[p[1] for p in plans]),
        compiler_params=_cparams("parallel"),
        name="in_proj",
    )(x, wa, wr, wt, ba, br, bt, *[stack for stack, _ in casts])
    return out[:6], out[6:]


def _head_rows(xt, h, pos):
    xh = xt[h * HEAD_DIM:(h + 1) * HEAD_DIM]
    z = jnp.zeros_like(xh)
    return jnp.concatenate([xh, z] if pos == 0 else [z, xh], axis=0)


def _col_bcast(eye_b, rows):
    hi = rows.astype(BF16)
    lo = (rows - hi.astype(F32)).astype(BF16)
    dn = (((1,), (1,)), ((), ()))
    return (lax.dot_general(eye_b, hi, dn, preferred_element_type=F32)
            + lax.dot_general(eye_b, lo, dn, preferred_element_type=F32))


def _rep_heads(rows4, n):
    return jnp.concatenate([jnp.broadcast_to(rows4[h:h + 1], (n, rows4.shape[1])) for h in range(N_HEADS)], axis=0)


def _seg_scan(x, seg_pos, seg_len, op, fill):
    s = 1
    while s < seg_len:
        x = op(x, jnp.where(seg_pos >= s, pltpu.roll(x, s, 1), fill))
        s *= 2
    return x


def _shift_rows(x, s, fill):
    if s % 8 == 0:
        return jnp.concatenate([jnp.full((s, x.shape[1]), fill, x.dtype), x[:x.shape[0] - s]], axis=0)
    row = lax.broadcasted_iota(jnp.int32, x.shape, 0)
    return jnp.where(row >= s, pltpu.roll(x, s, 0), fill)


def _mixer_kernel(pa_ref, pr_ref, kat_ref, kbt_ref, kct_ref, gt_ref, pastk_ref, pastv_ref, rel_ref,
                  cos_ref, sin_ref, cost_ref, sint_ref, dec_ref, qdec_ref, kdect_ref, sdec_ref,
                  ret0_ref, c0_ref, n0_ref, m0_ref,
                  cw_ref, cb_ref, wa_ref, ba_ref, wi_ref, bi_ref, lam_ref, buf0_ref, h0_ref,
                  y_ref, ret_out, c_out, n_out, m_out, h_out, buf_out,
                  bias_sc, spair, ppair, m_sc, win, h_sc, kpadt=None, vpad=None,
                  *, blk, seqs, n_steps, has_past, first_of_stack):
    state_outs = (ret_out, c_out, n_out, m_out, h_out, buf_out)
    if first_of_stack is not None:
        ret_out, c_out, n_out, m_out, h_out, buf_out = [ref.at[first_of_stack[0]] for ref in state_outs]
    L = blk
    band = A_REACH + L
    n_past = A_REACH // L if not has_past else 0
    n_var = n_past + 1
    bb = pl.program_id(0)
    tb = pl.program_id(1)
    tail = CONV_W - 1
    if n_steps == 1:
        at_first = at_last = lambda f: f()
    else:
        at_first = pl.when(tb == 0)
        at_last = pl.when(tb == n_steps - 1)

    li = lax.broadcasted_iota(jnp.int32, (L, L), 0)
    lj = lax.broadcasted_iota(jnp.int32, (L, L), 1)
    causal_neg = jnp.where(lj <= li, 0.0, -jnp.inf)
    eye_b = jnp.where(li == lj, 1.0, 0.0).astype(BF16)
    lane_lo = lax.broadcasted_iota(jnp.int32, (L, SLAB), 1) < HEAD_DIM
    ones_slab = jnp.ones((L, SLAB), BF16)
    ri = lax.broadcasted_iota(jnp.int32, (SLAB, SLAB), 0) // HEAD_DIM
    rj = lax.broadcasted_iota(jnp.int32, (SLAB, SLAB), 1) // HEAD_DIM
    smask = jnp.where(ri == rj, 1.0, 0.0)
    pmask = jnp.concatenate([smask, smask], axis=1)
    gi = lax.broadcasted_iota(jnp.int32, (W_GROUP, W_GROUP), 0) // HEAD_DIM
    gj = lax.broadcasted_iota(jnp.int32, (W_GROUP, W_GROUP), 1) // HEAD_DIM
    gmat = jnp.where(gi == gj, 1.0 / HEAD_DIM, 0.0).astype(BF16)
    e64i = lax.broadcasted_iota(jnp.int32, (HEAD_DIM, HEAD_DIM), 0)
    e64j = lax.broadcasted_iota(jnp.int32, (HEAD_DIM, HEAD_DIM), 1)
    eye64 = e64i == e64j
    row_lo = lax.broadcasted_iota(jnp.int32, (SLAB, 1), 0) < HEAD_DIM

    @pl.when((bb == 0) & (tb == 0))
    def _():
        qq = lax.broadcasted_iota(jnp.int32, (L, band), 0)
        kk = lax.broadcasted_iota(jnp.int32, (L, band), 1)
        off = kk - (qq // CHUNK) * CHUNK
        for h in range(N_HEADS):
            row = jnp.broadcast_to(rel_ref[h], (L, ROLL_W))
            tile = pltpu.roll(row, ROLL_W - L, 1, stride=1, stride_axis=0)[:, :band] * LOG2E
            tile = jnp.where(off >= 0, jnp.where(off < A_REACH + CHUNK, tile, -jnp.inf), -jnp.inf)
            for v in range(n_var):
                first_col = (n_past - v) * L
                bias_sc[v * N_HEADS + h] = jnp.where(kk >= first_col, tile, -jnp.inf) if first_col > 0 else tile

    for s in range(seqs):
        @at_first
        def _():
            for p in range(N_PAIRS):
                sblk, cblk = [], []
                for hh in range(2):
                    h = 2 * p + hh
                    z = jnp.zeros((HEAD_DIM, HEAD_DIM), F32)
                    s_h = ret0_ref[s, h]
                    c_h = c0_ref[s, h]
                    n_row = n0_ref[s, h:h + 1, :]
                    n_col = jnp.sum(jnp.where(eye64, jnp.broadcast_to(n_row, eye64.shape), 0.0),
                                    axis=1, keepdims=True)
                    n_rep = jnp.broadcast_to(n_col, (HEAD_DIM, HEAD_DIM))
                    sblk.append(jnp.concatenate([s_h, z] if hh == 0 else [z, s_h], axis=1))
                    cblk.append(jnp.concatenate([c_h, z, n_rep, z] if hh == 0 else [z, c_h, z, n_rep], axis=1))
                spair[s, p] = jnp.concatenate(sblk, axis=0)
                ppair[s, p] = jnp.concatenate(cblk, axis=0)
            m_sc[s, 0:N_HEADS, :] = jnp.broadcast_to(m0_ref[s], (N_HEADS, m_sc.shape[2]))
            win[s, CONV_PAD - tail:CONV_PAD, :] = buf0_ref[s]
            h_sc[s] = h0_ref[s]
            if not has_past:
                vpad[s, 0:A_REACH, :] = jnp.zeros((A_REACH, W_GROUP), BF16)
                kpadt[s, 0:n_past * W_GROUP, :] = jnp.zeros((n_past * W_GROUP, L), BF16)

    g8 = gt_ref[...]
    seg_pos = lax.broadcasted_iota(jnp.int32, g8.shape, 1) % L
    b8 = _seg_scan(jax.nn.log_sigmoid(g8), seg_pos, L, jnp.add, 0.0)
    a8 = g8 - pltpu.roll(b8, N_HEADS, 0)
    cm8 = _seg_scan(a8, seg_pos, L, jnp.maximum, -jnp.inf)
    b8 = pltpu.roll(b8, N_HEADS, 0)

    heads = [(h // 2, h % 2, h) for h in range(N_HEADS)]
    slab = lambda p: slice(p * SLAB, (p + 1) * SLAB)
    seg_of = lambda s: slice(s * L, (s + 1) * L)
    col = lambda s, j: pr_ref[s, :, j * W_GROUP:(j + 1) * W_GROUP]

    def halves(per_head):
        return jnp.concatenate([jnp.where(lane_lo, per_head[2 * p], per_head[2 * p + 1]) for p in range(N_PAIRS)],
                               axis=1)


    def attention_chain(s):
        pa = pa_ref[s]
        kat = kat_ref[:, seg_of(s)]
        v_new = pa[:, 2 * W_GROUP:].astype(BF16)
        if has_past:
            kpast = pastk_ref[s].astype(BF16)
            vband = jnp.concatenate([pastv_ref[s].astype(BF16), v_new], axis=0)
            bias_at = lambda h: bias_sc[h]
        else:
            kpadt[s, pl.ds(pl.multiple_of((tb + n_past) * W_GROUP, W_GROUP), W_GROUP), :] = kat
            kwin = kpadt[s, pl.ds(pl.multiple_of(tb * W_GROUP, W_GROUP), (n_past + 1) * W_GROUP), :]
            start = pl.multiple_of(tb * L, L)
            vpad[s, pl.ds(A_REACH + start, L), :] = v_new
            vband = vpad[s, pl.ds(start, band), :]
            var = jnp.minimum(tb, n_var - 1) * N_HEADS
            bias_at = lambda h: bias_sc[var + h]
        qa = (pa[:, 0:W_GROUP] * (HEAD_DIM ** -0.5 * LOG2E)).astype(BF16)
        sc = {}
        for p, hh, h in heads:
            q_slab = qa[:, slab(p)]
            if has_past:
                k_slab = kpast[:, slab(p)]
                lane = lax.broadcasted_iota(jnp.int32, k_slab.shape, 1)
                lane_h = lane < HEAD_DIM if hh == 0 else lane >= HEAD_DIM
                parts = [_dot_nt(q_slab, jnp.where(lane_h, k_slab, jnp.zeros_like(k_slab))),
                         jnp.dot(q_slab, _head_rows(kat, h, hh), preferred_element_type=F32)]
            else:
                parts = [jnp.dot(q_slab, _head_rows(kwin[i * W_GROUP:(i + 1) * W_GROUP], h, hh),
                                 preferred_element_type=F32) for i in range(n_past + 1)]
            sc[h] = jnp.concatenate(parts, axis=1)
        yield
        pe, den = {}, {}
        for _, _, h in heads:
            sh = sc[h] + bias_at(h)
            pe[h] = jnp.exp2(sh - jnp.max(sh, axis=1, keepdims=True))
            den[h] = jnp.sum(pe[h], axis=1, keepdims=True)
        yield
        res = {h: _dot(pe[h], vband)[:, slab(p)] for p, _, h in heads}
        yield
        y_ref[s, :, 0:W_GROUP] = halves({h: res[h] / den[h] for _, _, h in heads})

    def retention_chain(s):
        qr = (col(s, PR_QB) * cos_ref[...] + col(s, PR_QBS) * sin_ref[...]).astype(BF16)
        kbt = kbt_ref[:, seg_of(s)]
        half = HEAD_DIM // 2
        kbt_sw = jnp.concatenate([kbt[h * HEAD_DIM + o:h * HEAD_DIM + o + half]
                                  for h in range(N_HEADS) for o in (half, 0)], axis=0)
        krt = (kbt * cost_ref[:, seg_of(s)] + kbt_sw * sint_ref[:, seg_of(s)]) * HEAD_DIM ** -0.5
        krt_b = krt.astype(BF16)
        kdt_b = (krt * kdect_ref[...]).astype(BF16)
        vb = col(s, PR_VB).astype(BF16)
        raw = {h: jnp.dot(qr[:, slab(p)], _head_rows(krt_b, h, hh), preferred_element_type=F32)
               for p, hh, h in heads}
        s0 = {p: spair[s, p] for p in range(N_PAIRS)}
        inter = {p: _dot(qr[:, slab(p)], s0[p]) for p in range(N_PAIRS)}
        add = {p: _dot(kdt_b[slab(p)], vb[:, slab(p)]) for p in range(N_PAIRS)}
        yield
        scores = {h: raw[h] * dec_ref[h] for _, _, h in heads}
        for p in range(N_PAIRS):
            spair[s, p] = s0[p] * sdec_ref[p] + add[p] * smask
        yield
        o = {h: _dot(scores[h], vb[:, slab(p)]) for p, _, h in heads}
        yield
        ob = halves(o) + jnp.concatenate([inter[p] for p in range(N_PAIRS)], axis=1) * qdec_ref[...]
        mu = _dot(ob, gmat)
        yield
        oc = ob - mu
        var = _dot(oc * oc, gmat)
        yield
        y_ref[s, :, W_GROUP:2 * W_GROUP] = oc * lax.rsqrt(var + LN_EPS) * _silu(col(s, PR_GB))

    def mlstm_chain(s):
        a4 = a8[0:N_HEADS, seg_of(s)]
        b4 = b8[0:N_HEADS, seg_of(s)]
        m0c = m_sc[s, 0:N_HEADS, 0:1]
        big_m = jnp.maximum(m0c, cm8[0:N_HEADS, seg_of(s)])
        m4 = b4 + big_m
        w0_4 = jnp.exp(m0c - big_m)
        m_last = big_m[:, L - 1:L]
        wrow4 = jnp.exp(a4 - m_last)
        kct = kct_ref[:, seg_of(s)] * HEAD_DIM ** -0.5
        kct_b = kct.astype(BF16)
        kw_b = (kct * _rep_heads(wrow4, HEAD_DIM)).astype(BF16)
        qc = col(s, PR_QC).astype(BF16)
        vc = col(s, PR_VC).astype(BF16)
        v_aug = {p: jnp.concatenate([vc[:, slab(p)], ones_slab], axis=1) for p in range(N_PAIRS)}
        p0 = {p: ppair[s, p] for p in range(N_PAIRS)}
        w0_bc = _col_bcast(eye_b, _rep_heads(w0_4, HEAD_DIM))
        floor_bc = _col_bcast(eye_b, _rep_heads(jnp.exp(-m4), HEAD_DIM))
        m_bc = {h: _col_bcast(eye_b, jnp.broadcast_to(big_m[h:h + 1], (L, L))) for _, _, h in heads}
        raw = {h: jnp.dot(qc[:, slab(p)], _head_rows(kct_b, h, hh), preferred_element_type=F32)
               for p, hh, h in heads}
        inter = {p: _dot(qc[:, slab(p)], p0[p]) for p in range(N_PAIRS)}
        add = {p: _dot(kw_b[slab(p)], v_aug[p]) for p in range(N_PAIRS)}
        yield
        qk = {h: raw[h] * jnp.exp(a4[h:h + 1] - m_bc[h] + causal_neg) for _, _, h in heads}
        for p in range(N_PAIRS):
            w_state = jnp.where(row_lo, w0_4[2 * p:2 * p + 1, L - 1:L], w0_4[2 * p + 1:2 * p + 2, L - 1:L])
            ppair[s, p] = p0[p] * w_state + add[p] * pmask
        m_sc[s, 0:N_HEADS, :] = jnp.broadcast_to(m4[:, L - 1:L], (N_HEADS, m_sc.shape[2]))
        yield
        res = {h: _dot(qk[h], v_aug[p]) for p, _, h in heads}
        yield
        hc = []
        for p in range(N_PAIRS):
            w0 = w0_bc[:, slab(p)]
            num = jnp.where(lane_lo, res[2 * p][:, 0:SLAB], res[2 * p + 1][:, 0:SLAB]) + inter[p][:, 0:SLAB] * w0
            den = jnp.where(lane_lo, res[2 * p][:, SLAB:], res[2 * p + 1][:, SLAB:]) + inter[p][:, SLAB:] * w0
            hc.append(num / jnp.maximum(jnp.abs(den), floor_bc[:, slab(p)]))
        hc = jnp.concatenate(hc, axis=1)
        mu = _dot(hc, gmat)
        yield
        hcc = hc - mu
        var = _dot(hcc * hcc, gmat)
        yield
        y_ref[s, :, 2 * W_GROUP:3 * W_GROUP] = hcc * lax.rsqrt(var + LN_EPS) * jax.nn.sigmoid(col(s, PR_OC))

    def rglru_chain(s):
        xd = col(s, PR_XD)
        win[s, CONV_PAD:CONV_PAD + L, :] = xd
        xc = cb_ref[...]
        for j in range(CONV_W):
            off = CONV_PAD - tail + j
            xc = xc + win[s, off:off + L, :] * cw_ref[j:j + 1, :]
        win[s, CONV_PAD - tail:CONV_PAD, :] = xd[L - tail:, :]
        r_pre = _dot(xc, wa_ref[...])
        i_pre = _dot(xc, wi_ref[...])
        yield
        r = jax.nn.sigmoid(r_pre + ba_ref[...])
        i = jax.nn.sigmoid(i_pre + bi_ref[...])
        neg_lam = -lam_ref[...]
        softplus = jnp.maximum(neg_lam, 0.0) + jnp.log1p(jnp.exp(-jnp.abs(neg_lam)))
        log_a = -LRU_C * r * softplus
        a = jnp.exp(log_a)
        u = jnp.sqrt(-jnp.tanh(log_a) * (jnp.exp(2.0 * log_a) + 1.0)) * (i * xc)
        st = 1
        while st < L:
            u = a * _shift_rows(u, st, 0.0) + u
            a = a * _shift_rows(a, st, 1.0)
            st *= 2
            if st in (8, 64):
                yield
        hseq = u + a * h_sc[s]
        h_sc[s] = hseq[L - 1:L, :]
        y_ref[s, :, 3 * W_GROUP:] = hseq * jax.nn.gelu(col(s, PR_YD))

    if seqs == 1:
        order, lag = (attention_chain, rglru_chain, mlstm_chain, retention_chain), 2
    else:
        order, lag = (attention_chain, retention_chain, mlstm_chain, rglru_chain), 0
    _run_interleaved([chain(s) for chain in order for s in range(seqs)], lag=lag)

    for s in range(seqs):
        @at_last
        def _():
            for p in range(N_PAIRS):
                sp = spair[s, p]
                pp = ppair[s, p]
                for hh in range(2):
                    h = 2 * p + hh
                    rs = slice(hh * HEAD_DIM, (hh + 1) * HEAD_DIM)
                    ret_out[s, h] = sp[rs, rs]
                    c_out[s, h] = pp[rs, rs]
                    n_rep = pp[rs, SLAB + hh * HEAD_DIM:SLAB + (hh + 1) * HEAD_DIM]
                    n_out[s, h:h + 1, :] = jnp.sum(jnp.where(eye64, n_rep, 0.0), axis=0, keepdims=True)
            m_out[s] = m_sc[s, 0:N_HEADS, 0:1]
            h_out[s] = h_sc[s]
            buf_out[s] = pr_ref[s, L - tail:, PR_XD * W_GROUP:(PR_XD + 1) * W_GROUP]
            if first_of_stack is not None:
                for other in first_of_stack[1]:
                    for ref in state_outs:
                        ref[other, s] = jnp.zeros(ref.shape[2:], F32)


def _mixer_kernel_fresh(pa_ref, pr_ref, kat_ref, kbt_ref, kct_ref, gt_ref, *rest, **static):
    _mixer_kernel(pa_ref, pr_ref, kat_ref, kbt_ref, kct_ref, gt_ref, None, None, *rest, **static)


def _mixer_geometry(t, has_past):
    if has_past:
        assert t == CHUNK, "sequences with a carried cache are expected to be a single chunk"
        return t, MIX_SEQS
    assert t % MIX_BLOCK == 0 and A_REACH % MIX_BLOCK == 0
    return MIX_BLOCK, 1


def _mixer_tables(pos, blk, seqs):
    cos, sin = _rotary_tables(pos)
    return (cos, sin, jnp.tile(cos.T, (1, seqs)), jnp.tile(sin.T, (1, seqs))) + _retention_tables(blk)


def _mixer(pa, pr, kat, kbt, kct, gt, past, rel_rows, tables, rec, lw, n_layers, prev_states=None):
    ret0, c0, n0, m0, h0, buf0, rec_layer = rec
    m0, h0 = m0[..., None], h0[:, :, None, :]
    bsz, t, _ = pa.shape
    has_past = past is not None
    blk, seqs = _mixer_geometry(t, has_past)
    nb = t // blk
    assert bsz % seqs == 0
    band = A_REACH + blk
    n_var = 1 if has_past else A_REACH // blk + 1
    tail = CONV_W - 1

    tok3 = lambda w: pl.BlockSpec((seqs, blk, w), lambda b, c: (b, c, 0))
    trn = lambda r: pl.BlockSpec((r, seqs * blk), lambda b, c: (0, b * nb + c))
    per_seq = lambda *shape: pl.BlockSpec((seqs,) + shape, lambda b, c: (b,) + (0,) * len(shape))
    const = lambda *shape: pl.BlockSpec(shape, lambda b, c: (0,) * len(shape))
    carried = lambda *shape: pl.BlockSpec((None, seqs) + shape, lambda b, c: (rec_layer, b) + (0,) * len(shape))
    past_specs, past_args = [], ()
    if has_past:
        layer = past[2]
        cache = pl.BlockSpec((None, seqs, A_REACH, W_GROUP), lambda b, c: (layer, b, 0, 0))
        past_specs, past_args = [cache, cache], past[:2]
    in_specs = [tok3(3 * W_GROUP), tok3(N_PR * W_GROUP), trn(W_GROUP), trn(W_GROUP), trn(W_GROUP), trn(N_GATES),
                *past_specs, const(N_HEADS, 1, ROLL_W),
                pl.BlockSpec((blk, W_GROUP), lambda b, c: (c, 0)), pl.BlockSpec((blk, W_GROUP), lambda b, c: (c, 0)),
                pl.BlockSpec((W_GROUP, seqs * blk), lambda b, c: (0, c)),
                pl.BlockSpec((W_GROUP, seqs * blk), lambda b, c: (0, c)),
                const(N_HEADS, blk, blk), const(blk, W_GROUP), const(W_GROUP, blk), const(N_PAIRS, SLAB, SLAB),
                carried(N_HEADS, HEAD_DIM, HEAD_DIM), carried(N_HEADS, HEAD_DIM, HEAD_DIM),
                carried(N_HEADS, HEAD_DIM), carried(N_HEADS, 1),
                const(CONV_W, W_GROUP), const(1, W_GROUP), const(W_GROUP, W_GROUP), const(1, W_GROUP),
                const(W_GROUP, W_GROUP), const(1, W_GROUP), const(1, W_GROUP),
                carried(tail, W_GROUP), carried(1, W_GROUP)]
    out_layer = lw['layer']
    state_shapes = ((N_HEADS, HEAD_DIM, HEAD_DIM), (N_HEADS, HEAD_DIM, HEAD_DIM), (N_HEADS, HEAD_DIM), (N_HEADS, 1),
                    (1, W_GROUP), (tail, W_GROUP))
    if prev_states is None:
        first_of_stack = (out_layer, tuple(l for l in range(n_layers) if l != out_layer))
        produced = lambda *shape: pl.BlockSpec((n_layers, seqs) + shape, lambda b, c: (0, b) + (0,) * len(shape))
    else:
        first_of_stack = None
        produced = lambda *shape: pl.BlockSpec((None, seqs) + shape, lambda b, c: (out_layer, b) + (0,) * len(shape))
    out_shape = (jax.ShapeDtypeStruct((bsz, t, 4 * W_GROUP), F32),
                 *[jax.ShapeDtypeStruct((n_layers, bsz) + sh, F32) for sh in state_shapes])
    out_specs = (tok3(4 * W_GROUP), *[produced(*sh) for sh in state_shapes])
    n_in = len(in_specs)
    aliases = {}
    if prev_states is not None:
        in_specs += [pl.BlockSpec(memory_space=pl.ANY)] * len(state_shapes)
        aliases = {n_in + k: 1 + k for k in range(len(state_shapes))}
    scratch = [pltpu.VMEM((n_var * N_HEADS, blk, band), F32),
               pltpu.VMEM((seqs, N_PAIRS, SLAB, SLAB), F32), pltpu.VMEM((seqs, N_PAIRS, SLAB, 2 * SLAB), F32),
               pltpu.VMEM((seqs, 8, 128), F32), pltpu.VMEM((seqs, CONV_PAD + blk, W_GROUP), F32),
               pltpu.VMEM((seqs, 1, W_GROUP), F32)]
    if not has_past:
        scratch += [pltpu.VMEM((seqs, (nb + A_REACH // blk) * W_GROUP, blk), BF16),
                    pltpu.VMEM((seqs, A_REACH + t, W_GROUP), BF16)]
    body = functools.partial(_mixer_kernel if has_past else _mixer_kernel_fresh,
                             blk=blk, seqs=seqs, n_steps=nb, has_past=has_past, first_of_stack=first_of_stack)
    if prev_states is not None:
        inner, n_alias = body, len(state_shapes)
        body = lambda *refs: inner(*refs[:n_in], *refs[n_in + n_alias:])
    return pl.pallas_call(
        body,
        out_shape=out_shape,
        grid=(bsz // seqs, nb),
        in_specs=in_specs,
        out_specs=out_specs,
        scratch_shapes=scratch,
        input_output_aliases=aliases,
        compiler_params=_cparams("arbitrary", "arbitrary"),
        name="mixer",
    )(pa, pr, kat, kbt, kct, gt, *past_args, rel_rows, *tables,
      ret0, c0, n0, m0, lw['conv_w'], lw['conv_b'], lw['lru_wa'], lw['lru_ba'], lw['lru_wi'], lw['lru_bi'],
      lw['lru_lambda'], buf0, h0, *(prev_states or ()))


def _post_kernel(x_ref, y_ref, wo_ref, g1_ref, b1_ref, wq_ref, mk_ref, mv_ref, xo_ref,
                 g2_ref, b2_ref, o_ref, att_sc, *, alpha, seqs, subtiles):
    rows_sub = x_ref.shape[0] // subtiles
    rows = rows_sub // seqs
    head = lambda h: slice(h * HEAD_DIM, (h + 1) * HEAD_DIM)

    def subtile_chain(i):
        rs = slice(i * rows_sub, (i + 1) * rows_sub)
        x = x_ref[rs, :]
        mix = _dot(y_ref[rs, :], wo_ref[...])
        yield
        x2 = _layer_norm(alpha * x + mix, g1_ref[...], b1_ref[...])
        yield
        q = _dot(x2, wq_ref[...]).astype(BF16)
        yield

        def attend(s, h):
            qs = slice(s * rows, (s + 1) * rows)
            sc = _dot_nt(q[qs, head(h)], mk_ref[s, :, head(h)]) * HEAD_DIM ** -0.5
            yield
            p = jnp.exp(sc - jnp.max(sc, axis=1, keepdims=True))
            den = jnp.sum(p, axis=1, keepdims=True)
            yield
            o = _dot(p, mv_ref[s, :, head(h)])
            yield
            att_sc[i * rows_sub + s * rows:i * rows_sub + (s + 1) * rows, head(h)] = o / den

        yield from _lockstep([attend(s, h) for s in range(seqs) for h in range(N_HEADS)])
        out = _dot(att_sc[rs, :], xo_ref[...])
        yield
        o_ref[rs, :] = _layer_norm(alpha * x2 + out, g2_ref[...], b2_ref[...])

    _run_interleaved([subtile_chain(i) for i in range(subtiles)], lag=2)


def _post(x, y, wo, wq, xo, wlayer, g1, b1, g2, b2, mk, mv, layer, alpha, seq_len, casts=()):
    n, d = x.shape
    n_mem, dx = mk.shape[2], mk.shape[3]
    if seq_len >= POST_SUBTILES * TOKEN_TILE:
        subtiles, seqs = POST_SUBTILES, 1
        tm = subtiles * TOKEN_TILE
        mem = pl.BlockSpec((None, 1, n_mem, dx), lambda i: (layer, i // (seq_len // tm), 0, 0))
    else:
        tm = TOKEN_TILE
        subtiles, seqs = 1, tm // seq_len
        mem = pl.BlockSpec((None, seqs, n_mem, dx), lambda i: (layer, i, 0, 0))
    assert n % tm == 0 and tm % (subtiles * seqs) == 0
    tok = lambda w: pl.BlockSpec((tm, w), lambda i: (i, 0))
    plans = [_cast_plan(stack, lyr, n // tm) for stack, lyr in casts]
    out = pl.pallas_call(
        _with_casts(functools.partial(_post_kernel, alpha=alpha, seqs=seqs, subtiles=subtiles), 11, 1, len(casts)),
        out_shape=(jax.ShapeDtypeStruct((n, d), F32), *[p[2] for p in plans]),
        grid=(n // tm,),
        in_specs=[tok(d), tok(y.shape[1]),
                  _resident(wo.shape[1:], wlayer), _resident((1, d)), _resident((1, d)),
                  _resident(wq.shape[1:], wlayer), mem, mem, _resident(xo.shape[1:], wlayer),
                  _resident((1, d)), _resident((1, d)), *[p[0] for p in plans]],
        out_specs=(tok(d), *[p[1] for p in plans]),
        scratch_shapes=[pltpu.VMEM((tm, dx), F32)],
        compiler_params=_cparams("parallel"),
        name="post",
    )(x, y, wo, g1, b1, wq, mk, mv, xo, g2, b2, *[stack for stack, _ in casts])
    return out[0], out[1:]


def _mem_kv_kernel(x_ref, w_ref, o_ref):
    o_ref[...] = _dot(x_ref[...], w_ref[...])


def _mem_kv(mem, w):
    n, d = mem.shape
    tm = min(TOKEN_TILE, n)
    return pl.pallas_call(
        _mem_kv_kernel,
        out_shape=jax.ShapeDtypeStruct((n, w.shape[1]), F32),
        grid=(n // tm,),
        in_specs=[pl.BlockSpec((tm, d), lambda i: (i, 0)), _resident(w.shape)],
        out_specs=pl.BlockSpec((tm, w.shape[1]), lambda i: (i, 0)),
        compiler_params=_cparams("parallel"),
        name="mem_kv",
    )(mem, w)


def _rotary_tables(pos):
    half = HEAD_DIM // 2
    inv = jnp.exp(-jnp.log(10000.0) * jnp.arange(half, dtype=F32) / half)
    ang = pos.astype(F32)[:, None] * inv[None, :]
    cos, sin = jnp.cos(ang), jnp.sin(ang)
    cos_full = jnp.tile(jnp.concatenate([cos, cos], -1), (1, N_HEADS))
    sin_full = jnp.tile(jnp.concatenate([-sin, sin], -1), (1, N_HEADS))
    return cos_full, sin_full


def _retention_tables(blk):
    log_g = jnp.log1p(-jnp.exp2(-5.0 - jnp.arange(N_HEADS, dtype=F32)))
    idx = jnp.arange(blk, dtype=F32)
    diff = idx[:, None] - idx[None, :]
    dec = jnp.exp(jnp.where((diff >= 0)[None], diff[None] * log_g[:, None, None], -jnp.inf))
    q_dec = jnp.exp((idx[:, None] + 1.0) * log_g[None, :])
    k_dec = jnp.exp((blk - 1.0 - idx)[:, None] * log_g[None, :])
    s_dec = jnp.exp(blk * log_g)
    lanes = lambda tbl: jnp.repeat(tbl, HEAD_DIM, axis=1)
    pair = lambda p: jnp.kron(jnp.diag(s_dec[2 * p:2 * p + 2]), jnp.ones((HEAD_DIM, HEAD_DIM), F32))
    return dec, lanes(q_dec), lanes(k_dec).T, jnp.stack([pair(p) for p in range(N_PAIRS)])


def _rel_bias_rows(rel_bias, blk):
    idx = np.clip(A_REACH + blk - np.arange(ROLL_W), -REL_CLIP, REL_CLIP) + REL_CLIP
    return rel_bias[:, idx][:, None, :]


def _block_diag(w):
    h, c, _ = w.shape
    eye = jnp.eye(h, dtype=w.dtype)
    return (eye[:, None, :, None] * w[:, :, None, :]).reshape(h * c, h * c)


def _swap_perm():
    j = np.arange(W_GROUP)
    return (j // HEAD_DIM) * HEAD_DIM + (j % HEAD_DIM + HEAD_DIM // 2) % HEAD_DIM


def _mix_and_post(x1, proj, shape, tables, mem, past, rec, lw, alpha, n_layers, prev_states, post_casts=()):
    bsz, t, d = shape
    n = bsz * t
    g, b = lw['ln_g'], lw['ln_b']
    pa, pr, kat, kbt, kct, gt = proj
    pa = pa.reshape(bsz, t, 3 * W_GROUP)
    pr = pr.reshape(bsz, t, N_PR * W_GROUP)
    blk, _ = _mixer_geometry(t, past is not None)
    y, *rec_stacks = _mixer(pa, pr, kat, kbt, kct, gt, past, _rel_bias_rows(lw['rel_bias'], blk), tables, rec, lw,
                            n_layers, prev_states)
    x3, post_cast_out = _post(x1, y.reshape(n, 4 * W_GROUP), lw['w_out'], lw['x_wq'], lw['x_wo'], lw['layer'],
                              g[1:2], b[1:2], g[2:3], b[2:3], *mem, alpha, t, post_casts)
    heads = lambda a: a.reshape(bsz, -1, N_HEADS, HEAD_DIM)
    keep = min(A_REACH, t)
    a_k = heads(pa[:, t - keep:, W_GROUP:2 * W_GROUP])
    a_v = heads(pa[:, t - keep:, 2 * W_GROUP:])
    return x3, (a_k, a_v), rec_stacks, post_cast_out


def _result_states(stacks):
    ret, c, n, m, h, conv = stacks
    return [ret, c, n, m[..., 0], h[:, :, 0, :], conv]


def kernel(x_prompt, x_sample, mem_prompt, cache_a_k, cache_a_v, state_ret, state_mlstm_c, state_mlstm_n,
           state_mlstm_m, state_lru_h, state_conv, cache_mem_k, cache_mem_v, ln_g, ln_b,
           ffn1_gate, ffn1_up, ffn1_down, ffn2_gate, ffn2_up, ffn2_down, w_in, b_in, a_rel_bias,
           conv_w, conv_b, lru_wa, lru_ba, lru_wi, lru_bi, lru_lambda, w_out, x_wq, x_wk, x_wv, x_wo):
    depth = ln_g.shape[0]
    alpha = (2.0 * depth) ** 0.25
    row = lambda v: v[None, :]
    grp = lambda a, j: a[..., j * W_GROUP:(j + 1) * W_GROUP]
    perm = _swap_perm()
    QA, KA, VA, QB, KB, VB, GB, QC, KC, VC, OC, XD, YD = range(13)

    def layer_weights(l):
        w, bias = w_in[l], b_in[l]
        nat = lambda a: jnp.concatenate(
            [grp(a, QB), grp(a, QB)[..., perm], grp(a, VB), grp(a, GB), grp(a, QC), grp(a, VC), grp(a, OC),
             grp(a, XD), grp(a, YD)], axis=-1)
        trn = lambda a: jnp.concatenate([grp(a, KA), grp(a, KB), grp(a, KC), a[..., 13 * W_GROUP:]], axis=-1)
        return {'ln_g': ln_g[l], 'ln_b': ln_b[l],
                'layer': l, **stacks,
                'w_in_a': w[:, :3 * W_GROUP].astype(BF16), 'w_in_r': nat(w).astype(BF16),
                'w_in_t': trn(w).T.astype(BF16),
                'b_in_a': row(bias[:3 * W_GROUP]), 'b_in_r': row(nat(bias)), 'b_in_t': trn(bias)[:, None],
                'rel_bias': a_rel_bias[l],
                'conv_w': conv_w[l], 'conv_b': row(conv_b[l]),
                'lru_wa': _block_diag(lru_wa[l]).astype(BF16), 'lru_ba': row(lru_ba[l]),
                'lru_wi': _block_diag(lru_wi[l]).astype(BF16), 'lru_bi': row(lru_bi[l]),
                'lru_lambda': row(lru_lambda[l]),
                'x_wkv': jnp.concatenate([x_wk[l], x_wv[l]], axis=1).astype(BF16)}

    stacks = {name: w.astype(BF16) for name, w in (('w_out', w_out), ('x_wq', x_wq), ('x_wo', x_wo))}
    ffn1_f32, ffn2_f32 = (ffn1_gate, ffn1_up, ffn1_down), (ffn2_gate, ffn2_up, ffn2_down)
    ffn1_w = {0: tuple(w[0].astype(BF16) for w in ffn1_f32)}
    weights = [layer_weights(l) for l in range(depth)]
    dx = x_wk.shape[2]

    bp, tp, d = x_prompt.shape
    bs, ts, _ = x_sample.shape
    n_mem = mem_prompt.shape[1]
    tables_p = _mixer_tables(jnp.arange(tp), *_mixer_geometry(tp, False))
    rec0 = (jnp.zeros((1, bp, N_HEADS, HEAD_DIM, HEAD_DIM), F32), jnp.zeros((1, bp, N_HEADS, HEAD_DIM, HEAD_DIM), F32),
            jnp.zeros((1, bp, N_HEADS, HEAD_DIM), F32), jnp.zeros((1, bp, N_HEADS), F32),
            jnp.zeros((1, bp, W_GROUP), F32), jnp.zeros((1, bp, CONV_W - 1, W_GROUP), F32), 0)
    tables_s = _mixer_tables(PAST_LEN + jnp.arange(ts), *_mixer_geometry(ts, True))
    past_k_all = cache_a_k.reshape(depth, bs, A_REACH, W_GROUP)
    past_v_all = cache_a_v.reshape(depth, bs, A_REACH, W_GROUP)
    mem_k_all = cache_mem_k.reshape(depth, bs, n_mem, dx)
    mem_v_all = cache_mem_v.reshape(depth, bs, n_mem, dx)

    hp, hs = x_prompt.reshape(bp * tp, d), x_sample.reshape(bs * ts, d)
    prompt_states, sample_states, rec_p, rec_s = [], [], None, None
    for l in range(depth):
        lw = weights[l]
        g, b = lw['ln_g'], lw['ln_b']
        mkv = _mem_kv(mem_prompt.reshape(bp * n_mem, d), lw['x_wkv'])
        mk = mkv[:, :dx].reshape(bp, n_mem, dx)
        mv = mkv[:, dx:].reshape(bp, n_mem, dx)
        x1p, x1s = _ffn_ln(hp, hs, *ffn1_w[l], g[0:1], b[0:1], alpha)
        w_in = (lw['w_in_a'], lw['w_in_r'], lw['w_in_t'], lw['b_in_a'], lw['b_in_r'], lw['b_in_t'])
        proj_p, ffn2 = _in_proj(x1p, *w_in, [(w, l) for w in ffn2_f32])
        proj_s, _ = _in_proj(x1s, *w_in)
        post_casts = [(w, l + 1) for w in ffn1_f32] if l + 1 < depth else []
        x3p, st_p, rec_p, cast_next = _mix_and_post(x1p, proj_p, (bp, tp, d), tables_p, (mk[None], mv[None], 0), None,
                                                    rec0, lw, alpha, depth, rec_p, post_casts)
        rec = (state_ret, state_mlstm_c, state_mlstm_n, state_mlstm_m, state_lru_h, state_conv, l)
        x3s, st_s, rec_s, _ = _mix_and_post(x1s, proj_s, (bs, ts, d), tables_s, (mem_k_all, mem_v_all, l),
                                            (past_k_all, past_v_all, l), rec, lw, alpha, depth, rec_s)
        if post_casts:
            ffn1_w[l + 1] = cast_next
        hp, hs = _ffn_ln(x3p, x3s, *ffn2, g[3:4], b[3:4], alpha)
        prompt_states.append(st_p + (mk.reshape(bp, n_mem, -1, HEAD_DIM), mv.reshape(bp, n_mem, -1, HEAD_DIM)))
        sample_states.append(st_s)
    y_prompt, y_sample = hp.reshape(bp, tp, d), hs.reshape(bs, ts, d)
    p_ak, p_av, p_mk, p_mv = [jnp.stack(f) for f in zip(*prompt_states)]
    prompt_out = [p_ak, p_av, *_result_states(rec_p), p_mk, p_mv]
    sample_out = [jnp.stack(f) for f in zip(*sample_states)] + _result_states(rec_s)

    return (y_prompt, y_sample, *prompt_out, *sample_out)
```

```python
import functools

import numpy as np
import jax
import jax.numpy as jnp
from jax import lax
from jax.experimental import pallas as pl
from jax.experimental.pallas import tpu as pltpu

F32 = jnp.float32
BF16 = jnp.bfloat16

CHUNK = 64
HEAD_DIM = 64
N_HEADS = 4
W_GROUP = N_HEADS * HEAD_DIM
SLAB = 2 * HEAD_DIM
N_PAIRS = N_HEADS // 2
A_BAND_CHUNKS = 8
A_REACH = A_BAND_CHUNKS * CHUNK
REL_CLIP = 128
CONV_W = 4
LRU_C = 8.0
LN_EPS = 1e-5
LOG2E = 1.4426950408889634
N_GATES = 2 * N_HEADS
PAST_LEN = 4096
ROLL_W = 1024
CONV_PAD = 8

V7X_VMEM_LIMIT_BYTES = 56 * 1024 * 1024
TOKEN_TILE = 512
MIX_BLOCK = 256
MIX_SEQS = 4
POST_SUBTILES = 2

PR_QB, PR_QBS, PR_VB, PR_GB, PR_QC, PR_VC, PR_OC, PR_XD, PR_YD = range(9)
N_PR = 9


def _cparams(*sem):
    return pltpu.CompilerParams(dimension_semantics=sem, vmem_limit_bytes=V7X_VMEM_LIMIT_BYTES)


def _dot(a, b):
    return jnp.dot(a.astype(BF16), b.astype(BF16), preferred_element_type=F32)


def _dot_nt(a, b):
    return lax.dot_general(a.astype(BF16), b.astype(BF16), (((1,), (1,)), ((), ())), preferred_element_type=F32)


def _layer_norm(x, g, b):
    mu = jnp.mean(x, -1, keepdims=True)
    xc = x - mu
    var = jnp.mean(xc * xc, -1, keepdims=True)
    return xc * lax.rsqrt(var + LN_EPS) * g + b


def _silu(x):
    return x * jax.nn.sigmoid(x)


def _lockstep(chains):
    while chains:
        chains = [c for c in chains if next(c, "done") != "done"]
        yield


def _run_interleaved(chains, lag=0):
    pending, live, rnd = list(chains), [], 0
    while pending or live:
        while pending and rnd >= lag * (len(chains) - len(pending)):
            live.append(pending.pop(0))
        live = [c for c in live if next(c, "done") != "done"]
        rnd += 1


def _resident(shape, layer=None):
    nd = len(shape)
    if layer is None:
        return pl.BlockSpec(shape, lambda *_: (0,) * nd, pipeline_mode=pl.Buffered(1))
    return pl.BlockSpec((None,) + tuple(shape), lambda *_: (layer,) + (0,) * nd, pipeline_mode=pl.Buffered(1))


def _cast_plan(stack, layer, n_steps):
    _, r, c = stack.shape
    nb = next(k for k in range(n_steps, 0, -1) if n_steps % k == 0 and r % k == 0 and (r // k) % 16 == 0)
    every = n_steps // nb
    return (pl.BlockSpec((None, r // nb, c), lambda i: (layer, i // every, 0)),
            pl.BlockSpec((r // nb, c), lambda i: (i // every, 0)),
            jax.ShapeDtypeStruct((r, c), BF16))


def _with_casts(body, n_in, n_out, n_cast):
    def wrapped(*refs):
        ins, rest = refs[:n_in + n_cast], refs[n_in + n_cast:]
        outs, scratch = rest[:n_out + n_cast], rest[n_out + n_cast:]
        body(*ins[:n_in], *outs[:n_out], *scratch)
        for src, dst in zip(ins[n_in:], outs[n_out:]):
            dst[...] = src[...].astype(BF16)
    return wrapped


def _by_group(body, n_first, n_shared, n_out):
    def wrapped(xa_ref, xb_ref, *refs):
        shared, outs = refs[:n_shared], refs[n_shared:]
        i = pl.program_id(0)

        @pl.when(i < n_first)
        def _():
            body(xa_ref, *shared, *outs[:n_out])

        @pl.when(i >= n_first)
        def _():
            body(xb_ref, *shared, *outs[n_out:2 * n_out])
    return wrapped


def _group_blocks(tm, n_first):
    tok_a = lambda w: pl.BlockSpec((tm, w), lambda i: (jnp.minimum(i, n_first - 1), 0))
    tok_b = lambda w: pl.BlockSpec((tm, w), lambda i: (jnp.maximum(i - n_first, 0), 0))
    return tok_a, tok_b


def _ffn_ln_kernel(x_ref, wg_ref, wu_ref, wd_ref, g_ref, b_ref, o_ref, *, alpha):
    x = x_ref[...]
    xb = x.astype(BF16)
    gate = jnp.dot(xb, wg_ref[...], preferred_element_type=F32)
    up = jnp.dot(xb, wu_ref[...], preferred_element_type=F32)
    h = (_silu(gate) * up).astype(BF16)
    y = jnp.dot(h, wd_ref[...], preferred_element_type=F32)
    o_ref[...] = _layer_norm(alpha * x + 0.5 * y, g_ref[...], b_ref[...])


def _ffn_ln(xa, xb, wg, wu, wd, g, b, alpha):
    d = xa.shape[1]
    dff = wg.shape[1]
    tm = TOKEN_TILE
    na, nb = xa.shape[0] // tm, xb.shape[0] // tm
    tok_a, tok_b = _group_blocks(tm, na)
    return pl.pallas_call(
        _by_group(functools.partial(_ffn_ln_kernel, alpha=alpha), na, 5, 1),
        out_shape=(jax.ShapeDtypeStruct(xa.shape, F32), jax.ShapeDtypeStruct(xb.shape, F32)),
        grid=(na + nb,),
        in_specs=[tok_a(d), tok_b(d),
                  _resident((d, dff)), _resident((d, dff)), _resident((dff, d)),
                  _resident((1, d)), _resident((1, d))],
        out_specs=(tok_a(d), tok_b(d)),
        compiler_params=_cparams("arbitrary"),
        name="ffn_ln",
    )(xa, xb, wg, wu, wd, g, b)


def _in_proj_kernel(x_ref, wa_ref, wr_ref, wt_ref, ba_ref, br_ref, bt_ref,
                    pa_ref, pr_ref, kat_ref, kbt_ref, kct_ref, gt_ref, ak_ref, av_ref, *, first_of_stack):
    xb = x_ref[...].astype(BF16)
    pa = jnp.dot(xb, wa_ref[...], preferred_element_type=F32) + ba_ref[...]
    pa_ref[...] = pa
    for ref, col in ((ak_ref, W_GROUP), (av_ref, 2 * W_GROUP)):
        if first_of_stack is not None:
            for other in range(ref.shape[0]):
                if other != first_of_stack:
                    ref[other] = jnp.zeros(ref.shape[1:], F32)
            ref = ref.at[first_of_stack]
        ref[...] = pa[:, col:col + W_GROUP]
    pr_ref[...] = jnp.dot(xb, wr_ref[...], preferred_element_type=F32) + br_ref[...]
    tr = _dot_nt(wt_ref[...], xb) + bt_ref[...]
    kat_ref[...] = tr[0:W_GROUP].astype(BF16)
    kbt_ref[...] = tr[W_GROUP:2 * W_GROUP]
    kct_ref[...] = tr[2 * W_GROUP:3 * W_GROUP]
    gt_ref[...] = tr[3 * W_GROUP:3 * W_GROUP + N_GATES]


def _in_proj(x, wa, wr, wt, ba, br, bt, seq_len, layer, n_layers, prev_kv=None, casts=()):
    n, d = x.shape
    ca, cr, ct = wa.shape[1], wr.shape[1], wt.shape[0]
    tm = TOKEN_TILE
    per = max(seq_len // tm, 1)
    assert (seq_len % tm == 0 and A_REACH == tm) or (tm % seq_len == 0 and seq_len <= A_REACH)
    tok = lambda w: pl.BlockSpec((tm, w), lambda i: (i, 0))
    trn = lambda r: pl.BlockSpec((r, tm), lambda i: (0, i))
    plans = [_cast_plan(stack, layer_, n // tm) for stack, layer_ in casts]
    in_specs = [tok(d), _resident((d, ca)), _resident((d, cr)), _resident((ct, d)),
                _resident((1, ca)), _resident((1, cr)), _resident((ct, 1)), *[p[0] for p in plans]]
    n_in = len(in_specs)
    if prev_kv is None:
        first_of_stack, aliases = layer, {}
        kept = pl.BlockSpec((n_layers, tm, W_GROUP), lambda i: (0, i // per, 0))
    else:
        first_of_stack, aliases = None, {n_in: 6, n_in + 1: 7}
        kept = pl.BlockSpec((None, tm, W_GROUP), lambda i: (layer, i // per, 0))
        in_specs += [pl.BlockSpec(memory_space=pl.ANY)] * 2
    inner = _with_casts(functools.partial(_in_proj_kernel, first_of_stack=first_of_stack), 7, 8, len(casts))
    body = lambda *refs: inner(*refs[:n_in], *refs[n_in + len(aliases):])
    kv_shape = jax.ShapeDtypeStruct((n_layers, n // per, W_GROUP), F32)
    out = pl.pallas_call(
        body,
        out_shape=(jax.ShapeDtypeStruct((n, ca), F32), jax.ShapeDtypeStruct((n, cr), F32),
                   jax.ShapeDtypeStruct((W_GROUP, n), BF16), jax.ShapeDtypeStruct((W_GROUP, n), F32),
                   jax.ShapeDtypeStruct((W_GROUP, n), F32), jax.ShapeDtypeStruct((N_GATES, n), F32),
                   kv_shape, kv_shape, *[p[2] for p in plans]),
        grid=(n // tm,),
        in_specs=in_specs,
        out_specs=(tok(ca), tok(cr), trn(W_GROUP), trn(W_GROUP), trn(W_GROUP), trn(N_GATES), kept, kept,
                   *[p[1] for p in plans]),
        input_output_aliases=aliases,
        compiler_params=_cparams("arbitrary"),
        name="in_proj",
    )(x, wa, wr, wt, ba, br, bt, *[stack for stack, _ in casts], *(prev_kv or ()))
    return out[:6], out[6:8], out[8:]


def _head_rows(xt, h, pos):
    xh = xt[h * HEAD_DIM:(h + 1) * HEAD_DIM]
    z = jnp.zeros_like(xh)
    return jnp.concatenate([xh, z] if pos == 0 else [z, xh], axis=0)


def _col_bcast(eye_b, rows):
    hi = rows.astype(BF16)
    lo = (rows - hi.astype(F32)).astype(BF16)
    dn = (((1,), (1,)), ((), ()))
    return (lax.dot_general(eye_b, hi, dn, preferred_element_type=F32)
            + lax.dot_general(eye_b, lo, dn, preferred_element_type=F32))


def _rep_heads(rows4, n):
    return jnp.concatenate([jnp.broadcast_to(rows4[h:h + 1], (n, rows4.shape[1])) for h in range(N_HEADS)], axis=0)


def _seg_scan(x, seg_pos, seg_len, op, fill):
    s = 1
    while s < seg_len:
        x = op(x, jnp.where(seg_pos >= s, pltpu.roll(x, s, 1), fill))
        s *= 2
    return x


def _shift_rows(x, s, fill):
    if s % 8 == 0:
        return jnp.concatenate([jnp.full((s, x.shape[1]), fill, x.dtype), x[:x.shape[0] - s]], axis=0)
    row = lax.broadcasted_iota(jnp.int32, x.shape, 0)
    return jnp.where(row >= s, pltpu.roll(x, s, 0), fill)


def _mixer_kernel(pa_ref, pr_ref, kat_ref, kbt_ref, kct_ref, gt_ref, pastk_ref, pastv_ref, rel_ref,
                  cos_ref, sin_ref, cost_ref, sint_ref, dec_ref, qdec_ref, kdect_ref, sdec_ref,
                  ret0_ref, c0_ref, n0_ref, m0_ref,
                  cw_ref, cb_ref, wa_ref, ba_ref, wi_ref, bi_ref, lam_ref, buf0_ref, h0_ref,
                  y_ref, ret_out, c_out, n_out, m_out, h_out, buf_out,
                  bias_sc, spair, ppair, m_sc, win, h_sc, kpadt=None, vpad=None,
                  *, blk, seqs, n_steps, has_past, first_of_stack):
    state_outs = (ret_out, c_out, n_out, m_out, h_out, buf_out)
    if first_of_stack is not None:
        ret_out, c_out, n_out, m_out, h_out, buf_out = [ref.at[first_of_stack[0]] for ref in state_outs]
    L = blk
    band = A_REACH + L
    n_past = A_REACH // L if not has_past else 0
    n_var = n_past + 1
    bb = pl.program_id(0)
    tb = pl.program_id(1)
    tail = CONV_W - 1
    if n_steps == 1:
        at_first = at_last = lambda f: f()
    else:
        at_first = pl.when(tb == 0)
        at_last = pl.when(tb == n_steps - 1)

    li = lax.broadcasted_iota(jnp.int32, (L, L), 0)
    lj = lax.broadcasted_iota(jnp.int32, (L, L), 1)
    causal_neg = jnp.where(lj <= li, 0.0, -jnp.inf)
    eye_b = jnp.where(li == lj, 1.0, 0.0).astype(BF16)
    lane_lo = lax.broadcasted_iota(jnp.int32, (L, SLAB), 1) < HEAD_DIM
    ones_slab = jnp.ones((L, SLAB), BF16)
    ri = lax.broadcasted_iota(jnp.int32, (SLAB, SLAB), 0) // HEAD_DIM
    rj = lax.broadcasted_iota(jnp.int32, (SLAB, SLAB), 1) // HEAD_DIM
    smask = jnp.where(ri == rj, 1.0, 0.0)
    pmask = jnp.concatenate([smask, smask], axis=1)
    gi = lax.broadcasted_iota(jnp.int32, (W_GROUP, W_GROUP), 0) // HEAD_DIM
    gj = lax.broadcasted_iota(jnp.int32, (W_GROUP, W_GROUP), 1) // HEAD_DIM
    gmat = jnp.where(gi == gj, 1.0 / HEAD_DIM, 0.0).astype(BF16)
    e64i = lax.broadcasted_iota(jnp.int32, (HEAD_DIM, HEAD_DIM), 0)
    e64j = lax.broadcasted_iota(jnp.int32, (HEAD_DIM, HEAD_DIM), 1)
    eye64 = e64i == e64j
    row_lo = lax.broadcasted_iota(jnp.int32, (SLAB, 1), 0) < HEAD_DIM

    @pl.when((bb == 0) & (tb == 0))
    def _():
        qq = lax.broadcasted_iota(jnp.int32, (L, band), 0)
        kk = lax.broadcasted_iota(jnp.int32, (L, band), 1)
        off = kk - (qq // CHUNK) * CHUNK
        for h in range(N_HEADS):
            row = jnp.broadcast_to(rel_ref[h], (L, ROLL_W))
            tile = pltpu.roll(row, ROLL_W - L, 1, stride=1, stride_axis=0)[:, :band] * LOG2E
            tile = jnp.where(off >= 0, jnp.where(off < A_REACH + CHUNK, tile, -jnp.inf), -jnp.inf)
            for v in range(n_var):
                first_col = (n_past - v) * L
                bias_sc[v * N_HEADS + h] = jnp.where(kk >= first_col, tile, -jnp.inf) if first_col > 0 else tile

    for s in range(seqs):
        @at_first
        def _():
            for p in range(N_PAIRS):
                sblk, cblk = [], []
                for hh in range(2):
                    h = 2 * p + hh
                    z = jnp.zeros((HEAD_DIM, HEAD_DIM), F32)
                    s_h = ret0_ref[s, h]
                    c_h = c0_ref[s, h]
                    n_row = n0_ref[s, h:h + 1, :]
                    n_col = jnp.sum(jnp.where(eye64, jnp.broadcast_to(n_row, eye64.shape), 0.0),
                                    axis=1, keepdims=True)
                    n_rep = jnp.broadcast_to(n_col, (HEAD_DIM, HEAD_DIM))
                    sblk.append(jnp.concatenate([s_h, z] if hh == 0 else [z, s_h], axis=1))
                    cblk.append(jnp.concatenate([c_h, z, n_rep, z] if hh == 0 else [z, c_h, z, n_rep], axis=1))
                spair[s, p] = jnp.concatenate(sblk, axis=0)
                ppair[s, p] = jnp.concatenate(cblk, axis=0)
            m_sc[s, 0:N_HEADS, :] = jnp.broadcast_to(m0_ref[s], (N_HEADS, m_sc.shape[2]))
            win[s, CONV_PAD - tail:CONV_PAD, :] = buf0_ref[s]
            h_sc[s] = h0_ref[s]
            if not has_past:
                vpad[s, 0:A_REACH, :] = jnp.zeros((A_REACH, W_GROUP), BF16)
                kpadt[s, 0:n_past * W_GROUP, :] = jnp.zeros((n_past * W_GROUP, L), BF16)

    g8 = gt_ref[...]
    seg_pos = lax.broadcasted_iota(jnp.int32, g8.shape, 1) % L
    b8 = _seg_scan(jax.nn.log_sigmoid(g8), seg_pos, L, jnp.add, 0.0)
    a8 = g8 - pltpu.roll(b8, N_HEADS, 0)
    cm8 = _seg_scan(a8, seg_pos, L, jnp.maximum, -jnp.inf)
    b8 = pltpu.roll(b8, N_HEADS, 0)

    heads = [(h // 2, h % 2, h) for h in range(N_HEADS)]
    slab = lambda p: slice(p * SLAB, (p + 1) * SLAB)
    seg_of = lambda s: slice(s * L, (s + 1) * L)
    col = lambda s, j: pr_ref[s, :, j * W_GROUP:(j + 1) * W_GROUP]

    def halves(per_head):
        return jnp.concatenate([jnp.where(lane_lo, per_head[2 * p], per_head[2 * p + 1]) for p in range(N_PAIRS)],
                               axis=1)


    def attention_chain(s):
        pa = pa_ref[s]
        kat = kat_ref[:, seg_of(s)]
        v_new = pa[:, 2 * W_GROUP:].astype(BF16)
        if has_past:
            kpast = pastk_ref[s].astype(BF16)
            vband = jnp.concatenate([pastv_ref[s].astype(BF16), v_new], axis=0)
            bias_at = lambda h: bias_sc[h]
        else:
            kpadt[s, pl.ds(pl.multiple_of((tb + n_past) * W_GROUP, W_GROUP), W_GROUP), :] = kat
            kwin = kpadt[s, pl.ds(pl.multiple_of(tb * W_GROUP, W_GROUP), (n_past + 1) * W_GROUP), :]
            start = pl.multiple_of(tb * L, L)
            vpad[s, pl.ds(A_REACH + start, L), :] = v_new
            vband = vpad[s, pl.ds(start, band), :]
            var = jnp.minimum(tb, n_var - 1) * N_HEADS
            bias_at = lambda h: bias_sc[var + h]
        qa = (pa[:, 0:W_GROUP] * (HEAD_DIM ** -0.5 * LOG2E)).astype(BF16)
        sc = {}
        for p, hh, h in heads:
            q_slab = qa[:, slab(p)]
            if has_past:
                k_slab = kpast[:, slab(p)]
                lane = lax.broadcasted_iota(jnp.int32, k_slab.shape, 1)
                lane_h = lane < HEAD_DIM if hh == 0 else lane >= HEAD_DIM
                parts = [_dot_nt(q_slab, jnp.where(lane_h, k_slab, jnp.zeros_like(k_slab))),
                         jnp.dot(q_slab, _head_rows(kat, h, hh), preferred_element_type=F32)]
            else:
                parts = [jnp.dot(q_slab, _head_rows(kwin[i * W_GROUP:(i + 1) * W_GROUP], h, hh),
                                 preferred_element_type=F32) for i in range(n_past + 1)]
            sc[h] = jnp.concatenate(parts, axis=1)
        yield
        pe, den = {}, {}
        for _, _, h in heads:
            sh = sc[h] + bias_at(h)
            pe[h] = jnp.exp2(sh - jnp.max(sh, axis=1, keepdims=True))
            den[h] = jnp.sum(pe[h], axis=1, keepdims=True)
        yield
        res = {h: _dot(pe[h], vband)[:, slab(p)] for p, _, h in heads}
        yield
        y_ref[s, :, 0:W_GROUP] = halves({h: res[h] / den[h] for _, _, h in heads})

    def retention_chain(s):
        qr = (col(s, PR_QB) * cos_ref[...] + col(s, PR_QBS) * sin_ref[...]).astype(BF16)
        kbt = kbt_ref[:, seg_of(s)]
        half = HEAD_DIM // 2
        kbt_sw = jnp.concatenate([kbt[h * HEAD_DIM + o:h * HEAD_DIM + o + half]
                                  for h in range(N_HEADS) for o in (half, 0)], axis=0)
        krt = (kbt * cost_ref[:, seg_of(s)] + kbt_sw * sint_ref[:, seg_of(s)]) * HEAD_DIM ** -0.5
        krt_b = krt.astype(BF16)
        kdt_b = (krt * kdect_ref[...]).astype(BF16)
        vb = col(s, PR_VB).astype(BF16)
        raw = {h: jnp.dot(qr[:, slab(p)], _head_rows(krt_b, h, hh), preferred_element_type=F32)
               for p, hh, h in heads}
        s0 = {p: spair[s, p] for p in range(N_PAIRS)}
        inter = {p: _dot(qr[:, slab(p)], s0[p]) for p in range(N_PAIRS)}
        add = {p: _dot(kdt_b[slab(p)], vb[:, slab(p)]) for p in range(N_PAIRS)}
        yield
        scores = {h: raw[h] * dec_ref[h] for _, _, h in heads}
        for p in range(N_PAIRS):
            spair[s, p] = s0[p] * sdec_ref[p] + add[p] * smask
        yield
        o = {h: _dot(scores[h], vb[:, slab(p)]) for p, _, h in heads}
        yield
        ob = halves(o) + jnp.concatenate([inter[p] for p in range(N_PAIRS)], axis=1) * qdec_ref[...]
        mu = _dot(ob, gmat)
        yield
        oc = ob - mu
        var = _dot(oc * oc, gmat)
        yield
        y_ref[s, :, W_GROUP:2 * W_GROUP] = oc * lax.rsqrt(var + LN_EPS) * _silu(col(s, PR_GB))

    def mlstm_chain(s):
        a4 = a8[0:N_HEADS, seg_of(s)]
        b4 = b8[0:N_HEADS, seg_of(s)]
        m0c = m_sc[s, 0:N_HEADS, 0:1]
        big_m = jnp.maximum(m0c, cm8[0:N_HEADS, seg_of(s)])
        m4 = b4 + big_m
        w0_4 = jnp.exp(m0c - big_m)
        m_last = big_m[:, L - 1:L]
        wrow4 = jnp.exp(a4 - m_last)
        kct = kct_ref[:, seg_of(s)] * HEAD_DIM ** -0.5
        kct_b = kct.astype(BF16)
        kw_b = (kct * _rep_heads(wrow4, HEAD_DIM)).astype(BF16)
        qc = col(s, PR_QC).astype(BF16)
        vc = col(s, PR_VC).astype(BF16)
        v_aug = {p: jnp.concatenate([vc[:, slab(p)], ones_slab], axis=1) for p in range(N_PAIRS)}
        p0 = {p: ppair[s, p] for p in range(N_PAIRS)}
        w0_bc = _col_bcast(eye_b, _rep_heads(w0_4, HEAD_DIM))
        floor_bc = _col_bcast(eye_b, _rep_heads(jnp.exp(-m4), HEAD_DIM))
        m_bc = {h: _col_bcast(eye_b, jnp.broadcast_to(big_m[h:h + 1], (L, L))) for _, _, h in heads}
        raw = {h: jnp.dot(qc[:, slab(p)], _head_rows(kct_b, h, hh), preferred_element_type=F32)
               for p, hh, h in heads}
        inter = {p: _dot(qc[:, slab(p)], p0[p]) for p in range(N_PAIRS)}
        add = {p: _dot(kw_b[slab(p)], v_aug[p]) for p in range(N_PAIRS)}
        yield
        qk = {h: raw[h] * jnp.exp(a4[h:h + 1] - m_bc[h] + causal_neg) for _, _, h in heads}
        for p in range(N_PAIRS):
            w_state = jnp.where(row_lo, w0_4[2 * p:2 * p + 1, L - 1:L], w0_4[2 * p + 1:2 * p + 2, L - 1:L])
            ppair[s, p] = p0[p] * w_state + add[p] * pmask
        m_sc[s, 0:N_HEADS, :] = jnp.broadcast_to(m4[:, L - 1:L], (N_HEADS, m_sc.shape[2]))
        yield
        res = {h: _dot(qk[h], v_aug[p]) for p, _, h in heads}
        yield
        hc = []
        for p in range(N_PAIRS):
            w0 = w0_bc[:, slab(p)]
            num = jnp.where(lane_lo, res[2 * p][:, 0:SLAB], res[2 * p + 1][:, 0:SLAB]) + inter[p][:, 0:SLAB] * w0
            den = jnp.where(lane_lo, res[2 * p][:, SLAB:], res[2 * p + 1][:, SLAB:]) + inter[p][:, SLAB:] * w0
            hc.append(num / jnp.maximum(jnp.abs(den), floor_bc[:, slab(p)]))
        hc = jnp.concatenate(hc, axis=1)
        mu = _dot(hc, gmat)
        yield
        hcc = hc - mu
        var = _dot(hcc * hcc, gmat)
        yield
        y_ref[s, :, 2 * W_GROUP:3 * W_GROUP] = hcc * lax.rsqrt(var + LN_EPS) * jax.nn.sigmoid(col(s, PR_OC))

    def rglru_chain(s):
        xd = col(s, PR_XD)
        win[s, CONV_PAD:CONV_PAD + L, :] = xd
        xc = cb_ref[...]
        for j in range(CONV_W):
            off = CONV_PAD - tail + j
            xc = xc + win[s, off:off + L, :] * cw_ref[j:j + 1, :]
        win[s, CONV_PAD - tail:CONV_PAD, :] = xd[L - tail:, :]
        r_pre = _dot(xc, wa_ref[...])
        i_pre = _dot(xc, wi_ref[...])
        yield
        r = jax.nn.sigmoid(r_pre + ba_ref[...])
        i = jax.nn.sigmoid(i_pre + bi_ref[...])
        neg_lam = -lam_ref[...]
        softplus = jnp.maximum(neg_lam, 0.0) + jnp.log1p(jnp.exp(-jnp.abs(neg_lam)))
        log_a = -LRU_C * r * softplus
        a = jnp.exp(log_a)
        u = jnp.sqrt(-jnp.tanh(log_a) * (jnp.exp(2.0 * log_a) + 1.0)) * (i * xc)
        st = 1
        while st < L:
            u = a * _shift_rows(u, st, 0.0) + u
            a = a * _shift_rows(a, st, 1.0)
            st *= 2
            if st in (8, 64):
                yield
        hseq = u + a * h_sc[s]
        h_sc[s] = hseq[L - 1:L, :]
        y_ref[s, :, 3 * W_GROUP:] = hseq * jax.nn.gelu(col(s, PR_YD))

    if seqs == 1:
        order, lag = (attention_chain, rglru_chain, mlstm_chain, retention_chain), 2
    else:
        order, lag = (attention_chain, retention_chain, mlstm_chain, rglru_chain), 0
    _run_interleaved([chain(s) for chain in order for s in range(seqs)], lag=lag)

    for s in range(seqs):
        @at_last
        def _():
            for p in range(N_PAIRS):
                sp = spair[s, p]
                pp = ppair[s, p]
                for hh in range(2):
                    h = 2 * p + hh
                    rs = slice(hh * HEAD_DIM, (hh + 1) * HEAD_DIM)
                    ret_out[s, h] = sp[rs, rs]
                    c_out[s, h] = pp[rs, rs]
                    n_rep = pp[rs, SLAB + hh * HEAD_DIM:SLAB + (hh + 1) * HEAD_DIM]
                    n_out[s, h:h + 1, :] = jnp.sum(jnp.where(eye64, n_rep, 0.0), axis=0, keepdims=True)
            m_out[s] = m_sc[s, 0:N_HEADS, 0:1]
            h_out[s] = h_sc[s]
            buf_out[s] = pr_ref[s, L - tail:, PR_XD * W_GROUP:(PR_XD + 1) * W_GROUP]
            if first_of_stack is not None:
                for other in first_of_stack[1]:
                    for ref in state_outs:
                        ref[other, s] = jnp.zeros(ref.shape[2:], F32)


def _mixer_kernel_fresh(pa_ref, pr_ref, kat_ref, kbt_ref, kct_ref, gt_ref, *rest, **static):
    _mixer_kernel(pa_ref, pr_ref, kat_ref, kbt_ref, kct_ref, gt_ref, None, None, *rest, **static)


def _mixer_geometry(t, has_past):
    if has_past:
        assert t == CHUNK, "sequences with a carried cache are expected to be a single chunk"
        return t, MIX_SEQS
    assert t % MIX_BLOCK == 0 and A_REACH % MIX_BLOCK == 0
    return MIX_BLOCK, 1


def _mixer_tables(pos, blk, seqs):
    cos, sin = _rotary_tables(pos)
    return (cos, sin, jnp.tile(cos.T, (1, seqs)), jnp.tile(sin.T, (1, seqs))) + _retention_tables(blk)


def _mixer(pa, pr, kat, kbt, kct, gt, past, rel_rows, tables, rec, lw, n_layers, prev_states=None):
    ret0, c0, n0, m0, h0, buf0, rec_layer = rec
    m0, h0 = m0[..., None], h0[:, :, None, :]
    bsz, t, _ = pa.shape
    has_past = past is not None
    blk, seqs = _mixer_geometry(t, has_past)
    nb = t // blk
    assert bsz % seqs == 0
    band = A_REACH + blk
    n_var = 1 if has_past else A_REACH // blk + 1
    tail = CONV_W - 1

    tok3 = lambda w: pl.BlockSpec((seqs, blk, w), lambda b, c: (b, c, 0))
    trn = lambda r: pl.BlockSpec((r, seqs * blk), lambda b, c: (0, b * nb + c))
    per_seq = lambda *shape: pl.BlockSpec((seqs,) + shape, lambda b, c: (b,) + (0,) * len(shape))
    const = lambda *shape: pl.BlockSpec(shape, lambda b, c: (0,) * len(shape))
    carried = lambda *shape: pl.BlockSpec((None, seqs) + shape, lambda b, c: (rec_layer, b) + (0,) * len(shape))
    past_specs, past_args = [], ()
    if has_past:
        layer = past[2]
        cache = pl.BlockSpec((None, seqs, A_REACH, W_GROUP), lambda b, c: (layer, b, 0, 0))
        past_specs, past_args = [cache, cache], past[:2]
    in_specs = [tok3(3 * W_GROUP), tok3(N_PR * W_GROUP), trn(W_GROUP), trn(W_GROUP), trn(W_GROUP), trn(N_GATES),
                *past_specs, const(N_HEADS, 1, ROLL_W),
                pl.BlockSpec((blk, W_GROUP), lambda b, c: (c, 0)), pl.BlockSpec((blk, W_GROUP), lambda b, c: (c, 0)),
                pl.BlockSpec((W_GROUP, seqs * blk), lambda b, c: (0, c)),
                pl.BlockSpec((W_GROUP, seqs * blk), lambda b, c: (0, c)),
                const(N_HEADS, blk, blk), const(blk, W_GROUP), const(W_GROUP, blk), const(N_PAIRS, SLAB, SLAB),
                carried(N_HEADS, HEAD_DIM, HEAD_DIM), carried(N_HEADS, HEAD_DIM, HEAD_DIM),
                carried(N_HEADS, HEAD_DIM), carried(N_HEADS, 1),
                const(CONV_W, W_GROUP), const(1, W_GROUP), const(W_GROUP, W_GROUP), const(1, W_GROUP),
                const(W_GROUP, W_GROUP), const(1, W_GROUP), const(1, W_GROUP),
                carried(tail, W_GROUP), carried(1, W_GROUP)]
    out_layer = lw['layer']
    state_shapes = ((N_HEADS, HEAD_DIM, HEAD_DIM), (N_HEADS, HEAD_DIM, HEAD_DIM), (N_HEADS, HEAD_DIM), (N_HEADS, 1),
                    (1, W_GROUP), (tail, W_GROUP))
    if prev_states is None:
        first_of_stack = (out_layer, tuple(l for l in range(n_layers) if l != out_layer))
        produced = lambda *shape: pl.BlockSpec((n_layers, seqs) + shape, lambda b, c: (0, b) + (0,) * len(shape))
    else:
        first_of_stack = None
        produced = lambda *shape: pl.BlockSpec((None, seqs) + shape, lambda b, c: (out_layer, b) + (0,) * len(shape))
    out_shape = (jax.ShapeDtypeStruct((bsz, t, 4 * W_GROUP), F32),
                 *[jax.ShapeDtypeStruct((n_layers, bsz) + sh, F32) for sh in state_shapes])
    out_specs = (tok3(4 * W_GROUP), *[produced(*sh) for sh in state_shapes])
    n_in = len(in_specs)
    aliases = {}
    if prev_states is not None:
        in_specs += [pl.BlockSpec(memory_space=pl.ANY)] * len(state_shapes)
        aliases = {n_in + k: 1 + k for k in range(len(state_shapes))}
    scratch = [pltpu.VMEM((n_var * N_HEADS, blk, band), F32),
               pltpu.VMEM((seqs, N_PAIRS, SLAB, SLAB), F32), pltpu.VMEM((seqs, N_PAIRS, SLAB, 2 * SLAB), F32),
               pltpu.VMEM((seqs, 8, 128), F32), pltpu.VMEM((seqs, CONV_PAD + blk, W_GROUP), F32),
               pltpu.VMEM((seqs, 1, W_GROUP), F32)]
    if not has_past:
        scratch += [pltpu.VMEM((seqs, (nb + A_REACH // blk) * W_GROUP, blk), BF16),
                    pltpu.VMEM((seqs, A_REACH + t, W_GROUP), BF16)]
    body = functools.partial(_mixer_kernel if has_past else _mixer_kernel_fresh,
                             blk=blk, seqs=seqs, n_steps=nb, has_past=has_past, first_of_stack=first_of_stack)
    if prev_states is not None:
        inner, n_alias = body, len(state_shapes)
        body = lambda *refs: inner(*refs[:n_in], *refs[n_in + n_alias:])
    return pl.pallas_call(
        body,
        out_shape=out_shape,
        grid=(bsz // seqs, nb),
        in_specs=in_specs,
        out_specs=out_specs,
        scratch_shapes=scratch,
        input_output_aliases=aliases,
        compiler_params=_cparams("arbitrary", "arbitrary"),
        name="mixer",
    )(pa, pr, kat, kbt, kct, gt, *past_args, rel_rows, *tables,
      ret0, c0, n0, m0, lw['conv_w'], lw['conv_b'], lw['lru_wa'], lw['lru_ba'], lw['lru_wi'], lw['lru_bi'],
      lw['lru_lambda'], buf0, h0, *(prev_states or ()))


def _post_kernel(x_ref, y_ref, wo_ref, g1_ref, b1_ref, wq_ref, mk_ref, mv_ref, xo_ref,
                 g2_ref, b2_ref, o_ref, att_sc, *, alpha, seqs, subtiles):
    rows_sub = x_ref.shape[0] // subtiles
    rows = rows_sub // seqs
    head = lambda h: slice(h * HEAD_DIM, (h + 1) * HEAD_DIM)

    def subtile_chain(i):
        rs = slice(i * rows_sub, (i + 1) * rows_sub)
        x = x_ref[rs, :]
        mix = _dot(y_ref[rs, :], wo_ref[...])
        yield
        x2 = _layer_norm(alpha * x + mix, g1_ref[...], b1_ref[...])
        yield
        q = _dot(x2, wq_ref[...]).astype(BF16)
        yield

        def attend(s, h):
            qs = slice(s * rows, (s + 1) * rows)
            sc = _dot_nt(q[qs, head(h)], mk_ref[s, :, head(h)]) * HEAD_DIM ** -0.5
            yield
            p = jnp.exp(sc - jnp.max(sc, axis=1, keepdims=True))
            den = jnp.sum(p, axis=1, keepdims=True)
            yield
            o = _dot(p, mv_ref[s, :, head(h)])
            yield
            att_sc[i * rows_sub + s * rows:i * rows_sub + (s + 1) * rows, head(h)] = o / den

        yield from _lockstep([attend(s, h) for s in range(seqs) for h in range(N_HEADS)])
        out = _dot(att_sc[rs, :], xo_ref[...])
        yield
        o_ref[rs, :] = _layer_norm(alpha * x2 + out, g2_ref[...], b2_ref[...])

    _run_interleaved([subtile_chain(i) for i in range(subtiles)], lag=2)


def _post(x, y, wo, wq, xo, wlayer, g1, b1, g2, b2, mk, mv, layer, alpha, seq_len, casts=()):
    n, d = x.shape
    n_mem, dx = mk.shape[2], mk.shape[3]
    if seq_len >= POST_SUBTILES * TOKEN_TILE:
        subtiles, seqs = POST_SUBTILES, 1
        tm = subtiles * TOKEN_TILE
        mem = pl.BlockSpec((None, 1, n_mem, dx), lambda i: (layer, i // (seq_len // tm), 0, 0))
    else:
        tm = TOKEN_TILE
        subtiles, seqs = 1, tm // seq_len
        mem = pl.BlockSpec((None, seqs, n_mem, dx), lambda i: (layer, i, 0, 0))
    assert n % tm == 0 and tm % (subtiles * seqs) == 0
    tok = lambda w: pl.BlockSpec((tm, w), lambda i: (i, 0))
    plans = [_cast_plan(stack, lyr, n // tm) for stack, lyr in casts]
    out = pl.pallas_call(
        _with_casts(functools.partial(_post_kernel, alpha=alpha, seqs=seqs, subtiles=subtiles), 11, 1, len(casts)),
        out_shape=(jax.ShapeDtypeStruct((n, d), F32), *[p[2] for p in plans]),
        grid=(n // tm,),
        in_specs=[tok(d), tok(y.shape[1]),
                  _resident(wo.shape[1:], wlayer), _resident((1, d)), _resident((1, d)),
                  _resident(wq.shape[1:], wlayer), mem, mem, _resident(xo.shape[1:], wlayer),
                  _resident((1, d)), _resident((1, d)), *[p[0] for p in plans]],
        out_specs=(tok(d), *[p[1] for p in plans]),
        scratch_shapes=[pltpu.VMEM((tm, dx), F32)],
        compiler_params=_cparams("parallel"),
        name="post",
    )(x, y, wo, g1, b1, wq, mk, mv, xo, g2, b2, *[stack for stack, _ in casts])
    return out[0], out[1:]


def _mem_kv_kernel(x_ref, w_ref, o_ref):
    o_ref[...] = _dot(x_ref[...], w_ref[...])


def _mem_kv(mem, w):
    n, d = mem.shape
    tm = min(TOKEN_TILE, n)
    return pl.pallas_call(
        _mem_kv_kernel,
        out_shape=jax.ShapeDtypeStruct((n, w.shape[1]), F32),
        grid=(n // tm,),
        in_specs=[pl.BlockSpec((tm, d), lambda i: (i, 0)), _resident(w.shape)],
        out_specs=pl.BlockSpec((tm, w.shape[1]), lambda i: (i, 0)),
        compiler_params=_cparams("parallel"),
        name="mem_kv",
    )(mem, w)


def _rotary_tables(pos):
    half = HEAD_DIM // 2
    inv = jnp.exp(-jnp.log(10000.0) * jnp.arange(half, dtype=F32) / half)
    ang = pos.astype(F32)[:, None] * inv[None, :]
    cos, sin = jnp.cos(ang), jnp.sin(ang)
    cos_full = jnp.tile(jnp.concatenate([cos, cos], -1), (1, N_HEADS))
    sin_full = jnp.tile(jnp.concatenate([-sin, sin], -1), (1, N_HEADS))
    return cos_full, sin_full


def _retention_tables(blk):
    log_g = jnp.log1p(-jnp.exp2(-5.0 - jnp.arange(N_HEADS, dtype=F32)))
    idx = jnp.arange(blk, dtype=F32)
    diff = idx[:, None] - idx[None, :]
    dec = jnp.exp(jnp.where((diff >= 0)[None], diff[None] * log_g[:, None, None], -jnp.inf))
    q_dec = jnp.exp((idx[:, None] + 1.0) * log_g[None, :])
    k_dec = jnp.exp((blk - 1.0 - idx)[:, None] * log_g[None, :])
    s_dec = jnp.exp(blk * log_g)
    lanes = lambda tbl: jnp.repeat(tbl, HEAD_DIM, axis=1)
    pair = lambda p: jnp.kron(jnp.diag(s_dec[2 * p:2 * p + 2]), jnp.ones((HEAD_DIM, HEAD_DIM), F32))
    return dec, lanes(q_dec), lanes(k_dec).T, jnp.stack([pair(p) for p in range(N_PAIRS)])


def _rel_bias_rows(rel_bias, blk):
    idx = np.clip(A_REACH + blk - np.arange(ROLL_W), -REL_CLIP, REL_CLIP) + REL_CLIP
    h, n_rel = rel_bias.shape
    lead = A_REACH + blk - REL_CLIP
    rest = ROLL_W - lead - n_rel
    if lead >= 0 and rest >= 0:
        assert np.array_equal(idx, np.concatenate([np.full(lead, n_rel - 1), np.arange(n_rel)[::-1],
                                                   np.zeros(rest, np.int64)]))
        rows = jnp.concatenate([jnp.broadcast_to(rel_bias[:, n_rel - 1:], (h, lead)), rel_bias[:, ::-1],
                                jnp.broadcast_to(rel_bias[:, :1], (h, rest))], axis=1)
    else:
        rows = rel_bias[:, idx]
    return rows[:, None, :]


def _block_diag(w):
    h, c, _ = w.shape
    eye = jnp.eye(h, dtype=w.dtype)
    return (eye[:, None, :, None] * w[:, :, None, :]).reshape(h * c, h * c)


def _swap_halves(a):
    half = HEAD_DIM // 2
    return jnp.concatenate([a[..., h * HEAD_DIM + s:h * HEAD_DIM + s + half]
                            for h in range(N_HEADS) for s in (half, 0)], axis=-1)


def _mix_and_post(x1, proj, shape, tables, mem, past, rec, lw, alpha, n_layers, prev_states, post_casts=()):
    bsz, t, d = shape
    n = bsz * t
    g, b = lw['ln_g'], lw['ln_b']
    pa, pr, kat, kbt, kct, gt = proj
    pa = pa.reshape(bsz, t, 3 * W_GROUP)
    pr = pr.reshape(bsz, t, N_PR * W_GROUP)
    blk, _ = _mixer_geometry(t, past is not None)
    y, *rec_stacks = _mixer(pa, pr, kat, kbt, kct, gt, past, _rel_bias_rows(lw['rel_bias'], blk), tables, rec, lw,
                            n_layers, prev_states)
    x3, post_cast_out = _post(x1, y.reshape(n, 4 * W_GROUP), lw['w_out'], lw['x_wq'], lw['x_wo'], lw['layer'],
                              g[1:2], b[1:2], g[2:3], b[2:3], *mem, alpha, t, post_casts)
    return x3, rec_stacks, post_cast_out


def _result_states(stacks):
    ret, c, n, m, h, conv = stacks
    return [ret, c, n, m[..., 0], h[:, :, 0, :], conv]


def kernel(x_prompt, x_sample, mem_prompt, cache_a_k, cache_a_v, state_ret, state_mlstm_c, state_mlstm_n,
           state_mlstm_m, state_lru_h, state_conv, cache_mem_k, cache_mem_v, ln_g, ln_b,
           ffn1_gate, ffn1_up, ffn1_down, ffn2_gate, ffn2_up, ffn2_down, w_in, b_in, a_rel_bias,
           conv_w, conv_b, lru_wa, lru_ba, lru_wi, lru_bi, lru_lambda, w_out, x_wq, x_wk, x_wv, x_wo):
    depth = ln_g.shape[0]
    alpha = (2.0 * depth) ** 0.25
    row = lambda v: v[None, :]
    grp = lambda a, j: a[..., j * W_GROUP:(j + 1) * W_GROUP]
    QA, KA, VA, QB, KB, VB, GB, QC, KC, VC, OC, XD, YD = range(13)

    def layer_weights(l):
        w, bias = w_in[l], b_in[l]
        nat = lambda a: jnp.concatenate(
            [grp(a, QB), _swap_halves(grp(a, QB)), grp(a, VB), grp(a, GB), grp(a, QC), grp(a, VC), grp(a, OC),
             grp(a, XD), grp(a, YD)], axis=-1)
        trn = lambda a: jnp.concatenate([grp(a, KA), grp(a, KB), grp(a, KC), a[..., 13 * W_GROUP:]], axis=-1)
        return {'ln_g': ln_g[l], 'ln_b': ln_b[l],
                'layer': l, **stacks,
                'w_in_a': w[:, :3 * W_GROUP].astype(BF16), 'w_in_r': nat(w).astype(BF16),
                'w_in_t': trn(w).T.astype(BF16),
                'b_in_a': row(bias[:3 * W_GROUP]), 'b_in_r': row(nat(bias)), 'b_in_t': trn(bias)[:, None],
                'rel_bias': a_rel_bias[l],
                'conv_w': conv_w[l], 'conv_b': row(conv_b[l]),
                'lru_wa': _block_diag(lru_wa[l]).astype(BF16), 'lru_ba': row(lru_ba[l]),
                'lru_wi': _block_diag(lru_wi[l]).astype(BF16), 'lru_bi': row(lru_bi[l]),
                'lru_lambda': row(lru_lambda[l]),
                'x_wkv': jnp.concatenate([x_wk[l], x_wv[l]], axis=1).astype(BF16)}

    stacks = {name: w.astype(BF16) for name, w in (('w_out', w_out), ('x_wq', x_wq), ('x_wo', x_wo))}
    ffn1_f32, ffn2_f32 = (ffn1_gate, ffn1_up, ffn1_down), (ffn2_gate, ffn2_up, ffn2_down)
    ffn1_w = {0: tuple(w[0].astype(BF16) for w in ffn1_f32)}
    weights = [layer_weights(l) for l in range(depth)]
    dx = x_wk.shape[2]

    bp, tp, d = x_prompt.shape
    bs, ts, _ = x_sample.shape
    n_mem = mem_prompt.shape[1]
    tables_p = _mixer_tables(jnp.arange(tp), *_mixer_geometry(tp, False))
    rec0 = (jnp.zeros((1, bp, N_HEADS, HEAD_DIM, HEAD_DIM), F32), jnp.zeros((1, bp, N_HEADS, HEAD_DIM, HEAD_DIM), F32),
            jnp.zeros((1, bp, N_HEADS, HEAD_DIM), F32), jnp.zeros((1, bp, N_HEADS), F32),
            jnp.zeros((1, bp, W_GROUP), F32), jnp.zeros((1, bp, CONV_W - 1, W_GROUP), F32), 0)
    tables_s = _mixer_tables(PAST_LEN + jnp.arange(ts), *_mixer_geometry(ts, True))
    past_k_all = cache_a_k.reshape(depth, bs, A_REACH, W_GROUP)
    past_v_all = cache_a_v.reshape(depth, bs, A_REACH, W_GROUP)
    mem_k_all = cache_mem_k.reshape(depth, bs, n_mem, dx)
    mem_v_all = cache_mem_v.reshape(depth, bs, n_mem, dx)

    hp, hs = x_prompt.reshape(bp * tp, d), x_sample.reshape(bs * ts, d)
    mem_states, rec_p, rec_s, kv_p, kv_s = [], None, None, None, None
    for l in range(depth):
        lw = weights[l]
        g, b = lw['ln_g'], lw['ln_b']
        mkv = _mem_kv(mem_prompt.reshape(bp * n_mem, d), lw['x_wkv'])
        mk = mkv[:, :dx].reshape(bp, n_mem, dx)
        mv = mkv[:, dx:].reshape(bp, n_mem, dx)
        x1p, x1s = _ffn_ln(hp, hs, *ffn1_w[l], g[0:1], b[0:1], alpha)
        w_in = (lw['w_in_a'], lw['w_in_r'], lw['w_in_t'], lw['b_in_a'], lw['b_in_r'], lw['b_in_t'])
        proj_p, kv_p, ffn2 = _in_proj(x1p, *w_in, tp, l, depth, kv_p, [(w, l) for w in ffn2_f32])
        proj_s, kv_s, _ = _in_proj(x1s, *w_in, ts, l, depth, kv_s)
        post_casts = [(w, l + 1) for w in ffn1_f32] if l + 1 < depth else []
        x3p, rec_p, cast_next = _mix_and_post(x1p, proj_p, (bp, tp, d), tables_p, (mk[None], mv[None], 0), None,
                                              rec0, lw, alpha, depth, rec_p, post_casts)
        rec = (state_ret, state_mlstm_c, state_mlstm_n, state_mlstm_m, state_lru_h, state_conv, l)
        x3s, rec_s, _ = _mix_and_post(x1s, proj_s, (bs, ts, d), tables_s, (mem_k_all, mem_v_all, l),
                                      (past_k_all, past_v_all, l), rec, lw, alpha, depth, rec_s)
        if post_casts:
            ffn1_w[l + 1] = cast_next
        hp, hs = _ffn_ln(x3p, x3s, *ffn2, g[3:4], b[3:4], alpha)
        mem_states.append((mk.reshape(bp, n_mem, -1, HEAD_DIM), mv.reshape(bp, n_mem, -1, HEAD_DIM)))
    y_prompt, y_sample = hp.reshape(bp, tp, d), hs.reshape(bs, ts, d)
    heads = lambda a, bsz: a.reshape(depth, bsz, -1, N_HEADS, HEAD_DIM)
    p_mk, p_mv = [jnp.stack(f) for f in zip(*mem_states)]
    prompt_out = [heads(kv_p[0], bp), heads(kv_p[1], bp), *_result_states(rec_p), p_mk, p_mv]
    sample_out = [heads(kv_s[0], bs), heads(kv_s[1], bs)] + _result_states(rec_s)

    return (y_prompt, y_sample, *prompt_out, *sample_out)
```

```python
import functools

import numpy as np
import jax
import jax.numpy as jnp
from jax import lax
from jax.experimental import pallas as pl
from jax.experimental.pallas import tpu as pltpu

F32 = jnp.float32
BF16 = jnp.bfloat16

CHUNK = 64
HEAD_DIM = 64
N_HEADS = 4
W_GROUP = N_HEADS * HEAD_DIM
SLAB = 2 * HEAD_DIM
N_PAIRS = N_HEADS // 2
A_BAND_CHUNKS = 8
A_REACH = A_BAND_CHUNKS * CHUNK
REL_CLIP = 128
CONV_W = 4
LRU_C = 8.0
LN_EPS = 1e-5
LOG2E = 1.4426950408889634
N_GATES = 2 * N_HEADS
PAST_LEN = 4096
ROLL_W = 1024
CONV_PAD = 8

V7X_VMEM_LIMIT_BYTES = 56 * 1024 * 1024
TOKEN_TILE = 512
MIX_BLOCK = 256
MIX_SEQS = 4
POST_SUBTILES = 2

PR_QB, PR_QBS, PR_VB, PR_GB, PR_QC, PR_VC, PR_OC, PR_XD, PR_YD = range(9)
N_PR = 9


def _cparams(*sem):
    return pltpu.CompilerParams(dimension_semantics=sem, vmem_limit_bytes=V7X_VMEM_LIMIT_BYTES)


def _dot(a, b):
    return jnp.dot(a.astype(BF16), b.astype(BF16), preferred_element_type=F32)


def _dot_nt(a, b):
    return lax.dot_general(a.astype(BF16), b.astype(BF16), (((1,), (1,)), ((), ())), preferred_element_type=F32)


def _layer_norm(x, g, b):
    mu = jnp.mean(x, -1, keepdims=True)
    xc = x - mu
    var = jnp.mean(xc * xc, -1, keepdims=True)
    return xc * lax.rsqrt(var + LN_EPS) * g + b


def _silu(x):
    return x * jax.nn.sigmoid(x)


def _lockstep(chains):
    while chains:
        chains = [c for c in chains if next(c, "done") != "done"]
        yield


def _run_interleaved(chains, lag=0):
    pending, live, rnd = list(chains), [], 0
    while pending or live:
        while pending and rnd >= lag * (len(chains) - len(pending)):
            live.append(pending.pop(0))
        live = [c for c in live if next(c, "done") != "done"]
        rnd += 1


def _resident(shape, layer=None):
    nd = len(shape)
    if layer is None:
        return pl.BlockSpec(shape, lambda *_: (0,) * nd, pipeline_mode=pl.Buffered(1))
    return pl.BlockSpec((None,) + tuple(shape), lambda *_: (layer,) + (0,) * nd, pipeline_mode=pl.Buffered(1))


def _cast_plan(stack, layer, n_steps):
    _, r, c = stack.shape
    nb = next(k for k in range(n_steps, 0, -1) if n_steps % k == 0 and r % k == 0 and (r // k) % 16 == 0)
    every = n_steps // nb
    return (pl.BlockSpec((None, r // nb, c), lambda i: (layer, i // every, 0)),
            pl.BlockSpec((r // nb, c), lambda i: (i // every, 0)),
            jax.ShapeDtypeStruct((r, c), BF16))


def _with_casts(body, n_in, n_out, n_cast):
    def wrapped(*refs):
        ins, rest = refs[:n_in + n_cast], refs[n_in + n_cast:]
        outs, scratch = rest[:n_out + n_cast], rest[n_out + n_cast:]
        body(*ins[:n_in], *outs[:n_out], *scratch)
        for src, dst in zip(ins[n_in:], outs[n_out:]):
            dst[...] = src[...].astype(BF16)
    return wrapped


def _by_group(body, n_first, n_shared, n_out):
    def wrapped(xa_ref, xb_ref, *refs):
        shared, outs = refs[:n_shared], refs[n_shared:]
        i = pl.program_id(0)

        @pl.when(i < n_first)
        def _():
            body(xa_ref, *shared, *outs[:n_out])

        @pl.when(i >= n_first)
        def _():
            body(xb_ref, *shared, *outs[n_out:2 * n_out])
    return wrapped


def _group_blocks(tm, n_first):
    tok_a = lambda w: pl.BlockSpec((tm, w), lambda i: (jnp.minimum(i, n_first - 1), 0))
    tok_b = lambda w: pl.BlockSpec((tm, w), lambda i: (jnp.maximum(i - n_first, 0), 0))
    return tok_a, tok_b


def _ffn_ln_kernel(x_ref, wg_ref, wu_ref, wd_ref, g_ref, b_ref, o_ref, *, alpha):
    x = x_ref[...]
    xb = x.astype(BF16)
    gate = jnp.dot(xb, wg_ref[...], preferred_element_type=F32)
    up = jnp.dot(xb, wu_ref[...], preferred_element_type=F32)
    h = (_silu(gate) * up).astype(BF16)
    y = jnp.dot(h, wd_ref[...], preferred_element_type=F32)
    o_ref[...] = _layer_norm(alpha * x + 0.5 * y, g_ref[...], b_ref[...])


def _ffn_ln(xa, xb, wg, wu, wd, g, b, alpha):
    d = xa.shape[1]
    dff = wg.shape[1]
    tm = TOKEN_TILE
    na, nb = xa.shape[0] // tm, xb.shape[0] // tm
    tok_a, tok_b = _group_blocks(tm, na)
    return pl.pallas_call(
        _by_group(functools.partial(_ffn_ln_kernel, alpha=alpha), na, 5, 1),
        out_shape=(jax.ShapeDtypeStruct(xa.shape, F32), jax.ShapeDtypeStruct(xb.shape, F32)),
        grid=(na + nb,),
        in_specs=[tok_a(d), tok_b(d),
                  _resident((d, dff)), _resident((d, dff)), _resident((dff, d)),
                  _resident((1, d)), _resident((1, d))],
        out_specs=(tok_a(d), tok_b(d)),
        compiler_params=_cparams("arbitrary"),
        name="ffn_ln",
    )(xa, xb, wg, wu, wd, g, b)


def _in_proj_kernel(x_ref, wa_ref, wr_ref, wt_ref, ba_ref, br_ref, bt_ref,
                    pa_ref, pr_ref, kat_ref, kbt_ref, kct_ref, gt_ref, ak_ref, av_ref, *, first_of_stack, kept_every):
    xb = x_ref[...].astype(BF16)
    pa = jnp.dot(xb, wa_ref[...], preferred_element_type=F32) + ba_ref[...]
    pa_ref[...] = pa

    def keep():
        for ref, col in ((ak_ref, W_GROUP), (av_ref, 2 * W_GROUP)):
            if first_of_stack is not None:
                for other in range(ref.shape[0]):
                    if other != first_of_stack:
                        ref[other] = jnp.zeros(ref.shape[1:], F32)
                ref = ref.at[first_of_stack]
            kv = pa[:, col:col + W_GROUP]
            ref[...] = kv if kept_every is None else kv.T

    if kept_every is None:
        keep()
    else:
        pl.when(pl.program_id(0) % kept_every == kept_every - 1)(keep)
    pr_ref[...] = jnp.dot(xb, wr_ref[...], preferred_element_type=F32) + br_ref[...]
    tr = _dot_nt(wt_ref[...], xb) + bt_ref[...]
    kat_ref[...] = tr[0:W_GROUP].astype(BF16)
    kbt_ref[...] = tr[W_GROUP:2 * W_GROUP]
    kct_ref[...] = tr[2 * W_GROUP:3 * W_GROUP]
    gt_ref[...] = tr[3 * W_GROUP:3 * W_GROUP + N_GATES]


def _in_proj(x, wa, wr, wt, ba, br, bt, seq_len, layer, n_layers, prev_kv=None, casts=()):
    n, d = x.shape
    ca, cr, ct = wa.shape[1], wr.shape[1], wt.shape[0]
    tm = TOKEN_TILE
    per = max(seq_len // tm, 1)
    long_seq = seq_len >= tm
    assert (seq_len % tm == 0 and A_REACH == tm) if long_seq else (tm % seq_len == 0 and seq_len <= A_REACH)
    tok = lambda w: pl.BlockSpec((tm, w), lambda i: (i, 0))
    trn = lambda r: pl.BlockSpec((r, tm), lambda i: (0, i))
    plans = [_cast_plan(stack, layer_, n // tm) for stack, layer_ in casts]
    in_specs = [tok(d), _resident((d, ca)), _resident((d, cr)), _resident((ct, d)),
                _resident((1, ca)), _resident((1, cr)), _resident((ct, 1)), *[p[0] for p in plans]]
    n_in = len(in_specs)
    if prev_kv is None:
        first_of_stack, aliases = layer, {}
        lead, at = n_layers, 0
    else:
        first_of_stack, aliases = None, {n_in: 6, n_in + 1: 7}
        lead, at = None, layer
        in_specs += [pl.BlockSpec(memory_space=pl.ANY)] * 2
    if long_seq:
        kept = pl.BlockSpec((lead, None, W_GROUP, tm), lambda i: (at, i // per, 0, 0))
        kv_shape = jax.ShapeDtypeStruct((n_layers, n // seq_len, W_GROUP, tm), F32)
    else:
        kept = pl.BlockSpec((lead, tm, W_GROUP), lambda i: (at, i, 0))
        kv_shape = jax.ShapeDtypeStruct((n_layers, n, W_GROUP), F32)
    inner = _with_casts(functools.partial(_in_proj_kernel, first_of_stack=first_of_stack,
                                          kept_every=per if long_seq else None), 7, 8, len(casts))
    body = lambda *refs: inner(*refs[:n_in], *refs[n_in + len(aliases):])
    out = pl.pallas_call(
        body,
        out_shape=(jax.ShapeDtypeStruct((n, ca), F32), jax.ShapeDtypeStruct((n, cr), F32),
                   jax.ShapeDtypeStruct((W_GROUP, n), BF16), jax.ShapeDtypeStruct((W_GROUP, n), F32),
                   jax.ShapeDtypeStruct((W_GROUP, n), F32), jax.ShapeDtypeStruct((N_GATES, n), F32),
                   kv_shape, kv_shape, *[p[2] for p in plans]),
        grid=(n // tm,),
        in_specs=in_specs,
        out_specs=(tok(ca), tok(cr), trn(W_GROUP), trn(W_GROUP), trn(W_GROUP), trn(N_GATES), kept, kept,
                   *[p[1] for p in plans]),
        input_output_aliases=aliases,
        compiler_params=_cparams("arbitrary"),
        name="in_proj",
    )(x, wa, wr, wt, ba, br, bt, *[stack for stack, _ in casts], *(prev_kv or ()))
    return out[:6], out[6:8], out[8:]


def _kv_result(stack, bsz, seq_len):
    n_layers = stack.shape[0]
    if seq_len >= TOKEN_TILE:
        return stack.reshape(n_layers, bsz, N_HEADS, HEAD_DIM, -1).transpose(0, 1, 4, 2, 3)
    return stack.reshape(n_layers, bsz, -1, N_HEADS, HEAD_DIM)


def _head_rows(xt, h, pos):
    xh = xt[h * HEAD_DIM:(h + 1) * HEAD_DIM]
    z = jnp.zeros_like(xh)
    return jnp.concatenate([xh, z] if pos == 0 else [z, xh], axis=0)


def _col_bcast(eye_b, rows):
    hi = rows.astype(BF16)
    lo = (rows - hi.astype(F32)).astype(BF16)
    dn = (((1,), (1,)), ((), ()))
    return (lax.dot_general(eye_b, hi, dn, preferred_element_type=F32)
            + lax.dot_general(eye_b, lo, dn, preferred_element_type=F32))


def _rep_heads(rows4, n):
    return jnp.concatenate([jnp.broadcast_to(rows4[h:h + 1], (n, rows4.shape[1])) for h in range(N_HEADS)], axis=0)


def _seg_scan(x, seg_pos, seg_len, op, fill):
    s = 1
    while s < seg_len:
        x = op(x, jnp.where(seg_pos >= s, pltpu.roll(x, s, 1), fill))
        s *= 2
    return x


def _shift_rows(x, s, fill):
    if s % 8 == 0:
        return jnp.concatenate([jnp.full((s, x.shape[1]), fill, x.dtype), x[:x.shape[0] - s]], axis=0)
    row = lax.broadcasted_iota(jnp.int32, x.shape, 0)
    return jnp.where(row >= s, pltpu.roll(x, s, 0), fill)


def _mixer_kernel(pa_ref, pr_ref, kat_ref, kbt_ref, kct_ref, gt_ref, pastk_ref, pastv_ref, rel_ref,
                  cos_ref, sin_ref, cost_ref, sint_ref, dec_ref, qdec_ref, kdect_ref, sdec_ref,
                  ret0_ref, c0_ref, n0_ref, m0_ref,
                  cw_ref, cb_ref, wa_ref, ba_ref, wi_ref, bi_ref, lam_ref, buf0_ref, h0_ref,
                  y_ref, ret_out, c_out, n_out, m_out, h_out, buf_out,
                  bias_sc, spair, ppair, m_sc, win, h_sc, kpadt=None, vpad=None,
                  *, blk, seqs, n_steps, has_past, first_of_stack):
    state_outs = (ret_out, c_out, n_out, m_out, h_out, buf_out)
    if first_of_stack is not None:
        ret_out, c_out, n_out, m_out, h_out, buf_out = [ref.at[first_of_stack[0]] for ref in state_outs]
    L = blk
    band = A_REACH + L
    n_past = A_REACH // L if not has_past else 0
    n_var = n_past + 1
    bb = pl.program_id(0)
    tb = pl.program_id(1)
    tail = CONV_W - 1
    if n_steps == 1:
        at_first = at_last = lambda f: f()
    else:
        at_first = pl.when(tb == 0)
        at_last = pl.when(tb == n_steps - 1)

    li = lax.broadcasted_iota(jnp.int32, (L, L), 0)
    lj = lax.broadcasted_iota(jnp.int32, (L, L), 1)
    causal_neg = jnp.where(lj <= li, 0.0, -jnp.inf)
    eye_b = jnp.where(li == lj, 1.0, 0.0).astype(BF16)
    lane_lo = lax.broadcasted_iota(jnp.int32, (L, SLAB), 1) < HEAD_DIM
    ones_slab = jnp.ones((L, SLAB), BF16)
    ri = lax.broadcasted_iota(jnp.int32, (SLAB, SLAB), 0) // HEAD_DIM
    rj = lax.broadcasted_iota(jnp.int32, (SLAB, SLAB), 1) // HEAD_DIM
    smask = jnp.where(ri == rj, 1.0, 0.0)
    pmask = jnp.concatenate([smask, smask], axis=1)
    gi = lax.broadcasted_iota(jnp.int32, (W_GROUP, W_GROUP), 0) // HEAD_DIM
    gj = lax.broadcasted_iota(jnp.int32, (W_GROUP, W_GROUP), 1) // HEAD_DIM
    gmat = jnp.where(gi == gj, 1.0 / HEAD_DIM, 0.0).astype(BF16)
    e64i = lax.broadcasted_iota(jnp.int32, (HEAD_DIM, HEAD_DIM), 0)
    e64j = lax.broadcasted_iota(jnp.int32, (HEAD_DIM, HEAD_DIM), 1)
    eye64 = e64i == e64j
    row_lo = lax.broadcasted_iota(jnp.int32, (SLAB, 1), 0) < HEAD_DIM

    @pl.when((bb == 0) & (tb == 0))
    def _():
        qq = lax.broadcasted_iota(jnp.int32, (L, band), 0)
        kk = lax.broadcasted_iota(jnp.int32, (L, band), 1)
        off = kk - (qq // CHUNK) * CHUNK
        for h in range(N_HEADS):
            row = jnp.broadcast_to(rel_ref[h], (L, ROLL_W))
            tile = pltpu.roll(row, ROLL_W - L, 1, stride=1, stride_axis=0)[:, :band] * LOG2E
            tile = jnp.where(off >= 0, jnp.where(off < A_REACH + CHUNK, tile, -jnp.inf), -jnp.inf)
            for v in range(n_var):
                first_col = (n_past - v) * L
                bias_sc[v * N_HEADS + h] = jnp.where(kk >= first_col, tile, -jnp.inf) if first_col > 0 else tile

    for s in range(seqs):
        @at_first
        def _():
            for p in range(N_PAIRS):
                sblk, cblk = [], []
                for hh in range(2):
                    h = 2 * p + hh
                    z = jnp.zeros((HEAD_DIM, HEAD_DIM), F32)
                    s_h = ret0_ref[s, h]
                    c_h = c0_ref[s, h]
                    n_row = n0_ref[s, h:h + 1, :]
                    n_col = jnp.sum(jnp.where(eye64, jnp.broadcast_to(n_row, eye64.shape), 0.0),
                                    axis=1, keepdims=True)
                    n_rep = jnp.broadcast_to(n_col, (HEAD_DIM, HEAD_DIM))
                    sblk.append(jnp.concatenate([s_h, z] if hh == 0 else [z, s_h], axis=1))
                    cblk.append(jnp.concatenate([c_h, z, n_rep, z] if hh == 0 else [z, c_h, z, n_rep], axis=1))
                spair[s, p] = jnp.concatenate(sblk, axis=0)
                ppair[s, p] = jnp.concatenate(cblk, axis=0)
            m_sc[s, 0:N_HEADS, :] = jnp.broadcast_to(m0_ref[s], (N_HEADS, m_sc.shape[2]))
            win[s, CONV_PAD - tail:CONV_PAD, :] = buf0_ref[s]
            h_sc[s] = h0_ref[s]
            if not has_past:
                vpad[s, 0:A_REACH, :] = jnp.zeros((A_REACH, W_GROUP), BF16)
                kpadt[s, 0:n_past * W_GROUP, :] = jnp.zeros((n_past * W_GROUP, L), BF16)

    g8 = gt_ref[...]
    seg_pos = lax.broadcasted_iota(jnp.int32, g8.shape, 1) % L
    b8 = _seg_scan(jax.nn.log_sigmoid(g8), seg_pos, L, jnp.add, 0.0)
    a8 = g8 - pltpu.roll(b8, N_HEADS, 0)
    cm8 = _seg_scan(a8, seg_pos, L, jnp.maximum, -jnp.inf)
    b8 = pltpu.roll(b8, N_HEADS, 0)

    heads = [(h // 2, h % 2, h) for h in range(N_HEADS)]
    slab = lambda p: slice(p * SLAB, (p + 1) * SLAB)
    seg_of = lambda s: slice(s * L, (s + 1) * L)
    col = lambda s, j: pr_ref[s, :, j * W_GROUP:(j + 1) * W_GROUP]

    def halves(per_head):
        return jnp.concatenate([jnp.where(lane_lo, per_head[2 * p], per_head[2 * p + 1]) for p in range(N_PAIRS)],
                               axis=1)


    def attention_chain(s):
        pa = pa_ref[s]
        kat = kat_ref[:, seg_of(s)]
        v_new = pa[:, 2 * W_GROUP:].astype(BF16)
        if has_past:
            kpast = pastk_ref[s].astype(BF16)
            vband = jnp.concatenate([pastv_ref[s].astype(BF16), v_new], axis=0)
            bias_at = lambda h: bias_sc[h]
        else:
            kpadt[s, pl.ds(pl.multiple_of((tb + n_past) * W_GROUP, W_GROUP), W_GROUP), :] = kat
            kwin = kpadt[s, pl.ds(pl.multiple_of(tb * W_GROUP, W_GROUP), (n_past + 1) * W_GROUP), :]
            start = pl.multiple_of(tb * L, L)
            vpad[s, pl.ds(A_REACH + start, L), :] = v_new
            vband = vpad[s, pl.ds(start, band), :]
            var = jnp.minimum(tb, n_var - 1) * N_HEADS
            bias_at = lambda h: bias_sc[var + h]
        qa = (pa[:, 0:W_GROUP] * (HEAD_DIM ** -0.5 * LOG2E)).astype(BF16)
        sc = {}
        for p, hh, h in heads:
            q_slab = qa[:, slab(p)]
            if has_past:
                k_slab = kpast[:, slab(p)]
                lane = lax.broadcasted_iota(jnp.int32, k_slab.shape, 1)
                lane_h = lane < HEAD_DIM if hh == 0 else lane >= HEAD_DIM
                parts = [_dot_nt(q_slab, jnp.where(lane_h, k_slab, jnp.zeros_like(k_slab))),
                         jnp.dot(q_slab, _head_rows(kat, h, hh), preferred_element_type=F32)]
            else:
                parts = [jnp.dot(q_slab, _head_rows(kwin[i * W_GROUP:(i + 1) * W_GROUP], h, hh),
                                 preferred_element_type=F32) for i in range(n_past + 1)]
            sc[h] = jnp.concatenate(parts, axis=1)
        yield
        pe, den = {}, {}
        for _, _, h in heads:
            sh = sc[h] + bias_at(h)
            pe[h] = jnp.exp2(sh - jnp.max(sh, axis=1, keepdims=True))
            den[h] = jnp.sum(pe[h], axis=1, keepdims=True)
        yield
        res = {h: _dot(pe[h], vband)[:, slab(p)] for p, _, h in heads}
        yield
        y_ref[s, :, 0:W_GROUP] = halves({h: res[h] / den[h] for _, _, h in heads})

    def retention_chain(s):
        qr = (col(s, PR_QB) * cos_ref[...] + col(s, PR_QBS) * sin_ref[...]).astype(BF16)
        kbt = kbt_ref[:, seg_of(s)]
        half = HEAD_DIM // 2
        kbt_sw = jnp.concatenate([kbt[h * HEAD_DIM + o:h * HEAD_DIM + o + half]
                                  for h in range(N_HEADS) for o in (half, 0)], axis=0)
        krt = (kbt * cost_ref[:, seg_of(s)] + kbt_sw * sint_ref[:, seg_of(s)]) * HEAD_DIM ** -0.5
        krt_b = krt.astype(BF16)
        kdt_b = (krt * kdect_ref[...]).astype(BF16)
        vb = col(s, PR_VB).astype(BF16)
        raw = {h: jnp.dot(qr[:, slab(p)], _head_rows(krt_b, h, hh), preferred_element_type=F32)
               for p, hh, h in heads}
        s0 = {p: spair[s, p] for p in range(N_PAIRS)}
        inter = {p: _dot(qr[:, slab(p)], s0[p]) for p in range(N_PAIRS)}
        add = {p: _dot(kdt_b[slab(p)], vb[:, slab(p)]) for p in range(N_PAIRS)}
        yield
        scores = {h: raw[h] * dec_ref[h] for _, _, h in heads}
        for p in range(N_PAIRS):
            spair[s, p] = s0[p] * sdec_ref[p] + add[p] * smask
        yield
        o = {h: _dot(scores[h], vb[:, slab(p)]) for p, _, h in heads}
        yield
        ob = halves(o) + jnp.concatenate([inter[p] for p in range(N_PAIRS)], axis=1) * qdec_ref[...]
        mu = _dot(ob, gmat)
        yield
        oc = ob - mu
        var = _dot(oc * oc, gmat)
        yield
        y_ref[s, :, W_GROUP:2 * W_GROUP] = oc * lax.rsqrt(var + LN_EPS) * _silu(col(s, PR_GB))

    def mlstm_chain(s):
        a4 = a8[0:N_HEADS, seg_of(s)]
        b4 = b8[0:N_HEADS, seg_of(s)]
        m0c = m_sc[s, 0:N_HEADS, 0:1]
        big_m = jnp.maximum(m0c, cm8[0:N_HEADS, seg_of(s)])
        m4 = b4 + big_m
        w0_4 = jnp.exp(m0c - big_m)
        m_last = big_m[:, L - 1:L]
        wrow4 = jnp.exp(a4 - m_last)
        kct = kct_ref[:, seg_of(s)] * HEAD_DIM ** -0.5
        kct_b = kct.astype(BF16)
        kw_b = (kct * _rep_heads(wrow4, HEAD_DIM)).astype(BF16)
        qc = col(s, PR_QC).astype(BF16)
        vc = col(s, PR_VC).astype(BF16)
        v_aug = {p: jnp.concatenate([vc[:, slab(p)], ones_slab], axis=1) for p in range(N_PAIRS)}
        p0 = {p: ppair[s, p] for p in range(N_PAIRS)}
        w0_bc = _col_bcast(eye_b, _rep_heads(w0_4, HEAD_DIM))
        floor_bc = _col_bcast(eye_b, _rep_heads(jnp.exp(-m4), HEAD_DIM))
        m_bc = {h: _col_bcast(eye_b, jnp.broadcast_to(big_m[h:h + 1], (L, L))) for _, _, h in heads}
        raw = {h: jnp.dot(qc[:, slab(p)], _head_rows(kct_b, h, hh), preferred_element_type=F32)
               for p, hh, h in heads}
        inter = {p: _dot(qc[:, slab(p)], p0[p]) for p in range(N_PAIRS)}
        add = {p: _dot(kw_b[slab(p)], v_aug[p]) for p in range(N_PAIRS)}
        yield
        qk = {h: raw[h] * jnp.exp(a4[h:h + 1] - m_bc[h] + causal_neg) for _, _, h in heads}
        for p in range(N_PAIRS):
            w_state = jnp.where(row_lo, w0_4[2 * p:2 * p + 1, L - 1:L], w0_4[2 * p + 1:2 * p + 2, L - 1:L])
            ppair[s, p] = p0[p] * w_state + add[p] * pmask
        m_sc[s, 0:N_HEADS, :] = jnp.broadcast_to(m4[:, L - 1:L], (N_HEADS, m_sc.shape[2]))
        yield
        res = {h: _dot(qk[h], v_aug[p]) for p, _, h in heads}
        yield
        hc = []
        for p in range(N_PAIRS):
            w0 = w0_bc[:, slab(p)]
            num = jnp.where(lane_lo, res[2 * p][:, 0:SLAB], res[2 * p + 1][:, 0:SLAB]) + inter[p][:, 0:SLAB] * w0
            den = jnp.where(lane_lo, res[2 * p][:, SLAB:], res[2 * p + 1][:, SLAB:]) + inter[p][:, SLAB:] * w0
            hc.append(num / jnp.maximum(jnp.abs(den), floor_bc[:, slab(p)]))
        hc = jnp.concatenate(hc, axis=1)
        mu = _dot(hc, gmat)
        yield
        hcc = hc - mu
        var = _dot(hcc * hcc, gmat)
        yield
        y_ref[s, :, 2 * W_GROUP:3 * W_GROUP] = hcc * lax.rsqrt(var + LN_EPS) * jax.nn.sigmoid(col(s, PR_OC))

    def rglru_chain(s):
        xd = col(s, PR_XD)
        win[s, CONV_PAD:CONV_PAD + L, :] = xd
        xc = cb_ref[...]
        for j in range(CONV_W):
            off = CONV_PAD - tail + j
            xc = xc + win[s, off:off + L, :] * cw_ref[j:j + 1, :]
        win[s, CONV_PAD - tail:CONV_PAD, :] = xd[L - tail:, :]
        r_pre = _dot(xc, wa_ref[...])
        i_pre = _dot(xc, wi_ref[...])
        yield
        r = jax.nn.sigmoid(r_pre + ba_ref[...])
        i = jax.nn.sigmoid(i_pre + bi_ref[...])
        neg_lam = -lam_ref[...]
        softplus = jnp.maximum(neg_lam, 0.0) + jnp.log1p(jnp.exp(-jnp.abs(neg_lam)))
        log_a = -LRU_C * r * softplus
        a = jnp.exp(log_a)
        u = jnp.sqrt(-jnp.tanh(log_a) * (jnp.exp(2.0 * log_a) + 1.0)) * (i * xc)
        st = 1
        while st < L:
            u = a * _shift_rows(u, st, 0.0) + u
            a = a * _shift_rows(a, st, 1.0)
            st *= 2
            if st in (8, 64):
                yield
        hseq = u + a * h_sc[s]
        h_sc[s] = hseq[L - 1:L, :]
        y_ref[s, :, 3 * W_GROUP:] = hseq * jax.nn.gelu(col(s, PR_YD))

    if seqs == 1:
        order, lag = (attention_chain, rglru_chain, mlstm_chain, retention_chain), 2
    else:
        order, lag = (attention_chain, retention_chain, mlstm_chain, rglru_chain), 0
    _run_interleaved([chain(s) for chain in order for s in range(seqs)], lag=lag)

    for s in range(seqs):
        @at_last
        def _():
            for p in range(N_PAIRS):
                sp = spair[s, p]
                pp = ppair[s, p]
                for hh in range(2):
                    h = 2 * p + hh
                    rs = slice(hh * HEAD_DIM, (hh + 1) * HEAD_DIM)
                    ret_out[s, h] = sp[rs, rs]
                    c_out[s, h] = pp[rs, rs]
                    n_rep = pp[rs, SLAB + hh * HEAD_DIM:SLAB + (hh + 1) * HEAD_DIM]
                    n_out[s, h:h + 1, :] = jnp.sum(jnp.where(eye64, n_rep, 0.0), axis=0, keepdims=True)
            m_out[s] = m_sc[s, 0:N_HEADS, 0:1]
            h_out[s] = h_sc[s]
            buf_out[s] = pr_ref[s, L - tail:, PR_XD * W_GROUP:(PR_XD + 1) * W_GROUP]
            if first_of_stack is not None:
                for other in first_of_stack[1]:
                    for ref in state_outs:
                        ref[other, s] = jnp.zeros(ref.shape[2:], F32)


def _mixer_kernel_fresh(pa_ref, pr_ref, kat_ref, kbt_ref, kct_ref, gt_ref, *rest, **static):
    _mixer_kernel(pa_ref, pr_ref, kat_ref, kbt_ref, kct_ref, gt_ref, None, None, *rest, **static)


def _mixer_geometry(t, has_past):
    if has_past:
        assert t == CHUNK, "sequences with a carried cache are expected to be a single chunk"
        return t, MIX_SEQS
    assert t % MIX_BLOCK == 0 and A_REACH % MIX_BLOCK == 0
    return MIX_BLOCK, 1


def _mixer_tables(pos, blk, seqs):
    cos, sin = _rotary_tables(pos)
    return (cos, sin, jnp.tile(cos.T, (1, seqs)), jnp.tile(sin.T, (1, seqs))) + _retention_tables(blk)


def _mixer(pa, pr, kat, kbt, kct, gt, past, rel_rows, tables, rec, lw, n_layers, prev_states=None):
    ret0, c0, n0, m0, h0, buf0, rec_layer = rec
    m0, h0 = m0[..., None], h0[:, :, None, :]
    bsz, t, _ = pa.shape
    has_past = past is not None
    blk, seqs = _mixer_geometry(t, has_past)
    nb = t // blk
    assert bsz % seqs == 0
    band = A_REACH + blk
    n_var = 1 if has_past else A_REACH // blk + 1
    tail = CONV_W - 1

    tok3 = lambda w: pl.BlockSpec((seqs, blk, w), lambda b, c: (b, c, 0))
    trn = lambda r: pl.BlockSpec((r, seqs * blk), lambda b, c: (0, b * nb + c))
    per_seq = lambda *shape: pl.BlockSpec((seqs,) + shape, lambda b, c: (b,) + (0,) * len(shape))
    const = lambda *shape: pl.BlockSpec(shape, lambda b, c: (0,) * len(shape))
    carried = lambda *shape: pl.BlockSpec((None, seqs) + shape, lambda b, c: (rec_layer, b) + (0,) * len(shape))
    past_specs, past_args = [], ()
    if has_past:
        layer = past[2]
        cache = pl.BlockSpec((None, seqs, A_REACH, W_GROUP), lambda b, c: (layer, b, 0, 0))
        past_specs, past_args = [cache, cache], past[:2]
    in_specs = [tok3(3 * W_GROUP), tok3(N_PR * W_GROUP), trn(W_GROUP), trn(W_GROUP), trn(W_GROUP), trn(N_GATES),
                *past_specs, const(N_HEADS, 1, ROLL_W),
                pl.BlockSpec((blk, W_GROUP), lambda b, c: (c, 0)), pl.BlockSpec((blk, W_GROUP), lambda b, c: (c, 0)),
                pl.BlockSpec((W_GROUP, seqs * blk), lambda b, c: (0, c)),
                pl.BlockSpec((W_GROUP, seqs * blk), lambda b, c: (0, c)),
                const(N_HEADS, blk, blk), const(blk, W_GROUP), const(W_GROUP, blk), const(N_PAIRS, SLAB, SLAB),
                carried(N_HEADS, HEAD_DIM, HEAD_DIM), carried(N_HEADS, HEAD_DIM, HEAD_DIM),
                carried(N_HEADS, HEAD_DIM), carried(N_HEADS, 1),
                const(CONV_W, W_GROUP), const(1, W_GROUP), const(W_GROUP, W_GROUP), const(1, W_GROUP),
                const(W_GROUP, W_GROUP), const(1, W_GROUP), const(1, W_GROUP),
                carried(tail, W_GROUP), carried(1, W_GROUP)]
    out_layer = lw['layer']
    state_shapes = ((N_HEADS, HEAD_DIM, HEAD_DIM), (N_HEADS, HEAD_DIM, HEAD_DIM), (N_HEADS, HEAD_DIM), (N_HEADS, 1),
                    (1, W_GROUP), (tail, W_GROUP))
    if prev_states is None:
        first_of_stack = (out_layer, tuple(l for l in range(n_layers) if l != out_layer))
        produced = lambda *shape: pl.BlockSpec((n_layers, seqs) + shape, lambda b, c: (0, b) + (0,) * len(shape))
    else:
        first_of_stack = None
        produced = lambda *shape: pl.BlockSpec((None, seqs) + shape, lambda b, c: (out_layer, b) + (0,) * len(shape))
    out_shape = (jax.ShapeDtypeStruct((bsz, t, 4 * W_GROUP), F32),
                 *[jax.ShapeDtypeStruct((n_layers, bsz) + sh, F32) for sh in state_shapes])
    out_specs = (tok3(4 * W_GROUP), *[produced(*sh) for sh in state_shapes])
    n_in = len(in_specs)
    aliases = {}
    if prev_states is not None:
        in_specs += [pl.BlockSpec(memory_space=pl.ANY)] * len(state_shapes)
        aliases = {n_in + k: 1 + k for k in range(len(state_shapes))}
    scratch = [pltpu.VMEM((n_var * N_HEADS, blk, band), F32),
               pltpu.VMEM((seqs, N_PAIRS, SLAB, SLAB), F32), pltpu.VMEM((seqs, N_PAIRS, SLAB, 2 * SLAB), F32),
               pltpu.VMEM((seqs, 8, 128), F32), pltpu.VMEM((seqs, CONV_PAD + blk, W_GROUP), F32),
               pltpu.VMEM((seqs, 1, W_GROUP), F32)]
    if not has_past:
        scratch += [pltpu.VMEM((seqs, (nb + A_REACH // blk) * W_GROUP, blk), BF16),
                    pltpu.VMEM((seqs, A_REACH + t, W_GROUP), BF16)]
    body = functools.partial(_mixer_kernel if has_past else _mixer_kernel_fresh,
                             blk=blk, seqs=seqs, n_steps=nb, has_past=has_past, first_of_stack=first_of_stack)
    if prev_states is not None:
        inner, n_alias = body, len(state_shapes)
        body = lambda *refs: inner(*refs[:n_in], *refs[n_in + n_alias:])
    return pl.pallas_call(
        body,
        out_shape=out_shape,
        grid=(bsz // seqs, nb),
        in_specs=in_specs,
        out_specs=out_specs,
        scratch_shapes=scratch,
        input_output_aliases=aliases,
        compiler_params=_cparams("arbitrary", "arbitrary"),
        name="mixer",
    )(pa, pr, kat, kbt, kct, gt, *past_args, rel_rows, *tables,
      ret0, c0, n0, m0, lw['conv_w'], lw['conv_b'], lw['lru_wa'], lw['lru_ba'], lw['lru_wi'], lw['lru_bi'],
      lw['lru_lambda'], buf0, h0, *(prev_states or ()))


def _post_kernel(x_ref, y_ref, wo_ref, g1_ref, b1_ref, wq_ref, mk_ref, mv_ref, xo_ref,
                 g2_ref, b2_ref, o_ref, att_sc, *, alpha, seqs, subtiles):
    rows_sub = x_ref.shape[0] // subtiles
    rows = rows_sub // seqs
    head = lambda h: slice(h * HEAD_DIM, (h + 1) * HEAD_DIM)

    def subtile_chain(i):
        rs = slice(i * rows_sub, (i + 1) * rows_sub)
        x = x_ref[rs, :]
        mix = _dot(y_ref[rs, :], wo_ref[...])
        yield
        x2 = _layer_norm(alpha * x + mix, g1_ref[...], b1_ref[...])
        yield
        q = _dot(x2, wq_ref[...]).astype(BF16)
        yield

        def attend(s, h):
            qs = slice(s * rows, (s + 1) * rows)
            sc = _dot_nt(q[qs, head(h)], mk_ref[s, :, head(h)]) * HEAD_DIM ** -0.5
            yield
            p = jnp.exp(sc - jnp.max(sc, axis=1, keepdims=True))
            den = jnp.sum(p, axis=1, keepdims=True)
            yield
            o = _dot(p, mv_ref[s, :, head(h)])
            yield
            att_sc[i * rows_sub + s * rows:i * rows_sub + (s + 1) * rows, head(h)] = o / den

        yield from _lockstep([attend(s, h) for s in range(seqs) for h in range(N_HEADS)])
        out = _dot(att_sc[rs, :], xo_ref[...])
        yield
        o_ref[rs, :] = _layer_norm(alpha * x2 + out, g2_ref[...], b2_ref[...])

    _run_interleaved([subtile_chain(i) for i in range(subtiles)], lag=2)


def _post(x, y, wo, wq, xo, wlayer, g1, b1, g2, b2, mk, mv, layer, alpha, seq_len, casts=()):
    n, d = x.shape
    n_mem, dx = mk.shape[2], mk.shape[3]
    if seq_len >= POST_SUBTILES * TOKEN_TILE:
        subtiles, seqs = POST_SUBTILES, 1
        tm = subtiles * TOKEN_TILE
        mem = pl.BlockSpec((None, 1, n_mem, dx), lambda i: (layer, i // (seq_len // tm), 0, 0))
    else:
        tm = TOKEN_TILE
        subtiles, seqs = 1, tm // seq_len
        mem = pl.BlockSpec((None, seqs, n_mem, dx), lambda i: (layer, i, 0, 0))
    assert n % tm == 0 and tm % (subtiles * seqs) == 0
    tok = lambda w: pl.BlockSpec((tm, w), lambda i: (i, 0))
    plans = [_cast_plan(stack, lyr, n // tm) for stack, lyr in casts]
    out = pl.pallas_call(
        _with_casts(functools.partial(_post_kernel, alpha=alpha, seqs=seqs, subtiles=subtiles), 11, 1, len(casts)),
        out_shape=(jax.ShapeDtypeStruct((n, d), F32), *[p[2] for p in plans]),
        grid=(n // tm,),
        in_specs=[tok(d), tok(y.shape[1]),
                  _resident(wo.shape[1:], wlayer), _resident((1, d)), _resident((1, d)),
                  _resident(wq.shape[1:], wlayer), mem, mem, _resident(xo.shape[1:], wlayer),
                  _resident((1, d)), _resident((1, d)), *[p[0] for p in plans]],
        out_specs=(tok(d), *[p[1] for p in plans]),
        scratch_shapes=[pltpu.VMEM((tm, dx), F32)],
        compiler_params=_cparams("parallel"),
        name="post",
    )(x, y, wo, g1, b1, wq, mk, mv, xo, g2, b2, *[stack for stack, _ in casts])
    return out[0], out[1:]


def _mem_kv_kernel(x_ref, w_ref, o_ref):
    o_ref[...] = _dot(x_ref[...], w_ref[...])


def _mem_kv(mem, w):
    n, d = mem.shape
    tm = min(TOKEN_TILE, n)
    return pl.pallas_call(
        _mem_kv_kernel,
        out_shape=jax.ShapeDtypeStruct((n, w.shape[1]), F32),
        grid=(n // tm,),
        in_specs=[pl.BlockSpec((tm, d), lambda i: (i, 0)), _resident(w.shape)],
        out_specs=pl.BlockSpec((tm, w.shape[1]), lambda i: (i, 0)),
        compiler_params=_cparams("parallel"),
        name="mem_kv",
    )(mem, w)


def _rotary_tables(pos):
    half = HEAD_DIM // 2
    inv = jnp.exp(-jnp.log(10000.0) * jnp.arange(half, dtype=F32) / half)
    ang = pos.astype(F32)[:, None] * inv[None, :]
    cos, sin = jnp.cos(ang), jnp.sin(ang)
    cos_full = jnp.tile(jnp.concatenate([cos, cos], -1), (1, N_HEADS))
    sin_full = jnp.tile(jnp.concatenate([-sin, sin], -1), (1, N_HEADS))
    return cos_full, sin_full


def _retention_tables(blk):
    log_g = jnp.log1p(-jnp.exp2(-5.0 - jnp.arange(N_HEADS, dtype=F32)))
    idx = jnp.arange(blk, dtype=F32)
    diff = idx[:, None] - idx[None, :]
    dec = jnp.exp(jnp.where((diff >= 0)[None], diff[None] * log_g[:, None, None], -jnp.inf))
    q_dec = jnp.exp((idx[:, None] + 1.0) * log_g[None, :])
    k_dec = jnp.exp((blk - 1.0 - idx)[:, None] * log_g[None, :])
    s_dec = jnp.exp(blk * log_g)
    lanes = lambda tbl: jnp.repeat(tbl, HEAD_DIM, axis=1)
    pair = lambda p: jnp.kron(jnp.diag(s_dec[2 * p:2 * p + 2]), jnp.ones((HEAD_DIM, HEAD_DIM), F32))
    return dec, lanes(q_dec), lanes(k_dec).T, jnp.stack([pair(p) for p in range(N_PAIRS)])


def _rel_bias_rows(rel_bias, blk):
    idx = np.clip(A_REACH + blk - np.arange(ROLL_W), -REL_CLIP, REL_CLIP) + REL_CLIP
    h, n_rel = rel_bias.shape
    lead = A_REACH + blk - REL_CLIP
    rest = ROLL_W - lead - n_rel
    if lead >= 0 and rest >= 0:
        assert np.array_equal(idx, np.concatenate([np.full(lead, n_rel - 1), np.arange(n_rel)[::-1],
                                                   np.zeros(rest, np.int64)]))
        rows = jnp.concatenate([jnp.broadcast_to(rel_bias[:, n_rel - 1:], (h, lead)), rel_bias[:, ::-1],
                                jnp.broadcast_to(rel_bias[:, :1], (h, rest))], axis=1)
    else:
        rows = rel_bias[:, idx]
    return rows[:, None, :]


def _block_diag(w):
    h, c, _ = w.shape
    eye = jnp.eye(h, dtype=w.dtype)
    return (eye[:, None, :, None] * w[:, :, None, :]).reshape(h * c, h * c)


def _swap_halves(a):
    half = HEAD_DIM // 2
    return jnp.concatenate([a[..., h * HEAD_DIM + s:h * HEAD_DIM + s + half]
                            for h in range(N_HEADS) for s in (half, 0)], axis=-1)


def _mix_and_post(x1, proj, shape, tables, mem, past, rec, lw, alpha, n_layers, prev_states, post_casts=()):
    bsz, t, d = shape
    n = bsz * t
    g, b = lw['ln_g'], lw['ln_b']
    pa, pr, kat, kbt, kct, gt = proj
    pa = pa.reshape(bsz, t, 3 * W_GROUP)
    pr = pr.reshape(bsz, t, N_PR * W_GROUP)
    blk, _ = _mixer_geometry(t, past is not None)
    y, *rec_stacks = _mixer(pa, pr, kat, kbt, kct, gt, past, _rel_bias_rows(lw['rel_bias'], blk), tables, rec, lw,
                            n_layers, prev_states)
    x3, post_cast_out = _post(x1, y.reshape(n, 4 * W_GROUP), lw['w_out'], lw['x_wq'], lw['x_wo'], lw['layer'],
                              g[1:2], b[1:2], g[2:3], b[2:3], *mem, alpha, t, post_casts)
    return x3, rec_stacks, post_cast_out


def _result_states(stacks):
    ret, c, n, m, h, conv = stacks
    return [ret, c, n, m[..., 0], h[:, :, 0, :], conv]


def kernel(x_prompt, x_sample, mem_prompt, cache_a_k, cache_a_v, state_ret, state_mlstm_c, state_mlstm_n,
           state_mlstm_m, state_lru_h, state_conv, cache_mem_k, cache_mem_v, ln_g, ln_b,
           ffn1_gate, ffn1_up, ffn1_down, ffn2_gate, ffn2_up, ffn2_down, w_in, b_in, a_rel_bias,
           conv_w, conv_b, lru_wa, lru_ba, lru_wi, lru_bi, lru_lambda, w_out, x_wq, x_wk, x_wv, x_wo):
    depth = ln_g.shape[0]
    alpha = (2.0 * depth) ** 0.25
    row = lambda v: v[None, :]
    grp = lambda a, j: a[..., j * W_GROUP:(j + 1) * W_GROUP]
    QA, KA, VA, QB, KB, VB, GB, QC, KC, VC, OC, XD, YD = range(13)

    def layer_weights(l):
        w, bias = w_in[l], b_in[l]
        nat = lambda a: jnp.concatenate(
            [grp(a, QB), _swap_halves(grp(a, QB)), grp(a, VB), grp(a, GB), grp(a, QC), grp(a, VC), grp(a, OC),
             grp(a, XD), grp(a, YD)], axis=-1)
        trn = lambda a: jnp.concatenate([grp(a, KA), grp(a, KB), grp(a, KC), a[..., 13 * W_GROUP:]], axis=-1)
        return {'ln_g': ln_g[l], 'ln_b': ln_b[l],
                'layer': l, **stacks,
                'w_in_a': w[:, :3 * W_GROUP].astype(BF16), 'w_in_r': nat(w).astype(BF16),
                'w_in_t': trn(w).T.astype(BF16),
                'b_in_a': row(bias[:3 * W_GROUP]), 'b_in_r': row(nat(bias)), 'b_in_t': trn(bias)[:, None],
                'rel_bias': a_rel_bias[l],
                'conv_w': conv_w[l], 'conv_b': row(conv_b[l]),
                'lru_wa': _block_diag(lru_wa[l]).astype(BF16), 'lru_ba': row(lru_ba[l]),
                'lru_wi': _block_diag(lru_wi[l]).astype(BF16), 'lru_bi': row(lru_bi[l]),
                'lru_lambda': row(lru_lambda[l]),
                'x_wkv': jnp.concatenate([x_wk[l], x_wv[l]], axis=1).astype(BF16)}

    stacks = {name: w.astype(BF16) for name, w in (('w_out', w_out), ('x_wq', x_wq), ('x_wo', x_wo))}
    ffn1_f32, ffn2_f32 = (ffn1_gate, ffn1_up, ffn1_down), (ffn2_gate, ffn2_up, ffn2_down)
    ffn1_w = {0: tuple(w[0].astype(BF16) for w in ffn1_f32)}
    weights = [layer_weights(l) for l in range(depth)]
    dx = x_wk.shape[2]

    bp, tp, d = x_prompt.shape
    bs, ts, _ = x_sample.shape
    n_mem = mem_prompt.shape[1]
    tables_p = _mixer_tables(jnp.arange(tp), *_mixer_geometry(tp, False))
    rec0 = (jnp.zeros((1, bp, N_HEADS, HEAD_DIM, HEAD_DIM), F32), jnp.zeros((1, bp, N_HEADS, HEAD_DIM, HEAD_DIM), F32),
            jnp.zeros((1, bp, N_HEADS, HEAD_DIM), F32), jnp.zeros((1, bp, N_HEADS), F32),
            jnp.zeros((1, bp, W_GROUP), F32), jnp.zeros((1, bp, CONV_W - 1, W_GROUP), F32), 0)
    tables_s = _mixer_tables(PAST_LEN + jnp.arange(ts), *_mixer_geometry(ts, True))
    past_k_all = cache_a_k.reshape(depth, bs, A_REACH, W_GROUP)
    past_v_all = cache_a_v.reshape(depth, bs, A_REACH, W_GROUP)
    mem_k_all = cache_mem_k.reshape(depth, bs, n_mem, dx)
    mem_v_all = cache_mem_v.reshape(depth, bs, n_mem, dx)

    hp, hs = x_prompt.reshape(bp * tp, d), x_sample.reshape(bs * ts, d)
    mem_states, rec_p, rec_s, kv_p, kv_s = [], None, None, None, None
    for l in range(depth):
        lw = weights[l]
        g, b = lw['ln_g'], lw['ln_b']
        mkv = _mem_kv(mem_prompt.reshape(bp * n_mem, d), lw['x_wkv'])
        mk = mkv[:, :dx].reshape(bp, n_mem, dx)
        mv = mkv[:, dx:].reshape(bp, n_mem, dx)
        x1p, x1s = _ffn_ln(hp, hs, *ffn1_w[l], g[0:1], b[0:1], alpha)
        w_in = (lw['w_in_a'], lw['w_in_r'], lw['w_in_t'], lw['b_in_a'], lw['b_in_r'], lw['b_in_t'])
        proj_p, kv_p, ffn2 = _in_proj(x1p, *w_in, tp, l, depth, kv_p, [(w, l) for w in ffn2_f32])
        proj_s, kv_s, _ = _in_proj(x1s, *w_in, ts, l, depth, kv_s)
        post_casts = [(w, l + 1) for w in ffn1_f32] if l + 1 < depth else []
        x3p, rec_p, cast_next = _mix_and_post(x1p, proj_p, (bp, tp, d), tables_p, (mk[None], mv[None], 0), None,
                                              rec0, lw, alpha, depth, rec_p, post_casts)
        rec = (state_ret, state_mlstm_c, state_mlstm_n, state_mlstm_m, state_lru_h, state_conv, l)
        x3s, rec_s, _ = _mix_and_post(x1s, proj_s, (bs, ts, d), tables_s, (mem_k_all, mem_v_all, l),
                                      (past_k_all, past_v_all, l), rec, lw, alpha, depth, rec_s)
        if post_casts:
            ffn1_w[l + 1] = cast_next
        hp, hs = _ffn_ln(x3p, x3s, *ffn2, g[3:4], b[3:4], alpha)
        mem_states.append((mk.reshape(bp, n_mem, -1, HEAD_DIM), mv.reshape(bp, n_mem, -1, HEAD_DIM)))
    y_prompt, y_sample = hp.reshape(bp, tp, d), hs.reshape(bs, ts, d)
    p_mk, p_mv = [jnp.stack(f) for f in zip(*mem_states)]
    prompt_out = [*[_kv_result(a, bp, tp) for a in kv_p], *_result_states(rec_p), p_mk, p_mv]
    sample_out = [_kv_result(a, bs, ts) for a in kv_s] + _result_states(rec_s)

    return (y_prompt, y_sample, *prompt_out, *sample_out)
```

```python
import functools

import numpy as np
import jax
import jax.numpy as jnp
from jax import lax
from jax.experimental import pallas as pl
from jax.experimental.pallas import tpu as pltpu

F32 = jnp.float32
BF16 = jnp.bfloat16

CHUNK = 64
HEAD_DIM = 64
N_HEADS = 4
W_GROUP = N_HEADS * HEAD_DIM
SLAB = 2 * HEAD_DIM
N_PAIRS = N_HEADS // 2
A_BAND_CHUNKS = 8
A_REACH = A_BAND_CHUNKS * CHUNK
REL_CLIP = 128
CONV_W = 4
LRU_C = 8.0
LN_EPS = 1e-5
LOG2E = 1.4426950408889634
N_GATES = 2 * N_HEADS
PAST_LEN = 4096
ROLL_W = 1024
CONV_PAD = 8

V7X_VMEM_LIMIT_BYTES = 56 * 1024 * 1024
TOKEN_TILE = 512
MIX_BLOCK = 256
MIX_SEQS = 4
POST_SUBTILES = 2

PR_QB, PR_QBS, PR_VB, PR_GB, PR_QC, PR_VC, PR_OC, PR_XD, PR_YD = range(9)
N_PR = 9


def _cparams(*sem):
    return pltpu.CompilerParams(dimension_semantics=sem, vmem_limit_bytes=V7X_VMEM_LIMIT_BYTES)


def _dot(a, b):
    return jnp.dot(a.astype(BF16), b.astype(BF16), preferred_element_type=F32)


def _dot_nt(a, b):
    return lax.dot_general(a.astype(BF16), b.astype(BF16), (((1,), (1,)), ((), ())), preferred_element_type=F32)


def _layer_norm(x, g, b):
    mu = jnp.mean(x, -1, keepdims=True)
    xc = x - mu
    var = jnp.mean(xc * xc, -1, keepdims=True)
    return xc * lax.rsqrt(var + LN_EPS) * g + b


def _silu(x):
    return x * jax.nn.sigmoid(x)


def _lockstep(chains):
    while chains:
        chains = [c for c in chains if next(c, "done") != "done"]
        yield


def _run_interleaved(chains, lag=0):
    pending, live, rnd = list(chains), [], 0
    while pending or live:
        while pending and rnd >= lag * (len(chains) - len(pending)):
            live.append(pending.pop(0))
        live = [c for c in live if next(c, "done") != "done"]
        rnd += 1


def _resident(shape, layer=None):
    nd = len(shape)
    if layer is None:
        return pl.BlockSpec(shape, lambda *_: (0,) * nd, pipeline_mode=pl.Buffered(1))
    return pl.BlockSpec((None,) + tuple(shape), lambda *_: (layer,) + (0,) * nd, pipeline_mode=pl.Buffered(1))


def _cast_plan(stack, layer, n_steps):
    _, r, c = stack.shape
    nb = next(k for k in range(n_steps, 0, -1) if n_steps % k == 0 and r % k == 0 and (r // k) % 16 == 0)
    every = n_steps // nb
    return (pl.BlockSpec((None, r // nb, c), lambda i: (layer, i // every, 0)),
            pl.BlockSpec((r // nb, c), lambda i: (i // every, 0)),
            jax.ShapeDtypeStruct((r, c), BF16))


def _with_casts(body, n_in, n_out, n_cast):
    def wrapped(*refs):
        ins, rest = refs[:n_in + n_cast], refs[n_in + n_cast:]
        outs, scratch = rest[:n_out + n_cast], rest[n_out + n_cast:]
        body(*ins[:n_in], *outs[:n_out], *scratch)
        for src, dst in zip(ins[n_in:], outs[n_out:]):
            dst[...] = src[...].astype(BF16)
    return wrapped


def _by_group(body, n_first, n_shared, n_out):
    def wrapped(xa_ref, xb_ref, *refs):
        shared, outs = refs[:n_shared], refs[n_shared:]
        i = pl.program_id(0)

        @pl.when(i < n_first)
        def _():
            body(xa_ref, *shared, *outs[:n_out])

        @pl.when(i >= n_first)
        def _():
            body(xb_ref, *shared, *outs[n_out:2 * n_out])
    return wrapped


def _group_blocks(tm, n_first):
    tok_a = lambda w: pl.BlockSpec((tm, w), lambda i: (jnp.minimum(i, n_first - 1), 0))
    tok_b = lambda w: pl.BlockSpec((tm, w), lambda i: (jnp.maximum(i - n_first, 0), 0))
    return tok_a, tok_b


def _ffn_ln_kernel(x_ref, wg_ref, wu_ref, wd_ref, g_ref, b_ref, o_ref, *, alpha):
    x = x_ref[...]
    xb = x.astype(BF16)
    gate = jnp.dot(xb, wg_ref[...], preferred_element_type=F32)
    up = jnp.dot(xb, wu_ref[...], preferred_element_type=F32)
    h = (_silu(gate) * up).astype(BF16)
    y = jnp.dot(h, wd_ref[...], preferred_element_type=F32)
    o_ref[...] = _layer_norm(alpha * x + 0.5 * y, g_ref[...], b_ref[...])


def _ffn_ln(xa, xb, wg, wu, wd, g, b, alpha):
    d = xa.shape[1]
    dff = wg.shape[1]
    tm = TOKEN_TILE
    na, nb = xa.shape[0] // tm, xb.shape[0] // tm
    tok_a, tok_b = _group_blocks(tm, na)
    return pl.pallas_call(
        _by_group(functools.partial(_ffn_ln_kernel, alpha=alpha), na, 5, 1),
        out_shape=(jax.ShapeDtypeStruct(xa.shape, F32), jax.ShapeDtypeStruct(xb.shape, F32)),
        grid=(na + nb,),
        in_specs=[tok_a(d), tok_b(d),
                  _resident((d, dff)), _resident((d, dff)), _resident((dff, d)),
                  _resident((1, d)), _resident((1, d))],
        out_specs=(tok_a(d), tok_b(d)),
        compiler_params=_cparams("arbitrary"),
        name="ffn_ln",
    )(xa, xb, wg, wu, wd, g, b)


def _in_proj_kernel(x_ref, wa_ref, wr_ref, wt_ref, ba_ref, br_ref, bt_ref,
                    pa_ref, pr_ref, kat_ref, kbt_ref, kct_ref, gt_ref, ak_ref, av_ref, *, first_of_stack, kept_every):
    xb = x_ref[...].astype(BF16)
    pa = jnp.dot(xb, wa_ref[...], preferred_element_type=F32) + ba_ref[...]
    pa_ref[...] = pa

    def keep():
        for ref, col in ((ak_ref, W_GROUP), (av_ref, 2 * W_GROUP)):
            if first_of_stack is not None:
                for other in range(ref.shape[0]):
                    if other != first_of_stack:
                        ref[other] = jnp.zeros(ref.shape[1:], F32)
                ref = ref.at[first_of_stack]
            ref[...] = pa[:, col:col + W_GROUP] if kept_every is None else pa_ref[:, col:col + W_GROUP].T

    if kept_every is None:
        keep()
    pr_ref[...] = jnp.dot(xb, wr_ref[...], preferred_element_type=F32) + br_ref[...]
    tr = _dot_nt(wt_ref[...], xb) + bt_ref[...]
    kat_ref[...] = tr[0:W_GROUP].astype(BF16)
    kbt_ref[...] = tr[W_GROUP:2 * W_GROUP]
    kct_ref[...] = tr[2 * W_GROUP:3 * W_GROUP]
    gt_ref[...] = tr[3 * W_GROUP:3 * W_GROUP + N_GATES]
    if kept_every is not None:
        pl.when(pl.program_id(0) % kept_every == kept_every - 1)(keep)


def _in_proj(x, wa, wr, wt, ba, br, bt, seq_len, layer, n_layers, prev_kv=None, casts=()):
    n, d = x.shape
    ca, cr, ct = wa.shape[1], wr.shape[1], wt.shape[0]
    tm = TOKEN_TILE
    per = max(seq_len // tm, 1)
    long_seq = seq_len >= tm
    assert (seq_len % tm == 0 and A_REACH == tm) if long_seq else (tm % seq_len == 0 and seq_len <= A_REACH)
    tok = lambda w: pl.BlockSpec((tm, w), lambda i: (i, 0))
    trn = lambda r: pl.BlockSpec((r, tm), lambda i: (0, i))
    plans = [_cast_plan(stack, layer_, n // tm) for stack, layer_ in casts]
    in_specs = [tok(d), _resident((d, ca)), _resident((d, cr)), _resident((ct, d)),
                _resident((1, ca)), _resident((1, cr)), _resident((ct, 1)), *[p[0] for p in plans]]
    n_in = len(in_specs)
    if prev_kv is None:
        first_of_stack, aliases = layer, {}
        lead, at = n_layers, 0
    else:
        first_of_stack, aliases = None, {n_in: 6, n_in + 1: 7}
        lead, at = None, layer
        in_specs += [pl.BlockSpec(memory_space=pl.ANY)] * 2
    if long_seq:
        kept = pl.BlockSpec((lead, None, W_GROUP, tm), lambda i: (at, i // per, 0, 0))
        kv_shape = jax.ShapeDtypeStruct((n_layers, n // seq_len, W_GROUP, tm), F32)
    else:
        kept = pl.BlockSpec((lead, tm, W_GROUP), lambda i: (at, i, 0))
        kv_shape = jax.ShapeDtypeStruct((n_layers, n, W_GROUP), F32)
    inner = _with_casts(functools.partial(_in_proj_kernel, first_of_stack=first_of_stack,
                                          kept_every=per if long_seq else None), 7, 8, len(casts))
    body = lambda *refs: inner(*refs[:n_in], *refs[n_in + len(aliases):])
    out = pl.pallas_call(
        body,
        out_shape=(jax.ShapeDtypeStruct((n, ca), F32), jax.ShapeDtypeStruct((n, cr), F32),
                   jax.ShapeDtypeStruct((W_GROUP, n), BF16), jax.ShapeDtypeStruct((W_GROUP, n), F32),
                   jax.ShapeDtypeStruct((W_GROUP, n), F32), jax.ShapeDtypeStruct((N_GATES, n), F32),
                   kv_shape, kv_shape, *[p[2] for p in plans]),
        grid=(n // tm,),
        in_specs=in_specs,
        out_specs=(tok(ca), tok(cr), trn(W_GROUP), trn(W_GROUP), trn(W_GROUP), trn(N_GATES), kept, kept,
                   *[p[1] for p in plans]),
        input_output_aliases=aliases,
        compiler_params=_cparams("arbitrary"),
        name="in_proj",
    )(x, wa, wr, wt, ba, br, bt, *[stack for stack, _ in casts], *(prev_kv or ()))
    return out[:6], out[6:8], out[8:]


def _kv_result(stack, bsz, seq_len):
    n_layers = stack.shape[0]
    if seq_len >= TOKEN_TILE:
        return stack.reshape(n_layers, bsz, N_HEADS, HEAD_DIM, -1).transpose(0, 1, 4, 2, 3)
    return stack.reshape(n_layers, bsz, -1, N_HEADS, HEAD_DIM)


def _head_rows(xt, h, pos):
    xh = xt[h * HEAD_DIM:(h + 1) * HEAD_DIM]
    z = jnp.zeros_like(xh)
    return jnp.concatenate([xh, z] if pos == 0 else [z, xh], axis=0)


def _col_bcast(eye_b, rows):
    hi = rows.astype(BF16)
    lo = (rows - hi.astype(F32)).astype(BF16)
    dn = (((1,), (1,)), ((), ()))
    return (lax.dot_general(eye_b, hi, dn, preferred_element_type=F32)
            + lax.dot_general(eye_b, lo, dn, preferred_element_type=F32))


def _rep_heads(rows4, n):
    return jnp.concatenate([jnp.broadcast_to(rows4[h:h + 1], (n, rows4.shape[1])) for h in range(N_HEADS)], axis=0)


def _seg_scan(x, seg_pos, seg_len, op, fill):
    s = 1
    while s < seg_len:
        x = op(x, jnp.where(seg_pos >= s, pltpu.roll(x, s, 1), fill))
        s *= 2
    return x


def _shift_rows(x, s, fill):
    if s % 8 == 0:
        return jnp.concatenate([jnp.full((s, x.shape[1]), fill, x.dtype), x[:x.shape[0] - s]], axis=0)
    row = lax.broadcasted_iota(jnp.int32, x.shape, 0)
    return jnp.where(row >= s, pltpu.roll(x, s, 0), fill)


def _mixer_kernel(pa_ref, pr_ref, kat_ref, kbt_ref, kct_ref, gt_ref, pastk_ref, pastv_ref, rel_ref,
                  cos_ref, sin_ref, cost_ref, sint_ref, dec_ref, qdec_ref, kdect_ref, sdec_ref,
                  ret0_ref, c0_ref, n0_ref, m0_ref,
                  cw_ref, cb_ref, wa_ref, ba_ref, wi_ref, bi_ref, lam_ref, buf0_ref, h0_ref,
                  y_ref, ret_out, c_out, n_out, m_out, h_out, buf_out,
                  bias_sc, spair, ppair, m_sc, win, h_sc, kpadt=None, vpad=None,
                  *, blk, seqs, n_steps, has_past, first_of_stack):
    state_outs = (ret_out, c_out, n_out, m_out, h_out, buf_out)
    if first_of_stack is not None:
        ret_out, c_out, n_out, m_out, h_out, buf_out = [ref.at[first_of_stack[0]] for ref in state_outs]
    L = blk
    band = A_REACH + L
    n_past = A_REACH // L if not has_past else 0
    n_var = n_past + 1
    bb = pl.program_id(0)
    tb = pl.program_id(1)
    tail = CONV_W - 1
    if n_steps == 1:
        at_first = at_last = lambda f: f()
    else:
        at_first = pl.when(tb == 0)
        at_last = pl.when(tb == n_steps - 1)

    li = lax.broadcasted_iota(jnp.int32, (L, L), 0)
    lj = lax.broadcasted_iota(jnp.int32, (L, L), 1)
    causal_neg = jnp.where(lj <= li, 0.0, -jnp.inf)
    eye_b = jnp.where(li == lj, 1.0, 0.0).astype(BF16)
    lane_lo = lax.broadcasted_iota(jnp.int32, (L, SLAB), 1) < HEAD_DIM
    ones_slab = jnp.ones((L, SLAB), BF16)
    ri = lax.broadcasted_iota(jnp.int32, (SLAB, SLAB), 0) // HEAD_DIM
    rj = lax.broadcasted_iota(jnp.int32, (SLAB, SLAB), 1) // HEAD_DIM
    smask = jnp.where(ri == rj, 1.0, 0.0)
    pmask = jnp.concatenate([smask, smask], axis=1)
    gi = lax.broadcasted_iota(jnp.int32, (W_GROUP, W_GROUP), 0) // HEAD_DIM
    gj = lax.broadcasted_iota(jnp.int32, (W_GROUP, W_GROUP), 1) // HEAD_DIM
    gmat = jnp.where(gi == gj, 1.0 / HEAD_DIM, 0.0).astype(BF16)
    e64i = lax.broadcasted_iota(jnp.int32, (HEAD_DIM, HEAD_DIM), 0)
    e64j = lax.broadcasted_iota(jnp.int32, (HEAD_DIM, HEAD_DIM), 1)
    eye64 = e64i == e64j
    row_lo = lax.broadcasted_iota(jnp.int32, (SLAB, 1), 0) < HEAD_DIM

    @pl.when((bb == 0) & (tb == 0))
    def _():
        qq = lax.broadcasted_iota(jnp.int32, (L, band), 0)
        kk = lax.broadcasted_iota(jnp.int32, (L, band), 1)
        off = kk - (qq // CHUNK) * CHUNK
        for h in range(N_HEADS):
            row = jnp.broadcast_to(rel_ref[h], (L, ROLL_W))
            tile = pltpu.roll(row, ROLL_W - L, 1, stride=1, stride_axis=0)[:, :band] * LOG2E
            tile = jnp.where(off >= 0, jnp.where(off < A_REACH + CHUNK, tile, -jnp.inf), -jnp.inf)
            for v in range(n_var):
                first_col = (n_past - v) * L
                bias_sc[v * N_HEADS + h] = jnp.where(kk >= first_col, tile, -jnp.inf) if first_col > 0 else tile

    for s in range(seqs):
        @at_first
        def _():
            for p in range(N_PAIRS):
                sblk, cblk = [], []
                for hh in range(2):
                    h = 2 * p + hh
                    z = jnp.zeros((HEAD_DIM, HEAD_DIM), F32)
                    s_h = ret0_ref[s, h]
                    c_h = c0_ref[s, h]
                    n_row = n0_ref[s, h:h + 1, :]
                    n_col = jnp.sum(jnp.where(eye64, jnp.broadcast_to(n_row, eye64.shape), 0.0),
                                    axis=1, keepdims=True)
                    n_rep = jnp.broadcast_to(n_col, (HEAD_DIM, HEAD_DIM))
                    sblk.append(jnp.concatenate([s_h, z] if hh == 0 else [z, s_h], axis=1))
                    cblk.append(jnp.concatenate([c_h, z, n_rep, z] if hh == 0 else [z, c_h, z, n_rep], axis=1))
                spair[s, p] = jnp.concatenate(sblk, axis=0)
                ppair[s, p] = jnp.concatenate(cblk, axis=0)
            m_sc[s, 0:N_HEADS, :] = jnp.broadcast_to(m0_ref[s], (N_HEADS, m_sc.shape[2]))
            win[s, CONV_PAD - tail:CONV_PAD, :] = buf0_ref[s]
            h_sc[s] = h0_ref[s]
            if not has_past:
                vpad[s, 0:A_REACH, :] = jnp.zeros((A_REACH, W_GROUP), BF16)
                kpadt[s, 0:n_past * W_GROUP, :] = jnp.zeros((n_past * W_GROUP, L), BF16)

    g8 = gt_ref[...]
    seg_pos = lax.broadcasted_iota(jnp.int32, g8.shape, 1) % L
    b8 = _seg_scan(jax.nn.log_sigmoid(g8), seg_pos, L, jnp.add, 0.0)
    a8 = g8 - pltpu.roll(b8, N_HEADS, 0)
    cm8 = _seg_scan(a8, seg_pos, L, jnp.maximum, -jnp.inf)
    b8 = pltpu.roll(b8, N_HEADS, 0)

    heads = [(h // 2, h % 2, h) for h in range(N_HEADS)]
    slab = lambda p: slice(p * SLAB, (p + 1) * SLAB)
    seg_of = lambda s: slice(s * L, (s + 1) * L)
    col = lambda s, j: pr_ref[s, :, j * W_GROUP:(j + 1) * W_GROUP]

    def halves(per_head):
        return jnp.concatenate([jnp.where(lane_lo, per_head[2 * p], per_head[2 * p + 1]) for p in range(N_PAIRS)],
                               axis=1)


    def attention_chain(s):
        pa = pa_ref[s]
        kat = kat_ref[:, seg_of(s)]
        v_new = pa[:, 2 * W_GROUP:].astype(BF16)
        if has_past:
            kpast = pastk_ref[s].astype(BF16)
            vband = jnp.concatenate([pastv_ref[s].astype(BF16), v_new], axis=0)
            bias_at = lambda h: bias_sc[h]
        else:
            kpadt[s, pl.ds(pl.multiple_of((tb + n_past) * W_GROUP, W_GROUP), W_GROUP), :] = kat
            kwin = kpadt[s, pl.ds(pl.multiple_of(tb * W_GROUP, W_GROUP), (n_past + 1) * W_GROUP), :]
            start = pl.multiple_of(tb * L, L)
            vpad[s, pl.ds(A_REACH + start, L), :] = v_new
            vband = vpad[s, pl.ds(start, band), :]
            var = jnp.minimum(tb, n_var - 1) * N_HEADS
            bias_at = lambda h: bias_sc[var + h]
        qa = (pa[:, 0:W_GROUP] * (HEAD_DIM ** -0.5 * LOG2E)).astype(BF16)
        sc = {}
        for p, hh, h in heads:
            q_slab = qa[:, slab(p)]
            if has_past:
                k_slab = kpast[:, slab(p)]
                lane = lax.broadcasted_iota(jnp.int32, k_slab.shape, 1)
                lane_h = lane < HEAD_DIM if hh == 0 else lane >= HEAD_DIM
                parts = [_dot_nt(q_slab, jnp.where(lane_h, k_slab, jnp.zeros_like(k_slab))),
                         jnp.dot(q_slab, _head_rows(kat, h, hh), preferred_element_type=F32)]
            else:
                parts = [jnp.dot(q_slab, _head_rows(kwin[i * W_GROUP:(i + 1) * W_GROUP], h, hh),
                                 preferred_element_type=F32) for i in range(n_past + 1)]
            sc[h] = jnp.concatenate(parts, axis=1)
        yield
        pe, den = {}, {}
        for _, _, h in heads:
            sh = sc[h] + bias_at(h)
            pe[h] = jnp.exp2(sh - jnp.max(sh, axis=1, keepdims=True))
            den[h] = jnp.sum(pe[h], axis=1, keepdims=True)
        yield
        res = {h: _dot(pe[h], vband)[:, slab(p)] for p, _, h in heads}
        yield
        y_ref[s, :, 0:W_GROUP] = halves({h: res[h] / den[h] for _, _, h in heads})

    def retention_chain(s):
        qr = (col(s, PR_QB) * cos_ref[...] + col(s, PR_QBS) * sin_ref[...]).astype(BF16)
        kbt = kbt_ref[:, seg_of(s)]
        half = HEAD_DIM // 2
        kbt_sw = jnp.concatenate([kbt[h * HEAD_DIM + o:h * HEAD_DIM + o + half]
                                  for h in range(N_HEADS) for o in (half, 0)], axis=0)
        krt = (kbt * cost_ref[:, seg_of(s)] + kbt_sw * sint_ref[:, seg_of(s)]) * HEAD_DIM ** -0.5
        krt_b = krt.astype(BF16)
        kdt_b = (krt * kdect_ref[...]).astype(BF16)
        vb = col(s, PR_VB).astype(BF16)
        raw = {h: jnp.dot(qr[:, slab(p)], _head_rows(krt_b, h, hh), preferred_element_type=F32)
               for p, hh, h in heads}
        s0 = {p: spair[s, p] for p in range(N_PAIRS)}
        inter = {p: _dot(qr[:, slab(p)], s0[p]) for p in range(N_PAIRS)}
        add = {p: _dot(kdt_b[slab(p)], vb[:, slab(p)]) for p in range(N_PAIRS)}
        yield
        scores = {h: raw[h] * dec_ref[h] for _, _, h in heads}
        for p in range(N_PAIRS):
            spair[s, p] = s0[p] * sdec_ref[p] + add[p] * smask
        yield
        o = {h: _dot(scores[h], vb[:, slab(p)]) for p, _, h in heads}
        yield
        ob = halves(o) + jnp.concatenate([inter[p] for p in range(N_PAIRS)], axis=1) * qdec_ref[...]
        mu = _dot(ob, gmat)
        yield
        oc = ob - mu
        var = _dot(oc * oc, gmat)
        yield
        y_ref[s, :, W_GROUP:2 * W_GROUP] = oc * lax.rsqrt(var + LN_EPS) * _silu(col(s, PR_GB))

    def mlstm_chain(s):
        a4 = a8[0:N_HEADS, seg_of(s)]
        b4 = b8[0:N_HEADS, seg_of(s)]
        m0c = m_sc[s, 0:N_HEADS, 0:1]
        big_m = jnp.maximum(m0c, cm8[0:N_HEADS, seg_of(s)])
        m4 = b4 + big_m
        w0_4 = jnp.exp(m0c - big_m)
        m_last = big_m[:, L - 1:L]
        wrow4 = jnp.exp(a4 - m_last)
        kct = kct_ref[:, seg_of(s)] * HEAD_DIM ** -0.5
        kct_b = kct.astype(BF16)
        kw_b = (kct * _rep_heads(wrow4, HEAD_DIM)).astype(BF16)
        qc = col(s, PR_QC).astype(BF16)
        vc = col(s, PR_VC).astype(BF16)
        v_aug = {p: jnp.concatenate([vc[:, slab(p)], ones_slab], axis=1) for p in range(N_PAIRS)}
        p0 = {p: ppair[s, p] for p in range(N_PAIRS)}
        w0_bc = _col_bcast(eye_b, _rep_heads(w0_4, HEAD_DIM))
        floor_bc = _col_bcast(eye_b, _rep_heads(jnp.exp(-m4), HEAD_DIM))
        m_bc = {h: _col_bcast(eye_b, jnp.broadcast_to(big_m[h:h + 1], (L, L))) for _, _, h in heads}
        raw = {h: jnp.dot(qc[:, slab(p)], _head_rows(kct_b, h, hh), preferred_element_type=F32)
               for p, hh, h in heads}
        inter = {p: _dot(qc[:, slab(p)], p0[p]) for p in range(N_PAIRS)}
        add = {p: _dot(kw_b[slab(p)], v_aug[p]) for p in range(N_PAIRS)}
        yield
        qk = {h: raw[h] * jnp.exp(a4[h:h + 1] - m_bc[h] + causal_neg) for _, _, h in heads}
        for p in range(N_PAIRS):
            w_state = jnp.where(row_lo, w0_4[2 * p:2 * p + 1, L - 1:L], w0_4[2 * p + 1:2 * p + 2, L - 1:L])
            ppair[s, p] = p0[p] * w_state + add[p] * pmask
        m_sc[s, 0:N_HEADS, :] = jnp.broadcast_to(m4[:, L - 1:L], (N_HEADS, m_sc.shape[2]))
        yield
        res = {h: _dot(qk[h], v_aug[p]) for p, _, h in heads}
        yield
        hc = []
        for p in range(N_PAIRS):
            w0 = w0_bc[:, slab(p)]
            num = jnp.where(lane_lo, res[2 * p][:, 0:SLAB], res[2 * p + 1][:, 0:SLAB]) + inter[p][:, 0:SLAB] * w0
            den = jnp.where(lane_lo, res[2 * p][:, SLAB:], res[2 * p + 1][:, SLAB:]) + inter[p][:, SLAB:] * w0
            hc.append(num / jnp.maximum(jnp.abs(den), floor_bc[:, slab(p)]))
        hc = jnp.concatenate(hc, axis=1)
        mu = _dot(hc, gmat)
        yield
        hcc = hc - mu
        var = _dot(hcc * hcc, gmat)
        yield
        y_ref[s, :, 2 * W_GROUP:3 * W_GROUP] = hcc * lax.rsqrt(var + LN_EPS) * jax.nn.sigmoid(col(s, PR_OC))

    def rglru_chain(s):
        xd = col(s, PR_XD)
        win[s, CONV_PAD:CONV_PAD + L, :] = xd
        xc = cb_ref[...]
        for j in range(CONV_W):
            off = CONV_PAD - tail + j
            xc = xc + win[s, off:off + L, :] * cw_ref[j:j + 1, :]
        win[s, CONV_PAD - tail:CONV_PAD, :] = xd[L - tail:, :]
        r_pre = _dot(xc, wa_ref[...])
        i_pre = _dot(xc, wi_ref[...])
        yield
        r = jax.nn.sigmoid(r_pre + ba_ref[...])
        i = jax.nn.sigmoid(i_pre + bi_ref[...])
        neg_lam = -lam_ref[...]
        softplus = jnp.maximum(neg_lam, 0.0) + jnp.log1p(jnp.exp(-jnp.abs(neg_lam)))
        log_a = -LRU_C * r * softplus
        a = jnp.exp(log_a)
        u = jnp.sqrt(-jnp.tanh(log_a) * (jnp.exp(2.0 * log_a) + 1.0)) * (i * xc)
        st = 1
        while st < L:
            u = a * _shift_rows(u, st, 0.0) + u
            a = a * _shift_rows(a, st, 1.0)
            st *= 2
            if st in (8, 64):
                yield
        hseq = u + a * h_sc[s]
        h_sc[s] = hseq[L - 1:L, :]
        y_ref[s, :, 3 * W_GROUP:] = hseq * jax.nn.gelu(col(s, PR_YD))

    if seqs == 1:
        order, lag = (attention_chain, rglru_chain, mlstm_chain, retention_chain), 2
    else:
        order, lag = (attention_chain, retention_chain, mlstm_chain, rglru_chain), 0
    _run_interleaved([chain(s) for chain in order for s in range(seqs)], lag=lag)

    for s in range(seqs):
        @at_last
        def _():
            for p in range(N_PAIRS):
                sp = spair[s, p]
                pp = ppair[s, p]
                for hh in range(2):
                    h = 2 * p + hh
                    rs = slice(hh * HEAD_DIM, (hh + 1) * HEAD_DIM)
                    ret_out[s, h] = sp[rs, rs]
                    c_out[s, h] = pp[rs, rs]
                    n_rep = pp[rs, SLAB + hh * HEAD_DIM:SLAB + (hh + 1) * HEAD_DIM]
                    n_out[s, h:h + 1, :] = jnp.sum(jnp.where(eye64, n_rep, 0.0), axis=0, keepdims=True)
            m_out[s] = m_sc[s, 0:N_HEADS, 0:1]
            h_out[s] = h_sc[s]
            buf_out[s] = pr_ref[s, L - tail:, PR_XD * W_GROUP:(PR_XD + 1) * W_GROUP]
            if first_of_stack is not None:
                for other in first_of_stack[1]:
                    for ref in state_outs:
                        ref[other, s] = jnp.zeros(ref.shape[2:], F32)


def _mixer_kernel_fresh(pa_ref, pr_ref, kat_ref, kbt_ref, kct_ref, gt_ref, *rest, **static):
    _mixer_kernel(pa_ref, pr_ref, kat_ref, kbt_ref, kct_ref, gt_ref, None, None, *rest, **static)


def _mixer_geometry(t, has_past):
    if has_past:
        assert t == CHUNK, "sequences with a carried cache are expected to be a single chunk"
        return t, MIX_SEQS
    assert t % MIX_BLOCK == 0 and A_REACH % MIX_BLOCK == 0
    return MIX_BLOCK, 1


def _mixer_tables(pos, blk, seqs):
    cos, sin = _rotary_tables(pos)
    return (cos, sin, jnp.tile(cos.T, (1, seqs)), jnp.tile(sin.T, (1, seqs))) + _retention_tables(blk)


def _mixer(pa, pr, kat, kbt, kct, gt, past, rel_rows, tables, rec, lw, n_layers, prev_states=None):
    ret0, c0, n0, m0, h0, buf0, rec_layer = rec
    m0, h0 = m0[..., None], h0[:, :, None, :]
    bsz, t, _ = pa.shape
    has_past = past is not None
    blk, seqs = _mixer_geometry(t, has_past)
    nb = t // blk
    assert bsz % seqs == 0
    band = A_REACH + blk
    n_var = 1 if has_past else A_REACH // blk + 1
    tail = CONV_W - 1

    tok3 = lambda w: pl.BlockSpec((seqs, blk, w), lambda b, c: (b, c, 0))
    trn = lambda r: pl.BlockSpec((r, seqs * blk), lambda b, c: (0, b * nb + c))
    per_seq = lambda *shape: pl.BlockSpec((seqs,) + shape, lambda b, c: (b,) + (0,) * len(shape))
    const = lambda *shape: pl.BlockSpec(shape, lambda b, c: (0,) * len(shape))
    carried = lambda *shape: pl.BlockSpec((None, seqs) + shape, lambda b, c: (rec_layer, b) + (0,) * len(shape))
    past_specs, past_args = [], ()
    if has_past:
        layer = past[2]
        cache = pl.BlockSpec((None, seqs, A_REACH, W_GROUP), lambda b, c: (layer, b, 0, 0))
        past_specs, past_args = [cache, cache], past[:2]
    in_specs = [tok3(3 * W_GROUP), tok3(N_PR * W_GROUP), trn(W_GROUP), trn(W_GROUP), trn(W_GROUP), trn(N_GATES),
                *past_specs, const(N_HEADS, 1, ROLL_W),
                pl.BlockSpec((blk, W_GROUP), lambda b, c: (c, 0)), pl.BlockSpec((blk, W_GROUP), lambda b, c: (c, 0)),
                pl.BlockSpec((W_GROUP, seqs * blk), lambda b, c: (0, c)),
                pl.BlockSpec((W_GROUP, seqs * blk), lambda b, c: (0, c)),
                const(N_HEADS, blk, blk), const(blk, W_GROUP), const(W_GROUP, blk), const(N_PAIRS, SLAB, SLAB),
                carried(N_HEADS, HEAD_DIM, HEAD_DIM), carried(N_HEADS, HEAD_DIM, HEAD_DIM),
                carried(N_HEADS, HEAD_DIM), carried(N_HEADS, 1),
                const(CONV_W, W_GROUP), const(1, W_GROUP), const(W_GROUP, W_GROUP), const(1, W_GROUP),
                const(W_GROUP, W_GROUP), const(1, W_GROUP), const(1, W_GROUP),
                carried(tail, W_GROUP), carried(1, W_GROUP)]
    out_layer = lw['layer']
    state_shapes = ((N_HEADS, HEAD_DIM, HEAD_DIM), (N_HEADS, HEAD_DIM, HEAD_DIM), (N_HEADS, HEAD_DIM), (N_HEADS, 1),
                    (1, W_GROUP), (tail, W_GROUP))
    if prev_states is None:
        first_of_stack = (out_layer, tuple(l for l in range(n_layers) if l != out_layer))
        produced = lambda *shape: pl.BlockSpec((n_layers, seqs) + shape, lambda b, c: (0, b) + (0,) * len(shape))
    else:
        first_of_stack = None
        produced = lambda *shape: pl.BlockSpec((None, seqs) + shape, lambda b, c: (out_layer, b) + (0,) * len(shape))
    out_shape = (jax.ShapeDtypeStruct((bsz, t, 4 * W_GROUP), F32),
                 *[jax.ShapeDtypeStruct((n_layers, bsz) + sh, F32) for sh in state_shapes])
    out_specs = (tok3(4 * W_GROUP), *[produced(*sh) for sh in state_shapes])
    n_in = len(in_specs)
    aliases = {}
    if prev_states is not None:
        in_specs += [pl.BlockSpec(memory_space=pl.ANY)] * len(state_shapes)
        aliases = {n_in + k: 1 + k for k in range(len(state_shapes))}
    scratch = [pltpu.VMEM((n_var * N_HEADS, blk, band), F32),
               pltpu.VMEM((seqs, N_PAIRS, SLAB, SLAB), F32), pltpu.VMEM((seqs, N_PAIRS, SLAB, 2 * SLAB), F32),
               pltpu.VMEM((seqs, 8, 128), F32), pltpu.VMEM((seqs, CONV_PAD + blk, W_GROUP), F32),
               pltpu.VMEM((seqs, 1, W_GROUP), F32)]
    if not has_past:
        scratch += [pltpu.VMEM((seqs, (nb + A_REACH // blk) * W_GROUP, blk), BF16),
                    pltpu.VMEM((seqs, A_REACH + t, W_GROUP), BF16)]
    body = functools.partial(_mixer_kernel if has_past else _mixer_kernel_fresh,
                             blk=blk, seqs=seqs, n_steps=nb, has_past=has_past, first_of_stack=first_of_stack)
    if prev_states is not None:
        inner, n_alias = body, len(state_shapes)
        body = lambda *refs: inner(*refs[:n_in], *refs[n_in + n_alias:])
    return pl.pallas_call(
        body,
        out_shape=out_shape,
        grid=(bsz // seqs, nb),
        in_specs=in_specs,
        out_specs=out_specs,
        scratch_shapes=scratch,
        input_output_aliases=aliases,
        compiler_params=_cparams("arbitrary", "arbitrary"),
        name="mixer",
    )(pa, pr, kat, kbt, kct, gt, *past_args, rel_rows, *tables,
      ret0, c0, n0, m0, lw['conv_w'], lw['conv_b'], lw['lru_wa'], lw['lru_ba'], lw['lru_wi'], lw['lru_bi'],
      lw['lru_lambda'], buf0, h0, *(prev_states or ()))


def _post_kernel(x_ref, y_ref, wo_ref, g1_ref, b1_ref, wq_ref, mk_ref, mv_ref, xo_ref,
                 g2_ref, b2_ref, o_ref, att_sc, *, alpha, seqs, subtiles):
    rows_sub = x_ref.shape[0] // subtiles
    rows = rows_sub // seqs
    head = lambda h: slice(h * HEAD_DIM, (h + 1) * HEAD_DIM)

    def subtile_chain(i):
        rs = slice(i * rows_sub, (i + 1) * rows_sub)
        x = x_ref[rs, :]
        mix = _dot(y_ref[rs, :], wo_ref[...])
        yield
        x2 = _layer_norm(alpha * x + mix, g1_ref[...], b1_ref[...])
        yield
        q = _dot(x2, wq_ref[...]).astype(BF16)
        yield

        def attend(s, h):
            qs = slice(s * rows, (s + 1) * rows)
            sc = _dot_nt(q[qs, head(h)], mk_ref[s, :, head(h)]) * HEAD_DIM ** -0.5
            yield
            p = jnp.exp(sc - jnp.max(sc, axis=1, keepdims=True))
            den = jnp.sum(p, axis=1, keepdims=True)
            yield
            o = _dot(p, mv_ref[s, :, head(h)])
            yield
            att_sc[i * rows_sub + s * rows:i * rows_sub + (s + 1) * rows, head(h)] = o / den

        yield from _lockstep([attend(s, h) for s in range(seqs) for h in range(N_HEADS)])
        out = _dot(att_sc[rs, :], xo_ref[...])
        yield
        o_ref[rs, :] = _layer_norm(alpha * x2 + out, g2_ref[...], b2_ref[...])

    _run_interleaved([subtile_chain(i) for i in range(subtiles)], lag=2)


def _post(x, y, wo, wq, xo, wlayer, g1, b1, g2, b2, mk, mv, layer, alpha, seq_len, casts=()):
    n, d = x.shape
    n_mem, dx = mk.shape[2], mk.shape[3]
    if seq_len >= POST_SUBTILES * TOKEN_TILE:
        subtiles, seqs = POST_SUBTILES, 1
        tm = subtiles * TOKEN_TILE
        mem = pl.BlockSpec((None, 1, n_mem, dx), lambda i: (layer, i // (seq_len // tm), 0, 0))
    else:
        tm = TOKEN_TILE
        subtiles, seqs = 1, tm // seq_len
        mem = pl.BlockSpec((None, seqs, n_mem, dx), lambda i: (layer, i, 0, 0))
    assert n % tm == 0 and tm % (subtiles * seqs) == 0
    tok = lambda w: pl.BlockSpec((tm, w), lambda i: (i, 0))
    plans = [_cast_plan(stack, lyr, n // tm) for stack, lyr in casts]
    out = pl.pallas_call(
        _with_casts(functools.partial(_post_kernel, alpha=alpha, seqs=seqs, subtiles=subtiles), 11, 1, len(casts)),
        out_shape=(jax.ShapeDtypeStruct((n, d), F32), *[p[2] for p in plans]),
        grid=(n // tm,),
        in_specs=[tok(d), tok(y.shape[1]),
                  _resident(wo.shape[1:], wlayer), _resident((1, d)), _resident((1, d)),
                  _resident(wq.shape[1:], wlayer), mem, mem, _resident(xo.shape[1:], wlayer),
                  _resident((1, d)), _resident((1, d)), *[p[0] for p in plans]],
        out_specs=(tok(d), *[p[1] for p in plans]),
        scratch_shapes=[pltpu.VMEM((tm, dx), F32)],
        compiler_params=_cparams("parallel"),
        name="post",
    )(x, y, wo, g1, b1, wq, mk, mv, xo, g2, b2, *[stack for stack, _ in casts])
    return out[0], out[1:]


def _mem_kv_kernel(x_ref, w_ref, o_ref):
    o_ref[...] = _dot(x_ref[...], w_ref[...])


def _mem_kv(mem, w):
    n, d = mem.shape
    tm = min(TOKEN_TILE, n)
    return pl.pallas_call(
        _mem_kv_kernel,
        out_shape=jax.ShapeDtypeStruct((n, w.shape[1]), F32),
        grid=(n // tm,),
        in_specs=[pl.BlockSpec((tm, d), lambda i: (i, 0)), _resident(w.shape)],
        out_specs=pl.BlockSpec((tm, w.shape[1]), lambda i: (i, 0)),
        compiler_params=_cparams("parallel"),
        name="mem_kv",
    )(mem, w)


def _rotary_tables(pos):
    half = HEAD_DIM // 2
    inv = jnp.exp(-jnp.log(10000.0) * jnp.arange(half, dtype=F32) / half)
    ang = pos.astype(F32)[:, None] * inv[None, :]
    cos, sin = jnp.cos(ang), jnp.sin(ang)
    cos_full = jnp.tile(jnp.concatenate([cos, cos], -1), (1, N_HEADS))
    sin_full = jnp.tile(jnp.concatenate([-sin, sin], -1), (1, N_HEADS))
    return cos_full, sin_full


def _retention_tables(blk):
    log_g = jnp.log1p(-jnp.exp2(-5.0 - jnp.arange(N_HEADS, dtype=F32)))
    idx = jnp.arange(blk, dtype=F32)
    diff = idx[:, None] - idx[None, :]
    dec = jnp.exp(jnp.where((diff >= 0)[None], diff[None] * log_g[:, None, None], -jnp.inf))
    q_dec = jnp.exp((idx[:, None] + 1.0) * log_g[None, :])
    k_dec = jnp.exp((blk - 1.0 - idx)[:, None] * log_g[None, :])
    s_dec = jnp.exp(blk * log_g)
    lanes = lambda tbl: jnp.repeat(tbl, HEAD_DIM, axis=1)
    pair = lambda p: jnp.kron(jnp.diag(s_dec[2 * p:2 * p + 2]), jnp.ones((HEAD_DIM, HEAD_DIM), F32))
    return dec, lanes(q_dec), lanes(k_dec).T, jnp.stack([pair(p) for p in range(N_PAIRS)])


def _rel_bias_rows(rel_bias, blk):
    idx = np.clip(A_REACH + blk - np.arange(ROLL_W), -REL_CLIP, REL_CLIP) + REL_CLIP
    h, n_rel = rel_bias.shape
    lead = A_REACH + blk - REL_CLIP
    rest = ROLL_W - lead - n_rel
    if lead >= 0 and rest >= 0:
        assert np.array_equal(idx, np.concatenate([np.full(lead, n_rel - 1), np.arange(n_rel)[::-1],
                                                   np.zeros(rest, np.int64)]))
        rows = jnp.concatenate([jnp.broadcast_to(rel_bias[:, n_rel - 1:], (h, lead)), rel_bias[:, ::-1],
                                jnp.broadcast_to(rel_bias[:, :1], (h, rest))], axis=1)
    else:
        rows = rel_bias[:, idx]
    return rows[:, None, :]


def _block_diag(w):
    h, c, _ = w.shape
    eye = jnp.eye(h, dtype=w.dtype)
    return (eye[:, None, :, None] * w[:, :, None, :]).reshape(h * c, h * c)


def _swap_halves(a):
    half = HEAD_DIM // 2
    return jnp.concatenate([a[..., h * HEAD_DIM + s:h * HEAD_DIM + s + half]
                            for h in range(N_HEADS) for s in (half, 0)], axis=-1)


def _mix_and_post(x1, proj, shape, tables, mem, past, rec, lw, alpha, n_layers, prev_states, post_casts=()):
    bsz, t, d = shape
    n = bsz * t
    g, b = lw['ln_g'], lw['ln_b']
    pa, pr, kat, kbt, kct, gt = proj
    pa = pa.reshape(bsz, t, 3 * W_GROUP)
    pr = pr.reshape(bsz, t, N_PR * W_GROUP)
    blk, _ = _mixer_geometry(t, past is not None)
    y, *rec_stacks = _mixer(pa, pr, kat, kbt, kct, gt, past, _rel_bias_rows(lw['rel_bias'], blk), tables, rec, lw,
                            n_layers, prev_states)
    x3, post_cast_out = _post(x1, y.reshape(n, 4 * W_GROUP), lw['w_out'], lw['x_wq'], lw['x_wo'], lw['layer'],
                              g[1:2], b[1:2], g[2:3], b[2:3], *mem, alpha, t, post_casts)
    return x3, rec_stacks, post_cast_out


def _result_states(stacks):
    ret, c, n, m, h, conv = stacks
    return [ret, c, n, m[..., 0], h[:, :, 0, :], conv]


def kernel(x_prompt, x_sample, mem_prompt, cache_a_k, cache_a_v, state_ret, state_mlstm_c, state_mlstm_n,
           state_mlstm_m, state_lru_h, state_conv, cache_mem_k, cache_mem_v, ln_g, ln_b,
           ffn1_gate, ffn1_up, ffn1_down, ffn2_gate, ffn2_up, ffn2_down, w_in, b_in, a_rel_bias,
           conv_w, conv_b, lru_wa, lru_ba, lru_wi, lru_bi, lru_lambda, w_out, x_wq, x_wk, x_wv, x_wo):
    depth = ln_g.shape[0]
    alpha = (2.0 * depth) ** 0.25
    row = lambda v: v[None, :]
    grp = lambda a, j: a[..., j * W_GROUP:(j + 1) * W_GROUP]
    QA, KA, VA, QB, KB, VB, GB, QC, KC, VC, OC, XD, YD = range(13)

    def layer_weights(l):
        w, bias = w_in[l], b_in[l]
        nat = lambda a: jnp.concatenate(
            [grp(a, QB), _swap_halves(grp(a, QB)), grp(a, VB), grp(a, GB), grp(a, QC), grp(a, VC), grp(a, OC),
             grp(a, XD), grp(a, YD)], axis=-1)
        trn = lambda a: jnp.concatenate([grp(a, KA), grp(a, KB), grp(a, KC), a[..., 13 * W_GROUP:]], axis=-1)
        return {'ln_g': ln_g[l], 'ln_b': ln_b[l],
                'layer': l, **stacks,
                'w_in_a': w[:, :3 * W_GROUP].astype(BF16), 'w_in_r': nat(w).astype(BF16),
                'w_in_t': trn(w).T.astype(BF16),
                'b_in_a': row(bias[:3 * W_GROUP]), 'b_in_r': row(nat(bias)), 'b_in_t': trn(bias)[:, None],
                'rel_bias': a_rel_bias[l],
                'conv_w': conv_w[l], 'conv_b': row(conv_b[l]),
                'lru_wa': _block_diag(lru_wa[l]).astype(BF16), 'lru_ba': row(lru_ba[l]),
                'lru_wi': _block_diag(lru_wi[l]).astype(BF16), 'lru_bi': row(lru_bi[l]),
                'lru_lambda': row(lru_lambda[l]),
                'x_wkv': jnp.concatenate([x_wk[l], x_wv[l]], axis=1).astype(BF16)}

    stacks = {name: w.astype(BF16) for name, w in (('w_out', w_out), ('x_wq', x_wq), ('x_wo', x_wo))}
    ffn1_f32, ffn2_f32 = (ffn1_gate, ffn1_up, ffn1_down), (ffn2_gate, ffn2_up, ffn2_down)
    ffn1_w = {0: tuple(w[0].astype(BF16) for w in ffn1_f32)}
    weights = [layer_weights(l) for l in range(depth)]
    dx = x_wk.shape[2]

    bp, tp, d = x_prompt.shape
    bs, ts, _ = x_sample.shape
    n_mem = mem_prompt.shape[1]
    tables_p = _mixer_tables(jnp.arange(tp), *_mixer_geometry(tp, False))
    rec0 = (jnp.zeros((1, bp, N_HEADS, HEAD_DIM, HEAD_DIM), F32), jnp.zeros((1, bp, N_HEADS, HEAD_DIM, HEAD_DIM), F32),
            jnp.zeros((1, bp, N_HEADS, HEAD_DIM), F32), jnp.zeros((1, bp, N_HEADS), F32),
            jnp.zeros((1, bp, W_GROUP), F32), jnp.zeros((1, bp, CONV_W - 1, W_GROUP), F32), 0)
    tables_s = _mixer_tables(PAST_LEN + jnp.arange(ts), *_mixer_geometry(ts, True))
    past_k_all = cache_a_k.reshape(depth, bs, A_REACH, W_GROUP)
    past_v_all = cache_a_v.reshape(depth, bs, A_REACH, W_GROUP)
    mem_k_all = cache_mem_k.reshape(depth, bs, n_mem, dx)
    mem_v_all = cache_mem_v.reshape(depth, bs, n_mem, dx)

    hp, hs = x_prompt.reshape(bp * tp, d), x_sample.reshape(bs * ts, d)
    mem_states, rec_p, rec_s, kv_p, kv_s = [], None, None, None, None
    for l in range(depth):
        lw = weights[l]
        g, b = lw['ln_g'], lw['ln_b']
        mkv = _mem_kv(mem_prompt.reshape(bp * n_mem, d), lw['x_wkv'])
        mk = mkv[:, :dx].reshape(bp, n_mem, dx)
        mv = mkv[:, dx:].reshape(bp, n_mem, dx)
        x1p, x1s = _ffn_ln(hp, hs, *ffn1_w[l], g[0:1], b[0:1], alpha)
        w_in = (lw['w_in_a'], lw['w_in_r'], lw['w_in_t'], lw['b_in_a'], lw['b_in_r'], lw['b_in_t'])
        proj_p, kv_p, ffn2 = _in_proj(x1p, *w_in, tp, l, depth, kv_p, [(w, l) for w in ffn2_f32])
        proj_s, kv_s, _ = _in_proj(x1s, *w_in, ts, l, depth, kv_s)
        post_casts = [(w, l + 1) for w in ffn1_f32] if l + 1 < depth else []
        x3p, rec_p, cast_next = _mix_and_post(x1p, proj_p, (bp, tp, d), tables_p, (mk[None], mv[None], 0), None,
                                              rec0, lw, alpha, depth, rec_p, post_casts)
        rec = (state_ret, state_mlstm_c, state_mlstm_n, state_mlstm_m, state_lru_h, state_conv, l)
        x3s, rec_s, _ = _mix_and_post(x1s, proj_s, (bs, ts, d), tables_s, (mem_k_all, mem_v_all, l),
                                      (past_k_all, past_v_all, l), rec, lw, alpha, depth, rec_s)
        if post_casts:
            ffn1_w[l + 1] = cast_next
        hp, hs = _ffn_ln(x3p, x3s, *ffn2, g[3:4], b[3:4], alpha)
        mem_states.append((mk.reshape(bp, n_mem, -1, HEAD_DIM), mv.reshape(bp, n_mem, -1, HEAD_DIM)))
    y_prompt, y_sample = hp.reshape(bp, tp, d), hs.reshape(bs, ts, d)
    p_mk, p_mv = [jnp.stack(f) for f in zip(*mem_states)]
    prompt_out = [*[_kv_result(a, bp, tp) for a in kv_p], *_result_states(rec_p), p_mk, p_mv]
    sample_out = [_kv_result(a, bs, ts) for a in kv_s] + _result_states(rec_s)

    return (y_prompt, y_sample, *prompt_out, *sample_out)
```

```python
import functools

import numpy as np
import jax
import jax.numpy as jnp
from jax import lax
from jax.experimental import pallas as pl
from jax.experimental.pallas import tpu as pltpu

F32 = jnp.float32
BF16 = jnp.bfloat16

CHUNK = 64
HEAD_DIM = 64
N_HEADS = 4
W_GROUP = N_HEADS * HEAD_DIM
SLAB = 2 * HEAD_DIM
N_PAIRS = N_HEADS // 2
A_BAND_CHUNKS = 8
A_REACH = A_BAND_CHUNKS * CHUNK
REL_CLIP = 128
FFN_CHUNK = 256
CONV_W = 4
LRU_C = 8.0
LN_EPS = 1e-5
LOG2E = 1.4426950408889634
N_GATES = 2 * N_HEADS
PAST_LEN = 4096
ROLL_W = 1024
CONV_PAD = 8

V7X_VMEM_LIMIT_BYTES = 56 * 1024 * 1024
TOKEN_TILE = 512
MIX_BLOCK = 256
MIX_SEQS = 4
POST_SUBTILES = 2

PR_QB, PR_QBS, PR_VB, PR_GB, PR_QC, PR_VC, PR_OC, PR_XD, PR_YD = range(9)
N_PR = 9


def _cparams(*sem):
    return pltpu.CompilerParams(dimension_semantics=sem, vmem_limit_bytes=V7X_VMEM_LIMIT_BYTES)


def _dot(a, b):
    return jnp.dot(a.astype(BF16), b.astype(BF16), preferred_element_type=F32)


def _dot_nt(a, b):
    return lax.dot_general(a.astype(BF16), b.astype(BF16), (((1,), (1,)), ((), ())), preferred_element_type=F32)


def _layer_norm(x, g, b):
    mu = jnp.mean(x, -1, keepdims=True)
    xc = x - mu
    var = jnp.mean(xc * xc, -1, keepdims=True)
    return xc * lax.rsqrt(var + LN_EPS) * g + b


def _silu(x):
    return x * jax.nn.sigmoid(x)


def _lockstep(chains):
    while chains:
        chains = [c for c in chains if next(c, "done") != "done"]
        yield


def _run_interleaved(chains, lag=0):
    pending, live, rnd = list(chains), [], 0
    while pending or live:
        while pending and rnd >= lag * (len(chains) - len(pending)):
            live.append(pending.pop(0))
        live = [c for c in live if next(c, "done") != "done"]
        rnd += 1


def _resident(shape, layer=None):
    nd = len(shape)
    if layer is None:
        return pl.BlockSpec(shape, lambda *_: (0,) * nd, pipeline_mode=pl.Buffered(1))
    return pl.BlockSpec((None,) + tuple(shape), lambda *_: (layer,) + (0,) * nd, pipeline_mode=pl.Buffered(1))


def _cast_plan(stack, layer, n_steps):
    _, r, c = stack.shape
    nb = next(k for k in range(n_steps, 0, -1) if n_steps % k == 0 and r % k == 0 and (r // k) % 16 == 0)
    every = n_steps // nb
    return (pl.BlockSpec((None, r // nb, c), lambda i: (layer, i // every, 0)),
            pl.BlockSpec((r // nb, c), lambda i: (i // every, 0)),
            jax.ShapeDtypeStruct((r, c), BF16))


def _with_casts(body, n_in, n_out, n_cast):
    def wrapped(*refs):
        ins, rest = refs[:n_in + n_cast], refs[n_in + n_cast:]
        outs, scratch = rest[:n_out + n_cast], rest[n_out + n_cast:]
        body(*ins[:n_in], *outs[:n_out], *scratch)
        for src, dst in zip(ins[n_in:], outs[n_out:]):
            dst[...] = src[...].astype(BF16)
    return wrapped


def _by_group(body, n_first, n_shared, n_out):
    def wrapped(xa_ref, xb_ref, *refs):
        shared, outs = refs[:n_shared], refs[n_shared:]
        i = pl.program_id(0)

        @pl.when(i < n_first)
        def _():
            body(xa_ref, *shared, *outs[:n_out])

        @pl.when(i >= n_first)
        def _():
            body(xb_ref, *shared, *outs[n_out:2 * n_out])
    return wrapped


def _group_blocks(tm, n_first):
    tok_a = lambda w: pl.BlockSpec((tm, w), lambda i: (jnp.minimum(i, n_first - 1), 0))
    tok_b = lambda w: pl.BlockSpec((tm, w), lambda i: (jnp.maximum(i - n_first, 0), 0))
    return tok_a, tok_b


def _ffn_ln_kernel(x_ref, wg_ref, wu_ref, wd_ref, g_ref, b_ref, o_ref, *, alpha):
    x = x_ref[...]
    xb = x.astype(BF16)
    dff = wg_ref.shape[1]
    cw = FFN_CHUNK if dff % FFN_CHUNK == 0 else dff
    y = None
    for c in range(0, dff, cw):
        gate = jnp.dot(xb, wg_ref[:, c:c + cw], preferred_element_type=F32)
        up = jnp.dot(xb, wu_ref[:, c:c + cw], preferred_element_type=F32)
        h = (_silu(gate) * up).astype(BF16)
        part = jnp.dot(h, wd_ref[c:c + cw, :], preferred_element_type=F32)
        y = part if y is None else y + part
    o_ref[...] = _layer_norm(alpha * x + 0.5 * y, g_ref[...], b_ref[...])


def _ffn_ln(xa, xb, wg, wu, wd, g, b, alpha):
    d = xa.shape[1]
    dff = wg.shape[1]
    tm = TOKEN_TILE
    na, nb = xa.shape[0] // tm, xb.shape[0] // tm
    tok_a, tok_b = _group_blocks(tm, na)
    return pl.pallas_call(
        _by_group(functools.partial(_ffn_ln_kernel, alpha=alpha), na, 5, 1),
        out_shape=(jax.ShapeDtypeStruct(xa.shape, F32), jax.ShapeDtypeStruct(xb.shape, F32)),
        grid=(na + nb,),
        in_specs=[tok_a(d), tok_b(d),
                  _resident((d, dff)), _resident((d, dff)), _resident((dff, d)),
                  _resident((1, d)), _resident((1, d))],
        out_specs=(tok_a(d), tok_b(d)),
        compiler_params=_cparams("arbitrary"),
        name="ffn_ln",
    )(xa, xb, wg, wu, wd, g, b)


def _in_proj_kernel(x_ref, wa_ref, wr_ref, wt_ref, ba_ref, br_ref, bt_ref,
                    pa_ref, pr_ref, kat_ref, kbt_ref, kct_ref, gt_ref, ak_ref, av_ref, *, first_of_stack, kept_every):
    xb = x_ref[...].astype(BF16)
    pa = jnp.dot(xb, wa_ref[...], preferred_element_type=F32) + ba_ref[...]
    pa_ref[...] = pa

    def keep():
        for ref, col in ((ak_ref, W_GROUP), (av_ref, 2 * W_GROUP)):
            if first_of_stack is not None:
                for other in range(ref.shape[0]):
                    if other != first_of_stack:
                        ref[other] = jnp.zeros(ref.shape[1:], F32)
                ref = ref.at[first_of_stack]
            ref[...] = pa[:, col:col + W_GROUP] if kept_every is None else pa_ref[:, col:col + W_GROUP].T

    if kept_every is None:
        keep()
    pr_ref[...] = jnp.dot(xb, wr_ref[...], preferred_element_type=F32) + br_ref[...]
    tr = _dot_nt(wt_ref[...], xb) + bt_ref[...]
    kat_ref[...] = tr[0:W_GROUP].astype(BF16)
    kbt_ref[...] = tr[W_GROUP:2 * W_GROUP]
    kct_ref[...] = tr[2 * W_GROUP:3 * W_GROUP]
    gt_ref[...] = tr[3 * W_GROUP:3 * W_GROUP + N_GATES]
    if kept_every is not None:
        pl.when(pl.program_id(0) % kept_every == kept_every - 1)(keep)


def _in_proj(x, wa, wr, wt, ba, br, bt, seq_len, layer, n_layers, prev_kv=None, casts=()):
    n, d = x.shape
    ca, cr, ct = wa.shape[1], wr.shape[1], wt.shape[0]
    tm = TOKEN_TILE
    per = max(seq_len // tm, 1)
    long_seq = seq_len >= tm
    assert (seq_len % tm == 0 and A_REACH == tm) if long_seq else (tm % seq_len == 0 and seq_len <= A_REACH)
    tok = lambda w: pl.BlockSpec((tm, w), lambda i: (i, 0))
    trn = lambda r: pl.BlockSpec((r, tm), lambda i: (0, i))
    plans = [_cast_plan(stack, layer_, n // tm) for stack, layer_ in casts]
    in_specs = [tok(d), _resident((d, ca)), _resident((d, cr)), _resident((ct, d)),
                _resident((1, ca)), _resident((1, cr)), _resident((ct, 1)), *[p[0] for p in plans]]
    n_in = len(in_specs)
    if prev_kv is None:
        first_of_stack, aliases = layer, {}
        lead, at = n_layers, 0
    else:
        first_of_stack, aliases = None, {n_in: 6, n_in + 1: 7}
        lead, at = None, layer
        in_specs += [pl.BlockSpec(memory_space=pl.ANY)] * 2
    if long_seq:
        kept = pl.BlockSpec((lead, None, W_GROUP, tm), lambda i: (at, i // per, 0, 0))
        kv_shape = jax.ShapeDtypeStruct((n_layers, n // seq_len, W_GROUP, tm), F32)
    else:
        kept = pl.BlockSpec((lead, tm, W_GROUP), lambda i: (at, i, 0))
        kv_shape = jax.ShapeDtypeStruct((n_layers, n, W_GROUP), F32)
    inner = _with_casts(functools.partial(_in_proj_kernel, first_of_stack=first_of_stack,
                                          kept_every=per if long_seq else None), 7, 8, len(casts))
    body = lambda *refs: inner(*refs[:n_in], *refs[n_in + len(aliases):])
    out = pl.pallas_call(
        body,
        out_shape=(jax.ShapeDtypeStruct((n, ca), F32), jax.ShapeDtypeStruct((n, cr), F32),
                   jax.ShapeDtypeStruct((W_GROUP, n), BF16), jax.ShapeDtypeStruct((W_GROUP, n), F32),
                   jax.ShapeDtypeStruct((W_GROUP, n), F32), jax.ShapeDtypeStruct((N_GATES, n), F32),
                   kv_shape, kv_shape, *[p[2] for p in plans]),
        grid=(n // tm,),
        in_specs=in_specs,
        out_specs=(tok(ca), tok(cr), trn(W_GROUP), trn(W_GROUP), trn(W_GROUP), trn(N_GATES), kept, kept,
                   *[p[1] for p in plans]),
        input_output_aliases=aliases,
        compiler_params=_cparams("arbitrary"),
        name="in_proj",
    )(x, wa, wr, wt, ba, br, bt, *[stack for stack, _ in casts], *(prev_kv or ()))
    return out[:6], out[6:8], out[8:]


def _kv_result(stack, bsz, seq_len):
    n_layers = stack.shape[0]
    if seq_len >= TOKEN_TILE:
        return stack.reshape(n_layers, bsz, N_HEADS, HEAD_DIM, -1).transpose(0, 1, 4, 2, 3)
    return stack.reshape(n_layers, bsz, -1, N_HEADS, HEAD_DIM)


def _head_rows(xt, h, pos):
    xh = xt[h * HEAD_DIM:(h + 1) * HEAD_DIM]
    z = jnp.zeros_like(xh)
    return jnp.concatenate([xh, z] if pos == 0 else [z, xh], axis=0)


def _col_bcast(eye_b, rows):
    hi = rows.astype(BF16)
    lo = (rows - hi.astype(F32)).astype(BF16)
    dn = (((1,), (1,)), ((), ()))
    return (lax.dot_general(eye_b, hi, dn, preferred_element_type=F32)
            + lax.dot_general(eye_b, lo, dn, preferred_element_type=F32))


def _rep_heads(rows4, n):
    return jnp.concatenate([jnp.broadcast_to(rows4[h:h + 1], (n, rows4.shape[1])) for h in range(N_HEADS)], axis=0)


def _seg_scan(x, seg_pos, seg_len, op, fill):
    s = 1
    while s < seg_len:
        x = op(x, jnp.where(seg_pos >= s, pltpu.roll(x, s, 1), fill))
        s *= 2
    return x


def _shift_rows(x, s, fill):
    if s % 8 == 0:
        return jnp.concatenate([jnp.full((s, x.shape[1]), fill, x.dtype), x[:x.shape[0] - s]], axis=0)
    row = lax.broadcasted_iota(jnp.int32, x.shape, 0)
    return jnp.where(row >= s, pltpu.roll(x, s, 0), fill)


def _mixer_kernel(pa_ref, pr_ref, kat_ref, kbt_ref, kct_ref, gt_ref, pastk_ref, pastv_ref, rel_ref,
                  cos_ref, sin_ref, cost_ref, sint_ref, dec_ref, qdec_ref, kdect_ref, sdec_ref,
                  ret0_ref, c0_ref, n0_ref, m0_ref,
                  cw_ref, cb_ref, wa_ref, ba_ref, wi_ref, bi_ref, lam_ref, buf0_ref, h0_ref,
                  y_ref, ret_out, c_out, n_out, m_out, h_out, buf_out,
                  bias_sc, spair, ppair, m_sc, win, h_sc, kpadt=None, vpad=None,
                  *, blk, seqs, n_steps, has_past, first_of_stack):
    state_outs = (ret_out, c_out, n_out, m_out, h_out, buf_out)
    if first_of_stack is not None:
        ret_out, c_out, n_out, m_out, h_out, buf_out = [ref.at[first_of_stack[0]] for ref in state_outs]
    L = blk
    band = A_REACH + L
    n_past = A_REACH // L if not has_past else 0
    n_var = n_past + 1
    bb = pl.program_id(0)
    tb = pl.program_id(1)
    tail = CONV_W - 1
    if n_steps == 1:
        at_first = at_last = lambda f: f()
    else:
        at_first = pl.when(tb == 0)
        at_last = pl.when(tb == n_steps - 1)

    li = lax.broadcasted_iota(jnp.int32, (L, L), 0)
    lj = lax.broadcasted_iota(jnp.int32, (L, L), 1)
    causal_neg = jnp.where(lj <= li, 0.0, -jnp.inf)
    eye_b = jnp.where(li == lj, 1.0, 0.0).astype(BF16)
    lane_lo = lax.broadcasted_iota(jnp.int32, (L, SLAB), 1) < HEAD_DIM
    ones_slab = jnp.ones((L, SLAB), BF16)
    ri = lax.broadcasted_iota(jnp.int32, (SLAB, SLAB), 0) // HEAD_DIM
    rj = lax.broadcasted_iota(jnp.int32, (SLAB, SLAB), 1) // HEAD_DIM
    smask = jnp.where(ri == rj, 1.0, 0.0)
    pmask = jnp.concatenate([smask, smask], axis=1)
    gi = lax.broadcasted_iota(jnp.int32, (W_GROUP, W_GROUP), 0) // HEAD_DIM
    gj = lax.broadcasted_iota(jnp.int32, (W_GROUP, W_GROUP), 1) // HEAD_DIM
    gmat = jnp.where(gi == gj, 1.0 / HEAD_DIM, 0.0).astype(BF16)
    e64i = lax.broadcasted_iota(jnp.int32, (HEAD_DIM, HEAD_DIM), 0)
    e64j = lax.broadcasted_iota(jnp.int32, (HEAD_DIM, HEAD_DIM), 1)
    eye64 = e64i == e64j
    row_lo = lax.broadcasted_iota(jnp.int32, (SLAB, 1), 0) < HEAD_DIM

    @pl.when((bb == 0) & (tb == 0))
    def _():
        qq = lax.broadcasted_iota(jnp.int32, (L, band), 0)
        kk = lax.broadcasted_iota(jnp.int32, (L, band), 1)
        off = kk - (qq // CHUNK) * CHUNK
        for h in range(N_HEADS):
            row = jnp.broadcast_to(rel_ref[h], (L, ROLL_W))
            tile = pltpu.roll(row, ROLL_W - L, 1, stride=1, stride_axis=0)[:, :band] * LOG2E
            tile = jnp.where(off >= 0, jnp.where(off < A_REACH + CHUNK, tile, -jnp.inf), -jnp.inf)
            for v in range(n_var):
                first_col = (n_past - v) * L
                bias_sc[v * N_HEADS + h] = jnp.where(kk >= first_col, tile, -jnp.inf) if first_col > 0 else tile

    for s in range(seqs):
        @at_first
        def _():
            for p in range(N_PAIRS):
                sblk, cblk = [], []
                for hh in range(2):
                    h = 2 * p + hh
                    z = jnp.zeros((HEAD_DIM, HEAD_DIM), F32)
                    s_h = ret0_ref[s, h]
                    c_h = c0_ref[s, h]
                    n_row = n0_ref[s, h:h + 1, :]
                    n_col = jnp.sum(jnp.where(eye64, jnp.broadcast_to(n_row, eye64.shape), 0.0),
                                    axis=1, keepdims=True)
                    n_rep = jnp.broadcast_to(n_col, (HEAD_DIM, HEAD_DIM))
                    sblk.append(jnp.concatenate([s_h, z] if hh == 0 else [z, s_h], axis=1))
                    cblk.append(jnp.concatenate([c_h, z, n_rep, z] if hh == 0 else [z, c_h, z, n_rep], axis=1))
                spair[s, p] = jnp.concatenate(sblk, axis=0)
                ppair[s, p] = jnp.concatenate(cblk, axis=0)
            m_sc[s, 0:N_HEADS, :] = jnp.broadcast_to(m0_ref[s], (N_HEADS, m_sc.shape[2]))
            win[s, CONV_PAD - tail:CONV_PAD, :] = buf0_ref[s]
            h_sc[s] = h0_ref[s]
            if not has_past:
                vpad[s, 0:A_REACH, :] = jnp.zeros((A_REACH, W_GROUP), BF16)
                kpadt[s, 0:n_past * W_GROUP, :] = jnp.zeros((n_past * W_GROUP, L), BF16)

    g8 = gt_ref[...]
    seg_pos = lax.broadcasted_iota(jnp.int32, g8.shape, 1) % L
    b8 = _seg_scan(jax.nn.log_sigmoid(g8), seg_pos, L, jnp.add, 0.0)
    a8 = g8 - pltpu.roll(b8, N_HEADS, 0)
    cm8 = _seg_scan(a8, seg_pos, L, jnp.maximum, -jnp.inf)
    b8 = pltpu.roll(b8, N_HEADS, 0)

    heads = [(h // 2, h % 2, h) for h in range(N_HEADS)]
    slab = lambda p: slice(p * SLAB, (p + 1) * SLAB)
    seg_of = lambda s: slice(s * L, (s + 1) * L)
    col = lambda s, j: pr_ref[s, :, j * W_GROUP:(j + 1) * W_GROUP]

    def halves(per_head):
        return jnp.concatenate([jnp.where(lane_lo, per_head[2 * p], per_head[2 * p + 1]) for p in range(N_PAIRS)],
                               axis=1)


    def attention_chain(s):
        pa = pa_ref[s]
        kat = kat_ref[:, seg_of(s)]
        v_new = pa[:, 2 * W_GROUP:].astype(BF16)
        if has_past:
            kpast = pastk_ref[s].astype(BF16)
            vband = jnp.concatenate([pastv_ref[s].astype(BF16), v_new], axis=0)
            bias_at = lambda h: bias_sc[h]
        else:
            kpadt[s, pl.ds(pl.multiple_of((tb + n_past) * W_GROUP, W_GROUP), W_GROUP), :] = kat
            kwin = kpadt[s, pl.ds(pl.multiple_of(tb * W_GROUP, W_GROUP), (n_past + 1) * W_GROUP), :]
            start = pl.multiple_of(tb * L, L)
            vpad[s, pl.ds(A_REACH + start, L), :] = v_new
            vband = vpad[s, pl.ds(start, band), :]
            var = jnp.minimum(tb, n_var - 1) * N_HEADS
            bias_at = lambda h: bias_sc[var + h]
        qa = (pa[:, 0:W_GROUP] * (HEAD_DIM ** -0.5 * LOG2E)).astype(BF16)
        sc = {}
        for p, hh, h in heads:
            q_slab = qa[:, slab(p)]
            if has_past:
                k_slab = kpast[:, slab(p)]
                lane = lax.broadcasted_iota(jnp.int32, k_slab.shape, 1)
                lane_h = lane < HEAD_DIM if hh == 0 else lane >= HEAD_DIM
                parts = [_dot_nt(q_slab, jnp.where(lane_h, k_slab, jnp.zeros_like(k_slab))),
                         jnp.dot(q_slab, _head_rows(kat, h, hh), preferred_element_type=F32)]
            else:
                parts = [jnp.dot(q_slab, _head_rows(kwin[i * W_GROUP:(i + 1) * W_GROUP], h, hh),
                                 preferred_element_type=F32) for i in range(n_past + 1)]
            sc[h] = jnp.concatenate(parts, axis=1)
        yield
        pe, den = {}, {}
        for _, _, h in heads:
            sh = sc[h] + bias_at(h)
            pe[h] = jnp.exp2(sh - jnp.max(sh, axis=1, keepdims=True))
            den[h] = jnp.sum(pe[h], axis=1, keepdims=True)
        yield
        res = {h: _dot(pe[h], vband)[:, slab(p)] for p, _, h in heads}
        yield
        y_ref[s, :, 0:W_GROUP] = halves({h: res[h] / den[h] for _, _, h in heads})

    def retention_chain(s):
        qr = (col(s, PR_QB) * cos_ref[...] + col(s, PR_QBS) * sin_ref[...]).astype(BF16)
        kbt = kbt_ref[:, seg_of(s)]
        half = HEAD_DIM // 2
        kbt_sw = jnp.concatenate([kbt[h * HEAD_DIM + o:h * HEAD_DIM + o + half]
                                  for h in range(N_HEADS) for o in (half, 0)], axis=0)
        krt = (kbt * cost_ref[:, seg_of(s)] + kbt_sw * sint_ref[:, seg_of(s)]) * HEAD_DIM ** -0.5
        krt_b = krt.astype(BF16)
        kdt_b = (krt * kdect_ref[...]).astype(BF16)
        vb = col(s, PR_VB).astype(BF16)
        raw = {h: jnp.dot(qr[:, slab(p)], _head_rows(krt_b, h, hh), preferred_element_type=F32)
               for p, hh, h in heads}
        s0 = {p: spair[s, p] for p in range(N_PAIRS)}
        inter = {p: _dot(qr[:, slab(p)], s0[p]) for p in range(N_PAIRS)}
        add = {p: _dot(kdt_b[slab(p)], vb[:, slab(p)]) for p in range(N_PAIRS)}
        yield
        scores = {h: raw[h] * dec_ref[h] for _, _, h in heads}
        for p in range(N_PAIRS):
            spair[s, p] = s0[p] * sdec_ref[p] + add[p] * smask
        yield
        o = {h: _dot(scores[h], vb[:, slab(p)]) for p, _, h in heads}
        yield
        ob = halves(o) + jnp.concatenate([inter[p] for p in range(N_PAIRS)], axis=1) * qdec_ref[...]
        mu = _dot(ob, gmat)
        yield
        oc = ob - mu
        var = _dot(oc * oc, gmat)
        yield
        y_ref[s, :, W_GROUP:2 * W_GROUP] = oc * lax.rsqrt(var + LN_EPS) * _silu(col(s, PR_GB))

    def mlstm_chain(s):
        a4 = a8[0:N_HEADS, seg_of(s)]
        b4 = b8[0:N_HEADS, seg_of(s)]
        m0c = m_sc[s, 0:N_HEADS, 0:1]
        big_m = jnp.maximum(m0c, cm8[0:N_HEADS, seg_of(s)])
        m4 = b4 + big_m
        w0_4 = jnp.exp(m0c - big_m)
        m_last = big_m[:, L - 1:L]
        wrow4 = jnp.exp(a4 - m_last)
        kct = kct_ref[:, seg_of(s)] * HEAD_DIM ** -0.5
        kct_b = kct.astype(BF16)
        kw_b = (kct * _rep_heads(wrow4, HEAD_DIM)).astype(BF16)
        qc = col(s, PR_QC).astype(BF16)
        vc = col(s, PR_VC).astype(BF16)
        v_aug = {p: jnp.concatenate([vc[:, slab(p)], ones_slab], axis=1) for p in range(N_PAIRS)}
        p0 = {p: ppair[s, p] for p in range(N_PAIRS)}
        w0_bc = _col_bcast(eye_b, _rep_heads(w0_4, HEAD_DIM))
        floor_bc = _col_bcast(eye_b, _rep_heads(jnp.exp(-m4), HEAD_DIM))
        m_bc = {h: _col_bcast(eye_b, jnp.broadcast_to(big_m[h:h + 1], (L, L))) for _, _, h in heads}
        raw = {h: jnp.dot(qc[:, slab(p)], _head_rows(kct_b, h, hh), preferred_element_type=F32)
               for p, hh, h in heads}
        inter = {p: _dot(qc[:, slab(p)], p0[p]) for p in range(N_PAIRS)}
        add = {p: _dot(kw_b[slab(p)], v_aug[p]) for p in range(N_PAIRS)}
        yield
        qk = {h: raw[h] * jnp.exp(a4[h:h + 1] - m_bc[h] + causal_neg) for _, _, h in heads}
        for p in range(N_PAIRS):
            w_state = jnp.where(row_lo, w0_4[2 * p:2 * p + 1, L - 1:L], w0_4[2 * p + 1:2 * p + 2, L - 1:L])
            ppair[s, p] = p0[p] * w_state + add[p] * pmask
        m_sc[s, 0:N_HEADS, :] = jnp.broadcast_to(m4[:, L - 1:L], (N_HEADS, m_sc.shape[2]))
        yield
        res = {h: _dot(qk[h], v_aug[p]) for p, _, h in heads}
        yield
        hc = []
        for p in range(N_PAIRS):
            w0 = w0_bc[:, slab(p)]
            num = jnp.where(lane_lo, res[2 * p][:, 0:SLAB], res[2 * p + 1][:, 0:SLAB]) + inter[p][:, 0:SLAB] * w0
            den = jnp.where(lane_lo, res[2 * p][:, SLAB:], res[2 * p + 1][:, SLAB:]) + inter[p][:, SLAB:] * w0
            hc.append(num / jnp.maximum(jnp.abs(den), floor_bc[:, slab(p)]))
        hc = jnp.concatenate(hc, axis=1)
        mu = _dot(hc, gmat)
        yield
        hcc = hc - mu
        var = _dot(hcc * hcc, gmat)
        yield
        y_ref[s, :, 2 * W_GROUP:3 * W_GROUP] = hcc * lax.rsqrt(var + LN_EPS) * jax.nn.sigmoid(col(s, PR_OC))

    def rglru_chain(s):
        xd = col(s, PR_XD)
        win[s, CONV_PAD:CONV_PAD + L, :] = xd
        xc = cb_ref[...]
        for j in range(CONV_W):
            off = CONV_PAD - tail + j
            xc = xc + win[s, off:off + L, :] * cw_ref[j:j + 1, :]
        win[s, CONV_PAD - tail:CONV_PAD, :] = xd[L - tail:, :]
        r_pre = _dot(xc, wa_ref[...])
        i_pre = _dot(xc, wi_ref[...])
        yield
        r = jax.nn.sigmoid(r_pre + ba_ref[...])
        i = jax.nn.sigmoid(i_pre + bi_ref[...])
        neg_lam = -lam_ref[...]
        softplus = jnp.maximum(neg_lam, 0.0) + jnp.log1p(jnp.exp(-jnp.abs(neg_lam)))
        log_a = -LRU_C * r * softplus
        a = jnp.exp(log_a)
        u = jnp.sqrt(-jnp.tanh(log_a) * (jnp.exp(2.0 * log_a) + 1.0)) * (i * xc)
        st = 1
        while st < L:
            u = a * _shift_rows(u, st, 0.0) + u
            a = a * _shift_rows(a, st, 1.0)
            st *= 2
            if st in (8, 64):
                yield
        hseq = u + a * h_sc[s]
        h_sc[s] = hseq[L - 1:L, :]
        y_ref[s, :, 3 * W_GROUP:] = hseq * jax.nn.gelu(col(s, PR_YD))

    if seqs == 1:
        order, lag = (attention_chain, rglru_chain, mlstm_chain, retention_chain), 2
    else:
        order, lag = (attention_chain, retention_chain, mlstm_chain, rglru_chain), 0
    _run_interleaved([chain(s) for chain in order for s in range(seqs)], lag=lag)

    for s in range(seqs):
        @at_last
        def _():
            for p in range(N_PAIRS):
                sp = spair[s, p]
                pp = ppair[s, p]
                for hh in range(2):
                    h = 2 * p + hh
                    rs = slice(hh * HEAD_DIM, (hh + 1) * HEAD_DIM)
                    ret_out[s, h] = sp[rs, rs]
                    c_out[s, h] = pp[rs, rs]
                    n_rep = pp[rs, SLAB + hh * HEAD_DIM:SLAB + (hh + 1) * HEAD_DIM]
                    n_out[s, h:h + 1, :] = jnp.sum(jnp.where(eye64, n_rep, 0.0), axis=0, keepdims=True)
            m_out[s] = m_sc[s, 0:N_HEADS, 0:1]
            h_out[s] = h_sc[s]
            buf_out[s] = pr_ref[s, L - tail:, PR_XD * W_GROUP:(PR_XD + 1) * W_GROUP]
            if first_of_stack is not None:
                for other in first_of_stack[1]:
                    for ref in state_outs:
                        ref[other, s] = jnp.zeros(ref.shape[2:], F32)


def _mixer_kernel_fresh(pa_ref, pr_ref, kat_ref, kbt_ref, kct_ref, gt_ref, *rest, **static):
    _mixer_kernel(pa_ref, pr_ref, kat_ref, kbt_ref, kct_ref, gt_ref, None, None, *rest, **static)


def _mixer_geometry(t, has_past):
    if has_past:
        assert t == CHUNK, "sequences with a carried cache are expected to be a single chunk"
        return t, MIX_SEQS
    assert t % MIX_BLOCK == 0 and A_REACH % MIX_BLOCK == 0
    return MIX_BLOCK, 1


def _mixer_tables(pos, blk, seqs):
    cos, sin = _rotary_tables(pos)
    return (cos, sin, jnp.tile(cos.T, (1, seqs)), jnp.tile(sin.T, (1, seqs))) + _retention_tables(blk)


def _mixer(pa, pr, kat, kbt, kct, gt, past, rel_rows, tables, rec, lw, n_layers, prev_states=None):
    ret0, c0, n0, m0, h0, buf0, rec_layer = rec
    m0, h0 = m0[..., None], h0[:, :, None, :]
    bsz, t, _ = pa.shape
    has_past = past is not None
    blk, seqs = _mixer_geometry(t, has_past)
    nb = t // blk
    assert bsz % seqs == 0
    band = A_REACH + blk
    n_var = 1 if has_past else A_REACH // blk + 1
    tail = CONV_W - 1

    tok3 = lambda w: pl.BlockSpec((seqs, blk, w), lambda b, c: (b, c, 0))
    trn = lambda r: pl.BlockSpec((r, seqs * blk), lambda b, c: (0, b * nb + c))
    per_seq = lambda *shape: pl.BlockSpec((seqs,) + shape, lambda b, c: (b,) + (0,) * len(shape))
    const = lambda *shape: pl.BlockSpec(shape, lambda b, c: (0,) * len(shape))
    carried = lambda *shape: pl.BlockSpec((None, seqs) + shape, lambda b, c: (rec_layer, b) + (0,) * len(shape))
    past_specs, past_args = [], ()
    if has_past:
        layer = past[2]
        cache = pl.BlockSpec((None, seqs, A_REACH, W_GROUP), lambda b, c: (layer, b, 0, 0))
        past_specs, past_args = [cache, cache], past[:2]
    in_specs = [tok3(3 * W_GROUP), tok3(N_PR * W_GROUP), trn(W_GROUP), trn(W_GROUP), trn(W_GROUP), trn(N_GATES),
                *past_specs, const(N_HEADS, 1, ROLL_W),
                pl.BlockSpec((blk, W_GROUP), lambda b, c: (c, 0)), pl.BlockSpec((blk, W_GROUP), lambda b, c: (c, 0)),
                pl.BlockSpec((W_GROUP, seqs * blk), lambda b, c: (0, c)),
                pl.BlockSpec((W_GROUP, seqs * blk), lambda b, c: (0, c)),
                const(N_HEADS, blk, blk), const(blk, W_GROUP), const(W_GROUP, blk), const(N_PAIRS, SLAB, SLAB),
                carried(N_HEADS, HEAD_DIM, HEAD_DIM), carried(N_HEADS, HEAD_DIM, HEAD_DIM),
                carried(N_HEADS, HEAD_DIM), carried(N_HEADS, 1),
                const(CONV_W, W_GROUP), const(1, W_GROUP), const(W_GROUP, W_GROUP), const(1, W_GROUP),
                const(W_GROUP, W_GROUP), const(1, W_GROUP), const(1, W_GROUP),
                carried(tail, W_GROUP), carried(1, W_GROUP)]
    out_layer = lw['layer']
    state_shapes = ((N_HEADS, HEAD_DIM, HEAD_DIM), (N_HEADS, HEAD_DIM, HEAD_DIM), (N_HEADS, HEAD_DIM), (N_HEADS, 1),
                    (1, W_GROUP), (tail, W_GROUP))
    if prev_states is None:
        first_of_stack = (out_layer, tuple(l for l in range(n_layers) if l != out_layer))
        produced = lambda *shape: pl.BlockSpec((n_layers, seqs) + shape, lambda b, c: (0, b) + (0,) * len(shape))
    else:
        first_of_stack = None
        produced = lambda *shape: pl.BlockSpec((None, seqs) + shape, lambda b, c: (out_layer, b) + (0,) * len(shape))
    out_shape = (jax.ShapeDtypeStruct((bsz, t, 4 * W_GROUP), F32),
                 *[jax.ShapeDtypeStruct((n_layers, bsz) + sh, F32) for sh in state_shapes])
    out_specs = (tok3(4 * W_GROUP), *[produced(*sh) for sh in state_shapes])
    n_in = len(in_specs)
    aliases = {}
    if prev_states is not None:
        in_specs += [pl.BlockSpec(memory_space=pl.ANY)] * len(state_shapes)
        aliases = {n_in + k: 1 + k for k in range(len(state_shapes))}
    scratch = [pltpu.VMEM((n_var * N_HEADS, blk, band), F32),
               pltpu.VMEM((seqs, N_PAIRS, SLAB, SLAB), F32), pltpu.VMEM((seqs, N_PAIRS, SLAB, 2 * SLAB), F32),
               pltpu.VMEM((seqs, 8, 128), F32), pltpu.VMEM((seqs, CONV_PAD + blk, W_GROUP), F32),
               pltpu.VMEM((seqs, 1, W_GROUP), F32)]
    if not has_past:
        scratch += [pltpu.VMEM((seqs, (nb + A_REACH // blk) * W_GROUP, blk), BF16),
                    pltpu.VMEM((seqs, A_REACH + t, W_GROUP), BF16)]
    body = functools.partial(_mixer_kernel if has_past else _mixer_kernel_fresh,
                             blk=blk, seqs=seqs, n_steps=nb, has_past=has_past, first_of_stack=first_of_stack)
    if prev_states is not None:
        inner, n_alias = body, len(state_shapes)
        body = lambda *refs: inner(*refs[:n_in], *refs[n_in + n_alias:])
    return pl.pallas_call(
        body,
        out_shape=out_shape,
        grid=(bsz // seqs, nb),
        in_specs=in_specs,
        out_specs=out_specs,
        scratch_shapes=scratch,
        input_output_aliases=aliases,
        compiler_params=_cparams("arbitrary", "arbitrary"),
        name="mixer",
    )(pa, pr, kat, kbt, kct, gt, *past_args, rel_rows, *tables,
      ret0, c0, n0, m0, lw['conv_w'], lw['conv_b'], lw['lru_wa'], lw['lru_ba'], lw['lru_wi'], lw['lru_bi'],
      lw['lru_lambda'], buf0, h0, *(prev_states or ()))


def _post_kernel(x_ref, y_ref, wo_ref, g1_ref, b1_ref, wq_ref, mk_ref, mv_ref, xo_ref,
                 g2_ref, b2_ref, o_ref, att_sc, *, alpha, seqs, subtiles):
    rows_sub = x_ref.shape[0] // subtiles
    rows = rows_sub // seqs
    head = lambda h: slice(h * HEAD_DIM, (h + 1) * HEAD_DIM)

    def subtile_chain(i):
        rs = slice(i * rows_sub, (i + 1) * rows_sub)
        x = x_ref[rs, :]
        mix = _dot(y_ref[rs, :], wo_ref[...])
        yield
        x2 = _layer_norm(alpha * x + mix, g1_ref[...], b1_ref[...])
        yield
        q = _dot(x2, wq_ref[...]).astype(BF16)
        yield

        def attend(s, h):
            qs = slice(s * rows, (s + 1) * rows)
            sc = _dot_nt(q[qs, head(h)], mk_ref[s, :, head(h)]) * HEAD_DIM ** -0.5
            yield
            p = jnp.exp(sc - jnp.max(sc, axis=1, keepdims=True))
            den = jnp.sum(p, axis=1, keepdims=True)
            yield
            o = _dot(p, mv_ref[s, :, head(h)])
            yield
            att_sc[i * rows_sub + s * rows:i * rows_sub + (s + 1) * rows, head(h)] = o / den

        yield from _lockstep([attend(s, h) for s in range(seqs) for h in range(N_HEADS)])
        out = _dot(att_sc[rs, :], xo_ref[...])
        yield
        o_ref[rs, :] = _layer_norm(alpha * x2 + out, g2_ref[...], b2_ref[...])

    _run_interleaved([subtile_chain(i) for i in range(subtiles)], lag=2)


def _post(x, y, wo, wq, xo, wlayer, g1, b1, g2, b2, mk, mv, layer, alpha, seq_len, casts=()):
    n, d = x.shape
    n_mem, dx = mk.shape[2], mk.shape[3]
    if seq_len >= POST_SUBTILES * TOKEN_TILE:
        subtiles, seqs = POST_SUBTILES, 1
        tm = subtiles * TOKEN_TILE
        mem = pl.BlockSpec((None, 1, n_mem, dx), lambda i: (layer, i // (seq_len // tm), 0, 0))
    else:
        tm = TOKEN_TILE
        subtiles, seqs = 1, tm // seq_len
        mem = pl.BlockSpec((None, seqs, n_mem, dx), lambda i: (layer, i, 0, 0))
    assert n % tm == 0 and tm % (subtiles * seqs) == 0
    tok = lambda w: pl.BlockSpec((tm, w), lambda i: (i, 0))
    plans = [_cast_plan(stack, lyr, n // tm) for stack, lyr in casts]
    out = pl.pallas_call(
        _with_casts(functools.partial(_post_kernel, alpha=alpha, seqs=seqs, subtiles=subtiles), 11, 1, len(casts)),
        out_shape=(jax.ShapeDtypeStruct((n, d), F32), *[p[2] for p in plans]),
        grid=(n // tm,),
        in_specs=[tok(d), tok(y.shape[1]),
                  _resident(wo.shape[1:], wlayer), _resident((1, d)), _resident((1, d)),
                  _resident(wq.shape[1:], wlayer), mem, mem, _resident(xo.shape[1:], wlayer),
                  _resident((1, d)), _resident((1, d)), *[p[0] for p in plans]],
        out_specs=(tok(d), *[p[1] for p in plans]),
        scratch_shapes=[pltpu.VMEM((tm, dx), F32)],
        compiler_params=_cparams("parallel"),
        name="post",
    )(x, y, wo, g1, b1, wq, mk, mv, xo, g2, b2, *[stack for stack, _ in casts])
    return out[0], out[1:]


def _mem_kv_kernel(x_ref, w_ref, o_ref):
    o_ref[...] = _dot(x_ref[...], w_ref[...])


def _mem_kv(mem, w):
    n, d = mem.shape
    tm = min(TOKEN_TILE, n)
    return pl.pallas_call(
        _mem_kv_kernel,
        out_shape=jax.ShapeDtypeStruct((n, w.shape[1]), F32),
        grid=(n // tm,),
        in_specs=[pl.BlockSpec((tm, d), lambda i: (i, 0)), _resident(w.shape)],
        out_specs=pl.BlockSpec((tm, w.shape[1]), lambda i: (i, 0)),
        compiler_params=_cparams("parallel"),
        name="mem_kv",
    )(mem, w)


def _rotary_tables(pos):
    half = HEAD_DIM // 2
    inv = jnp.exp(-jnp.log(10000.0) * jnp.arange(half, dtype=F32) / half)
    ang = pos.astype(F32)[:, None] * inv[None, :]
    cos, sin = jnp.cos(ang), jnp.sin(ang)
    cos_full = jnp.tile(jnp.concatenate([cos, cos], -1), (1, N_HEADS))
    sin_full = jnp.tile(jnp.concatenate([-sin, sin], -1), (1, N_HEADS))
    return cos_full, sin_full


def _retention_tables(blk):
    log_g = jnp.log1p(-jnp.exp2(-5.0 - jnp.arange(N_HEADS, dtype=F32)))
    idx = jnp.arange(blk, dtype=F32)
    diff = idx[:, None] - idx[None, :]
    dec = jnp.exp(jnp.where((diff >= 0)[None], diff[None] * log_g[:, None, None], -jnp.inf))
    q_dec = jnp.exp((idx[:, None] + 1.0) * log_g[None, :])
    k_dec = jnp.exp((blk - 1.0 - idx)[:, None] * log_g[None, :])
    s_dec = jnp.exp(blk * log_g)
    lanes = lambda tbl: jnp.repeat(tbl, HEAD_DIM, axis=1)
    pair = lambda p: jnp.kron(jnp.diag(s_dec[2 * p:2 * p + 2]), jnp.ones((HEAD_DIM, HEAD_DIM), F32))
    return dec, lanes(q_dec), lanes(k_dec).T, jnp.stack([pair(p) for p in range(N_PAIRS)])


def _rel_bias_rows(rel_bias, blk):
    idx = np.clip(A_REACH + blk - np.arange(ROLL_W), -REL_CLIP, REL_CLIP) + REL_CLIP
    h, n_rel = rel_bias.shape
    lead = A_REACH + blk - REL_CLIP
    rest = ROLL_W - lead - n_rel
    if lead >= 0 and rest >= 0:
        assert np.array_equal(idx, np.concatenate([np.full(lead, n_rel - 1), np.arange(n_rel)[::-1],
                                                   np.zeros(rest, np.int64)]))
        rows = jnp.concatenate([jnp.broadcast_to(rel_bias[:, n_rel - 1:], (h, lead)), rel_bias[:, ::-1],
                                jnp.broadcast_to(rel_bias[:, :1], (h, rest))], axis=1)
    else:
        rows = rel_bias[:, idx]
    return rows[:, None, :]


def _block_diag(w):
    h, c, _ = w.shape
    eye = jnp.eye(h, dtype=w.dtype)
    return (eye[:, None, :, None] * w[:, :, None, :]).reshape(h * c, h * c)


def _swap_halves(a):
    half = HEAD_DIM // 2
    return jnp.concatenate([a[..., h * HEAD_DIM + s:h * HEAD_DIM + s + half]
                            for h in range(N_HEADS) for s in (half, 0)], axis=-1)


def _mix_and_post(x1, proj, shape, tables, mem, past, rec, lw, alpha, n_layers, prev_states, post_casts=()):
    bsz, t, d = shape
    n = bsz * t
    g, b = lw['ln_g'], lw['ln_b']
    pa, pr, kat, kbt, kct, gt = proj
    pa = pa.reshape(bsz, t, 3 * W_GROUP)
    pr = pr.reshape(bsz, t, N_PR * W_GROUP)
    blk, _ = _mixer_geometry(t, past is not None)
    y, *rec_stacks = _mixer(pa, pr, kat, kbt, kct, gt, past, _rel_bias_rows(lw['rel_bias'], blk), tables, rec, lw,
                            n_layers, prev_states)
    x3, post_cast_out = _post(x1, y.reshape(n, 4 * W_GROUP), lw['w_out'], lw['x_wq'], lw['x_wo'], lw['layer'],
                              g[1:2], b[1:2], g[2:3], b[2:3], *mem, alpha, t, post_casts)
    return x3, rec_stacks, post_cast_out


def _result_states(stacks):
    ret, c, n, m, h, conv = stacks
    return [ret, c, n, m[..., 0], h[:, :, 0, :], conv]


def kernel(x_prompt, x_sample, mem_prompt, cache_a_k, cache_a_v, state_ret, state_mlstm_c, state_mlstm_n,
           state_mlstm_m, state_lru_h, state_conv, cache_mem_k, cache_mem_v, ln_g, ln_b,
           ffn1_gate, ffn1_up, ffn1_down, ffn2_gate, ffn2_up, ffn2_down, w_in, b_in, a_rel_bias,
           conv_w, conv_b, lru_wa, lru_ba, lru_wi, lru_bi, lru_lambda, w_out, x_wq, x_wk, x_wv, x_wo):
    depth = ln_g.shape[0]
    alpha = (2.0 * depth) ** 0.25
    row = lambda v: v[None, :]
    grp = lambda a, j: a[..., j * W_GROUP:(j + 1) * W_GROUP]
    QA, KA, VA, QB, KB, VB, GB, QC, KC, VC, OC, XD, YD = range(13)

    def layer_weights(l):
        w, bias = w_in[l], b_in[l]
        nat = lambda a: jnp.concatenate(
            [grp(a, QB), _swap_halves(grp(a, QB)), grp(a, VB), grp(a, GB), grp(a, QC), grp(a, VC), grp(a, OC),
             grp(a, XD), grp(a, YD)], axis=-1)
        trn = lambda a: jnp.concatenate([grp(a, KA), grp(a, KB), grp(a, KC), a[..., 13 * W_GROUP:]], axis=-1)
        return {'ln_g': ln_g[l], 'ln_b': ln_b[l],
                'layer': l, **stacks,
                'w_in_a': w[:, :3 * W_GROUP].astype(BF16), 'w_in_r': nat(w).astype(BF16),
                'w_in_t': trn(w).T.astype(BF16),
                'b_in_a': row(bias[:3 * W_GROUP]), 'b_in_r': row(nat(bias)), 'b_in_t': trn(bias)[:, None],
                'rel_bias': a_rel_bias[l],
                'conv_w': conv_w[l], 'conv_b': row(conv_b[l]),
                'lru_wa': _block_diag(lru_wa[l]).astype(BF16), 'lru_ba': row(lru_ba[l]),
                'lru_wi': _block_diag(lru_wi[l]).astype(BF16), 'lru_bi': row(lru_bi[l]),
                'lru_lambda': row(lru_lambda[l]),
                'x_wkv': jnp.concatenate([x_wk[l], x_wv[l]], axis=1).astype(BF16)}

    stacks = {name: w.astype(BF16) for name, w in (('w_out', w_out), ('x_wq', x_wq), ('x_wo', x_wo))}
    ffn1_f32, ffn2_f32 = (ffn1_gate, ffn1_up, ffn1_down), (ffn2_gate, ffn2_up, ffn2_down)
    ffn1_w = {0: tuple(w[0].astype(BF16) for w in ffn1_f32)}
    weights = [layer_weights(l) for l in range(depth)]
    dx = x_wk.shape[2]

    bp, tp, d = x_prompt.shape
    bs, ts, _ = x_sample.shape
    n_mem = mem_prompt.shape[1]
    tables_p = _mixer_tables(jnp.arange(tp), *_mixer_geometry(tp, False))
    rec0 = (jnp.zeros((1, bp, N_HEADS, HEAD_DIM, HEAD_DIM), F32), jnp.zeros((1, bp, N_HEADS, HEAD_DIM, HEAD_DIM), F32),
            jnp.zeros((1, bp, N_HEADS, HEAD_DIM), F32), jnp.zeros((1, bp, N_HEADS), F32),
            jnp.zeros((1, bp, W_GROUP), F32), jnp.zeros((1, bp, CONV_W - 1, W_GROUP), F32), 0)
    tables_s = _mixer_tables(PAST_LEN + jnp.arange(ts), *_mixer_geometry(ts, True))
    past_k_all = cache_a_k.reshape(depth, bs, A_REACH, W_GROUP)
    past_v_all = cache_a_v.reshape(depth, bs, A_REACH, W_GROUP)
    mem_k_all = cache_mem_k.reshape(depth, bs, n_mem, dx)
    mem_v_all = cache_mem_v.reshape(depth, bs, n_mem, dx)

    hp, hs = x_prompt.reshape(bp * tp, d), x_sample.reshape(bs * ts, d)
    mem_states, rec_p, rec_s, kv_p, kv_s = [], None, None, None, None
    for l in range(depth):
        lw = weights[l]
        g, b = lw['ln_g'], lw['ln_b']
        mkv = _mem_kv(mem_prompt.reshape(bp * n_mem, d), lw['x_wkv'])
        mk = mkv[:, :dx].reshape(bp, n_mem, dx)
        mv = mkv[:, dx:].reshape(bp, n_mem, dx)
        x1p, x1s = _ffn_ln(hp, hs, *ffn1_w[l], g[0:1], b[0:1], alpha)
        w_in = (lw['w_in_a'], lw['w_in_r'], lw['w_in_t'], lw['b_in_a'], lw['b_in_r'], lw['b_in_t'])
        proj_p, kv_p, ffn2 = _in_proj(x1p, *w_in, tp, l, depth, kv_p, [(w, l) for w in ffn2_f32])
        proj_s, kv_s, _ = _in_proj(x1s, *w_in, ts, l, depth, kv_s)
        post_casts = [(w, l + 1) for w in ffn1_f32] if l + 1 < depth else []
        x3p, rec_p, cast_next = _mix_and_post(x1p, proj_p, (bp, tp, d), tables_p, (mk[None], mv[None], 0), None,
                                              rec0, lw, alpha, depth, rec_p, post_casts)
        rec = (state_ret, state_mlstm_c, state_mlstm_n, state_mlstm_m, state_lru_h, state_conv, l)
        x3s, rec_s, _ = _mix_and_post(x1s, proj_s, (bs, ts, d), tables_s, (mem_k_all, mem_v_all, l),
                                      (past_k_all, past_v_all, l), rec, lw, alpha, depth, rec_s)
        if post_casts:
            ffn1_w[l + 1] = cast_next
        hp, hs = _ffn_ln(x3p, x3s, *ffn2, g[3:4], b[3:4], alpha)
        mem_states.append((mk.reshape(bp, n_mem, -1, HEAD_DIM), mv.reshape(bp, n_mem, -1, HEAD_DIM)))
    y_prompt, y_sample = hp.reshape(bp, tp, d), hs.reshape(bs, ts, d)
    p_mk, p_mv = [jnp.stack(f) for f in zip(*mem_states)]
    prompt_out = [*[_kv_result(a, bp, tp) for a in kv_p], *_result_states(rec_p), p_mk, p_mv]
    sample_out = [_kv_result(a, bs, ts) for a in kv_s] + _result_states(rec_s)

    return (y_prompt, y_sample, *prompt_out, *sample_out)
```

```python
import functools

import numpy as np
import jax
import jax.numpy as jnp
from jax import lax
from jax.experimental import pallas as pl
from jax.experimental.pallas import tpu as pltpu

F32 = jnp.float32
BF16 = jnp.bfloat16

CHUNK = 64
HEAD_DIM = 64
N_HEADS = 4
W_GROUP = N_HEADS * HEAD_DIM
SLAB = 2 * HEAD_DIM
N_PAIRS = N_HEADS // 2
A_BAND_CHUNKS = 8
A_REACH = A_BAND_CHUNKS * CHUNK
REL_CLIP = 128
FFN_CHUNK = 256
CONV_W = 4
LRU_C = 8.0
LN_EPS = 1e-5
LOG2E = 1.4426950408889634
N_GATES = 2 * N_HEADS
PAST_LEN = 4096
ROLL_W = 1024
CONV_PAD = 8

V7X_VMEM_LIMIT_BYTES = 56 * 1024 * 1024
TOKEN_TILE = 512
MIX_BLOCK = 256
MIX_SEQS = 4
POST_SUBTILES = 2

PR_QB, PR_QBS, PR_VB, PR_GB, PR_QC, PR_VC, PR_OC, PR_XD, PR_YD = range(9)
N_PR = 9


def _cparams(*sem):
    return pltpu.CompilerParams(dimension_semantics=sem, vmem_limit_bytes=V7X_VMEM_LIMIT_BYTES)


def _dot(a, b):
    return jnp.dot(a.astype(BF16), b.astype(BF16), preferred_element_type=F32)


def _dot_nt(a, b):
    return lax.dot_general(a.astype(BF16), b.astype(BF16), (((1,), (1,)), ((), ())), preferred_element_type=F32)


def _layer_norm(x, g, b):
    mu = jnp.mean(x, -1, keepdims=True)
    xc = x - mu
    var = jnp.mean(xc * xc, -1, keepdims=True)
    return xc * lax.rsqrt(var + LN_EPS) * g + b


def _silu(x):
    return x * jax.nn.sigmoid(x)


def _lockstep(chains):
    while chains:
        chains = [c for c in chains if next(c, "done") != "done"]
        yield


def _run_interleaved(chains, lag=0):
    pending, live, rnd = list(chains), [], 0
    while pending or live:
        while pending and rnd >= lag * (len(chains) - len(pending)):
            live.append(pending.pop(0))
        live = [c for c in live if next(c, "done") != "done"]
        rnd += 1


def _resident(shape, layer=None):
    nd = len(shape)
    if layer is None:
        return pl.BlockSpec(shape, lambda *_: (0,) * nd, pipeline_mode=pl.Buffered(1))
    return pl.BlockSpec((None,) + tuple(shape), lambda *_: (layer,) + (0,) * nd, pipeline_mode=pl.Buffered(1))


def _cast_plan(stack, layer, n_steps):
    _, r, c = stack.shape
    nb = next(k for k in range(n_steps, 0, -1) if n_steps % k == 0 and r % k == 0 and (r // k) % 16 == 0)
    every = n_steps // nb
    return (pl.BlockSpec((None, r // nb, c), lambda i: (layer, i // every, 0)),
            pl.BlockSpec((r // nb, c), lambda i: (i // every, 0)),
            jax.ShapeDtypeStruct((r, c), BF16))


def _with_casts(body, n_in, n_out, n_cast):
    def wrapped(*refs):
        ins, rest = refs[:n_in + n_cast], refs[n_in + n_cast:]
        outs, scratch = rest[:n_out + n_cast], rest[n_out + n_cast:]
        body(*ins[:n_in], *outs[:n_out], *scratch)
        for src, dst in zip(ins[n_in:], outs[n_out:]):
            dst[...] = src[...].astype(BF16)
    return wrapped


def _by_group(body, n_first, n_shared, n_out):
    def wrapped(xa_ref, xb_ref, *refs):
        shared, outs = refs[:n_shared], refs[n_shared:]
        i = pl.program_id(0)

        @pl.when(i < n_first)
        def _():
            body(xa_ref, *shared, *outs[:n_out])

        @pl.when(i >= n_first)
        def _():
            body(xb_ref, *shared, *outs[n_out:2 * n_out])
    return wrapped


def _group_blocks(tm, n_first):
    tok_a = lambda w: pl.BlockSpec((tm, w), lambda i: (jnp.minimum(i, n_first - 1), 0))
    tok_b = lambda w: pl.BlockSpec((tm, w), lambda i: (jnp.maximum(i - n_first, 0), 0))
    return tok_a, tok_b


def _ffn_ln_kernel(x_ref, wg_ref, wu_ref, wd_ref, g_ref, b_ref, o_ref, *, alpha):
    x = x_ref[...]
    xb = x.astype(BF16)
    dff = wg_ref.shape[1]
    cw = FFN_CHUNK if dff % FFN_CHUNK == 0 else dff
    y = None
    for c in range(0, dff, cw):
        gate = jnp.dot(xb, wg_ref[:, c:c + cw], preferred_element_type=F32)
        up = jnp.dot(xb, wu_ref[:, c:c + cw], preferred_element_type=F32)
        h = (_silu(gate) * up).astype(BF16)
        part = jnp.dot(h, wd_ref[c:c + cw, :], preferred_element_type=F32)
        y = part if y is None else y + part
    o_ref[...] = _layer_norm(alpha * x + 0.5 * y, g_ref[...], b_ref[...])


def _ffn_ln(xa, xb, wg, wu, wd, g, b, alpha):
    d = xa.shape[1]
    dff = wg.shape[1]
    tm = TOKEN_TILE
    na, nb = xa.shape[0] // tm, xb.shape[0] // tm
    tok_a, tok_b = _group_blocks(tm, na)
    return pl.pallas_call(
        _by_group(functools.partial(_ffn_ln_kernel, alpha=alpha), na, 5, 1),
        out_shape=(jax.ShapeDtypeStruct(xa.shape, F32), jax.ShapeDtypeStruct(xb.shape, F32)),
        grid=(na + nb,),
        in_specs=[tok_a(d), tok_b(d),
                  _resident((d, dff)), _resident((d, dff)), _resident((dff, d)),
                  _resident((1, d)), _resident((1, d))],
        out_specs=(tok_a(d), tok_b(d)),
        compiler_params=_cparams("arbitrary"),
        name="ffn_ln",
    )(xa, xb, wg, wu, wd, g, b)


def _in_proj_kernel(x_ref, wa_ref, wr_ref, wt_ref, ba_ref, br_ref, bt_ref,
                    pa_ref, pr_ref, kat_ref, kbt_ref, kct_ref, gt_ref, ak_ref, av_ref, *, first_of_stack, kept_every):
    xb = x_ref[...].astype(BF16)
    pa = jnp.dot(xb, wa_ref[...], preferred_element_type=F32) + ba_ref[...]
    pa_ref[...] = pa

    def keep():
        for ref, col in ((ak_ref, W_GROUP), (av_ref, 2 * W_GROUP)):
            if first_of_stack is not None:
                for other in range(ref.shape[0]):
                    if other != first_of_stack:
                        ref[other] = jnp.zeros(ref.shape[1:], F32)
                ref = ref.at[first_of_stack]
            ref[...] = pa[:, col:col + W_GROUP] if kept_every is None else pa_ref[:, col:col + W_GROUP].T

    if kept_every is None:
        keep()
    cr = pr_ref.shape[1]
    cw = 3 * W_GROUP if cr % (3 * W_GROUP) == 0 else cr
    for c in range(0, cr, cw):
        pr_ref[:, c:c + cw] = jnp.dot(xb, wr_ref[:, c:c + cw], preferred_element_type=F32) + br_ref[:, c:c + cw]
    tr = _dot_nt(wt_ref[...], xb) + bt_ref[...]
    kat_ref[...] = tr[0:W_GROUP].astype(BF16)
    kbt_ref[...] = tr[W_GROUP:2 * W_GROUP]
    kct_ref[...] = tr[2 * W_GROUP:3 * W_GROUP]
    gt_ref[...] = tr[3 * W_GROUP:3 * W_GROUP + N_GATES]
    if kept_every is not None:
        pl.when(pl.program_id(0) % kept_every == kept_every - 1)(keep)


def _in_proj(x, wa, wr, wt, ba, br, bt, seq_len, layer, n_layers, prev_kv=None, casts=()):
    n, d = x.shape
    ca, cr, ct = wa.shape[1], wr.shape[1], wt.shape[0]
    tm = TOKEN_TILE
    per = max(seq_len // tm, 1)
    long_seq = seq_len >= tm
    assert (seq_len % tm == 0 and A_REACH == tm) if long_seq else (tm % seq_len == 0 and seq_len <= A_REACH)
    tok = lambda w: pl.BlockSpec((tm, w), lambda i: (i, 0))
    trn = lambda r: pl.BlockSpec((r, tm), lambda i: (0, i))
    plans = [_cast_plan(stack, layer_, n // tm) for stack, layer_ in casts]
    in_specs = [tok(d), _resident((d, ca)), _resident((d, cr)), _resident((ct, d)),
                _resident((1, ca)), _resident((1, cr)), _resident((ct, 1)), *[p[0] for p in plans]]
    n_in = len(in_specs)
    if prev_kv is None:
        first_of_stack, aliases = layer, {}
        lead, at = n_layers, 0
    else:
        first_of_stack, aliases = None, {n_in: 6, n_in + 1: 7}
        lead, at = None, layer
        in_specs += [pl.BlockSpec(memory_space=pl.ANY)] * 2
    if long_seq:
        kept = pl.BlockSpec((lead, None, W_GROUP, tm), lambda i: (at, i // per, 0, 0))
        kv_shape = jax.ShapeDtypeStruct((n_layers, n // seq_len, W_GROUP, tm), F32)
    else:
        kept = pl.BlockSpec((lead, tm, W_GROUP), lambda i: (at, i, 0))
        kv_shape = jax.ShapeDtypeStruct((n_layers, n, W_GROUP), F32)
    inner = _with_casts(functools.partial(_in_proj_kernel, first_of_stack=first_of_stack,
                                          kept_every=per if long_seq else None), 7, 8, len(casts))
    body = lambda *refs: inner(*refs[:n_in], *refs[n_in + len(aliases):])
    out = pl.pallas_call(
        body,
        out_shape=(jax.ShapeDtypeStruct((n, ca), F32), jax.ShapeDtypeStruct((n, cr), F32),
                   jax.ShapeDtypeStruct((W_GROUP, n), BF16), jax.ShapeDtypeStruct((W_GROUP, n), F32),
                   jax.ShapeDtypeStruct((W_GROUP, n), F32), jax.ShapeDtypeStruct((N_GATES, n), F32),
                   kv_shape, kv_shape, *[p[2] for p in plans]),
        grid=(n // tm,),
        in_specs=in_specs,
        out_specs=(tok(ca), tok(cr), trn(W_GROUP), trn(W_GROUP), trn(W_GROUP), trn(N_GATES), kept, kept,
                   *[p[1] for p in plans]),
        input_output_aliases=aliases,
        compiler_params=_cparams("arbitrary"),
        name="in_proj",
    )(x, wa, wr, wt, ba, br, bt, *[stack for stack, _ in casts], *(prev_kv or ()))
    return out[:6], out[6:8], out[8:]


def _kv_result(stack, bsz, seq_len):
    n_layers = stack.shape[0]
    if seq_len >= TOKEN_TILE:
        return stack.reshape(n_layers, bsz, N_HEADS, HEAD_DIM, -1).transpose(0, 1, 4, 2, 3)
    return stack.reshape(n_layers, bsz, -1, N_HEADS, HEAD_DIM)


def _head_rows(xt, h, pos):
    xh = xt[h * HEAD_DIM:(h + 1) * HEAD_DIM]
    z = jnp.zeros_like(xh)
    return jnp.concatenate([xh, z] if pos == 0 else [z, xh], axis=0)


def _col_bcast(eye_b, rows):
    hi = rows.astype(BF16)
    lo = (rows - hi.astype(F32)).astype(BF16)
    dn = (((1,), (1,)), ((), ()))
    return (lax.dot_general(eye_b, hi, dn, preferred_element_type=F32)
            + lax.dot_general(eye_b, lo, dn, preferred_element_type=F32))


def _rep_heads(rows4, n):
    return jnp.concatenate([jnp.broadcast_to(rows4[h:h + 1], (n, rows4.shape[1])) for h in range(N_HEADS)], axis=0)


def _seg_scan(x, seg_pos, seg_len, op, fill):
    s = 1
    while s < seg_len:
        x = op(x, jnp.where(seg_pos >= s, pltpu.roll(x, s, 1), fill))
        s *= 2
    return x


def _shift_rows(x, s, fill):
    if s % 8 == 0:
        return jnp.concatenate([jnp.full((s, x.shape[1]), fill, x.dtype), x[:x.shape[0] - s]], axis=0)
    row = lax.broadcasted_iota(jnp.int32, x.shape, 0)
    return jnp.where(row >= s, pltpu.roll(x, s, 0), fill)


def _mixer_kernel(pa_ref, pr_ref, kat_ref, kbt_ref, kct_ref, gt_ref, pastk_ref, pastv_ref, rel_ref,
                  cos_ref, sin_ref, cost_ref, sint_ref, dec_ref, qdec_ref, kdect_ref, sdec_ref,
                  ret0_ref, c0_ref, n0_ref, m0_ref,
                  cw_ref, cb_ref, wa_ref, ba_ref, wi_ref, bi_ref, lam_ref, buf0_ref, h0_ref,
                  y_ref, ret_out, c_out, n_out, m_out, h_out, buf_out,
                  bias_sc, spair, ppair, m_sc, win, h_sc, kpadt=None, vpad=None,
                  *, blk, seqs, n_steps, has_past, first_of_stack):
    state_outs = (ret_out, c_out, n_out, m_out, h_out, buf_out)
    if first_of_stack is not None:
        ret_out, c_out, n_out, m_out, h_out, buf_out = [ref.at[first_of_stack[0]] for ref in state_outs]
    L = blk
    band = A_REACH + L
    n_past = A_REACH // L if not has_past else 0
    n_var = n_past + 1
    bb = pl.program_id(0)
    tb = pl.program_id(1)
    tail = CONV_W - 1
    if n_steps == 1:
        at_first = at_last = lambda f: f()
    else:
        at_first = pl.when(tb == 0)
        at_last = pl.when(tb == n_steps - 1)

    li = lax.broadcasted_iota(jnp.int32, (L, L), 0)
    lj = lax.broadcasted_iota(jnp.int32, (L, L), 1)
    causal_neg = jnp.where(lj <= li, 0.0, -jnp.inf)
    eye_b = jnp.where(li == lj, 1.0, 0.0).astype(BF16)
    lane_lo = lax.broadcasted_iota(jnp.int32, (L, SLAB), 1) < HEAD_DIM
    ones_slab = jnp.ones((L, SLAB), BF16)
    ri = lax.broadcasted_iota(jnp.int32, (SLAB, SLAB), 0) // HEAD_DIM
    rj = lax.broadcasted_iota(jnp.int32, (SLAB, SLAB), 1) // HEAD_DIM
    smask = jnp.where(ri == rj, 1.0, 0.0)
    pmask = jnp.concatenate([smask, smask], axis=1)
    gi = lax.broadcasted_iota(jnp.int32, (W_GROUP, W_GROUP), 0) // HEAD_DIM
    gj = lax.broadcasted_iota(jnp.int32, (W_GROUP, W_GROUP), 1) // HEAD_DIM
    gmat = jnp.where(gi == gj, 1.0 / HEAD_DIM, 0.0).astype(BF16)
    e64i = lax.broadcasted_iota(jnp.int32, (HEAD_DIM, HEAD_DIM), 0)
    e64j = lax.broadcasted_iota(jnp.int32, (HEAD_DIM, HEAD_DIM), 1)
    eye64 = e64i == e64j
    row_lo = lax.broadcasted_iota(jnp.int32, (SLAB, 1), 0) < HEAD_DIM

    @pl.when((bb == 0) & (tb == 0))
    def _():
        qq = lax.broadcasted_iota(jnp.int32, (L, band), 0)
        kk = lax.broadcasted_iota(jnp.int32, (L, band), 1)
        off = kk - (qq // CHUNK) * CHUNK
        for h in range(N_HEADS):
            row = jnp.broadcast_to(rel_ref[h], (L, ROLL_W))
            tile = pltpu.roll(row, ROLL_W - L, 1, stride=1, stride_axis=0)[:, :band] * LOG2E
            tile = jnp.where(off >= 0, jnp.where(off < A_REACH + CHUNK, tile, -jnp.inf), -jnp.inf)
            for v in range(n_var):
                first_col = (n_past - v) * L
                bias_sc[v * N_HEADS + h] = jnp.where(kk >= first_col, tile, -jnp.inf) if first_col > 0 else tile

    for s in range(seqs):
        @at_first
        def _():
            for p in range(N_PAIRS):
                sblk, cblk = [], []
                for hh in range(2):
                    h = 2 * p + hh
                    z = jnp.zeros((HEAD_DIM, HEAD_DIM), F32)
                    s_h = ret0_ref[s, h]
                    c_h = c0_ref[s, h]
                    n_row = n0_ref[s, h:h + 1, :]
                    n_col = jnp.sum(jnp.where(eye64, jnp.broadcast_to(n_row, eye64.shape), 0.0),
                                    axis=1, keepdims=True)
                    n_rep = jnp.broadcast_to(n_col, (HEAD_DIM, HEAD_DIM))
                    sblk.append(jnp.concatenate([s_h, z] if hh == 0 else [z, s_h], axis=1))
                    cblk.append(jnp.concatenate([c_h, z, n_rep, z] if hh == 0 else [z, c_h, z, n_rep], axis=1))
                spair[s, p] = jnp.concatenate(sblk, axis=0)
                ppair[s, p] = jnp.concatenate(cblk, axis=0)
            m_sc[s, 0:N_HEADS, :] = jnp.broadcast_to(m0_ref[s], (N_HEADS, m_sc.shape[2]))
            win[s, CONV_PAD - tail:CONV_PAD, :] = buf0_ref[s]
            h_sc[s] = h0_ref[s]
            if not has_past:
                vpad[s, 0:A_REACH, :] = jnp.zeros((A_REACH, W_GROUP), BF16)
                kpadt[s, 0:n_past * W_GROUP, :] = jnp.zeros((n_past * W_GROUP, L), BF16)

    g8 = gt_ref[...]
    seg_pos = lax.broadcasted_iota(jnp.int32, g8.shape, 1) % L
    b8 = _seg_scan(jax.nn.log_sigmoid(g8), seg_pos, L, jnp.add, 0.0)
    a8 = g8 - pltpu.roll(b8, N_HEADS, 0)
    cm8 = _seg_scan(a8, seg_pos, L, jnp.maximum, -jnp.inf)
    b8 = pltpu.roll(b8, N_HEADS, 0)

    heads = [(h // 2, h % 2, h) for h in range(N_HEADS)]
    slab = lambda p: slice(p * SLAB, (p + 1) * SLAB)
    seg_of = lambda s: slice(s * L, (s + 1) * L)
    col = lambda s, j: pr_ref[s, :, j * W_GROUP:(j + 1) * W_GROUP]

    def halves(per_head):
        return jnp.concatenate([jnp.where(lane_lo, per_head[2 * p], per_head[2 * p + 1]) for p in range(N_PAIRS)],
                               axis=1)


    def attention_chain(s):
        pa = pa_ref[s]
        kat = kat_ref[:, seg_of(s)]
        v_new = pa[:, 2 * W_GROUP:].astype(BF16)
        if has_past:
            kpast = pastk_ref[s].astype(BF16)
            vband = jnp.concatenate([pastv_ref[s].astype(BF16), v_new], axis=0)
            bias_at = lambda h: bias_sc[h]
        else:
            kpadt[s, pl.ds(pl.multiple_of((tb + n_past) * W_GROUP, W_GROUP), W_GROUP), :] = kat
            kwin = kpadt[s, pl.ds(pl.multiple_of(tb * W_GROUP, W_GROUP), (n_past + 1) * W_GROUP), :]
            start = pl.multiple_of(tb * L, L)
            vpad[s, pl.ds(A_REACH + start, L), :] = v_new
            vband = vpad[s, pl.ds(start, band), :]
            var = jnp.minimum(tb, n_var - 1) * N_HEADS
            bias_at = lambda h: bias_sc[var + h]
        qa = (pa[:, 0:W_GROUP] * (HEAD_DIM ** -0.5 * LOG2E)).astype(BF16)
        sc = {}
        for p, hh, h in heads:
            q_slab = qa[:, slab(p)]
            if has_past:
                k_slab = kpast[:, slab(p)]
                lane = lax.broadcasted_iota(jnp.int32, k_slab.shape, 1)
                lane_h = lane < HEAD_DIM if hh == 0 else lane >= HEAD_DIM
                parts = [_dot_nt(q_slab, jnp.where(lane_h, k_slab, jnp.zeros_like(k_slab))),
                         jnp.dot(q_slab, _head_rows(kat, h, hh), preferred_element_type=F32)]
            else:
                parts = [jnp.dot(q_slab, _head_rows(kwin[i * W_GROUP:(i + 1) * W_GROUP], h, hh),
                                 preferred_element_type=F32) for i in range(n_past + 1)]
            sc[h] = jnp.concatenate(parts, axis=1)
        yield
        pe, den = {}, {}
        for _, _, h in heads:
            sh = sc[h] + bias_at(h)
            pe[h] = jnp.exp2(sh - jnp.max(sh, axis=1, keepdims=True))
            den[h] = jnp.sum(pe[h], axis=1, keepdims=True)
        yield
        res = {h: _dot(pe[h], vband)[:, slab(p)] for p, _, h in heads}
        yield
        y_ref[s, :, 0:W_GROUP] = halves({h: res[h] / den[h] for _, _, h in heads})

    def retention_chain(s):
        qr = (col(s, PR_QB) * cos_ref[...] + col(s, PR_QBS) * sin_ref[...]).astype(BF16)
        kbt = kbt_ref[:, seg_of(s)]
        half = HEAD_DIM // 2
        kbt_sw = jnp.concatenate([kbt[h * HEAD_DIM + o:h * HEAD_DIM + o + half]
                                  for h in range(N_HEADS) for o in (half, 0)], axis=0)
        krt = (kbt * cost_ref[:, seg_of(s)] + kbt_sw * sint_ref[:, seg_of(s)]) * HEAD_DIM ** -0.5
        krt_b = krt.astype(BF16)
        kdt_b = (krt * kdect_ref[...]).astype(BF16)
        vb = col(s, PR_VB).astype(BF16)
        raw = {h: jnp.dot(qr[:, slab(p)], _head_rows(krt_b, h, hh), preferred_element_type=F32)
               for p, hh, h in heads}
        s0 = {p: spair[s, p] for p in range(N_PAIRS)}
        inter = {p: _dot(qr[:, slab(p)], s0[p]) for p in range(N_PAIRS)}
        add = {p: _dot(kdt_b[slab(p)], vb[:, slab(p)]) for p in range(N_PAIRS)}
        yield
        scores = {h: raw[h] * dec_ref[h] for _, _, h in heads}
        for p in range(N_PAIRS):
            spair[s, p] = s0[p] * sdec_ref[p] + add[p] * smask
        yield
        o = {h: _dot(scores[h], vb[:, slab(p)]) for p, _, h in heads}
        yield
        ob = halves(o) + jnp.concatenate([inter[p] for p in range(N_PAIRS)], axis=1) * qdec_ref[...]
        mu = _dot(ob, gmat)
        yield
        oc = ob - mu
        var = _dot(oc * oc, gmat)
        yield
        y_ref[s, :, W_GROUP:2 * W_GROUP] = oc * lax.rsqrt(var + LN_EPS) * _silu(col(s, PR_GB))

    def mlstm_chain(s):
        a4 = a8[0:N_HEADS, seg_of(s)]
        b4 = b8[0:N_HEADS, seg_of(s)]
        m0c = m_sc[s, 0:N_HEADS, 0:1]
        big_m = jnp.maximum(m0c, cm8[0:N_HEADS, seg_of(s)])
        m4 = b4 + big_m
        w0_4 = jnp.exp(m0c - big_m)
        m_last = big_m[:, L - 1:L]
        wrow4 = jnp.exp(a4 - m_last)
        kct = kct_ref[:, seg_of(s)] * HEAD_DIM ** -0.5
        kct_b = kct.astype(BF16)
        kw_b = (kct * _rep_heads(wrow4, HEAD_DIM)).astype(BF16)
        qc = col(s, PR_QC).astype(BF16)
        vc = col(s, PR_VC).astype(BF16)
        v_aug = {p: jnp.concatenate([vc[:, slab(p)], ones_slab], axis=1) for p in range(N_PAIRS)}
        p0 = {p: ppair[s, p] for p in range(N_PAIRS)}
        w0_bc = _col_bcast(eye_b, _rep_heads(w0_4, HEAD_DIM))
        floor_bc = _col_bcast(eye_b, _rep_heads(jnp.exp(-m4), HEAD_DIM))
        m_bc = {h: _col_bcast(eye_b, jnp.broadcast_to(big_m[h:h + 1], (L, L))) for _, _, h in heads}
        raw = {h: jnp.dot(qc[:, slab(p)], _head_rows(kct_b, h, hh), preferred_element_type=F32)
               for p, hh, h in heads}
        inter = {p: _dot(qc[:, slab(p)], p0[p]) for p in range(N_PAIRS)}
        add = {p: _dot(kw_b[slab(p)], v_aug[p]) for p in range(N_PAIRS)}
        yield
        qk = {h: raw[h] * jnp.exp(a4[h:h + 1] - m_bc[h] + causal_neg) for _, _, h in heads}
        for p in range(N_PAIRS):
            w_state = jnp.where(row_lo, w0_4[2 * p:2 * p + 1, L - 1:L], w0_4[2 * p + 1:2 * p + 2, L - 1:L])
            ppair[s, p] = p0[p] * w_state + add[p] * pmask
        m_sc[s, 0:N_HEADS, :] = jnp.broadcast_to(m4[:, L - 1:L], (N_HEADS, m_sc.shape[2]))
        yield
        res = {h: _dot(qk[h], v_aug[p]) for p, _, h in heads}
        yield
        hc = []
        for p in range(N_PAIRS):
            w0 = w0_bc[:, slab(p)]
            num = jnp.where(lane_lo, res[2 * p][:, 0:SLAB], res[2 * p + 1][:, 0:SLAB]) + inter[p][:, 0:SLAB] * w0
            den = jnp.where(lane_lo, res[2 * p][:, SLAB:], res[2 * p + 1][:, SLAB:]) + inter[p][:, SLAB:] * w0
            hc.append(num / jnp.maximum(jnp.abs(den), floor_bc[:, slab(p)]))
        hc = jnp.concatenate(hc, axis=1)
        mu = _dot(hc, gmat)
        yield
        hcc = hc - mu
        var = _dot(hcc * hcc, gmat)
        yield
        y_ref[s, :, 2 * W_GROUP:3 * W_GROUP] = hcc * lax.rsqrt(var + LN_EPS) * jax.nn.sigmoid(col(s, PR_OC))

    def rglru_chain(s):
        xd = col(s, PR_XD)
        win[s, CONV_PAD:CONV_PAD + L, :] = xd
        xc = cb_ref[...]
        for j in range(CONV_W):
            off = CONV_PAD - tail + j
            xc = xc + win[s, off:off + L, :] * cw_ref[j:j + 1, :]
        win[s, CONV_PAD - tail:CONV_PAD, :] = xd[L - tail:, :]
        r_pre = _dot(xc, wa_ref[...])
        i_pre = _dot(xc, wi_ref[...])
        yield
        r = jax.nn.sigmoid(r_pre + ba_ref[...])
        i = jax.nn.sigmoid(i_pre + bi_ref[...])
        neg_lam = -lam_ref[...]
        softplus = jnp.maximum(neg_lam, 0.0) + jnp.log1p(jnp.exp(-jnp.abs(neg_lam)))
        log_a = -LRU_C * r * softplus
        a = jnp.exp(log_a)
        u = jnp.sqrt(-jnp.tanh(log_a) * (jnp.exp(2.0 * log_a) + 1.0)) * (i * xc)
        st = 1
        while st < L:
            u = a * _shift_rows(u, st, 0.0) + u
            a = a * _shift_rows(a, st, 1.0)
            st *= 2
            if st in (8, 64):
                yield
        hseq = u + a * h_sc[s]
        h_sc[s] = hseq[L - 1:L, :]
        y_ref[s, :, 3 * W_GROUP:] = hseq * jax.nn.gelu(col(s, PR_YD))

    if seqs == 1:
        order, lag = (attention_chain, rglru_chain, mlstm_chain, retention_chain), 2
    else:
        order, lag = (attention_chain, retention_chain, mlstm_chain, rglru_chain), 0
    _run_interleaved([chain(s) for chain in order for s in range(seqs)], lag=lag)

    for s in range(seqs):
        @at_last
        def _():
            for p in range(N_PAIRS):
                sp = spair[s, p]
                pp = ppair[s, p]
                for hh in range(2):
                    h = 2 * p + hh
                    rs = slice(hh * HEAD_DIM, (hh + 1) * HEAD_DIM)
                    ret_out[s, h] = sp[rs, rs]
                    c_out[s, h] = pp[rs, rs]
                    n_rep = pp[rs, SLAB + hh * HEAD_DIM:SLAB + (hh + 1) * HEAD_DIM]
                    n_out[s, h:h + 1, :] = jnp.sum(jnp.where(eye64, n_rep, 0.0), axis=0, keepdims=True)
            m_out[s] = m_sc[s, 0:N_HEADS, 0:1]
            h_out[s] = h_sc[s]
            buf_out[s] = pr_ref[s, L - tail:, PR_XD * W_GROUP:(PR_XD + 1) * W_GROUP]
            if first_of_stack is not None:
                for other in first_of_stack[1]:
                    for ref in state_outs:
                        ref[other, s] = jnp.zeros(ref.shape[2:], F32)


def _mixer_kernel_fresh(pa_ref, pr_ref, kat_ref, kbt_ref, kct_ref, gt_ref, *rest, **static):
    _mixer_kernel(pa_ref, pr_ref, kat_ref, kbt_ref, kct_ref, gt_ref, None, None, *rest, **static)


def _mixer_geometry(t, has_past):
    if has_past:
        assert t == CHUNK, "sequences with a carried cache are expected to be a single chunk"
        return t, MIX_SEQS
    assert t % MIX_BLOCK == 0 and A_REACH % MIX_BLOCK == 0
    return MIX_BLOCK, 1


def _mixer_tables(pos, blk, seqs):
    cos, sin = _rotary_tables(pos)
    return (cos, sin, jnp.tile(cos.T, (1, seqs)), jnp.tile(sin.T, (1, seqs))) + _retention_tables(blk)


def _mixer(pa, pr, kat, kbt, kct, gt, past, rel_rows, tables, rec, lw, n_layers, prev_states=None):
    ret0, c0, n0, m0, h0, buf0, rec_layer = rec
    m0, h0 = m0[..., None], h0[:, :, None, :]
    bsz, t, _ = pa.shape
    has_past = past is not None
    blk, seqs = _mixer_geometry(t, has_past)
    nb = t // blk
    assert bsz % seqs == 0
    band = A_REACH + blk
    n_var = 1 if has_past else A_REACH // blk + 1
    tail = CONV_W - 1

    tok3 = lambda w: pl.BlockSpec((seqs, blk, w), lambda b, c: (b, c, 0))
    trn = lambda r: pl.BlockSpec((r, seqs * blk), lambda b, c: (0, b * nb + c))
    per_seq = lambda *shape: pl.BlockSpec((seqs,) + shape, lambda b, c: (b,) + (0,) * len(shape))
    const = lambda *shape: pl.BlockSpec(shape, lambda b, c: (0,) * len(shape))
    carried = lambda *shape: pl.BlockSpec((None, seqs) + shape, lambda b, c: (rec_layer, b) + (0,) * len(shape))
    past_specs, past_args = [], ()
    if has_past:
        layer = past[2]
        cache = pl.BlockSpec((None, seqs, A_REACH, W_GROUP), lambda b, c: (layer, b, 0, 0))
        past_specs, past_args = [cache, cache], past[:2]
    in_specs = [tok3(3 * W_GROUP), tok3(N_PR * W_GROUP), trn(W_GROUP), trn(W_GROUP), trn(W_GROUP), trn(N_GATES),
                *past_specs, const(N_HEADS, 1, ROLL_W),
                pl.BlockSpec((blk, W_GROUP), lambda b, c: (c, 0)), pl.BlockSpec((blk, W_GROUP), lambda b, c: (c, 0)),
                pl.BlockSpec((W_GROUP, seqs * blk), lambda b, c: (0, c)),
                pl.BlockSpec((W_GROUP, seqs * blk), lambda b, c: (0, c)),
                const(N_HEADS, blk, blk), const(blk, W_GROUP), const(W_GROUP, blk), const(N_PAIRS, SLAB, SLAB),
                carried(N_HEADS, HEAD_DIM, HEAD_DIM), carried(N_HEADS, HEAD_DIM, HEAD_DIM),
                carried(N_HEADS, HEAD_DIM), carried(N_HEADS, 1),
                const(CONV_W, W_GROUP), const(1, W_GROUP), const(W_GROUP, W_GROUP), const(1, W_GROUP),
                const(W_GROUP, W_GROUP), const(1, W_GROUP), const(1, W_GROUP),
                carried(tail, W_GROUP), carried(1, W_GROUP)]
    out_layer = lw['layer']
    state_shapes = ((N_HEADS, HEAD_DIM, HEAD_DIM), (N_HEADS, HEAD_DIM, HEAD_DIM), (N_HEADS, HEAD_DIM), (N_HEADS, 1),
                    (1, W_GROUP), (tail, W_GROUP))
    if prev_states is None:
        first_of_stack = (out_layer, tuple(l for l in range(n_layers) if l != out_layer))
        produced = lambda *shape: pl.BlockSpec((n_layers, seqs) + shape, lambda b, c: (0, b) + (0,) * len(shape))
    else:
        first_of_stack = None
        produced = lambda *shape: pl.BlockSpec((None, seqs) + shape, lambda b, c: (out_layer, b) + (0,) * len(shape))
    out_shape = (jax.ShapeDtypeStruct((bsz, t, 4 * W_GROUP), F32),
                 *[jax.ShapeDtypeStruct((n_layers, bsz) + sh, F32) for sh in state_shapes])
    out_specs = (tok3(4 * W_GROUP), *[produced(*sh) for sh in state_shapes])
    n_in = len(in_specs)
    aliases = {}
    if prev_states is not None:
        in_specs += [pl.BlockSpec(memory_space=pl.ANY)] * len(state_shapes)
        aliases = {n_in + k: 1 + k for k in range(len(state_shapes))}
    scratch = [pltpu.VMEM((n_var * N_HEADS, blk, band), F32),
               pltpu.VMEM((seqs, N_PAIRS, SLAB, SLAB), F32), pltpu.VMEM((seqs, N_PAIRS, SLAB, 2 * SLAB), F32),
               pltpu.VMEM((seqs, 8, 128), F32), pltpu.VMEM((seqs, CONV_PAD + blk, W_GROUP), F32),
               pltpu.VMEM((seqs, 1, W_GROUP), F32)]
    if not has_past:
        scratch += [pltpu.VMEM((seqs, (nb + A_REACH // blk) * W_GROUP, blk), BF16),
                    pltpu.VMEM((seqs, A_REACH + t, W_GROUP), BF16)]
    body = functools.partial(_mixer_kernel if has_past else _mixer_kernel_fresh,
                             blk=blk, seqs=seqs, n_steps=nb, has_past=has_past, first_of_stack=first_of_stack)
    if prev_states is not None:
        inner, n_alias = body, len(state_shapes)
        body = lambda *refs: inner(*refs[:n_in], *refs[n_in + n_alias:])
    return pl.pallas_call(
        body,
        out_shape=out_shape,
        grid=(bsz // seqs, nb),
        in_specs=in_specs,
        out_specs=out_specs,
        scratch_shapes=scratch,
        input_output_aliases=aliases,
        compiler_params=_cparams("arbitrary", "arbitrary"),
        name="mixer",
    )(pa, pr, kat, kbt, kct, gt, *past_args, rel_rows, *tables,
      ret0, c0, n0, m0, lw['conv_w'], lw['conv_b'], lw['lru_wa'], lw['lru_ba'], lw['lru_wi'], lw['lru_bi'],
      lw['lru_lambda'], buf0, h0, *(prev_states or ()))


def _post_kernel(x_ref, y_ref, wo_ref, g1_ref, b1_ref, wq_ref, mk_ref, mv_ref, xo_ref,
                 g2_ref, b2_ref, o_ref, att_sc, *, alpha, seqs, subtiles):
    rows_sub = x_ref.shape[0] // subtiles
    rows = rows_sub // seqs
    head = lambda h: slice(h * HEAD_DIM, (h + 1) * HEAD_DIM)

    def subtile_chain(i):
        rs = slice(i * rows_sub, (i + 1) * rows_sub)
        x = x_ref[rs, :]
        mix = _dot(y_ref[rs, :], wo_ref[...])
        yield
        x2 = _layer_norm(alpha * x + mix, g1_ref[...], b1_ref[...])
        yield
        q = _dot(x2, wq_ref[...]).astype(BF16)
        yield

        def attend(s, h):
            qs = slice(s * rows, (s + 1) * rows)
            sc = _dot_nt(q[qs, head(h)], mk_ref[s, :, head(h)]) * HEAD_DIM ** -0.5
            yield
            p = jnp.exp(sc - jnp.max(sc, axis=1, keepdims=True))
            den = jnp.sum(p, axis=1, keepdims=True)
            yield
            o = _dot(p, mv_ref[s, :, head(h)])
            yield
            att_sc[i * rows_sub + s * rows:i * rows_sub + (s + 1) * rows, head(h)] = o / den

        yield from _lockstep([attend(s, h) for s in range(seqs) for h in range(N_HEADS)])
        out = _dot(att_sc[rs, :], xo_ref[...])
        yield
        o_ref[rs, :] = _layer_norm(alpha * x2 + out, g2_ref[...], b2_ref[...])

    _run_interleaved([subtile_chain(i) for i in range(subtiles)], lag=2)


def _post(x, y, wo, wq, xo, wlayer, g1, b1, g2, b2, mk, mv, layer, alpha, seq_len, casts=()):
    n, d = x.shape
    n_mem, dx = mk.shape[2], mk.shape[3]
    if seq_len >= POST_SUBTILES * TOKEN_TILE:
        subtiles, seqs = POST_SUBTILES, 1
        tm = subtiles * TOKEN_TILE
        mem = pl.BlockSpec((None, 1, n_mem, dx), lambda i: (layer, i // (seq_len // tm), 0, 0))
    else:
        tm = TOKEN_TILE
        subtiles, seqs = 1, tm // seq_len
        mem = pl.BlockSpec((None, seqs, n_mem, dx), lambda i: (layer, i, 0, 0))
    assert n % tm == 0 and tm % (subtiles * seqs) == 0
    tok = lambda w: pl.BlockSpec((tm, w), lambda i: (i, 0))
    plans = [_cast_plan(stack, lyr, n // tm) for stack, lyr in casts]
    out = pl.pallas_call(
        _with_casts(functools.partial(_post_kernel, alpha=alpha, seqs=seqs, subtiles=subtiles), 11, 1, len(casts)),
        out_shape=(jax.ShapeDtypeStruct((n, d), F32), *[p[2] for p in plans]),
        grid=(n // tm,),
        in_specs=[tok(d), tok(y.shape[1]),
                  _resident(wo.shape[1:], wlayer), _resident((1, d)), _resident((1, d)),
                  _resident(wq.shape[1:], wlayer), mem, mem, _resident(xo.shape[1:], wlayer),
                  _resident((1, d)), _resident((1, d)), *[p[0] for p in plans]],
        out_specs=(tok(d), *[p[1] for p in plans]),
        scratch_shapes=[pltpu.VMEM((tm, dx), F32)],
        compiler_params=_cparams("parallel"),
        name="post",
    )(x, y, wo, g1, b1, wq, mk, mv, xo, g2, b2, *[stack for stack, _ in casts])
    return out[0], out[1:]


def _mem_kv_kernel(x_ref, w_ref, o_ref):
    o_ref[...] = _dot(x_ref[...], w_ref[...])


def _mem_kv(mem, w):
    n, d = mem.shape
    tm = min(TOKEN_TILE, n)
    return pl.pallas_call(
        _mem_kv_kernel,
        out_shape=jax.ShapeDtypeStruct((n, w.shape[1]), F32),
        grid=(n // tm,),
        in_specs=[pl.BlockSpec((tm, d), lambda i: (i, 0)), _resident(w.shape)],
        out_specs=pl.BlockSpec((tm, w.shape[1]), lambda i: (i, 0)),
        compiler_params=_cparams("parallel"),
        name="mem_kv",
    )(mem, w)


def _rotary_tables(pos):
    half = HEAD_DIM // 2
    inv = jnp.exp(-jnp.log(10000.0) * jnp.arange(half, dtype=F32) / half)
    ang = pos.astype(F32)[:, None] * inv[None, :]
    cos, sin = jnp.cos(ang), jnp.sin(ang)
    cos_full = jnp.tile(jnp.concatenate([cos, cos], -1), (1, N_HEADS))
    sin_full = jnp.tile(jnp.concatenate([-sin, sin], -1), (1, N_HEADS))
    return cos_full, sin_full


def _retention_tables(blk):
    log_g = jnp.log1p(-jnp.exp2(-5.0 - jnp.arange(N_HEADS, dtype=F32)))
    idx = jnp.arange(blk, dtype=F32)
    diff = idx[:, None] - idx[None, :]
    dec = jnp.exp(jnp.where((diff >= 0)[None], diff[None] * log_g[:, None, None], -jnp.inf))
    q_dec = jnp.exp((idx[:, None] + 1.0) * log_g[None, :])
    k_dec = jnp.exp((blk - 1.0 - idx)[:, None] * log_g[None, :])
    s_dec = jnp.exp(blk * log_g)
    lanes = lambda tbl: jnp.repeat(tbl, HEAD_DIM, axis=1)
    pair = lambda p: jnp.kron(jnp.diag(s_dec[2 * p:2 * p + 2]), jnp.ones((HEAD_DIM, HEAD_DIM), F32))
    return dec, lanes(q_dec), lanes(k_dec).T, jnp.stack([pair(p) for p in range(N_PAIRS)])


def _rel_bias_rows(rel_bias, blk):
    idx = np.clip(A_REACH + blk - np.arange(ROLL_W), -REL_CLIP, REL_CLIP) + REL_CLIP
    h, n_rel = rel_bias.shape
    lead = A_REACH + blk - REL_CLIP
    rest = ROLL_W - lead - n_rel
    if lead >= 0 and rest >= 0:
        assert np.array_equal(idx, np.concatenate([np.full(lead, n_rel - 1), np.arange(n_rel)[::-1],
                                                   np.zeros(rest, np.int64)]))
        rows = jnp.concatenate([jnp.broadcast_to(rel_bias[:, n_rel - 1:], (h, lead)), rel_bias[:, ::-1],
                                jnp.broadcast_to(rel_bias[:, :1], (h, rest))], axis=1)
    else:
        rows = rel_bias[:, idx]
    return rows[:, None, :]


def _block_diag(w):
    h, c, _ = w.shape
    eye = jnp.eye(h, dtype=w.dtype)
    return (eye[:, None, :, None] * w[:, :, None, :]).reshape(h * c, h * c)


def _swap_halves(a):
    half = HEAD_DIM // 2
    return jnp.concatenate([a[..., h * HEAD_DIM + s:h * HEAD_DIM + s + half]
                            for h in range(N_HEADS) for s in (half, 0)], axis=-1)


def _mix_and_post(x1, proj, shape, tables, mem, past, rec, lw, alpha, n_layers, prev_states, post_casts=()):
    bsz, t, d = shape
    n = bsz * t
    g, b = lw['ln_g'], lw['ln_b']
    pa, pr, kat, kbt, kct, gt = proj
    pa = pa.reshape(bsz, t, 3 * W_GROUP)
    pr = pr.reshape(bsz, t, N_PR * W_GROUP)
    blk, _ = _mixer_geometry(t, past is not None)
    y, *rec_stacks = _mixer(pa, pr, kat, kbt, kct, gt, past, _rel_bias_rows(lw['rel_bias'], blk), tables, rec, lw,
                            n_layers, prev_states)
    x3, post_cast_out = _post(x1, y.reshape(n, 4 * W_GROUP), lw['w_out'], lw['x_wq'], lw['x_wo'], lw['layer'],
                              g[1:2], b[1:2], g[2:3], b[2:3], *mem, alpha, t, post_casts)
    return x3, rec_stacks, post_cast_out


def _result_states(stacks):
    ret, c, n, m, h, conv = stacks
    return [ret, c, n, m[..., 0], h[:, :, 0, :], conv]


def kernel(x_prompt, x_sample, mem_prompt, cache_a_k, cache_a_v, state_ret, state_mlstm_c, state_mlstm_n,
           state_mlstm_m, state_lru_h, state_conv, cache_mem_k, cache_mem_v, ln_g, ln_b,
           ffn1_gate, ffn1_up, ffn1_down, ffn2_gate, ffn2_up, ffn2_down, w_in, b_in, a_rel_bias,
           conv_w, conv_b, lru_wa, lru_ba, lru_wi, lru_bi, lru_lambda, w_out, x_wq, x_wk, x_wv, x_wo):
    depth = ln_g.shape[0]
    alpha = (2.0 * depth) ** 0.25
    row = lambda v: v[None, :]
    grp = lambda a, j: a[..., j * W_GROUP:(j + 1) * W_GROUP]
    QA, KA, VA, QB, KB, VB, GB, QC, KC, VC, OC, XD, YD = range(13)

    def layer_weights(l):
        w, bias = w_in[l], b_in[l]
        nat = lambda a: jnp.concatenate(
            [grp(a, QB), _swap_halves(grp(a, QB)), grp(a, VB), grp(a, GB), grp(a, QC), grp(a, VC), grp(a, OC),
             grp(a, XD), grp(a, YD)], axis=-1)
        trn = lambda a: jnp.concatenate([grp(a, KA), grp(a, KB), grp(a, KC), a[..., 13 * W_GROUP:]], axis=-1)
        return {'ln_g': ln_g[l], 'ln_b': ln_b[l],
                'layer': l, **stacks,
                'w_in_a': w[:, :3 * W_GROUP].astype(BF16), 'w_in_r': nat(w).astype(BF16),
                'w_in_t': trn(w).T.astype(BF16),
                'b_in_a': row(bias[:3 * W_GROUP]), 'b_in_r': row(nat(bias)), 'b_in_t': trn(bias)[:, None],
                'rel_bias': a_rel_bias[l],
                'conv_w': conv_w[l], 'conv_b': row(conv_b[l]),
                'lru_wa': _block_diag(lru_wa[l]).astype(BF16), 'lru_ba': row(lru_ba[l]),
                'lru_wi': _block_diag(lru_wi[l]).astype(BF16), 'lru_bi': row(lru_bi[l]),
                'lru_lambda': row(lru_lambda[l]),
                'x_wkv': jnp.concatenate([x_wk[l], x_wv[l]], axis=1).astype(BF16)}

    stacks = {name: w.astype(BF16) for name, w in (('w_out', w_out), ('x_wq', x_wq), ('x_wo', x_wo))}
    ffn1_f32, ffn2_f32 = (ffn1_gate, ffn1_up, ffn1_down), (ffn2_gate, ffn2_up, ffn2_down)
    ffn1_w = {0: tuple(w[0].astype(BF16) for w in ffn1_f32)}
    weights = [layer_weights(l) for l in range(depth)]
    dx = x_wk.shape[2]

    bp, tp, d = x_prompt.shape
    bs, ts, _ = x_sample.shape
    n_mem = mem_prompt.shape[1]
    tables_p = _mixer_tables(jnp.arange(tp), *_mixer_geometry(tp, False))
    rec0 = (jnp.zeros((1, bp, N_HEADS, HEAD_DIM, HEAD_DIM), F32), jnp.zeros((1, bp, N_HEADS, HEAD_DIM, HEAD_DIM), F32),
            jnp.zeros((1, bp, N_HEADS, HEAD_DIM), F32), jnp.zeros((1, bp, N_HEADS), F32),
            jnp.zeros((1, bp, W_GROUP), F32), jnp.zeros((1, bp, CONV_W - 1, W_GROUP), F32), 0)
    tables_s = _mixer_tables(PAST_LEN + jnp.arange(ts), *_mixer_geometry(ts, True))
    past_k_all = cache_a_k.reshape(depth, bs, A_REACH, W_GROUP)
    past_v_all = cache_a_v.reshape(depth, bs, A_REACH, W_GROUP)
    mem_k_all = cache_mem_k.reshape(depth, bs, n_mem, dx)
    mem_v_all = cache_mem_v.reshape(depth, bs, n_mem, dx)

    hp, hs = x_prompt.reshape(bp * tp, d), x_sample.reshape(bs * ts, d)
    mem_states, rec_p, rec_s, kv_p, kv_s = [], None, None, None, None
    for l in range(depth):
        lw = weights[l]
        g, b = lw['ln_g'], lw['ln_b']
        mkv = _mem_kv(mem_prompt.reshape(bp * n_mem, d), lw['x_wkv'])
        mk = mkv[:, :dx].reshape(bp, n_mem, dx)
        mv = mkv[:, dx:].reshape(bp, n_mem, dx)
        x1p, x1s = _ffn_ln(hp, hs, *ffn1_w[l], g[0:1], b[0:1], alpha)
        w_in = (lw['w_in_a'], lw['w_in_r'], lw['w_in_t'], lw['b_in_a'], lw['b_in_r'], lw['b_in_t'])
        proj_p, kv_p, ffn2 = _in_proj(x1p, *w_in, tp, l, depth, kv_p, [(w, l) for w in ffn2_f32])
        proj_s, kv_s, _ = _in_proj(x1s, *w_in, ts, l, depth, kv_s)
        post_casts = [(w, l + 1) for w in ffn1_f32] if l + 1 < depth else []
        x3p, rec_p, cast_next = _mix_and_post(x1p, proj_p, (bp, tp, d), tables_p, (mk[None], mv[None], 0), None,
                                              rec0, lw, alpha, depth, rec_p, post_casts)
        rec = (state_ret, state_mlstm_c, state_mlstm_n, state_mlstm_m, state_lru_h, state_conv, l)
        x3s, rec_s, _ = _mix_and_post(x1s, proj_s, (bs, ts, d), tables_s, (mem_k_all, mem_v_all, l),
                                      (past_k_all, past_v_all, l), rec, lw, alpha, depth, rec_s)
        if post_casts:
            ffn1_w[l + 1] = cast_next
        hp, hs = _ffn_ln(x3p, x3s, *ffn2, g[3:4], b[3:4], alpha)
        mem_states.append((mk.reshape(bp, n_mem, -1, HEAD_DIM), mv.reshape(bp, n_mem, -1, HEAD_DIM)))
    y_prompt, y_sample = hp.reshape(bp, tp, d), hs.reshape(bs, ts, d)
    p_mk, p_mv = [jnp.stack(f) for f in zip(*mem_states)]
    prompt_out = [*[_kv_result(a, bp, tp) for a in kv_p], *_result_states(rec_p), p_mk, p_mv]
    sample_out = [_kv_result(a, bs, ts) for a in kv_s] + _result_states(rec_s)

    return (y_prompt, y_sample, *prompt_out, *sample_out)
```
